```python
import math
import jax, jax.numpy as jnp
from jax import lax
import numpy as np

D_MODEL = 1024
BATCH = 4
SEQ = 4096
DEPTH = 2
DEC_BATCH = 128
DEC_SEQ = 8
PAST_LEN = 16384
PAGE_SIZE = 128

HEAD_DIM = 64
N_HEADS_A = 8
DILATED_PATTERNS = ((128, 1), (512, 4), (2048, 16))
WIN_A = 2048
N_HEADS_B = 8
N_KV_B = 2
GQA_GROUP = N_HEADS_B // N_KV_B
WIN_B = 128
BLOCK = 128
ROPE_THETA = 10000.0
D_A = N_HEADS_A * HEAD_DIM
D_BQ = N_HEADS_B * HEAD_DIM
D_BKV = N_KV_B * HEAD_DIM
D_IN_ATTN = 3 * D_A + D_BQ + 2 * D_BKV
D_ATTN_OUT = D_A + D_BQ
D_C = D_MODEL
SSM_GROUP = 16
N_SSM_GROUPS = D_C // SSM_GROUP
SSM_STATE = 64
D_FF = ((8 * D_MODEL // 3 + 255) // 256) * 256
N_ATTN_LAYERS = (DEPTH + 1) // 2
N_SSM_LAYERS = DEPTH // 2
DN_ALPHA = (2 * DEPTH) ** 0.25
DN_BETA = (8 * DEPTH) ** -0.25
FFN_RES = 0.5
LN_EPS = 1e-5
ATTN_SCALE = HEAD_DIM ** -0.5
DT_MIN = 1e-3
DT_MAX = 1e-1

kernel_name = 'hybrid_dilated_swa_s5_decode_step'


def layer_norm(x, g, b):
    xf = x.astype(jnp.float32)
    mu = jnp.mean(xf, -1, keepdims=True)
    var = jnp.mean(jnp.square(xf - mu), -1, keepdims=True)
    return ((xf - mu) * lax.rsqrt(var + LN_EPS) * g + b).astype(x.dtype)


def swiglu(x, w_gate, w_up, w_down):
    return (jax.nn.silu(x @ w_gate) * (x @ w_up)) @ w_down


def ffn_residual(x, w_gate, w_up, w_down, g, b):
    return layer_norm(DN_ALPHA * x + FFN_RES * swiglu(x, w_gate, w_up, w_down), g, b)


def rope(x, pos):
    half = HEAD_DIM // 2
    inv_freq = ROPE_THETA ** (-jnp.arange(half, dtype=jnp.float32) / half)
    ang = pos.astype(jnp.float32)[:, None] * inv_freq[None, :]
    cos = jnp.cos(ang)[None, :, None, :]
    sin = jnp.sin(ang)[None, :, None, :]
    xf = x.astype(jnp.float32)
    x1, x2 = xf[..., :half], xf[..., half:]
    return jnp.concatenate([x1 * cos - x2 * sin, x2 * cos + x1 * sin], -1).astype(x.dtype)


def attn_project(h, pos, w_in):
    bsz, L, _ = h.shape
    z = h @ w_in
    cuts = [D_A, 2 * D_A, 3 * D_A, 3 * D_A + D_BQ, 3 * D_A + D_BQ + D_BKV]
    qa, ka, va, qb, kb, vb = jnp.split(z, cuts, axis=-1)
    qa = rope(qa.reshape(bsz, L, N_HEADS_A, HEAD_DIM), pos)
    ka = rope(ka.reshape(bsz, L, N_HEADS_A, HEAD_DIM), pos)
    va = va.reshape(bsz, L, N_HEADS_A, HEAD_DIM)
    qb = rope(qb.reshape(bsz, L, N_HEADS_B, HEAD_DIM), pos).reshape(bsz, L, N_KV_B, GQA_GROUP, HEAD_DIM)
    kb = rope(kb.reshape(bsz, L, N_KV_B, HEAD_DIM), pos)
    vb = vb.reshape(bsz, L, N_KV_B, HEAD_DIM)
    return qa, ka, va, qb, kb, vb


def dilated_mixture_attend(q, k_all, v_all, q_idx):
    outs, lses = [], []
    for window, dil in DILATED_PATTERNS:
        steps = jnp.arange(window // dil + 1) * dil
        idx = q_idx[:, None] - steps[None, :]
        valid = idx >= 0
        idx = jnp.maximum(idx, 0)
        kg = jnp.take(k_all, idx, axis=1)
        vg = jnp.take(v_all, idx, axis=1)
        s = jnp.einsum('bqhd,bqjhd->bhqj', q, kg).astype(jnp.float32) * ATTN_SCALE
        s = jnp.where(valid[None, None], s, -jnp.inf)
        m = jnp.max(s, -1, keepdims=True)
        p = jnp.exp(s - m)
        den = jnp.sum(p, -1, keepdims=True)
        o = jnp.einsum('bhqj,bqjhd->bqhd', p / den, vg.astype(jnp.float32))
        lses.append(jnp.transpose((m + jnp.log(den))[..., 0], (0, 2, 1)))
        outs.append(o)
    wts = jax.nn.softmax(jnp.stack(lses, 0), axis=0)[..., None]
    return jnp.sum(wts * jnp.stack(outs, 0), 0).astype(q.dtype)


def sink_attend(q, k, v, mask, sinks):
    s = jnp.einsum('bnqkgd,bnskd->bnkgqs', q, k).astype(jnp.float32) * ATTN_SCALE
    s = jnp.where(mask[None, :, None, None], s, -jnp.inf)
    sink = sinks.astype(jnp.float32).reshape(N_KV_B, GQA_GROUP)[None, None, :, :, None, None]
    m = jnp.maximum(jnp.max(s, -1, keepdims=True), sink)
    p = jnp.exp(s - m)
    den = jnp.sum(p, -1, keepdims=True) + jnp.exp(sink - m)
    o = jnp.einsum('bnkgqs,bnskd->bnqkgd', p / den, v.astype(jnp.float32))
    return o.astype(q.dtype)


def band_blocks(x):
    bsz, L = x.shape[:2]
    xb = x.reshape(bsz, L // BLOCK, BLOCK, *x.shape[2:])
    prev = jnp.pad(xb[:, :-1], ((0, 0), (1, 0), (0, 0), (0, 0), (0, 0)))
    return jnp.concatenate([prev, xb], axis=2)


def band_mask(n_blocks):
    blk = jnp.arange(n_blocks)[:, None, None]
    qpos = blk * BLOCK + jnp.arange(BLOCK)[None, :, None]
    kpos = (blk - 1) * BLOCK + jnp.arange(2 * BLOCK)[None, None, :]
    dist = qpos - kpos
    return (dist >= 0) & (dist < WIN_B) & (kpos >= 0)


def attn_mixer_prompt(h, w_in, sinks, w_out):
    bsz, L, _ = h.shape
    nb = L // BLOCK
    qa, ka, va, qb, kb, vb = attn_project(h, jnp.arange(L), w_in)
    q_blocks = jnp.moveaxis(qa.reshape(bsz, nb, BLOCK, N_HEADS_A, HEAD_DIM), 1, 0)
    starts = jnp.arange(nb) * BLOCK
    oa = lax.map(lambda qs: dilated_mixture_attend(qs[0], ka, va, qs[1] + jnp.arange(BLOCK)), (q_blocks, starts))
    oa = jnp.moveaxis(oa, 0, 1).reshape(bsz, L, D_A)
    ob = sink_attend(qb.reshape(bsz, nb, BLOCK, N_KV_B, GQA_GROUP, HEAD_DIM),
                     band_blocks(kb), band_blocks(vb), band_mask(nb), sinks).reshape(bsz, L, D_BQ)
    out = jnp.concatenate([oa, ob], -1) @ w_out
    return out, (ka[:, -WIN_A:], va[:, -WIN_A:], kb[:, -WIN_B:], vb[:, -WIN_B:])


def attn_mixer_sample(h, past_a_k, past_a_v, past_b_k, past_b_v, w_in, sinks, w_out):
    bsz, L, _ = h.shape
    qa, ka, va, qb, kb, vb = attn_project(h, PAST_LEN + jnp.arange(L), w_in)
    n_a = past_a_k.shape[1]
    ka_all = jnp.concatenate([past_a_k, ka], 1)
    va_all = jnp.concatenate([past_a_v, va], 1)
    oa = dilated_mixture_attend(qa, ka_all, va_all, n_a + jnp.arange(L)).reshape(bsz, L, D_A)
    n_b = past_b_k.shape[1]
    kb_all = jnp.concatenate([past_b_k, kb], 1)
    vb_all = jnp.concatenate([past_b_v, vb], 1)
    dist = (n_b + jnp.arange(L))[:, None] - jnp.arange(n_b + L)[None, :]
    mask = ((dist >= 0) & (dist < WIN_B))[None]
    ob = sink_attend(qb[:, None], kb_all[:, None], vb_all[:, None], mask, sinks)[:, 0].reshape(bsz, L, D_BQ)
    out = jnp.concatenate([oa, ob], -1) @ w_out
    return out, (ka, va, kb, vb)


def complex_affine_combine(e1, e2):
    a1r, a1i, b1r, b1i = e1
    a2r, a2i, b2r, b2i = e2
    return (a1r * a2r - a1i * a2i, a1r * a2i + a1i * a2r,
            a2r * b1r - a2i * b1i + b2r, a2r * b1i + a2i * b1r + b2i)


def ssm_discretize(lam_re, lam_im, log_dt, b_re, b_im):
    dt = jnp.exp(log_dt.astype(jnp.float32))[:, None]
    lr, li = lam_re.astype(jnp.float32), lam_im.astype(jnp.float32)
    mag = jnp.exp(lr * dt)
    ab_re, ab_im = mag * jnp.cos(li * dt), mag * jnp.sin(li * dt)
    nr, ni = ab_re - 1.0, ab_im
    den = lr * lr + li * li
    fr, fi = (nr * lr + ni * li) / den, (ni * lr - nr * li) / den
    bb_re = fr[..., None] * b_re - fi[..., None] * b_im
    bb_im = fr[..., None] * b_im + fi[..., None] * b_re
    return ab_re, ab_im, bb_re, bb_im


def ssm_mixer(h, h0_re, h0_im, w_in, lam_re, lam_im, log_dt, b_re, b_im, c_re, c_im, d_skip, w_glu, b_glu, w_out):
    bsz, L, _ = h.shape
    u = (h @ w_in).astype(jnp.float32).reshape(bsz, L, N_SSM_GROUPS, SSM_GROUP)
    ab_re, ab_im, bb_re, bb_im = ssm_discretize(lam_re, lam_im, log_dt, b_re, b_im)
    bu_re = jnp.einsum('blgn,gpn->lbgp', u, bb_re)
    bu_im = jnp.einsum('blgn,gpn->lbgp', u, bb_im)
    bu_re = bu_re.at[0].add(ab_re * h0_re - ab_im * h0_im)
    bu_im = bu_im.at[0].add(ab_re * h0_im + ab_im * h0_re)
    a_re = jnp.broadcast_to(ab_re, (L, 1, N_SSM_GROUPS, SSM_STATE))
    a_im = jnp.broadcast_to(ab_im, (L, 1, N_SSM_GROUPS, SSM_STATE))
    _, _, hr, hi = lax.associative_scan(complex_affine_combine, (a_re, a_im, bu_re, bu_im), axis=0)
    y = jnp.einsum('lbgp,gnp->blgn', hr, c_re) - jnp.einsum('lbgp,gnp->blgn', hi, c_im)
    y = y + d_skip.reshape(N_SSM_GROUPS, SSM_GROUP) * u
    y = jax.nn.gelu(y.reshape(bsz, L, D_C))
    y = y * jax.nn.sigmoid(y @ w_glu + b_glu)
    return y.astype(h.dtype) @ w_out, hr[-1], hi[-1]


def setup_inputs(seed: int = 0) -> dict:
    key = jax.random.key(seed)
    ks = iter(jax.random.split(key, 40))
    nrm = lambda shape, s=1.0: s * jax.random.normal(next(ks), shape, jnp.float32)
    nbuf_a = min(WIN_A, PAST_LEN)
    nbuf_b = min(WIN_B, PAST_LEN)
    G, P, N = N_SSM_GROUPS, SSM_STATE, SSM_GROUP
    NA, NS = N_ATTN_LAYERS, N_SSM_LAYERS
    return {
        'x_prompt': nrm((BATCH, SEQ, D_MODEL)),
        'x_sample': nrm((DEC_BATCH, DEC_SEQ, D_MODEL)),
        'cache_a_k': nrm((NA, DEC_BATCH, nbuf_a, N_HEADS_A, HEAD_DIM)),
        'cache_a_v': nrm((NA, DEC_BATCH, nbuf_a, N_HEADS_A, HEAD_DIM)),
        'cache_b_k': nrm((NA, DEC_BATCH, nbuf_b, N_KV_B, HEAD_DIM)),
        'cache_b_v': nrm((NA, DEC_BATCH, nbuf_b, N_KV_B, HEAD_DIM)),
        'state_c_re': nrm((NS, DEC_BATCH, G, P), 0.1),
        'state_c_im': nrm((NS, DEC_BATCH, G, P), 0.1),
        'ln_g': 1.0 + nrm((DEPTH, 3, D_MODEL), 0.01),
        'ln_b': nrm((DEPTH, 3, D_MODEL), 0.01),
        'ffn_w_gate': nrm((DEPTH, 2, D_MODEL, D_FF), D_MODEL ** -0.5),
        'ffn_w_up': nrm((DEPTH, 2, D_MODEL, D_FF), D_MODEL ** -0.5),
        'ffn_w_down': nrm((DEPTH, 2, D_FF, D_MODEL), DN_BETA * D_FF ** -0.5),
        'attn_w_in': nrm((NA, D_MODEL, D_IN_ATTN), D_MODEL ** -0.5),
        'attn_sinks': nrm((NA, N_HEADS_B), 0.5),
        'attn_w_out': nrm((NA, D_ATTN_OUT, D_MODEL), DN_BETA * D_ATTN_OUT ** -0.5),
        'ssm_w_in': nrm((NS, D_MODEL, D_C), D_MODEL ** -0.5),
        'ssm_lambda_re': -0.5 + nrm((NS, G, P), 0.01),
        'ssm_lambda_im': math.pi * jnp.arange(P, dtype=jnp.float32) + nrm((NS, G, P), 0.01),
        'ssm_log_dt': jax.random.uniform(next(ks), (NS, G), jnp.float32, math.log(DT_MIN), math.log(DT_MAX)),
        'ssm_b_re': nrm((NS, G, P, N), (2 * N) ** -0.5),
        'ssm_b_im': nrm((NS, G, P, N), (2 * N) ** -0.5),
        'ssm_c_re': nrm((NS, G, N, P), (2 * P) ** -0.5),
        'ssm_c_im': nrm((NS, G, N, P), (2 * P) ** -0.5),
        'ssm_d': nrm((NS, D_C)),
        'ssm_w_glu': nrm((NS, D_C, D_C), D_C ** -0.5),
        'ssm_b_glu': nrm((NS, D_C), 0.01),
        'ssm_w_out': nrm((NS, D_C, D_MODEL), DN_BETA * D_C ** -0.5),
    }


def reference(x_prompt, x_sample, cache_a_k, cache_a_v, cache_b_k, cache_b_v, state_c_re, state_c_im,
              ln_g, ln_b, ffn_w_gate, ffn_w_up, ffn_w_down, attn_w_in, attn_sinks, attn_w_out,
              ssm_w_in, ssm_lambda_re, ssm_lambda_im, ssm_log_dt, ssm_b_re, ssm_b_im, ssm_c_re, ssm_c_im,
              ssm_d, ssm_w_glu, ssm_b_glu, ssm_w_out):
    yp, ys = x_prompt, x_sample
    pa_k, pa_v, pb_k, pb_v, pc_re, pc_im = [], [], [], [], [], []
    sa_k, sa_v, sb_k, sb_v, sc_re, sc_im = [], [], [], [], [], []
    for l in range(DEPTH):
        i = l // 2
        yp = ffn_residual(yp, ffn_w_gate[l, 0], ffn_w_up[l, 0], ffn_w_down[l, 0], ln_g[l, 0], ln_b[l, 0])
        ys = ffn_residual(ys, ffn_w_gate[l, 0], ffn_w_up[l, 0], ffn_w_down[l, 0], ln_g[l, 0], ln_b[l, 0])
        if l % 2 == 0:
            mp, (ka, va, kb, vb) = attn_mixer_prompt(yp, attn_w_in[i], attn_sinks[i], attn_w_out[i])
            pa_k.append(ka); pa_v.append(va); pb_k.append(kb); pb_v.append(vb)
            ms, (ka, va, kb, vb) = attn_mixer_sample(ys, cache_a_k[i], cache_a_v[i], cache_b_k[i], cache_b_v[i],
                                                     attn_w_in[i], attn_sinks[i], attn_w_out[i])
            sa_k.append(ka); sa_v.append(va); sb_k.append(kb); sb_v.append(vb)
        else:
            ssm_w = (ssm_w_in[i], ssm_lambda_re[i], ssm_lambda_im[i], ssm_log_dt[i], ssm_b_re[i], ssm_b_im[i],
                     ssm_c_re[i], ssm_c_im[i], ssm_d[i], ssm_w_glu[i], ssm_b_glu[i], ssm_w_out[i])
            h0 = jnp.zeros((yp.shape[0], N_SSM_GROUPS, SSM_STATE), jnp.float32)
            mp, hr, hi = ssm_mixer(yp, h0, h0, *ssm_w)
            pc_re.append(hr); pc_im.append(hi)
            ms, hr, hi = ssm_mixer(ys, state_c_re[i], state_c_im[i], *ssm_w)
            sc_re.append(hr); sc_im.append(hi)
        yp = layer_norm(DN_ALPHA * yp + mp, ln_g[l, 1], ln_b[l, 1])
        ys = layer_norm(DN_ALPHA * ys + ms, ln_g[l, 1], ln_b[l, 1])
        yp = ffn_residual(yp, ffn_w_gate[l, 1], ffn_w_up[l, 1], ffn_w_down[l, 1], ln_g[l, 2], ln_b[l, 2])
        ys = ffn_residual(ys, ffn_w_gate[l, 1], ffn_w_up[l, 1], ffn_w_down[l, 1], ln_g[l, 2], ln_b[l, 2])
    return (yp, ys,
            jnp.stack(pa_k), jnp.stack(pa_v), jnp.stack(pb_k), jnp.stack(pb_v), jnp.stack(pc_re), jnp.stack(pc_im),
            jnp.stack(sa_k), jnp.stack(sa_v), jnp.stack(sb_k), jnp.stack(sb_v), jnp.stack(sc_re), jnp.stack(sc_im))
```

```python
import functools
import math

import jax
import jax.numpy as jnp
from jax import lax
from jax.experimental import pallas as pl
from jax.experimental.pallas import tpu as pltpu

F32 = jnp.float32
BF16 = jnp.bfloat16

D_MODEL = 1024
BATCH = 4
SEQ = 4096
DEPTH = 2
DEC_BATCH = 128
DEC_SEQ = 8
PAST_LEN = 16384
HEAD_DIM = 64
N_HEADS_A = 8
DILATIONS = (1, 4, 16)
WIN_A = 2048
N_HEADS_B = 8
N_KV_B = 2
WIN_B = 128
ROPE_THETA = 10000.0
D_A = N_HEADS_A * HEAD_DIM
D_BQ = N_HEADS_B * HEAD_DIM
D_BKV = N_KV_B * HEAD_DIM
D_IN_ATTN = 3 * D_A + D_BQ + 2 * D_BKV
SSM_GROUP = 16
N_SSM_GROUPS = D_MODEL // SSM_GROUP
SSM_STATE = 64
N_STATE = N_SSM_GROUPS * SSM_STATE
D_FF = 2816
DN_ALPHA = (2 * DEPTH) ** 0.25
FFN_RES = 0.5
LN_EPS = 1e-5
ATTN_SCALE = HEAD_DIM ** -0.5

LANES = 128
SUBLANES = 8
MXU_N = 256
VMEM_LIMIT = 56 * 1024 * 1024

ROW_TILE = 512
FF_CHUNK = MXU_N
TQ = 128
N_SEG = SUBLANES
SEG_LEN = SEQ // N_SEG
N_LCHUNK = D_MODEL // LANES
ST_CHUNK = N_STATE // N_LCHUNK
SCAN_TILE = 512
NK_PAD = WIN_A + LANES
NKB_PAD = 2 * WIN_B

NEG_INF = float("-inf")


def _params(n_axes, vmem=VMEM_LIMIT):
    return pltpu.CompilerParams(dimension_semantics=("arbitrary",) * n_axes, vmem_limit_bytes=vmem)


def _resident(shape):
    return pl.BlockSpec(shape, lambda *_: (0,) * len(shape), pipeline_mode=pl.Buffered(1))


def _layer_norm(x, g, b):
    mu = jnp.mean(x, -1, keepdims=True)
    xc = x - mu
    var = jnp.mean(xc * xc, -1, keepdims=True)
    return xc * lax.rsqrt(var + LN_EPS) * g + b


def _dot(a, b):
    return jnp.dot(a, b, preferred_element_type=F32)


def _dot_nt(a, b):
    return lax.dot_general(a, b, (((1,), (1,)), ((), ())), preferred_element_type=F32)


def _ffn_kernel(x_ref, wg_ref, wu_ref, wd_ref, g_ref, b_ref, o_ref, h_ref):
    x = x_ref[...]
    xb = x.astype(BF16)
    for c in range(D_FF // FF_CHUNK):
        sl = slice(c * FF_CHUNK, (c + 1) * FF_CHUNK)
        gate = _dot(xb, wg_ref[:, sl])
        up = _dot(xb, wu_ref[:, sl])
        h_ref[:, sl] = (gate * jax.nn.sigmoid(gate) * up).astype(BF16)
    y = DN_ALPHA * x + FFN_RES * _dot(h_ref[...], wd_ref[...])
    o_ref[...] = _layer_norm(y, g_ref[...], b_ref[...])


def _ffn(x, wg, wu, wd, g, b):
    n = x.shape[0]
    tm = min(ROW_TILE, n)
    row = pl.BlockSpec((tm, D_MODEL), lambda i: (i, 0))
    return pl.pallas_call(
        _ffn_kernel,
        grid=(n // tm,),
        in_specs=[row, _resident((D_MODEL, D_FF)), _resident((D_MODEL, D_FF)), _resident((D_FF, D_MODEL)),
                  _resident((1, D_MODEL)), _resident((1, D_MODEL))],
        out_specs=row,
        out_shape=jax.ShapeDtypeStruct((n, D_MODEL), F32),
        scratch_shapes=[pltpu.VMEM((tm, D_FF), BF16)],
        compiler_params=_params(1),
        name="ffn",
    )(x, wg, wu, wd, g, b)


def _rope_tables(pos):
    half = HEAD_DIM // 2
    inv_freq = ROPE_THETA ** (-jnp.arange(half, dtype=F32) / half)
    ang = pos.astype(F32)[:, None] * inv_freq[None, :]
    cos, sin = jnp.cos(ang), jnp.sin(ang)
    cos_t = jnp.concatenate([cos, cos, cos, cos], -1)
    sin_t = jnp.concatenate([-sin, sin, -sin, sin], -1)
    return cos_t, sin_t


def _attn_proj_kernel(x_ref, w_ref, cos_ref, sin_ref, qa_ref, ka_ref, va_ref, qb_ref, kb_ref, vb_ref,
                      kaf_ref, vaf_ref, kbf_ref, vbf_ref):
    xb = x_ref[...].astype(BF16)
    cos = cos_ref[...]
    sin = sin_ref[...]
    lane = lax.broadcasted_iota(jnp.int32, cos.shape, 1)
    first_half = (lane & (HEAD_DIM // 2)) == 0

    def rope(z):
        rot = jnp.where(first_half, pltpu.roll(z, LANES - HEAD_DIM // 2, 1), pltpu.roll(z, HEAD_DIM // 2, 1))
        return z * cos + rot * sin

    def project(col0, ncols):
        return _dot(xb, w_ref[:, col0:col0 + ncols])

    def rope_chunks(z):
        return [rope(z[:, c * LANES:(c + 1) * LANES]) for c in range(z.shape[1] // LANES)]

    col = 0
    for c, r in enumerate(rope_chunks(project(col, D_A))):
        qa_ref[:, c * LANES:(c + 1) * LANES] = (r * ATTN_SCALE).astype(qa_ref.dtype)
    col += D_A
    for c, r in enumerate(rope_chunks(project(col, D_A))):
        ka_ref[:, c * LANES:(c + 1) * LANES] = r.astype(ka_ref.dtype)
        kaf_ref[:, c * LANES:(c + 1) * LANES] = r
    col += D_A
    z = project(col, D_A)
    va_ref[...] = z.astype(va_ref.dtype)
    vaf_ref[...] = z
    col += D_A
    for c, r in enumerate(rope_chunks(project(col, D_BQ))):
        qb_ref[:, c * LANES:(c + 1) * LANES] = (r * ATTN_SCALE).astype(qb_ref.dtype)
    col += D_BQ
    z = project(col, 2 * D_BKV)
    r = rope(z[:, :D_BKV])
    kb_ref[...] = r.astype(kb_ref.dtype)
    kbf_ref[...] = r
    vb_ref[...] = z[:, D_BKV:].astype(vb_ref.dtype)
    vbf_ref[...] = z[:, D_BKV:]


def _attn_proj(x, w, cos_t, sin_t, table_block, act_dtype):
    n = x.shape[0]
    tm = min(ROW_TILE, n)

    def row(width):
        return pl.BlockSpec((tm, width), lambda i: (i, 0))

    tab = pl.BlockSpec((tm, LANES), lambda i: (table_block(i), 0))
    widths = (D_A, D_A, D_A, D_BQ, D_BKV, D_BKV)
    out_shape = [jax.ShapeDtypeStruct((n, wd), act_dtype) for wd in widths]
    out_shape += [jax.ShapeDtypeStruct((n, wd), F32) for wd in (D_A, D_A, D_BKV, D_BKV)]
    return pl.pallas_call(
        _attn_proj_kernel,
        grid=(n // tm,),
        in_specs=[row(D_MODEL), _resident((D_MODEL, D_IN_ATTN)), tab, tab],
        out_specs=[row(wd) for wd in widths] + [row(D_A), row(D_A), row(D_BKV), row(D_BKV)],
        out_shape=out_shape,
        compiler_params=_params(1),
        name="attn_proj",
    )(x, w, cos_t, sin_t)


def _lane_lo(shape):
    return lax.broadcasted_iota(jnp.int32, shape, 1) < HEAD_DIM


def _head_scores(q2, lo_mask_bf16, half, kc, kp, mask_c, mask_p):
    sel = lo_mask_bf16 if half == 0 else (1 - lo_mask_bf16)
    qh = q2 * sel
    s_c = jnp.where(mask_c, _dot_nt(qh, kc), NEG_INF)
    s_p = jnp.where(mask_p, _dot_nt(qh, kp), NEG_INF)
    return s_c, s_p


def _band_a_kernel(q_ref, kc_ref, kp_ref, vc_ref, vp_ref, o_ref, lse_ref):
    t = pl.program_id(2)
    row = lax.broadcasted_iota(jnp.int32, (TQ, TQ), 0)
    col = lax.broadcasted_iota(jnp.int32, (TQ, TQ), 1)
    mask_c = col <= row
    mask_p = col >= row + jnp.where(t > 0, 0, TQ)
    lo = _lane_lo((TQ, LANES))
    lo_bf = jnp.where(_lane_lo((1, LANES)), 1.0, 0.0).astype(BF16)
    for c in range(D_A // LANES):
        sl = slice(c * LANES, (c + 1) * LANES)
        q2, kc, kp, vc, vp = q_ref[:, sl], kc_ref[:, sl], kp_ref[:, sl], vc_ref[:, sl], vp_ref[:, sl]
        outs, lses = [], []
        for half in range(2):
            s_c, s_p = _head_scores(q2, lo_bf, half, kc, kp, mask_c, mask_p)
            m = jnp.max(jnp.maximum(s_c, s_p), -1, keepdims=True)
            p_c = jnp.exp(s_c - m)
            p_p = jnp.exp(s_p - m)
            den = jnp.sum(p_c + p_p, -1, keepdims=True)
            acc = _dot(p_c.astype(BF16), vc) + _dot(p_p.astype(BF16), vp)
            outs.append(acc * (1.0 / den))
            lses.append(jnp.broadcast_to(m + jnp.log(den), (TQ, LANES)))
        o_ref[:, sl] = jnp.where(lo, outs[0], outs[1])
        lse_ref[:, sl] = jnp.where(lo, lses[0], lses[1])


def _band_a(q, k, v, dil):
    bsz, seq, _ = q.shape
    sub = seq // dil
    view = lambda a: a.reshape(bsz, sub, dil * D_A)
    cur = pl.BlockSpec((None, TQ, D_A), lambda b, r, t: (b, t, r))
    prev = pl.BlockSpec((None, TQ, D_A), lambda b, r, t: (b, jnp.maximum(t - 1, 0), r))
    o, lse = pl.pallas_call(
        _band_a_kernel,
        grid=(bsz, dil, sub // TQ),
        in_specs=[cur, cur, prev, cur, prev],
        out_specs=[cur, cur],
        out_shape=[jax.ShapeDtypeStruct((bsz, sub, dil * D_A), F32)] * 2,
        compiler_params=_params(3),
        name=f"band_a_d{dil}",
    )(view(q), view(k), view(k), view(v), view(v))
    return o.reshape(bsz * seq, D_A), lse.reshape(bsz * seq, D_A)


def _swap_halves(x):
    return jnp.concatenate([x[:, HEAD_DIM:], x[:, :HEAD_DIM]], axis=1)


def _band_b_kernel(sink_ref, q_ref, kc_ref, kp_ref, vc_ref, vp_ref, o_ref):
    t = pl.program_id(1)
    row = lax.broadcasted_iota(jnp.int32, (TQ, TQ), 0)
    col = lax.broadcasted_iota(jnp.int32, (TQ, TQ), 1)
    mask_c = col <= row
    mask_p = col > row + jnp.where(t > 0, 0, TQ)
    lo = _lane_lo((TQ, LANES))
    lo_bf = jnp.where(_lane_lo((1, LANES)), 1.0, 0.0).astype(BF16)
    kv = (kc_ref[...], kp_ref[...], vc_ref[...], vp_ref[...])
    kv_swapped = tuple(_swap_halves(a) for a in kv)
    for c in range(D_BQ // LANES):
        q2 = q_ref[:, c * LANES:(c + 1) * LANES]
        outs = []
        for half in range(2):
            head = 2 * c + half
            g = head // (N_HEADS_B // N_KV_B)
            kc, kp, vc, vp = kv if g == half else kv_swapped
            s_c, s_p = _head_scores(q2, lo_bf, half, kc, kp, mask_c, mask_p)
            sink = sink_ref[head]
            m = jnp.maximum(jnp.max(jnp.maximum(s_c, s_p), -1, keepdims=True), sink)
            p_c = jnp.exp(s_c - m)
            p_p = jnp.exp(s_p - m)
            den = jnp.sum(p_c + p_p, -1, keepdims=True) + jnp.exp(sink - m)
            acc = _dot(p_c.astype(BF16), vc) + _dot(p_p.astype(BF16), vp)
            outs.append(acc * (1.0 / den))
        o_ref[:, c * LANES:(c + 1) * LANES] = jnp.where(lo, outs[0], outs[1])


def _band_b(q, k, v, sinks):
    bsz, seq, _ = q.shape
    qs = pl.BlockSpec((None, TQ, D_BQ), lambda b, t: (b, t, 0))
    cur = pl.BlockSpec((None, TQ, D_BKV), lambda b, t: (b, t, 0))
    prev = pl.BlockSpec((None, TQ, D_BKV), lambda b, t: (b, jnp.maximum(t - 1, 0), 0))
    o = pl.pallas_call(
        _band_b_kernel,
        grid=(bsz, seq // TQ),
        in_specs=[pl.BlockSpec(memory_space=pltpu.SMEM), qs, cur, prev, cur, prev],
        out_specs=qs,
        out_shape=jax.ShapeDtypeStruct((bsz, seq, D_BQ), F32),
        compiler_params=_params(2),
        name="band_b",
    )(sinks, q, k, k, v, v)
    return o.reshape(bsz * seq, D_BQ)


def _sample_tables():
    i = jnp.arange(DEC_SEQ)[:, None]
    j = jnp.arange(NK_PAD)[None, :]
    dist = WIN_A + i - j
    cnt = jnp.zeros((DEC_SEQ, NK_PAD), F32)
    for dil in DILATIONS:
        hit = (dist >= 0) & (dist <= 128 * dil) & (dist % dil == 0) & (j < WIN_A + DEC_SEQ)
        cnt = cnt + hit.astype(F32)
    cnt_a = jnp.tile(cnt, (N_HEADS_A, 1))
    jb = jnp.arange(NKB_PAD)[None, :]
    dist_b = WIN_B + i - jb
    ok_b = (dist_b >= 0) & (dist_b < WIN_B) & (jb < WIN_B + DEC_SEQ)
    mask_b = jnp.tile(ok_b.astype(F32), (N_HEADS_B, 1))
    return cnt_a, mask_b


def _attn_sample_kernel(qa_ref, kan_ref, van_ref, kac_ref, vac_ref, cnt_ref,
                        qb_ref, kbn_ref, vbn_ref, kbc_ref, vbc_ref, maskb_ref, sinkcol_ref,
                        o_ref, ka_s, va_s, kb_s, vb_s):
    n_pad_a = NK_PAD - WIN_A - DEC_SEQ
    ka_s[0:WIN_A, :] = kac_ref[...].astype(BF16)
    va_s[0:WIN_A, :] = vac_ref[...].astype(BF16)
    pad_a = jnp.zeros((n_pad_a, D_A), F32)
    ka_s[WIN_A:NK_PAD, :] = jnp.concatenate([kan_ref[...], pad_a], 0).astype(BF16)
    va_s[WIN_A:NK_PAD, :] = jnp.concatenate([van_ref[...], pad_a], 0).astype(BF16)

    q = qa_ref[...]
    rows = N_HEADS_A * DEC_SEQ
    q_rep = jnp.concatenate([q] * N_HEADS_A, axis=0)
    row_head = jnp.right_shift(lax.broadcasted_iota(jnp.int32, (rows, D_A), 0), int(math.log2(DEC_SEQ)))
    lane_head = jnp.right_shift(lax.broadcasted_iota(jnp.int32, (rows, D_A), 1), int(math.log2(HEAD_DIM)))
    own = row_head == lane_head
    q_bd = jnp.where(own, q_rep, 0.0).astype(BF16)
    cnt = cnt_ref[...]
    s = jnp.where(cnt > 0.0, _dot_nt(q_bd, ka_s[...]), NEG_INF)
    m = jnp.max(s, -1, keepdims=True)
    p = jnp.exp(s - m) * cnt
    den = jnp.sum(p, -1, keepdims=True)
    o_full = jnp.where(own, _dot(p.astype(BF16), va_s[...]) * (1.0 / den), 0.0)
    oa = o_full[0:DEC_SEQ]
    for h in range(1, N_HEADS_A):
        oa = oa + o_full[h * DEC_SEQ:(h + 1) * DEC_SEQ]
    o_ref[:, 0:D_A] = oa

    n_pad_b = NKB_PAD - WIN_B - DEC_SEQ
    pad_b = jnp.zeros((n_pad_b, D_BKV), F32)
    kb_s[...] = jnp.concatenate([kbc_ref[...], kbn_ref[...], pad_b], 0).astype(BF16)
    vb_s[...] = jnp.concatenate([vbc_ref[...], vbn_ref[...], pad_b], 0).astype(BF16)
    qb = qb_ref[...]
    lo8 = _lane_lo((DEC_SEQ, LANES))
    group = N_HEADS_B // N_KV_B
    pieces = []
    for h in range(N_HEADS_B):
        chunk = qb[:, (h // 2) * LANES:(h // 2 + 1) * LANES]
        g = h // group
        if h % 2 != g:
            chunk = pltpu.roll(chunk, HEAD_DIM, 1)
        pieces.append(jnp.where(lo8 if g == 0 else jnp.logical_not(lo8), chunk, 0.0))
    qb_bd = jnp.concatenate(pieces, axis=0).astype(BF16)
    mask_b = maskb_ref[...]
    sb = jnp.where(mask_b > 0.0, _dot_nt(qb_bd, kb_s[...]), NEG_INF)
    sink = sinkcol_ref[...][:, 0:1]
    mb = jnp.maximum(jnp.max(sb, -1, keepdims=True), sink)
    pb = jnp.exp(sb - mb) * mask_b
    den_b = jnp.sum(pb, -1, keepdims=True) + jnp.exp(sink - mb)
    ob_full = _dot(pb.astype(BF16), vb_s[...]) * (1.0 / den_b)
    for c in range(D_BQ // LANES):
        halves = []
        for half in range(2):
            h = 2 * c + half
            piece = ob_full[h * DEC_SEQ:(h + 1) * DEC_SEQ]
            if half != h // group:
                piece = pltpu.roll(piece, HEAD_DIM, 1)
            halves.append(piece)
        o_ref[:, D_A + c * LANES:D_A + (c + 1) * LANES] = jnp.where(lo8, halves[0], halves[1])


def _attn_sample(qa, kan, van, cache_ak, cache_av, qb, kbn, vbn, cache_bk, cache_bv, sinks):
    cnt_a, mask_b = _sample_tables()
    sink_col = jnp.broadcast_to(jnp.repeat(sinks.astype(F32), DEC_SEQ)[:, None], (N_HEADS_B * DEC_SEQ, LANES))
    new = lambda width: pl.BlockSpec((DEC_SEQ, width), lambda b: (b, 0))
    cache = lambda rows, width: pl.BlockSpec((None, rows, width), lambda b: (b, 0, 0))
    rows = N_HEADS_A * DEC_SEQ
    return pl.pallas_call(
        _attn_sample_kernel,
        grid=(DEC_BATCH,),
        in_specs=[new(D_A), new(D_A), new(D_A), cache(WIN_A, D_A), cache(WIN_A, D_A), _resident((rows, NK_PAD)),
                  new(D_BQ), new(D_BKV), new(D_BKV), cache(WIN_B, D_BKV), cache(WIN_B, D_BKV),
                  _resident((rows, NKB_PAD)), _resident((rows, LANES))],
        out_specs=pl.BlockSpec((DEC_SEQ, D_A + D_BQ), lambda b: (b, 0)),
        out_shape=jax.ShapeDtypeStruct((DEC_BATCH * DEC_SEQ, D_A + D_BQ), F32),
        scratch_shapes=[pltpu.VMEM((NK_PAD, D_A), BF16), pltpu.VMEM((NK_PAD, D_A), BF16),
                        pltpu.VMEM((NKB_PAD, D_BKV), BF16), pltpu.VMEM((NKB_PAD, D_BKV), BF16)],
        compiler_params=_params(1),
        name="attn_sample",
    )(qa, kan, van, cache_ak, cache_av, cnt_a, qb, kbn, vbn, cache_bk, cache_bv, mask_b, sink_col)


def _attn_out_prompt_kernel(o1_ref, l1_ref, o4_ref, l4_ref, o16_ref, l16_ref, ob_ref, y_ref, w_ref, g_ref, b_ref,
                            out_ref):
    l1, l4, l16 = l1_ref[...], l4_ref[...], l16_ref[...]
    m = jnp.maximum(jnp.maximum(l1, l4), l16)
    e1, e4, e16 = jnp.exp(l1 - m), jnp.exp(l4 - m), jnp.exp(l16 - m)
    oa = (e1 * o1_ref[...] + e4 * o4_ref[...] + e16 * o16_ref[...]) * (1.0 / (e1 + e4 + e16))
    mix = _dot(oa.astype(BF16), w_ref[0:D_A, :]) + _dot(ob_ref[...].astype(BF16), w_ref[D_A:, :])
    out_ref[...] = _layer_norm(DN_ALPHA * y_ref[...] + mix, g_ref[...], b_ref[...])


def _attn_out_prompt(pats, ob, y, w, g, b):
    n = y.shape[0]
    half = pl.BlockSpec((ROW_TILE, D_A), lambda i: (i, 0))
    full = pl.BlockSpec((ROW_TILE, D_MODEL), lambda i: (i, 0))
    flat = [a for pair in pats for a in pair]
    return pl.pallas_call(
        _attn_out_prompt_kernel,
        grid=(n // ROW_TILE,),
        in_specs=[half] * 7 + [full, _resident((D_MODEL, D_MODEL)), _resident((1, D_MODEL)), _resident((1, D_MODEL))],
        out_specs=full,
        out_shape=jax.ShapeDtypeStruct((n, D_MODEL), F32),
        compiler_params=_params(1),
        name="attn_out_prompt",
    )(*flat, ob, y, w, g, b)


def _mix_out_kernel(o_ref, y_ref, w_ref, g_ref, b_ref, out_ref):
    mix = _dot(o_ref[...].astype(BF16), w_ref[...])
    out_ref[...] = _layer_norm(DN_ALPHA * y_ref[...] + mix, g_ref[...], b_ref[...])


def _attn_out_sample(o, y, w, g, b):
    n = y.shape[0]
    tm = min(ROW_TILE, n)
    full = pl.BlockSpec((tm, D_MODEL), lambda i: (i, 0))
    return pl.pallas_call(
        _mix_out_kernel,
        grid=(n // tm,),
        in_specs=[full, full, _resident((D_MODEL, D_MODEL)), _resident((1, D_MODEL)), _resident((1, D_MODEL))],
        out_specs=full,
        out_shape=jax.ShapeDtypeStruct((n, D_MODEL), F32),
        compiler_params=_params(1),
        name="attn_out_sample",
    )(o, y, w, g, b)


def _ssm_discretize(lam_re, lam_im, log_dt, b_re, b_im):
    dt = jnp.exp(log_dt.astype(F32))[:, None]
    lr, li = lam_re.astype(F32), lam_im.astype(F32)
    mag = jnp.exp(lr * dt)
    ab_re, ab_im = mag * jnp.cos(li * dt), mag * jnp.sin(li * dt)
    nr, ni = ab_re - 1.0, ab_im
    den = lr * lr + li * li
    fr, fi = (nr * lr + ni * li) / den, (ni * lr - nr * li) / den
    bb_re = fr[..., None] * b_re - fi[..., None] * b_im
    bb_im = fr[..., None] * b_im + fi[..., None] * b_re
    return ab_re, ab_im, bb_re, bb_im


def _ssm_matrices(bb_re, bb_im, c_re, c_im):
    gpc = LANES // SSM_GROUP
    eye = jnp.eye(gpc, dtype=F32)

    def in_blocks(bb):
        a = bb.reshape(N_LCHUNK, gpc, SSM_STATE, SSM_GROUP)
        return jnp.einsum("jgpn,gh->jgnhp", a, eye).reshape(N_LCHUNK, LANES, ST_CHUNK)

    def out_blocks(cc):
        a = cc.reshape(N_LCHUNK, gpc, SSM_GROUP, SSM_STATE)
        return jnp.einsum("jgnp,gh->jgphn", a, eye).reshape(N_LCHUNK, ST_CHUNK, LANES)

    bmat = jnp.concatenate([in_blocks(bb_re), in_blocks(bb_im)], -1)
    cmat = jnp.concatenate([out_blocks(c_re), -out_blocks(c_im)], 1)
    return bmat.astype(BF16), cmat.astype(BF16)


def _matmul_kernel(x_ref, w_ref, o_ref):
    o_ref[...] = _dot(x_ref[...].astype(BF16), w_ref[...])


def _ssm_in_prompt(y, w):
    return pl.pallas_call(
        _matmul_kernel,
        grid=(BATCH, N_SEG),
        in_specs=[pl.BlockSpec((SEG_LEN, D_MODEL), lambda b, s: (b * N_SEG + s, 0)), _resident((D_MODEL, D_MODEL))],
        out_specs=pl.BlockSpec((None, SEG_LEN, D_MODEL), lambda b, s: (b, 0, s)),
        out_shape=jax.ShapeDtypeStruct((BATCH, SEG_LEN, N_SEG * D_MODEL), F32),
        compiler_params=_params(2),
        name="ssm_in_prompt",
    )(y, w).reshape(BATCH, SEQ, D_MODEL)


def _ssm_in_sample(y, w):
    return pl.pallas_call(
        _matmul_kernel,
        grid=(DEC_SEQ,),
        in_specs=[pl.BlockSpec((DEC_BATCH, D_MODEL), lambda l: (0, l)), _resident((D_MODEL, D_MODEL))],
        out_specs=pl.BlockSpec((None, DEC_BATCH, D_MODEL), lambda l: (l, 0, 0)),
        out_shape=jax.ShapeDtypeStruct((DEC_SEQ, DEC_BATCH, D_MODEL), F32),
        compiler_params=_params(1),
        name="ssm_in_sample",
    )(y.reshape(DEC_BATCH, DEC_SEQ * D_MODEL), w)


def _cmul(ar, ai, br, bi):
    return ar * br - ai * bi, ar * bi + ai * br


def _scan_prompt_kernel(u_ref, bmat_ref, cmat_ref, are_ref, aim_ref, d_ref, h0r_ref, h0i_ref,
                        y_ref, hnr_ref, hni_ref, bu_s, hs_s):
    n_tiles = SEQ // SCAN_TILE
    steps = SCAN_TILE // N_SEG
    for i in range(n_tiles):
        rows = slice(i * SCAN_TILE, (i + 1) * SCAN_TILE)
        bu_s[rows, :] = _dot(u_ref[rows, :].astype(BF16), bmat_ref[...])
    a_re1, a_im1 = are_ref[...], aim_ref[...]
    a_re = jnp.broadcast_to(a_re1, (N_SEG, ST_CHUNK))
    a_im = jnp.broadcast_to(a_im1, (N_SEG, ST_CHUNK))

    def advance(row, hr, hi):
        bur = bu_s[pl.ds(row, N_SEG), 0:ST_CHUNK]
        bui = bu_s[pl.ds(row, N_SEG), ST_CHUNK:2 * ST_CHUNK]
        return a_re * hr - a_im * hi + bur, a_re * hi + a_im * hr + bui

    def pass1(k, carry):
        return advance(pl.multiple_of(k * N_SEG, N_SEG), *carry)

    zero = jnp.zeros((N_SEG, ST_CHUNK), F32)
    er, ei = lax.fori_loop(0, SEG_LEN, pass1, (zero, zero), unroll=8)

    pr, pi = a_re1, a_im1
    for _ in range(int(math.log2(SEG_LEN))):
        pr, pi = _cmul(pr, pi, pr, pi)
    hr, hi = h0r_ref[...], h0i_ref[...]
    starts_r, starts_i = [], []
    for s in range(N_SEG):
        starts_r.append(hr)
        starts_i.append(hi)
        gr, gi = _cmul(pr, pi, hr, hi)
        hr, hi = gr + er[s:s + 1], gi + ei[s:s + 1]
    hnr_ref[...] = hr
    hni_ref[...] = hi
    init = (jnp.concatenate(starts_r, 0), jnp.concatenate(starts_i, 0))

    def tile(i, carry):
        base = pl.multiple_of(i * SCAN_TILE, SCAN_TILE)

        def pass2(k, c):
            off = pl.multiple_of(k * N_SEG, N_SEG)
            nr, ni = advance(base + off, *c)
            hs_s[pl.ds(off, N_SEG), 0:ST_CHUNK] = nr
            hs_s[pl.ds(off, N_SEG), ST_CHUNK:2 * ST_CHUNK] = ni
            return nr, ni

        carry = lax.fori_loop(0, steps, pass2, carry, unroll=8)
        y = _dot(hs_s[...].astype(BF16), cmat_ref[...]) + d_ref[...] * u_ref[pl.ds(base, SCAN_TILE), :]
        y_ref[pl.ds(base, SCAN_TILE), :] = y
        return carry

    lax.fori_loop(0, n_tiles, tile, init)


def _scan_prompt(u, bmat, cmat, a_re, a_im, d_skip, h0r, h0i):
    chunk = lambda rows: pl.BlockSpec((None, rows, LANES), lambda b, j: (b, 0, j))
    per_j = lambda r, c: pl.BlockSpec((None, r, c), lambda b, j: (j, 0, 0))
    state = pl.BlockSpec((None, 1, ST_CHUNK), lambda b, j: (b, 0, j))
    return pl.pallas_call(
        _scan_prompt_kernel,
        grid=(BATCH, N_LCHUNK),
        in_specs=[chunk(SEQ), per_j(LANES, 2 * ST_CHUNK), per_j(2 * ST_CHUNK, LANES), per_j(1, ST_CHUNK),
                  per_j(1, ST_CHUNK), per_j(1, LANES), state, state],
        out_specs=[chunk(SEQ), state, state],
        out_shape=[jax.ShapeDtypeStruct((BATCH, SEQ, D_MODEL), F32),
                   jax.ShapeDtypeStruct((BATCH, 1, N_STATE), F32), jax.ShapeDtypeStruct((BATCH, 1, N_STATE), F32)],
        scratch_shapes=[pltpu.VMEM((SEQ, 2 * ST_CHUNK), F32), pltpu.VMEM((SCAN_TILE, 2 * ST_CHUNK), F32)],
        compiler_params=_params(2),
        name="ssm_scan_prompt",
    )(u, bmat, cmat, a_re, a_im, d_skip, h0r, h0i)


def _scan_sample_kernel(u_ref, bmat_ref, cmat_ref, are_ref, aim_ref, d_ref, h0r_ref, h0i_ref,
                        y_ref, hnr_ref, hni_ref):
    a_re, a_im = are_ref[...], aim_ref[...]
    hr, hi = h0r_ref[...], h0i_ref[...]
    for l in range(DEC_SEQ):
        u = u_ref[l]
        bu = _dot(u.astype(BF16), bmat_ref[...])
        gr, gi = _cmul(a_re, a_im, hr, hi)
        hr, hi = gr + bu[:, :ST_CHUNK], gi + bu[:, ST_CHUNK:]
        h = jnp.concatenate([hr, hi], axis=1).astype(BF16)
        y_ref[l] = _dot(h, cmat_ref[...]) + d_ref[...] * u
    hnr_ref[...] = hr
    hni_ref[...] = hi


def _scan_sample(u, bmat, cmat, a_re, a_im, d_skip, h0r, h0i):
    chunk = pl.BlockSpec((DEC_SEQ, DEC_BATCH, LANES), lambda j: (0, 0, j))
    per_j = lambda r, c: pl.BlockSpec((None, r, c), lambda j: (j, 0, 0))
    state = pl.BlockSpec((DEC_BATCH, ST_CHUNK), lambda j: (0, j))
    return pl.pallas_call(
        _scan_sample_kernel,
        grid=(N_LCHUNK,),
        in_specs=[chunk, per_j(LANES, 2 * ST_CHUNK), per_j(2 * ST_CHUNK, LANES), per_j(1, ST_CHUNK),
                  per_j(1, ST_CHUNK), per_j(1, LANES), state, state],
        out_specs=[chunk, state, state],
        out_shape=[jax.ShapeDtypeStruct((DEC_SEQ, DEC_BATCH, D_MODEL), F32),
                   jax.ShapeDtypeStruct((DEC_BATCH, N_STATE), F32), jax.ShapeDtypeStruct((DEC_BATCH, N_STATE), F32)],
        compiler_params=_params(1),
        name="ssm_scan_sample",
    )(u, bmat, cmat, a_re, a_im, d_skip, h0r, h0i)


def _ssm_out_kernel(s_ref, y_ref, wglu_ref, bglu_ref, wout_ref, g_ref, b_ref, out_ref):
    z = jax.nn.gelu(s_ref[...])
    gate = jax.nn.sigmoid(_dot(z.astype(BF16), wglu_ref[...]) + bglu_ref[...])
    mix = _dot((z * gate).astype(BF16), wout_ref[...])
    out_ref[...] = _layer_norm(DN_ALPHA * y_ref[...] + mix, g_ref[...], b_ref[...])


def _ssm_out(s, y, s_spec, y_spec, grid, w_glu, b_glu, w_out, g, b, name):
    return pl.pallas_call(
        _ssm_out_kernel,
        grid=grid,
        in_specs=[s_spec, y_spec, _resident((D_MODEL, D_MODEL)), _resident((1, D_MODEL)),
                  _resident((D_MODEL, D_MODEL)), _resident((1, D_MODEL)), _resident((1, D_MODEL))],
        out_specs=y_spec,
        out_shape=jax.ShapeDtypeStruct(y.shape, F32),
        compiler_params=_params(len(grid)),
        name=name,
    )(s, y, w_glu, b_glu, w_out, g, b)


def _attn_prompt(yp, w_in, sinks, w_out, g, b):
    cos_p, sin_p = _rope_tables(jnp.arange(SEQ))
    tiles_per_seq = SEQ // ROW_TILE
    qa, ka, va, qb, kb, vb, kaf, vaf, kbf, vbf = _attn_proj(
        yp, w_in, cos_p, sin_p, lambda i: i % tiles_per_seq, BF16)
    seq3 = lambda a: a.reshape(BATCH, SEQ, a.shape[-1])
    pats = [_band_a(seq3(qa), seq3(ka), seq3(va), dil) for dil in DILATIONS]
    ob = _band_b(seq3(qb), seq3(kb), seq3(vb), sinks)
    yp = _attn_out_prompt(pats, ob, yp, w_out, g, b)
    heads = lambda a, n, nh: a.reshape(BATCH, SEQ, nh, HEAD_DIM)[None, :, SEQ - n:]
    prompt_cache = (heads(kaf, WIN_A, N_HEADS_A), heads(vaf, WIN_A, N_HEADS_A),
                    heads(kbf, WIN_B, N_KV_B), heads(vbf, WIN_B, N_KV_B))
    return yp, prompt_cache


def _attn_sample_path(ys, cache_ak, cache_av, cache_bk, cache_bv, w_in, sinks, w_out, g, b):
    cos_s, sin_s = _rope_tables(PAST_LEN + jnp.arange(DEC_SEQ))
    reps = DEC_BATCH * DEC_SEQ // DEC_SEQ
    cos_s, sin_s = jnp.tile(cos_s, (reps, 1)), jnp.tile(sin_s, (reps, 1))
    qa, ka, va, qb, kb, vb, _, _, _, _ = _attn_proj(ys, w_in, cos_s, sin_s, lambda i: i, F32)
    o = _attn_sample(qa, ka, va, cache_ak.reshape(DEC_BATCH, WIN_A, D_A), cache_av.reshape(DEC_BATCH, WIN_A, D_A),
                     qb, kb, vb, cache_bk.reshape(DEC_BATCH, WIN_B, D_BKV), cache_bv.reshape(DEC_BATCH, WIN_B, D_BKV),
                     sinks)
    ys = _attn_out_sample(o, ys, w_out, g, b)
    new = lambda a, nh: a.reshape(1, DEC_BATCH, DEC_SEQ, nh, HEAD_DIM)
    sample_cache = (new(ka, N_HEADS_A), new(va, N_HEADS_A), new(kb, N_KV_B), new(vb, N_KV_B))
    return ys, sample_cache


def _attn_layer(yp, ys, cache_ak, cache_av, cache_bk, cache_bv, w_in, sinks, w_out, g, b):
    w_in = w_in.astype(BF16)
    w_out = w_out.astype(BF16)
    yp, prompt_cache = _attn_prompt(yp, w_in, sinks, w_out, g, b)
    ys, sample_cache = _attn_sample_path(ys, cache_ak, cache_av, cache_bk, cache_bv, w_in, sinks, w_out, g, b)
    return yp, ys, prompt_cache, sample_cache


def _ssm_layer(yp, ys, state_re, state_im, w_in, lam_re, lam_im, log_dt, b_re, b_im, c_re, c_im, d_skip,
               w_glu, b_glu, w_out, g, b):
    w_in, w_glu, w_out = w_in.astype(BF16), w_glu.astype(BF16), w_out.astype(BF16)
    ab_re, ab_im, bb_re, bb_im = _ssm_discretize(lam_re, lam_im, log_dt, b_re, b_im)
    bmat, cmat = _ssm_matrices(bb_re, bb_im, c_re, c_im)
    a_re = ab_re.reshape(N_LCHUNK, 1, ST_CHUNK)
    a_im = ab_im.reshape(N_LCHUNK, 1, ST_CHUNK)
    d3 = d_skip.astype(F32).reshape(N_LCHUNK, 1, LANES)
    b_glu = b_glu.reshape(1, D_MODEL)
    mats = (bmat, cmat, a_re, a_im, d3)
    yp, prompt_state = _ssm_prompt(yp, w_in, mats, w_glu, b_glu, w_out, g, b)
    ys, sample_state = _ssm_sample(ys, state_re, state_im, w_in, mats, w_glu, b_glu, w_out, g, b)
    return yp, ys, prompt_state, sample_state


def _ssm_prompt(yp, w_in, mats, w_glu, b_glu, w_out, g, b):
    bmat, cmat, a_re, a_im, d3 = mats
    up = _ssm_in_prompt(yp, w_in)
    zero = jnp.zeros((BATCH, 1, N_STATE), F32)
    sp, pr, pi = _scan_prompt(up, bmat, cmat, a_re, a_im, d3, zero, zero)
    yp = _ssm_out(sp.reshape(BATCH, SEG_LEN, N_SEG * D_MODEL), yp,
                  pl.BlockSpec((None, SEG_LEN, D_MODEL), lambda bb, s: (bb, 0, s)),
                  pl.BlockSpec((SEG_LEN, D_MODEL), lambda bb, s: (bb * N_SEG + s, 0)),
                  (BATCH, N_SEG), w_glu, b_glu, w_out, g, b, "ssm_out_prompt")
    prompt_state = (pr.reshape(1, BATCH, N_SSM_GROUPS, SSM_STATE), pi.reshape(1, BATCH, N_SSM_GROUPS, SSM_STATE))
    return yp, prompt_state


def _ssm_sample(ys, state_re, state_im, w_in, mats, w_glu, b_glu, w_out, g, b):
    bmat, cmat, a_re, a_im, d3 = mats
    us = _ssm_in_sample(ys, w_in)
    ss, sr, si = _scan_sample(us, bmat, cmat, a_re, a_im, d3,
                              state_re.reshape(DEC_BATCH, N_STATE), state_im.reshape(DEC_BATCH, N_STATE))
    ys = _ssm_out(ss, ys.reshape(DEC_BATCH, DEC_SEQ * D_MODEL),
                  pl.BlockSpec((None, DEC_BATCH, D_MODEL), lambda l: (l, 0, 0)),
                  pl.BlockSpec((DEC_BATCH, D_MODEL), lambda l: (0, l)),
                  (DEC_SEQ,), w_glu, b_glu, w_out, g, b, "ssm_out_sample").reshape(DEC_BATCH * DEC_SEQ, D_MODEL)
    sample_state = (sr.reshape(1, DEC_BATCH, N_SSM_GROUPS, SSM_STATE), si.reshape(1, DEC_BATCH, N_SSM_GROUPS, SSM_STATE))
    return ys, sample_state


def kernel(x_prompt, x_sample, cache_a_k, cache_a_v, cache_b_k, cache_b_v, state_c_re, state_c_im, ln_g, ln_b, ffn_w_gate, ffn_w_up, ffn_w_down, attn_w_in, attn_sinks, attn_w_out, ssm_w_in, ssm_lambda_re, ssm_lambda_im, ssm_log_dt, ssm_b_re, ssm_b_im, ssm_c_re, ssm_c_im, ssm_d, ssm_w_glu, ssm_b_glu, ssm_w_out):
    yp = x_prompt.reshape(BATCH * SEQ, D_MODEL)
    ys = x_sample.reshape(DEC_BATCH * DEC_SEQ, D_MODEL)
    ln = lambda l, k: (ln_g[l, k].reshape(1, D_MODEL), ln_b[l, k].reshape(1, D_MODEL))

    def ffn_pair(yp, ys, l, k, ln_idx):
        wg, wu, wd = ffn_w_gate[l, k].astype(BF16), ffn_w_up[l, k].astype(BF16), ffn_w_down[l, k].astype(BF16)
        g, b = ln(l, ln_idx)
        return _ffn(yp, wg, wu, wd, g, b), _ffn(ys, wg, wu, wd, g, b)

    yp, ys = ffn_pair(yp, ys, 0, 0, 0)
    yp, ys, p_cache, s_cache = _attn_layer(yp, ys, cache_a_k[0], cache_a_v[0], cache_b_k[0], cache_b_v[0],
                                           attn_w_in[0], attn_sinks[0], attn_w_out[0], *ln(0, 1))
    yp, ys = ffn_pair(yp, ys, 0, 1, 2)
    yp, ys = ffn_pair(yp, ys, 1, 0, 0)
    yp, ys, p_state, s_state = _ssm_layer(yp, ys, state_c_re[0], state_c_im[0], ssm_w_in[0], ssm_lambda_re[0],
                                          ssm_lambda_im[0], ssm_log_dt[0], ssm_b_re[0], ssm_b_im[0], ssm_c_re[0],
                                          ssm_c_im[0], ssm_d[0], ssm_w_glu[0], ssm_b_glu[0], ssm_w_out[0], *ln(1, 1))
    yp, ys = ffn_pair(yp, ys, 1, 1, 2)
    return (yp.reshape(BATCH, SEQ, D_MODEL), ys.reshape(DEC_BATCH, DEC_SEQ, D_MODEL),
            *p_cache, *p_state, *s_cache, *s_state)
```

```python
import functools
import math

import jax
import jax.numpy as jnp
from jax import lax
from jax.experimental import pallas as pl
from jax.experimental.pallas import tpu as pltpu

F32 = jnp.float32
BF16 = jnp.bfloat16

D_MODEL = 1024
BATCH = 4
SEQ = 4096
DEPTH = 2
DEC_BATCH = 128
DEC_SEQ = 8
PAST_LEN = 16384
HEAD_DIM = 64
N_HEADS_A = 8
DILATIONS = (1, 4, 16)
WIN_A = 2048
N_HEADS_B = 8
N_KV_B = 2
WIN_B = 128
ROPE_THETA = 10000.0
D_A = N_HEADS_A * HEAD_DIM
D_BQ = N_HEADS_B * HEAD_DIM
D_BKV = N_KV_B * HEAD_DIM
D_IN_ATTN = 3 * D_A + D_BQ + 2 * D_BKV
SSM_GROUP = 16
N_SSM_GROUPS = D_MODEL // SSM_GROUP
SSM_STATE = 64
N_STATE = N_SSM_GROUPS * SSM_STATE
D_FF = 2816
DN_ALPHA = (2 * DEPTH) ** 0.25
FFN_RES = 0.5
LN_EPS = 1e-5
ATTN_SCALE = HEAD_DIM ** -0.5

LANES = 128
SUBLANES = 8
MXU_N = 256
VMEM_LIMIT = 56 * 1024 * 1024

ROW_TILE = 512
FF_CHUNK = MXU_N
TQ = 128
N_SEG = SUBLANES
SEG_LEN = SEQ // N_SEG
N_LCHUNK = D_MODEL // LANES
ST_CHUNK = N_STATE // N_LCHUNK
SCAN_TILE = 512
NK_PAD = WIN_A + LANES
NKB_PAD = 2 * WIN_B

NEG_INF = float("-inf")


def _params(n_axes, vmem=VMEM_LIMIT):
    return pltpu.CompilerParams(dimension_semantics=("arbitrary",) * n_axes, vmem_limit_bytes=vmem)


def _resident(shape):
    return pl.BlockSpec(shape, lambda *_: (0,) * len(shape), pipeline_mode=pl.Buffered(1))


def _layer_norm(x, g, b):
    mu = jnp.mean(x, -1, keepdims=True)
    xc = x - mu
    var = jnp.mean(xc * xc, -1, keepdims=True)
    return xc * lax.rsqrt(var + LN_EPS) * g + b


def _dot(a, b):
    return jnp.dot(a, b, preferred_element_type=F32)


def _dot_nt(a, b):
    return lax.dot_general(a, b, (((1,), (1,)), ((), ())), preferred_element_type=F32)


def _ffn_kernel(x_ref, wg_ref, wu_ref, wd_ref, g_ref, b_ref, o_ref, h_ref):
    x = x_ref[...]
    xb = x.astype(BF16)
    for c in range(D_FF // FF_CHUNK):
        sl = slice(c * FF_CHUNK, (c + 1) * FF_CHUNK)
        gate = _dot(xb, wg_ref[:, sl])
        up = _dot(xb, wu_ref[:, sl])
        h_ref[:, sl] = (gate * jax.nn.sigmoid(gate) * up).astype(BF16)
    y = DN_ALPHA * x + FFN_RES * _dot(h_ref[...], wd_ref[...])
    o_ref[...] = _layer_norm(y, g_ref[...], b_ref[...])


def _ffn(x, wg, wu, wd, g, b, layer=0, which=0):
    n = x.shape[0]
    tm = min(ROW_TILE, n)
    row = pl.BlockSpec((tm, D_MODEL), lambda i: (i, 0))
    if wg.ndim == 4:
        pick = lambda r, c: pl.BlockSpec((None, None, r, c), lambda i: (layer, which, 0, 0),
                                         pipeline_mode=pl.Buffered(1))
    else:
        pick = lambda r, c: _resident((r, c))
    return pl.pallas_call(
        _ffn_kernel,
        grid=(n // tm,),
        in_specs=[row, pick(D_MODEL, D_FF), pick(D_MODEL, D_FF), pick(D_FF, D_MODEL),
                  _resident((1, D_MODEL)), _resident((1, D_MODEL))],
        out_specs=row,
        out_shape=jax.ShapeDtypeStruct((n, D_MODEL), F32),
        scratch_shapes=[pltpu.VMEM((tm, D_FF), BF16)],
        compiler_params=_params(1),
        name="ffn",
    )(x, wg, wu, wd, g, b)


def _rope_tables(pos):
    half = HEAD_DIM // 2
    inv_freq = ROPE_THETA ** (-jnp.arange(half, dtype=F32) / half)
    ang = pos.astype(F32)[:, None] * inv_freq[None, :]
    cos, sin = jnp.cos(ang), jnp.sin(ang)
    cos_t = jnp.concatenate([cos, cos, cos, cos], -1)
    sin_t = jnp.concatenate([-sin, sin, -sin, sin], -1)
    return cos_t, sin_t


def _attn_proj_kernel(x_ref, w_ref, cos_ref, sin_ref, qa_ref, ka_ref, va_ref, qb_ref, kb_ref, vb_ref,
                      kaf_ref, vaf_ref, kbf_ref, vbf_ref, *dilated):
    xb = x_ref[...].astype(BF16)
    slab_ref = dilated[-1] if dilated else None
    n_chunks = D_A // LANES

    def keep(tensor, c, val):
        if slab_ref is not None:
            slab_ref[tensor * n_chunks + c] = val
    cos = cos_ref[...]
    sin = sin_ref[...]
    lane = lax.broadcasted_iota(jnp.int32, cos.shape, 1)
    first_half = (lane & (HEAD_DIM // 2)) == 0

    def rope(z):
        rot = jnp.where(first_half, pltpu.roll(z, LANES - HEAD_DIM // 2, 1), pltpu.roll(z, HEAD_DIM // 2, 1))
        return z * cos + rot * sin

    def project(col0, ncols):
        return _dot(xb, w_ref[:, col0:col0 + ncols])

    def rope_chunks(z):
        return [rope(z[:, c * LANES:(c + 1) * LANES]) for c in range(z.shape[1] // LANES)]

    col = 0
    for c, r in enumerate(rope_chunks(project(col, D_A))):
        r = r * ATTN_SCALE
        qa_ref[:, c * LANES:(c + 1) * LANES] = r.astype(qa_ref.dtype)
        keep(0, c, r)
    col += D_A
    for c, r in enumerate(rope_chunks(project(col, D_A))):
        ka_ref[:, c * LANES:(c + 1) * LANES] = r.astype(ka_ref.dtype)
        kaf_ref[:, c * LANES:(c + 1) * LANES] = r
        keep(1, c, r)
    col += D_A
    z = project(col, D_A)
    va_ref[...] = z.astype(va_ref.dtype)
    vaf_ref[...] = z
    for c in range(n_chunks):
        keep(2, c, z[:, c * LANES:(c + 1) * LANES])
    col += D_A
    if dilated:
        tm = x_ref.shape[0]
        for i, dil in enumerate(DILATIONS[1:]):
            for tensor in range(3):
                out_ref = dilated[3 * i + tensor]
                for r in range(dil):
                    for c in range(n_chunks):
                        rows = slab_ref[tensor * n_chunks + c, pl.ds(r, tm // dil, stride=dil), :]
                        out_ref[r, :, c * LANES:(c + 1) * LANES] = rows.astype(out_ref.dtype)
    for c, r in enumerate(rope_chunks(project(col, D_BQ))):
        qb_ref[:, c * LANES:(c + 1) * LANES] = (r * ATTN_SCALE).astype(qb_ref.dtype)
    col += D_BQ
    z = project(col, 2 * D_BKV)
    r = rope(z[:, :D_BKV])
    kb_ref[...] = r.astype(kb_ref.dtype)
    kbf_ref[...] = r
    vb_ref[...] = z[:, D_BKV:].astype(vb_ref.dtype)
    vbf_ref[...] = z[:, D_BKV:]


def _attn_proj(x, w, cos_t, sin_t, table_block, act_dtype, dilated_seq=None):
    n = x.shape[0]
    tm = min(ROW_TILE, n)

    def row(width):
        return pl.BlockSpec((tm, width), lambda i: (i, 0))

    tab = pl.BlockSpec((tm, LANES), lambda i: (table_block(i), 0))
    widths = (D_A, D_A, D_A, D_BQ, D_BKV, D_BKV)
    out_shape = [jax.ShapeDtypeStruct((n, wd), act_dtype) for wd in widths]
    out_shape += [jax.ShapeDtypeStruct((n, wd), F32) for wd in (D_A, D_A, D_BKV, D_BKV)]
    out_specs = [row(wd) for wd in widths] + [row(D_A), row(D_A), row(D_BKV), row(D_BKV)]
    scratch = []
    if dilated_seq is not None:
        bsz, seq = dilated_seq
        tps = seq // tm
        for dil in DILATIONS[1:]:
            out_shape += [jax.ShapeDtypeStruct((bsz, dil, seq // dil, D_A), BF16)] * 3
            out_specs += [pl.BlockSpec((None, dil, tm // dil, D_A), lambda i: (i // tps, 0, i % tps, 0))] * 3
        scratch = [pltpu.VMEM((3 * D_A // LANES, tm, LANES), F32)]
    return pl.pallas_call(
        _attn_proj_kernel,
        grid=(n // tm,),
        in_specs=[row(D_MODEL), _resident((D_MODEL, D_IN_ATTN)), tab, tab],
        out_specs=out_specs,
        out_shape=out_shape,
        scratch_shapes=scratch,
        compiler_params=_params(1),
        name="attn_proj",
    )(x, w, cos_t, sin_t)


def _lane_lo(shape):
    return lax.broadcasted_iota(jnp.int32, shape, 1) < HEAD_DIM


def _half_masks_bf16():
    lo = jnp.where(_lane_lo((1, LANES)), 1.0, 0.0).astype(BF16)
    return lo, 1 - lo


def _band_masks(n_heads, t, sub, prev_strict):
    row = lax.broadcasted_iota(jnp.int32, (n_heads * TQ, TQ), 0) & (TQ - 1)
    col = lax.broadcasted_iota(jnp.int32, (n_heads * TQ, TQ), 1)
    mask_c = col <= row
    shift = jnp.where(t > 0, 0, TQ) if sub == 0 else 0
    mask_p = (col > row + shift) if prev_strict else (col >= row + shift)
    return mask_c, mask_p


def _sub_tile_kv(sub, sl, kc_ref, kp_ref, vc_ref, vp_ref):
    if sub == 0:
        return kc_ref[0:TQ, sl], kp_ref[:, sl], vc_ref[0:TQ, sl], vp_ref[:, sl]
    return kc_ref[TQ:2 * TQ, sl], kc_ref[0:TQ, sl], vc_ref[TQ:2 * TQ, sl], vc_ref[0:TQ, sl]


def _band_softmax(qs, kc, kp, vc, vp, mask_c, mask_p, sink=None):
    s_c = jnp.where(mask_c, _dot_nt(qs, kc), NEG_INF)
    s_p = jnp.where(mask_p, _dot_nt(qs, kp), NEG_INF)
    m = jnp.max(jnp.maximum(s_c, s_p), -1, keepdims=True)
    if sink is not None:
        m = jnp.maximum(m, sink)
    p_c = jnp.exp(s_c - m)
    p_p = jnp.exp(s_p - m)
    den = jnp.sum(p_c + p_p, -1, keepdims=True)
    if sink is not None:
        den = den + jnp.exp(sink - m)
    acc = _dot(p_c.astype(BF16), vc) + _dot(p_p.astype(BF16), vp)
    return acc * (1.0 / den), m, den


def _band_a_kernel(q_ref, kc_ref, kp_ref, vc_ref, vp_ref, o_ref, lse_ref):
    t = pl.program_id(2)
    lo = _lane_lo((TQ, LANES))
    lo_bf, hi_bf = _half_masks_bf16()
    for sub in range(2):
        rows = slice(sub * TQ, (sub + 1) * TQ)
        mask_c, mask_p = _band_masks(2, t, sub, prev_strict=False)
        for c in range(D_A // LANES):
            sl = slice(c * LANES, (c + 1) * LANES)
            kc, kp, vc, vp = _sub_tile_kv(sub, sl, kc_ref, kp_ref, vc_ref, vp_ref)
            q2 = q_ref[rows, sl]
            qs = jnp.concatenate([q2 * lo_bf, q2 * hi_bf], axis=0)
            out, m, den = _band_softmax(qs, kc, kp, vc, vp, mask_c, mask_p)
            lse = jnp.broadcast_to(m + jnp.log(den), (2 * TQ, LANES))
            o_ref[rows, sl] = jnp.where(lo, out[0:TQ], out[TQ:])
            lse_ref[rows, sl] = jnp.where(lo, lse[0:TQ], lse[TQ:])


def _band_a(q, k, v):
    bsz, dil, sub, _ = q.shape
    cur = pl.BlockSpec((None, None, 2 * TQ, D_A), lambda b, r, t: (b, r, t, 0))
    prev = pl.BlockSpec((None, None, TQ, D_A), lambda b, r, t: (b, r, jnp.maximum(2 * t - 1, 0), 0))
    return pl.pallas_call(
        _band_a_kernel,
        grid=(bsz, dil, sub // (2 * TQ)),
        in_specs=[cur, cur, prev, cur, prev],
        out_specs=[cur, cur],
        out_shape=[jax.ShapeDtypeStruct((bsz, dil, sub, D_A), F32)] * 2,
        compiler_params=_params(3),
        name=f"band_a_d{dil}",
    )(q, k, k, v, v)


def _swap_halves(x):
    return jnp.concatenate([x[:, HEAD_DIM:], x[:, :HEAD_DIM]], axis=1)


def _band_b_kernel(sink_ref, q_ref, kc_ref, kp_ref, vc_ref, vp_ref, o_ref):
    t = pl.program_id(1)
    group = N_HEADS_B // N_KV_B
    lo = _lane_lo((TQ, LANES))
    lo_bf, hi_bf = _half_masks_bf16()
    for sub in range(2):
        rows = slice(sub * TQ, (sub + 1) * TQ)
        mask_c, mask_p = _band_masks(group, t, sub, prev_strict=True)
        kv = _sub_tile_kv(sub, slice(0, D_BKV), kc_ref, kp_ref, vc_ref, vp_ref)
        kv_swapped = tuple(_swap_halves(a) for a in kv)
        for g in range(N_KV_B):
            own, other = (lo_bf, hi_bf) if g == 0 else (hi_bf, lo_bf)
            kc, kp, vc, vp = (a * own + a_sw * other for a, a_sw in zip(kv, kv_swapped))
            heads = range(g * group, (g + 1) * group)
            qs = jnp.concatenate(
                [q_ref[rows, (h // 2) * LANES:(h // 2 + 1) * LANES] * (lo_bf if h % 2 == 0 else hi_bf) for h in heads],
                axis=0)
            sink = jnp.concatenate([jnp.full((TQ, 1), sink_ref[h], F32) for h in heads], axis=0)
            out, _, _ = _band_softmax(qs, kc, kp, vc, vp, mask_c, mask_p, sink)
            for i in range(group // 2):
                c = g * (group // 2) + i
                even, odd = out[2 * i * TQ:(2 * i + 1) * TQ], out[(2 * i + 1) * TQ:(2 * i + 2) * TQ]
                o_ref[rows, c * LANES:(c + 1) * LANES] = jnp.where(lo, even, odd)


def _band_b(q, k, v, sinks):
    bsz, seq, _ = q.shape
    qs = pl.BlockSpec((None, 2 * TQ, D_BQ), lambda b, t: (b, t, 0))
    cur = pl.BlockSpec((None, 2 * TQ, D_BKV), lambda b, t: (b, t, 0))
    prev = pl.BlockSpec((None, TQ, D_BKV), lambda b, t: (b, jnp.maximum(2 * t - 1, 0), 0))
    o = pl.pallas_call(
        _band_b_kernel,
        grid=(bsz, seq // (2 * TQ)),
        in_specs=[pl.BlockSpec(memory_space=pltpu.SMEM), qs, cur, prev, cur, prev],
        out_specs=qs,
        out_shape=jax.ShapeDtypeStruct((bsz, seq, D_BQ), F32),
        compiler_params=_params(2),
        name="band_b",
    )(sinks, q, k, k, v, v)
    return o.reshape(bsz * seq, D_BQ)


def _sample_tables():
    i = jnp.arange(DEC_SEQ)[:, None]
    j = jnp.arange(NK_PAD)[None, :]
    dist = WIN_A + i - j
    cnt = jnp.zeros((DEC_SEQ, NK_PAD), F32)
    for dil in DILATIONS:
        hit = (dist >= 0) & (dist <= 128 * dil) & (dist % dil == 0) & (j < WIN_A + DEC_SEQ)
        cnt = cnt + hit.astype(F32)
    cnt_a = jnp.tile(cnt, (N_HEADS_A, 1))
    jb = jnp.arange(NKB_PAD)[None, :]
    dist_b = WIN_B + i - jb
    ok_b = (dist_b >= 0) & (dist_b < WIN_B) & (jb < WIN_B + DEC_SEQ)
    mask_b = jnp.tile(ok_b.astype(F32), (N_HEADS_B, 1))
    return cnt_a, mask_b


def _attn_sample_kernel(qa_ref, kan_ref, van_ref, kac_ref, vac_ref, cnt_ref,
                        qb_ref, kbn_ref, vbn_ref, kbc_ref, vbc_ref, maskb_ref, sinkcol_ref,
                        o_ref, ka_s, va_s, kb_s, vb_s):
    n_pad_a = NK_PAD - WIN_A - DEC_SEQ
    ka_s[0:WIN_A, :] = kac_ref[...].astype(BF16)
    va_s[0:WIN_A, :] = vac_ref[...].astype(BF16)
    pad_a = jnp.zeros((n_pad_a, D_A), F32)
    ka_s[WIN_A:NK_PAD, :] = jnp.concatenate([kan_ref[...], pad_a], 0).astype(BF16)
    va_s[WIN_A:NK_PAD, :] = jnp.concatenate([van_ref[...], pad_a], 0).astype(BF16)

    q = qa_ref[...]
    rows = N_HEADS_A * DEC_SEQ
    q_rep = jnp.concatenate([q] * N_HEADS_A, axis=0)
    row_head = jnp.right_shift(lax.broadcasted_iota(jnp.int32, (rows, D_A), 0), int(math.log2(DEC_SEQ)))
    lane_head = jnp.right_shift(lax.broadcasted_iota(jnp.int32, (rows, D_A), 1), int(math.log2(HEAD_DIM)))
    own = row_head == lane_head
    q_bd = jnp.where(own, q_rep, 0.0).astype(BF16)
    cnt = cnt_ref[...]
    s = jnp.where(cnt > 0.0, _dot_nt(q_bd, ka_s[...]), NEG_INF)
    m = jnp.max(s, -1, keepdims=True)
    p = jnp.exp(s - m) * cnt
    den = jnp.sum(p, -1, keepdims=True)
    o_full = jnp.where(own, _dot(p.astype(BF16), va_s[...]) * (1.0 / den), 0.0)
    oa = o_full[0:DEC_SEQ]
    for h in range(1, N_HEADS_A):
        oa = oa + o_full[h * DEC_SEQ:(h + 1) * DEC_SEQ]
    o_ref[:, 0:D_A] = oa

    n_pad_b = NKB_PAD - WIN_B - DEC_SEQ
    pad_b = jnp.zeros((n_pad_b, D_BKV), F32)
    kb_s[...] = jnp.concatenate([kbc_ref[...], kbn_ref[...], pad_b], 0).astype(BF16)
    vb_s[...] = jnp.concatenate([vbc_ref[...], vbn_ref[...], pad_b], 0).astype(BF16)
    qb = qb_ref[...]
    lo8 = _lane_lo((DEC_SEQ, LANES))
    group = N_HEADS_B // N_KV_B
    pieces = []
    for h in range(N_HEADS_B):
        chunk = qb[:, (h // 2) * LANES:(h // 2 + 1) * LANES]
        g = h // group
        if h % 2 != g:
            chunk = pltpu.roll(chunk, HEAD_DIM, 1)
        pieces.append(jnp.where(lo8 if g == 0 else jnp.logical_not(lo8), chunk, 0.0))
    qb_bd = jnp.concatenate(pieces, axis=0).astype(BF16)
    mask_b = maskb_ref[...]
    sb = jnp.where(mask_b > 0.0, _dot_nt(qb_bd, kb_s[...]), NEG_INF)
    sink = sinkcol_ref[...][:, 0:1]
    mb = jnp.maximum(jnp.max(sb, -1, keepdims=True), sink)
    pb = jnp.exp(sb - mb) * mask_b
    den_b = jnp.sum(pb, -1, keepdims=True) + jnp.exp(sink - mb)
    ob_full = _dot(pb.astype(BF16), vb_s[...]) * (1.0 / den_b)
    for c in range(D_BQ // LANES):
        halves = []
        for half in range(2):
            h = 2 * c + half
            piece = ob_full[h * DEC_SEQ:(h + 1) * DEC_SEQ]
            if half != h // group:
                piece = pltpu.roll(piece, HEAD_DIM, 1)
            halves.append(piece)
        o_ref[:, D_A + c * LANES:D_A + (c + 1) * LANES] = jnp.where(lo8, halves[0], halves[1])


def _attn_sample(qa, kan, van, cache_ak, cache_av, qb, kbn, vbn, cache_bk, cache_bv, sinks):
    cnt_a, mask_b = _sample_tables()
    sink_col = jnp.broadcast_to(jnp.repeat(sinks.astype(F32), DEC_SEQ)[:, None], (N_HEADS_B * DEC_SEQ, LANES))
    new = lambda width: pl.BlockSpec((DEC_SEQ, width), lambda b: (b, 0))
    cache = lambda rows, width: pl.BlockSpec((None, rows, width), lambda b: (b, 0, 0))
    rows = N_HEADS_A * DEC_SEQ
    return pl.pallas_call(
        _attn_sample_kernel,
        grid=(DEC_BATCH,),
        in_specs=[new(D_A), new(D_A), new(D_A), cache(WIN_A, D_A), cache(WIN_A, D_A), _resident((rows, NK_PAD)),
                  new(D_BQ), new(D_BKV), new(D_BKV), cache(WIN_B, D_BKV), cache(WIN_B, D_BKV),
                  _resident((rows, NKB_PAD)), _resident((rows, LANES))],
        out_specs=pl.BlockSpec((DEC_SEQ, D_A + D_BQ), lambda b: (b, 0)),
        out_shape=jax.ShapeDtypeStruct((DEC_BATCH * DEC_SEQ, D_A + D_BQ), F32),
        scratch_shapes=[pltpu.VMEM((NK_PAD, D_A), BF16), pltpu.VMEM((NK_PAD, D_A), BF16),
                        pltpu.VMEM((NKB_PAD, D_BKV), BF16), pltpu.VMEM((NKB_PAD, D_BKV), BF16)],
        compiler_params=_params(1),
        name="attn_sample",
    )(qa, kan, van, cache_ak, cache_av, cnt_a, qb, kbn, vbn, cache_bk, cache_bv, mask_b, sink_col)


def _attn_out_prompt_kernel(o1_ref, l1_ref, o4_ref, l4_ref, o16_ref, l16_ref, ob_ref, y_ref, w_ref, g_ref, b_ref,
                            out_ref, slab_ref, oa_ref):
    tm = y_ref.shape[0]
    n_chunks = D_A // LANES
    for i, (dil, src) in enumerate(((4, o4_ref), (4, l4_ref), (16, o16_ref), (16, l16_ref))):
        for r in range(dil):
            for c in range(n_chunks):
                slab_ref[i * n_chunks + c, pl.ds(r, tm // dil, stride=dil), :] = src[r, :, c * LANES:(c + 1) * LANES]
    for c in range(n_chunks):
        sl = slice(c * LANES, (c + 1) * LANES)
        l1, l4, l16 = l1_ref[:, sl], slab_ref[n_chunks + c], slab_ref[3 * n_chunks + c]
        m = jnp.maximum(jnp.maximum(l1, l4), l16)
        e1, e4, e16 = jnp.exp(l1 - m), jnp.exp(l4 - m), jnp.exp(l16 - m)
        oa = (e1 * o1_ref[:, sl] + e4 * slab_ref[c] + e16 * slab_ref[2 * n_chunks + c]) * (1.0 / (e1 + e4 + e16))
        oa_ref[:, sl] = oa.astype(BF16)
    mix = _dot(oa_ref[...], w_ref[0:D_A, :]) + _dot(ob_ref[...].astype(BF16), w_ref[D_A:, :])
    out_ref[...] = _layer_norm(DN_ALPHA * y_ref[...] + mix, g_ref[...], b_ref[...])


def _attn_out_prompt(pats, ob, y, w, g, b, seq):
    n = y.shape[0]
    tm = ROW_TILE
    tps = seq // tm
    half = pl.BlockSpec((tm, D_A), lambda i: (i, 0))
    full = pl.BlockSpec((tm, D_MODEL), lambda i: (i, 0))
    planes = lambda dil: pl.BlockSpec((None, dil, tm // dil, D_A), lambda i: (i // tps, 0, i % tps, 0))
    (o1, l1), (o4, l4), (o16, l16) = pats
    return pl.pallas_call(
        _attn_out_prompt_kernel,
        grid=(n // tm,),
        in_specs=[half, half, planes(4), planes(4), planes(16), planes(16), half, full,
                  _resident((D_MODEL, D_MODEL)), _resident((1, D_MODEL)), _resident((1, D_MODEL))],
        out_specs=full,
        out_shape=jax.ShapeDtypeStruct((n, D_MODEL), F32),
        scratch_shapes=[pltpu.VMEM((4 * D_A // LANES, tm, LANES), F32), pltpu.VMEM((tm, D_A), BF16)],
        compiler_params=_params(1),
        name="attn_out_prompt",
    )(o1, l1, o4, l4, o16, l16, ob, y, w, g, b)


def _mix_out_kernel(o_ref, y_ref, w_ref, g_ref, b_ref, out_ref):
    mix = _dot(o_ref[...].astype(BF16), w_ref[...])
    out_ref[...] = _layer_norm(DN_ALPHA * y_ref[...] + mix, g_ref[...], b_ref[...])


def _attn_out_sample(o, y, w, g, b):
    n = y.shape[0]
    tm = min(ROW_TILE, n)
    full = pl.BlockSpec((tm, D_MODEL), lambda i: (i, 0))
    return pl.pallas_call(
        _mix_out_kernel,
        grid=(n // tm,),
        in_specs=[full, full, _resident((D_MODEL, D_MODEL)), _resident((1, D_MODEL)), _resident((1, D_MODEL))],
        out_specs=full,
        out_shape=jax.ShapeDtypeStruct((n, D_MODEL), F32),
        compiler_params=_params(1),
        name="attn_out_sample",
    )(o, y, w, g, b)


def _ssm_discretize(lam_re, lam_im, log_dt, b_re, b_im):
    dt = jnp.exp(log_dt.astype(F32))[:, None]
    lr, li = lam_re.astype(F32), lam_im.astype(F32)
    mag = jnp.exp(lr * dt)
    ab_re, ab_im = mag * jnp.cos(li * dt), mag * jnp.sin(li * dt)
    nr, ni = ab_re - 1.0, ab_im
    den = lr * lr + li * li
    fr, fi = (nr * lr + ni * li) / den, (ni * lr - nr * li) / den
    bb_re = fr[..., None] * b_re - fi[..., None] * b_im
    bb_im = fr[..., None] * b_im + fi[..., None] * b_re
    return ab_re, ab_im, bb_re, bb_im


def _ssm_matrices(bb_re, bb_im, c_re, c_im):
    gpc = LANES // SSM_GROUP
    eye = jnp.eye(gpc, dtype=F32)

    def in_blocks(bb):
        a = bb.reshape(N_LCHUNK, gpc, SSM_STATE, SSM_GROUP)
        return jnp.einsum("jgpn,gh->jgnhp", a, eye).reshape(N_LCHUNK, LANES, ST_CHUNK)

    def out_blocks(cc):
        a = cc.reshape(N_LCHUNK, gpc, SSM_GROUP, SSM_STATE)
        return jnp.einsum("jgnp,gh->jgphn", a, eye).reshape(N_LCHUNK, ST_CHUNK, LANES)

    bmat = jnp.concatenate([in_blocks(bb_re), in_blocks(bb_im)], -1)
    cmat = jnp.concatenate([out_blocks(c_re), -out_blocks(c_im)], 1)
    return bmat.astype(BF16), cmat.astype(BF16)


def _matmul_kernel(x_ref, w_ref, o_ref):
    o_ref[...] = _dot(x_ref[...].astype(BF16), w_ref[...])


def _ssm_in_prompt(y, w):
    return pl.pallas_call(
        _matmul_kernel,
        grid=(BATCH, N_SEG),
        in_specs=[pl.BlockSpec((SEG_LEN, D_MODEL), lambda b, s: (b * N_SEG + s, 0)), _resident((D_MODEL, D_MODEL))],
        out_specs=pl.BlockSpec((None, SEG_LEN, D_MODEL), lambda b, s: (b, 0, s)),
        out_shape=jax.ShapeDtypeStruct((BATCH, SEG_LEN, N_SEG * D_MODEL), F32),
        compiler_params=_params(2),
        name="ssm_in_prompt",
    )(y, w).reshape(BATCH, SEQ, D_MODEL)


def _ssm_in_sample(y, w):
    return pl.pallas_call(
        _matmul_kernel,
        grid=(DEC_SEQ,),
        in_specs=[pl.BlockSpec((DEC_BATCH, D_MODEL), lambda l: (0, l)), _resident((D_MODEL, D_MODEL))],
        out_specs=pl.BlockSpec((None, DEC_BATCH, D_MODEL), lambda l: (l, 0, 0)),
        out_shape=jax.ShapeDtypeStruct((DEC_SEQ, DEC_BATCH, D_MODEL), F32),
        compiler_params=_params(1),
        name="ssm_in_sample",
    )(y.reshape(DEC_BATCH, DEC_SEQ * D_MODEL), w)


def _cmul(ar, ai, br, bi):
    return ar * br - ai * bi, ar * bi + ai * br


def _scan_prompt_kernel(u_ref, bmat_ref, cmat_ref, are_ref, aim_ref, d_ref, h0r_ref, h0i_ref,
                        y_ref, hnr_ref, hni_ref, bu_s, hs_s):
    n_tiles = SEQ // SCAN_TILE
    steps = SCAN_TILE // N_SEG
    for i in range(n_tiles):
        rows = slice(i * SCAN_TILE, (i + 1) * SCAN_TILE)
        bu_s[rows, :] = _dot(u_ref[rows, :].astype(BF16), bmat_ref[...])
    a_re1, a_im1 = are_ref[...], aim_ref[...]
    a_re = jnp.broadcast_to(a_re1, (N_SEG, ST_CHUNK))
    a_im = jnp.broadcast_to(a_im1, (N_SEG, ST_CHUNK))

    def advance(row, hr, hi):
        bur = bu_s[pl.ds(row, N_SEG), 0:ST_CHUNK]
        bui = bu_s[pl.ds(row, N_SEG), ST_CHUNK:2 * ST_CHUNK]
        return a_re * hr - a_im * hi + bur, a_re * hi + a_im * hr + bui

    def pass1(k, carry):
        return advance(pl.multiple_of(k * N_SEG, N_SEG), *carry)

    zero = jnp.zeros((N_SEG, ST_CHUNK), F32)
    er, ei = lax.fori_loop(0, SEG_LEN, pass1, (zero, zero), unroll=8)

    pr, pi = a_re1, a_im1
    for _ in range(int(math.log2(SEG_LEN))):
        pr, pi = _cmul(pr, pi, pr, pi)
    hr, hi = h0r_ref[...], h0i_ref[...]
    starts_r, starts_i = [], []
    for s in range(N_SEG):
        starts_r.append(hr)
        starts_i.append(hi)
        gr, gi = _cmul(pr, pi, hr, hi)
        hr, hi = gr + er[s:s + 1], gi + ei[s:s + 1]
    hnr_ref[...] = hr
    hni_ref[...] = hi
    init = (jnp.concatenate(starts_r, 0), jnp.concatenate(starts_i, 0))

    def tile(i, carry):
        base = pl.multiple_of(i * SCAN_TILE, SCAN_TILE)

        def pass2(k, c):
            off = pl.multiple_of(k * N_SEG, N_SEG)
            nr, ni = advance(base + off, *c)
            hs_s[pl.ds(off, N_SEG), 0:ST_CHUNK] = nr
            hs_s[pl.ds(off, N_SEG), ST_CHUNK:2 * ST_CHUNK] = ni
            return nr, ni

        carry = lax.fori_loop(0, steps, pass2, carry, unroll=8)
        y = _dot(hs_s[...].astype(BF16), cmat_ref[...]) + d_ref[...] * u_ref[pl.ds(base, SCAN_TILE), :]
        y_ref[pl.ds(base, SCAN_TILE), :] = y
        return carry

    lax.fori_loop(0, n_tiles, tile, init)


def _scan_prompt(u, bmat, cmat, a_re, a_im, d_skip, h0r, h0i):
    chunk = lambda rows: pl.BlockSpec((None, rows, LANES), lambda b, j: (b, 0, j))
    per_j = lambda r, c: pl.BlockSpec((None, r, c), lambda b, j: (j, 0, 0))
    state = pl.BlockSpec((None, 1, ST_CHUNK), lambda b, j: (b, 0, j))
    return pl.pallas_call(
        _scan_prompt_kernel,
        grid=(BATCH, N_LCHUNK),
        in_specs=[chunk(SEQ), per_j(LANES, 2 * ST_CHUNK), per_j(2 * ST_CHUNK, LANES), per_j(1, ST_CHUNK),
                  per_j(1, ST_CHUNK), per_j(1, LANES), state, state],
        out_specs=[chunk(SEQ), state, state],
        out_shape=[jax.ShapeDtypeStruct((BATCH, SEQ, D_MODEL), F32),
                   jax.ShapeDtypeStruct((BATCH, 1, N_STATE), F32), jax.ShapeDtypeStruct((BATCH, 1, N_STATE), F32)],
        scratch_shapes=[pltpu.VMEM((SEQ, 2 * ST_CHUNK), F32), pltpu.VMEM((SCAN_TILE, 2 * ST_CHUNK), F32)],
        compiler_params=_params(2),
        name="ssm_scan_prompt",
    )(u, bmat, cmat, a_re, a_im, d_skip, h0r, h0i)


def _scan_sample_kernel(u_ref, bmat_ref, cmat_ref, are_ref, aim_ref, d_ref, h0r_ref, h0i_ref,
                        y_ref, hnr_ref, hni_ref):
    a_re, a_im = are_ref[...], aim_ref[...]
    hr, hi = h0r_ref[...], h0i_ref[...]
    for l in range(DEC_SEQ):
        u = u_ref[l]
        bu = _dot(u.astype(BF16), bmat_ref[...])
        gr, gi = _cmul(a_re, a_im, hr, hi)
        hr, hi = gr + bu[:, :ST_CHUNK], gi + bu[:, ST_CHUNK:]
        h = jnp.concatenate([hr, hi], axis=1).astype(BF16)
        y_ref[l] = _dot(h, cmat_ref[...]) + d_ref[...] * u
    hnr_ref[...] = hr
    hni_ref[...] = hi


def _scan_sample(u, bmat, cmat, a_re, a_im, d_skip, h0r, h0i):
    chunk = pl.BlockSpec((DEC_SEQ, DEC_BATCH, LANES), lambda j: (0, 0, j))
    per_j = lambda r, c: pl.BlockSpec((None, r, c), lambda j: (j, 0, 0))
    state = pl.BlockSpec((DEC_BATCH, ST_CHUNK), lambda j: (0, j))
    return pl.pallas_call(
        _scan_sample_kernel,
        grid=(N_LCHUNK,),
        in_specs=[chunk, per_j(LANES, 2 * ST_CHUNK), per_j(2 * ST_CHUNK, LANES), per_j(1, ST_CHUNK),
                  per_j(1, ST_CHUNK), per_j(1, LANES), state, state],
        out_specs=[chunk, state, state],
        out_shape=[jax.ShapeDtypeStruct((DEC_SEQ, DEC_BATCH, D_MODEL), F32),
                   jax.ShapeDtypeStruct((DEC_BATCH, N_STATE), F32), jax.ShapeDtypeStruct((DEC_BATCH, N_STATE), F32)],
        compiler_params=_params(1),
        name="ssm_scan_sample",
    )(u, bmat, cmat, a_re, a_im, d_skip, h0r, h0i)


def _ssm_out_kernel(s_ref, y_ref, wglu_ref, bglu_ref, wout_ref, g_ref, b_ref, out_ref):
    z = jax.nn.gelu(s_ref[...])
    gate = jax.nn.sigmoid(_dot(z.astype(BF16), wglu_ref[...]) + bglu_ref[...])
    mix = _dot((z * gate).astype(BF16), wout_ref[...])
    out_ref[...] = _layer_norm(DN_ALPHA * y_ref[...] + mix, g_ref[...], b_ref[...])


def _ssm_out(s, y, s_spec, y_spec, grid, w_glu, b_glu, w_out, g, b, name):
    return pl.pallas_call(
        _ssm_out_kernel,
        grid=grid,
        in_specs=[s_spec, y_spec, _resident((D_MODEL, D_MODEL)), _resident((1, D_MODEL)),
                  _resident((D_MODEL, D_MODEL)), _resident((1, D_MODEL)), _resident((1, D_MODEL))],
        out_specs=y_spec,
        out_shape=jax.ShapeDtypeStruct(y.shape, F32),
        compiler_params=_params(len(grid)),
        name=name,
    )(s, y, w_glu, b_glu, w_out, g, b)


def _attn_prompt(yp, w_in, sinks, w_out, g, b):
    cos_p, sin_p = _rope_tables(jnp.arange(SEQ))
    tiles_per_seq = SEQ // ROW_TILE
    qa, ka, va, qb, kb, vb, kaf, vaf, kbf, vbf, *dilated = _attn_proj(
        yp, w_in, cos_p, sin_p, lambda i: i % tiles_per_seq, BF16, dilated_seq=(BATCH, SEQ))
    seq3 = lambda a: a.reshape(BATCH, SEQ, a.shape[-1])
    plane1 = lambda a: a.reshape(BATCH, 1, SEQ, D_A)
    o1, l1 = _band_a(plane1(qa), plane1(ka), plane1(va))
    pats = [(o1.reshape(BATCH * SEQ, D_A), l1.reshape(BATCH * SEQ, D_A))]
    for i in range(len(DILATIONS) - 1):
        pats.append(_band_a(*dilated[3 * i:3 * i + 3]))
    ob = _band_b(seq3(qb), seq3(kb), seq3(vb), sinks)
    yp = _attn_out_prompt(pats, ob, yp, w_out, g, b, SEQ)
    heads = lambda a, n, nh: a.reshape(BATCH, SEQ, nh, HEAD_DIM)[None, :, SEQ - n:]
    prompt_cache = (heads(kaf, WIN_A, N_HEADS_A), heads(vaf, WIN_A, N_HEADS_A),
                    heads(kbf, WIN_B, N_KV_B), heads(vbf, WIN_B, N_KV_B))
    return yp, prompt_cache


def _attn_sample_path(ys, cache_ak, cache_av, cache_bk, cache_bv, w_in, sinks, w_out, g, b):
    cos_s, sin_s = _rope_tables(PAST_LEN + jnp.arange(DEC_SEQ))
    reps = DEC_BATCH * DEC_SEQ // DEC_SEQ
    cos_s, sin_s = jnp.tile(cos_s, (reps, 1)), jnp.tile(sin_s, (reps, 1))
    qa, ka, va, qb, kb, vb, _, _, _, _ = _attn_proj(ys, w_in, cos_s, sin_s, lambda i: i, F32)
    o = _attn_sample(qa, ka, va, cache_ak.reshape(DEC_BATCH, WIN_A, D_A), cache_av.reshape(DEC_BATCH, WIN_A, D_A),
                     qb, kb, vb, cache_bk.reshape(DEC_BATCH, WIN_B, D_BKV), cache_bv.reshape(DEC_BATCH, WIN_B, D_BKV),
                     sinks)
    ys = _attn_out_sample(o, ys, w_out, g, b)
    new = lambda a, nh: a.reshape(1, DEC_BATCH, DEC_SEQ, nh, HEAD_DIM)
    sample_cache = (new(ka, N_HEADS_A), new(va, N_HEADS_A), new(kb, N_KV_B), new(vb, N_KV_B))
    return ys, sample_cache


def _attn_layer(yp, ys, cache_ak, cache_av, cache_bk, cache_bv, w_in, sinks, w_out, g, b):
    w_in = w_in.astype(BF16)
    w_out = w_out.astype(BF16)
    yp, prompt_cache = _attn_prompt(yp, w_in, sinks, w_out, g, b)
    ys, sample_cache = _attn_sample_path(ys, cache_ak, cache_av, cache_bk, cache_bv, w_in, sinks, w_out, g, b)
    return yp, ys, prompt_cache, sample_cache


def _ssm_layer(yp, ys, state_re, state_im, w_in, lam_re, lam_im, log_dt, b_re, b_im, c_re, c_im, d_skip,
               w_glu, b_glu, w_out, g, b):
    w_in, w_glu, w_out = w_in.astype(BF16), w_glu.astype(BF16), w_out.astype(BF16)
    ab_re, ab_im, bb_re, bb_im = _ssm_discretize(lam_re, lam_im, log_dt, b_re, b_im)
    bmat, cmat = _ssm_matrices(bb_re, bb_im, c_re, c_im)
    a_re = ab_re.reshape(N_LCHUNK, 1, ST_CHUNK)
    a_im = ab_im.reshape(N_LCHUNK, 1, ST_CHUNK)
    d3 = d_skip.astype(F32).reshape(N_LCHUNK, 1, LANES)
    b_glu = b_glu.reshape(1, D_MODEL)
    mats = (bmat, cmat, a_re, a_im, d3)
    yp, prompt_state = _ssm_prompt(yp, w_in, mats, w_glu, b_glu, w_out, g, b)
    ys, sample_state = _ssm_sample(ys, state_re, state_im, w_in, mats, w_glu, b_glu, w_out, g, b)
    return yp, ys, prompt_state, sample_state


def _ssm_prompt(yp, w_in, mats, w_glu, b_glu, w_out, g, b):
    bmat, cmat, a_re, a_im, d3 = mats
    up = _ssm_in_prompt(yp, w_in)
    zero = jnp.zeros((BATCH, 1, N_STATE), F32)
    sp, pr, pi = _scan_prompt(up, bmat, cmat, a_re, a_im, d3, zero, zero)
    yp = _ssm_out(sp.reshape(BATCH, SEG_LEN, N_SEG * D_MODEL), yp,
                  pl.BlockSpec((None, SEG_LEN, D_MODEL), lambda bb, s: (bb, 0, s)),
                  pl.BlockSpec((SEG_LEN, D_MODEL), lambda bb, s: (bb * N_SEG + s, 0)),
                  (BATCH, N_SEG), w_glu, b_glu, w_out, g, b, "ssm_out_prompt")
    prompt_state = (pr.reshape(1, BATCH, N_SSM_GROUPS, SSM_STATE), pi.reshape(1, BATCH, N_SSM_GROUPS, SSM_STATE))
    return yp, prompt_state


def _ssm_sample(ys, state_re, state_im, w_in, mats, w_glu, b_glu, w_out, g, b):
    bmat, cmat, a_re, a_im, d3 = mats
    us = _ssm_in_sample(ys, w_in)
    ss, sr, si = _scan_sample(us, bmat, cmat, a_re, a_im, d3,
                              state_re.reshape(DEC_BATCH, N_STATE), state_im.reshape(DEC_BATCH, N_STATE))
    ys = _ssm_out(ss, ys.reshape(DEC_BATCH, DEC_SEQ * D_MODEL),
                  pl.BlockSpec((None, DEC_BATCH, D_MODEL), lambda l: (l, 0, 0)),
                  pl.BlockSpec((DEC_BATCH, D_MODEL), lambda l: (0, l)),
                  (DEC_SEQ,), w_glu, b_glu, w_out, g, b, "ssm_out_sample").reshape(DEC_BATCH * DEC_SEQ, D_MODEL)
    sample_state = (sr.reshape(1, DEC_BATCH, N_SSM_GROUPS, SSM_STATE), si.reshape(1, DEC_BATCH, N_SSM_GROUPS, SSM_STATE))
    return ys, sample_state


def kernel(x_prompt, x_sample, cache_a_k, cache_a_v, cache_b_k, cache_b_v, state_c_re, state_c_im, ln_g, ln_b, ffn_w_gate, ffn_w_up, ffn_w_down, attn_w_in, attn_sinks, attn_w_out, ssm_w_in, ssm_lambda_re, ssm_lambda_im, ssm_log_dt, ssm_b_re, ssm_b_im, ssm_c_re, ssm_c_im, ssm_d, ssm_w_glu, ssm_b_glu, ssm_w_out):
    yp = x_prompt.reshape(BATCH * SEQ, D_MODEL)
    ys = x_sample.reshape(DEC_BATCH * DEC_SEQ, D_MODEL)
    ln = lambda l, k: (ln_g[l, k].reshape(1, D_MODEL), ln_b[l, k].reshape(1, D_MODEL))

    wg, wu, wd = ffn_w_gate.astype(BF16), ffn_w_up.astype(BF16), ffn_w_down.astype(BF16)

    def ffn_pair(yp, ys, l, k, ln_idx):
        g, b = ln(l, ln_idx)
        return _ffn(yp, wg, wu, wd, g, b, l, k), _ffn(ys, wg, wu, wd, g, b, l, k)

    yp, ys = ffn_pair(yp, ys, 0, 0, 0)
    yp, ys, p_cache, s_cache = _attn_layer(yp, ys, cache_a_k[0], cache_a_v[0], cache_b_k[0], cache_b_v[0],
                                           attn_w_in[0], attn_sinks[0], attn_w_out[0], *ln(0, 1))
    yp, ys = ffn_pair(yp, ys, 0, 1, 2)
    yp, ys = ffn_pair(yp, ys, 1, 0, 0)
    yp, ys, p_state, s_state = _ssm_layer(yp, ys, state_c_re[0], state_c_im[0], ssm_w_in[0], ssm_lambda_re[0],
                                          ssm_lambda_im[0], ssm_log_dt[0], ssm_b_re[0], ssm_b_im[0], ssm_c_re[0],
                                          ssm_c_im[0], ssm_d[0], ssm_w_glu[0], ssm_b_glu[0], ssm_w_out[0], *ln(1, 1))
    yp, ys = ffn_pair(yp, ys, 1, 1, 2)
    return (yp.reshape(BATCH, SEQ, D_MODEL), ys.reshape(DEC_BATCH, DEC_SEQ, D_MODEL),
            *p_cache, *p_state, *s_cache, *s_state)
```

```python
import functools
import math

import jax
import jax.numpy as jnp
from jax import lax
from jax.experimental import pallas as pl
from jax.experimental.pallas import tpu as pltpu

F32 = jnp.float32
BF16 = jnp.bfloat16

D_MODEL = 1024
BATCH = 4
SEQ = 4096
DEPTH = 2
DEC_BATCH = 128
DEC_SEQ = 8
PAST_LEN = 16384
HEAD_DIM = 64
N_HEADS_A = 8
DILATIONS = (1, 4, 16)
WIN_A = 2048
N_HEADS_B = 8
N_KV_B = 2
WIN_B = 128
ROPE_THETA = 10000.0
D_A = N_HEADS_A * HEAD_DIM
D_BQ = N_HEADS_B * HEAD_DIM
D_BKV = N_KV_B * HEAD_DIM
D_IN_ATTN = 3 * D_A + D_BQ + 2 * D_BKV
SSM_GROUP = 16
N_SSM_GROUPS = D_MODEL // SSM_GROUP
SSM_STATE = 64
N_STATE = N_SSM_GROUPS * SSM_STATE
D_FF = 2816
DN_ALPHA = (2 * DEPTH) ** 0.25
FFN_RES = 0.5
LN_EPS = 1e-5
ATTN_SCALE = HEAD_DIM ** -0.5

LANES = 128
SUBLANES = 8
MXU_N = 256
VMEM_LIMIT = 56 * 1024 * 1024

ROW_TILE = 512
FF_CHUNK = MXU_N
TQ = 128
N_SEG = SUBLANES
SEG_LEN = SEQ // N_SEG
N_LCHUNK = D_MODEL // LANES
ST_CHUNK = N_STATE // N_LCHUNK
SCAN_TILE = 512
NK_PAD = WIN_A + LANES
NKB_PAD = 2 * WIN_B

NEG_INF = float("-inf")


def _params(n_axes, vmem=VMEM_LIMIT):
    return pltpu.CompilerParams(dimension_semantics=("arbitrary",) * n_axes, vmem_limit_bytes=vmem)


def _resident(shape):
    return pl.BlockSpec(shape, lambda *_: (0,) * len(shape), pipeline_mode=pl.Buffered(1))


def _layer_norm(x, g, b):
    mu = jnp.mean(x, -1, keepdims=True)
    xc = x - mu
    var = jnp.mean(xc * xc, -1, keepdims=True)
    return xc * lax.rsqrt(var + LN_EPS) * g + b


def _dot(a, b):
    return jnp.dot(a, b, preferred_element_type=F32)


def _dot_nt(a, b):
    return lax.dot_general(a, b, (((1,), (1,)), ((), ())), preferred_element_type=F32)


def _ffn_kernel(x_ref, wg_ref, wu_ref, wd_ref, g_ref, b_ref, o_ref, h_ref):
    x = x_ref[...]
    xb = x.astype(BF16)
    for c in range(D_FF // FF_CHUNK):
        sl = slice(c * FF_CHUNK, (c + 1) * FF_CHUNK)
        gate = _dot(xb, wg_ref[:, sl])
        up = _dot(xb, wu_ref[:, sl])
        h_ref[:, sl] = (gate * jax.nn.sigmoid(gate) * up).astype(BF16)
    y = DN_ALPHA * x + FFN_RES * _dot(h_ref[...], wd_ref[...])
    o_ref[...] = _layer_norm(y, g_ref[...], b_ref[...])


def _ffn(x, wg, wu, wd, g, b, layer=0, which=0):
    n = x.shape[0]
    tm = min(ROW_TILE, n)
    row = pl.BlockSpec((tm, D_MODEL), lambda i: (i, 0))
    if wg.ndim == 4:
        pick = lambda r, c: pl.BlockSpec((None, None, r, c), lambda i: (layer, which, 0, 0),
                                         pipeline_mode=pl.Buffered(1))
    else:
        pick = lambda r, c: _resident((r, c))
    return pl.pallas_call(
        _ffn_kernel,
        grid=(n // tm,),
        in_specs=[row, pick(D_MODEL, D_FF), pick(D_MODEL, D_FF), pick(D_FF, D_MODEL),
                  _resident((1, D_MODEL)), _resident((1, D_MODEL))],
        out_specs=row,
        out_shape=jax.ShapeDtypeStruct((n, D_MODEL), F32),
        scratch_shapes=[pltpu.VMEM((tm, D_FF), BF16)],
        compiler_params=_params(1),
        name="ffn",
    )(x, wg, wu, wd, g, b)


def _rope_tables(pos):
    half = HEAD_DIM // 2
    inv_freq = ROPE_THETA ** (-jnp.arange(half, dtype=F32) / half)
    ang = pos.astype(F32)[:, None] * inv_freq[None, :]
    cos, sin = jnp.cos(ang), jnp.sin(ang)
    cos_t = jnp.concatenate([cos, cos, cos, cos], -1)
    sin_t = jnp.concatenate([-sin, sin, -sin, sin], -1)
    return cos_t, sin_t


def _attn_proj_kernel(x_ref, w_ref, cos_ref, sin_ref, qa_ref, ka_ref, va_ref, qb_ref, kb_ref, vb_ref,
                      kaf_ref, vaf_ref, kbf_ref, vbf_ref, *dilated):
    xb = x_ref[...].astype(BF16)
    slab_ref = dilated[-1] if dilated else None
    n_chunks = D_A // LANES

    def keep(tensor, c, val):
        if slab_ref is not None:
            slab_ref[tensor * n_chunks + c] = val
    cos = cos_ref[...]
    sin = sin_ref[...]
    lane = lax.broadcasted_iota(jnp.int32, cos.shape, 1)
    first_half = (lane & (HEAD_DIM // 2)) == 0

    def rope(z):
        rot = jnp.where(first_half, pltpu.roll(z, LANES - HEAD_DIM // 2, 1), pltpu.roll(z, HEAD_DIM // 2, 1))
        return z * cos + rot * sin

    def project(col0, ncols):
        return _dot(xb, w_ref[:, col0:col0 + ncols])

    def rope_chunks(z):
        return [rope(z[:, c * LANES:(c + 1) * LANES]) for c in range(z.shape[1] // LANES)]

    col = 0
    for c, r in enumerate(rope_chunks(project(col, D_A))):
        r = r * ATTN_SCALE
        qa_ref[:, c * LANES:(c + 1) * LANES] = r.astype(qa_ref.dtype)
        keep(0, c, r)
    col += D_A
    for c, r in enumerate(rope_chunks(project(col, D_A))):
        ka_ref[:, c * LANES:(c + 1) * LANES] = r.astype(ka_ref.dtype)
        kaf_ref[:, c * LANES:(c + 1) * LANES] = r
        keep(1, c, r)
    col += D_A
    z = project(col, D_A)
    va_ref[...] = z.astype(va_ref.dtype)
    vaf_ref[...] = z
    for c in range(n_chunks):
        keep(2, c, z[:, c * LANES:(c + 1) * LANES])
    col += D_A
    if dilated:
        tm = x_ref.shape[0]
        for i, dil in enumerate(DILATIONS[1:]):
            for tensor in range(3):
                out_ref = dilated[3 * i + tensor]
                for r in range(dil):
                    for c in range(n_chunks):
                        rows = slab_ref[tensor * n_chunks + c, pl.ds(r, tm // dil, stride=dil), :]
                        out_ref[r, :, c * LANES:(c + 1) * LANES] = rows.astype(out_ref.dtype)
    for c, r in enumerate(rope_chunks(project(col, D_BQ))):
        qb_ref[:, c * LANES:(c + 1) * LANES] = (r * ATTN_SCALE).astype(qb_ref.dtype)
    col += D_BQ
    z = project(col, 2 * D_BKV)
    r = rope(z[:, :D_BKV])
    kb_ref[...] = r.astype(kb_ref.dtype)
    kbf_ref[...] = r
    vb_ref[...] = z[:, D_BKV:].astype(vb_ref.dtype)
    vbf_ref[...] = z[:, D_BKV:]


def _attn_proj(x, w, cos_t, sin_t, table_block, act_dtype, dilated_seq=None):
    n = x.shape[0]
    tm = min(ROW_TILE, n)

    def row(width):
        return pl.BlockSpec((tm, width), lambda i: (i, 0))

    tab = pl.BlockSpec((tm, LANES), lambda i: (table_block(i), 0))
    widths = (D_A, D_A, D_A, D_BQ, D_BKV, D_BKV)
    out_shape = [jax.ShapeDtypeStruct((n, wd), act_dtype) for wd in widths]
    out_shape += [jax.ShapeDtypeStruct((n, wd), F32) for wd in (D_A, D_A, D_BKV, D_BKV)]
    out_specs = [row(wd) for wd in widths] + [row(D_A), row(D_A), row(D_BKV), row(D_BKV)]
    scratch = []
    if dilated_seq is not None:
        bsz, seq = dilated_seq
        tps = seq // tm
        for dil in DILATIONS[1:]:
            out_shape += [jax.ShapeDtypeStruct((bsz, dil, seq // dil, D_A), BF16)] * 3
            out_specs += [pl.BlockSpec((None, dil, tm // dil, D_A), lambda i: (i // tps, 0, i % tps, 0))] * 3
        scratch = [pltpu.VMEM((3 * D_A // LANES, tm, LANES), F32)]
    return pl.pallas_call(
        _attn_proj_kernel,
        grid=(n // tm,),
        in_specs=[row(D_MODEL), _resident((D_MODEL, D_IN_ATTN)), tab, tab],
        out_specs=out_specs,
        out_shape=out_shape,
        scratch_shapes=scratch,
        compiler_params=_params(1),
        name="attn_proj",
    )(x, w, cos_t, sin_t)


def _lane_lo(shape):
    return lax.broadcasted_iota(jnp.int32, shape, 1) < HEAD_DIM


def _half_masks_bf16():
    lo = jnp.where(_lane_lo((1, LANES)), 1.0, 0.0).astype(BF16)
    return lo, 1 - lo


def _band_masks(n_heads, t, sub, prev_strict):
    row = lax.broadcasted_iota(jnp.int32, (n_heads * TQ, TQ), 0) & (TQ - 1)
    col = lax.broadcasted_iota(jnp.int32, (n_heads * TQ, TQ), 1)
    mask_c = col <= row
    shift = jnp.where(t > 0, 0, TQ) if sub == 0 else 0
    mask_p = (col > row + shift) if prev_strict else (col >= row + shift)
    return mask_c, mask_p


def _sub_tile_kv(sub, sl, kc_ref, kp_ref, vc_ref, vp_ref):
    if sub == 0:
        return kc_ref[0:TQ, sl], kp_ref[:, sl], vc_ref[0:TQ, sl], vp_ref[:, sl]
    return kc_ref[TQ:2 * TQ, sl], kc_ref[0:TQ, sl], vc_ref[TQ:2 * TQ, sl], vc_ref[0:TQ, sl]


def _band_softmax(qs, kc, kp, vc, vp, mask_c, mask_p, sink=None):
    s_c = jnp.where(mask_c, _dot_nt(qs, kc), NEG_INF)
    s_p = jnp.where(mask_p, _dot_nt(qs, kp), NEG_INF)
    m = jnp.max(jnp.maximum(s_c, s_p), -1, keepdims=True)
    if sink is not None:
        m = jnp.maximum(m, sink)
    p_c = jnp.exp(s_c - m)
    p_p = jnp.exp(s_p - m)
    den = jnp.sum(p_c + p_p, -1, keepdims=True)
    if sink is not None:
        den = den + jnp.exp(sink - m)
    acc = _dot(p_c.astype(BF16), vc) + _dot(p_p.astype(BF16), vp)
    return acc * (1.0 / den), m, den


def _band_a_kernel(q_ref, kc_ref, kp_ref, vc_ref, vp_ref, o_ref, lse_ref):
    t = pl.program_id(2)
    lo = _lane_lo((TQ, LANES))
    lo_bf, hi_bf = _half_masks_bf16()
    for sub in range(2):
        rows = slice(sub * TQ, (sub + 1) * TQ)
        mask_c, mask_p = _band_masks(2, t, sub, prev_strict=False)
        for c in range(D_A // LANES):
            sl = slice(c * LANES, (c + 1) * LANES)
            kc, kp, vc, vp = _sub_tile_kv(sub, sl, kc_ref, kp_ref, vc_ref, vp_ref)
            q2 = q_ref[rows, sl]
            qs = jnp.concatenate([q2 * lo_bf, q2 * hi_bf], axis=0)
            out, m, den = _band_softmax(qs, kc, kp, vc, vp, mask_c, mask_p)
            lse = jnp.broadcast_to(m + jnp.log(den), (2 * TQ, LANES))
            o_ref[rows, sl] = jnp.where(lo, out[0:TQ], out[TQ:])
            lse_ref[rows, sl] = jnp.where(lo, lse[0:TQ], lse[TQ:])


def _band_a(q, k, v):
    bsz, dil, sub, _ = q.shape
    cur = pl.BlockSpec((None, None, 2 * TQ, D_A), lambda b, r, t: (b, r, t, 0))
    prev = pl.BlockSpec((None, None, TQ, D_A), lambda b, r, t: (b, r, jnp.maximum(2 * t - 1, 0), 0))
    return pl.pallas_call(
        _band_a_kernel,
        grid=(bsz, dil, sub // (2 * TQ)),
        in_specs=[cur, cur, prev, cur, prev],
        out_specs=[cur, cur],
        out_shape=[jax.ShapeDtypeStruct((bsz, dil, sub, D_A), F32)] * 2,
        compiler_params=_params(3),
        name=f"band_a_d{dil}",
    )(q, k, k, v, v)


def _swap_halves(x):
    return jnp.concatenate([x[:, HEAD_DIM:], x[:, :HEAD_DIM]], axis=1)


def _band_b_kernel(sink_ref, q_ref, kc_ref, kp_ref, vc_ref, vp_ref, o_ref):
    t = pl.program_id(1)
    group = N_HEADS_B // N_KV_B
    lo = _lane_lo((TQ, LANES))
    lo_bf, hi_bf = _half_masks_bf16()
    for sub in range(2):
        rows = slice(sub * TQ, (sub + 1) * TQ)
        mask_c, mask_p = _band_masks(group, t, sub, prev_strict=True)
        kv = _sub_tile_kv(sub, slice(0, D_BKV), kc_ref, kp_ref, vc_ref, vp_ref)
        kv_swapped = tuple(_swap_halves(a) for a in kv)
        for g in range(N_KV_B):
            own, other = (lo_bf, hi_bf) if g == 0 else (hi_bf, lo_bf)
            kc, kp, vc, vp = (a * own + a_sw * other for a, a_sw in zip(kv, kv_swapped))
            heads = range(g * group, (g + 1) * group)
            qs = jnp.concatenate(
                [q_ref[rows, (h // 2) * LANES:(h // 2 + 1) * LANES] * (lo_bf if h % 2 == 0 else hi_bf) for h in heads],
                axis=0)
            sink = jnp.concatenate([jnp.full((TQ, 1), sink_ref[h], F32) for h in heads], axis=0)
            out, _, _ = _band_softmax(qs, kc, kp, vc, vp, mask_c, mask_p, sink)
            for i in range(group // 2):
                c = g * (group // 2) + i
                even, odd = out[2 * i * TQ:(2 * i + 1) * TQ], out[(2 * i + 1) * TQ:(2 * i + 2) * TQ]
                o_ref[rows, c * LANES:(c + 1) * LANES] = jnp.where(lo, even, odd)


def _band_b(q, k, v, sinks):
    bsz, seq, _ = q.shape
    qs = pl.BlockSpec((None, 2 * TQ, D_BQ), lambda b, t: (b, t, 0))
    cur = pl.BlockSpec((None, 2 * TQ, D_BKV), lambda b, t: (b, t, 0))
    prev = pl.BlockSpec((None, TQ, D_BKV), lambda b, t: (b, jnp.maximum(2 * t - 1, 0), 0))
    o = pl.pallas_call(
        _band_b_kernel,
        grid=(bsz, seq // (2 * TQ)),
        in_specs=[pl.BlockSpec(memory_space=pltpu.SMEM), qs, cur, prev, cur, prev],
        out_specs=qs,
        out_shape=jax.ShapeDtypeStruct((bsz, seq, D_BQ), F32),
        compiler_params=_params(2),
        name="band_b",
    )(sinks, q, k, k, v, v)
    return o.reshape(bsz * seq, D_BQ)


CACHE_MOD = 16
CACHE_GROUPS = WIN_A // CACHE_MOD
TAIL_GROUPS = 32


def _pattern_count(dist):
    cnt = jnp.zeros(dist.shape, F32)
    for dil in DILATIONS:
        cnt = cnt + ((dist >= 0) & (dist <= 128 * dil) & (dist % dil == 0)).astype(F32)
    return cnt


def _sample_tables():
    i = jnp.arange(DEC_SEQ)
    half = CACHE_MOD // 2
    pos_a = (CACHE_MOD * jnp.arange(CACHE_GROUPS)[:, None] + jnp.arange(half)[None, :]).reshape(-1)
    pos_b = (CACHE_MOD * (CACHE_GROUPS - TAIL_GROUPS + jnp.arange(TAIL_GROUPS))[:, None]
             + half + jnp.arange(half)[None, :]).reshape(-1)

    def cached(pos):
        cnt = _pattern_count(WIN_A + i[:, None] - pos[None, :])
        same_head = jnp.eye(N_HEADS_A, dtype=F32)
        full = cnt[None, :, :, None] * same_head[:, None, None, :]
        return full.reshape(N_HEADS_A * DEC_SEQ, -1)

    j = jnp.arange(LANES)
    cnt_n = jnp.where(j[None, :] < DEC_SEQ, _pattern_count(i[:, None] - j[None, :]), 0.0)
    cnt_n = jnp.tile(cnt_n, (N_HEADS_A, 1))
    jb = jnp.arange(NKB_PAD)[None, :]
    dist_b = WIN_B + i[:, None] - jb
    ok_b = (dist_b >= 0) & (dist_b < WIN_B) & (jb < WIN_B + DEC_SEQ)
    mask_b = jnp.tile(ok_b.astype(F32), (N_HEADS_B, 1))
    return cached(pos_a), cached(pos_b), cnt_n, mask_b


def _attn_sample_kernel(qa_ref, kan_ref, van_ref, ka_a_ref, ka_b_ref, va_a_ref, va_b_ref,
                        cnt_a_ref, cnt_b_ref, cnt_n_ref,
                        qb_ref, kbn_ref, vbn_ref, kbc_ref, vbc_ref, maskb_ref, sinkcol_ref,
                        o_ref, kb_s, vb_s):
    q = qa_ref[...]
    rows = N_HEADS_A * DEC_SEQ
    q_heads = jnp.concatenate([q[:, h * HEAD_DIM:(h + 1) * HEAD_DIM] for h in range(N_HEADS_A)], axis=0)
    q_heads = q_heads.astype(BF16)
    q_rep = jnp.concatenate([q] * N_HEADS_A, axis=0)
    row_head = jnp.right_shift(lax.broadcasted_iota(jnp.int32, (rows, D_A), 0), int(math.log2(DEC_SEQ)))
    lane_head = jnp.right_shift(lax.broadcasted_iota(jnp.int32, (rows, D_A), 1), int(math.log2(HEAD_DIM)))
    own = row_head == lane_head
    q_bd = jnp.where(own, q_rep, 0.0).astype(BF16)

    def rows_of(ref):
        x = ref[...]
        return x.reshape(x.shape[0] * x.shape[1] * x.shape[2], HEAD_DIM).astype(BF16)

    pad = jnp.zeros((LANES - DEC_SEQ, D_A), F32)
    kn = jnp.concatenate([kan_ref[...], pad], 0).astype(BF16)
    vn = jnp.concatenate([van_ref[...], pad], 0).astype(BF16)
    cnts = (cnt_a_ref[...], cnt_b_ref[...], cnt_n_ref[...])
    raw = (_dot_nt(q_heads, rows_of(ka_a_ref)), _dot_nt(q_heads, rows_of(ka_b_ref)), _dot_nt(q_bd, kn))
    scores = [jnp.where(c > 0.0, s, NEG_INF) for c, s in zip(cnts, raw)]
    m = functools.reduce(jnp.maximum, [jnp.max(s, -1, keepdims=True) for s in scores])
    probs = [jnp.exp(s - m) * c for c, s in zip(cnts, scores)]
    den = functools.reduce(jnp.add, [jnp.sum(p, -1, keepdims=True) for p in probs])
    inv = 1.0 / den
    p_a, p_b, p_n = (p.astype(BF16) for p in probs)
    out_c = (_dot(p_a, rows_of(va_a_ref)) + _dot(p_b, rows_of(va_b_ref))) * inv
    out_n = jnp.where(own, _dot(p_n, vn) * inv, 0.0)
    oa = jnp.concatenate([out_c[h * DEC_SEQ:(h + 1) * DEC_SEQ] for h in range(N_HEADS_A)], axis=1)
    for h in range(N_HEADS_A):
        oa = oa + out_n[h * DEC_SEQ:(h + 1) * DEC_SEQ]
    o_ref[:, 0:D_A] = oa

    n_pad_b = NKB_PAD - WIN_B - DEC_SEQ
    pad_b = jnp.zeros((n_pad_b, D_BKV), F32)
    kb_s[...] = jnp.concatenate([kbc_ref[...], kbn_ref[...], pad_b], 0).astype(BF16)
    vb_s[...] = jnp.concatenate([vbc_ref[...], vbn_ref[...], pad_b], 0).astype(BF16)
    qb = qb_ref[...]
    lo8 = _lane_lo((DEC_SEQ, LANES))
    group = N_HEADS_B // N_KV_B
    pieces = []
    for h in range(N_HEADS_B):
        chunk = qb[:, (h // 2) * LANES:(h // 2 + 1) * LANES]
        g = h // group
        if h % 2 != g:
            chunk = pltpu.roll(chunk, HEAD_DIM, 1)
        pieces.append(jnp.where(lo8 if g == 0 else jnp.logical_not(lo8), chunk, 0.0))
    qb_bd = jnp.concatenate(pieces, axis=0).astype(BF16)
    mask_b = maskb_ref[...]
    sb = jnp.where(mask_b > 0.0, _dot_nt(qb_bd, kb_s[...]), NEG_INF)
    sink = sinkcol_ref[...][:, 0:1]
    mb = jnp.maximum(jnp.max(sb, -1, keepdims=True), sink)
    pb = jnp.exp(sb - mb) * mask_b
    den_b = jnp.sum(pb, -1, keepdims=True) + jnp.exp(sink - mb)
    ob_full = _dot(pb.astype(BF16), vb_s[...]) * (1.0 / den_b)
    for c in range(D_BQ // LANES):
        halves = []
        for half in range(2):
            h = 2 * c + half
            piece = ob_full[h * DEC_SEQ:(h + 1) * DEC_SEQ]
            if half != h // group:
                piece = pltpu.roll(piece, HEAD_DIM, 1)
            halves.append(piece)
        o_ref[:, D_A + c * LANES:D_A + (c + 1) * LANES] = jnp.where(lo8, halves[0], halves[1])


def _attn_sample(qa, kan, van, cache_ak, cache_av, qb, kbn, vbn, cache_bk, cache_bv, sinks):
    cnt_a, cnt_b, cnt_n, mask_b = _sample_tables()
    sink_col = jnp.broadcast_to(jnp.repeat(sinks.astype(F32), DEC_SEQ)[:, None], (N_HEADS_B * DEC_SEQ, LANES))
    new = lambda width: pl.BlockSpec((DEC_SEQ, width), lambda b: (b, 0))
    cache = lambda rows, width: pl.BlockSpec((None, rows, width), lambda b: (b, 0, 0))
    half = CACHE_MOD // 2
    grouped = lambda a: a.reshape(DEC_BATCH, CACHE_GROUPS, CACHE_MOD, N_HEADS_A, HEAD_DIM)
    part_a = pl.BlockSpec((None, CACHE_GROUPS, half, N_HEADS_A, HEAD_DIM), lambda b: (b, 0, 0, 0, 0))
    part_b = pl.BlockSpec((None, TAIL_GROUPS, half, N_HEADS_A, HEAD_DIM),
                          lambda b: (b, CACHE_GROUPS // TAIL_GROUPS - 1, 1, 0, 0))
    rows = N_HEADS_A * DEC_SEQ
    return pl.pallas_call(
        _attn_sample_kernel,
        grid=(DEC_BATCH,),
        in_specs=[new(D_A), new(D_A), new(D_A), part_a, part_b, part_a, part_b,
                  _resident(cnt_a.shape), _resident(cnt_b.shape), _resident(cnt_n.shape),
                  new(D_BQ), new(D_BKV), new(D_BKV), cache(WIN_B, D_BKV), cache(WIN_B, D_BKV),
                  _resident((rows, NKB_PAD)), _resident((rows, LANES))],
        out_specs=pl.BlockSpec((DEC_SEQ, D_A + D_BQ), lambda b: (b, 0)),
        out_shape=jax.ShapeDtypeStruct((DEC_BATCH * DEC_SEQ, D_A + D_BQ), F32),
        scratch_shapes=[pltpu.VMEM((NKB_PAD, D_BKV), BF16), pltpu.VMEM((NKB_PAD, D_BKV), BF16)],
        compiler_params=_params(1),
        name="attn_sample",
    )(qa, kan, van, grouped(cache_ak), grouped(cache_ak), grouped(cache_av), grouped(cache_av),
      cnt_a, cnt_b, cnt_n, qb, kbn, vbn, cache_bk, cache_bv, mask_b, sink_col)


def _attn_out_prompt_kernel(o1_ref, l1_ref, o4_ref, l4_ref, o16_ref, l16_ref, ob_ref, y_ref, w_ref, g_ref, b_ref,
                            out_ref, slab_ref, oa_ref):
    tm = y_ref.shape[0]
    n_chunks = D_A // LANES
    for i, (dil, src) in enumerate(((4, o4_ref), (4, l4_ref), (16, o16_ref), (16, l16_ref))):
        for r in range(dil):
            for c in range(n_chunks):
                slab_ref[i * n_chunks + c, pl.ds(r, tm // dil, stride=dil), :] = src[r, :, c * LANES:(c + 1) * LANES]
    for c in range(n_chunks):
        sl = slice(c * LANES, (c + 1) * LANES)
        l1, l4, l16 = l1_ref[:, sl], slab_ref[n_chunks + c], slab_ref[3 * n_chunks + c]
        m = jnp.maximum(jnp.maximum(l1, l4), l16)
        e1, e4, e16 = jnp.exp(l1 - m), jnp.exp(l4 - m), jnp.exp(l16 - m)
        oa = (e1 * o1_ref[:, sl] + e4 * slab_ref[c] + e16 * slab_ref[2 * n_chunks + c]) * (1.0 / (e1 + e4 + e16))
        oa_ref[:, sl] = oa.astype(BF16)
    mix = _dot(oa_ref[...], w_ref[0:D_A, :]) + _dot(ob_ref[...].astype(BF16), w_ref[D_A:, :])
    out_ref[...] = _layer_norm(DN_ALPHA * y_ref[...] + mix, g_ref[...], b_ref[...])


def _attn_out_prompt(pats, ob, y, w, g, b, seq):
    n = y.shape[0]
    tm = ROW_TILE
    tps = seq // tm
    half = pl.BlockSpec((tm, D_A), lambda i: (i, 0))
    full = pl.BlockSpec((tm, D_MODEL), lambda i: (i, 0))
    planes = lambda dil: pl.BlockSpec((None, dil, tm // dil, D_A), lambda i: (i // tps, 0, i % tps, 0))
    (o1, l1), (o4, l4), (o16, l16) = pats
    return pl.pallas_call(
        _attn_out_prompt_kernel,
        grid=(n // tm,),
        in_specs=[half, half, planes(4), planes(4), planes(16), planes(16), half, full,
                  _resident((D_MODEL, D_MODEL)), _resident((1, D_MODEL)), _resident((1, D_MODEL))],
        out_specs=full,
        out_shape=jax.ShapeDtypeStruct((n, D_MODEL), F32),
        scratch_shapes=[pltpu.VMEM((4 * D_A // LANES, tm, LANES), F32), pltpu.VMEM((tm, D_A), BF16)],
        compiler_params=_params(1),
        name="attn_out_prompt",
    )(o1, l1, o4, l4, o16, l16, ob, y, w, g, b)


def _mix_out_kernel(o_ref, y_ref, w_ref, g_ref, b_ref, out_ref):
    mix = _dot(o_ref[...].astype(BF16), w_ref[...])
    out_ref[...] = _layer_norm(DN_ALPHA * y_ref[...] + mix, g_ref[...], b_ref[...])


def _attn_out_sample(o, y, w, g, b):
    n = y.shape[0]
    tm = min(ROW_TILE, n)
    full = pl.BlockSpec((tm, D_MODEL), lambda i: (i, 0))
    return pl.pallas_call(
        _mix_out_kernel,
        grid=(n // tm,),
        in_specs=[full, full, _resident((D_MODEL, D_MODEL)), _resident((1, D_MODEL)), _resident((1, D_MODEL))],
        out_specs=full,
        out_shape=jax.ShapeDtypeStruct((n, D_MODEL), F32),
        compiler_params=_params(1),
        name="attn_out_sample",
    )(o, y, w, g, b)


def _ssm_discretize(lam_re, lam_im, log_dt, b_re, b_im):
    dt = jnp.exp(log_dt.astype(F32))[:, None]
    lr, li = lam_re.astype(F32), lam_im.astype(F32)
    mag = jnp.exp(lr * dt)
    ab_re, ab_im = mag * jnp.cos(li * dt), mag * jnp.sin(li * dt)
    nr, ni = ab_re - 1.0, ab_im
    den = lr * lr + li * li
    fr, fi = (nr * lr + ni * li) / den, (ni * lr - nr * li) / den
    bb_re = fr[..., None] * b_re - fi[..., None] * b_im
    bb_im = fr[..., None] * b_im + fi[..., None] * b_re
    return ab_re, ab_im, bb_re, bb_im


def _ssm_matrices(bb_re, bb_im, c_re, c_im):
    gpc = LANES // SSM_GROUP
    eye = jnp.eye(gpc, dtype=F32)

    def in_blocks(bb):
        a = bb.reshape(N_LCHUNK, gpc, SSM_STATE, SSM_GROUP)
        return jnp.einsum("jgpn,gh->jgnhp", a, eye).reshape(N_LCHUNK, LANES, ST_CHUNK)

    def out_blocks(cc):
        a = cc.reshape(N_LCHUNK, gpc, SSM_GROUP, SSM_STATE)
        return jnp.einsum("jgnp,gh->jgphn", a, eye).reshape(N_LCHUNK, ST_CHUNK, LANES)

    bmat = jnp.concatenate([in_blocks(bb_re), in_blocks(bb_im)], -1)
    cmat = jnp.concatenate([out_blocks(c_re), -out_blocks(c_im)], 1)
    return bmat.astype(BF16), cmat.astype(BF16)


SEG_TILE = ROW_TILE // N_SEG


def _ssm_in_prompt_kernel(x_ref, w_ref, o_ref):
    x = x_ref[...].reshape(N_SEG * SEG_TILE, D_MODEL)
    u = _dot(x.astype(BF16), w_ref[...])
    for s in range(N_SEG):
        for c in range(N_LCHUNK):
            o_ref[c, pl.ds(s, SEG_TILE, stride=N_SEG), :] = u[s * SEG_TILE:(s + 1) * SEG_TILE, c * LANES:(c + 1) * LANES]


def _ssm_in_prompt(y, w):
    return pl.pallas_call(
        _ssm_in_prompt_kernel,
        grid=(BATCH, SEG_LEN // SEG_TILE),
        in_specs=[pl.BlockSpec((None, N_SEG, SEG_TILE, D_MODEL), lambda b, t: (b, 0, t, 0)),
                  _resident((D_MODEL, D_MODEL))],
        out_specs=pl.BlockSpec((None, N_LCHUNK, ROW_TILE, LANES), lambda b, t: (b, 0, t, 0)),
        out_shape=jax.ShapeDtypeStruct((BATCH, N_LCHUNK, SEQ, LANES), F32),
        compiler_params=_params(2),
        name="ssm_in_prompt",
    )(y.reshape(BATCH, N_SEG, SEG_LEN, D_MODEL), w)


def _ssm_in_sample_kernel(x_ref, w_ref, o_ref, slab_ref):
    u = _dot(x_ref[...].astype(BF16), w_ref[...])
    for c in range(N_LCHUNK):
        slab_ref[c] = u[:, c * LANES:(c + 1) * LANES]
    for l in range(DEC_SEQ):
        for c in range(N_LCHUNK):
            o_ref[c, l * DEC_BATCH:(l + 1) * DEC_BATCH, :] = slab_ref[c, pl.ds(l, DEC_BATCH, stride=DEC_SEQ), :]


def _ssm_in_sample(y, w):
    n = DEC_BATCH * DEC_SEQ
    return pl.pallas_call(
        _ssm_in_sample_kernel,
        grid=(1,),
        in_specs=[_resident((n, D_MODEL)), _resident((D_MODEL, D_MODEL))],
        out_specs=pl.BlockSpec((N_LCHUNK, n, LANES), lambda i: (0, 0, 0)),
        out_shape=jax.ShapeDtypeStruct((N_LCHUNK, n, LANES), F32),
        scratch_shapes=[pltpu.VMEM((N_LCHUNK, n, LANES), F32)],
        compiler_params=_params(1),
        name="ssm_in_sample",
    )(y, w)


def _cmul(ar, ai, br, bi):
    return ar * br - ai * bi, ar * bi + ai * br


def _scan_prompt_kernel(u_ref, bmat_ref, cmat_ref, are_ref, aim_ref, d_ref, h0r_ref, h0i_ref,
                        y_ref, hnr_ref, hni_ref, bu_s, hs_s):
    n_tiles = SEQ // SCAN_TILE
    steps = SCAN_TILE // N_SEG
    for i in range(n_tiles):
        rows = slice(i * SCAN_TILE, (i + 1) * SCAN_TILE)
        bu_s[rows, :] = _dot(u_ref[rows, :].astype(BF16), bmat_ref[...])
    a_re1, a_im1 = are_ref[...], aim_ref[...]
    a_re = jnp.broadcast_to(a_re1, (N_SEG, ST_CHUNK))
    a_im = jnp.broadcast_to(a_im1, (N_SEG, ST_CHUNK))

    def advance(row, hr, hi):
        bur = bu_s[pl.ds(row, N_SEG), 0:ST_CHUNK]
        bui = bu_s[pl.ds(row, N_SEG), ST_CHUNK:2 * ST_CHUNK]
        return a_re * hr - a_im * hi + bur, a_re * hi + a_im * hr + bui

    def pass1(k, carry):
        return advance(pl.multiple_of(k * N_SEG, N_SEG), *carry)

    zero = jnp.zeros((N_SEG, ST_CHUNK), F32)
    er, ei = lax.fori_loop(0, SEG_LEN, pass1, (zero, zero), unroll=8)

    pr, pi = a_re1, a_im1
    for _ in range(int(math.log2(SEG_LEN))):
        pr, pi = _cmul(pr, pi, pr, pi)
    hr, hi = h0r_ref[...], h0i_ref[...]
    starts_r, starts_i = [], []
    for s in range(N_SEG):
        starts_r.append(hr)
        starts_i.append(hi)
        gr, gi = _cmul(pr, pi, hr, hi)
        hr, hi = gr + er[s:s + 1], gi + ei[s:s + 1]
    hnr_ref[...] = hr
    hni_ref[...] = hi
    init = (jnp.concatenate(starts_r, 0), jnp.concatenate(starts_i, 0))

    def tile(i, carry):
        base = pl.multiple_of(i * SCAN_TILE, SCAN_TILE)

        def pass2(k, c):
            off = pl.multiple_of(k * N_SEG, N_SEG)
            nr, ni = advance(base + off, *c)
            hs_s[pl.ds(off, N_SEG), 0:ST_CHUNK] = nr
            hs_s[pl.ds(off, N_SEG), ST_CHUNK:2 * ST_CHUNK] = ni
            return nr, ni

        carry = lax.fori_loop(0, steps, pass2, carry, unroll=8)
        y = _dot(hs_s[...].astype(BF16), cmat_ref[...]) + d_ref[...] * u_ref[pl.ds(base, SCAN_TILE), :]
        y_ref[pl.ds(base, SCAN_TILE), :] = y
        return carry

    lax.fori_loop(0, n_tiles, tile, init)


def _scan_prompt(u, bmat, cmat, a_re, a_im, d_skip, h0r, h0i):
    chunk = lambda rows: pl.BlockSpec((None, None, rows, LANES), lambda b, j: (b, j, 0, 0))
    per_j = lambda r, c: pl.BlockSpec((None, r, c), lambda b, j: (j, 0, 0))
    state = pl.BlockSpec((None, 1, ST_CHUNK), lambda b, j: (b, 0, j))
    return pl.pallas_call(
        _scan_prompt_kernel,
        grid=(BATCH, N_LCHUNK),
        in_specs=[chunk(SEQ), per_j(LANES, 2 * ST_CHUNK), per_j(2 * ST_CHUNK, LANES), per_j(1, ST_CHUNK),
                  per_j(1, ST_CHUNK), per_j(1, LANES), state, state],
        out_specs=[chunk(SEQ), state, state],
        out_shape=[jax.ShapeDtypeStruct((BATCH, N_LCHUNK, SEQ, LANES), F32),
                   jax.ShapeDtypeStruct((BATCH, 1, N_STATE), F32), jax.ShapeDtypeStruct((BATCH, 1, N_STATE), F32)],
        scratch_shapes=[pltpu.VMEM((SEQ, 2 * ST_CHUNK), F32), pltpu.VMEM((SCAN_TILE, 2 * ST_CHUNK), F32)],
        compiler_params=_params(2),
        name="ssm_scan_prompt",
    )(u, bmat, cmat, a_re, a_im, d_skip, h0r, h0i)


def _scan_sample_kernel(u_ref, bmat_ref, cmat_ref, are_ref, aim_ref, d_ref, h0r_ref, h0i_ref,
                        y_ref, hnr_ref, hni_ref):
    a_re, a_im = are_ref[...], aim_ref[...]
    hr, hi = h0r_ref[...], h0i_ref[...]
    for l in range(DEC_SEQ):
        rows = slice(l * DEC_BATCH, (l + 1) * DEC_BATCH)
        u = u_ref[rows, :]
        bu = _dot(u.astype(BF16), bmat_ref[...])
        gr, gi = _cmul(a_re, a_im, hr, hi)
        hr, hi = gr + bu[:, :ST_CHUNK], gi + bu[:, ST_CHUNK:]
        h = jnp.concatenate([hr, hi], axis=1).astype(BF16)
        y_ref[rows, :] = _dot(h, cmat_ref[...]) + d_ref[...] * u
    hnr_ref[...] = hr
    hni_ref[...] = hi


def _scan_sample(u, bmat, cmat, a_re, a_im, d_skip, h0r, h0i):
    n = DEC_SEQ * DEC_BATCH
    chunk = pl.BlockSpec((None, n, LANES), lambda j: (j, 0, 0))
    per_j = lambda r, c: pl.BlockSpec((None, r, c), lambda j: (j, 0, 0))
    state = pl.BlockSpec((DEC_BATCH, ST_CHUNK), lambda j: (0, j))
    return pl.pallas_call(
        _scan_sample_kernel,
        grid=(N_LCHUNK,),
        in_specs=[chunk, per_j(LANES, 2 * ST_CHUNK), per_j(2 * ST_CHUNK, LANES), per_j(1, ST_CHUNK),
                  per_j(1, ST_CHUNK), per_j(1, LANES), state, state],
        out_specs=[chunk, state, state],
        out_shape=[jax.ShapeDtypeStruct((N_LCHUNK, n, LANES), F32),
                   jax.ShapeDtypeStruct((DEC_BATCH, N_STATE), F32), jax.ShapeDtypeStruct((DEC_BATCH, N_STATE), F32)],
        compiler_params=_params(1),
        name="ssm_scan_sample",
    )(u, bmat, cmat, a_re, a_im, d_skip, h0r, h0i)


def _ssm_out_kernel(sample, s_ref, y_ref, wglu_ref, bglu_ref, wout_ref, g_ref, b_ref, out_ref, z_ref):
    n_rows = z_ref.shape[1]
    if sample:
        parts = [(pl.ds(l, DEC_BATCH, stride=DEC_SEQ), slice(l * DEC_BATCH, (l + 1) * DEC_BATCH))
                 for l in range(DEC_SEQ)]
    else:
        parts = [(slice(s * SEG_TILE, (s + 1) * SEG_TILE), pl.ds(s, SEG_TILE, stride=N_SEG)) for s in range(N_SEG)]
    for c in range(N_LCHUNK):
        for tok_rows, slab_rows in parts:
            z_ref[c, tok_rows, :] = s_ref[c, slab_rows, :]
    z = jax.nn.gelu(jnp.concatenate([z_ref[c] for c in range(N_LCHUNK)], axis=1))
    gate = jax.nn.sigmoid(_dot(z.astype(BF16), wglu_ref[...]) + bglu_ref[...])
    mix = _dot((z * gate).astype(BF16), wout_ref[...])
    res = y_ref[...].reshape(n_rows, D_MODEL)
    out = _layer_norm(DN_ALPHA * res + mix, g_ref[...], b_ref[...])
    out_ref[...] = out.reshape(out_ref.shape)


def _ssm_out(s, y, s_spec, y_spec, grid, rows, sample, w_glu, b_glu, w_out, g, b, name):
    return pl.pallas_call(
        functools.partial(_ssm_out_kernel, sample),
        grid=grid,
        in_specs=[s_spec, y_spec, _resident((D_MODEL, D_MODEL)), _resident((1, D_MODEL)),
                  _resident((D_MODEL, D_MODEL)), _resident((1, D_MODEL)), _resident((1, D_MODEL))],
        out_specs=y_spec,
        out_shape=jax.ShapeDtypeStruct(y.shape, F32),
        scratch_shapes=[pltpu.VMEM((N_LCHUNK, rows, LANES), F32)],
        compiler_params=_params(len(grid)),
        name=name,
    )(s, y, w_glu, b_glu, w_out, g, b)


def _attn_prompt(yp, w_in, sinks, w_out, g, b):
    cos_p, sin_p = _rope_tables(jnp.arange(SEQ))
    tiles_per_seq = SEQ // ROW_TILE
    qa, ka, va, qb, kb, vb, kaf, vaf, kbf, vbf, *dilated = _attn_proj(
        yp, w_in, cos_p, sin_p, lambda i: i % tiles_per_seq, BF16, dilated_seq=(BATCH, SEQ))
    seq3 = lambda a: a.reshape(BATCH, SEQ, a.shape[-1])
    plane1 = lambda a: a.reshape(BATCH, 1, SEQ, D_A)
    o1, l1 = _band_a(plane1(qa), plane1(ka), plane1(va))
    pats = [(o1.reshape(BATCH * SEQ, D_A), l1.reshape(BATCH * SEQ, D_A))]
    for i in range(len(DILATIONS) - 1):
        pats.append(_band_a(*dilated[3 * i:3 * i + 3]))
    ob = _band_b(seq3(qb), seq3(kb), seq3(vb), sinks)
    yp = _attn_out_prompt(pats, ob, yp, w_out, g, b, SEQ)
    heads = lambda a, n, nh: a.reshape(BATCH, SEQ, nh, HEAD_DIM)[None, :, SEQ - n:]
    prompt_cache = (heads(kaf, WIN_A, N_HEADS_A), heads(vaf, WIN_A, N_HEADS_A),
                    heads(kbf, WIN_B, N_KV_B), heads(vbf, WIN_B, N_KV_B))
    return yp, prompt_cache


def _attn_sample_path(ys, cache_ak, cache_av, cache_bk, cache_bv, w_in, sinks, w_out, g, b):
    cos_s, sin_s = _rope_tables(PAST_LEN + jnp.arange(DEC_SEQ))
    reps = DEC_BATCH * DEC_SEQ // DEC_SEQ
    cos_s, sin_s = jnp.tile(cos_s, (reps, 1)), jnp.tile(sin_s, (reps, 1))
    qa, ka, va, qb, kb, vb, _, _, _, _ = _attn_proj(ys, w_in, cos_s, sin_s, lambda i: i, F32)
    o = _attn_sample(qa, ka, va, cache_ak, cache_av,
                     qb, kb, vb, cache_bk.reshape(DEC_BATCH, WIN_B, D_BKV), cache_bv.reshape(DEC_BATCH, WIN_B, D_BKV),
                     sinks)
    ys = _attn_out_sample(o, ys, w_out, g, b)
    new = lambda a, nh: a.reshape(1, DEC_BATCH, DEC_SEQ, nh, HEAD_DIM)
    sample_cache = (new(ka, N_HEADS_A), new(va, N_HEADS_A), new(kb, N_KV_B), new(vb, N_KV_B))
    return ys, sample_cache


def _attn_layer(yp, ys, cache_ak, cache_av, cache_bk, cache_bv, w_in, sinks, w_out, g, b):
    w_in = w_in.astype(BF16)
    w_out = w_out.astype(BF16)
    yp, prompt_cache = _attn_prompt(yp, w_in, sinks, w_out, g, b)
    ys, sample_cache = _attn_sample_path(ys, cache_ak, cache_av, cache_bk, cache_bv, w_in, sinks, w_out, g, b)
    return yp, ys, prompt_cache, sample_cache


def _ssm_layer(yp, ys, state_re, state_im, w_in, lam_re, lam_im, log_dt, b_re, b_im, c_re, c_im, d_skip,
               w_glu, b_glu, w_out, g, b):
    w_in, w_glu, w_out = w_in.astype(BF16), w_glu.astype(BF16), w_out.astype(BF16)
    ab_re, ab_im, bb_re, bb_im = _ssm_discretize(lam_re, lam_im, log_dt, b_re, b_im)
    bmat, cmat = _ssm_matrices(bb_re, bb_im, c_re, c_im)
    a_re = ab_re.reshape(N_LCHUNK, 1, ST_CHUNK)
    a_im = ab_im.reshape(N_LCHUNK, 1, ST_CHUNK)
    d3 = d_skip.astype(F32).reshape(N_LCHUNK, 1, LANES)
    b_glu = b_glu.reshape(1, D_MODEL)
    mats = (bmat, cmat, a_re, a_im, d3)
    yp, prompt_state = _ssm_prompt(yp, w_in, mats, w_glu, b_glu, w_out, g, b)
    ys, sample_state = _ssm_sample(ys, state_re, state_im, w_in, mats, w_glu, b_glu, w_out, g, b)
    return yp, ys, prompt_state, sample_state


def _ssm_prompt(yp, w_in, mats, w_glu, b_glu, w_out, g, b):
    bmat, cmat, a_re, a_im, d3 = mats
    up = _ssm_in_prompt(yp, w_in)
    zero = jnp.zeros((BATCH, 1, N_STATE), F32)
    sp, pr, pi = _scan_prompt(up, bmat, cmat, a_re, a_im, d3, zero, zero)
    yp = _ssm_out(sp, yp.reshape(BATCH, N_SEG, SEG_LEN, D_MODEL),
                  pl.BlockSpec((None, N_LCHUNK, ROW_TILE, LANES), lambda bb, t: (bb, 0, t, 0)),
                  pl.BlockSpec((None, N_SEG, SEG_TILE, D_MODEL), lambda bb, t: (bb, 0, t, 0)),
                  (BATCH, SEG_LEN // SEG_TILE), ROW_TILE, False, w_glu, b_glu, w_out, g, b,
                  "ssm_out_prompt").reshape(BATCH * SEQ, D_MODEL)
    prompt_state = (pr.reshape(1, BATCH, N_SSM_GROUPS, SSM_STATE), pi.reshape(1, BATCH, N_SSM_GROUPS, SSM_STATE))
    return yp, prompt_state


def _ssm_sample(ys, state_re, state_im, w_in, mats, w_glu, b_glu, w_out, g, b):
    bmat, cmat, a_re, a_im, d3 = mats
    us = _ssm_in_sample(ys, w_in)
    ss, sr, si = _scan_sample(us, bmat, cmat, a_re, a_im, d3,
                              state_re.reshape(DEC_BATCH, N_STATE), state_im.reshape(DEC_BATCH, N_STATE))
    n = DEC_BATCH * DEC_SEQ
    ys = _ssm_out(ss, ys,
                  pl.BlockSpec((N_LCHUNK, n, LANES), lambda i: (0, 0, 0)),
                  pl.BlockSpec((n, D_MODEL), lambda i: (0, 0)),
                  (1,), n, True, w_glu, b_glu, w_out, g, b, "ssm_out_sample")
    sample_state = (sr.reshape(1, DEC_BATCH, N_SSM_GROUPS, SSM_STATE), si.reshape(1, DEC_BATCH, N_SSM_GROUPS, SSM_STATE))
    return ys, sample_state


def kernel(x_prompt, x_sample, cache_a_k, cache_a_v, cache_b_k, cache_b_v, state_c_re, state_c_im, ln_g, ln_b, ffn_w_gate, ffn_w_up, ffn_w_down, attn_w_in, attn_sinks, attn_w_out, ssm_w_in, ssm_lambda_re, ssm_lambda_im, ssm_log_dt, ssm_b_re, ssm_b_im, ssm_c_re, ssm_c_im, ssm_d, ssm_w_glu, ssm_b_glu, ssm_w_out):
    yp = x_prompt.reshape(BATCH * SEQ, D_MODEL)
    ys = x_sample.reshape(DEC_BATCH * DEC_SEQ, D_MODEL)
    ln = lambda l, k: (ln_g[l, k].reshape(1, D_MODEL), ln_b[l, k].reshape(1, D_MODEL))

    wg, wu, wd = ffn_w_gate.astype(BF16), ffn_w_up.astype(BF16), ffn_w_down.astype(BF16)

    def ffn_pair(yp, ys, l, k, ln_idx):
        g, b = ln(l, ln_idx)
        return _ffn(yp, wg, wu, wd, g, b, l, k), _ffn(ys, wg, wu, wd, g, b, l, k)

    yp, ys = ffn_pair(yp, ys, 0, 0, 0)
    yp, ys, p_cache, s_cache = _attn_layer(yp, ys, cache_a_k[0], cache_a_v[0], cache_b_k[0], cache_b_v[0],
                                           attn_w_in[0], attn_sinks[0], attn_w_out[0], *ln(0, 1))
    yp, ys = ffn_pair(yp, ys, 0, 1, 2)
    yp, ys = ffn_pair(yp, ys, 1, 0, 0)
    yp, ys, p_state, s_state = _ssm_layer(yp, ys, state_c_re[0], state_c_im[0], ssm_w_in[0], ssm_lambda_re[0],
                                          ssm_lambda_im[0], ssm_log_dt[0], ssm_b_re[0], ssm_b_im[0], ssm_c_re[0],
                                          ssm_c_im[0], ssm_d[0], ssm_w_glu[0], ssm_b_glu[0], ssm_w_out[0], *ln(1, 1))
    yp, ys = ffn_pair(yp, ys, 1, 1, 2)
    return (yp.reshape(BATCH, SEQ, D_MODEL), ys.reshape(DEC_BATCH, DEC_SEQ, D_MODEL),
            *p_cache, *p_state, *s_cache, *s_state)
```

```python
import functools
import math

import jax
import jax.numpy as jnp
from jax import lax
from jax.experimental import pallas as pl
from jax.experimental.pallas import tpu as pltpu

F32 = jnp.float32
BF16 = jnp.bfloat16

D_MODEL = 1024
BATCH = 4
SEQ = 4096
DEPTH = 2
DEC_BATCH = 128
DEC_SEQ = 8
PAST_LEN = 16384
HEAD_DIM = 64
N_HEADS_A = 8
DILATIONS = (1, 4, 16)
WIN_A = 2048
N_HEADS_B = 8
N_KV_B = 2
WIN_B = 128
ROPE_THETA = 10000.0
D_A = N_HEADS_A * HEAD_DIM
D_BQ = N_HEADS_B * HEAD_DIM
D_BKV = N_KV_B * HEAD_DIM
D_IN_ATTN = 3 * D_A + D_BQ + 2 * D_BKV
SSM_GROUP = 16
N_SSM_GROUPS = D_MODEL // SSM_GROUP
SSM_STATE = 64
N_STATE = N_SSM_GROUPS * SSM_STATE
D_FF = 2816
DN_ALPHA = (2 * DEPTH) ** 0.25
FFN_RES = 0.5
LN_EPS = 1e-5
ATTN_SCALE = HEAD_DIM ** -0.5

LANES = 128
SUBLANES = 8
MXU_N = 256
VMEM_LIMIT = 56 * 1024 * 1024

ROW_TILE = 512
FF_CHUNK = MXU_N
TQ = 128
N_SEG = SUBLANES
SEG_LEN = SEQ // N_SEG
N_LCHUNK = D_MODEL // LANES
ST_CHUNK = N_STATE // N_LCHUNK
SCAN_TILE = 512
NK_PAD = WIN_A + LANES
NKB_PAD = 2 * WIN_B

NEG_INF = float("-inf")


def _params(n_axes, vmem=VMEM_LIMIT):
    return pltpu.CompilerParams(dimension_semantics=("arbitrary",) * n_axes, vmem_limit_bytes=vmem)


def _resident(shape):
    return pl.BlockSpec(shape, lambda *_: (0,) * len(shape), pipeline_mode=pl.Buffered(1))


def _layer_norm(x, g, b):
    mu = jnp.mean(x, -1, keepdims=True)
    xc = x - mu
    var = jnp.mean(xc * xc, -1, keepdims=True)
    return xc * lax.rsqrt(var + LN_EPS) * g + b


def _dot(a, b):
    return jnp.dot(a, b, preferred_element_type=F32)


def _dot_nt(a, b):
    return lax.dot_general(a, b, (((1,), (1,)), ((), ())), preferred_element_type=F32)


def _ffn_kernel(x_ref, wg_ref, wu_ref, wd_ref, g_ref, b_ref, o_ref, h_ref):
    x = x_ref[...]
    xb = x.astype(BF16)
    for c in range(D_FF // FF_CHUNK):
        sl = slice(c * FF_CHUNK, (c + 1) * FF_CHUNK)
        gate = _dot(xb, wg_ref[:, sl])
        up = _dot(xb, wu_ref[:, sl])
        h_ref[:, sl] = (gate * jax.nn.sigmoid(gate) * up).astype(BF16)
    y = DN_ALPHA * x + FFN_RES * _dot(h_ref[...], wd_ref[...])
    o_ref[...] = _layer_norm(y, g_ref[...], b_ref[...])


def _ffn(x, wg, wu, wd, g, b, layer=0, which=0):
    n = x.shape[0]
    tm = min(ROW_TILE, n)
    row = pl.BlockSpec((tm, D_MODEL), lambda i: (i, 0))
    if wg.ndim == 4:
        pick = lambda r, c: pl.BlockSpec((None, None, r, c), lambda i: (layer, which, 0, 0),
                                         pipeline_mode=pl.Buffered(1))
    else:
        pick = lambda r, c: _resident((r, c))
    return pl.pallas_call(
        _ffn_kernel,
        grid=(n // tm,),
        in_specs=[row, pick(D_MODEL, D_FF), pick(D_MODEL, D_FF), pick(D_FF, D_MODEL),
                  _resident((1, D_MODEL)), _resident((1, D_MODEL))],
        out_specs=row,
        out_shape=jax.ShapeDtypeStruct((n, D_MODEL), F32),
        scratch_shapes=[pltpu.VMEM((tm, D_FF), BF16)],
        compiler_params=_params(1),
        name="ffn",
    )(x, wg, wu, wd, g, b)


def _rope_tables(pos):
    half = HEAD_DIM // 2
    inv_freq = ROPE_THETA ** (-jnp.arange(half, dtype=F32) / half)
    ang = pos.astype(F32)[:, None] * inv_freq[None, :]
    cos, sin = jnp.cos(ang), jnp.sin(ang)
    cos_t = jnp.concatenate([cos, cos, cos, cos], -1)
    sin_t = jnp.concatenate([-sin, sin, -sin, sin], -1)
    return cos_t, sin_t


def _attn_proj_kernel(tiles_per_seq, x_ref, w_ref, cos_ref, sin_ref, qa_ref, ka_ref, va_ref, qb_ref, kb_ref, vb_ref,
                      *extra):
    xb = x_ref[...].astype(BF16)
    slab_ref = extra[-1] if extra else None
    dilated = extra[:6]
    n_chunks = D_A // LANES
    tm = x_ref.shape[0]

    def keep(tensor, c, val):
        if slab_ref is not None:
            slab_ref[tensor * n_chunks + c] = val
    cos = cos_ref[...]
    sin = sin_ref[...]
    lane = lax.broadcasted_iota(jnp.int32, cos.shape, 1)
    first_half = (lane & (HEAD_DIM // 2)) == 0

    def rope(z):
        rot = jnp.where(first_half, pltpu.roll(z, LANES - HEAD_DIM // 2, 1), pltpu.roll(z, HEAD_DIM // 2, 1))
        return z * cos + rot * sin

    def project(col0, ncols):
        return _dot(xb, w_ref[:, col0:col0 + ncols])

    def rope_chunks(z):
        return [rope(z[:, c * LANES:(c + 1) * LANES]) for c in range(z.shape[1] // LANES)]

    col = 0
    for c, r in enumerate(rope_chunks(project(col, D_A))):
        r = r * ATTN_SCALE
        qa_ref[:, c * LANES:(c + 1) * LANES] = r.astype(qa_ref.dtype)
        keep(0, c, r)
    col += D_A
    for c, r in enumerate(rope_chunks(project(col, D_A))):
        ka_ref[:, c * LANES:(c + 1) * LANES] = r.astype(ka_ref.dtype)
        keep(1, c, r)
    col += D_A
    z = project(col, D_A)
    va_ref[...] = z.astype(va_ref.dtype)
    for c in range(n_chunks):
        keep(2, c, z[:, c * LANES:(c + 1) * LANES])
    col += D_A
    if extra:
        kat_ref, vat_ref, kbt_ref, vbt_ref = extra[6:10]
        tile_in_seq = pl.program_id(0) % tiles_per_seq

        @pl.when(tile_in_seq >= tiles_per_seq - WIN_A // tm)
        def _():
            for c in range(n_chunks):
                kat_ref[c * LANES:(c + 1) * LANES, :] = slab_ref[n_chunks + c].T
                vat_ref[c * LANES:(c + 1) * LANES, :] = slab_ref[2 * n_chunks + c].T

        for i, dil in enumerate(DILATIONS[1:]):
            for tensor in range(3):
                out_ref = dilated[3 * i + tensor]
                for r in range(dil):
                    for c in range(n_chunks):
                        rows = slab_ref[tensor * n_chunks + c, pl.ds(r, tm // dil, stride=dil), :]
                        out_ref[r, :, c * LANES:(c + 1) * LANES] = rows.astype(out_ref.dtype)
    for c, r in enumerate(rope_chunks(project(col, D_BQ))):
        qb_ref[:, c * LANES:(c + 1) * LANES] = (r * ATTN_SCALE).astype(qb_ref.dtype)
    col += D_BQ
    z = project(col, 2 * D_BKV)
    r = rope(z[:, :D_BKV])
    kb_ref[...] = r.astype(kb_ref.dtype)
    vb_ref[...] = z[:, D_BKV:].astype(vb_ref.dtype)
    if extra:
        @pl.when(tile_in_seq == tiles_per_seq - 1)
        def _():
            kbt_ref[...] = r[tm - WIN_B:, :].T
            vbt_ref[...] = z[tm - WIN_B:, D_BKV:].T


def _attn_proj(x, w, cos_t, sin_t, table_block, act_dtype, prompt_seq=None):
    n = x.shape[0]
    tm = min(ROW_TILE, n)

    def row(width):
        return pl.BlockSpec((tm, width), lambda i: (i, 0))

    tab = pl.BlockSpec((tm, LANES), lambda i: (table_block(i), 0))
    widths = (D_A, D_A, D_A, D_BQ, D_BKV, D_BKV)
    out_shape = [jax.ShapeDtypeStruct((n, wd), act_dtype) for wd in widths]
    out_specs = [row(wd) for wd in widths]
    scratch = []
    tps = None
    if prompt_seq is not None:
        bsz, seq = prompt_seq
        tps = seq // tm
        for dil in DILATIONS[1:]:
            out_shape += [jax.ShapeDtypeStruct((bsz, dil, seq // dil, D_A), BF16)] * 3
            out_specs += [pl.BlockSpec((None, dil, tm // dil, D_A), lambda i: (i // tps, 0, i % tps, 0))] * 3
        first_tail = tps - WIN_A // tm
        out_shape += [jax.ShapeDtypeStruct((bsz, D_A, WIN_A), F32)] * 2
        out_specs += [pl.BlockSpec((None, D_A, tm), lambda i: (i // tps, 0, jnp.maximum(i % tps - first_tail, 0)))] * 2
        out_shape += [jax.ShapeDtypeStruct((bsz, D_BKV, WIN_B), F32)] * 2
        out_specs += [pl.BlockSpec((None, D_BKV, WIN_B), lambda i: (i // tps, 0, 0))] * 2
        scratch = [pltpu.VMEM((3 * D_A // LANES, tm, LANES), F32)]
    return pl.pallas_call(
        functools.partial(_attn_proj_kernel, tps),
        grid=(n // tm,),
        in_specs=[row(D_MODEL), _resident((D_MODEL, D_IN_ATTN)), tab, tab],
        out_specs=out_specs,
        out_shape=out_shape,
        scratch_shapes=scratch,
        compiler_params=_params(1),
        name="attn_proj",
    )(x, w, cos_t, sin_t)


def _lane_lo(shape):
    return lax.broadcasted_iota(jnp.int32, shape, 1) < HEAD_DIM


def _half_masks_bf16():
    lo = jnp.where(_lane_lo((1, LANES)), 1.0, 0.0).astype(BF16)
    return lo, 1 - lo


def _band_masks(n_heads, t, sub, prev_strict):
    row = lax.broadcasted_iota(jnp.int32, (n_heads * TQ, TQ), 0) & (TQ - 1)
    col = lax.broadcasted_iota(jnp.int32, (n_heads * TQ, TQ), 1)
    mask_c = col <= row
    shift = jnp.where(t > 0, 0, TQ) if sub == 0 else 0
    mask_p = (col > row + shift) if prev_strict else (col >= row + shift)
    return mask_c, mask_p


def _sub_tile_kv(sub, sl, kc_ref, kp_ref, vc_ref, vp_ref):
    if sub == 0:
        return kc_ref[0:TQ, sl], kp_ref[:, sl], vc_ref[0:TQ, sl], vp_ref[:, sl]
    return kc_ref[TQ:2 * TQ, sl], kc_ref[0:TQ, sl], vc_ref[TQ:2 * TQ, sl], vc_ref[0:TQ, sl]


def _band_softmax(qs, kc, kp, vc, vp, mask_c, mask_p, sink=None):
    s_c = jnp.where(mask_c, _dot_nt(qs, kc), NEG_INF)
    s_p = jnp.where(mask_p, _dot_nt(qs, kp), NEG_INF)
    m = jnp.max(jnp.maximum(s_c, s_p), -1, keepdims=True)
    if sink is not None:
        m = jnp.maximum(m, sink)
    p_c = jnp.exp(s_c - m)
    p_p = jnp.exp(s_p - m)
    den = jnp.sum(p_c + p_p, -1, keepdims=True)
    if sink is not None:
        den = den + jnp.exp(sink - m)
    acc = _dot(p_c.astype(BF16), vc) + _dot(p_p.astype(BF16), vp)
    return acc * (1.0 / den), m, den


def _band_a_kernel(q_ref, kc_ref, kp_ref, vc_ref, vp_ref, o_ref, lse_ref):
    t = pl.program_id(2)
    lo = _lane_lo((TQ, LANES))
    lo_bf, hi_bf = _half_masks_bf16()
    for sub in range(2):
        rows = slice(sub * TQ, (sub + 1) * TQ)
        mask_c, mask_p = _band_masks(2, t, sub, prev_strict=False)
        for c in range(D_A // LANES):
            sl = slice(c * LANES, (c + 1) * LANES)
            kc, kp, vc, vp = _sub_tile_kv(sub, sl, kc_ref, kp_ref, vc_ref, vp_ref)
            q2 = q_ref[rows, sl]
            qs = jnp.concatenate([q2 * lo_bf, q2 * hi_bf], axis=0)
            out, m, den = _band_softmax(qs, kc, kp, vc, vp, mask_c, mask_p)
            lse = jnp.broadcast_to(m + jnp.log(den), (2 * TQ, LANES))
            o_ref[rows, sl] = jnp.where(lo, out[0:TQ], out[TQ:])
            lse_ref[rows, sl] = jnp.where(lo, lse[0:TQ], lse[TQ:])


def _band_a(q, k, v):
    bsz, dil, sub, _ = q.shape
    cur = pl.BlockSpec((None, None, 2 * TQ, D_A), lambda b, r, t: (b, r, t, 0))
    prev = pl.BlockSpec((None, None, TQ, D_A), lambda b, r, t: (b, r, jnp.maximum(2 * t - 1, 0), 0))
    return pl.pallas_call(
        _band_a_kernel,
        grid=(bsz, dil, sub // (2 * TQ)),
        in_specs=[cur, cur, prev, cur, prev],
        out_specs=[cur, cur],
        out_shape=[jax.ShapeDtypeStruct((bsz, dil, sub, D_A), F32)] * 2,
        compiler_params=_params(3),
        name=f"band_a_d{dil}",
    )(q, k, k, v, v)


def _swap_halves(x):
    return jnp.concatenate([x[:, HEAD_DIM:], x[:, :HEAD_DIM]], axis=1)


def _band_b_kernel(sink_ref, q_ref, kc_ref, kp_ref, vc_ref, vp_ref, o_ref):
    t = pl.program_id(1)
    group = N_HEADS_B // N_KV_B
    lo = _lane_lo((TQ, LANES))
    lo_bf, hi_bf = _half_masks_bf16()
    for sub in range(2):
        rows = slice(sub * TQ, (sub + 1) * TQ)
        mask_c, mask_p = _band_masks(group, t, sub, prev_strict=True)
        kv = _sub_tile_kv(sub, slice(0, D_BKV), kc_ref, kp_ref, vc_ref, vp_ref)
        kv_swapped = tuple(_swap_halves(a) for a in kv)
        for g in range(N_KV_B):
            own, other = (lo_bf, hi_bf) if g == 0 else (hi_bf, lo_bf)
            kc, kp, vc, vp = (a * own + a_sw * other for a, a_sw in zip(kv, kv_swapped))
            heads = range(g * group, (g + 1) * group)
            qs = jnp.concatenate(
                [q_ref[rows, (h // 2) * LANES:(h // 2 + 1) * LANES] * (lo_bf if h % 2 == 0 else hi_bf) for h in heads],
                axis=0)
            sink = jnp.concatenate([jnp.full((TQ, 1), sink_ref[h], F32) for h in heads], axis=0)
            out, _, _ = _band_softmax(qs, kc, kp, vc, vp, mask_c, mask_p, sink)
            for i in range(group // 2):
                c = g * (group // 2) + i
                even, odd = out[2 * i * TQ:(2 * i + 1) * TQ], out[(2 * i + 1) * TQ:(2 * i + 2) * TQ]
                o_ref[rows, c * LANES:(c + 1) * LANES] = jnp.where(lo, even, odd)


def _band_b(q, k, v, sinks):
    bsz, seq, _ = q.shape
    qs = pl.BlockSpec((None, 2 * TQ, D_BQ), lambda b, t: (b, t, 0))
    cur = pl.BlockSpec((None, 2 * TQ, D_BKV), lambda b, t: (b, t, 0))
    prev = pl.BlockSpec((None, TQ, D_BKV), lambda b, t: (b, jnp.maximum(2 * t - 1, 0), 0))
    o = pl.pallas_call(
        _band_b_kernel,
        grid=(bsz, seq // (2 * TQ)),
        in_specs=[pl.BlockSpec(memory_space=pltpu.SMEM), qs, cur, prev, cur, prev],
        out_specs=qs,
        out_shape=jax.ShapeDtypeStruct((bsz, seq, D_BQ), F32),
        compiler_params=_params(2),
        name="band_b",
    )(sinks, q, k, k, v, v)
    return o.reshape(bsz * seq, D_BQ)


def _pattern_count(dist):
    cnt = jnp.zeros(dist.shape, F32)
    for dil in DILATIONS:
        cnt = cnt + ((dist >= 0) & (dist <= 128 * dil) & (dist % dil == 0)).astype(F32)
    return cnt


def _sample_tables():
    i = jnp.arange(DEC_SEQ)
    cnt_c = _pattern_count(WIN_A + i[:, None] - jnp.arange(WIN_A)[None, :])
    j = jnp.arange(LANES)
    cnt_n = jnp.where(j[None, :] < DEC_SEQ, _pattern_count(i[:, None] - j[None, :]), 0.0)
    cnt_n = jnp.tile(cnt_n, (N_HEADS_A, 1))
    jb = jnp.arange(NKB_PAD)[None, :]
    dist_b = WIN_B + i[:, None] - jb
    ok_b = (dist_b >= 0) & (dist_b < WIN_B) & (jb < WIN_B + DEC_SEQ)
    mask_b = jnp.tile(ok_b.astype(F32), (N_HEADS_B, 1))
    return cnt_c, cnt_n, mask_b


def _attn_sample_kernel(qa_ref, kan_ref, van_ref, kt_ref, vt_ref, cnt_c_ref, cnt_n_ref,
                        qb_ref, kbn_ref, vbn_ref, kbc_ref, vbc_ref, maskb_ref, sinkcol_ref,
                        o_ref, kb_s, vb_s):
    q = qa_ref[...]
    rows = N_HEADS_A * DEC_SEQ
    q_rep = jnp.concatenate([q] * N_HEADS_A, axis=0)
    row_head = jnp.right_shift(lax.broadcasted_iota(jnp.int32, (rows, D_A), 0), int(math.log2(DEC_SEQ)))
    lane_head = jnp.right_shift(lax.broadcasted_iota(jnp.int32, (rows, D_A), 1), int(math.log2(HEAD_DIM)))
    own = row_head == lane_head
    q_bd = jnp.where(own, q_rep, 0.0).astype(BF16)
    pad = jnp.zeros((LANES - DEC_SEQ, D_A), F32)
    kn = jnp.concatenate([kan_ref[...], pad], 0).astype(BF16)
    vn = jnp.concatenate([van_ref[...], pad], 0).astype(BF16)
    cnt_c, cnt_n = cnt_c_ref[...], cnt_n_ref[...]
    s_new = jnp.where(cnt_n > 0.0, _dot_nt(q_bd, kn), NEG_INF)
    outs, p_new = [], []
    for h in range(N_HEADS_A):
        head_rows = slice(h * DEC_SEQ, (h + 1) * DEC_SEQ)
        q_h = q[:, h * HEAD_DIM:(h + 1) * HEAD_DIM].astype(BF16)
        s_c = jnp.where(cnt_c > 0.0, _dot(q_h, kt_ref[h].astype(BF16)), NEG_INF)
        s_n = s_new[head_rows]
        m = jnp.maximum(jnp.max(s_c, -1, keepdims=True), jnp.max(s_n, -1, keepdims=True))
        p_c = jnp.exp(s_c - m) * cnt_c
        p_n = jnp.exp(s_n - m) * cnt_n[head_rows]
        inv = 1.0 / (jnp.sum(p_c, -1, keepdims=True) + jnp.sum(p_n, -1, keepdims=True))
        outs.append(_dot_nt(p_c.astype(BF16), vt_ref[h].astype(BF16)) * inv)
        p_new.append(p_n * inv)
    out_n = jnp.where(own, _dot(jnp.concatenate(p_new, axis=0).astype(BF16), vn), 0.0)
    oa = jnp.concatenate(outs, axis=1)
    for h in range(N_HEADS_A):
        oa = oa + out_n[h * DEC_SEQ:(h + 1) * DEC_SEQ]
    o_ref[:, 0:D_A] = oa

    n_pad_b = NKB_PAD - WIN_B - DEC_SEQ
    pad_b = jnp.zeros((n_pad_b, D_BKV), F32)
    kb_s[...] = jnp.concatenate([kbc_ref[...], kbn_ref[...], pad_b], 0).astype(BF16)
    vb_s[...] = jnp.concatenate([vbc_ref[...], vbn_ref[...], pad_b], 0).astype(BF16)
    qb = qb_ref[...]
    lo8 = _lane_lo((DEC_SEQ, LANES))
    group = N_HEADS_B // N_KV_B
    pieces = []
    for h in range(N_HEADS_B):
        chunk = qb[:, (h // 2) * LANES:(h // 2 + 1) * LANES]
        g = h // group
        if h % 2 != g:
            chunk = pltpu.roll(chunk, HEAD_DIM, 1)
        pieces.append(jnp.where(lo8 if g == 0 else jnp.logical_not(lo8), chunk, 0.0))
    qb_bd = jnp.concatenate(pieces, axis=0).astype(BF16)
    mask_b = maskb_ref[...]
    sb = jnp.where(mask_b > 0.0, _dot_nt(qb_bd, kb_s[...]), NEG_INF)
    sink = sinkcol_ref[...][:, 0:1]
    mb = jnp.maximum(jnp.max(sb, -1, keepdims=True), sink)
    pb = jnp.exp(sb - mb) * mask_b
    den_b = jnp.sum(pb, -1, keepdims=True) + jnp.exp(sink - mb)
    ob_full = _dot(pb.astype(BF16), vb_s[...]) * (1.0 / den_b)
    for c in range(D_BQ // LANES):
        halves = []
        for half in range(2):
            h = 2 * c + half
            piece = ob_full[h * DEC_SEQ:(h + 1) * DEC_SEQ]
            if half != h // group:
                piece = pltpu.roll(piece, HEAD_DIM, 1)
            halves.append(piece)
        o_ref[:, D_A + c * LANES:D_A + (c + 1) * LANES] = jnp.where(lo8, halves[0], halves[1])


def _attn_sample(qa, kan, van, cache_ak, cache_av, qb, kbn, vbn, cache_bk, cache_bv, sinks):
    cnt_c, cnt_n, mask_b = _sample_tables()
    sink_col = jnp.broadcast_to(jnp.repeat(sinks.astype(F32), DEC_SEQ)[:, None], (N_HEADS_B * DEC_SEQ, LANES))
    new = lambda width: pl.BlockSpec((DEC_SEQ, width), lambda b: (b, 0))
    cache = lambda rows, width: pl.BlockSpec((None, rows, width), lambda b: (b, 0, 0))
    stored = pl.BlockSpec((None, N_HEADS_A, HEAD_DIM, WIN_A), lambda b: (b, 0, 0, 0))
    rows = N_HEADS_A * DEC_SEQ
    return pl.pallas_call(
        _attn_sample_kernel,
        grid=(DEC_BATCH,),
        in_specs=[new(D_A), new(D_A), new(D_A), stored, stored, _resident(cnt_c.shape), _resident(cnt_n.shape),
                  new(D_BQ), new(D_BKV), new(D_BKV), cache(WIN_B, D_BKV), cache(WIN_B, D_BKV),
                  _resident((rows, NKB_PAD)), _resident((rows, LANES))],
        out_specs=pl.BlockSpec((DEC_SEQ, D_A + D_BQ), lambda b: (b, 0)),
        out_shape=jax.ShapeDtypeStruct((DEC_BATCH * DEC_SEQ, D_A + D_BQ), F32),
        scratch_shapes=[pltpu.VMEM((NKB_PAD, D_BKV), BF16), pltpu.VMEM((NKB_PAD, D_BKV), BF16)],
        compiler_params=_params(1),
        name="attn_sample",
    )(qa, kan, van, cache_ak, cache_av, cnt_c, cnt_n, qb, kbn, vbn, cache_bk, cache_bv, mask_b, sink_col)


def _attn_out_prompt_kernel(o1_ref, l1_ref, o4_ref, l4_ref, o16_ref, l16_ref, ob_ref, y_ref, w_ref, g_ref, b_ref,
                            out_ref, slab_ref, oa_ref):
    tm = y_ref.shape[0]
    n_chunks = D_A // LANES
    for i, (dil, src) in enumerate(((4, o4_ref), (4, l4_ref), (16, o16_ref), (16, l16_ref))):
        for r in range(dil):
            for c in range(n_chunks):
                slab_ref[i * n_chunks + c, pl.ds(r, tm // dil, stride=dil), :] = src[r, :, c * LANES:(c + 1) * LANES]
    for c in range(n_chunks):
        sl = slice(c * LANES, (c + 1) * LANES)
        l1, l4, l16 = l1_ref[:, sl], slab_ref[n_chunks + c], slab_ref[3 * n_chunks + c]
        m = jnp.maximum(jnp.maximum(l1, l4), l16)
        e1, e4, e16 = jnp.exp(l1 - m), jnp.exp(l4 - m), jnp.exp(l16 - m)
        oa = (e1 * o1_ref[:, sl] + e4 * slab_ref[c] + e16 * slab_ref[2 * n_chunks + c]) * (1.0 / (e1 + e4 + e16))
        oa_ref[:, sl] = oa.astype(BF16)
    mix = _dot(oa_ref[...], w_ref[0:D_A, :]) + _dot(ob_ref[...].astype(BF16), w_ref[D_A:, :])
    out_ref[...] = _layer_norm(DN_ALPHA * y_ref[...] + mix, g_ref[...], b_ref[...])


def _attn_out_prompt(pats, ob, y, w, g, b, seq):
    n = y.shape[0]
    tm = ROW_TILE
    tps = seq // tm
    half = pl.BlockSpec((tm, D_A), lambda i: (i, 0))
    full = pl.BlockSpec((tm, D_MODEL), lambda i: (i, 0))
    planes = lambda dil: pl.BlockSpec((None, dil, tm // dil, D_A), lambda i: (i // tps, 0, i % tps, 0))
    (o1, l1), (o4, l4), (o16, l16) = pats
    return pl.pallas_call(
        _attn_out_prompt_kernel,
        grid=(n // tm,),
        in_specs=[half, half, planes(4), planes(4), planes(16), planes(16), half, full,
                  _resident((D_MODEL, D_MODEL)), _resident((1, D_MODEL)), _resident((1, D_MODEL))],
        out_specs=full,
        out_shape=jax.ShapeDtypeStruct((n, D_MODEL), F32),
        scratch_shapes=[pltpu.VMEM((4 * D_A // LANES, tm, LANES), F32), pltpu.VMEM((tm, D_A), BF16)],
        compiler_params=_params(1),
        name="attn_out_prompt",
    )(o1, l1, o4, l4, o16, l16, ob, y, w, g, b)


def _mix_out_kernel(o_ref, y_ref, w_ref, g_ref, b_ref, out_ref):
    mix = _dot(o_ref[...].astype(BF16), w_ref[...])
    out_ref[...] = _layer_norm(DN_ALPHA * y_ref[...] + mix, g_ref[...], b_ref[...])


def _attn_out_sample(o, y, w, g, b):
    n = y.shape[0]
    tm = min(ROW_TILE, n)
    full = pl.BlockSpec((tm, D_MODEL), lambda i: (i, 0))
    return pl.pallas_call(
        _mix_out_kernel,
        grid=(n // tm,),
        in_specs=[full, full, _resident((D_MODEL, D_MODEL)), _resident((1, D_MODEL)), _resident((1, D_MODEL))],
        out_specs=full,
        out_shape=jax.ShapeDtypeStruct((n, D_MODEL), F32),
        compiler_params=_params(1),
        name="attn_out_sample",
    )(o, y, w, g, b)


def _ssm_discretize(lam_re, lam_im, log_dt, b_re, b_im):
    dt = jnp.exp(log_dt.astype(F32))[:, None]
    lr, li = lam_re.astype(F32), lam_im.astype(F32)
    mag = jnp.exp(lr * dt)
    ab_re, ab_im = mag * jnp.cos(li * dt), mag * jnp.sin(li * dt)
    nr, ni = ab_re - 1.0, ab_im
    den = lr * lr + li * li
    fr, fi = (nr * lr + ni * li) / den, (ni * lr - nr * li) / den
    bb_re = fr[..., None] * b_re - fi[..., None] * b_im
    bb_im = fr[..., None] * b_im + fi[..., None] * b_re
    return ab_re, ab_im, bb_re, bb_im


def _ssm_matrices(bb_re, bb_im, c_re, c_im):
    gpc = LANES // SSM_GROUP
    eye = jnp.eye(gpc, dtype=F32)

    def in_blocks(bb):
        a = bb.reshape(N_LCHUNK, gpc, SSM_STATE, SSM_GROUP)
        return jnp.einsum("jgpn,gh->jgnhp", a, eye).reshape(N_LCHUNK, LANES, ST_CHUNK)

    def out_blocks(cc):
        a = cc.reshape(N_LCHUNK, gpc, SSM_GROUP, SSM_STATE)
        return jnp.einsum("jgnp,gh->jgphn", a, eye).reshape(N_LCHUNK, ST_CHUNK, LANES)

    bmat = jnp.concatenate([in_blocks(bb_re), in_blocks(bb_im)], -1)
    cmat = jnp.concatenate([out_blocks(c_re), -out_blocks(c_im)], 1)
    return bmat.astype(BF16), cmat.astype(BF16)


SEG_TILE = ROW_TILE // N_SEG


def _ssm_in_prompt_kernel(x_ref, w_ref, o_ref):
    x = x_ref[...].reshape(N_SEG * SEG_TILE, D_MODEL)
    u = _dot(x.astype(BF16), w_ref[...])
    for s in range(N_SEG):
        for c in range(N_LCHUNK):
            o_ref[c, pl.ds(s, SEG_TILE, stride=N_SEG), :] = u[s * SEG_TILE:(s + 1) * SEG_TILE, c * LANES:(c + 1) * LANES]


def _ssm_in_prompt(y, w):
    return pl.pallas_call(
        _ssm_in_prompt_kernel,
        grid=(BATCH, SEG_LEN // SEG_TILE),
        in_specs=[pl.BlockSpec((None, N_SEG, SEG_TILE, D_MODEL), lambda b, t: (b, 0, t, 0)),
                  _resident((D_MODEL, D_MODEL))],
        out_specs=pl.BlockSpec((None, N_LCHUNK, ROW_TILE, LANES), lambda b, t: (b, 0, t, 0)),
        out_shape=jax.ShapeDtypeStruct((BATCH, N_LCHUNK, SEQ, LANES), F32),
        compiler_params=_params(2),
        name="ssm_in_prompt",
    )(y.reshape(BATCH, N_SEG, SEG_LEN, D_MODEL), w)


def _ssm_in_sample_kernel(x_ref, w_ref, o_ref, slab_ref):
    u = _dot(x_ref[...].astype(BF16), w_ref[...])
    for c in range(N_LCHUNK):
        slab_ref[c] = u[:, c * LANES:(c + 1) * LANES]
    for l in range(DEC_SEQ):
        for c in range(N_LCHUNK):
            o_ref[c, l * DEC_BATCH:(l + 1) * DEC_BATCH, :] = slab_ref[c, pl.ds(l, DEC_BATCH, stride=DEC_SEQ), :]


def _ssm_in_sample(y, w):
    n = DEC_BATCH * DEC_SEQ
    return pl.pallas_call(
        _ssm_in_sample_kernel,
        grid=(1,),
        in_specs=[_resident((n, D_MODEL)), _resident((D_MODEL, D_MODEL))],
        out_specs=pl.BlockSpec((N_LCHUNK, n, LANES), lambda i: (0, 0, 0)),
        out_shape=jax.ShapeDtypeStruct((N_LCHUNK, n, LANES), F32),
        scratch_shapes=[pltpu.VMEM((N_LCHUNK, n, LANES), F32)],
        compiler_params=_params(1),
        name="ssm_in_sample",
    )(y, w)


def _cmul(ar, ai, br, bi):
    return ar * br - ai * bi, ar * bi + ai * br


def _scan_prompt_kernel(u_ref, bmat_ref, cmat_ref, are_ref, aim_ref, d_ref, h0r_ref, h0i_ref,
                        y_ref, hnr_ref, hni_ref, bu_s, hs_s):
    n_tiles = SEQ // SCAN_TILE
    steps = SCAN_TILE // N_SEG
    for i in range(n_tiles):
        rows = slice(i * SCAN_TILE, (i + 1) * SCAN_TILE)
        bu_s[rows, :] = _dot(u_ref[rows, :].astype(BF16), bmat_ref[...])
    a_re1, a_im1 = are_ref[...], aim_ref[...]
    a_re = jnp.broadcast_to(a_re1, (N_SEG, ST_CHUNK))
    a_im = jnp.broadcast_to(a_im1, (N_SEG, ST_CHUNK))

    def advance(row, hr, hi):
        bur = bu_s[pl.ds(row, N_SEG), 0:ST_CHUNK]
        bui = bu_s[pl.ds(row, N_SEG), ST_CHUNK:2 * ST_CHUNK]
        return a_re * hr - a_im * hi + bur, a_re * hi + a_im * hr + bui

    def pass1(k, carry):
        return advance(pl.multiple_of(k * N_SEG, N_SEG), *carry)

    zero = jnp.zeros((N_SEG, ST_CHUNK), F32)
    er, ei = lax.fori_loop(0, SEG_LEN, pass1, (zero, zero), unroll=8)

    pr, pi = a_re1, a_im1
    for _ in range(int(math.log2(SEG_LEN))):
        pr, pi = _cmul(pr, pi, pr, pi)
    hr, hi = h0r_ref[...], h0i_ref[...]
    starts_r, starts_i = [], []
    for s in range(N_SEG):
        starts_r.append(hr)
        starts_i.append(hi)
        gr, gi = _cmul(pr, pi, hr, hi)
        hr, hi = gr + er[s:s + 1], gi + ei[s:s + 1]
    hnr_ref[...] = hr
    hni_ref[...] = hi
    init = (jnp.concatenate(starts_r, 0), jnp.concatenate(starts_i, 0))

    def tile(i, carry):
        base = pl.multiple_of(i * SCAN_TILE, SCAN_TILE)

        def pass2(k, c):
            off = pl.multiple_of(k * N_SEG, N_SEG)
            nr, ni = advance(base + off, *c)
            hs_s[pl.ds(off, N_SEG), 0:ST_CHUNK] = nr
            hs_s[pl.ds(off, N_SEG), ST_CHUNK:2 * ST_CHUNK] = ni
            return nr, ni

        carry = lax.fori_loop(0, steps, pass2, carry, unroll=8)
        y = _dot(hs_s[...].astype(BF16), cmat_ref[...]) + d_ref[...] * u_ref[pl.ds(base, SCAN_TILE), :]
        y_ref[pl.ds(base, SCAN_TILE), :] = y
        return carry

    lax.fori_loop(0, n_tiles, tile, init)


def _scan_prompt(u, bmat, cmat, a_re, a_im, d_skip, h0r, h0i):
    chunk = lambda rows: pl.BlockSpec((None, None, rows, LANES), lambda b, j: (b, j, 0, 0))
    per_j = lambda r, c: pl.BlockSpec((None, r, c), lambda b, j: (j, 0, 0))
    state = pl.BlockSpec((None, 1, ST_CHUNK), lambda b, j: (b, 0, j))
    return pl.pallas_call(
        _scan_prompt_kernel,
        grid=(BATCH, N_LCHUNK),
        in_specs=[chunk(SEQ), per_j(LANES, 2 * ST_CHUNK), per_j(2 * ST_CHUNK, LANES), per_j(1, ST_CHUNK),
                  per_j(1, ST_CHUNK), per_j(1, LANES), state, state],
        out_specs=[chunk(SEQ), state, state],
        out_shape=[jax.ShapeDtypeStruct((BATCH, N_LCHUNK, SEQ, LANES), F32),
                   jax.ShapeDtypeStruct((BATCH, 1, N_STATE), F32), jax.ShapeDtypeStruct((BATCH, 1, N_STATE), F32)],
        scratch_shapes=[pltpu.VMEM((SEQ, 2 * ST_CHUNK), F32), pltpu.VMEM((SCAN_TILE, 2 * ST_CHUNK), F32)],
        compiler_params=_params(2),
        name="ssm_scan_prompt",
    )(u, bmat, cmat, a_re, a_im, d_skip, h0r, h0i)


def _scan_sample_kernel(u_ref, bmat_ref, cmat_ref, are_ref, aim_ref, d_ref, h0r_ref, h0i_ref,
                        y_ref, hnr_ref, hni_ref):
    a_re, a_im = are_ref[...], aim_ref[...]
    hr, hi = h0r_ref[...].T, h0i_ref[...].T
    for l in range(DEC_SEQ):
        rows = slice(l * DEC_BATCH, (l + 1) * DEC_BATCH)
        u = u_ref[rows, :]
        bu = _dot(u.astype(BF16), bmat_ref[...])
        gr, gi = _cmul(a_re, a_im, hr, hi)
        hr, hi = gr + bu[:, :ST_CHUNK], gi + bu[:, ST_CHUNK:]
        h = jnp.concatenate([hr, hi], axis=1).astype(BF16)
        y_ref[rows, :] = _dot(h, cmat_ref[...]) + d_ref[...] * u
    hnr_ref[...] = hr.T
    hni_ref[...] = hi.T


def _scan_sample(u, bmat, cmat, a_re, a_im, d_skip, h0r, h0i):
    n = DEC_SEQ * DEC_BATCH
    chunk = pl.BlockSpec((None, n, LANES), lambda j: (j, 0, 0))
    per_j = lambda r, c: pl.BlockSpec((None, r, c), lambda j: (j, 0, 0))
    state = pl.BlockSpec((ST_CHUNK, DEC_BATCH), lambda j: (j, 0))
    return pl.pallas_call(
        _scan_sample_kernel,
        grid=(N_LCHUNK,),
        in_specs=[chunk, per_j(LANES, 2 * ST_CHUNK), per_j(2 * ST_CHUNK, LANES), per_j(1, ST_CHUNK),
                  per_j(1, ST_CHUNK), per_j(1, LANES), state, state],
        out_specs=[chunk, state, state],
        out_shape=[jax.ShapeDtypeStruct((N_LCHUNK, n, LANES), F32),
                   jax.ShapeDtypeStruct((N_STATE, DEC_BATCH), F32), jax.ShapeDtypeStruct((N_STATE, DEC_BATCH), F32)],
        compiler_params=_params(1),
        name="ssm_scan_sample",
    )(u, bmat, cmat, a_re, a_im, d_skip, h0r, h0i)


def _ssm_out_kernel(sample, s_ref, y_ref, wglu_ref, bglu_ref, wout_ref, g_ref, b_ref, out_ref, z_ref):
    n_rows = z_ref.shape[1]
    if sample:
        parts = [(pl.ds(l, DEC_BATCH, stride=DEC_SEQ), slice(l * DEC_BATCH, (l + 1) * DEC_BATCH))
                 for l in range(DEC_SEQ)]
    else:
        parts = [(slice(s * SEG_TILE, (s + 1) * SEG_TILE), pl.ds(s, SEG_TILE, stride=N_SEG)) for s in range(N_SEG)]
    for c in range(N_LCHUNK):
        for tok_rows, slab_rows in parts:
            z_ref[c, tok_rows, :] = s_ref[c, slab_rows, :]
    z = jax.nn.gelu(jnp.concatenate([z_ref[c] for c in range(N_LCHUNK)], axis=1))
    gate = jax.nn.sigmoid(_dot(z.astype(BF16), wglu_ref[...]) + bglu_ref[...])
    mix = _dot((z * gate).astype(BF16), wout_ref[...])
    res = y_ref[...].reshape(n_rows, D_MODEL)
    out = _layer_norm(DN_ALPHA * res + mix, g_ref[...], b_ref[...])
    out_ref[...] = out.reshape(out_ref.shape)


def _ssm_out(s, y, s_spec, y_spec, grid, rows, sample, w_glu, b_glu, w_out, g, b, name):
    return pl.pallas_call(
        functools.partial(_ssm_out_kernel, sample),
        grid=grid,
        in_specs=[s_spec, y_spec, _resident((D_MODEL, D_MODEL)), _resident((1, D_MODEL)),
                  _resident((D_MODEL, D_MODEL)), _resident((1, D_MODEL)), _resident((1, D_MODEL))],
        out_specs=y_spec,
        out_shape=jax.ShapeDtypeStruct(y.shape, F32),
        scratch_shapes=[pltpu.VMEM((N_LCHUNK, rows, LANES), F32)],
        compiler_params=_params(len(grid)),
        name=name,
    )(s, y, w_glu, b_glu, w_out, g, b)


def _attn_prompt(yp, w_in, sinks, w_out, g, b):
    cos_p, sin_p = _rope_tables(jnp.arange(SEQ))
    tiles_per_seq = SEQ // ROW_TILE
    qa, ka, va, qb, kb, vb, *extra = _attn_proj(
        yp, w_in, cos_p, sin_p, lambda i: i % tiles_per_seq, BF16, prompt_seq=(BATCH, SEQ))
    dilated, tails = extra[:6], extra[6:]
    seq3 = lambda a: a.reshape(BATCH, SEQ, a.shape[-1])
    plane1 = lambda a: a.reshape(BATCH, 1, SEQ, D_A)
    o1, l1 = _band_a(plane1(qa), plane1(ka), plane1(va))
    pats = [(o1.reshape(BATCH * SEQ, D_A), l1.reshape(BATCH * SEQ, D_A))]
    for i in range(len(DILATIONS) - 1):
        pats.append(_band_a(*dilated[3 * i:3 * i + 3]))
    ob = _band_b(seq3(qb), seq3(kb), seq3(vb), sinks)
    yp = _attn_out_prompt(pats, ob, yp, w_out, g, b, SEQ)
    heads = lambda a, nh: jnp.transpose(a.reshape(BATCH, nh, HEAD_DIM, a.shape[-1]), (0, 3, 1, 2))[None]
    prompt_cache = (heads(tails[0], N_HEADS_A), heads(tails[1], N_HEADS_A),
                    heads(tails[2], N_KV_B), heads(tails[3], N_KV_B))
    return yp, prompt_cache


def _attn_sample_path(ys, cache_ak, cache_av, cache_bk, cache_bv, w_in, sinks, w_out, g, b):
    cos_s, sin_s = _rope_tables(PAST_LEN + jnp.arange(DEC_SEQ))
    reps = DEC_BATCH * DEC_SEQ // DEC_SEQ
    cos_s, sin_s = jnp.tile(cos_s, (reps, 1)), jnp.tile(sin_s, (reps, 1))
    qa, ka, va, qb, kb, vb = _attn_proj(ys, w_in, cos_s, sin_s, lambda i: i, F32)
    stored = lambda a: jnp.transpose(a, (0, 2, 3, 1))
    o = _attn_sample(qa, ka, va, stored(cache_ak), stored(cache_av),
                     qb, kb, vb, cache_bk.reshape(DEC_BATCH, WIN_B, D_BKV), cache_bv.reshape(DEC_BATCH, WIN_B, D_BKV),
                     sinks)
    ys = _attn_out_sample(o, ys, w_out, g, b)
    new = lambda a, nh: a.reshape(1, DEC_BATCH, DEC_SEQ, nh, HEAD_DIM)
    sample_cache = (new(ka, N_HEADS_A), new(va, N_HEADS_A), new(kb, N_KV_B), new(vb, N_KV_B))
    return ys, sample_cache


def _attn_layer(yp, ys, cache_ak, cache_av, cache_bk, cache_bv, w_in, sinks, w_out, g, b):
    w_in = w_in.astype(BF16)
    w_out = w_out.astype(BF16)
    yp, prompt_cache = _attn_prompt(yp, w_in, sinks, w_out, g, b)
    ys, sample_cache = _attn_sample_path(ys, cache_ak, cache_av, cache_bk, cache_bv, w_in, sinks, w_out, g, b)
    return yp, ys, prompt_cache, sample_cache


def _ssm_layer(yp, ys, state_re, state_im, w_in, lam_re, lam_im, log_dt, b_re, b_im, c_re, c_im, d_skip,
               w_glu, b_glu, w_out, g, b):
    w_in, w_glu, w_out = w_in.astype(BF16), w_glu.astype(BF16), w_out.astype(BF16)
    ab_re, ab_im, bb_re, bb_im = _ssm_discretize(lam_re, lam_im, log_dt, b_re, b_im)
    bmat, cmat = _ssm_matrices(bb_re, bb_im, c_re, c_im)
    a_re = ab_re.reshape(N_LCHUNK, 1, ST_CHUNK)
    a_im = ab_im.reshape(N_LCHUNK, 1, ST_CHUNK)
    d3 = d_skip.astype(F32).reshape(N_LCHUNK, 1, LANES)
    b_glu = b_glu.reshape(1, D_MODEL)
    mats = (bmat, cmat, a_re, a_im, d3)
    yp, prompt_state = _ssm_prompt(yp, w_in, mats, w_glu, b_glu, w_out, g, b)
    ys, sample_state = _ssm_sample(ys, state_re, state_im, w_in, mats, w_glu, b_glu, w_out, g, b)
    return yp, ys, prompt_state, sample_state


def _ssm_prompt(yp, w_in, mats, w_glu, b_glu, w_out, g, b):
    bmat, cmat, a_re, a_im, d3 = mats
    up = _ssm_in_prompt(yp, w_in)
    zero = jnp.zeros((BATCH, 1, N_STATE), F32)
    sp, pr, pi = _scan_prompt(up, bmat, cmat, a_re, a_im, d3, zero, zero)
    yp = _ssm_out(sp, yp.reshape(BATCH, N_SEG, SEG_LEN, D_MODEL),
                  pl.BlockSpec((None, N_LCHUNK, ROW_TILE, LANES), lambda bb, t: (bb, 0, t, 0)),
                  pl.BlockSpec((None, N_SEG, SEG_TILE, D_MODEL), lambda bb, t: (bb, 0, t, 0)),
                  (BATCH, SEG_LEN // SEG_TILE), ROW_TILE, False, w_glu, b_glu, w_out, g, b,
                  "ssm_out_prompt").reshape(BATCH * SEQ, D_MODEL)
    prompt_state = (pr.reshape(1, BATCH, N_SSM_GROUPS, SSM_STATE), pi.reshape(1, BATCH, N_SSM_GROUPS, SSM_STATE))
    return yp, prompt_state


def _ssm_sample(ys, state_re, state_im, w_in, mats, w_glu, b_glu, w_out, g, b):
    bmat, cmat, a_re, a_im, d3 = mats
    us = _ssm_in_sample(ys, w_in)
    stored = lambda a: jnp.transpose(a, (1, 2, 0)).reshape(N_STATE, DEC_BATCH)
    logical = lambda a: jnp.transpose(a.reshape(N_SSM_GROUPS, SSM_STATE, DEC_BATCH), (2, 0, 1))[None]
    ss, sr, si = _scan_sample(us, bmat, cmat, a_re, a_im, d3, stored(state_re), stored(state_im))
    n = DEC_BATCH * DEC_SEQ
    ys = _ssm_out(ss, ys,
                  pl.BlockSpec((N_LCHUNK, n, LANES), lambda i: (0, 0, 0)),
                  pl.BlockSpec((n, D_MODEL), lambda i: (0, 0)),
                  (1,), n, True, w_glu, b_glu, w_out, g, b, "ssm_out_sample")
    return ys, (logical(sr), logical(si))


def kernel(x_prompt, x_sample, cache_a_k, cache_a_v, cache_b_k, cache_b_v, state_c_re, state_c_im, ln_g, ln_b, ffn_w_gate, ffn_w_up, ffn_w_down, attn_w_in, attn_sinks, attn_w_out, ssm_w_in, ssm_lambda_re, ssm_lambda_im, ssm_log_dt, ssm_b_re, ssm_b_im, ssm_c_re, ssm_c_im, ssm_d, ssm_w_glu, ssm_b_glu, ssm_w_out):
    yp = x_prompt.reshape(BATCH * SEQ, D_MODEL)
    ys = x_sample.reshape(DEC_BATCH * DEC_SEQ, D_MODEL)
    ln = lambda l, k: (ln_g[l, k].reshape(1, D_MODEL), ln_b[l, k].reshape(1, D_MODEL))

    wg, wu, wd = ffn_w_gate.astype(BF16), ffn_w_up.astype(BF16), ffn_w_down.astype(BF16)

    def ffn_pair(yp, ys, l, k, ln_idx):
        g, b = ln(l, ln_idx)
        return _ffn(yp, wg, wu, wd, g, b, l, k), _ffn(ys, wg, wu, wd, g, b, l, k)

    yp, ys = ffn_pair(yp, ys, 0, 0, 0)
    yp, ys, p_cache, s_cache = _attn_layer(yp, ys, cache_a_k[0], cache_a_v[0], cache_b_k[0], cache_b_v[0],
                                           attn_w_in[0], attn_sinks[0], attn_w_out[0], *ln(0, 1))
    yp, ys = ffn_pair(yp, ys, 0, 1, 2)
    yp, ys = ffn_pair(yp, ys, 1, 0, 0)
    yp, ys, p_state, s_state = _ssm_layer(yp, ys, state_c_re[0], state_c_im[0], ssm_w_in[0], ssm_lambda_re[0],
                                          ssm_lambda_im[0], ssm_log_dt[0], ssm_b_re[0], ssm_b_im[0], ssm_c_re[0],
                                          ssm_c_im[0], ssm_d[0], ssm_w_glu[0], ssm_b_glu[0], ssm_w_out[0], *ln(1, 1))
    yp, ys = ffn_pair(yp, ys, 1, 1, 2)
    return (yp.reshape(BATCH, SEQ, D_MODEL), ys.reshape(DEC_BATCH, DEC_SEQ, D_MODEL),
            *p_cache, *p_state, *s_cache, *s_state)
```

```python
import functools
import math

import jax
import jax.numpy as jnp
from jax import lax
from jax.experimental import pallas as pl
from jax.experimental.pallas import tpu as pltpu

F32 = jnp.float32
BF16 = jnp.bfloat16

D_MODEL = 1024
BATCH = 4
SEQ = 4096
DEPTH = 2
DEC_BATCH = 128
DEC_SEQ = 8
PAST_LEN = 16384
HEAD_DIM = 64
N_HEADS_A = 8
DILATIONS = (1, 4, 16)
WIN_A = 2048
N_HEADS_B = 8
N_KV_B = 2
WIN_B = 128
ROPE_THETA = 10000.0
D_A = N_HEADS_A * HEAD_DIM
D_BQ = N_HEADS_B * HEAD_DIM
D_BKV = N_KV_B * HEAD_DIM
D_IN_ATTN = 3 * D_A + D_BQ + 2 * D_BKV
SSM_GROUP = 16
N_SSM_GROUPS = D_MODEL // SSM_GROUP
SSM_STATE = 64
N_STATE = N_SSM_GROUPS * SSM_STATE
D_FF = 2816
DN_ALPHA = (2 * DEPTH) ** 0.25
FFN_RES = 0.5
LN_EPS = 1e-5
ATTN_SCALE = HEAD_DIM ** -0.5

LANES = 128
SUBLANES = 8
MXU_N = 256
VMEM_LIMIT = 56 * 1024 * 1024

ROW_TILE = 512
FFN_ROW_TILE = 1024
FF_CHUNK = MXU_N
TQ = 128
N_SEG = SUBLANES
SEG_LEN = SEQ // N_SEG
N_LCHUNK = D_MODEL // LANES
ST_CHUNK = N_STATE // N_LCHUNK
NK_PAD = WIN_A + LANES
NKB_PAD = 2 * WIN_B

NEG_INF = float("-inf")


def _params(n_axes, vmem=VMEM_LIMIT):
    return pltpu.CompilerParams(dimension_semantics=("arbitrary",) * n_axes, vmem_limit_bytes=vmem)


def _resident(shape):
    return pl.BlockSpec(shape, lambda *_: (0,) * len(shape), pipeline_mode=pl.Buffered(1))


def _layer_norm(x, g, b):
    mu = jnp.mean(x, -1, keepdims=True)
    xc = x - mu
    var = jnp.mean(xc * xc, -1, keepdims=True)
    return xc * lax.rsqrt(var + LN_EPS) * g + b


def _dot(a, b):
    return jnp.dot(a, b, preferred_element_type=F32)


def _dot_nt(a, b):
    return lax.dot_general(a, b, (((1,), (1,)), ((), ())), preferred_element_type=F32)


def _ffn_kernel(x_ref, wg_ref, wu_ref, wd_ref, g_ref, b_ref, o_ref, h_ref):
    x = x_ref[...]
    xb = x.astype(BF16)
    for c in range(D_FF // FF_CHUNK):
        sl = slice(c * FF_CHUNK, (c + 1) * FF_CHUNK)
        gate = _dot(xb, wg_ref[:, sl])
        up = _dot(xb, wu_ref[:, sl])
        h_ref[:, sl] = (gate * jax.nn.sigmoid(gate) * up).astype(BF16)
    y = DN_ALPHA * x + FFN_RES * _dot(h_ref[...], wd_ref[...])
    o_ref[...] = _layer_norm(y, g_ref[...], b_ref[...])


def _ffn(x, wg, wu, wd, g, b, layer=0, which=0):
    n = x.shape[0]
    tm = min(FFN_ROW_TILE, n)
    row = pl.BlockSpec((tm, D_MODEL), lambda i: (i, 0))
    if wg.ndim == 4:
        pick = lambda r, c: pl.BlockSpec((None, None, r, c), lambda i: (layer, which, 0, 0),
                                         pipeline_mode=pl.Buffered(1))
    else:
        pick = lambda r, c: _resident((r, c))
    return pl.pallas_call(
        _ffn_kernel,
        grid=(n // tm,),
        in_specs=[row, pick(D_MODEL, D_FF), pick(D_MODEL, D_FF), pick(D_FF, D_MODEL),
                  _resident((1, D_MODEL)), _resident((1, D_MODEL))],
        out_specs=row,
        out_shape=jax.ShapeDtypeStruct((n, D_MODEL), F32),
        scratch_shapes=[pltpu.VMEM((tm, D_FF), BF16)],
        compiler_params=_params(1),
        name="ffn",
    )(x, wg, wu, wd, g, b)


def _rope_tables(pos):
    half = HEAD_DIM // 2
    inv_freq = ROPE_THETA ** (-jnp.arange(half, dtype=F32) / half)
    ang = pos.astype(F32)[:, None] * inv_freq[None, :]
    cos, sin = jnp.cos(ang), jnp.sin(ang)
    cos_t = jnp.concatenate([cos, cos, cos, cos], -1)
    sin_t = jnp.concatenate([-sin, sin, -sin, sin], -1)
    return cos_t, sin_t


def _attn_proj_kernel(tiles_per_seq, x_ref, w_ref, cos_ref, sin_ref, qa_ref, ka_ref, va_ref, qb_ref, kb_ref, vb_ref,
                      *extra):
    xb = x_ref[...].astype(BF16)
    slab_ref = extra[-1] if extra else None
    dilated = extra[:6]
    n_chunks = D_A // LANES
    tm = x_ref.shape[0]

    def keep(tensor, c, val):
        if slab_ref is not None:
            slab_ref[tensor * n_chunks + c] = val
    cos = cos_ref[...]
    sin = sin_ref[...]
    lane = lax.broadcasted_iota(jnp.int32, cos.shape, 1)
    first_half = (lane & (HEAD_DIM // 2)) == 0

    def rope(z):
        rot = jnp.where(first_half, pltpu.roll(z, LANES - HEAD_DIM // 2, 1), pltpu.roll(z, HEAD_DIM // 2, 1))
        return z * cos + rot * sin

    def project(col0, ncols):
        return _dot(xb, w_ref[:, col0:col0 + ncols])

    def rope_chunks(z):
        return [rope(z[:, c * LANES:(c + 1) * LANES]) for c in range(z.shape[1] // LANES)]

    col = 0
    for c, r in enumerate(rope_chunks(project(col, D_A))):
        r = r * ATTN_SCALE
        qa_ref[:, c * LANES:(c + 1) * LANES] = r.astype(qa_ref.dtype)
        keep(0, c, r)
    col += D_A
    for c, r in enumerate(rope_chunks(project(col, D_A))):
        ka_ref[:, c * LANES:(c + 1) * LANES] = r.astype(ka_ref.dtype)
        keep(1, c, r)
    col += D_A
    z = project(col, D_A)
    va_ref[...] = z.astype(va_ref.dtype)
    for c in range(n_chunks):
        keep(2, c, z[:, c * LANES:(c + 1) * LANES])
    col += D_A
    if extra:
        kat_ref, vat_ref, kbt_ref, vbt_ref = extra[6:10]
        tile_in_seq = pl.program_id(0) % tiles_per_seq

        @pl.when(tile_in_seq >= tiles_per_seq - WIN_A // tm)
        def _():
            for c in range(n_chunks):
                kat_ref[c * LANES:(c + 1) * LANES, :] = slab_ref[n_chunks + c].T
                vat_ref[c * LANES:(c + 1) * LANES, :] = slab_ref[2 * n_chunks + c].T

        for i, dil in enumerate(DILATIONS[1:]):
            for tensor in range(3):
                out_ref = dilated[3 * i + tensor]
                for r in range(dil):
                    for c in range(n_chunks):
                        rows = slab_ref[tensor * n_chunks + c, pl.ds(r, tm // dil, stride=dil), :]
                        out_ref[r, :, c * LANES:(c + 1) * LANES] = rows.astype(out_ref.dtype)
    for c, r in enumerate(rope_chunks(project(col, D_BQ))):
        qb_ref[:, c * LANES:(c + 1) * LANES] = (r * ATTN_SCALE).astype(qb_ref.dtype)
    col += D_BQ
    z = project(col, 2 * D_BKV)
    r = rope(z[:, :D_BKV])
    kb_ref[...] = r.astype(kb_ref.dtype)
    vb_ref[...] = z[:, D_BKV:].astype(vb_ref.dtype)
    if extra:
        @pl.when(tile_in_seq == tiles_per_seq - 1)
        def _():
            kbt_ref[...] = r[tm - WIN_B:, :].T
            vbt_ref[...] = z[tm - WIN_B:, D_BKV:].T


def _attn_proj(x, w, cos_t, sin_t, table_block, act_dtype, prompt_seq=None):
    n = x.shape[0]
    tm = min(ROW_TILE, n)

    def row(width):
        return pl.BlockSpec((tm, width), lambda i: (i, 0))

    tab = pl.BlockSpec((tm, LANES), lambda i: (table_block(i), 0))
    widths = (D_A, D_A, D_A, D_BQ, D_BKV, D_BKV)
    out_shape = [jax.ShapeDtypeStruct((n, wd), act_dtype) for wd in widths]
    out_specs = [row(wd) for wd in widths]
    scratch = []
    tps = None
    if prompt_seq is not None:
        bsz, seq = prompt_seq
        tps = seq // tm
        for dil in DILATIONS[1:]:
            out_shape += [jax.ShapeDtypeStruct((bsz, dil, seq // dil, D_A), BF16)] * 3
            out_specs += [pl.BlockSpec((None, dil, tm // dil, D_A), lambda i: (i // tps, 0, i % tps, 0))] * 3
        first_tail = tps - WIN_A // tm
        out_shape += [jax.ShapeDtypeStruct((bsz, D_A, WIN_A), F32)] * 2
        out_specs += [pl.BlockSpec((None, D_A, tm), lambda i: (i // tps, 0, jnp.maximum(i % tps - first_tail, 0)))] * 2
        out_shape += [jax.ShapeDtypeStruct((bsz, D_BKV, WIN_B), F32)] * 2
        out_specs += [pl.BlockSpec((None, D_BKV, WIN_B), lambda i: (i // tps, 0, 0))] * 2
        scratch = [pltpu.VMEM((3 * D_A // LANES, tm, LANES), F32)]
    return pl.pallas_call(
        functools.partial(_attn_proj_kernel, tps),
        grid=(n // tm,),
        in_specs=[row(D_MODEL), _resident((D_MODEL, D_IN_ATTN)), tab, tab],
        out_specs=out_specs,
        out_shape=out_shape,
        scratch_shapes=scratch,
        compiler_params=_params(1),
        name="attn_proj",
    )(x, w, cos_t, sin_t)


def _lane_lo(shape):
    return lax.broadcasted_iota(jnp.int32, shape, 1) < HEAD_DIM


def _half_masks_bf16():
    lo = jnp.where(_lane_lo((1, LANES)), 1.0, 0.0).astype(BF16)
    return lo, 1 - lo


def _band_masks(n_heads, t, sub, prev_strict):
    row = lax.broadcasted_iota(jnp.int32, (n_heads * TQ, TQ), 0) & (TQ - 1)
    col = lax.broadcasted_iota(jnp.int32, (n_heads * TQ, TQ), 1)
    mask_c = col <= row
    shift = jnp.where(t > 0, 0, TQ) if sub == 0 else 0
    mask_p = (col > row + shift) if prev_strict else (col >= row + shift)
    return mask_c, mask_p


def _sub_tile_kv(sub, sl, kc_ref, kp_ref, vc_ref, vp_ref):
    if sub == 0:
        return kc_ref[0:TQ, sl], kp_ref[:, sl], vc_ref[0:TQ, sl], vp_ref[:, sl]
    return kc_ref[TQ:2 * TQ, sl], kc_ref[0:TQ, sl], vc_ref[TQ:2 * TQ, sl], vc_ref[0:TQ, sl]


def _band_softmax(qs, kc, kp, vc, vp, mask_c, mask_p, sink=None):
    s_c = jnp.where(mask_c, _dot_nt(qs, kc), NEG_INF)
    s_p = jnp.where(mask_p, _dot_nt(qs, kp), NEG_INF)
    m = jnp.max(jnp.maximum(s_c, s_p), -1, keepdims=True)
    if sink is not None:
        m = jnp.maximum(m, sink)
    p_c = jnp.exp(s_c - m)
    p_p = jnp.exp(s_p - m)
    den = jnp.sum(p_c + p_p, -1, keepdims=True)
    if sink is not None:
        den = den + jnp.exp(sink - m)
    acc = _dot(p_c.astype(BF16), vc) + _dot(p_p.astype(BF16), vp)
    return acc * (1.0 / den), m, den


def _band_a_kernel(q_ref, kc_ref, kp_ref, vc_ref, vp_ref, o_ref, lse_ref):
    t = pl.program_id(2)
    lo = _lane_lo((TQ, LANES))
    lo_bf, hi_bf = _half_masks_bf16()
    for sub in range(2):
        rows = slice(sub * TQ, (sub + 1) * TQ)
        mask_c, mask_p = _band_masks(2, t, sub, prev_strict=False)
        for c in range(D_A // LANES):
            sl = slice(c * LANES, (c + 1) * LANES)
            kc, kp, vc, vp = _sub_tile_kv(sub, sl, kc_ref, kp_ref, vc_ref, vp_ref)
            q2 = q_ref[rows, sl]
            qs = jnp.concatenate([q2 * lo_bf, q2 * hi_bf], axis=0)
            out, m, den = _band_softmax(qs, kc, kp, vc, vp, mask_c, mask_p)
            lse = jnp.broadcast_to(m + jnp.log(den), (2 * TQ, LANES))
            o_ref[rows, sl] = jnp.where(lo, out[0:TQ], out[TQ:])
            lse_ref[rows, sl] = jnp.where(lo, lse[0:TQ], lse[TQ:])


def _band_a(q, k, v):
    bsz, dil, sub, _ = q.shape
    cur = pl.BlockSpec((None, None, 2 * TQ, D_A), lambda b, r, t: (b, r, t, 0))
    prev = pl.BlockSpec((None, None, TQ, D_A), lambda b, r, t: (b, r, jnp.maximum(2 * t - 1, 0), 0))
    return pl.pallas_call(
        _band_a_kernel,
        grid=(bsz, dil, sub // (2 * TQ)),
        in_specs=[cur, cur, prev, cur, prev],
        out_specs=[cur, cur],
        out_shape=[jax.ShapeDtypeStruct((bsz, dil, sub, D_A), F32)] * 2,
        compiler_params=_params(3),
        name=f"band_a_d{dil}",
    )(q, k, k, v, v)


def _swap_halves(x):
    return jnp.concatenate([x[:, HEAD_DIM:], x[:, :HEAD_DIM]], axis=1)


def _band_b_kernel(sink_ref, q_ref, kc_ref, kp_ref, vc_ref, vp_ref, o_ref):
    t = pl.program_id(1)
    group = N_HEADS_B // N_KV_B
    lo = _lane_lo((TQ, LANES))
    lo_bf, hi_bf = _half_masks_bf16()
    for sub in range(2):
        rows = slice(sub * TQ, (sub + 1) * TQ)
        mask_c, mask_p = _band_masks(group, t, sub, prev_strict=True)
        kv = _sub_tile_kv(sub, slice(0, D_BKV), kc_ref, kp_ref, vc_ref, vp_ref)
        kv_swapped = tuple(_swap_halves(a) for a in kv)
        for g in range(N_KV_B):
            own, other = (lo_bf, hi_bf) if g == 0 else (hi_bf, lo_bf)
            kc, kp, vc, vp = (a * own + a_sw * other for a, a_sw in zip(kv, kv_swapped))
            heads = range(g * group, (g + 1) * group)
            qs = jnp.concatenate(
                [q_ref[rows, (h // 2) * LANES:(h // 2 + 1) * LANES] * (lo_bf if h % 2 == 0 else hi_bf) for h in heads],
                axis=0)
            sink = jnp.concatenate([jnp.full((TQ, 1), sink_ref[h], F32) for h in heads], axis=0)
            out, _, _ = _band_softmax(qs, kc, kp, vc, vp, mask_c, mask_p, sink)
            for i in range(group // 2):
                c = g * (group // 2) + i
                even, odd = out[2 * i * TQ:(2 * i + 1) * TQ], out[(2 * i + 1) * TQ:(2 * i + 2) * TQ]
                o_ref[rows, c * LANES:(c + 1) * LANES] = jnp.where(lo, even, odd).astype(o_ref.dtype)


def _band_b(q, k, v, sinks):
    bsz, seq, _ = q.shape
    qs = pl.BlockSpec((None, 2 * TQ, D_BQ), lambda b, t: (b, t, 0))
    cur = pl.BlockSpec((None, 2 * TQ, D_BKV), lambda b, t: (b, t, 0))
    prev = pl.BlockSpec((None, TQ, D_BKV), lambda b, t: (b, jnp.maximum(2 * t - 1, 0), 0))
    o = pl.pallas_call(
        _band_b_kernel,
        grid=(bsz, seq // (2 * TQ)),
        in_specs=[pl.BlockSpec(memory_space=pltpu.SMEM), qs, cur, prev, cur, prev],
        out_specs=qs,
        out_shape=jax.ShapeDtypeStruct((bsz, seq, D_BQ), BF16),
        compiler_params=_params(2),
        name="band_b",
    )(sinks, q, k, k, v, v)
    return o.reshape(bsz * seq, D_BQ)


def _pattern_count(dist):
    cnt = jnp.zeros(dist.shape, F32)
    for dil in DILATIONS:
        cnt = cnt + ((dist >= 0) & (dist <= 128 * dil) & (dist % dil == 0)).astype(F32)
    return cnt


def _sample_tables():
    i = jnp.arange(DEC_SEQ)
    cnt_c = _pattern_count(WIN_A + i[:, None] - jnp.arange(WIN_A)[None, :])
    j = jnp.arange(LANES)
    cnt_n = jnp.where(j[None, :] < DEC_SEQ, _pattern_count(i[:, None] - j[None, :]), 0.0)
    cnt_n = jnp.tile(cnt_n, (N_HEADS_A, 1))
    jb = jnp.arange(NKB_PAD)[None, :]
    dist_b = WIN_B + i[:, None] - jb
    ok_b = (dist_b >= 0) & (dist_b < WIN_B) & (jb < WIN_B + DEC_SEQ)
    mask_b = jnp.tile(ok_b.astype(F32), (N_HEADS_B, 1))
    return cnt_c, cnt_n, mask_b


def _attn_sample_kernel(qa_ref, kan_ref, van_ref, kt_ref, vt_ref, cnt_c_ref, cnt_n_ref,
                        qb_ref, kbn_ref, vbn_ref, kbc_ref, vbc_ref, maskb_ref, sinkcol_ref,
                        o_ref, kb_s, vb_s):
    q = qa_ref[...]
    rows = N_HEADS_A * DEC_SEQ
    q_rep = jnp.concatenate([q] * N_HEADS_A, axis=0)
    row_head = jnp.right_shift(lax.broadcasted_iota(jnp.int32, (rows, D_A), 0), int(math.log2(DEC_SEQ)))
    lane_head = jnp.right_shift(lax.broadcasted_iota(jnp.int32, (rows, D_A), 1), int(math.log2(HEAD_DIM)))
    own = row_head == lane_head
    q_bd = jnp.where(own, q_rep, 0.0).astype(BF16)
    pad = jnp.zeros((LANES - DEC_SEQ, D_A), F32)
    kn = jnp.concatenate([kan_ref[...], pad], 0).astype(BF16)
    vn = jnp.concatenate([van_ref[...], pad], 0).astype(BF16)
    cnt_c, cnt_n = cnt_c_ref[...], cnt_n_ref[...]
    s_new = jnp.where(cnt_n > 0.0, _dot_nt(q_bd, kn), NEG_INF)
    outs, p_new = [], []
    for h in range(N_HEADS_A):
        head_rows = slice(h * DEC_SEQ, (h + 1) * DEC_SEQ)
        q_h = q[:, h * HEAD_DIM:(h + 1) * HEAD_DIM].astype(BF16)
        s_c = jnp.where(cnt_c > 0.0, _dot(q_h, kt_ref[h].astype(BF16)), NEG_INF)
        s_n = s_new[head_rows]
        m = jnp.maximum(jnp.max(s_c, -1, keepdims=True), jnp.max(s_n, -1, keepdims=True))
        p_c = jnp.exp(s_c - m) * cnt_c
        p_n = jnp.exp(s_n - m) * cnt_n[head_rows]
        inv = 1.0 / (jnp.sum(p_c, -1, keepdims=True) + jnp.sum(p_n, -1, keepdims=True))
        outs.append(_dot_nt(p_c.astype(BF16), vt_ref[h].astype(BF16)) * inv)
        p_new.append(p_n * inv)
    out_n = jnp.where(own, _dot(jnp.concatenate(p_new, axis=0).astype(BF16), vn), 0.0)
    oa = jnp.concatenate(outs, axis=1)
    for h in range(N_HEADS_A):
        oa = oa + out_n[h * DEC_SEQ:(h + 1) * DEC_SEQ]
    o_ref[:, 0:D_A] = oa

    n_pad_b = NKB_PAD - WIN_B - DEC_SEQ
    pad_b = jnp.zeros((n_pad_b, D_BKV), F32)
    kb_s[...] = jnp.concatenate([kbc_ref[...], kbn_ref[...], pad_b], 0).astype(BF16)
    vb_s[...] = jnp.concatenate([vbc_ref[...], vbn_ref[...], pad_b], 0).astype(BF16)
    qb = qb_ref[...]
    lo8 = _lane_lo((DEC_SEQ, LANES))
    group = N_HEADS_B // N_KV_B
    pieces = []
    for h in range(N_HEADS_B):
        chunk = qb[:, (h // 2) * LANES:(h // 2 + 1) * LANES]
        g = h // group
        if h % 2 != g:
            chunk = pltpu.roll(chunk, HEAD_DIM, 1)
        pieces.append(jnp.where(lo8 if g == 0 else jnp.logical_not(lo8), chunk, 0.0))
    qb_bd = jnp.concatenate(pieces, axis=0).astype(BF16)
    mask_b = maskb_ref[...]
    sb = jnp.where(mask_b > 0.0, _dot_nt(qb_bd, kb_s[...]), NEG_INF)
    sink = sinkcol_ref[...][:, 0:1]
    mb = jnp.maximum(jnp.max(sb, -1, keepdims=True), sink)
    pb = jnp.exp(sb - mb) * mask_b
    den_b = jnp.sum(pb, -1, keepdims=True) + jnp.exp(sink - mb)
    ob_full = _dot(pb.astype(BF16), vb_s[...]) * (1.0 / den_b)
    for c in range(D_BQ // LANES):
        halves = []
        for half in range(2):
            h = 2 * c + half
            piece = ob_full[h * DEC_SEQ:(h + 1) * DEC_SEQ]
            if half != h // group:
                piece = pltpu.roll(piece, HEAD_DIM, 1)
            halves.append(piece)
        o_ref[:, D_A + c * LANES:D_A + (c + 1) * LANES] = jnp.where(lo8, halves[0], halves[1])


def _attn_sample(qa, kan, van, cache_ak, cache_av, qb, kbn, vbn, cache_bk, cache_bv, sinks):
    cnt_c, cnt_n, mask_b = _sample_tables()
    sink_col = jnp.broadcast_to(jnp.repeat(sinks.astype(F32), DEC_SEQ)[:, None], (N_HEADS_B * DEC_SEQ, LANES))
    new = lambda width: pl.BlockSpec((DEC_SEQ, width), lambda b: (b, 0))
    cache = lambda rows, width: pl.BlockSpec((None, rows, width), lambda b: (b, 0, 0))
    stored = pl.BlockSpec((None, N_HEADS_A, HEAD_DIM, WIN_A), lambda b: (b, 0, 0, 0))
    rows = N_HEADS_A * DEC_SEQ
    return pl.pallas_call(
        _attn_sample_kernel,
        grid=(DEC_BATCH,),
        in_specs=[new(D_A), new(D_A), new(D_A), stored, stored, _resident(cnt_c.shape), _resident(cnt_n.shape),
                  new(D_BQ), new(D_BKV), new(D_BKV), cache(WIN_B, D_BKV), cache(WIN_B, D_BKV),
                  _resident((rows, NKB_PAD)), _resident((rows, LANES))],
        out_specs=pl.BlockSpec((DEC_SEQ, D_A + D_BQ), lambda b: (b, 0)),
        out_shape=jax.ShapeDtypeStruct((DEC_BATCH * DEC_SEQ, D_A + D_BQ), F32),
        scratch_shapes=[pltpu.VMEM((NKB_PAD, D_BKV), BF16), pltpu.VMEM((NKB_PAD, D_BKV), BF16)],
        compiler_params=_params(1),
        name="attn_sample",
    )(qa, kan, van, cache_ak, cache_av, cnt_c, cnt_n, qb, kbn, vbn, cache_bk, cache_bv, mask_b, sink_col)


def _attn_out_prompt_kernel(o1_ref, l1_ref, o4_ref, l4_ref, o16_ref, l16_ref, ob_ref, y_ref, w_ref, g_ref, b_ref,
                            out_ref, slab_ref, oa_ref):
    tm = y_ref.shape[0]
    n_chunks = D_A // LANES
    for i, (dil, src) in enumerate(((4, o4_ref), (4, l4_ref), (16, o16_ref), (16, l16_ref))):
        for r in range(dil):
            for c in range(n_chunks):
                slab_ref[i * n_chunks + c, pl.ds(r, tm // dil, stride=dil), :] = src[r, :, c * LANES:(c + 1) * LANES]
    for c in range(n_chunks):
        sl = slice(c * LANES, (c + 1) * LANES)
        l1, l4, l16 = l1_ref[:, sl], slab_ref[n_chunks + c], slab_ref[3 * n_chunks + c]
        m = jnp.maximum(jnp.maximum(l1, l4), l16)
        e1, e4, e16 = jnp.exp(l1 - m), jnp.exp(l4 - m), jnp.exp(l16 - m)
        oa = (e1 * o1_ref[:, sl] + e4 * slab_ref[c] + e16 * slab_ref[2 * n_chunks + c]) * (1.0 / (e1 + e4 + e16))
        oa_ref[:, sl] = oa.astype(BF16)
    mix = _dot(oa_ref[...], w_ref[0:D_A, :]) + _dot(ob_ref[...], w_ref[D_A:, :])
    out_ref[...] = _layer_norm(DN_ALPHA * y_ref[...] + mix, g_ref[...], b_ref[...])


def _attn_out_prompt(pats, ob, y, w, g, b, seq):
    n = y.shape[0]
    tm = ROW_TILE
    tps = seq // tm
    half = pl.BlockSpec((tm, D_A), lambda i: (i, 0))
    full = pl.BlockSpec((tm, D_MODEL), lambda i: (i, 0))
    planes = lambda dil: pl.BlockSpec((None, dil, tm // dil, D_A), lambda i: (i // tps, 0, i % tps, 0))
    (o1, l1), (o4, l4), (o16, l16) = pats
    return pl.pallas_call(
        _attn_out_prompt_kernel,
        grid=(n // tm,),
        in_specs=[half, half, planes(4), planes(4), planes(16), planes(16), half, full,
                  _resident((D_MODEL, D_MODEL)), _resident((1, D_MODEL)), _resident((1, D_MODEL))],
        out_specs=full,
        out_shape=jax.ShapeDtypeStruct((n, D_MODEL), F32),
        scratch_shapes=[pltpu.VMEM((4 * D_A // LANES, tm, LANES), F32), pltpu.VMEM((tm, D_A), BF16)],
        compiler_params=_params(1),
        name="attn_out_prompt",
    )(o1, l1, o4, l4, o16, l16, ob, y, w, g, b)


def _mix_out_kernel(o_ref, y_ref, w_ref, g_ref, b_ref, out_ref):
    mix = _dot(o_ref[...].astype(BF16), w_ref[...])
    out_ref[...] = _layer_norm(DN_ALPHA * y_ref[...] + mix, g_ref[...], b_ref[...])


def _attn_out_sample(o, y, w, g, b):
    n = y.shape[0]
    tm = min(ROW_TILE, n)
    full = pl.BlockSpec((tm, D_MODEL), lambda i: (i, 0))
    return pl.pallas_call(
        _mix_out_kernel,
        grid=(n // tm,),
        in_specs=[full, full, _resident((D_MODEL, D_MODEL)), _resident((1, D_MODEL)), _resident((1, D_MODEL))],
        out_specs=full,
        out_shape=jax.ShapeDtypeStruct((n, D_MODEL), F32),
        compiler_params=_params(1),
        name="attn_out_sample",
    )(o, y, w, g, b)


def _ssm_discretize(lam_re, lam_im, log_dt, b_re, b_im):
    dt = jnp.exp(log_dt.astype(F32))[:, None]
    lr, li = lam_re.astype(F32), lam_im.astype(F32)
    mag = jnp.exp(lr * dt)
    ab_re, ab_im = mag * jnp.cos(li * dt), mag * jnp.sin(li * dt)
    nr, ni = ab_re - 1.0, ab_im
    den = lr * lr + li * li
    fr, fi = (nr * lr + ni * li) / den, (ni * lr - nr * li) / den
    bb_re = fr[..., None] * b_re - fi[..., None] * b_im
    bb_im = fr[..., None] * b_im + fi[..., None] * b_re
    return ab_re, ab_im, bb_re, bb_im


def _ssm_matrices(bb_re, bb_im, c_re, c_im):
    gpc = LANES // SSM_GROUP
    eye = jnp.eye(gpc, dtype=F32)

    def in_blocks(bb):
        a = bb.reshape(N_LCHUNK, gpc, SSM_STATE, SSM_GROUP)
        return jnp.einsum("jgpn,gh->jgnhp", a, eye).reshape(N_LCHUNK, LANES, ST_CHUNK)

    def out_blocks(cc):
        a = cc.reshape(N_LCHUNK, gpc, SSM_GROUP, SSM_STATE)
        return jnp.einsum("jgnp,gh->jgphn", a, eye).reshape(N_LCHUNK, ST_CHUNK, LANES)

    bmat = jnp.concatenate([in_blocks(bb_re), in_blocks(bb_im)], -1)
    cmat = jnp.concatenate([out_blocks(c_re), -out_blocks(c_im)], 1)
    return bmat, cmat


CHUNK = 8


def _dot3(a, b):
    a_hi, b_hi = a.astype(BF16), b.astype(BF16)
    a_lo, b_lo = (a - a_hi.astype(F32)).astype(BF16), (b - b_hi.astype(F32)).astype(BF16)
    return _dot(a_hi, b_hi) + _dot(a_hi, b_lo) + _dot(a_lo, b_hi)


def _chunk_weights_kernel(ar_ref, ai_ref, acr_ref, aci_ref, bmat_ref, cmat_ref,
                          we_ref, ws_ref, wi_ref, a8r_ref, a8i_ref):
    def powers(r, i, n):
        out = [(jnp.ones_like(r), jnp.zeros_like(r))]
        for _ in range(n):
            out.append(_cmul(out[-1][0], out[-1][1], r, i))
        return out

    row_pow = powers(ar_ref[...], ai_ref[...], CHUNK)
    col_pow = powers(acr_ref[...], aci_ref[...], CHUNK)
    b_re, b_im = bmat_ref[:, 0:ST_CHUNK], bmat_ref[:, ST_CHUNK:]
    c_re, c_im = cmat_ref[0:ST_CHUNK, :], -cmat_ref[ST_CHUNK:, :]

    def scaled_b(power):
        pr, pi = row_pow[power]
        return jnp.concatenate([b_re * pr - b_im * pi, b_re * pi + b_im * pr], axis=1)

    for k in range(CHUNK):
        cols = slice(k * LANES, (k + 1) * LANES)
        we_ref[cols, :] = scaled_b(CHUNK - 1 - k).astype(BF16)
        qr, qi = col_pow[k + 1]
        ws_ref[0:ST_CHUNK, cols] = (c_re * qr - c_im * qi).astype(BF16)
        ws_ref[ST_CHUNK:, cols] = (-(c_re * qi + c_im * qr)).astype(BF16)
    taps = [_dot3(scaled_b(tau), cmat_ref[...]).astype(BF16) for tau in range(CHUNK)]
    zero = jnp.zeros((LANES, LANES), BF16)
    for k_in in range(CHUNK):
        for k_out in range(CHUNK):
            wi_ref[k_in * LANES:(k_in + 1) * LANES, k_out * LANES:(k_out + 1) * LANES] = (
                taps[k_out - k_in] if k_out >= k_in else zero)
    a8r_ref[...], a8i_ref[...] = row_pow[CHUNK]


def _chunk_weights(a_re, a_im, bmat, cmat):
    cols = lambda a: jnp.broadcast_to(a.reshape(N_LCHUNK, ST_CHUNK, 1), (N_LCHUNK, ST_CHUNK, LANES))
    per_j = lambda r, c: pl.BlockSpec((None, r, c), lambda j: (j, 0, 0))
    wide = 2 * ST_CHUNK
    return pl.pallas_call(
        _chunk_weights_kernel,
        grid=(N_LCHUNK,),
        in_specs=[per_j(1, ST_CHUNK), per_j(1, ST_CHUNK), per_j(ST_CHUNK, LANES), per_j(ST_CHUNK, LANES),
                  per_j(LANES, wide), per_j(wide, LANES)],
        out_specs=[per_j(CHUNK * LANES, wide), per_j(wide, CHUNK * LANES), per_j(CHUNK * LANES, CHUNK * LANES),
                   per_j(1, ST_CHUNK), per_j(1, ST_CHUNK)],
        out_shape=[jax.ShapeDtypeStruct((N_LCHUNK, CHUNK * LANES, wide), BF16),
                   jax.ShapeDtypeStruct((N_LCHUNK, wide, CHUNK * LANES), BF16),
                   jax.ShapeDtypeStruct((N_LCHUNK, CHUNK * LANES, CHUNK * LANES), BF16),
                   jax.ShapeDtypeStruct((N_LCHUNK, 1, ST_CHUNK), F32),
                   jax.ShapeDtypeStruct((N_LCHUNK, 1, ST_CHUNK), F32)],
        compiler_params=_params(1),
        name="ssm_chunk_weights",
    )(a_re, a_im, cols(a_re), cols(a_im), bmat, cmat)


SEG_TILE = ROW_TILE // N_SEG


def _ssm_in_prompt_kernel(x_ref, w_ref, o_ref):
    x = x_ref[...].reshape(N_SEG * SEG_TILE, D_MODEL)
    u = _dot(x.astype(BF16), w_ref[...])
    for s in range(N_SEG):
        for c in range(N_LCHUNK):
            o_ref[c, pl.ds(s, SEG_TILE, stride=N_SEG), :] = u[s * SEG_TILE:(s + 1) * SEG_TILE, c * LANES:(c + 1) * LANES]


def _ssm_in_prompt(y, w):
    return pl.pallas_call(
        _ssm_in_prompt_kernel,
        grid=(BATCH, SEG_LEN // SEG_TILE),
        in_specs=[pl.BlockSpec((None, N_SEG, SEG_TILE, D_MODEL), lambda b, t: (b, 0, t, 0)),
                  _resident((D_MODEL, D_MODEL))],
        out_specs=pl.BlockSpec((None, N_LCHUNK, ROW_TILE, LANES), lambda b, t: (b, 0, t, 0)),
        out_shape=jax.ShapeDtypeStruct((BATCH, N_LCHUNK, SEQ, LANES), F32),
        compiler_params=_params(2),
        name="ssm_in_prompt",
    )(y.reshape(BATCH, N_SEG, SEG_LEN, D_MODEL), w)


def _ssm_in_sample_kernel(x_ref, w_ref, o_ref, slab_ref):
    u = _dot(x_ref[...].astype(BF16), w_ref[...])
    for c in range(N_LCHUNK):
        slab_ref[c] = u[:, c * LANES:(c + 1) * LANES]
    for l in range(DEC_SEQ):
        for c in range(N_LCHUNK):
            o_ref[c, l * DEC_BATCH:(l + 1) * DEC_BATCH, :] = slab_ref[c, pl.ds(l, DEC_BATCH, stride=DEC_SEQ), :]


def _ssm_in_sample(y, w):
    n = DEC_BATCH * DEC_SEQ
    return pl.pallas_call(
        _ssm_in_sample_kernel,
        grid=(1,),
        in_specs=[_resident((n, D_MODEL)), _resident((D_MODEL, D_MODEL))],
        out_specs=pl.BlockSpec((N_LCHUNK, n, LANES), lambda i: (0, 0, 0)),
        out_shape=jax.ShapeDtypeStruct((N_LCHUNK, n, LANES), F32),
        scratch_shapes=[pltpu.VMEM((N_LCHUNK, n, LANES), F32)],
        compiler_params=_params(1),
        name="ssm_in_sample",
    )(y, w)


def _cmul(ar, ai, br, bi):
    return ar * br - ai * bi, ar * bi + ai * br


def _scan_prompt_kernel(u_ref, we_ref, ws_ref, wi_ref, a8r_ref, a8i_ref, d_ref, h0r_ref, h0i_ref,
                        y_ref, hnr_ref, hni_ref, e_s, hs_s):
    n_chunks = u_ref.shape[0]
    rows = n_chunks * N_SEG
    u_flat = jnp.concatenate([u_ref[:, k].reshape(rows, LANES) for k in range(CHUNK)], axis=1)
    ub = u_flat.astype(BF16)
    e_s[...] = _dot(ub, we_ref[...])
    a8r1, a8i1 = a8r_ref[...], a8i_ref[...]
    a8r = jnp.broadcast_to(a8r1, (N_SEG, ST_CHUNK))
    a8i = jnp.broadcast_to(a8i1, (N_SEG, ST_CHUNK))

    def advance(row, hr, hi):
        er = e_s[pl.ds(row, N_SEG), 0:ST_CHUNK]
        ei = e_s[pl.ds(row, N_SEG), ST_CHUNK:2 * ST_CHUNK]
        return a8r * hr - a8i * hi + er, a8r * hi + a8i * hr + ei

    def pass1(c, carry):
        return advance(pl.multiple_of(c * N_SEG, N_SEG), *carry)

    zero = jnp.zeros((N_SEG, ST_CHUNK), F32)
    er, ei = lax.fori_loop(0, n_chunks, pass1, (zero, zero), unroll=8)

    pr, pi = a8r1, a8i1
    for _ in range(int(math.log2(n_chunks))):
        pr, pi = _cmul(pr, pi, pr, pi)
    hr, hi = h0r_ref[...], h0i_ref[...]
    starts_r, starts_i = [], []
    for s in range(N_SEG):
        starts_r.append(hr)
        starts_i.append(hi)
        gr, gi = _cmul(pr, pi, hr, hi)
        hr, hi = gr + er[s:s + 1], gi + ei[s:s + 1]
    hnr_ref[...] = hr
    hni_ref[...] = hi
    init = (jnp.concatenate(starts_r, 0), jnp.concatenate(starts_i, 0))

    def pass2(c, carry):
        row = pl.multiple_of(c * N_SEG, N_SEG)
        hs_s[pl.ds(row, N_SEG), 0:ST_CHUNK] = carry[0]
        hs_s[pl.ds(row, N_SEG), ST_CHUNK:2 * ST_CHUNK] = carry[1]
        return advance(row, *carry)

    lax.fori_loop(0, n_chunks, pass2, init, unroll=8)
    y = _dot(hs_s[...].astype(BF16), ws_ref[...]) + _dot(ub, wi_ref[...]) + d_ref[...] * u_flat
    for k in range(CHUNK):
        y_ref[:, k] = y[:, k * LANES:(k + 1) * LANES].reshape(n_chunks, N_SEG, LANES)


def _scan_prompt(u, weights, d_skip, h0r, h0i):
    bsz, _, seq, _ = u.shape
    n_chunks = seq // (CHUNK * N_SEG)
    rows = n_chunks * N_SEG
    wide = 2 * ST_CHUNK
    split = lambda a: a.reshape(bsz, N_LCHUNK, n_chunks, CHUNK, N_SEG, LANES)
    chunk = pl.BlockSpec((None, None, n_chunks, CHUNK, N_SEG, LANES), lambda j, b: (b, j, 0, 0, 0, 0))
    per_j = lambda r, c: pl.BlockSpec((None, r, c), lambda j, b: (j, 0, 0))
    state = pl.BlockSpec((None, 1, ST_CHUNK), lambda j, b: (b, 0, j))
    y, hr, hi = pl.pallas_call(
        _scan_prompt_kernel,
        grid=(N_LCHUNK, bsz),
        in_specs=[chunk, per_j(CHUNK * LANES, wide), per_j(wide, CHUNK * LANES), per_j(CHUNK * LANES, CHUNK * LANES),
                  per_j(1, ST_CHUNK), per_j(1, ST_CHUNK), per_j(1, CHUNK * LANES), state, state],
        out_specs=[chunk, state, state],
        out_shape=[jax.ShapeDtypeStruct((bsz, N_LCHUNK, n_chunks, CHUNK, N_SEG, LANES), F32),
                   jax.ShapeDtypeStruct((bsz, 1, N_STATE), F32), jax.ShapeDtypeStruct((bsz, 1, N_STATE), F32)],
        scratch_shapes=[pltpu.VMEM((rows, wide), F32), pltpu.VMEM((rows, wide), F32)],
        compiler_params=_params(2),
        name="ssm_scan_prompt",
    )(split(u), *weights, d_skip, h0r, h0i)
    return y.reshape(u.shape), hr, hi


def _scan_sample_kernel(u_ref, bmat_ref, cmat_ref, are_ref, aim_ref, d_ref, h0r_ref, h0i_ref,
                        y_ref, hnr_ref, hni_ref):
    a_re, a_im = are_ref[...], aim_ref[...]
    hr, hi = h0r_ref[...].T, h0i_ref[...].T
    for l in range(DEC_SEQ):
        rows = slice(l * DEC_BATCH, (l + 1) * DEC_BATCH)
        u = u_ref[rows, :]
        bu = _dot(u.astype(BF16), bmat_ref[...])
        gr, gi = _cmul(a_re, a_im, hr, hi)
        hr, hi = gr + bu[:, :ST_CHUNK], gi + bu[:, ST_CHUNK:]
        h = jnp.concatenate([hr, hi], axis=1).astype(BF16)
        y_ref[rows, :] = _dot(h, cmat_ref[...]) + d_ref[...] * u
    hnr_ref[...] = hr.T
    hni_ref[...] = hi.T


def _scan_sample(u, bmat, cmat, a_re, a_im, d_skip, h0r, h0i):
    n = DEC_SEQ * DEC_BATCH
    chunk = pl.BlockSpec((None, n, LANES), lambda j: (j, 0, 0))
    per_j = lambda r, c: pl.BlockSpec((None, r, c), lambda j: (j, 0, 0))
    state = pl.BlockSpec((ST_CHUNK, DEC_BATCH), lambda j: (j, 0))
    return pl.pallas_call(
        _scan_sample_kernel,
        grid=(N_LCHUNK,),
        in_specs=[chunk, per_j(LANES, 2 * ST_CHUNK), per_j(2 * ST_CHUNK, LANES), per_j(1, ST_CHUNK),
                  per_j(1, ST_CHUNK), per_j(1, LANES), state, state],
        out_specs=[chunk, state, state],
        out_shape=[jax.ShapeDtypeStruct((N_LCHUNK, n, LANES), F32),
                   jax.ShapeDtypeStruct((N_STATE, DEC_BATCH), F32), jax.ShapeDtypeStruct((N_STATE, DEC_BATCH), F32)],
        compiler_params=_params(1),
        name="ssm_scan_sample",
    )(u, bmat, cmat, a_re, a_im, d_skip, h0r, h0i)


def _ssm_out_kernel(sample, s_ref, y_ref, wglu_ref, bglu_ref, wout_ref, g_ref, b_ref, out_ref, z_ref):
    n_rows = z_ref.shape[1]
    if sample:
        parts = [(pl.ds(l, DEC_BATCH, stride=DEC_SEQ), slice(l * DEC_BATCH, (l + 1) * DEC_BATCH))
                 for l in range(DEC_SEQ)]
    else:
        parts = [(slice(s * SEG_TILE, (s + 1) * SEG_TILE), pl.ds(s, SEG_TILE, stride=N_SEG)) for s in range(N_SEG)]
    for c in range(N_LCHUNK):
        for tok_rows, slab_rows in parts:
            z_ref[c, tok_rows, :] = s_ref[c, slab_rows, :]
    z = jax.nn.gelu(jnp.concatenate([z_ref[c] for c in range(N_LCHUNK)], axis=1))
    gate = jax.nn.sigmoid(_dot(z.astype(BF16), wglu_ref[...]) + bglu_ref[...])
    mix = _dot((z * gate).astype(BF16), wout_ref[...])
    res = y_ref[...].reshape(n_rows, D_MODEL)
    out = _layer_norm(DN_ALPHA * res + mix, g_ref[...], b_ref[...])
    out_ref[...] = out.reshape(out_ref.shape)


def _ssm_out(s, y, s_spec, y_spec, grid, rows, sample, w_glu, b_glu, w_out, g, b, name):
    return pl.pallas_call(
        functools.partial(_ssm_out_kernel, sample),
        grid=grid,
        in_specs=[s_spec, y_spec, _resident((D_MODEL, D_MODEL)), _resident((1, D_MODEL)),
                  _resident((D_MODEL, D_MODEL)), _resident((1, D_MODEL)), _resident((1, D_MODEL))],
        out_specs=y_spec,
        out_shape=jax.ShapeDtypeStruct(y.shape, F32),
        scratch_shapes=[pltpu.VMEM((N_LCHUNK, rows, LANES), F32)],
        compiler_params=_params(len(grid)),
        name=name,
    )(s, y, w_glu, b_glu, w_out, g, b)


def _attn_prompt(yp, w_in, sinks, w_out, g, b):
    cos_p, sin_p = _rope_tables(jnp.arange(SEQ))
    tiles_per_seq = SEQ // ROW_TILE
    qa, ka, va, qb, kb, vb, *extra = _attn_proj(
        yp, w_in, cos_p, sin_p, lambda i: i % tiles_per_seq, BF16, prompt_seq=(BATCH, SEQ))
    dilated, tails = extra[:6], extra[6:]
    seq3 = lambda a: a.reshape(BATCH, SEQ, a.shape[-1])
    plane1 = lambda a: a.reshape(BATCH, 1, SEQ, D_A)
    o1, l1 = _band_a(plane1(qa), plane1(ka), plane1(va))
    pats = [(o1.reshape(BATCH * SEQ, D_A), l1.reshape(BATCH * SEQ, D_A))]
    for i in range(len(DILATIONS) - 1):
        pats.append(_band_a(*dilated[3 * i:3 * i + 3]))
    ob = _band_b(seq3(qb), seq3(kb), seq3(vb), sinks)
    yp = _attn_out_prompt(pats, ob, yp, w_out, g, b, SEQ)
    heads = lambda a, nh: jnp.transpose(a.reshape(BATCH, nh, HEAD_DIM, a.shape[-1]), (0, 3, 1, 2))[None]
    prompt_cache = (heads(tails[0], N_HEADS_A), heads(tails[1], N_HEADS_A),
                    heads(tails[2], N_KV_B), heads(tails[3], N_KV_B))
    return yp, prompt_cache


def _attn_sample_path(ys, cache_ak, cache_av, cache_bk, cache_bv, w_in, sinks, w_out, g, b):
    cos_s, sin_s = _rope_tables(PAST_LEN + jnp.arange(DEC_SEQ))
    reps = DEC_BATCH * DEC_SEQ // DEC_SEQ
    cos_s, sin_s = jnp.tile(cos_s, (reps, 1)), jnp.tile(sin_s, (reps, 1))
    qa, ka, va, qb, kb, vb = _attn_proj(ys, w_in, cos_s, sin_s, lambda i: i, F32)
    stored = lambda a: jnp.transpose(a, (0, 2, 3, 1))
    o = _attn_sample(qa, ka, va, stored(cache_ak), stored(cache_av),
                     qb, kb, vb, cache_bk.reshape(DEC_BATCH, WIN_B, D_BKV), cache_bv.reshape(DEC_BATCH, WIN_B, D_BKV),
                     sinks)
    ys = _attn_out_sample(o, ys, w_out, g, b)
    new = lambda a, nh: a.reshape(1, DEC_BATCH, DEC_SEQ, nh, HEAD_DIM)
    sample_cache = (new(ka, N_HEADS_A), new(va, N_HEADS_A), new(kb, N_KV_B), new(vb, N_KV_B))
    return ys, sample_cache


def _attn_layer(yp, ys, cache_ak, cache_av, cache_bk, cache_bv, w_in, sinks, w_out, g, b):
    w_in = w_in.astype(BF16)
    w_out = w_out.astype(BF16)
    yp, prompt_cache = _attn_prompt(yp, w_in, sinks, w_out, g, b)
    ys, sample_cache = _attn_sample_path(ys, cache_ak, cache_av, cache_bk, cache_bv, w_in, sinks, w_out, g, b)
    return yp, ys, prompt_cache, sample_cache


def _ssm_layer(yp, ys, state_re, state_im, w_in, lam_re, lam_im, log_dt, b_re, b_im, c_re, c_im, d_skip,
               w_glu, b_glu, w_out, g, b):
    w_in, w_glu, w_out = w_in.astype(BF16), w_glu.astype(BF16), w_out.astype(BF16)
    b_glu = b_glu.reshape(1, D_MODEL)
    mats = _ssm_prepare(lam_re, lam_im, log_dt, b_re, b_im, c_re, c_im, d_skip)
    yp, prompt_state = _ssm_prompt(yp, w_in, mats, w_glu, b_glu, w_out, g, b)
    ys, sample_state = _ssm_sample(ys, state_re, state_im, w_in, mats, w_glu, b_glu, w_out, g, b)
    return yp, ys, prompt_state, sample_state


def _ssm_prepare(lam_re, lam_im, log_dt, b_re, b_im, c_re, c_im, d_skip):
    ab_re, ab_im, bb_re, bb_im = _ssm_discretize(lam_re, lam_im, log_dt, b_re, b_im)
    bmat, cmat = _ssm_matrices(bb_re, bb_im, c_re, c_im)
    a_re = ab_re.reshape(N_LCHUNK, 1, ST_CHUNK)
    a_im = ab_im.reshape(N_LCHUNK, 1, ST_CHUNK)
    d3 = d_skip.astype(F32).reshape(N_LCHUNK, 1, LANES)
    prompt = (_chunk_weights(a_re, a_im, bmat, cmat), jnp.tile(d3, (1, 1, CHUNK)))
    sample = (bmat.astype(BF16), cmat.astype(BF16), a_re, a_im, d3)
    return prompt, sample


def _ssm_prompt(yp, w_in, mats, w_glu, b_glu, w_out, g, b):
    weights, d_tiled = mats[0]
    up = _ssm_in_prompt(yp, w_in)
    zero = jnp.zeros((BATCH, 1, N_STATE), F32)
    sp, pr, pi = _scan_prompt(up, weights, d_tiled, zero, zero)
    yp = _ssm_out(sp, yp.reshape(BATCH, N_SEG, SEG_LEN, D_MODEL),
                  pl.BlockSpec((None, N_LCHUNK, ROW_TILE, LANES), lambda bb, t: (bb, 0, t, 0)),
                  pl.BlockSpec((None, N_SEG, SEG_TILE, D_MODEL), lambda bb, t: (bb, 0, t, 0)),
                  (BATCH, SEG_LEN // SEG_TILE), ROW_TILE, False, w_glu, b_glu, w_out, g, b,
                  "ssm_out_prompt").reshape(BATCH * SEQ, D_MODEL)
    prompt_state = (pr.reshape(1, BATCH, N_SSM_GROUPS, SSM_STATE), pi.reshape(1, BATCH, N_SSM_GROUPS, SSM_STATE))
    return yp, prompt_state


def _ssm_sample(ys, state_re, state_im, w_in, mats, w_glu, b_glu, w_out, g, b):
    bmat, cmat, a_re, a_im, d3 = mats[1]
    us = _ssm_in_sample(ys, w_in)
    stored = lambda a: jnp.transpose(a, (1, 2, 0)).reshape(N_STATE, DEC_BATCH)
    logical = lambda a: jnp.transpose(a.reshape(N_SSM_GROUPS, SSM_STATE, DEC_BATCH), (2, 0, 1))[None]
    ss, sr, si = _scan_sample(us, bmat, cmat, a_re, a_im, d3, stored(state_re), stored(state_im))
    n = DEC_BATCH * DEC_SEQ
    ys = _ssm_out(ss, ys,
                  pl.BlockSpec((N_LCHUNK, n, LANES), lambda i: (0, 0, 0)),
                  pl.BlockSpec((n, D_MODEL), lambda i: (0, 0)),
                  (1,), n, True, w_glu, b_glu, w_out, g, b, "ssm_out_sample")
    return ys, (logical(sr), logical(si))


def kernel(x_prompt, x_sample, cache_a_k, cache_a_v, cache_b_k, cache_b_v, state_c_re, state_c_im, ln_g, ln_b, ffn_w_gate, ffn_w_up, ffn_w_down, attn_w_in, attn_sinks, attn_w_out, ssm_w_in, ssm_lambda_re, ssm_lambda_im, ssm_log_dt, ssm_b_re, ssm_b_im, ssm_c_re, ssm_c_im, ssm_d, ssm_w_glu, ssm_b_glu, ssm_w_out):
    yp = x_prompt.reshape(BATCH * SEQ, D_MODEL)
    ys = x_sample.reshape(DEC_BATCH * DEC_SEQ, D_MODEL)
    ln = lambda l, k: (ln_g[l, k].reshape(1, D_MODEL), ln_b[l, k].reshape(1, D_MODEL))

    wg, wu, wd = ffn_w_gate.astype(BF16), ffn_w_up.astype(BF16), ffn_w_down.astype(BF16)

    def ffn_pair(yp, ys, l, k, ln_idx):
        g, b = ln(l, ln_idx)
        return _ffn(yp, wg, wu, wd, g, b, l, k), _ffn(ys, wg, wu, wd, g, b, l, k)

    yp, ys = ffn_pair(yp, ys, 0, 0, 0)
    yp, ys, p_cache, s_cache = _attn_layer(yp, ys, cache_a_k[0], cache_a_v[0], cache_b_k[0], cache_b_v[0],
                                           attn_w_in[0], attn_sinks[0], attn_w_out[0], *ln(0, 1))
    yp, ys = ffn_pair(yp, ys, 0, 1, 2)
    yp, ys = ffn_pair(yp, ys, 1, 0, 0)
    yp, ys, p_state, s_state = _ssm_layer(yp, ys, state_c_re[0], state_c_im[0], ssm_w_in[0], ssm_lambda_re[0],
                                          ssm_lambda_im[0], ssm_log_dt[0], ssm_b_re[0], ssm_b_im[0], ssm_c_re[0],
                                          ssm_c_im[0], ssm_d[0], ssm_w_glu[0], ssm_b_glu[0], ssm_w_out[0], *ln(1, 1))
    yp, ys = ffn_pair(yp, ys, 1, 1, 2)
    return (yp.reshape(BATCH, SEQ, D_MODEL), ys.reshape(DEC_BATCH, DEC_SEQ, D_MODEL),
            *p_cache, *p_state, *s_cache, *s_state)
```

```python
import functools
import math

import jax
import jax.numpy as jnp
from jax import lax
from jax.experimental import pallas as pl
from jax.experimental.pallas import tpu as pltpu

F32 = jnp.float32
BF16 = jnp.bfloat16

D_MODEL = 1024
BATCH = 4
SEQ = 4096
DEPTH = 2
DEC_BATCH = 128
DEC_SEQ = 8
PAST_LEN = 16384
HEAD_DIM = 64
N_HEADS_A = 8
DILATIONS = (1, 4, 16)
WIN_A = 2048
N_HEADS_B = 8
N_KV_B = 2
WIN_B = 128
ROPE_THETA = 10000.0
D_A = N_HEADS_A * HEAD_DIM
D_BQ = N_HEADS_B * HEAD_DIM
D_BKV = N_KV_B * HEAD_DIM
D_IN_ATTN = 3 * D_A + D_BQ + 2 * D_BKV
SSM_GROUP = 16
N_SSM_GROUPS = D_MODEL // SSM_GROUP
SSM_STATE = 64
N_STATE = N_SSM_GROUPS * SSM_STATE
D_FF = 2816
DN_ALPHA = (2 * DEPTH) ** 0.25
FFN_RES = 0.5
LN_EPS = 1e-5
ATTN_SCALE = HEAD_DIM ** -0.5

LANES = 128
SUBLANES = 8
MXU_N = 256
VMEM_LIMIT = 56 * 1024 * 1024

ROW_TILE = 512
FFN_ROW_TILE = 1024
FF_CHUNK = MXU_N
TQ = 128
N_SEG = SUBLANES
SEG_LEN = SEQ // N_SEG
N_LCHUNK = D_MODEL // LANES
ST_CHUNK = N_STATE // N_LCHUNK
NK_PAD = WIN_A + LANES
NKB_PAD = 2 * WIN_B

NEG_INF = float("-inf")


def _params(n_axes, vmem=VMEM_LIMIT):
    return pltpu.CompilerParams(dimension_semantics=("arbitrary",) * n_axes, vmem_limit_bytes=vmem)


def _resident(shape):
    return pl.BlockSpec(shape, lambda *_: (0,) * len(shape), pipeline_mode=pl.Buffered(1))


def _layer_norm(x, g, b):
    mu = jnp.mean(x, -1, keepdims=True)
    xc = x - mu
    var = jnp.mean(xc * xc, -1, keepdims=True)
    return xc * lax.rsqrt(var + LN_EPS) * g + b


def _dot(a, b):
    return jnp.dot(a, b, preferred_element_type=F32)


def _dot_nt(a, b):
    return lax.dot_general(a, b, (((1,), (1,)), ((), ())), preferred_element_type=F32)


def _ffn_kernel(x_ref, wg_ref, wu_ref, wd_ref, g_ref, b_ref, o_ref, h_ref):
    x = x_ref[...]
    xb = x.astype(BF16)
    for c in range(D_FF // FF_CHUNK):
        sl = slice(c * FF_CHUNK, (c + 1) * FF_CHUNK)
        gate = _dot(xb, wg_ref[:, sl])
        up = _dot(xb, wu_ref[:, sl])
        h_ref[:, sl] = (gate * jax.nn.sigmoid(gate) * up).astype(BF16)
    y = DN_ALPHA * x + FFN_RES * _dot(h_ref[...], wd_ref[...])
    o_ref[...] = _layer_norm(y, g_ref[...], b_ref[...])


def _ffn(x, wg, wu, wd, g, b, layer=0, which=0):
    n = x.shape[0]
    tm = min(FFN_ROW_TILE, n)
    row = pl.BlockSpec((tm, D_MODEL), lambda i: (i, 0))
    if wg.ndim == 4:
        pick = lambda r, c: pl.BlockSpec((None, None, r, c), lambda i: (layer, which, 0, 0),
                                         pipeline_mode=pl.Buffered(1))
    else:
        pick = lambda r, c: _resident((r, c))
    return pl.pallas_call(
        _ffn_kernel,
        grid=(n // tm,),
        in_specs=[row, pick(D_MODEL, D_FF), pick(D_MODEL, D_FF), pick(D_FF, D_MODEL),
                  _resident((1, D_MODEL)), _resident((1, D_MODEL))],
        out_specs=row,
        out_shape=jax.ShapeDtypeStruct((n, D_MODEL), F32),
        scratch_shapes=[pltpu.VMEM((tm, D_FF), BF16)],
        compiler_params=_params(1),
        name="ffn",
    )(x, wg, wu, wd, g, b)


def _rope_tables(pos):
    half = HEAD_DIM // 2
    inv_freq = ROPE_THETA ** (-jnp.arange(half, dtype=F32) / half)
    ang = pos.astype(F32)[:, None] * inv_freq[None, :]
    cos, sin = jnp.cos(ang), jnp.sin(ang)
    cos_t = jnp.concatenate([cos, cos, cos, cos], -1)
    sin_t = jnp.concatenate([-sin, sin, -sin, sin], -1)
    return cos_t, sin_t


def _attn_proj_kernel(tiles_per_seq, x_ref, w_ref, cos_ref, sin_ref, qa_ref, ka_ref, va_ref, qb_ref, kb_ref, vb_ref,
                      *extra):
    xb = x_ref[...].astype(BF16)
    slab_ref = extra[-1] if extra else None
    dilated = extra[:6]
    n_chunks = D_A // LANES
    tm = x_ref.shape[0]

    def keep(tensor, c, val):
        if slab_ref is not None:
            slab_ref[tensor * n_chunks + c] = val
    cos = cos_ref[...]
    sin = sin_ref[...]
    lane = lax.broadcasted_iota(jnp.int32, cos.shape, 1)
    first_half = (lane & (HEAD_DIM // 2)) == 0

    def rope(z):
        rot = jnp.where(first_half, pltpu.roll(z, LANES - HEAD_DIM // 2, 1), pltpu.roll(z, HEAD_DIM // 2, 1))
        return z * cos + rot * sin

    def project(col0, ncols):
        return _dot(xb, w_ref[:, col0:col0 + ncols])

    def rope_chunks(z):
        return [rope(z[:, c * LANES:(c + 1) * LANES]) for c in range(z.shape[1] // LANES)]

    col = 0
    for c, r in enumerate(rope_chunks(project(col, D_A))):
        r = r * ATTN_SCALE
        qa_ref[:, c * LANES:(c + 1) * LANES] = r.astype(qa_ref.dtype)
        keep(0, c, r)
    col += D_A
    for c, r in enumerate(rope_chunks(project(col, D_A))):
        ka_ref[:, c * LANES:(c + 1) * LANES] = r.astype(ka_ref.dtype)
        keep(1, c, r)
    col += D_A
    z = project(col, D_A)
    va_ref[...] = z.astype(va_ref.dtype)
    for c in range(n_chunks):
        keep(2, c, z[:, c * LANES:(c + 1) * LANES])
    col += D_A
    if extra:
        kat_ref, vat_ref, kbt_ref, vbt_ref = extra[6:10]
        tile_in_seq = pl.program_id(0) % tiles_per_seq

        @pl.when(tile_in_seq >= tiles_per_seq - WIN_A // tm)
        def _():
            for c in range(n_chunks):
                kat_ref[c * LANES:(c + 1) * LANES, :] = slab_ref[n_chunks + c].T
                vat_ref[c * LANES:(c + 1) * LANES, :] = slab_ref[2 * n_chunks + c].T

        for i, dil in enumerate(DILATIONS[1:]):
            for tensor in range(3):
                out_ref = dilated[3 * i + tensor]
                for r in range(dil):
                    for c in range(n_chunks):
                        rows = slab_ref[tensor * n_chunks + c, pl.ds(r, tm // dil, stride=dil), :]
                        out_ref[r, :, c * LANES:(c + 1) * LANES] = rows.astype(out_ref.dtype)
    for c, r in enumerate(rope_chunks(project(col, D_BQ))):
        qb_ref[:, c * LANES:(c + 1) * LANES] = (r * ATTN_SCALE).astype(qb_ref.dtype)
    col += D_BQ
    z = project(col, 2 * D_BKV)
    r = rope(z[:, :D_BKV])
    kb_ref[...] = r.astype(kb_ref.dtype)
    vb_ref[...] = z[:, D_BKV:].astype(vb_ref.dtype)
    if extra:
        @pl.when(tile_in_seq == tiles_per_seq - 1)
        def _():
            kbt_ref[...] = r[tm - WIN_B:, :].T
            vbt_ref[...] = z[tm - WIN_B:, D_BKV:].T


def _attn_proj(x, w, cos_t, sin_t, table_block, act_dtype, prompt_seq=None):
    n = x.shape[0]
    tm = min(ROW_TILE, n)

    def row(width):
        return pl.BlockSpec((tm, width), lambda i: (i, 0))

    tab = pl.BlockSpec((tm, LANES), lambda i: (table_block(i), 0))
    widths = (D_A, D_A, D_A, D_BQ, D_BKV, D_BKV)
    out_shape = [jax.ShapeDtypeStruct((n, wd), act_dtype) for wd in widths]
    out_specs = [row(wd) for wd in widths]
    scratch = []
    tps = None
    if prompt_seq is not None:
        bsz, seq = prompt_seq
        tps = seq // tm
        for dil in DILATIONS[1:]:
            out_shape += [jax.ShapeDtypeStruct((bsz, dil, seq // dil, D_A), BF16)] * 3
            out_specs += [pl.BlockSpec((None, dil, tm // dil, D_A), lambda i: (i // tps, 0, i % tps, 0))] * 3
        first_tail = tps - WIN_A // tm
        out_shape += [jax.ShapeDtypeStruct((bsz, D_A, WIN_A), F32)] * 2
        out_specs += [pl.BlockSpec((None, D_A, tm), lambda i: (i // tps, 0, jnp.maximum(i % tps - first_tail, 0)))] * 2
        out_shape += [jax.ShapeDtypeStruct((bsz, D_BKV, WIN_B), F32)] * 2
        out_specs += [pl.BlockSpec((None, D_BKV, WIN_B), lambda i: (i // tps, 0, 0))] * 2
        scratch = [pltpu.VMEM((3 * D_A // LANES, tm, LANES), F32)]
    return pl.pallas_call(
        functools.partial(_attn_proj_kernel, tps),
        grid=(n // tm,),
        in_specs=[row(D_MODEL), _resident((D_MODEL, D_IN_ATTN)), tab, tab],
        out_specs=out_specs,
        out_shape=out_shape,
        scratch_shapes=scratch,
        compiler_params=_params(1),
        name="attn_proj",
    )(x, w, cos_t, sin_t)


def _lane_lo(shape):
    return lax.broadcasted_iota(jnp.int32, shape, 1) < HEAD_DIM


def _half_masks_bf16():
    lo = jnp.where(_lane_lo((1, LANES)), 1.0, 0.0).astype(BF16)
    return lo, 1 - lo


def _band_masks(n_heads, t, sub, prev_strict):
    row = lax.broadcasted_iota(jnp.int32, (n_heads * TQ, TQ), 0) & (TQ - 1)
    col = lax.broadcasted_iota(jnp.int32, (n_heads * TQ, TQ), 1)
    mask_c = col <= row
    shift = jnp.where(t > 0, 0, TQ) if sub == 0 else 0
    mask_p = (col > row + shift) if prev_strict else (col >= row + shift)
    return mask_c, mask_p


def _sub_tile_kv(sub, sl, kc_ref, kp_ref, vc_ref, vp_ref):
    if sub == 0:
        return kc_ref[0:TQ, sl], kp_ref[:, sl], vc_ref[0:TQ, sl], vp_ref[:, sl]
    return kc_ref[TQ:2 * TQ, sl], kc_ref[0:TQ, sl], vc_ref[TQ:2 * TQ, sl], vc_ref[0:TQ, sl]


def _band_softmax(qs, kc, kp, vc, vp, mask_c, mask_p, sink=None):
    s_c = jnp.where(mask_c, _dot_nt(qs, kc), NEG_INF)
    s_p = jnp.where(mask_p, _dot_nt(qs, kp), NEG_INF)
    m = jnp.max(jnp.maximum(s_c, s_p), -1, keepdims=True)
    if sink is not None:
        m = jnp.maximum(m, sink)
    p_c = jnp.exp(s_c - m)
    p_p = jnp.exp(s_p - m)
    den = jnp.sum(p_c + p_p, -1, keepdims=True)
    if sink is not None:
        den = den + jnp.exp(sink - m)
    acc = _dot(p_c.astype(BF16), vc) + _dot(p_p.astype(BF16), vp)
    return acc * (1.0 / den), m, den


def _band_a_kernel(q_ref, kc_ref, kp_ref, vc_ref, vp_ref, o_ref, lse_ref):
    t = pl.program_id(2)
    lo = _lane_lo((TQ, LANES))
    lo_bf, hi_bf = _half_masks_bf16()
    for sub in range(2):
        rows = slice(sub * TQ, (sub + 1) * TQ)
        mask_c, mask_p = _band_masks(2, t, sub, prev_strict=False)
        for c in range(D_A // LANES):
            sl = slice(c * LANES, (c + 1) * LANES)
            kc, kp, vc, vp = _sub_tile_kv(sub, sl, kc_ref, kp_ref, vc_ref, vp_ref)
            q2 = q_ref[rows, sl]
            qs = jnp.concatenate([q2 * lo_bf, q2 * hi_bf], axis=0)
            out, m, den = _band_softmax(qs, kc, kp, vc, vp, mask_c, mask_p)
            lse = jnp.broadcast_to(m + jnp.log(den), (2 * TQ, LANES))
            o_ref[rows, sl] = jnp.where(lo, out[0:TQ], out[TQ:])
            lse_ref[rows, sl] = jnp.where(lo, lse[0:TQ], lse[TQ:])


def _band_a(q, k, v):
    bsz, dil, sub, _ = q.shape
    cur = pl.BlockSpec((None, None, 2 * TQ, D_A), lambda b, r, t: (b, r, t, 0))
    prev = pl.BlockSpec((None, None, TQ, D_A), lambda b, r, t: (b, r, jnp.maximum(2 * t - 1, 0), 0))
    return pl.pallas_call(
        _band_a_kernel,
        grid=(bsz, dil, sub // (2 * TQ)),
        in_specs=[cur, cur, prev, cur, prev],
        out_specs=[cur, cur],
        out_shape=[jax.ShapeDtypeStruct((bsz, dil, sub, D_A), F32)] * 2,
        compiler_params=_params(3),
        name=f"band_a_d{dil}",
    )(q, k, k, v, v)


def _swap_halves(x):
    return jnp.concatenate([x[:, HEAD_DIM:], x[:, :HEAD_DIM]], axis=1)


def _band_b_kernel(sink_ref, q_ref, kc_ref, kp_ref, vc_ref, vp_ref, o_ref):
    t = pl.program_id(1)
    group = N_HEADS_B // N_KV_B
    lo = _lane_lo((TQ, LANES))
    lo_bf, hi_bf = _half_masks_bf16()
    for sub in range(2):
        rows = slice(sub * TQ, (sub + 1) * TQ)
        mask_c, mask_p = _band_masks(group, t, sub, prev_strict=True)
        kv = _sub_tile_kv(sub, slice(0, D_BKV), kc_ref, kp_ref, vc_ref, vp_ref)
        kv_swapped = tuple(_swap_halves(a) for a in kv)
        for g in range(N_KV_B):
            own, other = (lo_bf, hi_bf) if g == 0 else (hi_bf, lo_bf)
            kc, kp, vc, vp = (a * own + a_sw * other for a, a_sw in zip(kv, kv_swapped))
            heads = range(g * group, (g + 1) * group)
            qs = jnp.concatenate(
                [q_ref[rows, (h // 2) * LANES:(h // 2 + 1) * LANES] * (lo_bf if h % 2 == 0 else hi_bf) for h in heads],
                axis=0)
            sink = jnp.concatenate([jnp.full((TQ, 1), sink_ref[h], F32) for h in heads], axis=0)
            out, _, _ = _band_softmax(qs, kc, kp, vc, vp, mask_c, mask_p, sink)
            for i in range(group // 2):
                c = g * (group // 2) + i
                even, odd = out[2 * i * TQ:(2 * i + 1) * TQ], out[(2 * i + 1) * TQ:(2 * i + 2) * TQ]
                o_ref[rows, c * LANES:(c + 1) * LANES] = jnp.where(lo, even, odd).astype(o_ref.dtype)


def _band_b(q, k, v, sinks):
    bsz, seq, _ = q.shape
    qs = pl.BlockSpec((None, 2 * TQ, D_BQ), lambda b, t: (b, t, 0))
    cur = pl.BlockSpec((None, 2 * TQ, D_BKV), lambda b, t: (b, t, 0))
    prev = pl.BlockSpec((None, TQ, D_BKV), lambda b, t: (b, jnp.maximum(2 * t - 1, 0), 0))
    o = pl.pallas_call(
        _band_b_kernel,
        grid=(bsz, seq // (2 * TQ)),
        in_specs=[pl.BlockSpec(memory_space=pltpu.SMEM), qs, cur, prev, cur, prev],
        out_specs=qs,
        out_shape=jax.ShapeDtypeStruct((bsz, seq, D_BQ), BF16),
        compiler_params=_params(2),
        name="band_b",
    )(sinks, q, k, k, v, v)
    return o.reshape(bsz * seq, D_BQ)


def _pattern_count(dist):
    cnt = jnp.zeros(dist.shape, F32)
    for dil in DILATIONS:
        cnt = cnt + ((dist >= 0) & (dist <= 128 * dil) & (dist % dil == 0)).astype(F32)
    return cnt


def _sample_tables():
    i = jnp.arange(DEC_SEQ)
    cnt_c = _pattern_count(WIN_A + i[:, None] - jnp.arange(WIN_A)[None, :])
    j = jnp.arange(LANES)
    cnt_n = jnp.where(j[None, :] < DEC_SEQ, _pattern_count(i[:, None] - j[None, :]), 0.0)
    cnt_n = jnp.tile(cnt_n, (N_HEADS_A, 1))
    jb = jnp.arange(NKB_PAD)[None, :]
    dist_b = WIN_B + i[:, None] - jb
    ok_b = (dist_b >= 0) & (dist_b < WIN_B) & (jb < WIN_B + DEC_SEQ)
    mask_b = jnp.tile(ok_b.astype(F32), (N_HEADS_B, 1))
    return cnt_c, cnt_n, mask_b


def _sample_attend(q, kan, van, kt_ref, vt_ref, cnt_c, cnt_n, qb, kbn, vbn, kbc, vbc, mask_b, sink_col, kb_s, vb_s):
    rows = N_HEADS_A * DEC_SEQ
    q_rep = jnp.concatenate([q] * N_HEADS_A, axis=0)
    row_head = jnp.right_shift(lax.broadcasted_iota(jnp.int32, (rows, D_A), 0), int(math.log2(DEC_SEQ)))
    lane_head = jnp.right_shift(lax.broadcasted_iota(jnp.int32, (rows, D_A), 1), int(math.log2(HEAD_DIM)))
    own = row_head == lane_head
    q_bd = jnp.where(own, q_rep, 0.0).astype(BF16)
    pad = jnp.zeros((LANES - DEC_SEQ, D_A), F32)
    kn = jnp.concatenate([kan, pad], 0).astype(BF16)
    vn = jnp.concatenate([van, pad], 0).astype(BF16)
    s_new = jnp.where(cnt_n > 0.0, _dot_nt(q_bd, kn), NEG_INF)
    outs, p_new = [], []
    for h in range(N_HEADS_A):
        head_rows = slice(h * DEC_SEQ, (h + 1) * DEC_SEQ)
        q_h = q[:, h * HEAD_DIM:(h + 1) * HEAD_DIM].astype(BF16)
        s_c = jnp.where(cnt_c > 0.0, _dot(q_h, kt_ref[h].astype(BF16)), NEG_INF)
        s_n = s_new[head_rows]
        m = jnp.maximum(jnp.max(s_c, -1, keepdims=True), jnp.max(s_n, -1, keepdims=True))
        p_c = jnp.exp(s_c - m) * cnt_c
        p_n = jnp.exp(s_n - m) * cnt_n[head_rows]
        inv = 1.0 / (jnp.sum(p_c, -1, keepdims=True) + jnp.sum(p_n, -1, keepdims=True))
        outs.append(_dot_nt(p_c.astype(BF16), vt_ref[h].astype(BF16)) * inv)
        p_new.append(p_n * inv)
    out_n = jnp.where(own, _dot(jnp.concatenate(p_new, axis=0).astype(BF16), vn), 0.0)
    oa = jnp.concatenate(outs, axis=1)
    for h in range(N_HEADS_A):
        oa = oa + out_n[h * DEC_SEQ:(h + 1) * DEC_SEQ]

    n_pad_b = NKB_PAD - WIN_B - DEC_SEQ
    pad_b = jnp.zeros((n_pad_b, D_BKV), F32)
    kb_s[...] = jnp.concatenate([kbc, kbn, pad_b], 0).astype(BF16)
    vb_s[...] = jnp.concatenate([vbc, vbn, pad_b], 0).astype(BF16)
    lo8 = _lane_lo((DEC_SEQ, LANES))
    group = N_HEADS_B // N_KV_B
    pieces = []
    for h in range(N_HEADS_B):
        chunk = qb[:, (h // 2) * LANES:(h // 2 + 1) * LANES]
        g = h // group
        if h % 2 != g:
            chunk = pltpu.roll(chunk, HEAD_DIM, 1)
        pieces.append(jnp.where(lo8 if g == 0 else jnp.logical_not(lo8), chunk, 0.0))
    qb_bd = jnp.concatenate(pieces, axis=0).astype(BF16)
    sb = jnp.where(mask_b > 0.0, _dot_nt(qb_bd, kb_s[...]), NEG_INF)
    sink = sink_col[:, 0:1]
    mb = jnp.maximum(jnp.max(sb, -1, keepdims=True), sink)
    pb = jnp.exp(sb - mb) * mask_b
    den_b = jnp.sum(pb, -1, keepdims=True) + jnp.exp(sink - mb)
    ob_full = _dot(pb.astype(BF16), vb_s[...]) * (1.0 / den_b)
    ob = []
    for c in range(D_BQ // LANES):
        halves = []
        for half in range(2):
            h = 2 * c + half
            piece = ob_full[h * DEC_SEQ:(h + 1) * DEC_SEQ]
            if half != h // group:
                piece = pltpu.roll(piece, HEAD_DIM, 1)
            halves.append(piece)
        ob.append(jnp.where(lo8, halves[0], halves[1]))
    return jnp.concatenate([oa] + ob, axis=1)


FUSED_ROW_TILE = 512
SEQ_PER_STEP = DEC_BATCH // (BATCH * SEQ // FUSED_ROW_TILE)


def _kv_copies(kt_hbm, vt_hbm, kbuf, vbuf, sems, seq, slot):
    return (pltpu.make_async_copy(kt_hbm.at[seq], kbuf.at[slot], sems.at[0, slot]),
            pltpu.make_async_copy(vt_hbm.at[seq], vbuf.at[slot], sems.at[1, slot]))


def _ffn_attn_kernel(x_ref, wg_ref, wu_ref, wd_ref, g_ref, b_ref,
                     qa_ref, kan_ref, van_ref, kt_hbm, vt_hbm, cnt_c_ref, cnt_n_ref,
                     qb_ref, kbn_ref, vbn_ref, kbc_ref, vbc_ref, maskb_ref, sinkcol_ref,
                     y_ref, o_ref, h_ref, kbuf, vbuf, sems, kb_s, vb_s):
    step = pl.program_id(0)
    n_steps = pl.num_programs(0)
    copies = functools.partial(_kv_copies, kt_hbm, vt_hbm, kbuf, vbuf, sems)

    @pl.when(step == 0)
    def _():
        for cp in copies(0, 0):
            cp.start()

    x = x_ref[...]
    xb = x.astype(BF16)
    n_ff = D_FF // FF_CHUNK
    per_seq = -(-n_ff // SEQ_PER_STEP)
    cnt_c, cnt_n, mask_b, sink_col = cnt_c_ref[...], cnt_n_ref[...], maskb_ref[...], sinkcol_ref[...]
    for s in range(SEQ_PER_STEP):
        slot = s % 2
        seq = step * SEQ_PER_STEP + s
        for cp in copies(seq, slot):
            cp.wait()
        if s + 1 < SEQ_PER_STEP:
            for cp in copies(seq + 1, 1 - slot):
                cp.start()
        else:
            @pl.when(step + 1 < n_steps)
            def _():
                for cp in copies(seq + 1, 1 - slot):
                    cp.start()
        rows = slice(s * DEC_SEQ, (s + 1) * DEC_SEQ)
        o_ref[rows, :] = _sample_attend(
            qa_ref[rows, :], kan_ref[rows, :], van_ref[rows, :], kbuf.at[slot], vbuf.at[slot], cnt_c, cnt_n,
            qb_ref[rows, :], kbn_ref[rows, :], vbn_ref[rows, :], kbc_ref[s], vbc_ref[s], mask_b, sink_col, kb_s, vb_s)
        for c in range(s * per_seq, min((s + 1) * per_seq, n_ff)):
            sl = slice(c * FF_CHUNK, (c + 1) * FF_CHUNK)
            gate = _dot(xb, wg_ref[:, sl])
            up = _dot(xb, wu_ref[:, sl])
            h_ref[:, sl] = (gate * jax.nn.sigmoid(gate) * up).astype(BF16)
    y = DN_ALPHA * x + FFN_RES * _dot(h_ref[...], wd_ref[...])
    y_ref[...] = _layer_norm(y, g_ref[...], b_ref[...])


def _ffn_with_sample_attention(x, wg, wu, wd, g, b, layer, which,
                               qa, kan, van, cache_ak, cache_av, qb, kbn, vbn, cache_bk, cache_bv, sinks):
    n = x.shape[0]
    tm = FUSED_ROW_TILE
    assert n // tm * SEQ_PER_STEP == DEC_BATCH and SEQ_PER_STEP % 2 == 0
    cnt_c, cnt_n, mask_b = _sample_tables()
    sink_col = jnp.broadcast_to(jnp.repeat(sinks.astype(F32), DEC_SEQ)[:, None], (N_HEADS_B * DEC_SEQ, LANES))
    row = pl.BlockSpec((tm, D_MODEL), lambda i: (i, 0))
    pick = lambda r, c: pl.BlockSpec((None, None, r, c), lambda i: (layer, which, 0, 0), pipeline_mode=pl.Buffered(1))
    new = lambda width: pl.BlockSpec((SEQ_PER_STEP * DEC_SEQ, width), lambda i: (i, 0))
    cache_b = pl.BlockSpec((SEQ_PER_STEP, WIN_B, D_BKV), lambda i: (i, 0, 0))
    hbm = pl.BlockSpec(memory_space=pl.ANY)
    rows = N_HEADS_A * DEC_SEQ
    kv_slot = (2, N_HEADS_A, HEAD_DIM, WIN_A)
    return pl.pallas_call(
        _ffn_attn_kernel,
        grid=(n // tm,),
        in_specs=[row, pick(D_MODEL, D_FF), pick(D_MODEL, D_FF), pick(D_FF, D_MODEL),
                  _resident((1, D_MODEL)), _resident((1, D_MODEL)),
                  new(D_A), new(D_A), new(D_A), hbm, hbm, _resident(cnt_c.shape), _resident(cnt_n.shape),
                  new(D_BQ), new(D_BKV), new(D_BKV), cache_b, cache_b,
                  _resident((rows, NKB_PAD)), _resident((rows, LANES))],
        out_specs=[row, new(D_A + D_BQ)],
        out_shape=[jax.ShapeDtypeStruct((n, D_MODEL), F32),
                   jax.ShapeDtypeStruct((DEC_BATCH * DEC_SEQ, D_A + D_BQ), F32)],
        scratch_shapes=[pltpu.VMEM((tm, D_FF), BF16), pltpu.VMEM(kv_slot, F32), pltpu.VMEM(kv_slot, F32),
                        pltpu.SemaphoreType.DMA((2, 2)),
                        pltpu.VMEM((NKB_PAD, D_BKV), BF16), pltpu.VMEM((NKB_PAD, D_BKV), BF16)],
        compiler_params=_params(1, vmem=60 * 1024 * 1024),
        name="ffn_attn_sample",
    )(x, wg, wu, wd, g, b, qa, kan, van, cache_ak, cache_av, cnt_c, cnt_n, qb, kbn, vbn, cache_bk, cache_bv,
      mask_b, sink_col)


def _attn_out_prompt_kernel(o1_ref, l1_ref, o4_ref, l4_ref, o16_ref, l16_ref, ob_ref, y_ref, w_ref, g_ref, b_ref,
                            out_ref, slab_ref, oa_ref):
    tm = y_ref.shape[0]
    n_chunks = D_A // LANES
    for i, (dil, src) in enumerate(((4, o4_ref), (4, l4_ref), (16, o16_ref), (16, l16_ref))):
        for r in range(dil):
            for c in range(n_chunks):
                slab_ref[i * n_chunks + c, pl.ds(r, tm // dil, stride=dil), :] = src[r, :, c * LANES:(c + 1) * LANES]
    for c in range(n_chunks):
        sl = slice(c * LANES, (c + 1) * LANES)
        l1, l4, l16 = l1_ref[:, sl], slab_ref[n_chunks + c], slab_ref[3 * n_chunks + c]
        m = jnp.maximum(jnp.maximum(l1, l4), l16)
        e1, e4, e16 = jnp.exp(l1 - m), jnp.exp(l4 - m), jnp.exp(l16 - m)
        oa = (e1 * o1_ref[:, sl] + e4 * slab_ref[c] + e16 * slab_ref[2 * n_chunks + c]) * (1.0 / (e1 + e4 + e16))
        oa_ref[:, sl] = oa.astype(BF16)
    mix = _dot(oa_ref[...], w_ref[0:D_A, :]) + _dot(ob_ref[...], w_ref[D_A:, :])
    out_ref[...] = _layer_norm(DN_ALPHA * y_ref[...] + mix, g_ref[...], b_ref[...])


def _attn_out_prompt(pats, ob, y, w, g, b, seq):
    n = y.shape[0]
    tm = ROW_TILE
    tps = seq // tm
    half = pl.BlockSpec((tm, D_A), lambda i: (i, 0))
    full = pl.BlockSpec((tm, D_MODEL), lambda i: (i, 0))
    planes = lambda dil: pl.BlockSpec((None, dil, tm // dil, D_A), lambda i: (i // tps, 0, i % tps, 0))
    (o1, l1), (o4, l4), (o16, l16) = pats
    return pl.pallas_call(
        _attn_out_prompt_kernel,
        grid=(n // tm,),
        in_specs=[half, half, planes(4), planes(4), planes(16), planes(16), half, full,
                  _resident((D_MODEL, D_MODEL)), _resident((1, D_MODEL)), _resident((1, D_MODEL))],
        out_specs=full,
        out_shape=jax.ShapeDtypeStruct((n, D_MODEL), F32),
        scratch_shapes=[pltpu.VMEM((4 * D_A // LANES, tm, LANES), F32), pltpu.VMEM((tm, D_A), BF16)],
        compiler_params=_params(1),
        name="attn_out_prompt",
    )(o1, l1, o4, l4, o16, l16, ob, y, w, g, b)


def _mix_out_kernel(o_ref, y_ref, w_ref, g_ref, b_ref, out_ref):
    mix = _dot(o_ref[...].astype(BF16), w_ref[...])
    out_ref[...] = _layer_norm(DN_ALPHA * y_ref[...] + mix, g_ref[...], b_ref[...])


def _attn_out_sample(o, y, w, g, b):
    n = y.shape[0]
    tm = min(ROW_TILE, n)
    full = pl.BlockSpec((tm, D_MODEL), lambda i: (i, 0))
    return pl.pallas_call(
        _mix_out_kernel,
        grid=(n // tm,),
        in_specs=[full, full, _resident((D_MODEL, D_MODEL)), _resident((1, D_MODEL)), _resident((1, D_MODEL))],
        out_specs=full,
        out_shape=jax.ShapeDtypeStruct((n, D_MODEL), F32),
        compiler_params=_params(1),
        name="attn_out_sample",
    )(o, y, w, g, b)


def _ssm_discretize(lam_re, lam_im, log_dt, b_re, b_im):
    dt = jnp.exp(log_dt.astype(F32))[:, None]
    lr, li = lam_re.astype(F32), lam_im.astype(F32)
    mag = jnp.exp(lr * dt)
    ab_re, ab_im = mag * jnp.cos(li * dt), mag * jnp.sin(li * dt)
    nr, ni = ab_re - 1.0, ab_im
    den = lr * lr + li * li
    fr, fi = (nr * lr + ni * li) / den, (ni * lr - nr * li) / den
    bb_re = fr[..., None] * b_re - fi[..., None] * b_im
    bb_im = fr[..., None] * b_im + fi[..., None] * b_re
    return ab_re, ab_im, bb_re, bb_im


def _ssm_matrices(bb_re, bb_im, c_re, c_im):
    gpc = LANES // SSM_GROUP
    eye = jnp.eye(gpc, dtype=F32)

    def in_blocks(bb):
        a = bb.reshape(N_LCHUNK, gpc, SSM_STATE, SSM_GROUP)
        return jnp.einsum("jgpn,gh->jgnhp", a, eye).reshape(N_LCHUNK, LANES, ST_CHUNK)

    def out_blocks(cc):
        a = cc.reshape(N_LCHUNK, gpc, SSM_GROUP, SSM_STATE)
        return jnp.einsum("jgnp,gh->jgphn", a, eye).reshape(N_LCHUNK, ST_CHUNK, LANES)

    bmat = jnp.concatenate([in_blocks(bb_re), in_blocks(bb_im)], -1)
    cmat = jnp.concatenate([out_blocks(c_re), -out_blocks(c_im)], 1)
    return bmat, cmat


CHUNK = 8


def _dot3(a, b):
    a_hi, b_hi = a.astype(BF16), b.astype(BF16)
    a_lo, b_lo = (a - a_hi.astype(F32)).astype(BF16), (b - b_hi.astype(F32)).astype(BF16)
    return _dot(a_hi, b_hi) + _dot(a_hi, b_lo) + _dot(a_lo, b_hi)


def _chunk_weights_kernel(ar_ref, ai_ref, acr_ref, aci_ref, bmat_ref, cmat_ref,
                          we_ref, ws_ref, wi_ref, a8r_ref, a8i_ref):
    def powers(r, i, n):
        out = [(jnp.ones_like(r), jnp.zeros_like(r))]
        for _ in range(n):
            out.append(_cmul(out[-1][0], out[-1][1], r, i))
        return out

    row_pow = powers(ar_ref[...], ai_ref[...], CHUNK)
    col_pow = powers(acr_ref[...], aci_ref[...], CHUNK)
    b_re, b_im = bmat_ref[:, 0:ST_CHUNK], bmat_ref[:, ST_CHUNK:]
    c_re, c_im = cmat_ref[0:ST_CHUNK, :], -cmat_ref[ST_CHUNK:, :]

    def scaled_b(power):
        pr, pi = row_pow[power]
        return jnp.concatenate([b_re * pr - b_im * pi, b_re * pi + b_im * pr], axis=1)

    for k in range(CHUNK):
        cols = slice(k * LANES, (k + 1) * LANES)
        we_ref[cols, :] = scaled_b(CHUNK - 1 - k).astype(BF16)
        qr, qi = col_pow[k + 1]
        ws_ref[0:ST_CHUNK, cols] = (c_re * qr - c_im * qi).astype(BF16)
        ws_ref[ST_CHUNK:, cols] = (-(c_re * qi + c_im * qr)).astype(BF16)
    taps = [_dot3(scaled_b(tau), cmat_ref[...]).astype(BF16) for tau in range(CHUNK)]
    zero = jnp.zeros((LANES, LANES), BF16)
    for k_in in range(CHUNK):
        for k_out in range(CHUNK):
            wi_ref[k_in * LANES:(k_in + 1) * LANES, k_out * LANES:(k_out + 1) * LANES] = (
                taps[k_out - k_in] if k_out >= k_in else zero)
    a8r_ref[...], a8i_ref[...] = row_pow[CHUNK]


def _chunk_weights(a_re, a_im, bmat, cmat):
    cols = lambda a: jnp.broadcast_to(a.reshape(N_LCHUNK, ST_CHUNK, 1), (N_LCHUNK, ST_CHUNK, LANES))
    per_j = lambda r, c: pl.BlockSpec((None, r, c), lambda j: (j, 0, 0))
    wide = 2 * ST_CHUNK
    return pl.pallas_call(
        _chunk_weights_kernel,
        grid=(N_LCHUNK,),
        in_specs=[per_j(1, ST_CHUNK), per_j(1, ST_CHUNK), per_j(ST_CHUNK, LANES), per_j(ST_CHUNK, LANES),
                  per_j(LANES, wide), per_j(wide, LANES)],
        out_specs=[per_j(CHUNK * LANES, wide), per_j(wide, CHUNK * LANES), per_j(CHUNK * LANES, CHUNK * LANES),
                   per_j(1, ST_CHUNK), per_j(1, ST_CHUNK)],
        out_shape=[jax.ShapeDtypeStruct((N_LCHUNK, CHUNK * LANES, wide), BF16),
                   jax.ShapeDtypeStruct((N_LCHUNK, wide, CHUNK * LANES), BF16),
                   jax.ShapeDtypeStruct((N_LCHUNK, CHUNK * LANES, CHUNK * LANES), BF16),
                   jax.ShapeDtypeStruct((N_LCHUNK, 1, ST_CHUNK), F32),
                   jax.ShapeDtypeStruct((N_LCHUNK, 1, ST_CHUNK), F32)],
        compiler_params=_params(1),
        name="ssm_chunk_weights",
    )(a_re, a_im, cols(a_re), cols(a_im), bmat, cmat)


SEG_TILE = ROW_TILE // N_SEG


def _ssm_in_prompt_kernel(x_ref, w_ref, o_ref):
    x = x_ref[...].reshape(N_SEG * SEG_TILE, D_MODEL)
    u = _dot(x.astype(BF16), w_ref[...])
    for s in range(N_SEG):
        for c in range(N_LCHUNK):
            o_ref[c, pl.ds(s, SEG_TILE, stride=N_SEG), :] = u[s * SEG_TILE:(s + 1) * SEG_TILE, c * LANES:(c + 1) * LANES]


def _ssm_in_prompt(y, w):
    return pl.pallas_call(
        _ssm_in_prompt_kernel,
        grid=(BATCH, SEG_LEN // SEG_TILE),
        in_specs=[pl.BlockSpec((None, N_SEG, SEG_TILE, D_MODEL), lambda b, t: (b, 0, t, 0)),
                  _resident((D_MODEL, D_MODEL))],
        out_specs=pl.BlockSpec((None, N_LCHUNK, ROW_TILE, LANES), lambda b, t: (b, 0, t, 0)),
        out_shape=jax.ShapeDtypeStruct((BATCH, N_LCHUNK, SEQ, LANES), F32),
        compiler_params=_params(2),
        name="ssm_in_prompt",
    )(y.reshape(BATCH, N_SEG, SEG_LEN, D_MODEL), w)


def _ssm_in_sample_kernel(x_ref, w_ref, o_ref, slab_ref):
    u = _dot(x_ref[...].astype(BF16), w_ref[...])
    for c in range(N_LCHUNK):
        slab_ref[c] = u[:, c * LANES:(c + 1) * LANES]
    for l in range(DEC_SEQ):
        for c in range(N_LCHUNK):
            o_ref[c, l * DEC_BATCH:(l + 1) * DEC_BATCH, :] = slab_ref[c, pl.ds(l, DEC_BATCH, stride=DEC_SEQ), :]


def _ssm_in_sample(y, w):
    n = DEC_BATCH * DEC_SEQ
    return pl.pallas_call(
        _ssm_in_sample_kernel,
        grid=(1,),
        in_specs=[_resident((n, D_MODEL)), _resident((D_MODEL, D_MODEL))],
        out_specs=pl.BlockSpec((N_LCHUNK, n, LANES), lambda i: (0, 0, 0)),
        out_shape=jax.ShapeDtypeStruct((N_LCHUNK, n, LANES), F32),
        scratch_shapes=[pltpu.VMEM((N_LCHUNK, n, LANES), F32)],
        compiler_params=_params(1),
        name="ssm_in_sample",
    )(y, w)


def _cmul(ar, ai, br, bi):
    return ar * br - ai * bi, ar * bi + ai * br


def _scan_prompt_kernel(u_ref, we_ref, ws_ref, wi_ref, a8r_ref, a8i_ref, d_ref, h0r_ref, h0i_ref,
                        y_ref, hnr_ref, hni_ref, e_s, hs_s):
    n_chunks = u_ref.shape[0]
    rows = n_chunks * N_SEG
    u_flat = jnp.concatenate([u_ref[:, k].reshape(rows, LANES) for k in range(CHUNK)], axis=1)
    ub = u_flat.astype(BF16)
    e_s[...] = _dot(ub, we_ref[...])
    a8r1, a8i1 = a8r_ref[...], a8i_ref[...]
    a8r = jnp.broadcast_to(a8r1, (N_SEG, ST_CHUNK))
    a8i = jnp.broadcast_to(a8i1, (N_SEG, ST_CHUNK))

    def advance(row, hr, hi):
        er = e_s[pl.ds(row, N_SEG), 0:ST_CHUNK]
        ei = e_s[pl.ds(row, N_SEG), ST_CHUNK:2 * ST_CHUNK]
        return a8r * hr - a8i * hi + er, a8r * hi + a8i * hr + ei

    def pass1(c, carry):
        return advance(pl.multiple_of(c * N_SEG, N_SEG), *carry)

    zero = jnp.zeros((N_SEG, ST_CHUNK), F32)
    er, ei = lax.fori_loop(0, n_chunks, pass1, (zero, zero), unroll=8)

    pr, pi = a8r1, a8i1
    for _ in range(int(math.log2(n_chunks))):
        pr, pi = _cmul(pr, pi, pr, pi)
    hr, hi = h0r_ref[...], h0i_ref[...]
    starts_r, starts_i = [], []
    for s in range(N_SEG):
        starts_r.append(hr)
        starts_i.append(hi)
        gr, gi = _cmul(pr, pi, hr, hi)
        hr, hi = gr + er[s:s + 1], gi + ei[s:s + 1]
    hnr_ref[...] = hr
    hni_ref[...] = hi
    init = (jnp.concatenate(starts_r, 0), jnp.concatenate(starts_i, 0))

    def pass2(c, carry):
        row = pl.multiple_of(c * N_SEG, N_SEG)
        hs_s[pl.ds(row, N_SEG), 0:ST_CHUNK] = carry[0]
        hs_s[pl.ds(row, N_SEG), ST_CHUNK:2 * ST_CHUNK] = carry[1]
        return advance(row, *carry)

    lax.fori_loop(0, n_chunks, pass2, init, unroll=8)
    y = _dot(hs_s[...].astype(BF16), ws_ref[...]) + _dot(ub, wi_ref[...]) + d_ref[...] * u_flat
    for k in range(CHUNK):
        y_ref[:, k] = y[:, k * LANES:(k + 1) * LANES].reshape(n_chunks, N_SEG, LANES)


def _scan_prompt(u, weights, d_skip, h0r, h0i):
    bsz, _, seq, _ = u.shape
    n_chunks = seq // (CHUNK * N_SEG)
    rows = n_chunks * N_SEG
    wide = 2 * ST_CHUNK
    split = lambda a: a.reshape(bsz, N_LCHUNK, n_chunks, CHUNK, N_SEG, LANES)
    chunk = pl.BlockSpec((None, None, n_chunks, CHUNK, N_SEG, LANES), lambda j, b: (b, j, 0, 0, 0, 0))
    per_j = lambda r, c: pl.BlockSpec((None, r, c), lambda j, b: (j, 0, 0))
    state = pl.BlockSpec((None, 1, ST_CHUNK), lambda j, b: (b, 0, j))
    y, hr, hi = pl.pallas_call(
        _scan_prompt_kernel,
        grid=(N_LCHUNK, bsz),
        in_specs=[chunk, per_j(CHUNK * LANES, wide), per_j(wide, CHUNK * LANES), per_j(CHUNK * LANES, CHUNK * LANES),
                  per_j(1, ST_CHUNK), per_j(1, ST_CHUNK), per_j(1, CHUNK * LANES), state, state],
        out_specs=[chunk, state, state],
        out_shape=[jax.ShapeDtypeStruct((bsz, N_LCHUNK, n_chunks, CHUNK, N_SEG, LANES), F32),
                   jax.ShapeDtypeStruct((bsz, 1, N_STATE), F32), jax.ShapeDtypeStruct((bsz, 1, N_STATE), F32)],
        scratch_shapes=[pltpu.VMEM((rows, wide), F32), pltpu.VMEM((rows, wide), F32)],
        compiler_params=_params(2),
        name="ssm_scan_prompt",
    )(split(u), *weights, d_skip, h0r, h0i)
    return y.reshape(u.shape), hr, hi


def _scan_sample_kernel(u_ref, bmat_ref, cmat_ref, are_ref, aim_ref, d_ref, h0r_ref, h0i_ref,
                        y_ref, hnr_ref, hni_ref):
    a_re, a_im = are_ref[...], aim_ref[...]
    hr, hi = h0r_ref[...].T, h0i_ref[...].T
    for l in range(DEC_SEQ):
        rows = slice(l * DEC_BATCH, (l + 1) * DEC_BATCH)
        u = u_ref[rows, :]
        bu = _dot(u.astype(BF16), bmat_ref[...])
        gr, gi = _cmul(a_re, a_im, hr, hi)
        hr, hi = gr + bu[:, :ST_CHUNK], gi + bu[:, ST_CHUNK:]
        h = jnp.concatenate([hr, hi], axis=1).astype(BF16)
        y_ref[rows, :] = _dot(h, cmat_ref[...]) + d_ref[...] * u
    hnr_ref[...] = hr.T
    hni_ref[...] = hi.T


def _scan_sample(u, bmat, cmat, a_re, a_im, d_skip, h0r, h0i):
    n = DEC_SEQ * DEC_BATCH
    chunk = pl.BlockSpec((None, n, LANES), lambda j: (j, 0, 0))
    per_j = lambda r, c: pl.BlockSpec((None, r, c), lambda j: (j, 0, 0))
    state = pl.BlockSpec((ST_CHUNK, DEC_BATCH), lambda j: (j, 0))
    return pl.pallas_call(
        _scan_sample_kernel,
        grid=(N_LCHUNK,),
        in_specs=[chunk, per_j(LANES, 2 * ST_CHUNK), per_j(2 * ST_CHUNK, LANES), per_j(1, ST_CHUNK),
                  per_j(1, ST_CHUNK), per_j(1, LANES), state, state],
        out_specs=[chunk, state, state],
        out_shape=[jax.ShapeDtypeStruct((N_LCHUNK, n, LANES), F32),
                   jax.ShapeDtypeStruct((N_STATE, DEC_BATCH), F32), jax.ShapeDtypeStruct((N_STATE, DEC_BATCH), F32)],
        compiler_params=_params(1),
        name="ssm_scan_sample",
    )(u, bmat, cmat, a_re, a_im, d_skip, h0r, h0i)


def _ssm_out_kernel(sample, s_ref, y_ref, wglu_ref, bglu_ref, wout_ref, g_ref, b_ref, out_ref, z_ref):
    n_rows = z_ref.shape[1]
    if sample:
        parts = [(pl.ds(l, DEC_BATCH, stride=DEC_SEQ), slice(l * DEC_BATCH, (l + 1) * DEC_BATCH))
                 for l in range(DEC_SEQ)]
    else:
        parts = [(slice(s * SEG_TILE, (s + 1) * SEG_TILE), pl.ds(s, SEG_TILE, stride=N_SEG)) for s in range(N_SEG)]
    for c in range(N_LCHUNK):
        for tok_rows, slab_rows in parts:
            z_ref[c, tok_rows, :] = s_ref[c, slab_rows, :]
    z = jax.nn.gelu(jnp.concatenate([z_ref[c] for c in range(N_LCHUNK)], axis=1))
    gate = jax.nn.sigmoid(_dot(z.astype(BF16), wglu_ref[...]) + bglu_ref[...])
    mix = _dot((z * gate).astype(BF16), wout_ref[...])
    res = y_ref[...].reshape(n_rows, D_MODEL)
    out = _layer_norm(DN_ALPHA * res + mix, g_ref[...], b_ref[...])
    out_ref[...] = out.reshape(out_ref.shape)


def _ssm_out(s, y, s_spec, y_spec, grid, rows, sample, w_glu, b_glu, w_out, g, b, name):
    return pl.pallas_call(
        functools.partial(_ssm_out_kernel, sample),
        grid=grid,
        in_specs=[s_spec, y_spec, _resident((D_MODEL, D_MODEL)), _resident((1, D_MODEL)),
                  _resident((D_MODEL, D_MODEL)), _resident((1, D_MODEL)), _resident((1, D_MODEL))],
        out_specs=y_spec,
        out_shape=jax.ShapeDtypeStruct(y.shape, F32),
        scratch_shapes=[pltpu.VMEM((N_LCHUNK, rows, LANES), F32)],
        compiler_params=_params(len(grid)),
        name=name,
    )(s, y, w_glu, b_glu, w_out, g, b)


def _attn_prompt(yp, w_in, sinks, w_out, g, b):
    cos_p, sin_p = _rope_tables(jnp.arange(SEQ))
    tiles_per_seq = SEQ // ROW_TILE
    qa, ka, va, qb, kb, vb, *extra = _attn_proj(
        yp, w_in, cos_p, sin_p, lambda i: i % tiles_per_seq, BF16, prompt_seq=(BATCH, SEQ))
    dilated, tails = extra[:6], extra[6:]
    seq3 = lambda a: a.reshape(BATCH, SEQ, a.shape[-1])
    plane1 = lambda a: a.reshape(BATCH, 1, SEQ, D_A)
    o1, l1 = _band_a(plane1(qa), plane1(ka), plane1(va))
    pats = [(o1.reshape(BATCH * SEQ, D_A), l1.reshape(BATCH * SEQ, D_A))]
    for i in range(len(DILATIONS) - 1):
        pats.append(_band_a(*dilated[3 * i:3 * i + 3]))
    ob = _band_b(seq3(qb), seq3(kb), seq3(vb), sinks)
    yp = _attn_out_prompt(pats, ob, yp, w_out, g, b, SEQ)
    heads = lambda a, nh: jnp.transpose(a.reshape(BATCH, nh, HEAD_DIM, a.shape[-1]), (0, 3, 1, 2))[None]
    prompt_cache = (heads(tails[0], N_HEADS_A), heads(tails[1], N_HEADS_A),
                    heads(tails[2], N_KV_B), heads(tails[3], N_KV_B))
    return yp, prompt_cache


def _attn_sample_path(ys, cache_ak, cache_av, cache_bk, cache_bv, w_in, sinks, w_out, g, b, yp, ffn_args):
    cos_s, sin_s = _rope_tables(PAST_LEN + jnp.arange(DEC_SEQ))
    reps = DEC_BATCH * DEC_SEQ // DEC_SEQ
    cos_s, sin_s = jnp.tile(cos_s, (reps, 1)), jnp.tile(sin_s, (reps, 1))
    qa, ka, va, qb, kb, vb = _attn_proj(ys, w_in, cos_s, sin_s, lambda i: i, F32)
    stored = lambda a: jnp.transpose(a, (0, 2, 3, 1))
    yp, o = _ffn_with_sample_attention(
        yp, *ffn_args, qa, ka, va, stored(cache_ak), stored(cache_av),
        qb, kb, vb, cache_bk.reshape(DEC_BATCH, WIN_B, D_BKV), cache_bv.reshape(DEC_BATCH, WIN_B, D_BKV), sinks)
    ys = _attn_out_sample(o, ys, w_out, g, b)
    new = lambda a, nh: a.reshape(1, DEC_BATCH, DEC_SEQ, nh, HEAD_DIM)
    sample_cache = (new(ka, N_HEADS_A), new(va, N_HEADS_A), new(kb, N_KV_B), new(vb, N_KV_B))
    return yp, ys, sample_cache


def _attn_layer(yp, ys, cache_ak, cache_av, cache_bk, cache_bv, w_in, sinks, w_out, g, b, ffn_args):
    w_in = w_in.astype(BF16)
    w_out = w_out.astype(BF16)
    yp, prompt_cache = _attn_prompt(yp, w_in, sinks, w_out, g, b)
    yp, ys, sample_cache = _attn_sample_path(ys, cache_ak, cache_av, cache_bk, cache_bv, w_in, sinks, w_out, g, b,
                                             yp, ffn_args)
    return yp, ys, prompt_cache, sample_cache


def _ssm_layer(yp, ys, state_re, state_im, w_in, lam_re, lam_im, log_dt, b_re, b_im, c_re, c_im, d_skip,
               w_glu, b_glu, w_out, g, b):
    w_in, w_glu, w_out = w_in.astype(BF16), w_glu.astype(BF16), w_out.astype(BF16)
    b_glu = b_glu.reshape(1, D_MODEL)
    mats = _ssm_prepare(lam_re, lam_im, log_dt, b_re, b_im, c_re, c_im, d_skip)
    yp, prompt_state = _ssm_prompt(yp, w_in, mats, w_glu, b_glu, w_out, g, b)
    ys, sample_state = _ssm_sample(ys, state_re, state_im, w_in, mats, w_glu, b_glu, w_out, g, b)
    return yp, ys, prompt_state, sample_state


def _ssm_prepare(lam_re, lam_im, log_dt, b_re, b_im, c_re, c_im, d_skip):
    ab_re, ab_im, bb_re, bb_im = _ssm_discretize(lam_re, lam_im, log_dt, b_re, b_im)
    bmat, cmat = _ssm_matrices(bb_re, bb_im, c_re, c_im)
    a_re = ab_re.reshape(N_LCHUNK, 1, ST_CHUNK)
    a_im = ab_im.reshape(N_LCHUNK, 1, ST_CHUNK)
    d3 = d_skip.astype(F32).reshape(N_LCHUNK, 1, LANES)
    prompt = (_chunk_weights(a_re, a_im, bmat, cmat), jnp.tile(d3, (1, 1, CHUNK)))
    sample = (bmat.astype(BF16), cmat.astype(BF16), a_re, a_im, d3)
    return prompt, sample


def _ssm_prompt(yp, w_in, mats, w_glu, b_glu, w_out, g, b):
    weights, d_tiled = mats[0]
    up = _ssm_in_prompt(yp, w_in)
    zero = jnp.zeros((BATCH, 1, N_STATE), F32)
    sp, pr, pi = _scan_prompt(up, weights, d_tiled, zero, zero)
    yp = _ssm_out(sp, yp.reshape(BATCH, N_SEG, SEG_LEN, D_MODEL),
                  pl.BlockSpec((None, N_LCHUNK, ROW_TILE, LANES), lambda bb, t: (bb, 0, t, 0)),
                  pl.BlockSpec((None, N_SEG, SEG_TILE, D_MODEL), lambda bb, t: (bb, 0, t, 0)),
                  (BATCH, SEG_LEN // SEG_TILE), ROW_TILE, False, w_glu, b_glu, w_out, g, b,
                  "ssm_out_prompt").reshape(BATCH * SEQ, D_MODEL)
    prompt_state = (pr.reshape(1, BATCH, N_SSM_GROUPS, SSM_STATE), pi.reshape(1, BATCH, N_SSM_GROUPS, SSM_STATE))
    return yp, prompt_state


def _ssm_sample(ys, state_re, state_im, w_in, mats, w_glu, b_glu, w_out, g, b):
    bmat, cmat, a_re, a_im, d3 = mats[1]
    us = _ssm_in_sample(ys, w_in)
    stored = lambda a: jnp.transpose(a, (1, 2, 0)).reshape(N_STATE, DEC_BATCH)
    logical = lambda a: jnp.transpose(a.reshape(N_SSM_GROUPS, SSM_STATE, DEC_BATCH), (2, 0, 1))[None]
    ss, sr, si = _scan_sample(us, bmat, cmat, a_re, a_im, d3, stored(state_re), stored(state_im))
    n = DEC_BATCH * DEC_SEQ
    ys = _ssm_out(ss, ys,
                  pl.BlockSpec((N_LCHUNK, n, LANES), lambda i: (0, 0, 0)),
                  pl.BlockSpec((n, D_MODEL), lambda i: (0, 0)),
                  (1,), n, True, w_glu, b_glu, w_out, g, b, "ssm_out_sample")
    return ys, (logical(sr), logical(si))


def kernel(x_prompt, x_sample, cache_a_k, cache_a_v, cache_b_k, cache_b_v, state_c_re, state_c_im, ln_g, ln_b, ffn_w_gate, ffn_w_up, ffn_w_down, attn_w_in, attn_sinks, attn_w_out, ssm_w_in, ssm_lambda_re, ssm_lambda_im, ssm_log_dt, ssm_b_re, ssm_b_im, ssm_c_re, ssm_c_im, ssm_d, ssm_w_glu, ssm_b_glu, ssm_w_out):
    yp = x_prompt.reshape(BATCH * SEQ, D_MODEL)
    ys = x_sample.reshape(DEC_BATCH * DEC_SEQ, D_MODEL)
    ln = lambda l, k: (ln_g[l, k].reshape(1, D_MODEL), ln_b[l, k].reshape(1, D_MODEL))

    wg, wu, wd = ffn_w_gate.astype(BF16), ffn_w_up.astype(BF16), ffn_w_down.astype(BF16)

    def ffn_pair(yp, ys, l, k, ln_idx):
        g, b = ln(l, ln_idx)
        return _ffn(yp, wg, wu, wd, g, b, l, k), _ffn(ys, wg, wu, wd, g, b, l, k)

    yp, ys = ffn_pair(yp, ys, 0, 0, 0)
    yp, ys, p_cache, s_cache = _attn_layer(yp, ys, cache_a_k[0], cache_a_v[0], cache_b_k[0], cache_b_v[0],
                                           attn_w_in[0], attn_sinks[0], attn_w_out[0], *ln(0, 1),
                                           ffn_args=(wg, wu, wd, *ln(0, 2), 0, 1))
    ys = _ffn(ys, wg, wu, wd, *ln(0, 2), 0, 1)
    yp, ys = ffn_pair(yp, ys, 1, 0, 0)
    yp, ys, p_state, s_state = _ssm_layer(yp, ys, state_c_re[0], state_c_im[0], ssm_w_in[0], ssm_lambda_re[0],
                                          ssm_lambda_im[0], ssm_log_dt[0], ssm_b_re[0], ssm_b_im[0], ssm_c_re[0],
                                          ssm_c_im[0], ssm_d[0], ssm_w_glu[0], ssm_b_glu[0], ssm_w_out[0], *ln(1, 1))
    yp, ys = ffn_pair(yp, ys, 1, 1, 2)
    return (yp.reshape(BATCH, SEQ, D_MODEL), ys.reshape(DEC_BATCH, DEC_SEQ, D_MODEL),
            *p_cache, *p_state, *s_cache, *s_state)
```

```python
import functools
import math

import jax
import jax.numpy as jnp
from jax import lax
from jax.experimental import pallas as pl
from jax.experimental.pallas import tpu as pltpu

F32 = jnp.float32
BF16 = jnp.bfloat16

D_MODEL = 1024
BATCH = 4
SEQ = 4096
DEPTH = 2
DEC_BATCH = 128
DEC_SEQ = 8
PAST_LEN = 16384
HEAD_DIM = 64
N_HEADS_A = 8
DILATIONS = (1, 4, 16)
WIN_A = 2048
N_HEADS_B = 8
N_KV_B = 2
WIN_B = 128
ROPE_THETA = 10000.0
D_A = N_HEADS_A * HEAD_DIM
D_BQ = N_HEADS_B * HEAD_DIM
D_BKV = N_KV_B * HEAD_DIM
D_IN_ATTN = 3 * D_A + D_BQ + 2 * D_BKV
SSM_GROUP = 16
N_SSM_GROUPS = D_MODEL // SSM_GROUP
SSM_STATE = 64
N_STATE = N_SSM_GROUPS * SSM_STATE
D_FF = 2816
DN_ALPHA = (2 * DEPTH) ** 0.25
FFN_RES = 0.5
LN_EPS = 1e-5
ATTN_SCALE = HEAD_DIM ** -0.5
LOG2E = math.log2(math.e)
LSE_LANES = 16

LANES = 128
SUBLANES = 8
MXU_N = 256
VMEM_LIMIT = 56 * 1024 * 1024

ROW_TILE = 512
FFN_ROW_TILE = 1024
FF_CHUNK = MXU_N
TQ = 128
N_SEG = SUBLANES
SEG_LEN = SEQ // N_SEG
N_LCHUNK = D_MODEL // LANES
ST_CHUNK = N_STATE // N_LCHUNK
NK_PAD = WIN_A + LANES
NKB_PAD = 2 * WIN_B

NEG_INF = float("-inf")


def _params(n_axes, vmem=VMEM_LIMIT):
    return pltpu.CompilerParams(dimension_semantics=("arbitrary",) * n_axes, vmem_limit_bytes=vmem)


def _resident(shape):
    return pl.BlockSpec(shape, lambda *_: (0,) * len(shape), pipeline_mode=pl.Buffered(1))


def _layer_norm(x, g, b):
    mu = jnp.mean(x, -1, keepdims=True)
    xc = x - mu
    var = jnp.mean(xc * xc, -1, keepdims=True)
    return xc * lax.rsqrt(var + LN_EPS) * g + b


def _dot(a, b):
    return jnp.dot(a, b, preferred_element_type=F32)


def _dot_nt(a, b):
    return lax.dot_general(a, b, (((1,), (1,)), ((), ())), preferred_element_type=F32)


def _ffn_kernel(x_ref, wg_ref, wu_ref, wd_ref, g_ref, b_ref, o_ref, h_ref):
    x = x_ref[...]
    xb = x.astype(BF16)
    for c in range(D_FF // FF_CHUNK):
        sl = slice(c * FF_CHUNK, (c + 1) * FF_CHUNK)
        gate = _dot(xb, wg_ref[:, sl])
        up = _dot(xb, wu_ref[:, sl])
        h_ref[:, sl] = (gate * jax.nn.sigmoid(gate) * up).astype(BF16)
    y = DN_ALPHA * x + FFN_RES * _dot(h_ref[...], wd_ref[...])
    o_ref[...] = _layer_norm(y, g_ref[...], b_ref[...])


def _ffn(x, wg, wu, wd, g, b, layer=0, which=0):
    n = x.shape[0]
    tm = min(FFN_ROW_TILE, n)
    row = pl.BlockSpec((tm, D_MODEL), lambda i: (i, 0))
    if wg.ndim == 4:
        pick = lambda r, c: pl.BlockSpec((None, None, r, c), lambda i: (layer, which, 0, 0),
                                         pipeline_mode=pl.Buffered(1))
    else:
        pick = lambda r, c: _resident((r, c))
    return pl.pallas_call(
        _ffn_kernel,
        grid=(n // tm,),
        in_specs=[row, pick(D_MODEL, D_FF), pick(D_MODEL, D_FF), pick(D_FF, D_MODEL),
                  _resident((1, D_MODEL)), _resident((1, D_MODEL))],
        out_specs=row,
        out_shape=jax.ShapeDtypeStruct((n, D_MODEL), F32),
        scratch_shapes=[pltpu.VMEM((tm, D_FF), BF16)],
        compiler_params=_params(1),
        name="ffn",
    )(x, wg, wu, wd, g, b)


def _rope_tables(pos):
    half = HEAD_DIM // 2
    inv_freq = ROPE_THETA ** (-jnp.arange(half, dtype=F32) / half)
    ang = pos.astype(F32)[:, None] * inv_freq[None, :]
    cos, sin = jnp.cos(ang), jnp.sin(ang)
    cos_t = jnp.concatenate([cos, cos, cos, cos], -1)
    sin_t = jnp.concatenate([-sin, sin, -sin, sin], -1)
    return cos_t, sin_t


def _attn_proj_kernel(tiles_per_seq, x_ref, w_ref, cos_ref, sin_ref, qa_ref, ka_ref, va_ref, qb_ref, kb_ref, vb_ref,
                      *extra):
    xb = x_ref[...].astype(BF16)
    slab_ref = extra[-1] if extra else None
    dilated = extra[:6]
    n_chunks = D_A // LANES
    tm = x_ref.shape[0]

    def keep(tensor, c, val):
        if slab_ref is not None:
            slab_ref[tensor * n_chunks + c] = val
    cos = cos_ref[...]
    sin = sin_ref[...]
    lane = lax.broadcasted_iota(jnp.int32, cos.shape, 1)
    first_half = (lane & (HEAD_DIM // 2)) == 0

    def rope(z):
        rot = jnp.where(first_half, pltpu.roll(z, LANES - HEAD_DIM // 2, 1), pltpu.roll(z, HEAD_DIM // 2, 1))
        return z * cos + rot * sin

    def project(col0, ncols):
        return _dot(xb, w_ref[:, col0:col0 + ncols])

    def rope_chunks(z):
        return [rope(z[:, c * LANES:(c + 1) * LANES]) for c in range(z.shape[1] // LANES)]

    q_scale = ATTN_SCALE * LOG2E if extra else ATTN_SCALE
    col = 0
    for c, r in enumerate(rope_chunks(project(col, D_A))):
        r = r * q_scale
        qa_ref[:, c * LANES:(c + 1) * LANES] = r.astype(qa_ref.dtype)
        keep(0, c, r)
    col += D_A
    for c, r in enumerate(rope_chunks(project(col, D_A))):
        ka_ref[:, c * LANES:(c + 1) * LANES] = r.astype(ka_ref.dtype)
        keep(1, c, r)
    col += D_A
    z = project(col, D_A)
    va_ref[...] = z.astype(va_ref.dtype)
    for c in range(n_chunks):
        keep(2, c, z[:, c * LANES:(c + 1) * LANES])
    col += D_A
    if extra:
        kat_ref, vat_ref, kbt_ref, vbt_ref = extra[6:10]
        tile_in_seq = pl.program_id(0) % tiles_per_seq

        @pl.when(tile_in_seq >= tiles_per_seq - WIN_A // tm)
        def _():
            for c in range(n_chunks):
                kat_ref[c * LANES:(c + 1) * LANES, :] = slab_ref[n_chunks + c].T
                vat_ref[c * LANES:(c + 1) * LANES, :] = slab_ref[2 * n_chunks + c].T

        slab4_ref = extra[-2]
        d1, d2 = DILATIONS[1], DILATIONS[2] // DILATIONS[1]
        plane = tm // d1
        for tensor in range(3):
            out1_ref, out2_ref = dilated[tensor], dilated[3 + tensor]
            for c in range(n_chunks):
                idx = tensor * n_chunks + c
                lanes = slice(c * LANES, (c + 1) * LANES)
                for r in range(d1):
                    rows = slab_ref[idx, pl.ds(r, plane, stride=d1), :]
                    slab4_ref[idx, r * plane:(r + 1) * plane, :] = rows
                    out1_ref[r, :, lanes] = rows.astype(out1_ref.dtype)
                for r in range(d1):
                    for m in range(d2):
                        rows = slab4_ref[idx, pl.ds(r * plane + m, plane // d2, stride=d2), :]
                        out2_ref[r + d1 * m, :, lanes] = rows.astype(out2_ref.dtype)
    for c, r in enumerate(rope_chunks(project(col, D_BQ))):
        qb_ref[:, c * LANES:(c + 1) * LANES] = (r * q_scale).astype(qb_ref.dtype)
    col += D_BQ
    z = project(col, 2 * D_BKV)
    r = rope(z[:, :D_BKV])
    kb_ref[...] = r.astype(kb_ref.dtype)
    vb_ref[...] = z[:, D_BKV:].astype(vb_ref.dtype)
    if extra:
        @pl.when(tile_in_seq == tiles_per_seq - 1)
        def _():
            kbt_ref[...] = r[tm - WIN_B:, :].T
            vbt_ref[...] = z[tm - WIN_B:, D_BKV:].T


def _attn_proj(x, w, cos_t, sin_t, table_block, act_dtype, prompt_seq=None):
    n = x.shape[0]
    tm = min(ROW_TILE, n)

    def row(width):
        return pl.BlockSpec((tm, width), lambda i: (i, 0))

    tab = pl.BlockSpec((tm, LANES), lambda i: (table_block(i), 0))
    widths = (D_A, D_A, D_A, D_BQ, D_BKV, D_BKV)
    out_shape = [jax.ShapeDtypeStruct((n, wd), act_dtype) for wd in widths]
    out_specs = [row(wd) for wd in widths]
    scratch = []
    tps = None
    if prompt_seq is not None:
        bsz, seq = prompt_seq
        tps = seq // tm
        for dil in DILATIONS[1:]:
            out_shape += [jax.ShapeDtypeStruct((bsz, dil, seq // dil, D_A), BF16)] * 3
            out_specs += [pl.BlockSpec((None, dil, tm // dil, D_A), lambda i: (i // tps, 0, i % tps, 0))] * 3
        first_tail = tps - WIN_A // tm
        out_shape += [jax.ShapeDtypeStruct((bsz, D_A, WIN_A), F32)] * 2
        out_specs += [pl.BlockSpec((None, D_A, tm), lambda i: (i // tps, 0, jnp.maximum(i % tps - first_tail, 0)))] * 2
        out_shape += [jax.ShapeDtypeStruct((bsz, D_BKV, WIN_B), F32)] * 2
        out_specs += [pl.BlockSpec((None, D_BKV, WIN_B), lambda i: (i // tps, 0, 0))] * 2
        assert DILATIONS[2] == DILATIONS[1] ** 2
        scratch = [pltpu.VMEM((3 * D_A // LANES, tm, LANES), F32)] * 2
    return pl.pallas_call(
        functools.partial(_attn_proj_kernel, tps),
        grid=(n // tm,),
        in_specs=[row(D_MODEL), _resident((D_MODEL, D_IN_ATTN)), tab, tab],
        out_specs=out_specs,
        out_shape=out_shape,
        scratch_shapes=scratch,
        compiler_params=_params(1),
        name="attn_proj",
    )(x, w, cos_t, sin_t)


def _lane_lo(shape):
    return lax.broadcasted_iota(jnp.int32, shape, 1) < HEAD_DIM


def _half_masks_bf16():
    lo = jnp.where(_lane_lo((1, LANES)), 1.0, 0.0).astype(BF16)
    return lo, 1 - lo


def _band_masks(n_heads, t, sub, prev_strict):
    row = lax.broadcasted_iota(jnp.int32, (n_heads * TQ, TQ), 0) & (TQ - 1)
    col = lax.broadcasted_iota(jnp.int32, (n_heads * TQ, TQ), 1)
    mask_c = col <= row
    shift = jnp.where(t > 0, 0, TQ) if sub == 0 else 0
    mask_p = (col > row + shift) if prev_strict else (col >= row + shift)
    return mask_c, mask_p


def _sub_tile_kv(sub, sl, kc_ref, kp_ref, vc_ref, vp_ref):
    if sub == 0:
        return kc_ref[0:TQ, sl], kp_ref[:, sl], vc_ref[0:TQ, sl], vp_ref[:, sl]
    return kc_ref[TQ:2 * TQ, sl], kc_ref[0:TQ, sl], vc_ref[TQ:2 * TQ, sl], vc_ref[0:TQ, sl]


def _band_softmax(qs, kc, kp, vc, vp, mask_c, mask_p, sink=None):
    s_c = jnp.where(mask_c, _dot_nt(qs, kc), NEG_INF)
    s_p = jnp.where(mask_p, _dot_nt(qs, kp), NEG_INF)
    m = jnp.max(jnp.maximum(s_c, s_p), -1, keepdims=True)
    if sink is not None:
        m = jnp.maximum(m, sink)
    p_c = jnp.exp2(s_c - m)
    p_p = jnp.exp2(s_p - m)
    den = jnp.sum(p_c + p_p, -1, keepdims=True)
    if sink is not None:
        den = den + jnp.exp2(sink - m)
    acc = _dot(p_c.astype(BF16), vc) + _dot(p_p.astype(BF16), vp)
    return acc * (1.0 / den), m, den


def _band_a_kernel(q_ref, kc_ref, kp_ref, vc_ref, vp_ref, o_ref, lse_ref):
    t = pl.program_id(2)
    lo = _lane_lo((TQ, LANES))
    lo_bf, hi_bf = _half_masks_bf16()
    lane_head = jnp.right_shift(lax.broadcasted_iota(jnp.int32, (TQ, LANES), 1), int(math.log2(LSE_LANES)))
    for sub in range(2):
        rows = slice(sub * TQ, (sub + 1) * TQ)
        mask_c, mask_p = _band_masks(2, t, sub, prev_strict=False)
        lse_tile = jnp.zeros((TQ, LANES), F32)
        for c in range(D_A // LANES):
            sl = slice(c * LANES, (c + 1) * LANES)
            kc, kp, vc, vp = _sub_tile_kv(sub, sl, kc_ref, kp_ref, vc_ref, vp_ref)
            q2 = q_ref[rows, sl]
            qs = jnp.concatenate([q2 * lo_bf, q2 * hi_bf], axis=0)
            out, m, den = _band_softmax(qs, kc, kp, vc, vp, mask_c, mask_p)
            lse = m + jnp.log2(den)
            o_ref[rows, sl] = jnp.where(lo, out[0:TQ], out[TQ:])
            lse_tile = jnp.where(lane_head == 2 * c, lse[0:TQ], lse_tile)
            lse_tile = jnp.where(lane_head == 2 * c + 1, lse[TQ:], lse_tile)
        lse_ref[rows, :] = lse_tile


def _band_a(q, k, v):
    bsz, dil, sub, _ = q.shape
    cur = pl.BlockSpec((None, None, 2 * TQ, D_A), lambda b, r, t: (b, r, t, 0))
    prev = pl.BlockSpec((None, None, TQ, D_A), lambda b, r, t: (b, r, jnp.maximum(2 * t - 1, 0), 0))
    lse = pl.BlockSpec((None, None, 2 * TQ, LANES), lambda b, r, t: (b, r, t, 0))
    return pl.pallas_call(
        _band_a_kernel,
        grid=(bsz, dil, sub // (2 * TQ)),
        in_specs=[cur, cur, prev, cur, prev],
        out_specs=[cur, lse],
        out_shape=[jax.ShapeDtypeStruct((bsz, dil, sub, D_A), F32),
                   jax.ShapeDtypeStruct((bsz, dil, sub, LANES), F32)],
        compiler_params=_params(3),
        name=f"band_a_d{dil}",
    )(q, k, k, v, v)


def _swap_halves(x):
    return jnp.concatenate([x[:, HEAD_DIM:], x[:, :HEAD_DIM]], axis=1)


def _band_b_kernel(sink_ref, q_ref, kc_ref, kp_ref, vc_ref, vp_ref, o_ref):
    t = pl.program_id(1)
    group = N_HEADS_B // N_KV_B
    lo = _lane_lo((TQ, LANES))
    lo_bf, hi_bf = _half_masks_bf16()
    for sub in range(2):
        rows = slice(sub * TQ, (sub + 1) * TQ)
        mask_c, mask_p = _band_masks(group, t, sub, prev_strict=True)
        kv = _sub_tile_kv(sub, slice(0, D_BKV), kc_ref, kp_ref, vc_ref, vp_ref)
        kv_swapped = tuple(_swap_halves(a) for a in kv)
        for g in range(N_KV_B):
            own, other = (lo_bf, hi_bf) if g == 0 else (hi_bf, lo_bf)
            kc, kp, vc, vp = (a * own + a_sw * other for a, a_sw in zip(kv, kv_swapped))
            heads = range(g * group, (g + 1) * group)
            qs = jnp.concatenate(
                [q_ref[rows, (h // 2) * LANES:(h // 2 + 1) * LANES] * (lo_bf if h % 2 == 0 else hi_bf) for h in heads],
                axis=0)
            sink = jnp.concatenate([jnp.full((TQ, 1), sink_ref[h] * LOG2E, F32) for h in heads], axis=0)
            out, _, _ = _band_softmax(qs, kc, kp, vc, vp, mask_c, mask_p, sink)
            for i in range(group // 2):
                c = g * (group // 2) + i
                even, odd = out[2 * i * TQ:(2 * i + 1) * TQ], out[(2 * i + 1) * TQ:(2 * i + 2) * TQ]
                o_ref[rows, c * LANES:(c + 1) * LANES] = jnp.where(lo, even, odd).astype(o_ref.dtype)


def _band_b(q, k, v, sinks):
    bsz, seq, _ = q.shape
    qs = pl.BlockSpec((None, 2 * TQ, D_BQ), lambda b, t: (b, t, 0))
    cur = pl.BlockSpec((None, 2 * TQ, D_BKV), lambda b, t: (b, t, 0))
    prev = pl.BlockSpec((None, TQ, D_BKV), lambda b, t: (b, jnp.maximum(2 * t - 1, 0), 0))
    o = pl.pallas_call(
        _band_b_kernel,
        grid=(bsz, seq // (2 * TQ)),
        in_specs=[pl.BlockSpec(memory_space=pltpu.SMEM), qs, cur, prev, cur, prev],
        out_specs=qs,
        out_shape=jax.ShapeDtypeStruct((bsz, seq, D_BQ), BF16),
        compiler_params=_params(2),
        name="band_b",
    )(sinks, q, k, k, v, v)
    return o.reshape(bsz * seq, D_BQ)


def _pattern_count(dist):
    cnt = jnp.zeros(dist.shape, F32)
    for dil in DILATIONS:
        cnt = cnt + ((dist >= 0) & (dist <= 128 * dil) & (dist % dil == 0)).astype(F32)
    return cnt


def _sample_tables():
    i = jnp.arange(DEC_SEQ)
    cnt_c = _pattern_count(WIN_A + i[:, None] - jnp.arange(WIN_A)[None, :])
    j = jnp.arange(LANES)
    cnt_n = jnp.where(j[None, :] < DEC_SEQ, _pattern_count(i[:, None] - j[None, :]), 0.0)
    cnt_n = jnp.tile(cnt_n, (N_HEADS_A, 1))
    jb = jnp.arange(NKB_PAD)[None, :]
    dist_b = WIN_B + i[:, None] - jb
    ok_b = (dist_b >= 0) & (dist_b < WIN_B) & (jb < WIN_B + DEC_SEQ)
    mask_b = jnp.tile(ok_b.astype(F32), (N_HEADS_B, 1))
    return cnt_c, cnt_n, mask_b


def _sample_attend(q, kan, van, kt_ref, vt_ref, cnt_c, cnt_n, qb, kbn, vbn, kbc, vbc, mask_b, sink_col, kb_s, vb_s):
    rows = N_HEADS_A * DEC_SEQ
    q_rep = jnp.concatenate([q] * N_HEADS_A, axis=0)
    row_head = jnp.right_shift(lax.broadcasted_iota(jnp.int32, (rows, D_A), 0), int(math.log2(DEC_SEQ)))
    lane_head = jnp.right_shift(lax.broadcasted_iota(jnp.int32, (rows, D_A), 1), int(math.log2(HEAD_DIM)))
    own = row_head == lane_head
    q_bd = jnp.where(own, q_rep, 0.0).astype(BF16)
    pad = jnp.zeros((LANES - DEC_SEQ, D_A), F32)
    kn = jnp.concatenate([kan, pad], 0).astype(BF16)
    vn = jnp.concatenate([van, pad], 0).astype(BF16)
    s_new = jnp.where(cnt_n > 0.0, _dot_nt(q_bd, kn), NEG_INF)
    outs, p_new = [], []
    for h in range(N_HEADS_A):
        head_rows = slice(h * DEC_SEQ, (h + 1) * DEC_SEQ)
        q_h = q[:, h * HEAD_DIM:(h + 1) * HEAD_DIM].astype(BF16)
        s_c = jnp.where(cnt_c > 0.0, _dot(q_h, kt_ref[h].astype(BF16)), NEG_INF)
        s_n = s_new[head_rows]
        m = jnp.maximum(jnp.max(s_c, -1, keepdims=True), jnp.max(s_n, -1, keepdims=True))
        p_c = jnp.exp(s_c - m) * cnt_c
        p_n = jnp.exp(s_n - m) * cnt_n[head_rows]
        inv = 1.0 / (jnp.sum(p_c, -1, keepdims=True) + jnp.sum(p_n, -1, keepdims=True))
        outs.append(_dot_nt(p_c.astype(BF16), vt_ref[h].astype(BF16)) * inv)
        p_new.append(p_n * inv)
    out_n = jnp.where(own, _dot(jnp.concatenate(p_new, axis=0).astype(BF16), vn), 0.0)
    oa = jnp.concatenate(outs, axis=1)
    for h in range(N_HEADS_A):
        oa = oa + out_n[h * DEC_SEQ:(h + 1) * DEC_SEQ]

    n_pad_b = NKB_PAD - WIN_B - DEC_SEQ
    pad_b = jnp.zeros((n_pad_b, D_BKV), F32)
    kb_s[...] = jnp.concatenate([kbc, kbn, pad_b], 0).astype(BF16)
    vb_s[...] = jnp.concatenate([vbc, vbn, pad_b], 0).astype(BF16)
    lo8 = _lane_lo((DEC_SEQ, LANES))
    group = N_HEADS_B // N_KV_B
    pieces = []
    for h in range(N_HEADS_B):
        chunk = qb[:, (h // 2) * LANES:(h // 2 + 1) * LANES]
        g = h // group
        if h % 2 != g:
            chunk = pltpu.roll(chunk, HEAD_DIM, 1)
        pieces.append(jnp.where(lo8 if g == 0 else jnp.logical_not(lo8), chunk, 0.0))
    qb_bd = jnp.concatenate(pieces, axis=0).astype(BF16)
    sb = jnp.where(mask_b > 0.0, _dot_nt(qb_bd, kb_s[...]), NEG_INF)
    sink = sink_col[:, 0:1]
    mb = jnp.maximum(jnp.max(sb, -1, keepdims=True), sink)
    pb = jnp.exp(sb - mb) * mask_b
    den_b = jnp.sum(pb, -1, keepdims=True) + jnp.exp(sink - mb)
    ob_full = _dot(pb.astype(BF16), vb_s[...]) * (1.0 / den_b)
    ob = []
    for c in range(D_BQ // LANES):
        halves = []
        for half in range(2):
            h = 2 * c + half
            piece = ob_full[h * DEC_SEQ:(h + 1) * DEC_SEQ]
            if half != h // group:
                piece = pltpu.roll(piece, HEAD_DIM, 1)
            halves.append(piece)
        ob.append(jnp.where(lo8, halves[0], halves[1]))
    return jnp.concatenate([oa] + ob, axis=1)


FUSED_ROW_TILE = 512
SEQ_PER_STEP = DEC_BATCH // (BATCH * SEQ // FUSED_ROW_TILE)


def _kv_copies(kt_hbm, vt_hbm, kbuf, vbuf, sems, seq, slot):
    return (pltpu.make_async_copy(kt_hbm.at[seq], kbuf.at[slot], sems.at[0, slot]),
            pltpu.make_async_copy(vt_hbm.at[seq], vbuf.at[slot], sems.at[1, slot]))


def _ffn_attn_kernel(x_ref, wg_ref, wu_ref, wd_ref, g_ref, b_ref,
                     qa_ref, kan_ref, van_ref, kt_hbm, vt_hbm, cnt_c_ref, cnt_n_ref,
                     qb_ref, kbn_ref, vbn_ref, kbc_ref, vbc_ref, maskb_ref, sinkcol_ref,
                     y_ref, o_ref, h_ref, kbuf, vbuf, sems, kb_s, vb_s):
    step = pl.program_id(0)
    n_steps = pl.num_programs(0)
    copies = functools.partial(_kv_copies, kt_hbm, vt_hbm, kbuf, vbuf, sems)

    @pl.when(step == 0)
    def _():
        for cp in copies(0, 0):
            cp.start()

    x = x_ref[...]
    xb = x.astype(BF16)
    n_ff = D_FF // FF_CHUNK
    per_seq = -(-n_ff // SEQ_PER_STEP)
    cnt_c, cnt_n, mask_b, sink_col = cnt_c_ref[...], cnt_n_ref[...], maskb_ref[...], sinkcol_ref[...]
    for s in range(SEQ_PER_STEP):
        slot = s % 2
        seq = step * SEQ_PER_STEP + s
        for cp in copies(seq, slot):
            cp.wait()
        if s + 1 < SEQ_PER_STEP:
            for cp in copies(seq + 1, 1 - slot):
                cp.start()
        else:
            @pl.when(step + 1 < n_steps)
            def _():
                for cp in copies(seq + 1, 1 - slot):
                    cp.start()
        rows = slice(s * DEC_SEQ, (s + 1) * DEC_SEQ)
        o_ref[rows, :] = _sample_attend(
            qa_ref[rows, :], kan_ref[rows, :], van_ref[rows, :], kbuf.at[slot], vbuf.at[slot], cnt_c, cnt_n,
            qb_ref[rows, :], kbn_ref[rows, :], vbn_ref[rows, :], kbc_ref[s], vbc_ref[s], mask_b, sink_col, kb_s, vb_s)
        for c in range(s * per_seq, min((s + 1) * per_seq, n_ff)):
            sl = slice(c * FF_CHUNK, (c + 1) * FF_CHUNK)
            gate = _dot(xb, wg_ref[:, sl])
            up = _dot(xb, wu_ref[:, sl])
            h_ref[:, sl] = (gate * jax.nn.sigmoid(gate) * up).astype(BF16)
    y = DN_ALPHA * x + FFN_RES * _dot(h_ref[...], wd_ref[...])
    y_ref[...] = _layer_norm(y, g_ref[...], b_ref[...])


def _ffn_with_sample_attention(x, wg, wu, wd, g, b, layer, which,
                               qa, kan, van, cache_ak, cache_av, qb, kbn, vbn, cache_bk, cache_bv, sinks):
    n = x.shape[0]
    tm = FUSED_ROW_TILE
    assert n // tm * SEQ_PER_STEP == DEC_BATCH and SEQ_PER_STEP % 2 == 0
    cnt_c, cnt_n, mask_b = _sample_tables()
    sink_col = jnp.broadcast_to(jnp.repeat(sinks.astype(F32), DEC_SEQ)[:, None], (N_HEADS_B * DEC_SEQ, LANES))
    row = pl.BlockSpec((tm, D_MODEL), lambda i: (i, 0))
    pick = lambda r, c: pl.BlockSpec((None, None, r, c), lambda i: (layer, which, 0, 0), pipeline_mode=pl.Buffered(1))
    new = lambda width: pl.BlockSpec((SEQ_PER_STEP * DEC_SEQ, width), lambda i: (i, 0))
    cache_b = pl.BlockSpec((SEQ_PER_STEP, WIN_B, D_BKV), lambda i: (i, 0, 0))
    hbm = pl.BlockSpec(memory_space=pl.ANY)
    rows = N_HEADS_A * DEC_SEQ
    kv_slot = (2, N_HEADS_A, HEAD_DIM, WIN_A)
    return pl.pallas_call(
        _ffn_attn_kernel,
        grid=(n // tm,),
        in_specs=[row, pick(D_MODEL, D_FF), pick(D_MODEL, D_FF), pick(D_FF, D_MODEL),
                  _resident((1, D_MODEL)), _resident((1, D_MODEL)),
                  new(D_A), new(D_A), new(D_A), hbm, hbm, _resident(cnt_c.shape), _resident(cnt_n.shape),
                  new(D_BQ), new(D_BKV), new(D_BKV), cache_b, cache_b,
                  _resident((rows, NKB_PAD)), _resident((rows, LANES))],
        out_specs=[row, new(D_A + D_BQ)],
        out_shape=[jax.ShapeDtypeStruct((n, D_MODEL), F32),
                   jax.ShapeDtypeStruct((DEC_BATCH * DEC_SEQ, D_A + D_BQ), F32)],
        scratch_shapes=[pltpu.VMEM((tm, D_FF), BF16), pltpu.VMEM(kv_slot, F32), pltpu.VMEM(kv_slot, F32),
                        pltpu.SemaphoreType.DMA((2, 2)),
                        pltpu.VMEM((NKB_PAD, D_BKV), BF16), pltpu.VMEM((NKB_PAD, D_BKV), BF16)],
        compiler_params=_params(1, vmem=60 * 1024 * 1024),
        name="ffn_attn_sample",
    )(x, wg, wu, wd, g, b, qa, kan, van, cache_ak, cache_av, cnt_c, cnt_n, qb, kbn, vbn, cache_bk, cache_bv,
      mask_b, sink_col)


def _attn_out_prompt_kernel(o1_ref, l1_ref, o4_ref, l4_ref, o16_ref, l16_ref, ob_ref, y_ref, w_ref, g_ref, b_ref,
                            out_ref, slab_ref, oa_ref):
    tm = y_ref.shape[0]
    n_chunks = D_A // LANES
    slabs = {}
    base = 0
    for name, dil, src, width in (("o4", 4, o4_ref, n_chunks), ("l4", 4, l4_ref, 1),
                                  ("o16", 16, o16_ref, n_chunks), ("l16", 16, l16_ref, 1)):
        slabs[name] = base
        for r in range(dil):
            for c in range(width):
                slab_ref[base + c, pl.ds(r, tm // dil, stride=dil), :] = src[r, :, c * LANES:(c + 1) * LANES]
        base += width
    l1, l4, l16 = l1_ref[...], slab_ref[slabs["l4"]], slab_ref[slabs["l16"]]
    m = jnp.maximum(jnp.maximum(l1, l4), l16)
    e1, e4, e16 = jnp.exp2(l1 - m), jnp.exp2(l4 - m), jnp.exp2(l16 - m)
    inv = 1.0 / (e1 + e4 + e16)
    row = lax.broadcasted_iota(jnp.int32, (LANES, D_A), 0)
    head_of_col = jnp.right_shift(lax.broadcasted_iota(jnp.int32, (LANES, D_A), 1), int(math.log2(HEAD_DIM)))
    spread = jnp.where(row == head_of_col * LSE_LANES, 1.0, 0.0).astype(BF16)

    def per_head_lanes(w):
        hi = w.astype(BF16)
        lo = (w - hi.astype(F32)).astype(BF16)
        return _dot(hi, spread) + _dot(lo, spread)

    w1, w4, w16 = per_head_lanes(e1 * inv), per_head_lanes(e4 * inv), per_head_lanes(e16 * inv)
    for c in range(n_chunks):
        sl = slice(c * LANES, (c + 1) * LANES)
        oa = (w1[:, sl] * o1_ref[:, sl] + w4[:, sl] * slab_ref[slabs["o4"] + c]
              + w16[:, sl] * slab_ref[slabs["o16"] + c])
        oa_ref[:, sl] = oa.astype(BF16)
    mix = _dot(oa_ref[...], w_ref[0:D_A, :]) + _dot(ob_ref[...], w_ref[D_A:, :])
    out_ref[...] = _layer_norm(DN_ALPHA * y_ref[...] + mix, g_ref[...], b_ref[...])


def _attn_out_prompt(pats, ob, y, w, g, b, seq):
    n = y.shape[0]
    tm = ROW_TILE
    tps = seq // tm
    half = pl.BlockSpec((tm, D_A), lambda i: (i, 0))
    full = pl.BlockSpec((tm, D_MODEL), lambda i: (i, 0))
    lse1 = pl.BlockSpec((tm, LANES), lambda i: (i, 0))
    planes = lambda dil, width: pl.BlockSpec((None, dil, tm // dil, width), lambda i: (i // tps, 0, i % tps, 0))
    (o1, l1), (o4, l4), (o16, l16) = pats
    return pl.pallas_call(
        _attn_out_prompt_kernel,
        grid=(n // tm,),
        in_specs=[half, lse1, planes(4, D_A), planes(4, LANES), planes(16, D_A), planes(16, LANES), half, full,
                  _resident((D_MODEL, D_MODEL)), _resident((1, D_MODEL)), _resident((1, D_MODEL))],
        out_specs=full,
        out_shape=jax.ShapeDtypeStruct((n, D_MODEL), F32),
        scratch_shapes=[pltpu.VMEM((2 * (D_A // LANES + 1), tm, LANES), F32), pltpu.VMEM((tm, D_A), BF16)],
        compiler_params=_params(1),
        name="attn_out_prompt",
    )(o1, l1, o4, l4, o16, l16, ob, y, w, g, b)


def _mix_out_kernel(o_ref, y_ref, w_ref, g_ref, b_ref, out_ref):
    mix = _dot(o_ref[...].astype(BF16), w_ref[...])
    out_ref[...] = _layer_norm(DN_ALPHA * y_ref[...] + mix, g_ref[...], b_ref[...])


def _attn_out_sample(o, y, w, g, b):
    n = y.shape[0]
    tm = min(ROW_TILE, n)
    full = pl.BlockSpec((tm, D_MODEL), lambda i: (i, 0))
    return pl.pallas_call(
        _mix_out_kernel,
        grid=(n // tm,),
        in_specs=[full, full, _resident((D_MODEL, D_MODEL)), _resident((1, D_MODEL)), _resident((1, D_MODEL))],
        out_specs=full,
        out_shape=jax.ShapeDtypeStruct((n, D_MODEL), F32),
        compiler_params=_params(1),
        name="attn_out_sample",
    )(o, y, w, g, b)


def _ssm_discretize(lam_re, lam_im, log_dt, b_re, b_im):
    dt = jnp.exp(log_dt.astype(F32))[:, None]
    lr, li = lam_re.astype(F32), lam_im.astype(F32)
    mag = jnp.exp(lr * dt)
    ab_re, ab_im = mag * jnp.cos(li * dt), mag * jnp.sin(li * dt)
    nr, ni = ab_re - 1.0, ab_im
    den = lr * lr + li * li
    fr, fi = (nr * lr + ni * li) / den, (ni * lr - nr * li) / den
    bb_re = fr[..., None] * b_re - fi[..., None] * b_im
    bb_im = fr[..., None] * b_im + fi[..., None] * b_re
    return ab_re, ab_im, bb_re, bb_im


def _ssm_matrices(bb_re, bb_im, c_re, c_im):
    gpc = LANES // SSM_GROUP
    eye = jnp.eye(gpc, dtype=F32)

    def in_blocks(bb):
        a = bb.reshape(N_LCHUNK, gpc, SSM_STATE, SSM_GROUP)
        return jnp.einsum("jgpn,gh->jgnhp", a, eye).reshape(N_LCHUNK, LANES, ST_CHUNK)

    def out_blocks(cc):
        a = cc.reshape(N_LCHUNK, gpc, SSM_GROUP, SSM_STATE)
        return jnp.einsum("jgnp,gh->jgphn", a, eye).reshape(N_LCHUNK, ST_CHUNK, LANES)

    bmat = jnp.concatenate([in_blocks(bb_re), in_blocks(bb_im)], -1)
    cmat = jnp.concatenate([out_blocks(c_re), -out_blocks(c_im)], 1)
    return bmat, cmat


CHUNK = 8


def _dot3(a, b):
    a_hi, b_hi = a.astype(BF16), b.astype(BF16)
    a_lo, b_lo = (a - a_hi.astype(F32)).astype(BF16), (b - b_hi.astype(F32)).astype(BF16)
    return _dot(a_hi, b_hi) + _dot(a_hi, b_lo) + _dot(a_lo, b_hi)


def _chunk_weights_kernel(ar_ref, ai_ref, acr_ref, aci_ref, bmat_ref, cmat_ref,
                          we_ref, ws_ref, wi_ref, a8r_ref, a8i_ref):
    def powers(r, i, n):
        out = [(jnp.ones_like(r), jnp.zeros_like(r))]
        for _ in range(n):
            out.append(_cmul(out[-1][0], out[-1][1], r, i))
        return out

    row_pow = powers(ar_ref[...], ai_ref[...], CHUNK)
    col_pow = powers(acr_ref[...], aci_ref[...], CHUNK)
    b_re, b_im = bmat_ref[:, 0:ST_CHUNK], bmat_ref[:, ST_CHUNK:]
    c_re, c_im = cmat_ref[0:ST_CHUNK, :], -cmat_ref[ST_CHUNK:, :]

    def scaled_b(power):
        pr, pi = row_pow[power]
        return jnp.concatenate([b_re * pr - b_im * pi, b_re * pi + b_im * pr], axis=1)

    taps = []
    for tau in range(CHUNK):
        sb = scaled_b(tau)
        we_ref[(CHUNK - 1 - tau) * LANES:(CHUNK - tau) * LANES, :] = sb.astype(BF16)
        taps.append(_dot3(sb, cmat_ref[...]).astype(BF16))
    for k in range(CHUNK):
        cols = slice(k * LANES, (k + 1) * LANES)
        qr, qi = col_pow[k + 1]
        ws_ref[0:ST_CHUNK, cols] = (c_re * qr - c_im * qi).astype(BF16)
        ws_ref[ST_CHUNK:, cols] = (-(c_re * qi + c_im * qr)).astype(BF16)
    zero = jnp.zeros((LANES, LANES), BF16)
    for k_in in range(CHUNK):
        for k_out in range(CHUNK):
            wi_ref[k_in * LANES:(k_in + 1) * LANES, k_out * LANES:(k_out + 1) * LANES] = (
                taps[k_out - k_in] if k_out >= k_in else zero)
    a8r_ref[...], a8i_ref[...] = row_pow[CHUNK]


def _chunk_weights(a_re, a_im, bmat, cmat):
    cols = lambda a: jnp.broadcast_to(a.reshape(N_LCHUNK, ST_CHUNK, 1), (N_LCHUNK, ST_CHUNK, LANES))
    per_j = lambda r, c: pl.BlockSpec((None, r, c), lambda j: (j, 0, 0))
    wide = 2 * ST_CHUNK
    return pl.pallas_call(
        _chunk_weights_kernel,
        grid=(N_LCHUNK,),
        in_specs=[per_j(1, ST_CHUNK), per_j(1, ST_CHUNK), per_j(ST_CHUNK, LANES), per_j(ST_CHUNK, LANES),
                  per_j(LANES, wide), per_j(wide, LANES)],
        out_specs=[per_j(CHUNK * LANES, wide), per_j(wide, CHUNK * LANES), per_j(CHUNK * LANES, CHUNK * LANES),
                   per_j(1, ST_CHUNK), per_j(1, ST_CHUNK)],
        out_shape=[jax.ShapeDtypeStruct((N_LCHUNK, CHUNK * LANES, wide), BF16),
                   jax.ShapeDtypeStruct((N_LCHUNK, wide, CHUNK * LANES), BF16),
                   jax.ShapeDtypeStruct((N_LCHUNK, CHUNK * LANES, CHUNK * LANES), BF16),
                   jax.ShapeDtypeStruct((N_LCHUNK, 1, ST_CHUNK), F32),
                   jax.ShapeDtypeStruct((N_LCHUNK, 1, ST_CHUNK), F32)],
        compiler_params=_params(1),
        name="ssm_chunk_weights",
    )(a_re, a_im, cols(a_re), cols(a_im), bmat, cmat)


SEG_TILE = ROW_TILE // N_SEG


def _ssm_in_prompt_kernel(x_ref, w_ref, o_ref):
    x = x_ref[...].reshape(N_SEG * SEG_TILE, D_MODEL)
    u = _dot(x.astype(BF16), w_ref[...])
    for s in range(N_SEG):
        for c in range(N_LCHUNK):
            o_ref[c, pl.ds(s, SEG_TILE, stride=N_SEG), :] = u[s * SEG_TILE:(s + 1) * SEG_TILE, c * LANES:(c + 1) * LANES]


def _ssm_in_prompt(y, w):
    return pl.pallas_call(
        _ssm_in_prompt_kernel,
        grid=(BATCH, SEG_LEN // SEG_TILE),
        in_specs=[pl.BlockSpec((None, N_SEG, SEG_TILE, D_MODEL), lambda b, t: (b, 0, t, 0)),
                  _resident((D_MODEL, D_MODEL))],
        out_specs=pl.BlockSpec((None, N_LCHUNK, ROW_TILE, LANES), lambda b, t: (b, 0, t, 0)),
        out_shape=jax.ShapeDtypeStruct((BATCH, N_LCHUNK, SEQ, LANES), F32),
        compiler_params=_params(2),
        name="ssm_in_prompt",
    )(y.reshape(BATCH, N_SEG, SEG_LEN, D_MODEL), w)


def _ssm_in_sample_kernel(x_ref, w_ref, o_ref, slab_ref):
    u = _dot(x_ref[...].astype(BF16), w_ref[...])
    for c in range(N_LCHUNK):
        slab_ref[c] = u[:, c * LANES:(c + 1) * LANES]
    for l in range(DEC_SEQ):
        for c in range(N_LCHUNK):
            o_ref[c, l * DEC_BATCH:(l + 1) * DEC_BATCH, :] = slab_ref[c, pl.ds(l, DEC_BATCH, stride=DEC_SEQ), :]


def _ssm_in_sample(y, w):
    n = DEC_BATCH * DEC_SEQ
    return pl.pallas_call(
        _ssm_in_sample_kernel,
        grid=(1,),
        in_specs=[_resident((n, D_MODEL)), _resident((D_MODEL, D_MODEL))],
        out_specs=pl.BlockSpec((N_LCHUNK, n, LANES), lambda i: (0, 0, 0)),
        out_shape=jax.ShapeDtypeStruct((N_LCHUNK, n, LANES), F32),
        scratch_shapes=[pltpu.VMEM((N_LCHUNK, n, LANES), F32)],
        compiler_params=_params(1),
        name="ssm_in_sample",
    )(y, w)


def _cmul(ar, ai, br, bi):
    return ar * br - ai * bi, ar * bi + ai * br


def _scan_prompt_kernel(u_ref, we_ref, ws_ref, wi_ref, a8r_ref, a8i_ref, d_ref, h0r_ref, h0i_ref,
                        y_ref, hnr_ref, hni_ref, e_s, hs_s):
    n_chunks = u_ref.shape[0]
    rows = n_chunks * N_SEG
    u_flat = jnp.concatenate([u_ref[:, k].reshape(rows, LANES) for k in range(CHUNK)], axis=1)
    ub = u_flat.astype(BF16)
    e_s[...] = _dot(ub, we_ref[...])
    a8r1, a8i1 = a8r_ref[...], a8i_ref[...]
    a8r = jnp.broadcast_to(a8r1, (N_SEG, ST_CHUNK))
    a8i = jnp.broadcast_to(a8i1, (N_SEG, ST_CHUNK))

    def advance(row, hr, hi):
        er = e_s[pl.ds(row, N_SEG), 0:ST_CHUNK]
        ei = e_s[pl.ds(row, N_SEG), ST_CHUNK:2 * ST_CHUNK]
        return a8r * hr - a8i * hi + er, a8r * hi + a8i * hr + ei

    def pass1(c, carry):
        return advance(pl.multiple_of(c * N_SEG, N_SEG), *carry)

    zero = jnp.zeros((N_SEG, ST_CHUNK), F32)
    er, ei = lax.fori_loop(0, n_chunks, pass1, (zero, zero), unroll=8)

    pr, pi = a8r1, a8i1
    for _ in range(int(math.log2(n_chunks))):
        pr, pi = _cmul(pr, pi, pr, pi)
    hr, hi = h0r_ref[...], h0i_ref[...]
    starts_r, starts_i = [], []
    for s in range(N_SEG):
        starts_r.append(hr)
        starts_i.append(hi)
        gr, gi = _cmul(pr, pi, hr, hi)
        hr, hi = gr + er[s:s + 1], gi + ei[s:s + 1]
    hnr_ref[...] = hr
    hni_ref[...] = hi
    init = (jnp.concatenate(starts_r, 0), jnp.concatenate(starts_i, 0))

    def pass2(c, carry):
        row = pl.multiple_of(c * N_SEG, N_SEG)
        hs_s[pl.ds(row, N_SEG), 0:ST_CHUNK] = carry[0]
        hs_s[pl.ds(row, N_SEG), ST_CHUNK:2 * ST_CHUNK] = carry[1]
        return advance(row, *carry)

    lax.fori_loop(0, n_chunks, pass2, init, unroll=8)
    y = _dot(hs_s[...].astype(BF16), ws_ref[...]) + _dot(ub, wi_ref[...]) + d_ref[...] * u_flat
    for k in range(CHUNK):
        y_ref[:, k] = y[:, k * LANES:(k + 1) * LANES].reshape(n_chunks, N_SEG, LANES)


def _scan_prompt(u, weights, d_skip, h0r, h0i):
    bsz, _, seq, _ = u.shape
    n_chunks = seq // (CHUNK * N_SEG)
    rows = n_chunks * N_SEG
    wide = 2 * ST_CHUNK
    split = lambda a: a.reshape(bsz, N_LCHUNK, n_chunks, CHUNK, N_SEG, LANES)
    chunk = pl.BlockSpec((None, None, n_chunks, CHUNK, N_SEG, LANES), lambda j, b: (b, j, 0, 0, 0, 0))
    per_j = lambda r, c: pl.BlockSpec((None, r, c), lambda j, b: (j, 0, 0))
    state = pl.BlockSpec((None, 1, ST_CHUNK), lambda j, b: (b, 0, j))
    y, hr, hi = pl.pallas_call(
        _scan_prompt_kernel,
        grid=(N_LCHUNK, bsz),
        in_specs=[chunk, per_j(CHUNK * LANES, wide), per_j(wide, CHUNK * LANES), per_j(CHUNK * LANES, CHUNK * LANES),
                  per_j(1, ST_CHUNK), per_j(1, ST_CHUNK), per_j(1, CHUNK * LANES), state, state],
        out_specs=[chunk, state, state],
        out_shape=[jax.ShapeDtypeStruct((bsz, N_LCHUNK, n_chunks, CHUNK, N_SEG, LANES), F32),
                   jax.ShapeDtypeStruct((bsz, 1, N_STATE), F32), jax.ShapeDtypeStruct((bsz, 1, N_STATE), F32)],
        scratch_shapes=[pltpu.VMEM((rows, wide), F32), pltpu.VMEM((rows, wide), F32)],
        compiler_params=_params(2),
        name="ssm_scan_prompt",
    )(split(u), *weights, d_skip, h0r, h0i)
    return y.reshape(u.shape), hr, hi


def _scan_sample_kernel(u_ref, bmat_ref, cmat_ref, are_ref, aim_ref, d_ref, h0r_ref, h0i_ref,
                        y_ref, hnr_ref, hni_ref):
    a_re, a_im = are_ref[...], aim_ref[...]
    hr, hi = h0r_ref[...].T, h0i_ref[...].T
    for l in range(DEC_SEQ):
        rows = slice(l * DEC_BATCH, (l + 1) * DEC_BATCH)
        u = u_ref[rows, :]
        bu = _dot(u.astype(BF16), bmat_ref[...])
        gr, gi = _cmul(a_re, a_im, hr, hi)
        hr, hi = gr + bu[:, :ST_CHUNK], gi + bu[:, ST_CHUNK:]
        h = jnp.concatenate([hr, hi], axis=1).astype(BF16)
        y_ref[rows, :] = _dot(h, cmat_ref[...]) + d_ref[...] * u
    hnr_ref[...] = hr.T
    hni_ref[...] = hi.T


def _scan_sample(u, bmat, cmat, a_re, a_im, d_skip, h0r, h0i):
    n = DEC_SEQ * DEC_BATCH
    chunk = pl.BlockSpec((None, n, LANES), lambda j: (j, 0, 0))
    per_j = lambda r, c: pl.BlockSpec((None, r, c), lambda j: (j, 0, 0))
    state = pl.BlockSpec((ST_CHUNK, DEC_BATCH), lambda j: (j, 0))
    return pl.pallas_call(
        _scan_sample_kernel,
        grid=(N_LCHUNK,),
        in_specs=[chunk, per_j(LANES, 2 * ST_CHUNK), per_j(2 * ST_CHUNK, LANES), per_j(1, ST_CHUNK),
                  per_j(1, ST_CHUNK), per_j(1, LANES), state, state],
        out_specs=[chunk, state, state],
        out_shape=[jax.ShapeDtypeStruct((N_LCHUNK, n, LANES), F32),
                   jax.ShapeDtypeStruct((N_STATE, DEC_BATCH), F32), jax.ShapeDtypeStruct((N_STATE, DEC_BATCH), F32)],
        compiler_params=_params(1),
        name="ssm_scan_sample",
    )(u, bmat, cmat, a_re, a_im, d_skip, h0r, h0i)


def _ssm_out_kernel(sample, s_ref, y_ref, wglu_ref, bglu_ref, wout_ref, g_ref, b_ref, out_ref, z_ref):
    n_rows = z_ref.shape[1]
    if sample:
        parts = [(pl.ds(l, DEC_BATCH, stride=DEC_SEQ), slice(l * DEC_BATCH, (l + 1) * DEC_BATCH))
                 for l in range(DEC_SEQ)]
    else:
        parts = [(slice(s * SEG_TILE, (s + 1) * SEG_TILE), pl.ds(s, SEG_TILE, stride=N_SEG)) for s in range(N_SEG)]
    for c in range(N_LCHUNK):
        for tok_rows, slab_rows in parts:
            z_ref[c, tok_rows, :] = s_ref[c, slab_rows, :]
    z = jax.nn.gelu(jnp.concatenate([z_ref[c] for c in range(N_LCHUNK)], axis=1))
    gate = jax.nn.sigmoid(_dot(z.astype(BF16), wglu_ref[...]) + bglu_ref[...])
    mix = _dot((z * gate).astype(BF16), wout_ref[...])
    res = y_ref[...].reshape(n_rows, D_MODEL)
    out = _layer_norm(DN_ALPHA * res + mix, g_ref[...], b_ref[...])
    out_ref[...] = out.reshape(out_ref.shape)


def _ssm_out(s, y, s_spec, y_spec, grid, rows, sample, w_glu, b_glu, w_out, g, b, name):
    return pl.pallas_call(
        functools.partial(_ssm_out_kernel, sample),
        grid=grid,
        in_specs=[s_spec, y_spec, _resident((D_MODEL, D_MODEL)), _resident((1, D_MODEL)),
                  _resident((D_MODEL, D_MODEL)), _resident((1, D_MODEL)), _resident((1, D_MODEL))],
        out_specs=y_spec,
        out_shape=jax.ShapeDtypeStruct(y.shape, F32),
        scratch_shapes=[pltpu.VMEM((N_LCHUNK, rows, LANES), F32)],
        compiler_params=_params(len(grid)),
        name=name,
    )(s, y, w_glu, b_glu, w_out, g, b)


def _attn_prompt(yp, w_in, sinks, w_out, g, b):
    cos_p, sin_p = _rope_tables(jnp.arange(SEQ))
    tiles_per_seq = SEQ // ROW_TILE
    qa, ka, va, qb, kb, vb, *extra = _attn_proj(
        yp, w_in, cos_p, sin_p, lambda i: i % tiles_per_seq, BF16, prompt_seq=(BATCH, SEQ))
    dilated, tails = extra[:6], extra[6:]
    seq3 = lambda a: a.reshape(BATCH, SEQ, a.shape[-1])
    plane1 = lambda a: a.reshape(BATCH, 1, SEQ, D_A)
    o1, l1 = _band_a(plane1(qa), plane1(ka), plane1(va))
    pats = [(o1.reshape(BATCH * SEQ, D_A), l1.reshape(BATCH * SEQ, LANES))]
    for i in range(len(DILATIONS) - 1):
        pats.append(_band_a(*dilated[3 * i:3 * i + 3]))
    ob = _band_b(seq3(qb), seq3(kb), seq3(vb), sinks)
    yp = _attn_out_prompt(pats, ob, yp, w_out, g, b, SEQ)
    heads = lambda a, nh: jnp.transpose(a.reshape(BATCH, nh, HEAD_DIM, a.shape[-1]), (0, 3, 1, 2))[None]
    prompt_cache = (heads(tails[0], N_HEADS_A), heads(tails[1], N_HEADS_A),
                    heads(tails[2], N_KV_B), heads(tails[3], N_KV_B))
    return yp, prompt_cache


def _attn_sample_path(ys, cache_ak, cache_av, cache_bk, cache_bv, w_in, sinks, w_out, g, b, yp, ffn_args):
    cos_s, sin_s = _rope_tables(PAST_LEN + jnp.arange(DEC_SEQ))
    reps = DEC_BATCH * DEC_SEQ // DEC_SEQ
    cos_s, sin_s = jnp.tile(cos_s, (reps, 1)), jnp.tile(sin_s, (reps, 1))
    qa, ka, va, qb, kb, vb = _attn_proj(ys, w_in, cos_s, sin_s, lambda i: i, F32)
    stored = lambda a: jnp.transpose(a, (0, 2, 3, 1))
    yp, o = _ffn_with_sample_attention(
        yp, *ffn_args, qa, ka, va, stored(cache_ak), stored(cache_av),
        qb, kb, vb, cache_bk.reshape(DEC_BATCH, WIN_B, D_BKV), cache_bv.reshape(DEC_BATCH, WIN_B, D_BKV), sinks)
    ys = _attn_out_sample(o, ys, w_out, g, b)
    new = lambda a, nh: a.reshape(1, DEC_BATCH, DEC_SEQ, nh, HEAD_DIM)
    sample_cache = (new(ka, N_HEADS_A), new(va, N_HEADS_A), new(kb, N_KV_B), new(vb, N_KV_B))
    return yp, ys, sample_cache


def _attn_layer(yp, ys, cache_ak, cache_av, cache_bk, cache_bv, w_in, sinks, w_out, g, b, ffn_args):
    w_in = w_in.astype(BF16)
    w_out = w_out.astype(BF16)
    yp, prompt_cache = _attn_prompt(yp, w_in, sinks, w_out, g, b)
    yp, ys, sample_cache = _attn_sample_path(ys, cache_ak, cache_av, cache_bk, cache_bv, w_in, sinks, w_out, g, b,
                                             yp, ffn_args)
    return yp, ys, prompt_cache, sample_cache


def _ssm_layer(yp, ys, state_re, state_im, w_in, lam_re, lam_im, log_dt, b_re, b_im, c_re, c_im, d_skip,
               w_glu, b_glu, w_out, g, b):
    w_in, w_glu, w_out = w_in.astype(BF16), w_glu.astype(BF16), w_out.astype(BF16)
    b_glu = b_glu.reshape(1, D_MODEL)
    mats = _ssm_prepare(lam_re, lam_im, log_dt, b_re, b_im, c_re, c_im, d_skip)
    yp, prompt_state = _ssm_prompt(yp, w_in, mats, w_glu, b_glu, w_out, g, b)
    ys, sample_state = _ssm_sample(ys, state_re, state_im, w_in, mats, w_glu, b_glu, w_out, g, b)
    return yp, ys, prompt_state, sample_state


def _ssm_prepare(lam_re, lam_im, log_dt, b_re, b_im, c_re, c_im, d_skip):
    ab_re, ab_im, bb_re, bb_im = _ssm_discretize(lam_re, lam_im, log_dt, b_re, b_im)
    bmat, cmat = _ssm_matrices(bb_re, bb_im, c_re, c_im)
    a_re = ab_re.reshape(N_LCHUNK, 1, ST_CHUNK)
    a_im = ab_im.reshape(N_LCHUNK, 1, ST_CHUNK)
    d3 = d_skip.astype(F32).reshape(N_LCHUNK, 1, LANES)
    prompt = (_chunk_weights(a_re, a_im, bmat, cmat), jnp.tile(d3, (1, 1, CHUNK)))
    sample = (bmat.astype(BF16), cmat.astype(BF16), a_re, a_im, d3)
    return prompt, sample


def _ssm_prompt(yp, w_in, mats, w_glu, b_glu, w_out, g, b):
    weights, d_tiled = mats[0]
    up = _ssm_in_prompt(yp, w_in)
    zero = jnp.zeros((BATCH, 1, N_STATE), F32)
    sp, pr, pi = _scan_prompt(up, weights, d_tiled, zero, zero)
    yp = _ssm_out(sp, yp.reshape(BATCH, N_SEG, SEG_LEN, D_MODEL),
                  pl.BlockSpec((None, N_LCHUNK, ROW_TILE, LANES), lambda bb, t: (bb, 0, t, 0)),
                  pl.BlockSpec((None, N_SEG, SEG_TILE, D_MODEL), lambda bb, t: (bb, 0, t, 0)),
                  (BATCH, SEG_LEN // SEG_TILE), ROW_TILE, False, w_glu, b_glu, w_out, g, b,
                  "ssm_out_prompt").reshape(BATCH * SEQ, D_MODEL)
    prompt_state = (pr.reshape(1, BATCH, N_SSM_GROUPS, SSM_STATE), pi.reshape(1, BATCH, N_SSM_GROUPS, SSM_STATE))
    return yp, prompt_state


def _ssm_sample(ys, state_re, state_im, w_in, mats, w_glu, b_glu, w_out, g, b):
    bmat, cmat, a_re, a_im, d3 = mats[1]
    us = _ssm_in_sample(ys, w_in)
    stored = lambda a: jnp.transpose(a, (1, 2, 0)).reshape(N_STATE, DEC_BATCH)
    logical = lambda a: jnp.transpose(a.reshape(N_SSM_GROUPS, SSM_STATE, DEC_BATCH), (2, 0, 1))[None]
    ss, sr, si = _scan_sample(us, bmat, cmat, a_re, a_im, d3, stored(state_re), stored(state_im))
    n = DEC_BATCH * DEC_SEQ
    ys = _ssm_out(ss, ys,
                  pl.BlockSpec((N_LCHUNK, n, LANES), lambda i: (0, 0, 0)),
                  pl.BlockSpec((n, D_MODEL), lambda i: (0, 0)),
                  (1,), n, True, w_glu, b_glu, w_out, g, b, "ssm_out_sample")
    return ys, (logical(sr), logical(si))


def kernel(x_prompt, x_sample, cache_a_k, cache_a_v, cache_b_k, cache_b_v, state_c_re, state_c_im, ln_g, ln_b, ffn_w_gate, ffn_w_up, ffn_w_down, attn_w_in, attn_sinks, attn_w_out, ssm_w_in, ssm_lambda_re, ssm_lambda_im, ssm_log_dt, ssm_b_re, ssm_b_im, ssm_c_re, ssm_c_im, ssm_d, ssm_w_glu, ssm_b_glu, ssm_w_out):
    yp = x_prompt.reshape(BATCH * SEQ, D_MODEL)
    ys = x_sample.reshape(DEC_BATCH * DEC_SEQ, D_MODEL)
    ln = lambda l, k: (ln_g[l, k].reshape(1, D_MODEL), ln_b[l, k].reshape(1, D_MODEL))

    wg, wu, wd = ffn_w_gate.astype(BF16), ffn_w_up.astype(BF16), ffn_w_down.astype(BF16)

    def ffn_pair(yp, ys, l, k, ln_idx):
        g, b = ln(l, ln_idx)
        return _ffn(yp, wg, wu, wd, g, b, l, k), _ffn(ys, wg, wu, wd, g, b, l, k)

    yp, ys = ffn_pair(yp, ys, 0, 0, 0)
    yp, ys, p_cache, s_cache = _attn_layer(yp, ys, cache_a_k[0], cache_a_v[0], cache_b_k[0], cache_b_v[0],
                                           attn_w_in[0], attn_sinks[0], attn_w_out[0], *ln(0, 1),
                                           ffn_args=(wg, wu, wd, *ln(0, 2), 0, 1))
    ys = _ffn(ys, wg, wu, wd, *ln(0, 2), 0, 1)
    yp, ys = ffn_pair(yp, ys, 1, 0, 0)
    yp, ys, p_state, s_state = _ssm_layer(yp, ys, state_c_re[0], state_c_im[0], ssm_w_in[0], ssm_lambda_re[0],
                                          ssm_lambda_im[0], ssm_log_dt[0], ssm_b_re[0], ssm_b_im[0], ssm_c_re[0],
                                          ssm_c_im[0], ssm_d[0], ssm_w_glu[0], ssm_b_glu[0], ssm_w_out[0], *ln(1, 1))
    yp, ys = ffn_pair(yp, ys, 1, 1, 2)
    return (yp.reshape(BATCH, SEQ, D_MODEL), ys.reshape(DEC_BATCH, DEC_SEQ, D_MODEL),
            *p_cache, *p_state, *s_cache, *s_state)
```

```python
import functools
import math

import jax
import jax.numpy as jnp
from jax import lax
from jax.experimental import pallas as pl
from jax.experimental.pallas import tpu as pltpu

F32 = jnp.float32
BF16 = jnp.bfloat16

D_MODEL = 1024
BATCH = 4
SEQ = 4096
DEPTH = 2
DEC_BATCH = 128
DEC_SEQ = 8
PAST_LEN = 16384
HEAD_DIM = 64
N_HEADS_A = 8
DILATIONS = (1, 4, 16)
WIN_A = 2048
N_HEADS_B = 8
N_KV_B = 2
WIN_B = 128
ROPE_THETA = 10000.0
D_A = N_HEADS_A * HEAD_DIM
D_BQ = N_HEADS_B * HEAD_DIM
D_BKV = N_KV_B * HEAD_DIM
D_IN_ATTN = 3 * D_A + D_BQ + 2 * D_BKV
SSM_GROUP = 16
N_SSM_GROUPS = D_MODEL // SSM_GROUP
SSM_STATE = 64
N_STATE = N_SSM_GROUPS * SSM_STATE
D_FF = 2816
DN_ALPHA = (2 * DEPTH) ** 0.25
FFN_RES = 0.5
LN_EPS = 1e-5
ATTN_SCALE = HEAD_DIM ** -0.5
LOG2E = math.log2(math.e)
LSE_LANES = 16

LANES = 128
SUBLANES = 8
MXU_N = 256
VMEM_LIMIT = 56 * 1024 * 1024

ROW_TILE = 512
FFN_ROW_TILE = 1024
FF_CHUNK = MXU_N
TQ = 128
N_SEG = SUBLANES
SEG_LEN = SEQ // N_SEG
N_LCHUNK = D_MODEL // LANES
ST_CHUNK = N_STATE // N_LCHUNK
NK_PAD = WIN_A + LANES
NKB_PAD = 2 * WIN_B

NEG_INF = float("-inf")


def _params(n_axes, vmem=VMEM_LIMIT):
    return pltpu.CompilerParams(dimension_semantics=("arbitrary",) * n_axes, vmem_limit_bytes=vmem)


def _resident(shape):
    return pl.BlockSpec(shape, lambda *_: (0,) * len(shape), pipeline_mode=pl.Buffered(1))


def _layer_norm(x, g, b):
    mu = jnp.mean(x, -1, keepdims=True)
    xc = x - mu
    var = jnp.mean(xc * xc, -1, keepdims=True)
    return xc * lax.rsqrt(var + LN_EPS) * g + b


def _dot(a, b):
    return jnp.dot(a, b, preferred_element_type=F32)


def _dot_nt(a, b):
    return lax.dot_general(a, b, (((1,), (1,)), ((), ())), preferred_element_type=F32)


def _ffn_kernel(x_ref, wg_ref, wu_ref, wd_ref, g_ref, b_ref, o_ref, h_ref):
    x = x_ref[...]
    xb = x.astype(BF16)
    for c in range(D_FF // FF_CHUNK):
        sl = slice(c * FF_CHUNK, (c + 1) * FF_CHUNK)
        gate = _dot(xb, wg_ref[:, sl])
        up = _dot(xb, wu_ref[:, sl])
        h_ref[:, sl] = (gate * jax.nn.sigmoid(gate) * up).astype(BF16)
    y = DN_ALPHA * x + FFN_RES * _dot(h_ref[...], wd_ref[...])
    o_ref[...] = _layer_norm(y, g_ref[...], b_ref[...])


def _ffn(x, wg, wu, wd, g, b, layer=0, which=0):
    n = x.shape[0]
    tm = min(FFN_ROW_TILE, n)
    row = pl.BlockSpec((tm, D_MODEL), lambda i: (i, 0))
    if wg.ndim == 4:
        pick = lambda r, c: pl.BlockSpec((None, None, r, c), lambda i: (layer, which, 0, 0),
                                         pipeline_mode=pl.Buffered(1))
    else:
        pick = lambda r, c: _resident((r, c))
    return pl.pallas_call(
        _ffn_kernel,
        grid=(n // tm,),
        in_specs=[row, pick(D_MODEL, D_FF), pick(D_MODEL, D_FF), pick(D_FF, D_MODEL),
                  _resident((1, D_MODEL)), _resident((1, D_MODEL))],
        out_specs=row,
        out_shape=jax.ShapeDtypeStruct((n, D_MODEL), F32),
        scratch_shapes=[pltpu.VMEM((tm, D_FF), BF16)],
        compiler_params=_params(1),
        name="ffn",
    )(x, wg, wu, wd, g, b)


def _rope_tables(pos):
    half = HEAD_DIM // 2
    inv_freq = ROPE_THETA ** (-jnp.arange(half, dtype=F32) / half)
    ang = pos.astype(F32)[:, None] * inv_freq[None, :]
    cos, sin = jnp.cos(ang), jnp.sin(ang)
    cos_t = jnp.concatenate([cos, cos, cos, cos], -1)
    sin_t = jnp.concatenate([-sin, sin, -sin, sin], -1)
    return cos_t, sin_t


def _attn_proj_kernel(tiles_per_seq, x_ref, w_ref, cos_ref, sin_ref, qa_ref, ka_ref, va_ref, qb_ref, kb_ref, vb_ref,
                      *extra):
    xb = x_ref[...].astype(BF16)
    slab_ref = extra[-1] if extra else None
    dilated = extra[:6]
    n_chunks = D_A // LANES
    tm = x_ref.shape[0]

    def keep(tensor, c, val):
        if slab_ref is not None:
            slab_ref[tensor * n_chunks + c] = val
    cos = cos_ref[...]
    sin = sin_ref[...]
    lane = lax.broadcasted_iota(jnp.int32, cos.shape, 1)
    first_half = (lane & (HEAD_DIM // 2)) == 0

    def rope(z):
        rot = jnp.where(first_half, pltpu.roll(z, LANES - HEAD_DIM // 2, 1), pltpu.roll(z, HEAD_DIM // 2, 1))
        return z * cos + rot * sin

    def project(col0, ncols):
        return _dot(xb, w_ref[:, col0:col0 + ncols])

    def rope_chunks(z):
        return [rope(z[:, c * LANES:(c + 1) * LANES]) for c in range(z.shape[1] // LANES)]

    q_scale = ATTN_SCALE * LOG2E if extra else ATTN_SCALE
    col = 0
    for c, r in enumerate(rope_chunks(project(col, D_A))):
        r = r * q_scale
        qa_ref[:, c * LANES:(c + 1) * LANES] = r.astype(qa_ref.dtype)
        keep(0, c, r)
    col += D_A
    for c, r in enumerate(rope_chunks(project(col, D_A))):
        ka_ref[:, c * LANES:(c + 1) * LANES] = r.astype(ka_ref.dtype)
        keep(1, c, r)
    col += D_A
    z = project(col, D_A)
    va_ref[...] = z.astype(va_ref.dtype)
    for c in range(n_chunks):
        keep(2, c, z[:, c * LANES:(c + 1) * LANES])
    col += D_A
    if extra:
        kat_ref, vat_ref, kbt_ref, vbt_ref = extra[6:10]
        tile_in_seq = pl.program_id(0) % tiles_per_seq

        @pl.when(tile_in_seq >= tiles_per_seq - WIN_A // tm)
        def _():
            for c in range(n_chunks):
                kat_ref[c * LANES:(c + 1) * LANES, :] = slab_ref[n_chunks + c].T
                vat_ref[c * LANES:(c + 1) * LANES, :] = slab_ref[2 * n_chunks + c].T

        slab4_ref = extra[-2]
        d1, d2 = DILATIONS[1], DILATIONS[2] // DILATIONS[1]
        plane = tm // d1
        for tensor in range(3):
            out1_ref, out2_ref = dilated[tensor], dilated[3 + tensor]
            for c in range(n_chunks):
                idx = tensor * n_chunks + c
                lanes = slice(c * LANES, (c + 1) * LANES)
                for r in range(d1):
                    rows = slab_ref[idx, pl.ds(r, plane, stride=d1), :]
                    slab4_ref[idx, r * plane:(r + 1) * plane, :] = rows
                    out1_ref[r, :, lanes] = rows.astype(out1_ref.dtype)
                for r in range(d1):
                    for m in range(d2):
                        rows = slab4_ref[idx, pl.ds(r * plane + m, plane // d2, stride=d2), :]
                        out2_ref[r + d1 * m, :, lanes] = rows.astype(out2_ref.dtype)
    for c, r in enumerate(rope_chunks(project(col, D_BQ))):
        qb_ref[:, c * LANES:(c + 1) * LANES] = (r * q_scale).astype(qb_ref.dtype)
    col += D_BQ
    z = project(col, 2 * D_BKV)
    r = rope(z[:, :D_BKV])
    if extra:
        lo = _lane_lo(r.shape)
        for ref, val in ((kb_ref, r), (vb_ref, z[:, D_BKV:])):
            swapped = pltpu.roll(val, HEAD_DIM, 1)
            ref[:, 0:LANES] = jnp.where(lo, val, swapped).astype(ref.dtype)
            ref[:, LANES:] = jnp.where(lo, swapped, val).astype(ref.dtype)
    else:
        kb_ref[...] = r.astype(kb_ref.dtype)
        vb_ref[...] = z[:, D_BKV:].astype(vb_ref.dtype)
    if extra:
        @pl.when(tile_in_seq == tiles_per_seq - 1)
        def _():
            kbt_ref[...] = r[tm - WIN_B:, :].T
            vbt_ref[...] = z[tm - WIN_B:, D_BKV:].T


def _attn_proj(x, w, cos_t, sin_t, table_block, act_dtype, prompt_seq=None):
    n = x.shape[0]
    tm = min(ROW_TILE, n)

    def row(width):
        return pl.BlockSpec((tm, width), lambda i: (i, 0))

    tab = pl.BlockSpec((tm, LANES), lambda i: (table_block(i), 0))
    kv_b = D_BKV if prompt_seq is None else N_KV_B * LANES
    widths = (D_A, D_A, D_A, D_BQ, kv_b, kv_b)
    out_shape = [jax.ShapeDtypeStruct((n, wd), act_dtype) for wd in widths]
    out_specs = [row(wd) for wd in widths]
    scratch = []
    tps = None
    if prompt_seq is not None:
        bsz, seq = prompt_seq
        tps = seq // tm
        for dil in DILATIONS[1:]:
            out_shape += [jax.ShapeDtypeStruct((bsz, dil, seq // dil, D_A), BF16)] * 3
            out_specs += [pl.BlockSpec((None, dil, tm // dil, D_A), lambda i: (i // tps, 0, i % tps, 0))] * 3
        first_tail = tps - WIN_A // tm
        out_shape += [jax.ShapeDtypeStruct((bsz, D_A, WIN_A), F32)] * 2
        out_specs += [pl.BlockSpec((None, D_A, tm), lambda i: (i // tps, 0, jnp.maximum(i % tps - first_tail, 0)))] * 2
        out_shape += [jax.ShapeDtypeStruct((bsz, D_BKV, WIN_B), F32)] * 2
        out_specs += [pl.BlockSpec((None, D_BKV, WIN_B), lambda i: (i // tps, 0, 0))] * 2
        assert DILATIONS[2] == DILATIONS[1] ** 2
        scratch = [pltpu.VMEM((3 * D_A // LANES, tm, LANES), F32)] * 2
    return pl.pallas_call(
        functools.partial(_attn_proj_kernel, tps),
        grid=(n // tm,),
        in_specs=[row(D_MODEL), _resident((D_MODEL, D_IN_ATTN)), tab, tab],
        out_specs=out_specs,
        out_shape=out_shape,
        scratch_shapes=scratch,
        compiler_params=_params(1),
        name="attn_proj",
    )(x, w, cos_t, sin_t)


def _lane_lo(shape):
    return lax.broadcasted_iota(jnp.int32, shape, 1) < HEAD_DIM


def _half_masks_bf16():
    lo = jnp.where(_lane_lo((1, LANES)), 1.0, 0.0).astype(BF16)
    return lo, 1 - lo


def _band_masks(n_heads, t, sub):
    row = lax.broadcasted_iota(jnp.int32, (n_heads * TQ, TQ), 0) & (TQ - 1)
    col = lax.broadcasted_iota(jnp.int32, (n_heads * TQ, TQ), 1)
    shift = jnp.where(t > 0, 0, TQ) if sub == 0 else 0
    return col <= row, col >= row + shift


def _sub_tile_kv(sub, sl, kc_ref, kp_ref, vc_ref, vp_ref):
    if sub == 0:
        return kc_ref[0:TQ, sl], kp_ref[:, sl], vc_ref[0:TQ, sl], vp_ref[:, sl]
    return kc_ref[TQ:2 * TQ, sl], kc_ref[0:TQ, sl], vc_ref[TQ:2 * TQ, sl], vc_ref[0:TQ, sl]


def _band_softmax(qs, kc, kp, vc, vp, mask_c, mask_p):
    s_c = jnp.where(mask_c, _dot_nt(qs, kc), NEG_INF)
    s_p = jnp.where(mask_p, _dot_nt(qs, kp), NEG_INF)
    m = jnp.max(jnp.maximum(s_c, s_p), -1, keepdims=True)
    p_c = jnp.exp2(s_c - m)
    p_p = jnp.exp2(s_p - m)
    den = jnp.sum(p_c + p_p, -1, keepdims=True)
    acc = _dot(p_c.astype(BF16), vc) + _dot(p_p.astype(BF16), vp)
    return acc * (1.0 / den), m, den


def _band_a_kernel(q_ref, kc_ref, kp_ref, vc_ref, vp_ref, o_ref, lse_ref):
    t = pl.program_id(2)
    lo = _lane_lo((TQ, LANES))
    lo_bf, hi_bf = _half_masks_bf16()
    lane_head = jnp.right_shift(lax.broadcasted_iota(jnp.int32, (TQ, LANES), 1), int(math.log2(LSE_LANES)))
    for sub in range(2):
        rows = slice(sub * TQ, (sub + 1) * TQ)
        mask_c, mask_p = _band_masks(2, t, sub)
        lse_tile = jnp.zeros((TQ, LANES), F32)
        for c in range(D_A // LANES):
            sl = slice(c * LANES, (c + 1) * LANES)
            kc, kp, vc, vp = _sub_tile_kv(sub, sl, kc_ref, kp_ref, vc_ref, vp_ref)
            q2 = q_ref[rows, sl]
            qs = jnp.concatenate([q2 * lo_bf, q2 * hi_bf], axis=0)
            out, m, den = _band_softmax(qs, kc, kp, vc, vp, mask_c, mask_p)
            lse = m + jnp.log2(den)
            o_ref[rows, sl] = jnp.where(lo, out[0:TQ], out[TQ:])
            lse_tile = jnp.where(lane_head == 2 * c, lse[0:TQ], lse_tile)
            lse_tile = jnp.where(lane_head == 2 * c + 1, lse[TQ:], lse_tile)
        lse_ref[rows, :] = lse_tile


def _band_a(q, k, v):
    bsz, dil, sub, _ = q.shape
    cur = pl.BlockSpec((None, None, 2 * TQ, D_A), lambda b, r, t: (b, r, t, 0))
    prev = pl.BlockSpec((None, None, TQ, D_A), lambda b, r, t: (b, r, jnp.maximum(2 * t - 1, 0), 0))
    lse = pl.BlockSpec((None, None, 2 * TQ, LANES), lambda b, r, t: (b, r, t, 0))
    return pl.pallas_call(
        _band_a_kernel,
        grid=(bsz, dil, sub // (2 * TQ)),
        in_specs=[cur, cur, prev, cur, prev],
        out_specs=[cur, lse],
        out_shape=[jax.ShapeDtypeStruct((bsz, dil, sub, D_A), F32),
                   jax.ShapeDtypeStruct((bsz, dil, sub, LANES), F32)],
        compiler_params=_params(3),
        name=f"band_a_d{dil}",
    )(q, k, k, v, v)


def _band_b_body(first_tile, sink_ref, q_ref, kc_ref, kp_ref, vc_ref, vp_ref, o_ref):
    group = N_HEADS_B // N_KV_B
    lo = _lane_lo((TQ, LANES))
    lo_bf, hi_bf = _half_masks_bf16()
    row = lax.broadcasted_iota(jnp.int32, (group * TQ, TQ), 0) & (TQ - 1)
    col = lax.broadcasted_iota(jnp.int32, (group * TQ, TQ), 1)
    in_cur = col <= row
    cur_bf = jnp.where(in_cur, 1.0, 0.0).astype(BF16)
    prev_bf = 1 - cur_bf
    for sub in range(2):
        rows = slice(sub * TQ, (sub + 1) * TQ)
        for g in range(N_KV_B):
            sl = slice(g * LANES, (g + 1) * LANES)
            kc, kp, vc, vp = _sub_tile_kv(sub, sl, kc_ref, kp_ref, vc_ref, vp_ref)
            heads = range(g * group, (g + 1) * group)
            qs = jnp.concatenate(
                [q_ref[rows, (h // 2) * LANES:(h // 2 + 1) * LANES] * (lo_bf if h % 2 == 0 else hi_bf) for h in heads],
                axis=0)
            sink = jnp.concatenate([jnp.full((TQ, 1), sink_ref[h] * LOG2E, F32) for h in heads], axis=0)
            only_cur = first_tile and sub == 0
            s = jnp.where(in_cur, _dot_nt(qs, kc), NEG_INF if only_cur else _dot_nt(qs, kp))
            m = jnp.maximum(jnp.max(s, -1, keepdims=True), sink)
            p = jnp.exp2(s - m)
            den = jnp.sum(p, -1, keepdims=True) + jnp.exp2(sink - m)
            pb = p.astype(BF16)
            acc = _dot(pb, vc) if only_cur else _dot(pb * cur_bf, vc) + _dot(pb * prev_bf, vp)
            out = acc * (1.0 / den)
            for i in range(group // 2):
                c = g * (group // 2) + i
                even, odd = out[2 * i * TQ:(2 * i + 1) * TQ], out[(2 * i + 1) * TQ:(2 * i + 2) * TQ]
                o_ref[rows, c * LANES:(c + 1) * LANES] = jnp.where(lo, even, odd).astype(o_ref.dtype)


def _band_b_kernel(*refs):
    t = pl.program_id(1)
    pl.when(t == 0)(functools.partial(_band_b_body, True, *refs))
    pl.when(t > 0)(functools.partial(_band_b_body, False, *refs))


def _band_b(q, k, v, sinks):
    bsz, seq, _ = q.shape
    kv_lanes = N_KV_B * LANES
    qs = pl.BlockSpec((None, 2 * TQ, D_BQ), lambda b, t: (b, t, 0))
    cur = pl.BlockSpec((None, 2 * TQ, kv_lanes), lambda b, t: (b, t, 0))
    prev = pl.BlockSpec((None, TQ, kv_lanes), lambda b, t: (b, jnp.maximum(2 * t - 1, 0), 0))
    o = pl.pallas_call(
        _band_b_kernel,
        grid=(bsz, seq // (2 * TQ)),
        in_specs=[pl.BlockSpec(memory_space=pltpu.SMEM), qs, cur, prev, cur, prev],
        out_specs=qs,
        out_shape=jax.ShapeDtypeStruct((bsz, seq, D_BQ), BF16),
        compiler_params=_params(2),
        name="band_b",
    )(sinks, q, k, k, v, v)
    return o.reshape(bsz * seq, D_BQ)


def _pattern_count(dist):
    cnt = jnp.zeros(dist.shape, F32)
    for dil in DILATIONS:
        cnt = cnt + ((dist >= 0) & (dist <= 128 * dil) & (dist % dil == 0)).astype(F32)
    return cnt


def _sample_tables():
    i = jnp.arange(DEC_SEQ)
    cnt_c = _pattern_count(WIN_A + i[:, None] - jnp.arange(WIN_A)[None, :])
    j = jnp.arange(LANES)
    cnt_n = jnp.where(j[None, :] < DEC_SEQ, _pattern_count(i[:, None] - j[None, :]), 0.0)
    cnt_n = jnp.tile(cnt_n, (N_HEADS_A, 1))
    jb = jnp.arange(NKB_PAD)[None, :]
    dist_b = WIN_B + i[:, None] - jb
    ok_b = (dist_b >= 0) & (dist_b < WIN_B) & (jb < WIN_B + DEC_SEQ)
    mask_b = jnp.tile(ok_b.astype(F32), (N_HEADS_B, 1))
    return cnt_c, cnt_n, mask_b


def _sample_attend(q, kan, van, kt_ref, vt_ref, cnt_c, cnt_n, qb, kbn, vbn, kbc, vbc, mask_b, sink_col, kb_s, vb_s):
    rows = N_HEADS_A * DEC_SEQ
    q_rep = jnp.concatenate([q] * N_HEADS_A, axis=0)
    row_head = jnp.right_shift(lax.broadcasted_iota(jnp.int32, (rows, D_A), 0), int(math.log2(DEC_SEQ)))
    lane_head = jnp.right_shift(lax.broadcasted_iota(jnp.int32, (rows, D_A), 1), int(math.log2(HEAD_DIM)))
    own = row_head == lane_head
    q_bd = jnp.where(own, q_rep, 0.0).astype(BF16)
    pad = jnp.zeros((LANES - DEC_SEQ, D_A), F32)
    kn = jnp.concatenate([kan, pad], 0).astype(BF16)
    vn = jnp.concatenate([van, pad], 0).astype(BF16)
    s_new = jnp.where(cnt_n > 0.0, _dot_nt(q_bd, kn), NEG_INF)
    outs, p_new = [], []
    for h in range(N_HEADS_A):
        head_rows = slice(h * DEC_SEQ, (h + 1) * DEC_SEQ)
        q_h = q[:, h * HEAD_DIM:(h + 1) * HEAD_DIM].astype(BF16)
        s_c = jnp.where(cnt_c > 0.0, _dot(q_h, kt_ref[h].astype(BF16)), NEG_INF)
        s_n = s_new[head_rows]
        m = jnp.maximum(jnp.max(s_c, -1, keepdims=True), jnp.max(s_n, -1, keepdims=True))
        p_c = jnp.exp(s_c - m) * cnt_c
        p_n = jnp.exp(s_n - m) * cnt_n[head_rows]
        inv = 1.0 / (jnp.sum(p_c, -1, keepdims=True) + jnp.sum(p_n, -1, keepdims=True))
        outs.append(_dot_nt(p_c.astype(BF16), vt_ref[h].astype(BF16)) * inv)
        p_new.append(p_n * inv)
    out_n = jnp.where(own, _dot(jnp.concatenate(p_new, axis=0).astype(BF16), vn), 0.0)
    oa = jnp.concatenate(outs, axis=1)
    for h in range(N_HEADS_A):
        oa = oa + out_n[h * DEC_SEQ:(h + 1) * DEC_SEQ]

    n_pad_b = NKB_PAD - WIN_B - DEC_SEQ
    pad_b = jnp.zeros((n_pad_b, D_BKV), F32)
    kb_s[...] = jnp.concatenate([kbc, kbn, pad_b], 0).astype(BF16)
    vb_s[...] = jnp.concatenate([vbc, vbn, pad_b], 0).astype(BF16)
    lo8 = _lane_lo((DEC_SEQ, LANES))
    group = N_HEADS_B // N_KV_B
    pieces = []
    for h in range(N_HEADS_B):
        chunk = qb[:, (h // 2) * LANES:(h // 2 + 1) * LANES]
        g = h // group
        if h % 2 != g:
            chunk = pltpu.roll(chunk, HEAD_DIM, 1)
        pieces.append(jnp.where(lo8 if g == 0 else jnp.logical_not(lo8), chunk, 0.0))
    qb_bd = jnp.concatenate(pieces, axis=0).astype(BF16)
    sb = jnp.where(mask_b > 0.0, _dot_nt(qb_bd, kb_s[...]), NEG_INF)
    sink = sink_col[:, 0:1]
    mb = jnp.maximum(jnp.max(sb, -1, keepdims=True), sink)
    pb = jnp.exp(sb - mb) * mask_b
    den_b = jnp.sum(pb, -1, keepdims=True) + jnp.exp(sink - mb)
    ob_full = _dot(pb.astype(BF16), vb_s[...]) * (1.0 / den_b)
    ob = []
    for c in range(D_BQ // LANES):
        halves = []
        for half in range(2):
            h = 2 * c + half
            piece = ob_full[h * DEC_SEQ:(h + 1) * DEC_SEQ]
            if half != h // group:
                piece = pltpu.roll(piece, HEAD_DIM, 1)
            halves.append(piece)
        ob.append(jnp.where(lo8, halves[0], halves[1]))
    return jnp.concatenate([oa] + ob, axis=1)


FUSED_ROW_TILE = 512
SEQ_PER_STEP = DEC_BATCH // (BATCH * SEQ // FUSED_ROW_TILE)


def _kv_copies(kt_hbm, vt_hbm, kbuf, vbuf, sems, seq, slot):
    return (pltpu.make_async_copy(kt_hbm.at[seq], kbuf.at[slot], sems.at[0, slot]),
            pltpu.make_async_copy(vt_hbm.at[seq], vbuf.at[slot], sems.at[1, slot]))


def _ffn_attn_kernel(x_ref, wg_ref, wu_ref, wd_ref, g_ref, b_ref,
                     qa_ref, kan_ref, van_ref, kt_hbm, vt_hbm, cnt_c_ref, cnt_n_ref,
                     qb_ref, kbn_ref, vbn_ref, kbc_ref, vbc_ref, maskb_ref, sinkcol_ref,
                     y_ref, o_ref, h_ref, kbuf, vbuf, sems, kb_s, vb_s):
    step = pl.program_id(0)
    n_steps = pl.num_programs(0)
    copies = functools.partial(_kv_copies, kt_hbm, vt_hbm, kbuf, vbuf, sems)

    @pl.when(step == 0)
    def _():
        for cp in copies(0, 0):
            cp.start()

    x = x_ref[...]
    xb = x.astype(BF16)
    n_ff = D_FF // FF_CHUNK
    per_seq = -(-n_ff // SEQ_PER_STEP)
    cnt_c, cnt_n, mask_b, sink_col = cnt_c_ref[...], cnt_n_ref[...], maskb_ref[...], sinkcol_ref[...]
    for s in range(SEQ_PER_STEP):
        slot = s % 2
        seq = step * SEQ_PER_STEP + s
        for cp in copies(seq, slot):
            cp.wait()
        if s + 1 < SEQ_PER_STEP:
            for cp in copies(seq + 1, 1 - slot):
                cp.start()
        else:
            @pl.when(step + 1 < n_steps)
            def _():
                for cp in copies(seq + 1, 1 - slot):
                    cp.start()
        rows = slice(s * DEC_SEQ, (s + 1) * DEC_SEQ)
        o_ref[rows, :] = _sample_attend(
            qa_ref[rows, :], kan_ref[rows, :], van_ref[rows, :], kbuf.at[slot], vbuf.at[slot], cnt_c, cnt_n,
            qb_ref[rows, :], kbn_ref[rows, :], vbn_ref[rows, :], kbc_ref[s], vbc_ref[s], mask_b, sink_col, kb_s, vb_s)
        for c in range(s * per_seq, min((s + 1) * per_seq, n_ff)):
            sl = slice(c * FF_CHUNK, (c + 1) * FF_CHUNK)
            gate = _dot(xb, wg_ref[:, sl])
            up = _dot(xb, wu_ref[:, sl])
            h_ref[:, sl] = (gate * jax.nn.sigmoid(gate) * up).astype(BF16)
    y = DN_ALPHA * x + FFN_RES * _dot(h_ref[...], wd_ref[...])
    y_ref[...] = _layer_norm(y, g_ref[...], b_ref[...])


def _ffn_with_sample_attention(x, wg, wu, wd, g, b, layer, which,
                               qa, kan, van, cache_ak, cache_av, qb, kbn, vbn, cache_bk, cache_bv, sinks):
    n = x.shape[0]
    tm = FUSED_ROW_TILE
    assert n // tm * SEQ_PER_STEP == DEC_BATCH and SEQ_PER_STEP % 2 == 0
    cnt_c, cnt_n, mask_b = _sample_tables()
    sink_col = jnp.broadcast_to(jnp.repeat(sinks.astype(F32), DEC_SEQ)[:, None], (N_HEADS_B * DEC_SEQ, LANES))
    row = pl.BlockSpec((tm, D_MODEL), lambda i: (i, 0))
    pick = lambda r, c: pl.BlockSpec((None, None, r, c), lambda i: (layer, which, 0, 0), pipeline_mode=pl.Buffered(1))
    new = lambda width: pl.BlockSpec((SEQ_PER_STEP * DEC_SEQ, width), lambda i: (i, 0))
    cache_b = pl.BlockSpec((SEQ_PER_STEP, WIN_B, D_BKV), lambda i: (i, 0, 0))
    hbm = pl.BlockSpec(memory_space=pl.ANY)
    rows = N_HEADS_A * DEC_SEQ
    kv_slot = (2, N_HEADS_A, HEAD_DIM, WIN_A)
    return pl.pallas_call(
        _ffn_attn_kernel,
        grid=(n // tm,),
        in_specs=[row, pick(D_MODEL, D_FF), pick(D_MODEL, D_FF), pick(D_FF, D_MODEL),
                  _resident((1, D_MODEL)), _resident((1, D_MODEL)),
                  new(D_A), new(D_A), new(D_A), hbm, hbm, _resident(cnt_c.shape), _resident(cnt_n.shape),
                  new(D_BQ), new(D_BKV), new(D_BKV), cache_b, cache_b,
                  _resident((rows, NKB_PAD)), _resident((rows, LANES))],
        out_specs=[row, new(D_A + D_BQ)],
        out_shape=[jax.ShapeDtypeStruct((n, D_MODEL), F32),
                   jax.ShapeDtypeStruct((DEC_BATCH * DEC_SEQ, D_A + D_BQ), F32)],
        scratch_shapes=[pltpu.VMEM((tm, D_FF), BF16), pltpu.VMEM(kv_slot, F32), pltpu.VMEM(kv_slot, F32),
                        pltpu.SemaphoreType.DMA((2, 2)),
                        pltpu.VMEM((NKB_PAD, D_BKV), BF16), pltpu.VMEM((NKB_PAD, D_BKV), BF16)],
        compiler_params=_params(1, vmem=60 * 1024 * 1024),
        name="ffn_attn_sample",
    )(x, wg, wu, wd, g, b, qa, kan, van, cache_ak, cache_av, cnt_c, cnt_n, qb, kbn, vbn, cache_bk, cache_bv,
      mask_b, sink_col)


def _attn_out_prompt_kernel(o1_ref, l1_ref, o4_ref, l4_ref, o16_ref, l16_ref, ob_ref, y_ref, w_ref, g_ref, b_ref,
                            out_ref, slab_ref, oa_ref):
    tm = y_ref.shape[0]
    n_chunks = D_A // LANES
    slabs = {}
    base = 0
    for name, dil, src, width in (("o4", 4, o4_ref, n_chunks), ("l4", 4, l4_ref, 1),
                                  ("o16", 16, o16_ref, n_chunks), ("l16", 16, l16_ref, 1)):
        slabs[name] = base
        for r in range(dil):
            for c in range(width):
                slab_ref[base + c, pl.ds(r, tm // dil, stride=dil), :] = src[r, :, c * LANES:(c + 1) * LANES]
        base += width
    l1, l4, l16 = l1_ref[...], slab_ref[slabs["l4"]], slab_ref[slabs["l16"]]
    m = jnp.maximum(jnp.maximum(l1, l4), l16)
    e1, e4, e16 = jnp.exp2(l1 - m), jnp.exp2(l4 - m), jnp.exp2(l16 - m)
    inv = 1.0 / (e1 + e4 + e16)
    row = lax.broadcasted_iota(jnp.int32, (LANES, D_A), 0)
    head_of_col = jnp.right_shift(lax.broadcasted_iota(jnp.int32, (LANES, D_A), 1), int(math.log2(HEAD_DIM)))
    spread = jnp.where(row == head_of_col * LSE_LANES, 1.0, 0.0).astype(BF16)

    def per_head_lanes(w):
        hi = w.astype(BF16)
        lo = (w - hi.astype(F32)).astype(BF16)
        return _dot(hi, spread) + _dot(lo, spread)

    w1, w4, w16 = per_head_lanes(e1 * inv), per_head_lanes(e4 * inv), per_head_lanes(e16 * inv)
    for c in range(n_chunks):
        sl = slice(c * LANES, (c + 1) * LANES)
        oa = (w1[:, sl] * o1_ref[:, sl] + w4[:, sl] * slab_ref[slabs["o4"] + c]
              + w16[:, sl] * slab_ref[slabs["o16"] + c])
        oa_ref[:, sl] = oa.astype(BF16)
    mix = _dot(oa_ref[...], w_ref[0:D_A, :]) + _dot(ob_ref[...], w_ref[D_A:, :])
    out_ref[...] = _layer_norm(DN_ALPHA * y_ref[...] + mix, g_ref[...], b_ref[...])


def _attn_out_prompt(pats, ob, y, w, g, b, seq):
    n = y.shape[0]
    tm = ROW_TILE
    tps = seq // tm
    half = pl.BlockSpec((tm, D_A), lambda i: (i, 0))
    full = pl.BlockSpec((tm, D_MODEL), lambda i: (i, 0))
    lse1 = pl.BlockSpec((tm, LANES), lambda i: (i, 0))
    planes = lambda dil, width: pl.BlockSpec((None, dil, tm // dil, width), lambda i: (i // tps, 0, i % tps, 0))
    (o1, l1), (o4, l4), (o16, l16) = pats
    return pl.pallas_call(
        _attn_out_prompt_kernel,
        grid=(n // tm,),
        in_specs=[half, lse1, planes(4, D_A), planes(4, LANES), planes(16, D_A), planes(16, LANES), half, full,
                  _resident((D_MODEL, D_MODEL)), _resident((1, D_MODEL)), _resident((1, D_MODEL))],
        out_specs=full,
        out_shape=jax.ShapeDtypeStruct((n, D_MODEL), F32),
        scratch_shapes=[pltpu.VMEM((2 * (D_A // LANES + 1), tm, LANES), F32), pltpu.VMEM((tm, D_A), BF16)],
        compiler_params=_params(1),
        name="attn_out_prompt",
    )(o1, l1, o4, l4, o16, l16, ob, y, w, g, b)


def _mix_out_kernel(o_ref, y_ref, w_ref, g_ref, b_ref, out_ref):
    mix = _dot(o_ref[...].astype(BF16), w_ref[...])
    out_ref[...] = _layer_norm(DN_ALPHA * y_ref[...] + mix, g_ref[...], b_ref[...])


def _attn_out_sample(o, y, w, g, b):
    n = y.shape[0]
    tm = min(ROW_TILE, n)
    full = pl.BlockSpec((tm, D_MODEL), lambda i: (i, 0))
    return pl.pallas_call(
        _mix_out_kernel,
        grid=(n // tm,),
        in_specs=[full, full, _resident((D_MODEL, D_MODEL)), _resident((1, D_MODEL)), _resident((1, D_MODEL))],
        out_specs=full,
        out_shape=jax.ShapeDtypeStruct((n, D_MODEL), F32),
        compiler_params=_params(1),
        name="attn_out_sample",
    )(o, y, w, g, b)


def _ssm_discretize(lam_re, lam_im, log_dt, b_re, b_im):
    dt = jnp.exp(log_dt.astype(F32))[:, None]
    lr, li = lam_re.astype(F32), lam_im.astype(F32)
    mag = jnp.exp(lr * dt)
    ab_re, ab_im = mag * jnp.cos(li * dt), mag * jnp.sin(li * dt)
    nr, ni = ab_re - 1.0, ab_im
    den = lr * lr + li * li
    fr, fi = (nr * lr + ni * li) / den, (ni * lr - nr * li) / den
    bb_re = fr[..., None] * b_re - fi[..., None] * b_im
    bb_im = fr[..., None] * b_im + fi[..., None] * b_re
    return ab_re, ab_im, bb_re, bb_im


def _ssm_matrices(bb_re, bb_im, c_re, c_im):
    gpc = LANES // SSM_GROUP
    eye = jnp.eye(gpc, dtype=F32)

    def in_blocks(bb):
        a = bb.reshape(N_LCHUNK, gpc, SSM_STATE, SSM_GROUP)
        return jnp.einsum("jgpn,gh->jgnhp", a, eye).reshape(N_LCHUNK, LANES, ST_CHUNK)

    def out_blocks(cc):
        a = cc.reshape(N_LCHUNK, gpc, SSM_GROUP, SSM_STATE)
        return jnp.einsum("jgnp,gh->jgphn", a, eye).reshape(N_LCHUNK, ST_CHUNK, LANES)

    bmat = jnp.concatenate([in_blocks(bb_re), in_blocks(bb_im)], -1)
    cmat = jnp.concatenate([out_blocks(c_re), -out_blocks(c_im)], 1)
    return bmat, cmat


CHUNK = 8


def _dot3(a, b):
    a_hi, b_hi = a.astype(BF16), b.astype(BF16)
    a_lo, b_lo = (a - a_hi.astype(F32)).astype(BF16), (b - b_hi.astype(F32)).astype(BF16)
    return _dot(a_hi, b_hi) + _dot(a_hi, b_lo) + _dot(a_lo, b_hi)


def _chunk_weights_kernel(ar_ref, ai_ref, acr_ref, aci_ref, bmat_ref, cmat_ref,
                          we_ref, ws_ref, wi_ref, a8r_ref, a8i_ref):
    def powers(r, i, n):
        out = [(jnp.ones_like(r), jnp.zeros_like(r))]
        for _ in range(n):
            out.append(_cmul(out[-1][0], out[-1][1], r, i))
        return out

    row_pow = powers(ar_ref[...], ai_ref[...], CHUNK)
    col_pow = powers(acr_ref[...], aci_ref[...], CHUNK)
    b_re, b_im = bmat_ref[:, 0:ST_CHUNK], bmat_ref[:, ST_CHUNK:]
    c_re, c_im = cmat_ref[0:ST_CHUNK, :], -cmat_ref[ST_CHUNK:, :]

    def scaled_b(power):
        pr, pi = row_pow[power]
        return jnp.concatenate([b_re * pr - b_im * pi, b_re * pi + b_im * pr], axis=1)

    taps = []
    for tau in range(CHUNK):
        sb = scaled_b(tau)
        we_ref[(CHUNK - 1 - tau) * LANES:(CHUNK - tau) * LANES, :] = sb.astype(BF16)
        taps.append(_dot3(sb, cmat_ref[...]).astype(BF16))
    for k in range(CHUNK):
        cols = slice(k * LANES, (k + 1) * LANES)
        qr, qi = col_pow[k + 1]
        ws_ref[0:ST_CHUNK, cols] = (c_re * qr - c_im * qi).astype(BF16)
        ws_ref[ST_CHUNK:, cols] = (-(c_re * qi + c_im * qr)).astype(BF16)
    zero = jnp.zeros((LANES, LANES), BF16)
    for k_in in range(CHUNK):
        for k_out in range(CHUNK):
            wi_ref[k_in * LANES:(k_in + 1) * LANES, k_out * LANES:(k_out + 1) * LANES] = (
                taps[k_out - k_in] if k_out >= k_in else zero)
    a8r_ref[...], a8i_ref[...] = row_pow[CHUNK]


def _chunk_weights(a_re, a_im, bmat, cmat):
    cols = lambda a: jnp.broadcast_to(a.reshape(N_LCHUNK, ST_CHUNK, 1), (N_LCHUNK, ST_CHUNK, LANES))
    per_j = lambda r, c: pl.BlockSpec((None, r, c), lambda j: (j, 0, 0))
    wide = 2 * ST_CHUNK
    return pl.pallas_call(
        _chunk_weights_kernel,
        grid=(N_LCHUNK,),
        in_specs=[per_j(1, ST_CHUNK), per_j(1, ST_CHUNK), per_j(ST_CHUNK, LANES), per_j(ST_CHUNK, LANES),
                  per_j(LANES, wide), per_j(wide, LANES)],
        out_specs=[per_j(CHUNK * LANES, wide), per_j(wide, CHUNK * LANES), per_j(CHUNK * LANES, CHUNK * LANES),
                   per_j(1, ST_CHUNK), per_j(1, ST_CHUNK)],
        out_shape=[jax.ShapeDtypeStruct((N_LCHUNK, CHUNK * LANES, wide), BF16),
                   jax.ShapeDtypeStruct((N_LCHUNK, wide, CHUNK * LANES), BF16),
                   jax.ShapeDtypeStruct((N_LCHUNK, CHUNK * LANES, CHUNK * LANES), BF16),
                   jax.ShapeDtypeStruct((N_LCHUNK, 1, ST_CHUNK), F32),
                   jax.ShapeDtypeStruct((N_LCHUNK, 1, ST_CHUNK), F32)],
        compiler_params=_params(1),
        name="ssm_chunk_weights",
    )(a_re, a_im, cols(a_re), cols(a_im), bmat, cmat)


SEG_TILE = ROW_TILE // N_SEG


def _ssm_in_prompt_kernel(x_ref, w_ref, o_ref):
    x = x_ref[...].reshape(N_SEG * SEG_TILE, D_MODEL)
    u = _dot(x.astype(BF16), w_ref[...])
    for s in range(N_SEG):
        for c in range(N_LCHUNK):
            o_ref[c, pl.ds(s, SEG_TILE, stride=N_SEG), :] = u[s * SEG_TILE:(s + 1) * SEG_TILE, c * LANES:(c + 1) * LANES]


def _ssm_in_prompt(y, w):
    return pl.pallas_call(
        _ssm_in_prompt_kernel,
        grid=(BATCH, SEG_LEN // SEG_TILE),
        in_specs=[pl.BlockSpec((None, N_SEG, SEG_TILE, D_MODEL), lambda b, t: (b, 0, t, 0)),
                  _resident((D_MODEL, D_MODEL))],
        out_specs=pl.BlockSpec((None, N_LCHUNK, ROW_TILE, LANES), lambda b, t: (b, 0, t, 0)),
        out_shape=jax.ShapeDtypeStruct((BATCH, N_LCHUNK, SEQ, LANES), F32),
        compiler_params=_params(2),
        name="ssm_in_prompt",
    )(y.reshape(BATCH, N_SEG, SEG_LEN, D_MODEL), w)


def _ssm_in_sample_kernel(x_ref, w_ref, o_ref, slab_ref):
    u = _dot(x_ref[...].astype(BF16), w_ref[...])
    for c in range(N_LCHUNK):
        slab_ref[c] = u[:, c * LANES:(c + 1) * LANES]
    for l in range(DEC_SEQ):
        for c in range(N_LCHUNK):
            o_ref[c, l * DEC_BATCH:(l + 1) * DEC_BATCH, :] = slab_ref[c, pl.ds(l, DEC_BATCH, stride=DEC_SEQ), :]


def _ssm_in_sample(y, w):
    n = DEC_BATCH * DEC_SEQ
    return pl.pallas_call(
        _ssm_in_sample_kernel,
        grid=(1,),
        in_specs=[_resident((n, D_MODEL)), _resident((D_MODEL, D_MODEL))],
        out_specs=pl.BlockSpec((N_LCHUNK, n, LANES), lambda i: (0, 0, 0)),
        out_shape=jax.ShapeDtypeStruct((N_LCHUNK, n, LANES), F32),
        scratch_shapes=[pltpu.VMEM((N_LCHUNK, n, LANES), F32)],
        compiler_params=_params(1),
        name="ssm_in_sample",
    )(y, w)


def _cmul(ar, ai, br, bi):
    return ar * br - ai * bi, ar * bi + ai * br


def _scan_prompt_kernel(u_ref, we_ref, ws_ref, wi_ref, a8r_ref, a8i_ref, d_ref, h0r_ref, h0i_ref,
                        y_ref, hnr_ref, hni_ref, e_s, hs_s):
    n_chunks = u_ref.shape[0]
    rows = n_chunks * N_SEG
    u_flat = jnp.concatenate([u_ref[:, k].reshape(rows, LANES) for k in range(CHUNK)], axis=1)
    ub = u_flat.astype(BF16)
    e_s[...] = _dot(ub, we_ref[...])
    a8r1, a8i1 = a8r_ref[...], a8i_ref[...]
    a8r = jnp.broadcast_to(a8r1, (N_SEG, ST_CHUNK))
    a8i = jnp.broadcast_to(a8i1, (N_SEG, ST_CHUNK))

    def advance(row, hr, hi):
        er = e_s[pl.ds(row, N_SEG), 0:ST_CHUNK]
        ei = e_s[pl.ds(row, N_SEG), ST_CHUNK:2 * ST_CHUNK]
        return a8r * hr - a8i * hi + er, a8r * hi + a8i * hr + ei

    def pass1(c, carry):
        return advance(pl.multiple_of(c * N_SEG, N_SEG), *carry)

    zero = jnp.zeros((N_SEG, ST_CHUNK), F32)
    er, ei = lax.fori_loop(0, n_chunks, pass1, (zero, zero), unroll=8)

    pr, pi = a8r1, a8i1
    for _ in range(int(math.log2(n_chunks))):
        pr, pi = _cmul(pr, pi, pr, pi)
    hr, hi = h0r_ref[...], h0i_ref[...]
    starts_r, starts_i = [], []
    for s in range(N_SEG):
        starts_r.append(hr)
        starts_i.append(hi)
        gr, gi = _cmul(pr, pi, hr, hi)
        hr, hi = gr + er[s:s + 1], gi + ei[s:s + 1]
    hnr_ref[...] = hr
    hni_ref[...] = hi
    init = (jnp.concatenate(starts_r, 0), jnp.concatenate(starts_i, 0))

    def pass2(c, carry):
        row = pl.multiple_of(c * N_SEG, N_SEG)
        hs_s[pl.ds(row, N_SEG), 0:ST_CHUNK] = carry[0]
        hs_s[pl.ds(row, N_SEG), ST_CHUNK:2 * ST_CHUNK] = carry[1]
        return advance(row, *carry)

    lax.fori_loop(0, n_chunks, pass2, init, unroll=8)
    y = _dot(hs_s[...].astype(BF16), ws_ref[...]) + _dot(ub, wi_ref[...]) + d_ref[...] * u_flat
    for k in range(CHUNK):
        y_ref[:, k] = y[:, k * LANES:(k + 1) * LANES].reshape(n_chunks, N_SEG, LANES)


def _scan_prompt(u, weights, d_skip, h0r, h0i):
    bsz, _, seq, _ = u.shape
    n_chunks = seq // (CHUNK * N_SEG)
    rows = n_chunks * N_SEG
    wide = 2 * ST_CHUNK
    split = lambda a: a.reshape(bsz, N_LCHUNK, n_chunks, CHUNK, N_SEG, LANES)
    chunk = pl.BlockSpec((None, None, n_chunks, CHUNK, N_SEG, LANES), lambda j, b: (b, j, 0, 0, 0, 0))
    per_j = lambda r, c: pl.BlockSpec((None, r, c), lambda j, b: (j, 0, 0))
    state = pl.BlockSpec((None, 1, ST_CHUNK), lambda j, b: (b, 0, j))
    y, hr, hi = pl.pallas_call(
        _scan_prompt_kernel,
        grid=(N_LCHUNK, bsz),
        in_specs=[chunk, per_j(CHUNK * LANES, wide), per_j(wide, CHUNK * LANES), per_j(CHUNK * LANES, CHUNK * LANES),
                  per_j(1, ST_CHUNK), per_j(1, ST_CHUNK), per_j(1, CHUNK * LANES), state, state],
        out_specs=[chunk, state, state],
        out_shape=[jax.ShapeDtypeStruct((bsz, N_LCHUNK, n_chunks, CHUNK, N_SEG, LANES), F32),
                   jax.ShapeDtypeStruct((bsz, 1, N_STATE), F32), jax.ShapeDtypeStruct((bsz, 1, N_STATE), F32)],
        scratch_shapes=[pltpu.VMEM((rows, wide), F32), pltpu.VMEM((rows, wide), F32)],
        compiler_params=_params(2),
        name="ssm_scan_prompt",
    )(split(u), *weights, d_skip, h0r, h0i)
    return y.reshape(u.shape), hr, hi


def _scan_sample_kernel(u_ref, bmat_ref, cmat_ref, are_ref, aim_ref, d_ref, h0r_ref, h0i_ref,
                        y_ref, hnr_ref, hni_ref):
    a_re, a_im = are_ref[...], aim_ref[...]
    hr, hi = h0r_ref[...].T, h0i_ref[...].T
    for l in range(DEC_SEQ):
        rows = slice(l * DEC_BATCH, (l + 1) * DEC_BATCH)
        u = u_ref[rows, :]
        bu = _dot(u.astype(BF16), bmat_ref[...])
        gr, gi = _cmul(a_re, a_im, hr, hi)
        hr, hi = gr + bu[:, :ST_CHUNK], gi + bu[:, ST_CHUNK:]
        h = jnp.concatenate([hr, hi], axis=1).astype(BF16)
        y_ref[rows, :] = _dot(h, cmat_ref[...]) + d_ref[...] * u
    hnr_ref[...] = hr.T
    hni_ref[...] = hi.T


def _scan_sample(u, bmat, cmat, a_re, a_im, d_skip, h0r, h0i):
    n = DEC_SEQ * DEC_BATCH
    chunk = pl.BlockSpec((None, n, LANES), lambda j: (j, 0, 0))
    per_j = lambda r, c: pl.BlockSpec((None, r, c), lambda j: (j, 0, 0))
    state = pl.BlockSpec((ST_CHUNK, DEC_BATCH), lambda j: (j, 0))
    return pl.pallas_call(
        _scan_sample_kernel,
        grid=(N_LCHUNK,),
        in_specs=[chunk, per_j(LANES, 2 * ST_CHUNK), per_j(2 * ST_CHUNK, LANES), per_j(1, ST_CHUNK),
                  per_j(1, ST_CHUNK), per_j(1, LANES), state, state],
        out_specs=[chunk, state, state],
        out_shape=[jax.ShapeDtypeStruct((N_LCHUNK, n, LANES), F32),
                   jax.ShapeDtypeStruct((N_STATE, DEC_BATCH), F32), jax.ShapeDtypeStruct((N_STATE, DEC_BATCH), F32)],
        compiler_params=_params(1),
        name="ssm_scan_sample",
    )(u, bmat, cmat, a_re, a_im, d_skip, h0r, h0i)


def _ssm_out_kernel(sample, s_ref, y_ref, wglu_ref, bglu_ref, wout_ref, g_ref, b_ref, out_ref, z_ref):
    n_rows = z_ref.shape[1]
    if sample:
        parts = [(pl.ds(l, DEC_BATCH, stride=DEC_SEQ), slice(l * DEC_BATCH, (l + 1) * DEC_BATCH))
                 for l in range(DEC_SEQ)]
    else:
        parts = [(slice(s * SEG_TILE, (s + 1) * SEG_TILE), pl.ds(s, SEG_TILE, stride=N_SEG)) for s in range(N_SEG)]
    for c in range(N_LCHUNK):
        for tok_rows, slab_rows in parts:
            z_ref[c, tok_rows, :] = s_ref[c, slab_rows, :]
    z = jax.nn.gelu(jnp.concatenate([z_ref[c] for c in range(N_LCHUNK)], axis=1))
    gate = jax.nn.sigmoid(_dot(z.astype(BF16), wglu_ref[...]) + bglu_ref[...])
    mix = _dot((z * gate).astype(BF16), wout_ref[...])
    res = y_ref[...].reshape(n_rows, D_MODEL)
    out = _layer_norm(DN_ALPHA * res + mix, g_ref[...], b_ref[...])
    out_ref[...] = out.reshape(out_ref.shape)


def _ssm_out(s, y, s_spec, y_spec, grid, rows, sample, w_glu, b_glu, w_out, g, b, name):
    return pl.pallas_call(
        functools.partial(_ssm_out_kernel, sample),
        grid=grid,
        in_specs=[s_spec, y_spec, _resident((D_MODEL, D_MODEL)), _resident((1, D_MODEL)),
                  _resident((D_MODEL, D_MODEL)), _resident((1, D_MODEL)), _resident((1, D_MODEL))],
        out_specs=y_spec,
        out_shape=jax.ShapeDtypeStruct(y.shape, F32),
        scratch_shapes=[pltpu.VMEM((N_LCHUNK, rows, LANES), F32)],
        compiler_params=_params(len(grid)),
        name=name,
    )(s, y, w_glu, b_glu, w_out, g, b)


def _attn_prompt(yp, w_in, sinks, w_out, g, b):
    cos_p, sin_p = _rope_tables(jnp.arange(SEQ))
    tiles_per_seq = SEQ // ROW_TILE
    qa, ka, va, qb, kb, vb, *extra = _attn_proj(
        yp, w_in, cos_p, sin_p, lambda i: i % tiles_per_seq, BF16, prompt_seq=(BATCH, SEQ))
    dilated, tails = extra[:6], extra[6:]
    seq3 = lambda a: a.reshape(BATCH, SEQ, a.shape[-1])
    plane1 = lambda a: a.reshape(BATCH, 1, SEQ, D_A)
    o1, l1 = _band_a(plane1(qa), plane1(ka), plane1(va))
    pats = [(o1.reshape(BATCH * SEQ, D_A), l1.reshape(BATCH * SEQ, LANES))]
    for i in range(len(DILATIONS) - 1):
        pats.append(_band_a(*dilated[3 * i:3 * i + 3]))
    ob = _band_b(seq3(qb), seq3(kb), seq3(vb), sinks)
    yp = _attn_out_prompt(pats, ob, yp, w_out, g, b, SEQ)
    heads = lambda a, nh: jnp.transpose(a.reshape(BATCH, nh, HEAD_DIM, a.shape[-1]), (0, 3, 1, 2))[None]
    prompt_cache = (heads(tails[0], N_HEADS_A), heads(tails[1], N_HEADS_A),
                    heads(tails[2], N_KV_B), heads(tails[3], N_KV_B))
    return yp, prompt_cache


def _attn_sample_path(ys, cache_ak, cache_av, cache_bk, cache_bv, w_in, sinks, w_out, g, b, yp, ffn_args):
    cos_s, sin_s = _rope_tables(PAST_LEN + jnp.arange(DEC_SEQ))
    reps = DEC_BATCH * DEC_SEQ // DEC_SEQ
    cos_s, sin_s = jnp.tile(cos_s, (reps, 1)), jnp.tile(sin_s, (reps, 1))
    qa, ka, va, qb, kb, vb = _attn_proj(ys, w_in, cos_s, sin_s, lambda i: i, F32)
    stored = lambda a: jnp.transpose(a, (0, 2, 3, 1))
    yp, o = _ffn_with_sample_attention(
        yp, *ffn_args, qa, ka, va, stored(cache_ak), stored(cache_av),
        qb, kb, vb, cache_bk.reshape(DEC_BATCH, WIN_B, D_BKV), cache_bv.reshape(DEC_BATCH, WIN_B, D_BKV), sinks)
    ys = _attn_out_sample(o, ys, w_out, g, b)
    new = lambda a, nh: a.reshape(1, DEC_BATCH, DEC_SEQ, nh, HEAD_DIM)
    sample_cache = (new(ka, N_HEADS_A), new(va, N_HEADS_A), new(kb, N_KV_B), new(vb, N_KV_B))
    return yp, ys, sample_cache


def _attn_layer(yp, ys, cache_ak, cache_av, cache_bk, cache_bv, w_in, sinks, w_out, g, b, ffn_args):
    w_in = w_in.astype(BF16)
    w_out = w_out.astype(BF16)
    yp, prompt_cache = _attn_prompt(yp, w_in, sinks, w_out, g, b)
    yp, ys, sample_cache = _attn_sample_path(ys, cache_ak, cache_av, cache_bk, cache_bv, w_in, sinks, w_out, g, b,
                                             yp, ffn_args)
    return yp, ys, prompt_cache, sample_cache


def _ssm_layer(yp, ys, state_re, state_im, w_in, lam_re, lam_im, log_dt, b_re, b_im, c_re, c_im, d_skip,
               w_glu, b_glu, w_out, g, b):
    w_in, w_glu, w_out = w_in.astype(BF16), w_glu.astype(BF16), w_out.astype(BF16)
    b_glu = b_glu.reshape(1, D_MODEL)
    mats = _ssm_prepare(lam_re, lam_im, log_dt, b_re, b_im, c_re, c_im, d_skip)
    yp, prompt_state = _ssm_prompt(yp, w_in, mats, w_glu, b_glu, w_out, g, b)
    ys, sample_state = _ssm_sample(ys, state_re, state_im, w_in, mats, w_glu, b_glu, w_out, g, b)
    return yp, ys, prompt_state, sample_state


def _ssm_prepare(lam_re, lam_im, log_dt, b_re, b_im, c_re, c_im, d_skip):
    ab_re, ab_im, bb_re, bb_im = _ssm_discretize(lam_re, lam_im, log_dt, b_re, b_im)
    bmat, cmat = _ssm_matrices(bb_re, bb_im, c_re, c_im)
    a_re = ab_re.reshape(N_LCHUNK, 1, ST_CHUNK)
    a_im = ab_im.reshape(N_LCHUNK, 1, ST_CHUNK)
    d3 = d_skip.astype(F32).reshape(N_LCHUNK, 1, LANES)
    prompt = (_chunk_weights(a_re, a_im, bmat, cmat), jnp.tile(d3, (1, 1, CHUNK)))
    sample = (bmat.astype(BF16), cmat.astype(BF16), a_re, a_im, d3)
    return prompt, sample


def _ssm_prompt(yp, w_in, mats, w_glu, b_glu, w_out, g, b):
    weights, d_tiled = mats[0]
    up = _ssm_in_prompt(yp, w_in)
    zero = jnp.zeros((BATCH, 1, N_STATE), F32)
    sp, pr, pi = _scan_prompt(up, weights, d_tiled, zero, zero)
    yp = _ssm_out(sp, yp.reshape(BATCH, N_SEG, SEG_LEN, D_MODEL),
                  pl.BlockSpec((None, N_LCHUNK, ROW_TILE, LANES), lambda bb, t: (bb, 0, t, 0)),
                  pl.BlockSpec((None, N_SEG, SEG_TILE, D_MODEL), lambda bb, t: (bb, 0, t, 0)),
                  (BATCH, SEG_LEN // SEG_TILE), ROW_TILE, False, w_glu, b_glu, w_out, g, b,
                  "ssm_out_prompt").reshape(BATCH * SEQ, D_MODEL)
    prompt_state = (pr.reshape(1, BATCH, N_SSM_GROUPS, SSM_STATE), pi.reshape(1, BATCH, N_SSM_GROUPS, SSM_STATE))
    return yp, prompt_state


def _ssm_sample(ys, state_re, state_im, w_in, mats, w_glu, b_glu, w_out, g, b):
    bmat, cmat, a_re, a_im, d3 = mats[1]
    us = _ssm_in_sample(ys, w_in)
    stored = lambda a: jnp.transpose(a, (1, 2, 0)).reshape(N_STATE, DEC_BATCH)
    logical = lambda a: jnp.transpose(a.reshape(N_SSM_GROUPS, SSM_STATE, DEC_BATCH), (2, 0, 1))[None]
    ss, sr, si = _scan_sample(us, bmat, cmat, a_re, a_im, d3, stored(state_re), stored(state_im))
    n = DEC_BATCH * DEC_SEQ
    ys = _ssm_out(ss, ys,
                  pl.BlockSpec((N_LCHUNK, n, LANES), lambda i: (0, 0, 0)),
                  pl.BlockSpec((n, D_MODEL), lambda i: (0, 0)),
                  (1,), n, True, w_glu, b_glu, w_out, g, b, "ssm_out_sample")
    return ys, (logical(sr), logical(si))


def kernel(x_prompt, x_sample, cache_a_k, cache_a_v, cache_b_k, cache_b_v, state_c_re, state_c_im, ln_g, ln_b, ffn_w_gate, ffn_w_up, ffn_w_down, attn_w_in, attn_sinks, attn_w_out, ssm_w_in, ssm_lambda_re, ssm_lambda_im, ssm_log_dt, ssm_b_re, ssm_b_im, ssm_c_re, ssm_c_im, ssm_d, ssm_w_glu, ssm_b_glu, ssm_w_out):
    yp = x_prompt.reshape(BATCH * SEQ, D_MODEL)
    ys = x_sample.reshape(DEC_BATCH * DEC_SEQ, D_MODEL)
    ln = lambda l, k: (ln_g[l, k].reshape(1, D_MODEL), ln_b[l, k].reshape(1, D_MODEL))

    wg, wu, wd = ffn_w_gate.astype(BF16), ffn_w_up.astype(BF16), ffn_w_down.astype(BF16)

    def ffn_pair(yp, ys, l, k, ln_idx):
        g, b = ln(l, ln_idx)
        return _ffn(yp, wg, wu, wd, g, b, l, k), _ffn(ys, wg, wu, wd, g, b, l, k)

    yp, ys = ffn_pair(yp, ys, 0, 0, 0)
    yp, ys, p_cache, s_cache = _attn_layer(yp, ys, cache_a_k[0], cache_a_v[0], cache_b_k[0], cache_b_v[0],
                                           attn_w_in[0], attn_sinks[0], attn_w_out[0], *ln(0, 1),
                                           ffn_args=(wg, wu, wd, *ln(0, 2), 0, 1))
    ys = _ffn(ys, wg, wu, wd, *ln(0, 2), 0, 1)
    yp, ys = ffn_pair(yp, ys, 1, 0, 0)
    yp, ys, p_state, s_state = _ssm_layer(yp, ys, state_c_re[0], state_c_im[0], ssm_w_in[0], ssm_lambda_re[0],
                                          ssm_lambda_im[0], ssm_log_dt[0], ssm_b_re[0], ssm_b_im[0], ssm_c_re[0],
                                          ssm_c_im[0], ssm_d[0], ssm_w_glu[0], ssm_b_glu[0], ssm_w_out[0], *ln(1, 1))
    yp, ys = ffn_pair(yp, ys, 1, 1, 2)
    return (yp.reshape(BATCH, SEQ, D_MODEL), ys.reshape(DEC_BATCH, DEC_SEQ, D_MODEL),
            *p_cache, *p_state, *s_cache, *s_state)
```

```python
import functools
import math

import jax
import jax.numpy as jnp
from jax import lax
from jax.experimental import pallas as pl
from jax.experimental.pallas import tpu as pltpu

F32 = jnp.float32
BF16 = jnp.bfloat16

D_MODEL = 1024
BATCH = 4
SEQ = 4096
DEPTH = 2
DEC_BATCH = 128
DEC_SEQ = 8
PAST_LEN = 16384
HEAD_DIM = 64
N_HEADS_A = 8
DILATIONS = (1, 4, 16)
WIN_A = 2048
N_HEADS_B = 8
N_KV_B = 2
WIN_B = 128
ROPE_THETA = 10000.0
D_A = N_HEADS_A * HEAD_DIM
D_BQ = N_HEADS_B * HEAD_DIM
D_BKV = N_KV_B * HEAD_DIM
D_IN_ATTN = 3 * D_A + D_BQ + 2 * D_BKV
SSM_GROUP = 16
N_SSM_GROUPS = D_MODEL // SSM_GROUP
SSM_STATE = 64
N_STATE = N_SSM_GROUPS * SSM_STATE
D_FF = 2816
DN_ALPHA = (2 * DEPTH) ** 0.25
FFN_RES = 0.5
LN_EPS = 1e-5
ATTN_SCALE = HEAD_DIM ** -0.5
LOG2E = math.log2(math.e)
LSE_LANES = 16

LANES = 128
SUBLANES = 8
MXU_N = 256
VMEM_LIMIT = 56 * 1024 * 1024

ROW_TILE = 512
FFN_ROW_TILE = 1024
FF_CHUNK = MXU_N
TQ = 128
N_SEG = SUBLANES
SEG_LEN = SEQ // N_SEG
N_LCHUNK = D_MODEL // LANES
ST_CHUNK = N_STATE // N_LCHUNK
NK_PAD = WIN_A + LANES
NKB_PAD = 2 * WIN_B

NEG_INF = float("-inf")


def _params(n_axes, vmem=VMEM_LIMIT):
    return pltpu.CompilerParams(dimension_semantics=("arbitrary",) * n_axes, vmem_limit_bytes=vmem)


def _resident(shape):
    return pl.BlockSpec(shape, lambda *_: (0,) * len(shape), pipeline_mode=pl.Buffered(1))


def _layer_norm(x, g, b):
    mu = jnp.mean(x, -1, keepdims=True)
    xc = x - mu
    var = jnp.mean(xc * xc, -1, keepdims=True)
    return xc * lax.rsqrt(var + LN_EPS) * g + b


def _dot(a, b):
    return jnp.dot(a, b, preferred_element_type=F32)


def _dot_nt(a, b):
    return lax.dot_general(a, b, (((1,), (1,)), ((), ())), preferred_element_type=F32)


def _ffn_kernel(x_ref, wg_ref, wu_ref, wd_ref, g_ref, b_ref, o_ref, h_ref):
    x = x_ref[...]
    xb = x.astype(BF16)
    for c in range(D_FF // FF_CHUNK):
        sl = slice(c * FF_CHUNK, (c + 1) * FF_CHUNK)
        gate = _dot(xb, wg_ref[:, sl])
        up = _dot(xb, wu_ref[:, sl])
        h_ref[:, sl] = (gate * jax.nn.sigmoid(gate) * up).astype(BF16)
    y = DN_ALPHA * x + FFN_RES * _dot(h_ref[...], wd_ref[...])
    o_ref[...] = _layer_norm(y, g_ref[...], b_ref[...])


def _ffn(x, wg, wu, wd, g, b, layer=0, which=0):
    n = x.shape[0]
    tm = min(FFN_ROW_TILE, n)
    row = pl.BlockSpec((tm, D_MODEL), lambda i: (i, 0))
    if wg.ndim == 4:
        pick = lambda r, c: pl.BlockSpec((None, None, r, c), lambda i: (layer, which, 0, 0),
                                         pipeline_mode=pl.Buffered(1))
    else:
        pick = lambda r, c: _resident((r, c))
    return pl.pallas_call(
        _ffn_kernel,
        grid=(n // tm,),
        in_specs=[row, pick(D_MODEL, D_FF), pick(D_MODEL, D_FF), pick(D_FF, D_MODEL),
                  _resident((1, D_MODEL)), _resident((1, D_MODEL))],
        out_specs=row,
        out_shape=jax.ShapeDtypeStruct((n, D_MODEL), F32),
        scratch_shapes=[pltpu.VMEM((tm, D_FF), BF16)],
        compiler_params=_params(1),
        name="ffn",
    )(x, wg, wu, wd, g, b)


def _ffn_pair_kernel(n_first, xp_ref, xs_ref, wg_ref, wu_ref, wd_ref, g_ref, b_ref, op_ref, os_ref, h_ref):
    step = pl.program_id(0)
    weights = (wg_ref, wu_ref, wd_ref, g_ref, b_ref)
    pl.when(step < n_first)(functools.partial(_ffn_kernel, xp_ref, *weights, op_ref, h_ref))
    pl.when(step >= n_first)(functools.partial(_ffn_kernel, xs_ref, *weights, os_ref, h_ref))


def _ffn_pair(xp, xs, wg, wu, wd, g, b, layer, which):
    tm = ROW_TILE
    n_first, n_second = xp.shape[0] // tm, xs.shape[0] // tm
    first = pl.BlockSpec((tm, D_MODEL), lambda i: (jnp.minimum(i, n_first - 1), 0))
    second = pl.BlockSpec((tm, D_MODEL), lambda i: (jnp.maximum(i - n_first, 0), 0))
    pick = lambda r, c: pl.BlockSpec((None, None, r, c), lambda i: (layer, which, 0, 0), pipeline_mode=pl.Buffered(1))
    return pl.pallas_call(
        functools.partial(_ffn_pair_kernel, n_first),
        grid=(n_first + n_second,),
        in_specs=[first, second, pick(D_MODEL, D_FF), pick(D_MODEL, D_FF), pick(D_FF, D_MODEL),
                  _resident((1, D_MODEL)), _resident((1, D_MODEL))],
        out_specs=[first, second],
        out_shape=[jax.ShapeDtypeStruct(xp.shape, F32), jax.ShapeDtypeStruct(xs.shape, F32)],
        scratch_shapes=[pltpu.VMEM((tm, D_FF), BF16)],
        compiler_params=_params(1),
        name="ffn_pair",
    )(xp, xs, wg, wu, wd, g, b)


def _rope_tables(pos):
    half = HEAD_DIM // 2
    inv_freq = ROPE_THETA ** (-jnp.arange(half, dtype=F32) / half)
    ang = pos.astype(F32)[:, None] * inv_freq[None, :]
    cos, sin = jnp.cos(ang), jnp.sin(ang)
    cos_t = jnp.concatenate([cos, cos, cos, cos], -1)
    sin_t = jnp.concatenate([-sin, sin, -sin, sin], -1)
    return cos_t, sin_t


def _attn_proj_kernel(tiles_per_seq, x_ref, w_ref, cos_ref, sin_ref, qa_ref, ka_ref, va_ref, qb_ref, kb_ref, vb_ref,
                      *extra):
    xb = x_ref[...].astype(BF16)
    slab_ref = extra[-1] if extra else None
    dilated = extra[:6]
    n_chunks = D_A // LANES
    tm = x_ref.shape[0]

    def keep(tensor, c, val):
        if slab_ref is not None:
            slab_ref[tensor * n_chunks + c] = val
    cos = cos_ref[...]
    sin = sin_ref[...]
    lane = lax.broadcasted_iota(jnp.int32, cos.shape, 1)
    first_half = (lane & (HEAD_DIM // 2)) == 0

    def rope(z):
        rot = jnp.where(first_half, pltpu.roll(z, LANES - HEAD_DIM // 2, 1), pltpu.roll(z, HEAD_DIM // 2, 1))
        return z * cos + rot * sin

    def project(col0, ncols):
        return _dot(xb, w_ref[:, col0:col0 + ncols])

    def rope_chunks(z):
        return [rope(z[:, c * LANES:(c + 1) * LANES]) for c in range(z.shape[1] // LANES)]

    q_scale = ATTN_SCALE * LOG2E if extra else ATTN_SCALE
    col = 0
    for c, r in enumerate(rope_chunks(project(col, D_A))):
        r = r * q_scale
        qa_ref[:, c * LANES:(c + 1) * LANES] = r.astype(qa_ref.dtype)
        keep(0, c, r)
    col += D_A
    for c, r in enumerate(rope_chunks(project(col, D_A))):
        ka_ref[:, c * LANES:(c + 1) * LANES] = r.astype(ka_ref.dtype)
        keep(1, c, r)
    col += D_A
    z = project(col, D_A)
    va_ref[...] = z.astype(va_ref.dtype)
    for c in range(n_chunks):
        keep(2, c, z[:, c * LANES:(c + 1) * LANES])
    col += D_A
    if extra:
        kat_ref, vat_ref, kbt_ref, vbt_ref = extra[6:10]
        tile_in_seq = pl.program_id(0) % tiles_per_seq

        @pl.when(tile_in_seq >= tiles_per_seq - WIN_A // tm)
        def _():
            for c in range(n_chunks):
                kat_ref[c * LANES:(c + 1) * LANES, :] = slab_ref[n_chunks + c].T
                vat_ref[c * LANES:(c + 1) * LANES, :] = slab_ref[2 * n_chunks + c].T

        slab4_ref = extra[-2]
        d1, d2 = DILATIONS[1], DILATIONS[2] // DILATIONS[1]
        plane = tm // d1
        for tensor in range(3):
            out1_ref, out2_ref = dilated[tensor], dilated[3 + tensor]
            for c in range(n_chunks):
                idx = tensor * n_chunks + c
                lanes = slice(c * LANES, (c + 1) * LANES)
                for r in range(d1):
                    rows = slab_ref[idx, pl.ds(r, plane, stride=d1), :]
                    slab4_ref[idx, r * plane:(r + 1) * plane, :] = rows
                    out1_ref[r, :, lanes] = rows.astype(out1_ref.dtype)
                for r in range(d1):
                    for m in range(d2):
                        rows = slab4_ref[idx, pl.ds(r * plane + m, plane // d2, stride=d2), :]
                        out2_ref[r + d1 * m, :, lanes] = rows.astype(out2_ref.dtype)
    for c, r in enumerate(rope_chunks(project(col, D_BQ))):
        qb_ref[:, c * LANES:(c + 1) * LANES] = (r * q_scale).astype(qb_ref.dtype)
    col += D_BQ
    z = project(col, 2 * D_BKV)
    r = rope(z[:, :D_BKV])
    if extra:
        lo = _lane_lo(r.shape)
        for ref, val in ((kb_ref, r), (vb_ref, z[:, D_BKV:])):
            swapped = pltpu.roll(val, HEAD_DIM, 1)
            ref[:, 0:LANES] = jnp.where(lo, val, swapped).astype(ref.dtype)
            ref[:, LANES:] = jnp.where(lo, swapped, val).astype(ref.dtype)
    else:
        kb_ref[...] = r.astype(kb_ref.dtype)
        vb_ref[...] = z[:, D_BKV:].astype(vb_ref.dtype)
    if extra:
        @pl.when(tile_in_seq == tiles_per_seq - 1)
        def _():
            kbt_ref[...] = r[tm - WIN_B:, :].T
            vbt_ref[...] = z[tm - WIN_B:, D_BKV:].T


def _attn_proj(x, w, cos_t, sin_t, table_block, act_dtype, prompt_seq=None):
    n = x.shape[0]
    tm = min(ROW_TILE, n)

    def row(width):
        return pl.BlockSpec((tm, width), lambda i: (i, 0))

    tab = pl.BlockSpec((tm, LANES), lambda i: (table_block(i), 0))
    kv_b = D_BKV if prompt_seq is None else N_KV_B * LANES
    widths = (D_A, D_A, D_A, D_BQ, kv_b, kv_b)
    out_shape = [jax.ShapeDtypeStruct((n, wd), act_dtype) for wd in widths]
    out_specs = [row(wd) for wd in widths]
    scratch = []
    tps = None
    if prompt_seq is not None:
        bsz, seq = prompt_seq
        tps = seq // tm
        for dil in DILATIONS[1:]:
            out_shape += [jax.ShapeDtypeStruct((bsz, dil, seq // dil, D_A), BF16)] * 3
            out_specs += [pl.BlockSpec((None, dil, tm // dil, D_A), lambda i: (i // tps, 0, i % tps, 0))] * 3
        first_tail = tps - WIN_A // tm
        out_shape += [jax.ShapeDtypeStruct((bsz, D_A, WIN_A), F32)] * 2
        out_specs += [pl.BlockSpec((None, D_A, tm), lambda i: (i // tps, 0, jnp.maximum(i % tps - first_tail, 0)))] * 2
        out_shape += [jax.ShapeDtypeStruct((bsz, D_BKV, WIN_B), F32)] * 2
        out_specs += [pl.BlockSpec((None, D_BKV, WIN_B), lambda i: (i // tps, 0, 0))] * 2
        assert DILATIONS[2] == DILATIONS[1] ** 2
        scratch = [pltpu.VMEM((3 * D_A // LANES, tm, LANES), F32)] * 2
    return pl.pallas_call(
        functools.partial(_attn_proj_kernel, tps),
        grid=(n // tm,),
        in_specs=[row(D_MODEL), _resident((D_MODEL, D_IN_ATTN)), tab, tab],
        out_specs=out_specs,
        out_shape=out_shape,
        scratch_shapes=scratch,
        compiler_params=_params(1),
        name="attn_proj",
    )(x, w, cos_t, sin_t)


def _lane_lo(shape):
    return lax.broadcasted_iota(jnp.int32, shape, 1) < HEAD_DIM


def _half_masks_bf16():
    lo = jnp.where(_lane_lo((1, LANES)), 1.0, 0.0).astype(BF16)
    return lo, 1 - lo


def _band_masks(n_heads, t, sub):
    row = lax.broadcasted_iota(jnp.int32, (n_heads * TQ, TQ), 0) & (TQ - 1)
    col = lax.broadcasted_iota(jnp.int32, (n_heads * TQ, TQ), 1)
    shift = jnp.where(t > 0, 0, TQ) if sub == 0 else 0
    return col <= row, col >= row + shift


def _sub_tile_kv(sub, sl, kc_ref, kp_ref, vc_ref, vp_ref):
    if sub == 0:
        return kc_ref[0:TQ, sl], kp_ref[:, sl], vc_ref[0:TQ, sl], vp_ref[:, sl]
    return kc_ref[TQ:2 * TQ, sl], kc_ref[0:TQ, sl], vc_ref[TQ:2 * TQ, sl], vc_ref[0:TQ, sl]


def _band_softmax(qs, kc, kp, vc, vp, mask_c, mask_p):
    s_c = jnp.where(mask_c, _dot_nt(qs, kc), NEG_INF)
    s_p = jnp.where(mask_p, _dot_nt(qs, kp), NEG_INF)
    m = jnp.max(jnp.maximum(s_c, s_p), -1, keepdims=True)
    p_c = jnp.exp2(s_c - m)
    p_p = jnp.exp2(s_p - m)
    den = jnp.sum(p_c + p_p, -1, keepdims=True)
    acc = _dot(p_c.astype(BF16), vc) + _dot(p_p.astype(BF16), vp)
    return acc * (1.0 / den), m, den


def _band_a_kernel(q_ref, kc_ref, kp_ref, vc_ref, vp_ref, o_ref, lse_ref):
    t = pl.program_id(2)
    lo = _lane_lo((TQ, LANES))
    lo_bf, hi_bf = _half_masks_bf16()
    lane_head = jnp.right_shift(lax.broadcasted_iota(jnp.int32, (TQ, LANES), 1), int(math.log2(LSE_LANES)))
    for sub in range(2):
        rows = slice(sub * TQ, (sub + 1) * TQ)
        mask_c, mask_p = _band_masks(2, t, sub)
        lse_tile = jnp.zeros((TQ, LANES), F32)
        for c in range(D_A // LANES):
            sl = slice(c * LANES, (c + 1) * LANES)
            kc, kp, vc, vp = _sub_tile_kv(sub, sl, kc_ref, kp_ref, vc_ref, vp_ref)
            q2 = q_ref[rows, sl]
            qs = jnp.concatenate([q2 * lo_bf, q2 * hi_bf], axis=0)
            out, m, den = _band_softmax(qs, kc, kp, vc, vp, mask_c, mask_p)
            lse = m + jnp.log2(den)
            o_ref[rows, sl] = jnp.where(lo, out[0:TQ], out[TQ:])
            lse_tile = jnp.where(lane_head == 2 * c, lse[0:TQ], lse_tile)
            lse_tile = jnp.where(lane_head == 2 * c + 1, lse[TQ:], lse_tile)
        lse_ref[rows, :] = lse_tile


def _band_a(q, k, v):
    bsz, dil, sub, _ = q.shape
    cur = pl.BlockSpec((None, None, 2 * TQ, D_A), lambda b, r, t: (b, r, t, 0))
    prev = pl.BlockSpec((None, None, TQ, D_A), lambda b, r, t: (b, r, jnp.maximum(2 * t - 1, 0), 0))
    lse = pl.BlockSpec((None, None, 2 * TQ, LANES), lambda b, r, t: (b, r, t, 0))
    return pl.pallas_call(
        _band_a_kernel,
        grid=(bsz, dil, sub // (2 * TQ)),
        in_specs=[cur, cur, prev, cur, prev],
        out_specs=[cur, lse],
        out_shape=[jax.ShapeDtypeStruct((bsz, dil, sub, D_A), F32),
                   jax.ShapeDtypeStruct((bsz, dil, sub, LANES), F32)],
        compiler_params=_params(3),
        name=f"band_a_d{dil}",
    )(q, k, k, v, v)


def _band_b_body(first_tile, sink_ref, q_ref, kc_ref, kp_ref, vc_ref, vp_ref, o_ref):
    group = N_HEADS_B // N_KV_B
    lo = _lane_lo((TQ, LANES))
    lo_bf, hi_bf = _half_masks_bf16()
    row = lax.broadcasted_iota(jnp.int32, (group * TQ, TQ), 0) & (TQ - 1)
    col = lax.broadcasted_iota(jnp.int32, (group * TQ, TQ), 1)
    in_cur = col <= row
    cur_bf = jnp.where(in_cur, 1.0, 0.0).astype(BF16)
    prev_bf = 1 - cur_bf
    for sub in range(2):
        rows = slice(sub * TQ, (sub + 1) * TQ)
        for g in range(N_KV_B):
            sl = slice(g * LANES, (g + 1) * LANES)
            kc, kp, vc, vp = _sub_tile_kv(sub, sl, kc_ref, kp_ref, vc_ref, vp_ref)
            heads = range(g * group, (g + 1) * group)
            qs = jnp.concatenate(
                [q_ref[rows, (h // 2) * LANES:(h // 2 + 1) * LANES] * (lo_bf if h % 2 == 0 else hi_bf) for h in heads],
                axis=0)
            sink = jnp.concatenate([jnp.full((TQ, 1), sink_ref[h] * LOG2E, F32) for h in heads], axis=0)
            only_cur = first_tile and sub == 0
            s = jnp.where(in_cur, _dot_nt(qs, kc), NEG_INF if only_cur else _dot_nt(qs, kp))
            m = jnp.maximum(jnp.max(s, -1, keepdims=True), sink)
            p = jnp.exp2(s - m)
            den = jnp.sum(p, -1, keepdims=True) + jnp.exp2(sink - m)
            pb = p.astype(BF16)
            acc = _dot(pb, vc) if only_cur else _dot(pb * cur_bf, vc) + _dot(pb * prev_bf, vp)
            out = acc * (1.0 / den)
            for i in range(group // 2):
                c = g * (group // 2) + i
                even, odd = out[2 * i * TQ:(2 * i + 1) * TQ], out[(2 * i + 1) * TQ:(2 * i + 2) * TQ]
                o_ref[rows, c * LANES:(c + 1) * LANES] = jnp.where(lo, even, odd).astype(o_ref.dtype)


def _band_b_kernel(*refs):
    t = pl.program_id(1)
    pl.when(t == 0)(functools.partial(_band_b_body, True, *refs))
    pl.when(t > 0)(functools.partial(_band_b_body, False, *refs))


def _band_b(q, k, v, sinks):
    bsz, seq, _ = q.shape
    kv_lanes = N_KV_B * LANES
    qs = pl.BlockSpec((None, 2 * TQ, D_BQ), lambda b, t: (b, t, 0))
    cur = pl.BlockSpec((None, 2 * TQ, kv_lanes), lambda b, t: (b, t, 0))
    prev = pl.BlockSpec((None, TQ, kv_lanes), lambda b, t: (b, jnp.maximum(2 * t - 1, 0), 0))
    o = pl.pallas_call(
        _band_b_kernel,
        grid=(bsz, seq // (2 * TQ)),
        in_specs=[pl.BlockSpec(memory_space=pltpu.SMEM), qs, cur, prev, cur, prev],
        out_specs=qs,
        out_shape=jax.ShapeDtypeStruct((bsz, seq, D_BQ), BF16),
        compiler_params=_params(2),
        name="band_b",
    )(sinks, q, k, k, v, v)
    return o.reshape(bsz * seq, D_BQ)


def _pattern_count(dist):
    cnt = jnp.zeros(dist.shape, F32)
    for dil in DILATIONS:
        cnt = cnt + ((dist >= 0) & (dist <= 128 * dil) & (dist % dil == 0)).astype(F32)
    return cnt


def _sample_tables():
    i = jnp.arange(DEC_SEQ)
    cnt_c = _pattern_count(WIN_A + i[:, None] - jnp.arange(WIN_A)[None, :])
    j = jnp.arange(LANES)
    cnt_n = jnp.where(j[None, :] < DEC_SEQ, _pattern_count(i[:, None] - j[None, :]), 0.0)
    cnt_n = jnp.tile(cnt_n, (N_HEADS_A, 1))
    jb = jnp.arange(NKB_PAD)[None, :]
    dist_b = WIN_B + i[:, None] - jb
    ok_b = (dist_b >= 0) & (dist_b < WIN_B) & (jb < WIN_B + DEC_SEQ)
    mask_b = jnp.tile(ok_b.astype(F32), (N_HEADS_B, 1))
    return cnt_c, cnt_n, mask_b


def _sample_attend(q, kan, van, kt_ref, vt_ref, cnt_c, cnt_n, qb, kbn, vbn, kbc, vbc, mask_b, sink_col, kb_s, vb_s):
    rows = N_HEADS_A * DEC_SEQ
    q_rep = jnp.concatenate([q] * N_HEADS_A, axis=0)
    row_head = jnp.right_shift(lax.broadcasted_iota(jnp.int32, (rows, D_A), 0), int(math.log2(DEC_SEQ)))
    lane_head = jnp.right_shift(lax.broadcasted_iota(jnp.int32, (rows, D_A), 1), int(math.log2(HEAD_DIM)))
    own = row_head == lane_head
    q_bd = jnp.where(own, q_rep, 0.0).astype(BF16)
    pad = jnp.zeros((LANES - DEC_SEQ, D_A), F32)
    kn = jnp.concatenate([kan, pad], 0).astype(BF16)
    vn = jnp.concatenate([van, pad], 0).astype(BF16)
    s_new = jnp.where(cnt_n > 0.0, _dot_nt(q_bd, kn), NEG_INF)
    outs, p_new = [], []
    for h in range(N_HEADS_A):
        head_rows = slice(h * DEC_SEQ, (h + 1) * DEC_SEQ)
        q_h = q[:, h * HEAD_DIM:(h + 1) * HEAD_DIM].astype(BF16)
        s_c = jnp.where(cnt_c > 0.0, _dot(q_h, kt_ref[h].astype(BF16)), NEG_INF)
        s_n = s_new[head_rows]
        m = jnp.maximum(jnp.max(s_c, -1, keepdims=True), jnp.max(s_n, -1, keepdims=True))
        p_c = jnp.exp(s_c - m) * cnt_c
        p_n = jnp.exp(s_n - m) * cnt_n[head_rows]
        inv = 1.0 / (jnp.sum(p_c, -1, keepdims=True) + jnp.sum(p_n, -1, keepdims=True))
        outs.append(_dot_nt(p_c.astype(BF16), vt_ref[h].astype(BF16)) * inv)
        p_new.append(p_n * inv)
    out_n = jnp.where(own, _dot(jnp.concatenate(p_new, axis=0).astype(BF16), vn), 0.0)
    oa = jnp.concatenate(outs, axis=1)
    for h in range(N_HEADS_A):
        oa = oa + out_n[h * DEC_SEQ:(h + 1) * DEC_SEQ]

    n_pad_b = NKB_PAD - WIN_B - DEC_SEQ
    pad_b = jnp.zeros((n_pad_b, D_BKV), F32)
    kb_s[...] = jnp.concatenate([kbc, kbn, pad_b], 0).astype(BF16)
    vb_s[...] = jnp.concatenate([vbc, vbn, pad_b], 0).astype(BF16)
    lo8 = _lane_lo((DEC_SEQ, LANES))
    group = N_HEADS_B // N_KV_B
    pieces = []
    for h in range(N_HEADS_B):
        chunk = qb[:, (h // 2) * LANES:(h // 2 + 1) * LANES]
        g = h // group
        if h % 2 != g:
            chunk = pltpu.roll(chunk, HEAD_DIM, 1)
        pieces.append(jnp.where(lo8 if g == 0 else jnp.logical_not(lo8), chunk, 0.0))
    qb_bd = jnp.concatenate(pieces, axis=0).astype(BF16)
    sb = jnp.where(mask_b > 0.0, _dot_nt(qb_bd, kb_s[...]), NEG_INF)
    sink = sink_col[:, 0:1]
    mb = jnp.maximum(jnp.max(sb, -1, keepdims=True), sink)
    pb = jnp.exp(sb - mb) * mask_b
    den_b = jnp.sum(pb, -1, keepdims=True) + jnp.exp(sink - mb)
    ob_full = _dot(pb.astype(BF16), vb_s[...]) * (1.0 / den_b)
    ob = []
    for c in range(D_BQ // LANES):
        halves = []
        for half in range(2):
            h = 2 * c + half
            piece = ob_full[h * DEC_SEQ:(h + 1) * DEC_SEQ]
            if half != h // group:
                piece = pltpu.roll(piece, HEAD_DIM, 1)
            halves.append(piece)
        ob.append(jnp.where(lo8, halves[0], halves[1]))
    return jnp.concatenate([oa] + ob, axis=1)


FUSED_ROW_TILE = 512
SEQ_PER_STEP = DEC_BATCH // (BATCH * SEQ // FUSED_ROW_TILE)


def _kv_copies(kt_hbm, vt_hbm, kbuf, vbuf, sems, seq, slot):
    return (pltpu.make_async_copy(kt_hbm.at[seq], kbuf.at[slot], sems.at[0, slot]),
            pltpu.make_async_copy(vt_hbm.at[seq], vbuf.at[slot], sems.at[1, slot]))


def _ffn_attn_kernel(x_ref, wg_ref, wu_ref, wd_ref, g_ref, b_ref,
                     qa_ref, kan_ref, van_ref, kt_hbm, vt_hbm, cnt_c_ref, cnt_n_ref,
                     qb_ref, kbn_ref, vbn_ref, kbc_ref, vbc_ref, maskb_ref, sinkcol_ref,
                     y_ref, o_ref, h_ref, kbuf, vbuf, sems, kb_s, vb_s):
    step = pl.program_id(0)
    n_steps = pl.num_programs(0)
    copies = functools.partial(_kv_copies, kt_hbm, vt_hbm, kbuf, vbuf, sems)

    @pl.when(step == 0)
    def _():
        for cp in copies(0, 0):
            cp.start()

    x = x_ref[...]
    xb = x.astype(BF16)
    n_ff = D_FF // FF_CHUNK
    per_seq = -(-n_ff // SEQ_PER_STEP)
    cnt_c, cnt_n, mask_b, sink_col = cnt_c_ref[...], cnt_n_ref[...], maskb_ref[...], sinkcol_ref[...]
    for s in range(SEQ_PER_STEP):
        slot = s % 2
        seq = step * SEQ_PER_STEP + s
        for cp in copies(seq, slot):
            cp.wait()
        if s + 1 < SEQ_PER_STEP:
            for cp in copies(seq + 1, 1 - slot):
                cp.start()
        else:
            @pl.when(step + 1 < n_steps)
            def _():
                for cp in copies(seq + 1, 1 - slot):
                    cp.start()
        rows = slice(s * DEC_SEQ, (s + 1) * DEC_SEQ)
        o_ref[rows, :] = _sample_attend(
            qa_ref[rows, :], kan_ref[rows, :], van_ref[rows, :], kbuf.at[slot], vbuf.at[slot], cnt_c, cnt_n,
            qb_ref[rows, :], kbn_ref[rows, :], vbn_ref[rows, :], kbc_ref[s], vbc_ref[s], mask_b, sink_col, kb_s, vb_s)
        for c in range(s * per_seq, min((s + 1) * per_seq, n_ff)):
            sl = slice(c * FF_CHUNK, (c + 1) * FF_CHUNK)
            gate = _dot(xb, wg_ref[:, sl])
            up = _dot(xb, wu_ref[:, sl])
            h_ref[:, sl] = (gate * jax.nn.sigmoid(gate) * up).astype(BF16)
    y = DN_ALPHA * x + FFN_RES * _dot(h_ref[...], wd_ref[...])
    y_ref[...] = _layer_norm(y, g_ref[...], b_ref[...])


def _ffn_with_sample_attention(x, wg, wu, wd, g, b, layer, which,
                               qa, kan, van, cache_ak, cache_av, qb, kbn, vbn, cache_bk, cache_bv, sinks):
    n = x.shape[0]
    tm = FUSED_ROW_TILE
    assert n // tm * SEQ_PER_STEP == DEC_BATCH and SEQ_PER_STEP % 2 == 0
    cnt_c, cnt_n, mask_b = _sample_tables()
    sink_col = jnp.broadcast_to(jnp.repeat(sinks.astype(F32), DEC_SEQ)[:, None], (N_HEADS_B * DEC_SEQ, LANES))
    row = pl.BlockSpec((tm, D_MODEL), lambda i: (i, 0))
    pick = lambda r, c: pl.BlockSpec((None, None, r, c), lambda i: (layer, which, 0, 0), pipeline_mode=pl.Buffered(1))
    new = lambda width: pl.BlockSpec((SEQ_PER_STEP * DEC_SEQ, width), lambda i: (i, 0))
    cache_b = pl.BlockSpec((SEQ_PER_STEP, WIN_B, D_BKV), lambda i: (i, 0, 0))
    hbm = pl.BlockSpec(memory_space=pl.ANY)
    rows = N_HEADS_A * DEC_SEQ
    kv_slot = (2, N_HEADS_A, HEAD_DIM, WIN_A)
    return pl.pallas_call(
        _ffn_attn_kernel,
        grid=(n // tm,),
        in_specs=[row, pick(D_MODEL, D_FF), pick(D_MODEL, D_FF), pick(D_FF, D_MODEL),
                  _resident((1, D_MODEL)), _resident((1, D_MODEL)),
                  new(D_A), new(D_A), new(D_A), hbm, hbm, _resident(cnt_c.shape), _resident(cnt_n.shape),
                  new(D_BQ), new(D_BKV), new(D_BKV), cache_b, cache_b,
                  _resident((rows, NKB_PAD)), _resident((rows, LANES))],
        out_specs=[row, new(D_A + D_BQ)],
        out_shape=[jax.ShapeDtypeStruct((n, D_MODEL), F32),
                   jax.ShapeDtypeStruct((DEC_BATCH * DEC_SEQ, D_A + D_BQ), F32)],
        scratch_shapes=[pltpu.VMEM((tm, D_FF), BF16), pltpu.VMEM(kv_slot, F32), pltpu.VMEM(kv_slot, F32),
                        pltpu.SemaphoreType.DMA((2, 2)),
                        pltpu.VMEM((NKB_PAD, D_BKV), BF16), pltpu.VMEM((NKB_PAD, D_BKV), BF16)],
        compiler_params=_params(1, vmem=60 * 1024 * 1024),
        name="ffn_attn_sample",
    )(x, wg, wu, wd, g, b, qa, kan, van, cache_ak, cache_av, cnt_c, cnt_n, qb, kbn, vbn, cache_bk, cache_bv,
      mask_b, sink_col)


def _attn_out_prompt_kernel(o1_ref, l1_ref, o4_ref, l4_ref, o16_ref, l16_ref, ob_ref, y_ref, w_ref, g_ref, b_ref,
                            out_ref, slab_ref, oa_ref):
    tm = y_ref.shape[0]
    n_chunks = D_A // LANES
    slabs = {}
    base = 0
    for name, dil, src, width in (("o4", 4, o4_ref, n_chunks), ("l4", 4, l4_ref, 1),
                                  ("o16", 16, o16_ref, n_chunks), ("l16", 16, l16_ref, 1)):
        slabs[name] = base
        for r in range(dil):
            for c in range(width):
                slab_ref[base + c, pl.ds(r, tm // dil, stride=dil), :] = src[r, :, c * LANES:(c + 1) * LANES]
        base += width
    l1, l4, l16 = l1_ref[...], slab_ref[slabs["l4"]], slab_ref[slabs["l16"]]
    m = jnp.maximum(jnp.maximum(l1, l4), l16)
    e1, e4, e16 = jnp.exp2(l1 - m), jnp.exp2(l4 - m), jnp.exp2(l16 - m)
    inv = 1.0 / (e1 + e4 + e16)
    row = lax.broadcasted_iota(jnp.int32, (LANES, D_A), 0)
    head_of_col = jnp.right_shift(lax.broadcasted_iota(jnp.int32, (LANES, D_A), 1), int(math.log2(HEAD_DIM)))
    spread = jnp.where(row == head_of_col * LSE_LANES, 1.0, 0.0).astype(BF16)

    def per_head_lanes(w):
        hi = w.astype(BF16)
        lo = (w - hi.astype(F32)).astype(BF16)
        return _dot(hi, spread) + _dot(lo, spread)

    w1, w4, w16 = per_head_lanes(e1 * inv), per_head_lanes(e4 * inv), per_head_lanes(e16 * inv)
    for c in range(n_chunks):
        sl = slice(c * LANES, (c + 1) * LANES)
        oa = (w1[:, sl] * o1_ref[:, sl] + w4[:, sl] * slab_ref[slabs["o4"] + c]
              + w16[:, sl] * slab_ref[slabs["o16"] + c])
        oa_ref[:, sl] = oa.astype(BF16)
    mix = _dot(oa_ref[...], w_ref[0:D_A, :]) + _dot(ob_ref[...], w_ref[D_A:, :])
    out_ref[...] = _layer_norm(DN_ALPHA * y_ref[...] + mix, g_ref[...], b_ref[...])


def _attn_out_prompt(pats, ob, y, w, g, b, seq):
    n = y.shape[0]
    tm = ROW_TILE
    tps = seq // tm
    half = pl.BlockSpec((tm, D_A), lambda i: (i, 0))
    full = pl.BlockSpec((tm, D_MODEL), lambda i: (i, 0))
    lse1 = pl.BlockSpec((tm, LANES), lambda i: (i, 0))
    planes = lambda dil, width: pl.BlockSpec((None, dil, tm // dil, width), lambda i: (i // tps, 0, i % tps, 0))
    (o1, l1), (o4, l4), (o16, l16) = pats
    return pl.pallas_call(
        _attn_out_prompt_kernel,
        grid=(n // tm,),
        in_specs=[half, lse1, planes(4, D_A), planes(4, LANES), planes(16, D_A), planes(16, LANES), half, full,
                  _resident((D_MODEL, D_MODEL)), _resident((1, D_MODEL)), _resident((1, D_MODEL))],
        out_specs=full,
        out_shape=jax.ShapeDtypeStruct((n, D_MODEL), F32),
        scratch_shapes=[pltpu.VMEM((2 * (D_A // LANES + 1), tm, LANES), F32), pltpu.VMEM((tm, D_A), BF16)],
        compiler_params=_params(1),
        name="attn_out_prompt",
    )(o1, l1, o4, l4, o16, l16, ob, y, w, g, b)


def _mix_out_kernel(o_ref, y_ref, w_ref, g_ref, b_ref, out_ref):
    mix = _dot(o_ref[...].astype(BF16), w_ref[...])
    out_ref[...] = _layer_norm(DN_ALPHA * y_ref[...] + mix, g_ref[...], b_ref[...])


def _attn_out_sample(o, y, w, g, b):
    n = y.shape[0]
    tm = min(ROW_TILE, n)
    full = pl.BlockSpec((tm, D_MODEL), lambda i: (i, 0))
    return pl.pallas_call(
        _mix_out_kernel,
        grid=(n // tm,),
        in_specs=[full, full, _resident((D_MODEL, D_MODEL)), _resident((1, D_MODEL)), _resident((1, D_MODEL))],
        out_specs=full,
        out_shape=jax.ShapeDtypeStruct((n, D_MODEL), F32),
        compiler_params=_params(1),
        name="attn_out_sample",
    )(o, y, w, g, b)


def _ssm_discretize(lam_re, lam_im, log_dt, b_re, b_im):
    dt = jnp.exp(log_dt.astype(F32))[:, None]
    lr, li = lam_re.astype(F32), lam_im.astype(F32)
    mag = jnp.exp(lr * dt)
    ab_re, ab_im = mag * jnp.cos(li * dt), mag * jnp.sin(li * dt)
    nr, ni = ab_re - 1.0, ab_im
    den = lr * lr + li * li
    fr, fi = (nr * lr + ni * li) / den, (ni * lr - nr * li) / den
    bb_re = fr[..., None] * b_re - fi[..., None] * b_im
    bb_im = fr[..., None] * b_im + fi[..., None] * b_re
    return ab_re, ab_im, bb_re, bb_im


def _ssm_matrices(bb_re, bb_im, c_re, c_im):
    gpc = LANES // SSM_GROUP
    eye = jnp.eye(gpc, dtype=F32)

    def in_blocks(bb):
        a = bb.reshape(N_LCHUNK, gpc, SSM_STATE, SSM_GROUP)
        return jnp.einsum("jgpn,gh->jgnhp", a, eye).reshape(N_LCHUNK, LANES, ST_CHUNK)

    def out_blocks(cc):
        a = cc.reshape(N_LCHUNK, gpc, SSM_GROUP, SSM_STATE)
        return jnp.einsum("jgnp,gh->jgphn", a, eye).reshape(N_LCHUNK, ST_CHUNK, LANES)

    bmat = jnp.concatenate([in_blocks(bb_re), in_blocks(bb_im)], -1)
    cmat = jnp.concatenate([out_blocks(c_re), -out_blocks(c_im)], 1)
    return bmat, cmat


CHUNK = 8


def _dot3(a, b):
    a_hi, b_hi = a.astype(BF16), b.astype(BF16)
    a_lo, b_lo = (a - a_hi.astype(F32)).astype(BF16), (b - b_hi.astype(F32)).astype(BF16)
    return _dot(a_hi, b_hi) + _dot(a_hi, b_lo) + _dot(a_lo, b_hi)


def _chunk_weights_kernel(ar_ref, ai_ref, acr_ref, aci_ref, bmat_ref, cmat_ref,
                          we_ref, ws_ref, wi_ref, a8r_ref, a8i_ref):
    def powers(r, i, n):
        out = [(jnp.ones_like(r), jnp.zeros_like(r))]
        for _ in range(n):
            out.append(_cmul(out[-1][0], out[-1][1], r, i))
        return out

    row_pow = powers(ar_ref[...], ai_ref[...], CHUNK)
    col_pow = powers(acr_ref[...], aci_ref[...], CHUNK)
    b_re, b_im = bmat_ref[:, 0:ST_CHUNK], bmat_ref[:, ST_CHUNK:]
    c_re, c_im = cmat_ref[0:ST_CHUNK, :], -cmat_ref[ST_CHUNK:, :]

    def scaled_b(power):
        pr, pi = row_pow[power]
        return jnp.concatenate([b_re * pr - b_im * pi, b_re * pi + b_im * pr], axis=1)

    taps = []
    for tau in range(CHUNK):
        sb = scaled_b(tau)
        we_ref[(CHUNK - 1 - tau) * LANES:(CHUNK - tau) * LANES, :] = sb.astype(BF16)
        taps.append(_dot3(sb, cmat_ref[...]).astype(BF16))
    for k in range(CHUNK):
        cols = slice(k * LANES, (k + 1) * LANES)
        qr, qi = col_pow[k + 1]
        ws_ref[0:ST_CHUNK, cols] = (c_re * qr - c_im * qi).astype(BF16)
        ws_ref[ST_CHUNK:, cols] = (-(c_re * qi + c_im * qr)).astype(BF16)
    zero = jnp.zeros((LANES, LANES), BF16)
    for k_in in range(CHUNK):
        for k_out in range(CHUNK):
            wi_ref[k_in * LANES:(k_in + 1) * LANES, k_out * LANES:(k_out + 1) * LANES] = (
                taps[k_out - k_in] if k_out >= k_in else zero)
    a8r_ref[...], a8i_ref[...] = row_pow[CHUNK]


def _chunk_weights(a_re, a_im, bmat, cmat):
    cols = lambda a: jnp.broadcast_to(a.reshape(N_LCHUNK, ST_CHUNK, 1), (N_LCHUNK, ST_CHUNK, LANES))
    per_j = lambda r, c: pl.BlockSpec((None, r, c), lambda j: (j, 0, 0))
    wide = 2 * ST_CHUNK
    return pl.pallas_call(
        _chunk_weights_kernel,
        grid=(N_LCHUNK,),
        in_specs=[per_j(1, ST_CHUNK), per_j(1, ST_CHUNK), per_j(ST_CHUNK, LANES), per_j(ST_CHUNK, LANES),
                  per_j(LANES, wide), per_j(wide, LANES)],
        out_specs=[per_j(CHUNK * LANES, wide), per_j(wide, CHUNK * LANES), per_j(CHUNK * LANES, CHUNK * LANES),
                   per_j(1, ST_CHUNK), per_j(1, ST_CHUNK)],
        out_shape=[jax.ShapeDtypeStruct((N_LCHUNK, CHUNK * LANES, wide), BF16),
                   jax.ShapeDtypeStruct((N_LCHUNK, wide, CHUNK * LANES), BF16),
                   jax.ShapeDtypeStruct((N_LCHUNK, CHUNK * LANES, CHUNK * LANES), BF16),
                   jax.ShapeDtypeStruct((N_LCHUNK, 1, ST_CHUNK), F32),
                   jax.ShapeDtypeStruct((N_LCHUNK, 1, ST_CHUNK), F32)],
        compiler_params=_params(1),
        name="ssm_chunk_weights",
    )(a_re, a_im, cols(a_re), cols(a_im), bmat, cmat)


SEG_TILE = ROW_TILE // N_SEG


def _ssm_in_prompt_kernel(x_ref, w_ref, o_ref):
    x = x_ref[...].reshape(N_SEG * SEG_TILE, D_MODEL)
    u = _dot(x.astype(BF16), w_ref[...])
    for s in range(N_SEG):
        for c in range(N_LCHUNK):
            o_ref[c, pl.ds(s, SEG_TILE, stride=N_SEG), :] = u[s * SEG_TILE:(s + 1) * SEG_TILE, c * LANES:(c + 1) * LANES]


def _ssm_in_prompt(y, w):
    return pl.pallas_call(
        _ssm_in_prompt_kernel,
        grid=(BATCH, SEG_LEN // SEG_TILE),
        in_specs=[pl.BlockSpec((None, N_SEG, SEG_TILE, D_MODEL), lambda b, t: (b, 0, t, 0)),
                  _resident((D_MODEL, D_MODEL))],
        out_specs=pl.BlockSpec((None, N_LCHUNK, ROW_TILE, LANES), lambda b, t: (b, 0, t, 0)),
        out_shape=jax.ShapeDtypeStruct((BATCH, N_LCHUNK, SEQ, LANES), F32),
        compiler_params=_params(2),
        name="ssm_in_prompt",
    )(y.reshape(BATCH, N_SEG, SEG_LEN, D_MODEL), w)


def _ssm_in_sample_kernel(x_ref, w_ref, o_ref, slab_ref):
    u = _dot(x_ref[...].astype(BF16), w_ref[...])
    for c in range(N_LCHUNK):
        slab_ref[c] = u[:, c * LANES:(c + 1) * LANES]
    for l in range(DEC_SEQ):
        for c in range(N_LCHUNK):
            o_ref[c, l * DEC_BATCH:(l + 1) * DEC_BATCH, :] = slab_ref[c, pl.ds(l, DEC_BATCH, stride=DEC_SEQ), :]


def _ssm_in_sample(y, w):
    n = DEC_BATCH * DEC_SEQ
    return pl.pallas_call(
        _ssm_in_sample_kernel,
        grid=(1,),
        in_specs=[_resident((n, D_MODEL)), _resident((D_MODEL, D_MODEL))],
        out_specs=pl.BlockSpec((N_LCHUNK, n, LANES), lambda i: (0, 0, 0)),
        out_shape=jax.ShapeDtypeStruct((N_LCHUNK, n, LANES), F32),
        scratch_shapes=[pltpu.VMEM((N_LCHUNK, n, LANES), F32)],
        compiler_params=_params(1),
        name="ssm_in_sample",
    )(y, w)


def _cmul(ar, ai, br, bi):
    return ar * br - ai * bi, ar * bi + ai * br


def _scan_prompt_kernel(u_ref, we_ref, ws_ref, wi_ref, a8r_ref, a8i_ref, d_ref, h0r_ref, h0i_ref,
                        y_ref, hnr_ref, hni_ref, e_s, hs_s):
    n_chunks = u_ref.shape[0]
    rows = n_chunks * N_SEG
    u_flat = jnp.concatenate([u_ref[:, k].reshape(rows, LANES) for k in range(CHUNK)], axis=1)
    ub = u_flat.astype(BF16)
    e_s[...] = _dot(ub, we_ref[...])
    a8r1, a8i1 = a8r_ref[...], a8i_ref[...]
    a8r = jnp.broadcast_to(a8r1, (N_SEG, ST_CHUNK))
    a8i = jnp.broadcast_to(a8i1, (N_SEG, ST_CHUNK))

    def advance(row, hr, hi):
        er = e_s[pl.ds(row, N_SEG), 0:ST_CHUNK]
        ei = e_s[pl.ds(row, N_SEG), ST_CHUNK:2 * ST_CHUNK]
        return a8r * hr - a8i * hi + er, a8r * hi + a8i * hr + ei

    def pass1(c, carry):
        return advance(pl.multiple_of(c * N_SEG, N_SEG), *carry)

    zero = jnp.zeros((N_SEG, ST_CHUNK), F32)
    er, ei = lax.fori_loop(0, n_chunks, pass1, (zero, zero), unroll=8)

    pr, pi = a8r1, a8i1
    for _ in range(int(math.log2(n_chunks))):
        pr, pi = _cmul(pr, pi, pr, pi)
    hr, hi = h0r_ref[...], h0i_ref[...]
    starts_r, starts_i = [], []
    for s in range(N_SEG):
        starts_r.append(hr)
        starts_i.append(hi)
        gr, gi = _cmul(pr, pi, hr, hi)
        hr, hi = gr + er[s:s + 1], gi + ei[s:s + 1]
    hnr_ref[...] = hr
    hni_ref[...] = hi
    init = (jnp.concatenate(starts_r, 0), jnp.concatenate(starts_i, 0))

    def pass2(c, carry):
        row = pl.multiple_of(c * N_SEG, N_SEG)
        hs_s[pl.ds(row, N_SEG), 0:ST_CHUNK] = carry[0]
        hs_s[pl.ds(row, N_SEG), ST_CHUNK:2 * ST_CHUNK] = carry[1]
        return advance(row, *carry)

    lax.fori_loop(0, n_chunks, pass2, init, unroll=8)
    y = _dot(hs_s[...].astype(BF16), ws_ref[...]) + _dot(ub, wi_ref[...]) + d_ref[...] * u_flat
    for k in range(CHUNK):
        y_ref[:, k] = y[:, k * LANES:(k + 1) * LANES].reshape(n_chunks, N_SEG, LANES)


def _scan_prompt(u, weights, d_skip, h0r, h0i):
    bsz, _, seq, _ = u.shape
    n_chunks = seq // (CHUNK * N_SEG)
    rows = n_chunks * N_SEG
    wide = 2 * ST_CHUNK
    split = lambda a: a.reshape(bsz, N_LCHUNK, n_chunks, CHUNK, N_SEG, LANES)
    chunk = pl.BlockSpec((None, None, n_chunks, CHUNK, N_SEG, LANES), lambda j, b: (b, j, 0, 0, 0, 0))
    per_j = lambda r, c: pl.BlockSpec((None, r, c), lambda j, b: (j, 0, 0))
    state = pl.BlockSpec((None, 1, ST_CHUNK), lambda j, b: (b, 0, j))
    y, hr, hi = pl.pallas_call(
        _scan_prompt_kernel,
        grid=(N_LCHUNK, bsz),
        in_specs=[chunk, per_j(CHUNK * LANES, wide), per_j(wide, CHUNK * LANES), per_j(CHUNK * LANES, CHUNK * LANES),
                  per_j(1, ST_CHUNK), per_j(1, ST_CHUNK), per_j(1, CHUNK * LANES), state, state],
        out_specs=[chunk, state, state],
        out_shape=[jax.ShapeDtypeStruct((bsz, N_LCHUNK, n_chunks, CHUNK, N_SEG, LANES), F32),
                   jax.ShapeDtypeStruct((bsz, 1, N_STATE), F32), jax.ShapeDtypeStruct((bsz, 1, N_STATE), F32)],
        scratch_shapes=[pltpu.VMEM((rows, wide), F32), pltpu.VMEM((rows, wide), F32)],
        compiler_params=_params(2),
        name="ssm_scan_prompt",
    )(split(u), *weights, d_skip, h0r, h0i)
    return y.reshape(u.shape), hr, hi


def _scan_sample_kernel(u_ref, bmat_ref, cmat_ref, are_ref, aim_ref, d_ref, h0r_ref, h0i_ref,
                        y_ref, hnr_ref, hni_ref):
    a_re, a_im = are_ref[...], aim_ref[...]
    hr, hi = h0r_ref[...].T, h0i_ref[...].T
    for l in range(DEC_SEQ):
        rows = slice(l * DEC_BATCH, (l + 1) * DEC_BATCH)
        u = u_ref[rows, :]
        bu = _dot(u.astype(BF16), bmat_ref[...])
        gr, gi = _cmul(a_re, a_im, hr, hi)
        hr, hi = gr + bu[:, :ST_CHUNK], gi + bu[:, ST_CHUNK:]
        h = jnp.concatenate([hr, hi], axis=1).astype(BF16)
        y_ref[rows, :] = _dot(h, cmat_ref[...]) + d_ref[...] * u
    hnr_ref[...] = hr.T
    hni_ref[...] = hi.T


def _scan_sample(u, bmat, cmat, a_re, a_im, d_skip, h0r, h0i):
    n = DEC_SEQ * DEC_BATCH
    chunk = pl.BlockSpec((None, n, LANES), lambda j: (j, 0, 0))
    per_j = lambda r, c: pl.BlockSpec((None, r, c), lambda j: (j, 0, 0))
    state = pl.BlockSpec((ST_CHUNK, DEC_BATCH), lambda j: (j, 0))
    return pl.pallas_call(
        _scan_sample_kernel,
        grid=(N_LCHUNK,),
        in_specs=[chunk, per_j(LANES, 2 * ST_CHUNK), per_j(2 * ST_CHUNK, LANES), per_j(1, ST_CHUNK),
                  per_j(1, ST_CHUNK), per_j(1, LANES), state, state],
        out_specs=[chunk, state, state],
        out_shape=[jax.ShapeDtypeStruct((N_LCHUNK, n, LANES), F32),
                   jax.ShapeDtypeStruct((N_STATE, DEC_BATCH), F32), jax.ShapeDtypeStruct((N_STATE, DEC_BATCH), F32)],
        compiler_params=_params(1),
        name="ssm_scan_sample",
    )(u, bmat, cmat, a_re, a_im, d_skip, h0r, h0i)


def _ssm_out_kernel(sample, s_ref, y_ref, wglu_ref, bglu_ref, wout_ref, g_ref, b_ref, out_ref, z_ref):
    n_rows = z_ref.shape[1]
    if sample:
        parts = [(pl.ds(l, DEC_BATCH, stride=DEC_SEQ), slice(l * DEC_BATCH, (l + 1) * DEC_BATCH))
                 for l in range(DEC_SEQ)]
    else:
        parts = [(slice(s * SEG_TILE, (s + 1) * SEG_TILE), pl.ds(s, SEG_TILE, stride=N_SEG)) for s in range(N_SEG)]
    for c in range(N_LCHUNK):
        for tok_rows, slab_rows in parts:
            z_ref[c, tok_rows, :] = s_ref[c, slab_rows, :]
    z = jax.nn.gelu(jnp.concatenate([z_ref[c] for c in range(N_LCHUNK)], axis=1))
    gate = jax.nn.sigmoid(_dot(z.astype(BF16), wglu_ref[...]) + bglu_ref[...])
    mix = _dot((z * gate).astype(BF16), wout_ref[...])
    res = y_ref[...].reshape(n_rows, D_MODEL)
    out = _layer_norm(DN_ALPHA * res + mix, g_ref[...], b_ref[...])
    out_ref[...] = out.reshape(out_ref.shape)


def _ssm_out(s, y, s_spec, y_spec, grid, rows, sample, w_glu, b_glu, w_out, g, b, name):
    return pl.pallas_call(
        functools.partial(_ssm_out_kernel, sample),
        grid=grid,
        in_specs=[s_spec, y_spec, _resident((D_MODEL, D_MODEL)), _resident((1, D_MODEL)),
                  _resident((D_MODEL, D_MODEL)), _resident((1, D_MODEL)), _resident((1, D_MODEL))],
        out_specs=y_spec,
        out_shape=jax.ShapeDtypeStruct(y.shape, F32),
        scratch_shapes=[pltpu.VMEM((N_LCHUNK, rows, LANES), F32)],
        compiler_params=_params(len(grid)),
        name=name,
    )(s, y, w_glu, b_glu, w_out, g, b)


def _attn_prompt(yp, w_in, sinks, w_out, g, b):
    cos_p, sin_p = _rope_tables(jnp.arange(SEQ))
    tiles_per_seq = SEQ // ROW_TILE
    qa, ka, va, qb, kb, vb, *extra = _attn_proj(
        yp, w_in, cos_p, sin_p, lambda i: i % tiles_per_seq, BF16, prompt_seq=(BATCH, SEQ))
    dilated, tails = extra[:6], extra[6:]
    seq3 = lambda a: a.reshape(BATCH, SEQ, a.shape[-1])
    plane1 = lambda a: a.reshape(BATCH, 1, SEQ, D_A)
    o1, l1 = _band_a(plane1(qa), plane1(ka), plane1(va))
    pats = [(o1.reshape(BATCH * SEQ, D_A), l1.reshape(BATCH * SEQ, LANES))]
    for i in range(len(DILATIONS) - 1):
        pats.append(_band_a(*dilated[3 * i:3 * i + 3]))
    ob = _band_b(seq3(qb), seq3(kb), seq3(vb), sinks)
    yp = _attn_out_prompt(pats, ob, yp, w_out, g, b, SEQ)
    heads = lambda a, nh: jnp.transpose(a.reshape(BATCH, nh, HEAD_DIM, a.shape[-1]), (0, 3, 1, 2))[None]
    prompt_cache = (heads(tails[0], N_HEADS_A), heads(tails[1], N_HEADS_A),
                    heads(tails[2], N_KV_B), heads(tails[3], N_KV_B))
    return yp, prompt_cache


def _attn_sample_path(ys, cache_ak, cache_av, cache_bk, cache_bv, w_in, sinks, w_out, g, b, yp, ffn_args):
    cos_s, sin_s = _rope_tables(PAST_LEN + jnp.arange(DEC_SEQ))
    reps = DEC_BATCH * DEC_SEQ // DEC_SEQ
    cos_s, sin_s = jnp.tile(cos_s, (reps, 1)), jnp.tile(sin_s, (reps, 1))
    qa, ka, va, qb, kb, vb = _attn_proj(ys, w_in, cos_s, sin_s, lambda i: i, F32)
    stored = lambda a: jnp.transpose(a, (0, 2, 3, 1))
    yp, o = _ffn_with_sample_attention(
        yp, *ffn_args, qa, ka, va, stored(cache_ak), stored(cache_av),
        qb, kb, vb, cache_bk.reshape(DEC_BATCH, WIN_B, D_BKV), cache_bv.reshape(DEC_BATCH, WIN_B, D_BKV), sinks)
    ys = _attn_out_sample(o, ys, w_out, g, b)
    new = lambda a, nh: a.reshape(1, DEC_BATCH, DEC_SEQ, nh, HEAD_DIM)
    sample_cache = (new(ka, N_HEADS_A), new(va, N_HEADS_A), new(kb, N_KV_B), new(vb, N_KV_B))
    return yp, ys, sample_cache


def _attn_layer(yp, ys, cache_ak, cache_av, cache_bk, cache_bv, w_in, sinks, w_out, g, b, ffn_args):
    w_in = w_in.astype(BF16)
    w_out = w_out.astype(BF16)
    yp, prompt_cache = _attn_prompt(yp, w_in, sinks, w_out, g, b)
    yp, ys, sample_cache = _attn_sample_path(ys, cache_ak, cache_av, cache_bk, cache_bv, w_in, sinks, w_out, g, b,
                                             yp, ffn_args)
    return yp, ys, prompt_cache, sample_cache


def _ssm_layer(yp, ys, state_re, state_im, w_in, lam_re, lam_im, log_dt, b_re, b_im, c_re, c_im, d_skip,
               w_glu, b_glu, w_out, g, b):
    w_in, w_glu, w_out = w_in.astype(BF16), w_glu.astype(BF16), w_out.astype(BF16)
    b_glu = b_glu.reshape(1, D_MODEL)
    mats = _ssm_prepare(lam_re, lam_im, log_dt, b_re, b_im, c_re, c_im, d_skip)
    yp, prompt_state = _ssm_prompt(yp, w_in, mats, w_glu, b_glu, w_out, g, b)
    ys, sample_state = _ssm_sample(ys, state_re, state_im, w_in, mats, w_glu, b_glu, w_out, g, b)
    return yp, ys, prompt_state, sample_state


def _ssm_prepare(lam_re, lam_im, log_dt, b_re, b_im, c_re, c_im, d_skip):
    ab_re, ab_im, bb_re, bb_im = _ssm_discretize(lam_re, lam_im, log_dt, b_re, b_im)
    bmat, cmat = _ssm_matrices(bb_re, bb_im, c_re, c_im)
    a_re = ab_re.reshape(N_LCHUNK, 1, ST_CHUNK)
    a_im = ab_im.reshape(N_LCHUNK, 1, ST_CHUNK)
    d3 = d_skip.astype(F32).reshape(N_LCHUNK, 1, LANES)
    prompt = (_chunk_weights(a_re, a_im, bmat, cmat), jnp.tile(d3, (1, 1, CHUNK)))
    sample = (bmat.astype(BF16), cmat.astype(BF16), a_re, a_im, d3)
    return prompt, sample


def _ssm_prompt(yp, w_in, mats, w_glu, b_glu, w_out, g, b):
    weights, d_tiled = mats[0]
    up = _ssm_in_prompt(yp, w_in)
    zero = jnp.zeros((BATCH, 1, N_STATE), F32)
    sp, pr, pi = _scan_prompt(up, weights, d_tiled, zero, zero)
    yp = _ssm_out(sp, yp.reshape(BATCH, N_SEG, SEG_LEN, D_MODEL),
                  pl.BlockSpec((None, N_LCHUNK, ROW_TILE, LANES), lambda bb, t: (bb, 0, t, 0)),
                  pl.BlockSpec((None, N_SEG, SEG_TILE, D_MODEL), lambda bb, t: (bb, 0, t, 0)),
                  (BATCH, SEG_LEN // SEG_TILE), ROW_TILE, False, w_glu, b_glu, w_out, g, b,
                  "ssm_out_prompt").reshape(BATCH * SEQ, D_MODEL)
    prompt_state = (pr.reshape(1, BATCH, N_SSM_GROUPS, SSM_STATE), pi.reshape(1, BATCH, N_SSM_GROUPS, SSM_STATE))
    return yp, prompt_state


def _ssm_sample(ys, state_re, state_im, w_in, mats, w_glu, b_glu, w_out, g, b):
    bmat, cmat, a_re, a_im, d3 = mats[1]
    us = _ssm_in_sample(ys, w_in)
    stored = lambda a: jnp.transpose(a, (1, 2, 0)).reshape(N_STATE, DEC_BATCH)
    logical = lambda a: jnp.transpose(a.reshape(N_SSM_GROUPS, SSM_STATE, DEC_BATCH), (2, 0, 1))[None]
    ss, sr, si = _scan_sample(us, bmat, cmat, a_re, a_im, d3, stored(state_re), stored(state_im))
    n = DEC_BATCH * DEC_SEQ
    ys = _ssm_out(ss, ys,
                  pl.BlockSpec((N_LCHUNK, n, LANES), lambda i: (0, 0, 0)),
                  pl.BlockSpec((n, D_MODEL), lambda i: (0, 0)),
                  (1,), n, True, w_glu, b_glu, w_out, g, b, "ssm_out_sample")
    return ys, (logical(sr), logical(si))


def kernel(x_prompt, x_sample, cache_a_k, cache_a_v, cache_b_k, cache_b_v, state_c_re, state_c_im, ln_g, ln_b, ffn_w_gate, ffn_w_up, ffn_w_down, attn_w_in, attn_sinks, attn_w_out, ssm_w_in, ssm_lambda_re, ssm_lambda_im, ssm_log_dt, ssm_b_re, ssm_b_im, ssm_c_re, ssm_c_im, ssm_d, ssm_w_glu, ssm_b_glu, ssm_w_out):
    yp = x_prompt.reshape(BATCH * SEQ, D_MODEL)
    ys = x_sample.reshape(DEC_BATCH * DEC_SEQ, D_MODEL)
    ln = lambda l, k: (ln_g[l, k].reshape(1, D_MODEL), ln_b[l, k].reshape(1, D_MODEL))

    wg, wu, wd = ffn_w_gate.astype(BF16), ffn_w_up.astype(BF16), ffn_w_down.astype(BF16)

    def ffn_pair(yp, ys, l, k, ln_idx):
        g, b = ln(l, ln_idx)
        return _ffn_pair(yp, ys, wg, wu, wd, g, b, l, k)

    yp, ys = ffn_pair(yp, ys, 0, 0, 0)
    yp, ys, p_cache, s_cache = _attn_layer(yp, ys, cache_a_k[0], cache_a_v[0], cache_b_k[0], cache_b_v[0],
                                           attn_w_in[0], attn_sinks[0], attn_w_out[0], *ln(0, 1),
                                           ffn_args=(wg, wu, wd, *ln(0, 2), 0, 1))
    ys = _ffn(ys, wg, wu, wd, *ln(0, 2), 0, 1)
    yp, ys = ffn_pair(yp, ys, 1, 0, 0)
    yp, ys, p_state, s_state = _ssm_layer(yp, ys, state_c_re[0], state_c_im[0], ssm_w_in[0], ssm_lambda_re[0],
                                          ssm_lambda_im[0], ssm_log_dt[0], ssm_b_re[0], ssm_b_im[0], ssm_c_re[0],
                                          ssm_c_im[0], ssm_d[0], ssm_w_glu[0], ssm_b_glu[0], ssm_w_out[0], *ln(1, 1))
    yp, ys = ffn_pair(yp, ys, 1, 1, 2)
    return (yp.reshape(BATCH, SEQ, D_MODEL), ys.reshape(DEC_BATCH, DEC_SEQ, D_MODEL),
            *p_cache, *p_state, *s_cache, *s_state)
```

```python
import functools
import math

import jax
import jax.numpy as jnp
from jax import lax
from jax.experimental import pallas as pl
from jax.experimental.pallas import tpu as pltpu

F32 = jnp.float32
BF16 = jnp.bfloat16

D_MODEL = 1024
BATCH = 4
SEQ = 4096
DEPTH = 2
DEC_BATCH = 128
DEC_SEQ = 8
PAST_LEN = 16384
HEAD_DIM = 64
N_HEADS_A = 8
DILATIONS = (1, 4, 16)
WIN_A = 2048
N_HEADS_B = 8
N_KV_B = 2
WIN_B = 128
ROPE_THETA = 10000.0
D_A = N_HEADS_A * HEAD_DIM
D_BQ = N_HEADS_B * HEAD_DIM
D_BKV = N_KV_B * HEAD_DIM
D_IN_ATTN = 3 * D_A + D_BQ + 2 * D_BKV
SSM_GROUP = 16
N_SSM_GROUPS = D_MODEL // SSM_GROUP
SSM_STATE = 64
N_STATE = N_SSM_GROUPS * SSM_STATE
D_FF = 2816
DN_ALPHA = (2 * DEPTH) ** 0.25
FFN_RES = 0.5
LN_EPS = 1e-5
ATTN_SCALE = HEAD_DIM ** -0.5
LOG2E = math.log2(math.e)
LSE_LANES = 16

LANES = 128
SUBLANES = 8
MXU_N = 256
VMEM_LIMIT = 56 * 1024 * 1024

ROW_TILE = 512
FFN_ROW_TILE = 1024
FF_CHUNK = MXU_N
TQ = 128
N_SEG = SUBLANES
SEG_LEN = SEQ // N_SEG
N_LCHUNK = D_MODEL // LANES
ST_CHUNK = N_STATE // N_LCHUNK
NK_PAD = WIN_A + LANES
NKB_PAD = 2 * WIN_B

NEG_INF = float("-inf")


def _params(n_axes, vmem=VMEM_LIMIT):
    return pltpu.CompilerParams(dimension_semantics=("arbitrary",) * n_axes, vmem_limit_bytes=vmem)


def _resident(shape):
    return pl.BlockSpec(shape, lambda *_: (0,) * len(shape), pipeline_mode=pl.Buffered(1))


def _layer_norm(x, g, b):
    mu = jnp.mean(x, -1, keepdims=True)
    xc = x - mu
    var = jnp.mean(xc * xc, -1, keepdims=True)
    return xc * lax.rsqrt(var + LN_EPS) * g + b


def _dot(a, b):
    return jnp.dot(a, b, preferred_element_type=F32)


def _dot_nt(a, b):
    return lax.dot_general(a, b, (((1,), (1,)), ((), ())), preferred_element_type=F32)


def _ffn_kernel(x_ref, wg_ref, wu_ref, wd_ref, g_ref, b_ref, o_ref, h_ref):
    x = x_ref[...]
    xb = x.astype(BF16)
    for c in range(D_FF // FF_CHUNK):
        sl = slice(c * FF_CHUNK, (c + 1) * FF_CHUNK)
        gate = _dot(xb, wg_ref[:, sl])
        up = _dot(xb, wu_ref[:, sl])
        h_ref[:, sl] = (gate * jax.nn.sigmoid(gate) * up).astype(BF16)
    y = DN_ALPHA * x + FFN_RES * _dot(h_ref[...], wd_ref[...])
    o_ref[...] = _layer_norm(y, g_ref[...], b_ref[...])


def _ffn(x, wg, wu, wd, g, b, layer=0, which=0):
    n = x.shape[0]
    tm = min(FFN_ROW_TILE, n)
    row = pl.BlockSpec((tm, D_MODEL), lambda i: (i, 0))
    if wg.ndim == 4:
        pick = lambda r, c: pl.BlockSpec((None, None, r, c), lambda i: (layer, which, 0, 0),
                                         pipeline_mode=pl.Buffered(1))
    else:
        pick = lambda r, c: _resident((r, c))
    return pl.pallas_call(
        _ffn_kernel,
        grid=(n // tm,),
        in_specs=[row, pick(D_MODEL, D_FF), pick(D_MODEL, D_FF), pick(D_FF, D_MODEL),
                  _resident((1, D_MODEL)), _resident((1, D_MODEL))],
        out_specs=row,
        out_shape=jax.ShapeDtypeStruct((n, D_MODEL), F32),
        scratch_shapes=[pltpu.VMEM((tm, D_FF), BF16)],
        compiler_params=_params(1),
        name="ffn",
    )(x, wg, wu, wd, g, b)


def _ffn_pair_kernel(n_first, xp_ref, xs_ref, wg_ref, wu_ref, wd_ref, g_ref, b_ref, op_ref, os_ref, h_ref):
    step = pl.program_id(0)
    weights = (wg_ref, wu_ref, wd_ref, g_ref, b_ref)
    pl.when(step < n_first)(functools.partial(_ffn_kernel, xp_ref, *weights, op_ref, h_ref))
    pl.when(step >= n_first)(functools.partial(_ffn_kernel, xs_ref, *weights, os_ref, h_ref))


def _ffn_pair(xp, xs, wg, wu, wd, g, b, layer, which):
    tm = ROW_TILE
    n_first, n_second = xp.shape[0] // tm, xs.shape[0] // tm
    first = pl.BlockSpec((tm, D_MODEL), lambda i: (jnp.minimum(i, n_first - 1), 0))
    second = pl.BlockSpec((tm, D_MODEL), lambda i: (jnp.maximum(i - n_first, 0), 0))
    pick = lambda r, c: pl.BlockSpec((None, None, r, c), lambda i: (layer, which, 0, 0), pipeline_mode=pl.Buffered(1))
    return pl.pallas_call(
        functools.partial(_ffn_pair_kernel, n_first),
        grid=(n_first + n_second,),
        in_specs=[first, second, pick(D_MODEL, D_FF), pick(D_MODEL, D_FF), pick(D_FF, D_MODEL),
                  _resident((1, D_MODEL)), _resident((1, D_MODEL))],
        out_specs=[first, second],
        out_shape=[jax.ShapeDtypeStruct(xp.shape, F32), jax.ShapeDtypeStruct(xs.shape, F32)],
        scratch_shapes=[pltpu.VMEM((tm, D_FF), BF16)],
        compiler_params=_params(1),
        name="ffn_pair",
    )(xp, xs, wg, wu, wd, g, b)


def _rope_tables(pos):
    half = HEAD_DIM // 2
    inv_freq = ROPE_THETA ** (-jnp.arange(half, dtype=F32) / half)
    ang = pos.astype(F32)[:, None] * inv_freq[None, :]
    cos, sin = jnp.cos(ang), jnp.sin(ang)
    cos_t = jnp.concatenate([cos, cos, cos, cos], -1)
    sin_t = jnp.concatenate([-sin, sin, -sin, sin], -1)
    return cos_t, sin_t


def _attn_proj_kernel(tiles_per_seq, x_ref, w_ref, cos_ref, sin_ref, qa_ref, ka_ref, va_ref, qb_ref, kb_ref, vb_ref,
                      *extra):
    xb = x_ref[...].astype(BF16)
    slab_ref = extra[-1] if extra else None
    dilated = extra[:6]
    n_chunks = D_A // LANES
    tm = x_ref.shape[0]

    def keep(tensor, c, val):
        if slab_ref is not None:
            slab_ref[tensor * n_chunks + c] = val
    cos = cos_ref[...]
    sin = sin_ref[...]
    lane = lax.broadcasted_iota(jnp.int32, cos.shape, 1)
    first_half = (lane & (HEAD_DIM // 2)) == 0

    def rope(z):
        rot = jnp.where(first_half, pltpu.roll(z, LANES - HEAD_DIM // 2, 1), pltpu.roll(z, HEAD_DIM // 2, 1))
        return z * cos + rot * sin

    def project(col0, ncols):
        return _dot(xb, w_ref[:, col0:col0 + ncols])

    def rope_chunks(z):
        return [rope(z[:, c * LANES:(c + 1) * LANES]) for c in range(z.shape[1] // LANES)]

    q_scale = ATTN_SCALE * LOG2E if extra else ATTN_SCALE
    col = 0
    for c, r in enumerate(rope_chunks(project(col, D_A))):
        r = r * q_scale
        qa_ref[:, c * LANES:(c + 1) * LANES] = r.astype(qa_ref.dtype)
        keep(0, c, r)
    col += D_A
    for c, r in enumerate(rope_chunks(project(col, D_A))):
        ka_ref[:, c * LANES:(c + 1) * LANES] = r.astype(ka_ref.dtype)
        keep(1, c, r)
    col += D_A
    z = project(col, D_A)
    va_ref[...] = z.astype(va_ref.dtype)
    for c in range(n_chunks):
        keep(2, c, z[:, c * LANES:(c + 1) * LANES])
    col += D_A
    if extra:
        kat_ref, vat_ref, kbt_ref, vbt_ref = extra[6:10]
        tile_in_seq = pl.program_id(0) % tiles_per_seq

        @pl.when(tile_in_seq >= tiles_per_seq - WIN_A // tm)
        def _():
            for c in range(n_chunks):
                kat_ref[c * LANES:(c + 1) * LANES, :] = slab_ref[n_chunks + c].T
                vat_ref[c * LANES:(c + 1) * LANES, :] = slab_ref[2 * n_chunks + c].T

        slab4_ref = extra[-2]
        d1, d2 = DILATIONS[1], DILATIONS[2] // DILATIONS[1]
        plane = tm // d1
        for tensor in range(3):
            out1_ref, out2_ref = dilated[tensor], dilated[3 + tensor]
            for c in range(n_chunks):
                idx = tensor * n_chunks + c
                lanes = slice(c * LANES, (c + 1) * LANES)
                for r in range(d1):
                    rows = slab_ref[idx, pl.ds(r, plane, stride=d1), :]
                    slab4_ref[idx, r * plane:(r + 1) * plane, :] = rows
                    out1_ref[r, :, lanes] = rows.astype(out1_ref.dtype)
                for r in range(d1):
                    for m in range(d2):
                        rows = slab4_ref[idx, pl.ds(r * plane + m, plane // d2, stride=d2), :]
                        out2_ref[r + d1 * m, :, lanes] = rows.astype(out2_ref.dtype)
    for c, r in enumerate(rope_chunks(project(col, D_BQ))):
        qb_ref[:, c * LANES:(c + 1) * LANES] = (r * q_scale).astype(qb_ref.dtype)
    col += D_BQ
    z = project(col, 2 * D_BKV)
    r = rope(z[:, :D_BKV])
    if extra:
        lo = _lane_lo(r.shape)
        for ref, val in ((kb_ref, r), (vb_ref, z[:, D_BKV:])):
            swapped = pltpu.roll(val, HEAD_DIM, 1)
            ref[:, 0:LANES] = jnp.where(lo, val, swapped).astype(ref.dtype)
            ref[:, LANES:] = jnp.where(lo, swapped, val).astype(ref.dtype)
    else:
        kb_ref[...] = r.astype(kb_ref.dtype)
        vb_ref[...] = z[:, D_BKV:].astype(vb_ref.dtype)
    if extra:
        @pl.when(tile_in_seq == tiles_per_seq - 1)
        def _():
            kbt_ref[...] = r[tm - WIN_B:, :].T
            vbt_ref[...] = z[tm - WIN_B:, D_BKV:].T


def _attn_proj(x, w, cos_t, sin_t, table_block, act_dtype, prompt_seq=None):
    n = x.shape[0]
    tm = min(ROW_TILE, n)

    def row(width):
        return pl.BlockSpec((tm, width), lambda i: (i, 0))

    tab = pl.BlockSpec((tm, LANES), lambda i: (table_block(i), 0))
    kv_b = D_BKV if prompt_seq is None else N_KV_B * LANES
    widths = (D_A, D_A, D_A, D_BQ, kv_b, kv_b)
    out_shape = [jax.ShapeDtypeStruct((n, wd), act_dtype) for wd in widths]
    out_specs = [row(wd) for wd in widths]
    scratch = []
    tps = None
    if prompt_seq is not None:
        bsz, seq = prompt_seq
        tps = seq // tm
        for dil in DILATIONS[1:]:
            out_shape += [jax.ShapeDtypeStruct((bsz, dil, seq // dil, D_A), BF16)] * 3
            out_specs += [pl.BlockSpec((None, dil, tm // dil, D_A), lambda i: (i // tps, 0, i % tps, 0))] * 3
        first_tail = tps - WIN_A // tm
        out_shape += [jax.ShapeDtypeStruct((bsz, D_A, WIN_A), F32)] * 2
        out_specs += [pl.BlockSpec((None, D_A, tm), lambda i: (i // tps, 0, jnp.maximum(i % tps - first_tail, 0)))] * 2
        out_shape += [jax.ShapeDtypeStruct((bsz, D_BKV, WIN_B), F32)] * 2
        out_specs += [pl.BlockSpec((None, D_BKV, WIN_B), lambda i: (i // tps, 0, 0))] * 2
        assert DILATIONS[2] == DILATIONS[1] ** 2
        scratch = [pltpu.VMEM((3 * D_A // LANES, tm, LANES), F32)] * 2
    return pl.pallas_call(
        functools.partial(_attn_proj_kernel, tps),
        grid=(n // tm,),
        in_specs=[row(D_MODEL), _resident((D_MODEL, D_IN_ATTN)), tab, tab],
        out_specs=out_specs,
        out_shape=out_shape,
        scratch_shapes=scratch,
        compiler_params=_params(1),
        name="attn_proj",
    )(x, w, cos_t, sin_t)


def _lane_lo(shape):
    return lax.broadcasted_iota(jnp.int32, shape, 1) < HEAD_DIM


def _half_masks_bf16():
    lo = jnp.where(_lane_lo((1, LANES)), 1.0, 0.0).astype(BF16)
    return lo, 1 - lo


def _band_masks(n_heads, t, sub):
    row = lax.broadcasted_iota(jnp.int32, (n_heads * TQ, TQ), 0) & (TQ - 1)
    col = lax.broadcasted_iota(jnp.int32, (n_heads * TQ, TQ), 1)
    shift = jnp.where(t > 0, 0, TQ) if sub == 0 else 0
    return col <= row, col >= row + shift


def _sub_tile_kv(sub, sl, kc_ref, kp_ref, vc_ref, vp_ref):
    if sub == 0:
        return kc_ref[0:TQ, sl], kp_ref[:, sl], vc_ref[0:TQ, sl], vp_ref[:, sl]
    return kc_ref[TQ:2 * TQ, sl], kc_ref[0:TQ, sl], vc_ref[TQ:2 * TQ, sl], vc_ref[0:TQ, sl]


def _band_softmax(qs, kc, kp, vc, vp, mask_c, mask_p):
    s_c = jnp.where(mask_c, _dot_nt(qs, kc), NEG_INF)
    s_p = jnp.where(mask_p, _dot_nt(qs, kp), NEG_INF)
    m = jnp.max(jnp.maximum(s_c, s_p), -1, keepdims=True)
    p_c = jnp.exp2(s_c - m)
    p_p = jnp.exp2(s_p - m)
    den = jnp.sum(p_c + p_p, -1, keepdims=True)
    acc = _dot(p_c.astype(BF16), vc) + _dot(p_p.astype(BF16), vp)
    return acc * (1.0 / den), m, den


def _band_a_kernel(q_ref, kc_ref, kp_ref, vc_ref, vp_ref, o_ref, lse_ref):
    t = pl.program_id(2)
    lo = _lane_lo((TQ, LANES))
    lo_bf, hi_bf = _half_masks_bf16()
    lane_head = jnp.right_shift(lax.broadcasted_iota(jnp.int32, (TQ, LANES), 1), int(math.log2(LSE_LANES)))
    for sub in range(2):
        rows = slice(sub * TQ, (sub + 1) * TQ)
        mask_c, mask_p = _band_masks(2, t, sub)
        lse_tile = jnp.zeros((TQ, LANES), F32)
        for c in range(D_A // LANES):
            sl = slice(c * LANES, (c + 1) * LANES)
            kc, kp, vc, vp = _sub_tile_kv(sub, sl, kc_ref, kp_ref, vc_ref, vp_ref)
            q2 = q_ref[rows, sl]
            qs = jnp.concatenate([q2 * lo_bf, q2 * hi_bf], axis=0)
            out, m, den = _band_softmax(qs, kc, kp, vc, vp, mask_c, mask_p)
            lse = m + jnp.log2(den)
            o_ref[rows, sl] = jnp.where(lo, out[0:TQ], out[TQ:]).astype(o_ref.dtype)
            lse_tile = jnp.where(lane_head == 2 * c, lse[0:TQ], lse_tile)
            lse_tile = jnp.where(lane_head == 2 * c + 1, lse[TQ:], lse_tile)
        lse_ref[rows, :] = lse_tile


def _band_a(q, k, v):
    bsz, dil, sub, _ = q.shape
    cur = pl.BlockSpec((None, None, 2 * TQ, D_A), lambda b, r, t: (b, r, t, 0))
    prev = pl.BlockSpec((None, None, TQ, D_A), lambda b, r, t: (b, r, jnp.maximum(2 * t - 1, 0), 0))
    lse = pl.BlockSpec((None, None, 2 * TQ, LANES), lambda b, r, t: (b, r, t, 0))
    return pl.pallas_call(
        _band_a_kernel,
        grid=(bsz, dil, sub // (2 * TQ)),
        in_specs=[cur, cur, prev, cur, prev],
        out_specs=[cur, lse],
        out_shape=[jax.ShapeDtypeStruct((bsz, dil, sub, D_A), BF16),
                   jax.ShapeDtypeStruct((bsz, dil, sub, LANES), F32)],
        compiler_params=_params(3),
        name=f"band_a_d{dil}",
    )(q, k, k, v, v)


def _band_b_body(first_tile, sink_ref, q_ref, kc_ref, kp_ref, vc_ref, vp_ref, o_ref):
    group = N_HEADS_B // N_KV_B
    lo = _lane_lo((TQ, LANES))
    lo_bf, hi_bf = _half_masks_bf16()
    row = lax.broadcasted_iota(jnp.int32, (group * TQ, TQ), 0) & (TQ - 1)
    col = lax.broadcasted_iota(jnp.int32, (group * TQ, TQ), 1)
    in_cur = col <= row
    cur_bf = jnp.where(in_cur, 1.0, 0.0).astype(BF16)
    prev_bf = 1 - cur_bf
    for sub in range(2):
        rows = slice(sub * TQ, (sub + 1) * TQ)
        for g in range(N_KV_B):
            sl = slice(g * LANES, (g + 1) * LANES)
            kc, kp, vc, vp = _sub_tile_kv(sub, sl, kc_ref, kp_ref, vc_ref, vp_ref)
            heads = range(g * group, (g + 1) * group)
            qs = jnp.concatenate(
                [q_ref[rows, (h // 2) * LANES:(h // 2 + 1) * LANES] * (lo_bf if h % 2 == 0 else hi_bf) for h in heads],
                axis=0)
            sink = jnp.concatenate([jnp.full((TQ, 1), sink_ref[h] * LOG2E, F32) for h in heads], axis=0)
            only_cur = first_tile and sub == 0
            s = jnp.where(in_cur, _dot_nt(qs, kc), NEG_INF if only_cur else _dot_nt(qs, kp))
            m = jnp.maximum(jnp.max(s, -1, keepdims=True), sink)
            p = jnp.exp2(s - m)
            den = jnp.sum(p, -1, keepdims=True) + jnp.exp2(sink - m)
            pb = p.astype(BF16)
            acc = _dot(pb, vc) if only_cur else _dot(pb * cur_bf, vc) + _dot(pb * prev_bf, vp)
            out = acc * (1.0 / den)
            for i in range(group // 2):
                c = g * (group // 2) + i
                even, odd = out[2 * i * TQ:(2 * i + 1) * TQ], out[(2 * i + 1) * TQ:(2 * i + 2) * TQ]
                o_ref[rows, c * LANES:(c + 1) * LANES] = jnp.where(lo, even, odd).astype(o_ref.dtype)


def _band_b_kernel(*refs):
    t = pl.program_id(1)
    pl.when(t == 0)(functools.partial(_band_b_body, True, *refs))
    pl.when(t > 0)(functools.partial(_band_b_body, False, *refs))


def _band_b(q, k, v, sinks):
    bsz, seq, _ = q.shape
    kv_lanes = N_KV_B * LANES
    qs = pl.BlockSpec((None, 2 * TQ, D_BQ), lambda b, t: (b, t, 0))
    cur = pl.BlockSpec((None, 2 * TQ, kv_lanes), lambda b, t: (b, t, 0))
    prev = pl.BlockSpec((None, TQ, kv_lanes), lambda b, t: (b, jnp.maximum(2 * t - 1, 0), 0))
    o = pl.pallas_call(
        _band_b_kernel,
        grid=(bsz, seq // (2 * TQ)),
        in_specs=[pl.BlockSpec(memory_space=pltpu.SMEM), qs, cur, prev, cur, prev],
        out_specs=qs,
        out_shape=jax.ShapeDtypeStruct((bsz, seq, D_BQ), BF16),
        compiler_params=_params(2),
        name="band_b",
    )(sinks, q, k, k, v, v)
    return o.reshape(bsz * seq, D_BQ)


def _pattern_count(dist):
    cnt = jnp.zeros(dist.shape, F32)
    for dil in DILATIONS:
        cnt = cnt + ((dist >= 0) & (dist <= 128 * dil) & (dist % dil == 0)).astype(F32)
    return cnt


def _sample_tables():
    i = jnp.arange(DEC_SEQ)
    cnt_c = _pattern_count(WIN_A + i[:, None] - jnp.arange(WIN_A)[None, :])
    j = jnp.arange(LANES)
    cnt_n = jnp.where(j[None, :] < DEC_SEQ, _pattern_count(i[:, None] - j[None, :]), 0.0)
    cnt_n = jnp.tile(cnt_n, (N_HEADS_A, 1))
    jb = jnp.arange(NKB_PAD)[None, :]
    dist_b = WIN_B + i[:, None] - jb
    ok_b = (dist_b >= 0) & (dist_b < WIN_B) & (jb < WIN_B + DEC_SEQ)
    mask_b = jnp.tile(ok_b.astype(F32), (N_HEADS_B, 1))
    return cnt_c, cnt_n, mask_b


def _sample_attend(q, kan, van, kt_ref, vt_ref, cnt_c, cnt_n, qb, kbn, vbn, kbc, vbc, mask_b, sink_col, kb_s, vb_s):
    rows = N_HEADS_A * DEC_SEQ
    q_rep = jnp.concatenate([q] * N_HEADS_A, axis=0)
    row_head = jnp.right_shift(lax.broadcasted_iota(jnp.int32, (rows, D_A), 0), int(math.log2(DEC_SEQ)))
    lane_head = jnp.right_shift(lax.broadcasted_iota(jnp.int32, (rows, D_A), 1), int(math.log2(HEAD_DIM)))
    own = row_head == lane_head
    q_bd = jnp.where(own, q_rep, 0.0).astype(BF16)
    pad = jnp.zeros((LANES - DEC_SEQ, D_A), F32)
    kn = jnp.concatenate([kan, pad], 0).astype(BF16)
    vn = jnp.concatenate([van, pad], 0).astype(BF16)
    s_new = jnp.where(cnt_n > 0.0, _dot_nt(q_bd, kn), NEG_INF)
    outs, p_new = [], []
    for h in range(N_HEADS_A):
        head_rows = slice(h * DEC_SEQ, (h + 1) * DEC_SEQ)
        q_h = q[:, h * HEAD_DIM:(h + 1) * HEAD_DIM].astype(BF16)
        s_c = jnp.where(cnt_c > 0.0, _dot(q_h, kt_ref[h].astype(BF16)), NEG_INF)
        s_n = s_new[head_rows]
        m = jnp.maximum(jnp.max(s_c, -1, keepdims=True), jnp.max(s_n, -1, keepdims=True))
        p_c = jnp.exp(s_c - m) * cnt_c
        p_n = jnp.exp(s_n - m) * cnt_n[head_rows]
        inv = 1.0 / (jnp.sum(p_c, -1, keepdims=True) + jnp.sum(p_n, -1, keepdims=True))
        outs.append(_dot_nt(p_c.astype(BF16), vt_ref[h].astype(BF16)) * inv)
        p_new.append(p_n * inv)
    out_n = jnp.where(own, _dot(jnp.concatenate(p_new, axis=0).astype(BF16), vn), 0.0)
    oa = jnp.concatenate(outs, axis=1)
    for h in range(N_HEADS_A):
        oa = oa + out_n[h * DEC_SEQ:(h + 1) * DEC_SEQ]

    n_pad_b = NKB_PAD - WIN_B - DEC_SEQ
    pad_b = jnp.zeros((n_pad_b, D_BKV), F32)
    kb_s[...] = jnp.concatenate([kbc, kbn, pad_b], 0).astype(BF16)
    vb_s[...] = jnp.concatenate([vbc, vbn, pad_b], 0).astype(BF16)
    lo8 = _lane_lo((DEC_SEQ, LANES))
    group = N_HEADS_B // N_KV_B
    pieces = []
    for h in range(N_HEADS_B):
        chunk = qb[:, (h // 2) * LANES:(h // 2 + 1) * LANES]
        g = h // group
        if h % 2 != g:
            chunk = pltpu.roll(chunk, HEAD_DIM, 1)
        pieces.append(jnp.where(lo8 if g == 0 else jnp.logical_not(lo8), chunk, 0.0))
    qb_bd = jnp.concatenate(pieces, axis=0).astype(BF16)
    sb = jnp.where(mask_b > 0.0, _dot_nt(qb_bd, kb_s[...]), NEG_INF)
    sink = sink_col[:, 0:1]
    mb = jnp.maximum(jnp.max(sb, -1, keepdims=True), sink)
    pb = jnp.exp(sb - mb) * mask_b
    den_b = jnp.sum(pb, -1, keepdims=True) + jnp.exp(sink - mb)
    ob_full = _dot(pb.astype(BF16), vb_s[...]) * (1.0 / den_b)
    ob = []
    for c in range(D_BQ // LANES):
        halves = []
        for half in range(2):
            h = 2 * c + half
            piece = ob_full[h * DEC_SEQ:(h + 1) * DEC_SEQ]
            if half != h // group:
                piece = pltpu.roll(piece, HEAD_DIM, 1)
            halves.append(piece)
        ob.append(jnp.where(lo8, halves[0], halves[1]))
    return jnp.concatenate([oa] + ob, axis=1)


FUSED_ROW_TILE = 512
SEQ_PER_STEP = DEC_BATCH // (BATCH * SEQ // FUSED_ROW_TILE)


def _kv_copies(kt_hbm, vt_hbm, kbuf, vbuf, sems, seq, slot):
    return (pltpu.make_async_copy(kt_hbm.at[seq], kbuf.at[slot], sems.at[0, slot]),
            pltpu.make_async_copy(vt_hbm.at[seq], vbuf.at[slot], sems.at[1, slot]))


def _ffn_attn_kernel(x_ref, wg_ref, wu_ref, wd_ref, g_ref, b_ref,
                     qa_ref, kan_ref, van_ref, kt_hbm, vt_hbm, cnt_c_ref, cnt_n_ref,
                     qb_ref, kbn_ref, vbn_ref, kbc_ref, vbc_ref, maskb_ref, sinkcol_ref,
                     y_ref, o_ref, h_ref, kbuf, vbuf, sems, kb_s, vb_s):
    step = pl.program_id(0)
    n_steps = pl.num_programs(0)
    copies = functools.partial(_kv_copies, kt_hbm, vt_hbm, kbuf, vbuf, sems)

    @pl.when(step == 0)
    def _():
        for cp in copies(0, 0):
            cp.start()

    x = x_ref[...]
    xb = x.astype(BF16)
    n_ff = D_FF // FF_CHUNK
    per_seq = -(-n_ff // SEQ_PER_STEP)
    cnt_c, cnt_n, mask_b, sink_col = cnt_c_ref[...], cnt_n_ref[...], maskb_ref[...], sinkcol_ref[...]
    for s in range(SEQ_PER_STEP):
        slot = s % 2
        seq = step * SEQ_PER_STEP + s
        for cp in copies(seq, slot):
            cp.wait()
        if s + 1 < SEQ_PER_STEP:
            for cp in copies(seq + 1, 1 - slot):
                cp.start()
        else:
            @pl.when(step + 1 < n_steps)
            def _():
                for cp in copies(seq + 1, 1 - slot):
                    cp.start()
        rows = slice(s * DEC_SEQ, (s + 1) * DEC_SEQ)
        o_ref[rows, :] = _sample_attend(
            qa_ref[rows, :], kan_ref[rows, :], van_ref[rows, :], kbuf.at[slot], vbuf.at[slot], cnt_c, cnt_n,
            qb_ref[rows, :], kbn_ref[rows, :], vbn_ref[rows, :], kbc_ref[s], vbc_ref[s], mask_b, sink_col, kb_s, vb_s)
        for c in range(s * per_seq, min((s + 1) * per_seq, n_ff)):
            sl = slice(c * FF_CHUNK, (c + 1) * FF_CHUNK)
            gate = _dot(xb, wg_ref[:, sl])
            up = _dot(xb, wu_ref[:, sl])
            h_ref[:, sl] = (gate * jax.nn.sigmoid(gate) * up).astype(BF16)
    y = DN_ALPHA * x + FFN_RES * _dot(h_ref[...], wd_ref[...])
    y_ref[...] = _layer_norm(y, g_ref[...], b_ref[...])


def _ffn_with_sample_attention(x, wg, wu, wd, g, b, layer, which,
                               qa, kan, van, cache_ak, cache_av, qb, kbn, vbn, cache_bk, cache_bv, sinks):
    n = x.shape[0]
    tm = FUSED_ROW_TILE
    assert n // tm * SEQ_PER_STEP == DEC_BATCH and SEQ_PER_STEP % 2 == 0
    cnt_c, cnt_n, mask_b = _sample_tables()
    sink_col = jnp.broadcast_to(jnp.repeat(sinks.astype(F32), DEC_SEQ)[:, None], (N_HEADS_B * DEC_SEQ, LANES))
    row = pl.BlockSpec((tm, D_MODEL), lambda i: (i, 0))
    pick = lambda r, c: pl.BlockSpec((None, None, r, c), lambda i: (layer, which, 0, 0), pipeline_mode=pl.Buffered(1))
    new = lambda width: pl.BlockSpec((SEQ_PER_STEP * DEC_SEQ, width), lambda i: (i, 0))
    cache_b = pl.BlockSpec((SEQ_PER_STEP, WIN_B, D_BKV), lambda i: (i, 0, 0))
    hbm = pl.BlockSpec(memory_space=pl.ANY)
    rows = N_HEADS_A * DEC_SEQ
    kv_slot = (2, N_HEADS_A, HEAD_DIM, WIN_A)
    return pl.pallas_call(
        _ffn_attn_kernel,
        grid=(n // tm,),
        in_specs=[row, pick(D_MODEL, D_FF), pick(D_MODEL, D_FF), pick(D_FF, D_MODEL),
                  _resident((1, D_MODEL)), _resident((1, D_MODEL)),
                  new(D_A), new(D_A), new(D_A), hbm, hbm, _resident(cnt_c.shape), _resident(cnt_n.shape),
                  new(D_BQ), new(D_BKV), new(D_BKV), cache_b, cache_b,
                  _resident((rows, NKB_PAD)), _resident((rows, LANES))],
        out_specs=[row, new(D_A + D_BQ)],
        out_shape=[jax.ShapeDtypeStruct((n, D_MODEL), F32),
                   jax.ShapeDtypeStruct((DEC_BATCH * DEC_SEQ, D_A + D_BQ), F32)],
        scratch_shapes=[pltpu.VMEM((tm, D_FF), BF16), pltpu.VMEM(kv_slot, F32), pltpu.VMEM(kv_slot, F32),
                        pltpu.SemaphoreType.DMA((2, 2)),
                        pltpu.VMEM((NKB_PAD, D_BKV), BF16), pltpu.VMEM((NKB_PAD, D_BKV), BF16)],
        compiler_params=_params(1, vmem=60 * 1024 * 1024),
        name="ffn_attn_sample",
    )(x, wg, wu, wd, g, b, qa, kan, van, cache_ak, cache_av, cnt_c, cnt_n, qb, kbn, vbn, cache_bk, cache_bv,
      mask_b, sink_col)


def _attn_out_prompt_kernel(o1_ref, l1_ref, o4_ref, l4_ref, o16_ref, l16_ref, ob_ref, y_ref, w_ref, g_ref, b_ref,
                            out_ref, slab_ref, oa_ref):
    tm = y_ref.shape[0]
    n_chunks = D_A // LANES
    slabs = {}
    base = 0
    for name, dil, src, width in (("o4", 4, o4_ref, n_chunks), ("l4", 4, l4_ref, 1),
                                  ("o16", 16, o16_ref, n_chunks), ("l16", 16, l16_ref, 1)):
        slabs[name] = base
        for r in range(dil):
            for c in range(width):
                piece = src[r, :, c * LANES:(c + 1) * LANES]
                slab_ref[base + c, pl.ds(r, tm // dil, stride=dil), :] = piece.astype(F32)
        base += width
    l1, l4, l16 = l1_ref[...], slab_ref[slabs["l4"]], slab_ref[slabs["l16"]]
    m = jnp.maximum(jnp.maximum(l1, l4), l16)
    e1, e4, e16 = jnp.exp2(l1 - m), jnp.exp2(l4 - m), jnp.exp2(l16 - m)
    inv = 1.0 / (e1 + e4 + e16)
    row = lax.broadcasted_iota(jnp.int32, (LANES, D_A), 0)
    head_of_col = jnp.right_shift(lax.broadcasted_iota(jnp.int32, (LANES, D_A), 1), int(math.log2(HEAD_DIM)))
    spread = jnp.where(row == head_of_col * LSE_LANES, 1.0, 0.0).astype(BF16)

    def per_head_lanes(w):
        hi = w.astype(BF16)
        lo = (w - hi.astype(F32)).astype(BF16)
        return _dot(hi, spread) + _dot(lo, spread)

    w1, w4, w16 = per_head_lanes(e1 * inv), per_head_lanes(e4 * inv), per_head_lanes(e16 * inv)
    for c in range(n_chunks):
        sl = slice(c * LANES, (c + 1) * LANES)
        oa = (w1[:, sl] * o1_ref[:, sl].astype(F32) + w4[:, sl] * slab_ref[slabs["o4"] + c]
              + w16[:, sl] * slab_ref[slabs["o16"] + c])
        oa_ref[:, sl] = oa.astype(BF16)
    mix = _dot(oa_ref[...], w_ref[0:D_A, :]) + _dot(ob_ref[...], w_ref[D_A:, :])
    out_ref[...] = _layer_norm(DN_ALPHA * y_ref[...] + mix, g_ref[...], b_ref[...])


def _attn_out_prompt(pats, ob, y, w, g, b, seq):
    n = y.shape[0]
    tm = ROW_TILE
    tps = seq // tm
    half = pl.BlockSpec((tm, D_A), lambda i: (i, 0))
    full = pl.BlockSpec((tm, D_MODEL), lambda i: (i, 0))
    lse1 = pl.BlockSpec((tm, LANES), lambda i: (i, 0))
    planes = lambda dil, width: pl.BlockSpec((None, dil, tm // dil, width), lambda i: (i // tps, 0, i % tps, 0))
    (o1, l1), (o4, l4), (o16, l16) = pats
    return pl.pallas_call(
        _attn_out_prompt_kernel,
        grid=(n // tm,),
        in_specs=[half, lse1, planes(4, D_A), planes(4, LANES), planes(16, D_A), planes(16, LANES), half, full,
                  _resident((D_MODEL, D_MODEL)), _resident((1, D_MODEL)), _resident((1, D_MODEL))],
        out_specs=full,
        out_shape=jax.ShapeDtypeStruct((n, D_MODEL), F32),
        scratch_shapes=[pltpu.VMEM((2 * (D_A // LANES + 1), tm, LANES), F32), pltpu.VMEM((tm, D_A), BF16)],
        compiler_params=_params(1),
        name="attn_out_prompt",
    )(o1, l1, o4, l4, o16, l16, ob, y, w, g, b)


def _mix_out_kernel(o_ref, y_ref, w_ref, g_ref, b_ref, out_ref):
    mix = _dot(o_ref[...].astype(BF16), w_ref[...])
    out_ref[...] = _layer_norm(DN_ALPHA * y_ref[...] + mix, g_ref[...], b_ref[...])


def _attn_out_sample(o, y, w, g, b):
    n = y.shape[0]
    tm = min(ROW_TILE, n)
    full = pl.BlockSpec((tm, D_MODEL), lambda i: (i, 0))
    return pl.pallas_call(
        _mix_out_kernel,
        grid=(n // tm,),
        in_specs=[full, full, _resident((D_MODEL, D_MODEL)), _resident((1, D_MODEL)), _resident((1, D_MODEL))],
        out_specs=full,
        out_shape=jax.ShapeDtypeStruct((n, D_MODEL), F32),
        compiler_params=_params(1),
        name="attn_out_sample",
    )(o, y, w, g, b)


def _ssm_discretize(lam_re, lam_im, log_dt, b_re, b_im):
    dt = jnp.exp(log_dt.astype(F32))[:, None]
    lr, li = lam_re.astype(F32), lam_im.astype(F32)
    mag = jnp.exp(lr * dt)
    ab_re, ab_im = mag * jnp.cos(li * dt), mag * jnp.sin(li * dt)
    nr, ni = ab_re - 1.0, ab_im
    den = lr * lr + li * li
    fr, fi = (nr * lr + ni * li) / den, (ni * lr - nr * li) / den
    bb_re = fr[..., None] * b_re - fi[..., None] * b_im
    bb_im = fr[..., None] * b_im + fi[..., None] * b_re
    return ab_re, ab_im, bb_re, bb_im


def _ssm_matrices(bb_re, bb_im, c_re, c_im):
    gpc = LANES // SSM_GROUP
    eye = jnp.eye(gpc, dtype=F32)

    def in_blocks(bb):
        a = bb.reshape(N_LCHUNK, gpc, SSM_STATE, SSM_GROUP)
        return jnp.einsum("jgpn,gh->jgnhp", a, eye).reshape(N_LCHUNK, LANES, ST_CHUNK)

    def out_blocks(cc):
        a = cc.reshape(N_LCHUNK, gpc, SSM_GROUP, SSM_STATE)
        return jnp.einsum("jgnp,gh->jgphn", a, eye).reshape(N_LCHUNK, ST_CHUNK, LANES)

    bmat = jnp.concatenate([in_blocks(bb_re), in_blocks(bb_im)], -1)
    cmat = jnp.concatenate([out_blocks(c_re), -out_blocks(c_im)], 1)
    return bmat, cmat


CHUNK = 8


def _dot3(a, b):
    a_hi, b_hi = a.astype(BF16), b.astype(BF16)
    a_lo, b_lo = (a - a_hi.astype(F32)).astype(BF16), (b - b_hi.astype(F32)).astype(BF16)
    return _dot(a_hi, b_hi) + _dot(a_hi, b_lo) + _dot(a_lo, b_hi)


def _chunk_weights_kernel(ar_ref, ai_ref, acr_ref, aci_ref, bmat_ref, cmat_ref,
                          we_ref, ws_ref, wi_ref, a8r_ref, a8i_ref):
    def powers(r, i, n):
        out = [(jnp.ones_like(r), jnp.zeros_like(r))]
        for _ in range(n):
            out.append(_cmul(out[-1][0], out[-1][1], r, i))
        return out

    row_pow = powers(ar_ref[...], ai_ref[...], CHUNK)
    col_pow = powers(acr_ref[...], aci_ref[...], CHUNK)
    b_re, b_im = bmat_ref[:, 0:ST_CHUNK], bmat_ref[:, ST_CHUNK:]
    c_re, c_im = cmat_ref[0:ST_CHUNK, :], -cmat_ref[ST_CHUNK:, :]

    def scaled_b(power):
        pr, pi = row_pow[power]
        return jnp.concatenate([b_re * pr - b_im * pi, b_re * pi + b_im * pr], axis=1)

    taps = []
    for tau in range(CHUNK):
        sb = scaled_b(tau)
        we_ref[(CHUNK - 1 - tau) * LANES:(CHUNK - tau) * LANES, :] = sb.astype(BF16)
        taps.append(_dot3(sb, cmat_ref[...]).astype(BF16))
    for k in range(CHUNK):
        cols = slice(k * LANES, (k + 1) * LANES)
        qr, qi = col_pow[k + 1]
        ws_ref[0:ST_CHUNK, cols] = (c_re * qr - c_im * qi).astype(BF16)
        ws_ref[ST_CHUNK:, cols] = (-(c_re * qi + c_im * qr)).astype(BF16)
    zero = jnp.zeros((LANES, LANES), BF16)
    for k_in in range(CHUNK):
        for k_out in range(CHUNK):
            wi_ref[k_in * LANES:(k_in + 1) * LANES, k_out * LANES:(k_out + 1) * LANES] = (
                taps[k_out - k_in] if k_out >= k_in else zero)
    a8r_ref[...], a8i_ref[...] = row_pow[CHUNK]


def _chunk_weights(a_re, a_im, bmat, cmat):
    cols = lambda a: jnp.broadcast_to(a.reshape(N_LCHUNK, ST_CHUNK, 1), (N_LCHUNK, ST_CHUNK, LANES))
    per_j = lambda r, c: pl.BlockSpec((None, r, c), lambda j: (j, 0, 0))
    wide = 2 * ST_CHUNK
    return pl.pallas_call(
        _chunk_weights_kernel,
        grid=(N_LCHUNK,),
        in_specs=[per_j(1, ST_CHUNK), per_j(1, ST_CHUNK), per_j(ST_CHUNK, LANES), per_j(ST_CHUNK, LANES),
                  per_j(LANES, wide), per_j(wide, LANES)],
        out_specs=[per_j(CHUNK * LANES, wide), per_j(wide, CHUNK * LANES), per_j(CHUNK * LANES, CHUNK * LANES),
                   per_j(1, ST_CHUNK), per_j(1, ST_CHUNK)],
        out_shape=[jax.ShapeDtypeStruct((N_LCHUNK, CHUNK * LANES, wide), BF16),
                   jax.ShapeDtypeStruct((N_LCHUNK, wide, CHUNK * LANES), BF16),
                   jax.ShapeDtypeStruct((N_LCHUNK, CHUNK * LANES, CHUNK * LANES), BF16),
                   jax.ShapeDtypeStruct((N_LCHUNK, 1, ST_CHUNK), F32),
                   jax.ShapeDtypeStruct((N_LCHUNK, 1, ST_CHUNK), F32)],
        compiler_params=_params(1),
        name="ssm_chunk_weights",
    )(a_re, a_im, cols(a_re), cols(a_im), bmat, cmat)


SEG_TILE = ROW_TILE // N_SEG


def _ssm_in_prompt_kernel(x_ref, w_ref, o_ref):
    x = x_ref[...].reshape(N_SEG * SEG_TILE, D_MODEL)
    u = _dot(x.astype(BF16), w_ref[...])
    for s in range(N_SEG):
        for c in range(N_LCHUNK):
            o_ref[c, pl.ds(s, SEG_TILE, stride=N_SEG), :] = u[s * SEG_TILE:(s + 1) * SEG_TILE, c * LANES:(c + 1) * LANES]


def _ssm_in_prompt(y, w):
    return pl.pallas_call(
        _ssm_in_prompt_kernel,
        grid=(BATCH, SEG_LEN // SEG_TILE),
        in_specs=[pl.BlockSpec((None, N_SEG, SEG_TILE, D_MODEL), lambda b, t: (b, 0, t, 0)),
                  _resident((D_MODEL, D_MODEL))],
        out_specs=pl.BlockSpec((None, N_LCHUNK, ROW_TILE, LANES), lambda b, t: (b, 0, t, 0)),
        out_shape=jax.ShapeDtypeStruct((BATCH, N_LCHUNK, SEQ, LANES), F32),
        compiler_params=_params(2),
        name="ssm_in_prompt",
    )(y.reshape(BATCH, N_SEG, SEG_LEN, D_MODEL), w)


def _ssm_in_sample_kernel(x_ref, w_ref, o_ref, slab_ref):
    u = _dot(x_ref[...].astype(BF16), w_ref[...])
    for c in range(N_LCHUNK):
        slab_ref[c] = u[:, c * LANES:(c + 1) * LANES]
    for l in range(DEC_SEQ):
        for c in range(N_LCHUNK):
            o_ref[c, l * DEC_BATCH:(l + 1) * DEC_BATCH, :] = slab_ref[c, pl.ds(l, DEC_BATCH, stride=DEC_SEQ), :]


def _ssm_in_sample(y, w):
    n = DEC_BATCH * DEC_SEQ
    return pl.pallas_call(
        _ssm_in_sample_kernel,
        grid=(1,),
        in_specs=[_resident((n, D_MODEL)), _resident((D_MODEL, D_MODEL))],
        out_specs=pl.BlockSpec((N_LCHUNK, n, LANES), lambda i: (0, 0, 0)),
        out_shape=jax.ShapeDtypeStruct((N_LCHUNK, n, LANES), F32),
        scratch_shapes=[pltpu.VMEM((N_LCHUNK, n, LANES), F32)],
        compiler_params=_params(1),
        name="ssm_in_sample",
    )(y, w)


def _cmul(ar, ai, br, bi):
    return ar * br - ai * bi, ar * bi + ai * br


def _scan_prompt_kernel(u_ref, we_ref, ws_ref, wi_ref, a8r_ref, a8i_ref, d_ref, h0r_ref, h0i_ref,
                        y_ref, hnr_ref, hni_ref, e_s, hs_s):
    n_chunks = u_ref.shape[0]
    rows = n_chunks * N_SEG
    u_flat = jnp.concatenate([u_ref[:, k].reshape(rows, LANES) for k in range(CHUNK)], axis=1)
    ub = u_flat.astype(BF16)
    e_s[...] = _dot(ub, we_ref[...])
    a8r1, a8i1 = a8r_ref[...], a8i_ref[...]
    a8r = jnp.broadcast_to(a8r1, (N_SEG, ST_CHUNK))
    a8i = jnp.broadcast_to(a8i1, (N_SEG, ST_CHUNK))

    def advance(row, hr, hi):
        er = e_s[pl.ds(row, N_SEG), 0:ST_CHUNK]
        ei = e_s[pl.ds(row, N_SEG), ST_CHUNK:2 * ST_CHUNK]
        return a8r * hr - a8i * hi + er, a8r * hi + a8i * hr + ei

    def pass1(c, carry):
        return advance(pl.multiple_of(c * N_SEG, N_SEG), *carry)

    zero = jnp.zeros((N_SEG, ST_CHUNK), F32)
    er, ei = lax.fori_loop(0, n_chunks, pass1, (zero, zero), unroll=8)

    pr, pi = a8r1, a8i1
    for _ in range(int(math.log2(n_chunks))):
        pr, pi = _cmul(pr, pi, pr, pi)
    hr, hi = h0r_ref[...], h0i_ref[...]
    starts_r, starts_i = [], []
    for s in range(N_SEG):
        starts_r.append(hr)
        starts_i.append(hi)
        gr, gi = _cmul(pr, pi, hr, hi)
        hr, hi = gr + er[s:s + 1], gi + ei[s:s + 1]
    hnr_ref[...] = hr
    hni_ref[...] = hi
    init = (jnp.concatenate(starts_r, 0), jnp.concatenate(starts_i, 0))

    def pass2(c, carry):
        row = pl.multiple_of(c * N_SEG, N_SEG)
        hs_s[pl.ds(row, N_SEG), 0:ST_CHUNK] = carry[0]
        hs_s[pl.ds(row, N_SEG), ST_CHUNK:2 * ST_CHUNK] = carry[1]
        return advance(row, *carry)

    lax.fori_loop(0, n_chunks, pass2, init, unroll=8)
    y = _dot(hs_s[...].astype(BF16), ws_ref[...]) + _dot(ub, wi_ref[...]) + d_ref[...] * u_flat
    for k in range(CHUNK):
        y_ref[:, k] = y[:, k * LANES:(k + 1) * LANES].reshape(n_chunks, N_SEG, LANES)


def _scan_prompt(u, weights, d_skip, h0r, h0i):
    bsz, _, seq, _ = u.shape
    n_chunks = seq // (CHUNK * N_SEG)
    rows = n_chunks * N_SEG
    wide = 2 * ST_CHUNK
    split = lambda a: a.reshape(bsz, N_LCHUNK, n_chunks, CHUNK, N_SEG, LANES)
    chunk = pl.BlockSpec((None, None, n_chunks, CHUNK, N_SEG, LANES), lambda j, b: (b, j, 0, 0, 0, 0))
    per_j = lambda r, c: pl.BlockSpec((None, r, c), lambda j, b: (j, 0, 0))
    state = pl.BlockSpec((None, 1, ST_CHUNK), lambda j, b: (b, 0, j))
    y, hr, hi = pl.pallas_call(
        _scan_prompt_kernel,
        grid=(N_LCHUNK, bsz),
        in_specs=[chunk, per_j(CHUNK * LANES, wide), per_j(wide, CHUNK * LANES), per_j(CHUNK * LANES, CHUNK * LANES),
                  per_j(1, ST_CHUNK), per_j(1, ST_CHUNK), per_j(1, CHUNK * LANES), state, state],
        out_specs=[chunk, state, state],
        out_shape=[jax.ShapeDtypeStruct((bsz, N_LCHUNK, n_chunks, CHUNK, N_SEG, LANES), F32),
                   jax.ShapeDtypeStruct((bsz, 1, N_STATE), F32), jax.ShapeDtypeStruct((bsz, 1, N_STATE), F32)],
        scratch_shapes=[pltpu.VMEM((rows, wide), F32), pltpu.VMEM((rows, wide), F32)],
        compiler_params=_params(2),
        name="ssm_scan_prompt",
    )(split(u), *weights, d_skip, h0r, h0i)
    return y.reshape(u.shape), hr, hi


def _scan_sample_kernel(u_ref, bmat_ref, cmat_ref, are_ref, aim_ref, d_ref, h0r_ref, h0i_ref,
                        y_ref, hnr_ref, hni_ref, h_s):
    a_re, a_im = are_ref[...], aim_ref[...]
    hr, hi = h0r_ref[...].T, h0i_ref[...].T
    u = u_ref[...]
    bu = _dot(u.astype(BF16), bmat_ref[...])
    for l in range(DEC_SEQ):
        rows = slice(l * DEC_BATCH, (l + 1) * DEC_BATCH)
        gr, gi = _cmul(a_re, a_im, hr, hi)
        hr, hi = gr + bu[rows, :ST_CHUNK], gi + bu[rows, ST_CHUNK:]
        h_s[rows, 0:ST_CHUNK] = hr.astype(BF16)
        h_s[rows, ST_CHUNK:] = hi.astype(BF16)
    y_ref[...] = _dot(h_s[...], cmat_ref[...]) + d_ref[...] * u
    hnr_ref[...] = hr.T
    hni_ref[...] = hi.T


def _scan_sample(u, bmat, cmat, a_re, a_im, d_skip, h0r, h0i):
    n = DEC_SEQ * DEC_BATCH
    chunk = pl.BlockSpec((None, n, LANES), lambda j: (j, 0, 0))
    per_j = lambda r, c: pl.BlockSpec((None, r, c), lambda j: (j, 0, 0))
    state = pl.BlockSpec((ST_CHUNK, DEC_BATCH), lambda j: (j, 0))
    return pl.pallas_call(
        _scan_sample_kernel,
        grid=(N_LCHUNK,),
        in_specs=[chunk, per_j(LANES, 2 * ST_CHUNK), per_j(2 * ST_CHUNK, LANES), per_j(1, ST_CHUNK),
                  per_j(1, ST_CHUNK), per_j(1, LANES), state, state],
        out_specs=[chunk, state, state],
        out_shape=[jax.ShapeDtypeStruct((N_LCHUNK, n, LANES), F32),
                   jax.ShapeDtypeStruct((N_STATE, DEC_BATCH), F32), jax.ShapeDtypeStruct((N_STATE, DEC_BATCH), F32)],
        scratch_shapes=[pltpu.VMEM((n, 2 * ST_CHUNK), BF16)],
        compiler_params=_params(1),
        name="ssm_scan_sample",
    )(u, bmat, cmat, a_re, a_im, d_skip, h0r, h0i)


def _ssm_out_kernel(sample, s_ref, y_ref, wglu_ref, bglu_ref, wout_ref, g_ref, b_ref, out_ref, z_ref):
    n_rows = z_ref.shape[1]
    if sample:
        parts = [(pl.ds(l, DEC_BATCH, stride=DEC_SEQ), slice(l * DEC_BATCH, (l + 1) * DEC_BATCH))
                 for l in range(DEC_SEQ)]
    else:
        parts = [(slice(s * SEG_TILE, (s + 1) * SEG_TILE), pl.ds(s, SEG_TILE, stride=N_SEG)) for s in range(N_SEG)]
    for c in range(N_LCHUNK):
        for tok_rows, slab_rows in parts:
            z_ref[c, tok_rows, :] = s_ref[c, slab_rows, :]
    z = jax.nn.gelu(jnp.concatenate([z_ref[c] for c in range(N_LCHUNK)], axis=1))
    gate = jax.nn.sigmoid(_dot(z.astype(BF16), wglu_ref[...]) + bglu_ref[...])
    mix = _dot((z * gate).astype(BF16), wout_ref[...])
    res = y_ref[...].reshape(n_rows, D_MODEL)
    out = _layer_norm(DN_ALPHA * res + mix, g_ref[...], b_ref[...])
    out_ref[...] = out.reshape(out_ref.shape)


def _ssm_out(s, y, s_spec, y_spec, grid, rows, sample, w_glu, b_glu, w_out, g, b, name):
    return pl.pallas_call(
        functools.partial(_ssm_out_kernel, sample),
        grid=grid,
        in_specs=[s_spec, y_spec, _resident((D_MODEL, D_MODEL)), _resident((1, D_MODEL)),
                  _resident((D_MODEL, D_MODEL)), _resident((1, D_MODEL)), _resident((1, D_MODEL))],
        out_specs=y_spec,
        out_shape=jax.ShapeDtypeStruct(y.shape, F32),
        scratch_shapes=[pltpu.VMEM((N_LCHUNK, rows, LANES), F32)],
        compiler_params=_params(len(grid)),
        name=name,
    )(s, y, w_glu, b_glu, w_out, g, b)


def _attn_prompt(yp, w_in, sinks, w_out, g, b):
    cos_p, sin_p = _rope_tables(jnp.arange(SEQ))
    tiles_per_seq = SEQ // ROW_TILE
    qa, ka, va, qb, kb, vb, *extra = _attn_proj(
        yp, w_in, cos_p, sin_p, lambda i: i % tiles_per_seq, BF16, prompt_seq=(BATCH, SEQ))
    dilated, tails = extra[:6], extra[6:]
    seq3 = lambda a: a.reshape(BATCH, SEQ, a.shape[-1])
    plane1 = lambda a: a.reshape(BATCH, 1, SEQ, D_A)
    o1, l1 = _band_a(plane1(qa), plane1(ka), plane1(va))
    pats = [(o1.reshape(BATCH * SEQ, D_A), l1.reshape(BATCH * SEQ, LANES))]
    for i in range(len(DILATIONS) - 1):
        pats.append(_band_a(*dilated[3 * i:3 * i + 3]))
    ob = _band_b(seq3(qb), seq3(kb), seq3(vb), sinks)
    yp = _attn_out_prompt(pats, ob, yp, w_out, g, b, SEQ)
    heads = lambda a, nh: jnp.transpose(a.reshape(BATCH, nh, HEAD_DIM, a.shape[-1]), (0, 3, 1, 2))[None]
    prompt_cache = (heads(tails[0], N_HEADS_A), heads(tails[1], N_HEADS_A),
                    heads(tails[2], N_KV_B), heads(tails[3], N_KV_B))
    return yp, prompt_cache


def _attn_sample_path(ys, cache_ak, cache_av, cache_bk, cache_bv, w_in, sinks, w_out, g, b, yp, ffn_args):
    cos_s, sin_s = _rope_tables(PAST_LEN + jnp.arange(DEC_SEQ))
    reps = DEC_BATCH * DEC_SEQ // DEC_SEQ
    cos_s, sin_s = jnp.tile(cos_s, (reps, 1)), jnp.tile(sin_s, (reps, 1))
    qa, ka, va, qb, kb, vb = _attn_proj(ys, w_in, cos_s, sin_s, lambda i: i, F32)
    stored = lambda a: jnp.transpose(a, (0, 2, 3, 1))
    yp, o = _ffn_with_sample_attention(
        yp, *ffn_args, qa, ka, va, stored(cache_ak), stored(cache_av),
        qb, kb, vb, cache_bk.reshape(DEC_BATCH, WIN_B, D_BKV), cache_bv.reshape(DEC_BATCH, WIN_B, D_BKV), sinks)
    ys = _attn_out_sample(o, ys, w_out, g, b)
    new = lambda a, nh: a.reshape(1, DEC_BATCH, DEC_SEQ, nh, HEAD_DIM)
    sample_cache = (new(ka, N_HEADS_A), new(va, N_HEADS_A), new(kb, N_KV_B), new(vb, N_KV_B))
    return yp, ys, sample_cache


def _attn_layer(yp, ys, cache_ak, cache_av, cache_bk, cache_bv, w_in, sinks, w_out, g, b, ffn_args):
    w_in = w_in.astype(BF16)
    w_out = w_out.astype(BF16)
    yp, prompt_cache = _attn_prompt(yp, w_in, sinks, w_out, g, b)
    yp, ys, sample_cache = _attn_sample_path(ys, cache_ak, cache_av, cache_bk, cache_bv, w_in, sinks, w_out, g, b,
                                             yp, ffn_args)
    return yp, ys, prompt_cache, sample_cache


def _ssm_layer(yp, ys, state_re, state_im, w_in, lam_re, lam_im, log_dt, b_re, b_im, c_re, c_im, d_skip,
               w_glu, b_glu, w_out, g, b):
    w_in, w_glu, w_out = w_in.astype(BF16), w_glu.astype(BF16), w_out.astype(BF16)
    b_glu = b_glu.reshape(1, D_MODEL)
    mats = _ssm_prepare(lam_re, lam_im, log_dt, b_re, b_im, c_re, c_im, d_skip)
    yp, prompt_state = _ssm_prompt(yp, w_in, mats, w_glu, b_glu, w_out, g, b)
    ys, sample_state = _ssm_sample(ys, state_re, state_im, w_in, mats, w_glu, b_glu, w_out, g, b)
    return yp, ys, prompt_state, sample_state


def _ssm_prepare(lam_re, lam_im, log_dt, b_re, b_im, c_re, c_im, d_skip):
    ab_re, ab_im, bb_re, bb_im = _ssm_discretize(lam_re, lam_im, log_dt, b_re, b_im)
    bmat, cmat = _ssm_matrices(bb_re, bb_im, c_re, c_im)
    a_re = ab_re.reshape(N_LCHUNK, 1, ST_CHUNK)
    a_im = ab_im.reshape(N_LCHUNK, 1, ST_CHUNK)
    d3 = d_skip.astype(F32).reshape(N_LCHUNK, 1, LANES)
    prompt = (_chunk_weights(a_re, a_im, bmat, cmat), jnp.tile(d3, (1, 1, CHUNK)))
    sample = (bmat.astype(BF16), cmat.astype(BF16), a_re, a_im, d3)
    return prompt, sample


def _ssm_prompt(yp, w_in, mats, w_glu, b_glu, w_out, g, b):
    weights, d_tiled = mats[0]
    up = _ssm_in_prompt(yp, w_in)
    zero = jnp.zeros((BATCH, 1, N_STATE), F32)
    sp, pr, pi = _scan_prompt(up, weights, d_tiled, zero, zero)
    yp = _ssm_out(sp, yp.reshape(BATCH, N_SEG, SEG_LEN, D_MODEL),
                  pl.BlockSpec((None, N_LCHUNK, ROW_TILE, LANES), lambda bb, t: (bb, 0, t, 0)),
                  pl.BlockSpec((None, N_SEG, SEG_TILE, D_MODEL), lambda bb, t: (bb, 0, t, 0)),
                  (BATCH, SEG_LEN // SEG_TILE), ROW_TILE, False, w_glu, b_glu, w_out, g, b,
                  "ssm_out_prompt").reshape(BATCH * SEQ, D_MODEL)
    prompt_state = (pr.reshape(1, BATCH, N_SSM_GROUPS, SSM_STATE), pi.reshape(1, BATCH, N_SSM_GROUPS, SSM_STATE))
    return yp, prompt_state


def _ssm_sample(ys, state_re, state_im, w_in, mats, w_glu, b_glu, w_out, g, b):
    bmat, cmat, a_re, a_im, d3 = mats[1]
    us = _ssm_in_sample(ys, w_in)
    stored = lambda a: jnp.transpose(a, (1, 2, 0)).reshape(N_STATE, DEC_BATCH)
    logical = lambda a: jnp.transpose(a.reshape(N_SSM_GROUPS, SSM_STATE, DEC_BATCH), (2, 0, 1))[None]
    ss, sr, si = _scan_sample(us, bmat, cmat, a_re, a_im, d3, stored(state_re), stored(state_im))
    n = DEC_BATCH * DEC_SEQ
    ys = _ssm_out(ss, ys,
                  pl.BlockSpec((N_LCHUNK, n, LANES), lambda i: (0, 0, 0)),
                  pl.BlockSpec((n, D_MODEL), lambda i: (0, 0)),
                  (1,), n, True, w_glu, b_glu, w_out, g, b, "ssm_out_sample")
    return ys, (logical(sr), logical(si))


def kernel(x_prompt, x_sample, cache_a_k, cache_a_v, cache_b_k, cache_b_v, state_c_re, state_c_im, ln_g, ln_b, ffn_w_gate, ffn_w_up, ffn_w_down, attn_w_in, attn_sinks, attn_w_out, ssm_w_in, ssm_lambda_re, ssm_lambda_im, ssm_log_dt, ssm_b_re, ssm_b_im, ssm_c_re, ssm_c_im, ssm_d, ssm_w_glu, ssm_b_glu, ssm_w_out):
    yp = x_prompt.reshape(BATCH * SEQ, D_MODEL)
    ys = x_sample.reshape(DEC_BATCH * DEC_SEQ, D_MODEL)
    ln = lambda l, k: (ln_g[l, k].reshape(1, D_MODEL), ln_b[l, k].reshape(1, D_MODEL))

    wg, wu, wd = ffn_w_gate.astype(BF16), ffn_w_up.astype(BF16), ffn_w_down.astype(BF16)

    def ffn_pair(yp, ys, l, k, ln_idx):
        g, b = ln(l, ln_idx)
        return _ffn_pair(yp, ys, wg, wu, wd, g, b, l, k)

    yp, ys = ffn_pair(yp, ys, 0, 0, 0)
    yp, ys, p_cache, s_cache = _attn_layer(yp, ys, cache_a_k[0], cache_a_v[0], cache_b_k[0], cache_b_v[0],
                                           attn_w_in[0], attn_sinks[0], attn_w_out[0], *ln(0, 1),
                                           ffn_args=(wg, wu, wd, *ln(0, 2), 0, 1))
    ys = _ffn(ys, wg, wu, wd, *ln(0, 2), 0, 1)
    yp, ys = ffn_pair(yp, ys, 1, 0, 0)
    yp, ys, p_state, s_state = _ssm_layer(yp, ys, state_c_re[0], state_c_im[0], ssm_w_in[0], ssm_lambda_re[0],
                                          ssm_lambda_im[0], ssm_log_dt[0], ssm_b_re[0], ssm_b_im[0], ssm_c_re[0],
                                          ssm_c_im[0], ssm_d[0], ssm_w_glu[0], ssm_b_glu[0], ssm_w_out[0], *ln(1, 1))
    yp, ys = ffn_pair(yp, ys, 1, 1, 2)
    return (yp.reshape(BATCH, SEQ, D_MODEL), ys.reshape(DEC_BATCH, DEC_SEQ, D_MODEL),
            *p_cache, *p_state, *s_cache, *s_state)
```

```python
import functools
import math

import jax
import jax.numpy as jnp
from jax import lax
from jax.experimental import pallas as pl
from jax.experimental.pallas import tpu as pltpu

F32 = jnp.float32
BF16 = jnp.bfloat16

D_MODEL = 1024
BATCH = 4
SEQ = 4096
DEPTH = 2
DEC_BATCH = 128
DEC_SEQ = 8
PAST_LEN = 16384
HEAD_DIM = 64
N_HEADS_A = 8
DILATIONS = (1, 4, 16)
WIN_A = 2048
N_HEADS_B = 8
N_KV_B = 2
WIN_B = 128
ROPE_THETA = 10000.0
D_A = N_HEADS_A * HEAD_DIM
D_BQ = N_HEADS_B * HEAD_DIM
D_BKV = N_KV_B * HEAD_DIM
D_IN_ATTN = 3 * D_A + D_BQ + 2 * D_BKV
SSM_GROUP = 16
N_SSM_GROUPS = D_MODEL // SSM_GROUP
SSM_STATE = 64
N_STATE = N_SSM_GROUPS * SSM_STATE
D_FF = 2816
DN_ALPHA = (2 * DEPTH) ** 0.25
FFN_RES = 0.5
LN_EPS = 1e-5
ATTN_SCALE = HEAD_DIM ** -0.5
LOG2E = math.log2(math.e)
LSE_LANES = 16

LANES = 128
SUBLANES = 8
MXU_N = 256
VMEM_LIMIT = 56 * 1024 * 1024

ROW_TILE = 512
FFN_ROW_TILE = 1024
FF_CHUNK = MXU_N
TQ = 128
N_SEG = SUBLANES
SEG_LEN = SEQ // N_SEG
N_LCHUNK = D_MODEL // LANES
ST_CHUNK = N_STATE // N_LCHUNK
NK_PAD = WIN_A + LANES
NKB_PAD = 2 * WIN_B

NEG_INF = float("-inf")


def _params(n_axes, vmem=VMEM_LIMIT):
    return pltpu.CompilerParams(dimension_semantics=("arbitrary",) * n_axes, vmem_limit_bytes=vmem)


def _resident(shape):
    return pl.BlockSpec(shape, lambda *_: (0,) * len(shape), pipeline_mode=pl.Buffered(1))


def _layer_norm(x, g, b):
    mu = jnp.mean(x, -1, keepdims=True)
    xc = x - mu
    var = jnp.mean(xc * xc, -1, keepdims=True)
    return xc * lax.rsqrt(var + LN_EPS) * g + b


def _dot(a, b):
    return jnp.dot(a, b, preferred_element_type=F32)


def _dot_nt(a, b):
    return lax.dot_general(a, b, (((1,), (1,)), ((), ())), preferred_element_type=F32)


def _ffn_kernel(x_ref, wg_ref, wu_ref, wd_ref, g_ref, b_ref, o_ref, h_ref):
    x = x_ref[...]
    xb = x.astype(BF16)
    for c in range(D_FF // FF_CHUNK):
        sl = slice(c * FF_CHUNK, (c + 1) * FF_CHUNK)
        gate = _dot(xb, wg_ref[:, sl])
        up = _dot(xb, wu_ref[:, sl])
        h_ref[:, sl] = (gate * jax.nn.sigmoid(gate) * up).astype(BF16)
    y = DN_ALPHA * x + FFN_RES * _dot(h_ref[...], wd_ref[...])
    o_ref[...] = _layer_norm(y, g_ref[...], b_ref[...])


def _ffn(x, wg, wu, wd, g, b, layer=0, which=0):
    n = x.shape[0]
    tm = min(FFN_ROW_TILE, n)
    row = pl.BlockSpec((tm, D_MODEL), lambda i: (i, 0))
    if wg.ndim == 4:
        pick = lambda r, c: pl.BlockSpec((None, None, r, c), lambda i: (layer, which, 0, 0),
                                         pipeline_mode=pl.Buffered(1))
    else:
        pick = lambda r, c: _resident((r, c))
    return pl.pallas_call(
        _ffn_kernel,
        grid=(n // tm,),
        in_specs=[row, pick(D_MODEL, D_FF), pick(D_MODEL, D_FF), pick(D_FF, D_MODEL),
                  _resident((1, D_MODEL)), _resident((1, D_MODEL))],
        out_specs=row,
        out_shape=jax.ShapeDtypeStruct((n, D_MODEL), F32),
        scratch_shapes=[pltpu.VMEM((tm, D_FF), BF16)],
        compiler_params=_params(1),
        name="ffn",
    )(x, wg, wu, wd, g, b)


def _ffn_pair_kernel(n_first, xp_ref, xs_ref, wg_ref, wu_ref, wd_ref, g_ref, b_ref, op_ref, os_ref, h_ref):
    step = pl.program_id(0)
    weights = (wg_ref, wu_ref, wd_ref, g_ref, b_ref)
    pl.when(step < n_first)(functools.partial(_ffn_kernel, xp_ref, *weights, op_ref, h_ref))
    pl.when(step >= n_first)(functools.partial(_ffn_kernel, xs_ref, *weights, os_ref, h_ref))


def _ffn_pair(xp, xs, wg, wu, wd, g, b, layer, which):
    tm = ROW_TILE
    n_first, n_second = xp.shape[0] // tm, xs.shape[0] // tm
    first = pl.BlockSpec((tm, D_MODEL), lambda i: (jnp.minimum(i, n_first - 1), 0))
    second = pl.BlockSpec((tm, D_MODEL), lambda i: (jnp.maximum(i - n_first, 0), 0))
    pick = lambda r, c: pl.BlockSpec((None, None, r, c), lambda i: (layer, which, 0, 0), pipeline_mode=pl.Buffered(1))
    return pl.pallas_call(
        functools.partial(_ffn_pair_kernel, n_first),
        grid=(n_first + n_second,),
        in_specs=[first, second, pick(D_MODEL, D_FF), pick(D_MODEL, D_FF), pick(D_FF, D_MODEL),
                  _resident((1, D_MODEL)), _resident((1, D_MODEL))],
        out_specs=[first, second],
        out_shape=[jax.ShapeDtypeStruct(xp.shape, F32), jax.ShapeDtypeStruct(xs.shape, F32)],
        scratch_shapes=[pltpu.VMEM((tm, D_FF), BF16)],
        compiler_params=_params(1),
        name="ffn_pair",
    )(xp, xs, wg, wu, wd, g, b)


def _rope_tables(pos):
    half = HEAD_DIM // 2
    inv_freq = ROPE_THETA ** (-jnp.arange(half, dtype=F32) / half)
    ang = pos.astype(F32)[:, None] * inv_freq[None, :]
    cos, sin = jnp.cos(ang), jnp.sin(ang)
    cos_t = jnp.concatenate([cos, cos, cos, cos], -1)
    sin_t = jnp.concatenate([-sin, sin, -sin, sin], -1)
    return cos_t, sin_t


def _attn_proj_kernel(tiles_per_seq, x_ref, w_ref, cos_ref, sin_ref, qa_ref, ka_ref, va_ref, qb_ref, kb_ref, vb_ref,
                      *extra):
    xb = x_ref[...].astype(BF16)
    slab_ref = extra[-1] if extra else None
    dilated = extra[:6]
    n_chunks = D_A // LANES
    tm = x_ref.shape[0]

    def keep(tensor, c, val):
        if slab_ref is not None:
            slab_ref[tensor * n_chunks + c] = val
    cos = cos_ref[...]
    sin = sin_ref[...]
    lane = lax.broadcasted_iota(jnp.int32, cos.shape, 1)
    first_half = (lane & (HEAD_DIM // 2)) == 0

    def rope(z):
        rot = jnp.where(first_half, pltpu.roll(z, LANES - HEAD_DIM // 2, 1), pltpu.roll(z, HEAD_DIM // 2, 1))
        return z * cos + rot * sin

    def project(col0, ncols):
        return _dot(xb, w_ref[:, col0:col0 + ncols])

    def rope_chunks(z):
        return [rope(z[:, c * LANES:(c + 1) * LANES]) for c in range(z.shape[1] // LANES)]

    q_scale = ATTN_SCALE * LOG2E if extra else ATTN_SCALE
    col = 0
    for c, r in enumerate(rope_chunks(project(col, D_A))):
        r = r * q_scale
        qa_ref[:, c * LANES:(c + 1) * LANES] = r.astype(qa_ref.dtype)
        keep(0, c, r)
    col += D_A
    for c, r in enumerate(rope_chunks(project(col, D_A))):
        ka_ref[:, c * LANES:(c + 1) * LANES] = r.astype(ka_ref.dtype)
        keep(1, c, r)
    col += D_A
    z = project(col, D_A)
    va_ref[...] = z.astype(va_ref.dtype)
    for c in range(n_chunks):
        keep(2, c, z[:, c * LANES:(c + 1) * LANES])
    col += D_A
    if extra:
        kat_ref, vat_ref, kbt_ref, vbt_ref = extra[6:10]
        tile_in_seq = pl.program_id(0) % tiles_per_seq

        @pl.when(tile_in_seq >= tiles_per_seq - WIN_A // tm)
        def _():
            for c in range(n_chunks):
                kat_ref[c * LANES:(c + 1) * LANES, :] = slab_ref[n_chunks + c].T
                vat_ref[c * LANES:(c + 1) * LANES, :] = slab_ref[2 * n_chunks + c].T

        slab4_ref = extra[-2]
        d1, d2 = DILATIONS[1], DILATIONS[2] // DILATIONS[1]
        plane = tm // d1
        for tensor in range(3):
            out1_ref, out2_ref = dilated[tensor], dilated[3 + tensor]
            for c in range(n_chunks):
                idx = tensor * n_chunks + c
                lanes = slice(c * LANES, (c + 1) * LANES)
                for r in range(d1):
                    rows = slab_ref[idx, pl.ds(r, plane, stride=d1), :]
                    slab4_ref[idx, r * plane:(r + 1) * plane, :] = rows
                    out1_ref[r, :, lanes] = rows.astype(out1_ref.dtype)
                for r in range(d1):
                    for m in range(d2):
                        rows = slab4_ref[idx, pl.ds(r * plane + m, plane // d2, stride=d2), :]
                        out2_ref[r + d1 * m, :, lanes] = rows.astype(out2_ref.dtype)
    for c, r in enumerate(rope_chunks(project(col, D_BQ))):
        qb_ref[:, c * LANES:(c + 1) * LANES] = (r * q_scale).astype(qb_ref.dtype)
    col += D_BQ
    z = project(col, 2 * D_BKV)
    r = rope(z[:, :D_BKV])
    if extra:
        lo = _lane_lo(r.shape)
        for ref, val in ((kb_ref, r), (vb_ref, z[:, D_BKV:])):
            swapped = pltpu.roll(val, HEAD_DIM, 1)
            ref[:, 0:LANES] = jnp.where(lo, val, swapped).astype(ref.dtype)
            ref[:, LANES:] = jnp.where(lo, swapped, val).astype(ref.dtype)
    else:
        kb_ref[...] = r.astype(kb_ref.dtype)
        vb_ref[...] = z[:, D_BKV:].astype(vb_ref.dtype)
    if extra:
        @pl.when(tile_in_seq == tiles_per_seq - 1)
        def _():
            kbt_ref[...] = r[tm - WIN_B:, :].T
            vbt_ref[...] = z[tm - WIN_B:, D_BKV:].T


def _attn_proj(x, w, cos_t, sin_t, table_block, act_dtype, prompt_seq=None):
    n = x.shape[0]
    tm = min(ROW_TILE, n)

    def row(width):
        return pl.BlockSpec((tm, width), lambda i: (i, 0))

    tab = pl.BlockSpec((tm, LANES), lambda i: (table_block(i), 0))
    kv_b = D_BKV if prompt_seq is None else N_KV_B * LANES
    widths = (D_A, D_A, D_A, D_BQ, kv_b, kv_b)
    out_shape = [jax.ShapeDtypeStruct((n, wd), act_dtype) for wd in widths]
    out_specs = [row(wd) for wd in widths]
    scratch = []
    tps = None
    if prompt_seq is not None:
        bsz, seq = prompt_seq
        tps = seq // tm
        for dil in DILATIONS[1:]:
            out_shape += [jax.ShapeDtypeStruct((bsz, dil, seq // dil, D_A), BF16)] * 3
            out_specs += [pl.BlockSpec((None, dil, tm // dil, D_A), lambda i: (i // tps, 0, i % tps, 0))] * 3
        first_tail = tps - WIN_A // tm
        out_shape += [jax.ShapeDtypeStruct((bsz, D_A, WIN_A), F32)] * 2
        out_specs += [pl.BlockSpec((None, D_A, tm), lambda i: (i // tps, 0, jnp.maximum(i % tps - first_tail, 0)))] * 2
        out_shape += [jax.ShapeDtypeStruct((bsz, D_BKV, WIN_B), F32)] * 2
        out_specs += [pl.BlockSpec((None, D_BKV, WIN_B), lambda i: (i // tps, 0, 0))] * 2
        assert DILATIONS[2] == DILATIONS[1] ** 2
        scratch = [pltpu.VMEM((3 * D_A // LANES, tm, LANES), F32)] * 2
    return pl.pallas_call(
        functools.partial(_attn_proj_kernel, tps),
        grid=(n // tm,),
        in_specs=[row(D_MODEL), _resident((D_MODEL, D_IN_ATTN)), tab, tab],
        out_specs=out_specs,
        out_shape=out_shape,
        scratch_shapes=scratch,
        compiler_params=_params(1),
        name="attn_proj",
    )(x, w, cos_t, sin_t)


def _lane_lo(shape):
    return lax.broadcasted_iota(jnp.int32, shape, 1) < HEAD_DIM


def _half_masks_bf16():
    lo = jnp.where(_lane_lo((1, LANES)), 1.0, 0.0).astype(BF16)
    return lo, 1 - lo


def _band_masks(n_heads, t, sub):
    row = lax.broadcasted_iota(jnp.int32, (n_heads * TQ, TQ), 0) & (TQ - 1)
    col = lax.broadcasted_iota(jnp.int32, (n_heads * TQ, TQ), 1)
    shift = jnp.where(t > 0, 0, TQ) if sub == 0 else 0
    return col <= row, col >= row + shift


def _sub_tile_kv(sub, sl, kc_ref, kp_ref, vc_ref, vp_ref):
    if sub == 0:
        return kc_ref[0:TQ, sl], kp_ref[:, sl], vc_ref[0:TQ, sl], vp_ref[:, sl]
    return kc_ref[TQ:2 * TQ, sl], kc_ref[0:TQ, sl], vc_ref[TQ:2 * TQ, sl], vc_ref[0:TQ, sl]


def _band_softmax(qs, kc, kp, vc, vp, mask_c, mask_p):
    s_c = jnp.where(mask_c, _dot_nt(qs, kc), NEG_INF)
    s_p = jnp.where(mask_p, _dot_nt(qs, kp), NEG_INF)
    m = jnp.max(jnp.maximum(s_c, s_p), -1, keepdims=True)
    p_c = jnp.exp2(s_c - m)
    p_p = jnp.exp2(s_p - m)
    den = jnp.sum(p_c + p_p, -1, keepdims=True)
    acc = _dot(p_c.astype(BF16), vc) + _dot(p_p.astype(BF16), vp)
    return acc * (1.0 / den), m, den


def _band_a_kernel(q_ref, kc_ref, kp_ref, vc_ref, vp_ref, o_ref, lse_ref):
    t = pl.program_id(2)
    lo = _lane_lo((TQ, LANES))
    lo_bf, hi_bf = _half_masks_bf16()
    lane_head = jnp.right_shift(lax.broadcasted_iota(jnp.int32, (TQ, LANES), 1), int(math.log2(LSE_LANES)))
    for sub in range(2):
        rows = slice(sub * TQ, (sub + 1) * TQ)
        mask_c, mask_p = _band_masks(2, t, sub)
        lse_tile = jnp.zeros((TQ, LANES), F32)
        for c in range(D_A // LANES):
            sl = slice(c * LANES, (c + 1) * LANES)
            kc, kp, vc, vp = _sub_tile_kv(sub, sl, kc_ref, kp_ref, vc_ref, vp_ref)
            q2 = q_ref[rows, sl]
            qs = jnp.concatenate([q2 * lo_bf, q2 * hi_bf], axis=0)
            out, m, den = _band_softmax(qs, kc, kp, vc, vp, mask_c, mask_p)
            lse = m + jnp.log2(den)
            o_ref[rows, sl] = jnp.where(lo, out[0:TQ], out[TQ:]).astype(o_ref.dtype)
            lse_tile = jnp.where(lane_head == 2 * c, lse[0:TQ], lse_tile)
            lse_tile = jnp.where(lane_head == 2 * c + 1, lse[TQ:], lse_tile)
        lse_ref[rows, :] = lse_tile


def _band_a(q, k, v):
    bsz, dil, sub, _ = q.shape
    cur = pl.BlockSpec((None, None, 2 * TQ, D_A), lambda b, r, t: (b, r, t, 0))
    prev = pl.BlockSpec((None, None, TQ, D_A), lambda b, r, t: (b, r, jnp.maximum(2 * t - 1, 0), 0))
    lse = pl.BlockSpec((None, None, 2 * TQ, LANES), lambda b, r, t: (b, r, t, 0))
    return pl.pallas_call(
        _band_a_kernel,
        grid=(bsz, dil, sub // (2 * TQ)),
        in_specs=[cur, cur, prev, cur, prev],
        out_specs=[cur, lse],
        out_shape=[jax.ShapeDtypeStruct((bsz, dil, sub, D_A), BF16),
                   jax.ShapeDtypeStruct((bsz, dil, sub, LANES), F32)],
        compiler_params=_params(3),
        name=f"band_a_d{dil}",
    )(q, k, k, v, v)


def _band_b_body(first_tile, sink_ref, q_ref, kc_ref, kp_ref, vc_ref, vp_ref, o_ref):
    group = N_HEADS_B // N_KV_B
    lo = _lane_lo((TQ, LANES))
    lo_bf, hi_bf = _half_masks_bf16()
    row = lax.broadcasted_iota(jnp.int32, (group * TQ, TQ), 0) & (TQ - 1)
    col = lax.broadcasted_iota(jnp.int32, (group * TQ, TQ), 1)
    in_cur = col <= row
    cur_bf = jnp.where(in_cur, 1.0, 0.0).astype(BF16)
    prev_bf = 1 - cur_bf
    for sub in range(2):
        rows = slice(sub * TQ, (sub + 1) * TQ)
        for g in range(N_KV_B):
            sl = slice(g * LANES, (g + 1) * LANES)
            kc, kp, vc, vp = _sub_tile_kv(sub, sl, kc_ref, kp_ref, vc_ref, vp_ref)
            heads = range(g * group, (g + 1) * group)
            qs = jnp.concatenate(
                [q_ref[rows, (h // 2) * LANES:(h // 2 + 1) * LANES] * (lo_bf if h % 2 == 0 else hi_bf) for h in heads],
                axis=0)
            sink = jnp.concatenate([jnp.full((TQ, 1), sink_ref[h] * LOG2E, F32) for h in heads], axis=0)
            only_cur = first_tile and sub == 0
            s = jnp.where(in_cur, _dot_nt(qs, kc), NEG_INF if only_cur else _dot_nt(qs, kp))
            m = jnp.maximum(jnp.max(s, -1, keepdims=True), sink)
            p = jnp.exp2(s - m)
            den = jnp.sum(p, -1, keepdims=True) + jnp.exp2(sink - m)
            pb = p.astype(BF16)
            acc = _dot(pb, vc) if only_cur else _dot(pb * cur_bf, vc) + _dot(pb * prev_bf, vp)
            out = acc * (1.0 / den)
            for i in range(group // 2):
                c = g * (group // 2) + i
                even, odd = out[2 * i * TQ:(2 * i + 1) * TQ], out[(2 * i + 1) * TQ:(2 * i + 2) * TQ]
                o_ref[rows, c * LANES:(c + 1) * LANES] = jnp.where(lo, even, odd).astype(o_ref.dtype)


def _band_b_kernel(*refs):
    t = pl.program_id(1)
    pl.when(t == 0)(functools.partial(_band_b_body, True, *refs))
    pl.when(t > 0)(functools.partial(_band_b_body, False, *refs))


def _band_b(q, k, v, sinks):
    bsz, seq, _ = q.shape
    kv_lanes = N_KV_B * LANES
    qs = pl.BlockSpec((None, 2 * TQ, D_BQ), lambda b, t: (b, t, 0))
    cur = pl.BlockSpec((None, 2 * TQ, kv_lanes), lambda b, t: (b, t, 0))
    prev = pl.BlockSpec((None, TQ, kv_lanes), lambda b, t: (b, jnp.maximum(2 * t - 1, 0), 0))
    o = pl.pallas_call(
        _band_b_kernel,
        grid=(bsz, seq // (2 * TQ)),
        in_specs=[pl.BlockSpec(memory_space=pltpu.SMEM), qs, cur, prev, cur, prev],
        out_specs=qs,
        out_shape=jax.ShapeDtypeStruct((bsz, seq, D_BQ), BF16),
        compiler_params=_params(2),
        name="band_b",
    )(sinks, q, k, k, v, v)
    return o.reshape(bsz * seq, D_BQ)


def _pattern_count(dist):
    cnt = jnp.zeros(dist.shape, F32)
    for dil in DILATIONS:
        cnt = cnt + ((dist >= 0) & (dist <= 128 * dil) & (dist % dil == 0)).astype(F32)
    return cnt


def _sample_tables():
    i = jnp.arange(DEC_SEQ)
    cnt_c = _pattern_count(WIN_A + i[:, None] - jnp.arange(WIN_A)[None, :])
    j = jnp.arange(LANES)
    cnt_n = jnp.where(j[None, :] < DEC_SEQ, _pattern_count(i[:, None] - j[None, :]), 0.0)
    cnt_n = jnp.tile(cnt_n, (N_HEADS_A, 1))
    jb = jnp.arange(NKB_PAD)[None, :]
    dist_b = WIN_B + i[:, None] - jb
    ok_b = (dist_b >= 0) & (dist_b < WIN_B) & (jb < WIN_B + DEC_SEQ)
    mask_b = jnp.tile(ok_b.astype(F32), (N_HEADS_B, 1))
    return cnt_c, cnt_n, mask_b


def _sample_attend(q, kan, van, kt_ref, vt_ref, cnt_c, cnt_n, qb, kbn, vbn, kbc, vbc, mask_b, sink_col, kb_s, vb_s):
    rows = N_HEADS_A * DEC_SEQ
    q_rep = jnp.concatenate([q] * N_HEADS_A, axis=0)
    row_head = jnp.right_shift(lax.broadcasted_iota(jnp.int32, (rows, D_A), 0), int(math.log2(DEC_SEQ)))
    lane_head = jnp.right_shift(lax.broadcasted_iota(jnp.int32, (rows, D_A), 1), int(math.log2(HEAD_DIM)))
    own = row_head == lane_head
    q_bd = jnp.where(own, q_rep, 0.0).astype(BF16)
    pad = jnp.zeros((LANES - DEC_SEQ, D_A), F32)
    kn = jnp.concatenate([kan, pad], 0).astype(BF16)
    vn = jnp.concatenate([van, pad], 0).astype(BF16)
    s_new = jnp.where(cnt_n > 0.0, _dot_nt(q_bd, kn), NEG_INF)
    outs, p_new = [], []
    for h in range(N_HEADS_A):
        head_rows = slice(h * DEC_SEQ, (h + 1) * DEC_SEQ)
        q_h = q[:, h * HEAD_DIM:(h + 1) * HEAD_DIM].astype(BF16)
        s_c = jnp.where(cnt_c > 0.0, _dot(q_h, kt_ref[h].astype(BF16)), NEG_INF)
        s_n = s_new[head_rows]
        m = jnp.maximum(jnp.max(s_c, -1, keepdims=True), jnp.max(s_n, -1, keepdims=True))
        p_c = jnp.exp(s_c - m) * cnt_c
        p_n = jnp.exp(s_n - m) * cnt_n[head_rows]
        inv = 1.0 / (jnp.sum(p_c, -1, keepdims=True) + jnp.sum(p_n, -1, keepdims=True))
        outs.append(_dot_nt(p_c.astype(BF16), vt_ref[h].astype(BF16)) * inv)
        p_new.append(p_n * inv)
    out_n = jnp.where(own, _dot(jnp.concatenate(p_new, axis=0).astype(BF16), vn), 0.0)
    oa = jnp.concatenate(outs, axis=1)
    for h in range(N_HEADS_A):
        oa = oa + out_n[h * DEC_SEQ:(h + 1) * DEC_SEQ]

    n_pad_b = NKB_PAD - WIN_B - DEC_SEQ
    pad_b = jnp.zeros((n_pad_b, D_BKV), F32)
    kb_s[...] = jnp.concatenate([kbc, kbn, pad_b], 0).astype(BF16)
    vb_s[...] = jnp.concatenate([vbc, vbn, pad_b], 0).astype(BF16)
    lo8 = _lane_lo((DEC_SEQ, LANES))
    group = N_HEADS_B // N_KV_B
    pieces = []
    for h in range(N_HEADS_B):
        chunk = qb[:, (h // 2) * LANES:(h // 2 + 1) * LANES]
        g = h // group
        if h % 2 != g:
            chunk = pltpu.roll(chunk, HEAD_DIM, 1)
        pieces.append(jnp.where(lo8 if g == 0 else jnp.logical_not(lo8), chunk, 0.0))
    qb_bd = jnp.concatenate(pieces, axis=0).astype(BF16)
    sb = jnp.where(mask_b > 0.0, _dot_nt(qb_bd, kb_s[...]), NEG_INF)
    sink = sink_col[:, 0:1]
    mb = jnp.maximum(jnp.max(sb, -1, keepdims=True), sink)
    pb = jnp.exp(sb - mb) * mask_b
    den_b = jnp.sum(pb, -1, keepdims=True) + jnp.exp(sink - mb)
    ob_full = _dot(pb.astype(BF16), vb_s[...]) * (1.0 / den_b)
    ob = []
    for c in range(D_BQ // LANES):
        halves = []
        for half in range(2):
            h = 2 * c + half
            piece = ob_full[h * DEC_SEQ:(h + 1) * DEC_SEQ]
            if half != h // group:
                piece = pltpu.roll(piece, HEAD_DIM, 1)
            halves.append(piece)
        ob.append(jnp.where(lo8, halves[0], halves[1]))
    return jnp.concatenate([oa] + ob, axis=1)


FUSED_ROW_TILE = 512
SEQ_PER_STEP = DEC_BATCH // (BATCH * SEQ // FUSED_ROW_TILE)


def _kv_copies(kt_hbm, vt_hbm, kbuf, vbuf, sems, seq, slot):
    return (pltpu.make_async_copy(kt_hbm.at[seq], kbuf.at[slot], sems.at[0, slot]),
            pltpu.make_async_copy(vt_hbm.at[seq], vbuf.at[slot], sems.at[1, slot]))


def _ffn_attn_kernel(x_ref, wg_ref, wu_ref, wd_ref, g_ref, b_ref,
                     qa_ref, kan_ref, van_ref, kt_hbm, vt_hbm, cnt_c_ref, cnt_n_ref,
                     qb_ref, kbn_ref, vbn_ref, kbc_ref, vbc_ref, maskb_ref, sinkcol_ref,
                     y_ref, o_ref, h_ref, kbuf, vbuf, sems, kb_s, vb_s):
    step = pl.program_id(0)
    n_steps = pl.num_programs(0)
    copies = functools.partial(_kv_copies, kt_hbm, vt_hbm, kbuf, vbuf, sems)

    @pl.when(step == 0)
    def _():
        for cp in copies(0, 0):
            cp.start()

    x = x_ref[...]
    xb = x.astype(BF16)
    n_ff = D_FF // FF_CHUNK
    per_seq = -(-n_ff // SEQ_PER_STEP)
    cnt_c, cnt_n, mask_b, sink_col = cnt_c_ref[...], cnt_n_ref[...], maskb_ref[...], sinkcol_ref[...]
    for s in range(SEQ_PER_STEP):
        slot = s % 2
        seq = step * SEQ_PER_STEP + s
        for cp in copies(seq, slot):
            cp.wait()
        if s + 1 < SEQ_PER_STEP:
            for cp in copies(seq + 1, 1 - slot):
                cp.start()
        else:
            @pl.when(step + 1 < n_steps)
            def _():
                for cp in copies(seq + 1, 1 - slot):
                    cp.start()
        rows = slice(s * DEC_SEQ, (s + 1) * DEC_SEQ)
        kbc = jnp.concatenate([kbc_ref[s, g].T for g in range(N_KV_B)], axis=1)
        vbc = jnp.concatenate([vbc_ref[s, g].T for g in range(N_KV_B)], axis=1)
        o_ref[rows, :] = _sample_attend(
            qa_ref[rows, :], kan_ref[rows, :], van_ref[rows, :], kbuf.at[slot], vbuf.at[slot], cnt_c, cnt_n,
            qb_ref[rows, :], kbn_ref[rows, :], vbn_ref[rows, :], kbc, vbc, mask_b, sink_col, kb_s, vb_s)
        for c in range(s * per_seq, min((s + 1) * per_seq, n_ff)):
            sl = slice(c * FF_CHUNK, (c + 1) * FF_CHUNK)
            gate = _dot(xb, wg_ref[:, sl])
            up = _dot(xb, wu_ref[:, sl])
            h_ref[:, sl] = (gate * jax.nn.sigmoid(gate) * up).astype(BF16)
    y = DN_ALPHA * x + FFN_RES * _dot(h_ref[...], wd_ref[...])
    y_ref[...] = _layer_norm(y, g_ref[...], b_ref[...])


def _ffn_with_sample_attention(x, wg, wu, wd, g, b, layer, which,
                               qa, kan, van, cache_ak, cache_av, qb, kbn, vbn, cache_bk, cache_bv, sinks):
    n = x.shape[0]
    tm = FUSED_ROW_TILE
    assert n // tm * SEQ_PER_STEP == DEC_BATCH and SEQ_PER_STEP % 2 == 0
    cnt_c, cnt_n, mask_b = _sample_tables()
    sink_col = jnp.broadcast_to(jnp.repeat(sinks.astype(F32), DEC_SEQ)[:, None], (N_HEADS_B * DEC_SEQ, LANES))
    row = pl.BlockSpec((tm, D_MODEL), lambda i: (i, 0))
    pick = lambda r, c: pl.BlockSpec((None, None, r, c), lambda i: (layer, which, 0, 0), pipeline_mode=pl.Buffered(1))
    new = lambda width: pl.BlockSpec((SEQ_PER_STEP * DEC_SEQ, width), lambda i: (i, 0))
    cache_b = pl.BlockSpec((SEQ_PER_STEP, N_KV_B, HEAD_DIM, WIN_B), lambda i: (i, 0, 0, 0))
    hbm = pl.BlockSpec(memory_space=pl.ANY)
    rows = N_HEADS_A * DEC_SEQ
    kv_slot = (2, N_HEADS_A, HEAD_DIM, WIN_A)
    return pl.pallas_call(
        _ffn_attn_kernel,
        grid=(n // tm,),
        in_specs=[row, pick(D_MODEL, D_FF), pick(D_MODEL, D_FF), pick(D_FF, D_MODEL),
                  _resident((1, D_MODEL)), _resident((1, D_MODEL)),
                  new(D_A), new(D_A), new(D_A), hbm, hbm, _resident(cnt_c.shape), _resident(cnt_n.shape),
                  new(D_BQ), new(D_BKV), new(D_BKV), cache_b, cache_b,
                  _resident((rows, NKB_PAD)), _resident((rows, LANES))],
        out_specs=[row, new(D_A + D_BQ)],
        out_shape=[jax.ShapeDtypeStruct((n, D_MODEL), F32),
                   jax.ShapeDtypeStruct((DEC_BATCH * DEC_SEQ, D_A + D_BQ), F32)],
        scratch_shapes=[pltpu.VMEM((tm, D_FF), BF16), pltpu.VMEM(kv_slot, F32), pltpu.VMEM(kv_slot, F32),
                        pltpu.SemaphoreType.DMA((2, 2)),
                        pltpu.VMEM((NKB_PAD, D_BKV), BF16), pltpu.VMEM((NKB_PAD, D_BKV), BF16)],
        compiler_params=_params(1, vmem=60 * 1024 * 1024),
        name="ffn_attn_sample",
    )(x, wg, wu, wd, g, b, qa, kan, van, cache_ak, cache_av, cnt_c, cnt_n, qb, kbn, vbn, cache_bk, cache_bv,
      mask_b, sink_col)


def _attn_out_prompt_kernel(o1_ref, l1_ref, o4_ref, l4_ref, o16_ref, l16_ref, ob_ref, y_ref, w_ref, g_ref, b_ref,
                            out_ref, slab_ref, oa_ref):
    tm = y_ref.shape[0]
    n_chunks = D_A // LANES
    slabs = {}
    base = 0
    for name, dil, src, width in (("o4", 4, o4_ref, n_chunks), ("l4", 4, l4_ref, 1),
                                  ("o16", 16, o16_ref, n_chunks), ("l16", 16, l16_ref, 1)):
        slabs[name] = base
        for r in range(dil):
            for c in range(width):
                piece = src[r, :, c * LANES:(c + 1) * LANES]
                slab_ref[base + c, pl.ds(r, tm // dil, stride=dil), :] = piece.astype(F32)
        base += width
    l1, l4, l16 = l1_ref[...], slab_ref[slabs["l4"]], slab_ref[slabs["l16"]]
    m = jnp.maximum(jnp.maximum(l1, l4), l16)
    e1, e4, e16 = jnp.exp2(l1 - m), jnp.exp2(l4 - m), jnp.exp2(l16 - m)
    inv = 1.0 / (e1 + e4 + e16)
    row = lax.broadcasted_iota(jnp.int32, (LANES, D_A), 0)
    head_of_col = jnp.right_shift(lax.broadcasted_iota(jnp.int32, (LANES, D_A), 1), int(math.log2(HEAD_DIM)))
    spread = jnp.where(row == head_of_col * LSE_LANES, 1.0, 0.0).astype(BF16)

    def per_head_lanes(w):
        hi = w.astype(BF16)
        lo = (w - hi.astype(F32)).astype(BF16)
        return _dot(hi, spread) + _dot(lo, spread)

    w1, w4, w16 = per_head_lanes(e1 * inv), per_head_lanes(e4 * inv), per_head_lanes(e16 * inv)
    for c in range(n_chunks):
        sl = slice(c * LANES, (c + 1) * LANES)
        oa = (w1[:, sl] * o1_ref[:, sl].astype(F32) + w4[:, sl] * slab_ref[slabs["o4"] + c]
              + w16[:, sl] * slab_ref[slabs["o16"] + c])
        oa_ref[:, sl] = oa.astype(BF16)
    mix = _dot(oa_ref[...], w_ref[0:D_A, :]) + _dot(ob_ref[...], w_ref[D_A:, :])
    out_ref[...] = _layer_norm(DN_ALPHA * y_ref[...] + mix, g_ref[...], b_ref[...])


def _attn_out_prompt(pats, ob, y, w, g, b, seq):
    n = y.shape[0]
    tm = ROW_TILE
    tps = seq // tm
    half = pl.BlockSpec((tm, D_A), lambda i: (i, 0))
    full = pl.BlockSpec((tm, D_MODEL), lambda i: (i, 0))
    lse1 = pl.BlockSpec((tm, LANES), lambda i: (i, 0))
    planes = lambda dil, width: pl.BlockSpec((None, dil, tm // dil, width), lambda i: (i // tps, 0, i % tps, 0))
    (o1, l1), (o4, l4), (o16, l16) = pats
    return pl.pallas_call(
        _attn_out_prompt_kernel,
        grid=(n // tm,),
        in_specs=[half, lse1, planes(4, D_A), planes(4, LANES), planes(16, D_A), planes(16, LANES), half, full,
                  _resident((D_MODEL, D_MODEL)), _resident((1, D_MODEL)), _resident((1, D_MODEL))],
        out_specs=full,
        out_shape=jax.ShapeDtypeStruct((n, D_MODEL), F32),
        scratch_shapes=[pltpu.VMEM((2 * (D_A // LANES + 1), tm, LANES), F32), pltpu.VMEM((tm, D_A), BF16)],
        compiler_params=_params(1),
        name="attn_out_prompt",
    )(o1, l1, o4, l4, o16, l16, ob, y, w, g, b)


def _mix_out_kernel(o_ref, y_ref, w_ref, g_ref, b_ref, out_ref):
    mix = _dot(o_ref[...].astype(BF16), w_ref[...])
    out_ref[...] = _layer_norm(DN_ALPHA * y_ref[...] + mix, g_ref[...], b_ref[...])


def _attn_out_sample(o, y, w, g, b):
    n = y.shape[0]
    tm = min(ROW_TILE, n)
    full = pl.BlockSpec((tm, D_MODEL), lambda i: (i, 0))
    return pl.pallas_call(
        _mix_out_kernel,
        grid=(n // tm,),
        in_specs=[full, full, _resident((D_MODEL, D_MODEL)), _resident((1, D_MODEL)), _resident((1, D_MODEL))],
        out_specs=full,
        out_shape=jax.ShapeDtypeStruct((n, D_MODEL), F32),
        compiler_params=_params(1),
        name="attn_out_sample",
    )(o, y, w, g, b)


def _ssm_discretize(lam_re, lam_im, log_dt, b_re, b_im):
    dt = jnp.exp(log_dt.astype(F32))[:, None]
    lr, li = lam_re.astype(F32), lam_im.astype(F32)
    mag = jnp.exp(lr * dt)
    ab_re, ab_im = mag * jnp.cos(li * dt), mag * jnp.sin(li * dt)
    nr, ni = ab_re - 1.0, ab_im
    den = lr * lr + li * li
    fr, fi = (nr * lr + ni * li) / den, (ni * lr - nr * li) / den
    bb_re = fr[..., None] * b_re - fi[..., None] * b_im
    bb_im = fr[..., None] * b_im + fi[..., None] * b_re
    return ab_re, ab_im, bb_re, bb_im


def _ssm_matrices(bb_re, bb_im, c_re, c_im):
    gpc = LANES // SSM_GROUP
    eye = jnp.eye(gpc, dtype=F32)

    def in_blocks(bb):
        a = bb.reshape(N_LCHUNK, gpc, SSM_STATE, SSM_GROUP)
        return jnp.einsum("jgpn,gh->jgnhp", a, eye).reshape(N_LCHUNK, LANES, ST_CHUNK)

    def out_blocks(cc):
        a = cc.reshape(N_LCHUNK, gpc, SSM_GROUP, SSM_STATE)
        return jnp.einsum("jgnp,gh->jgphn", a, eye).reshape(N_LCHUNK, ST_CHUNK, LANES)

    bmat = jnp.concatenate([in_blocks(bb_re), in_blocks(bb_im)], -1)
    cmat = jnp.concatenate([out_blocks(c_re), -out_blocks(c_im)], 1)
    return bmat, cmat


CHUNK = 8


def _dot3(a, b):
    a_hi, b_hi = a.astype(BF16), b.astype(BF16)
    a_lo, b_lo = (a - a_hi.astype(F32)).astype(BF16), (b - b_hi.astype(F32)).astype(BF16)
    return _dot(a_hi, b_hi) + _dot(a_hi, b_lo) + _dot(a_lo, b_hi)


def _chunk_weights_kernel(ar_ref, ai_ref, acr_ref, aci_ref, bmat_ref, cmat_ref,
                          we_ref, ws_ref, wi_ref, a8r_ref, a8i_ref):
    def powers(r, i, n):
        out = [(jnp.ones_like(r), jnp.zeros_like(r))]
        for _ in range(n):
            out.append(_cmul(out[-1][0], out[-1][1], r, i))
        return out

    row_pow = powers(ar_ref[...], ai_ref[...], CHUNK)
    col_pow = powers(acr_ref[...], aci_ref[...], CHUNK)
    b_re, b_im = bmat_ref[:, 0:ST_CHUNK], bmat_ref[:, ST_CHUNK:]
    c_re, c_im = cmat_ref[0:ST_CHUNK, :], -cmat_ref[ST_CHUNK:, :]

    def scaled_b(power):
        pr, pi = row_pow[power]
        return jnp.concatenate([b_re * pr - b_im * pi, b_re * pi + b_im * pr], axis=1)

    taps = []
    for tau in range(CHUNK):
        sb = scaled_b(tau)
        we_ref[(CHUNK - 1 - tau) * LANES:(CHUNK - tau) * LANES, :] = sb.astype(BF16)
        taps.append(_dot3(sb, cmat_ref[...]).astype(BF16))
    for k in range(CHUNK):
        cols = slice(k * LANES, (k + 1) * LANES)
        qr, qi = col_pow[k + 1]
        ws_ref[0:ST_CHUNK, cols] = (c_re * qr - c_im * qi).astype(BF16)
        ws_ref[ST_CHUNK:, cols] = (-(c_re * qi + c_im * qr)).astype(BF16)
    zero = jnp.zeros((LANES, LANES), BF16)
    for k_in in range(CHUNK):
        for k_out in range(CHUNK):
            wi_ref[k_in * LANES:(k_in + 1) * LANES, k_out * LANES:(k_out + 1) * LANES] = (
                taps[k_out - k_in] if k_out >= k_in else zero)
    a8r_ref[...], a8i_ref[...] = row_pow[CHUNK]


def _chunk_weights(a_re, a_im, bmat, cmat):
    cols = lambda a: jnp.broadcast_to(a.reshape(N_LCHUNK, ST_CHUNK, 1), (N_LCHUNK, ST_CHUNK, LANES))
    per_j = lambda r, c: pl.BlockSpec((None, r, c), lambda j: (j, 0, 0))
    wide = 2 * ST_CHUNK
    return pl.pallas_call(
        _chunk_weights_kernel,
        grid=(N_LCHUNK,),
        in_specs=[per_j(1, ST_CHUNK), per_j(1, ST_CHUNK), per_j(ST_CHUNK, LANES), per_j(ST_CHUNK, LANES),
                  per_j(LANES, wide), per_j(wide, LANES)],
        out_specs=[per_j(CHUNK * LANES, wide), per_j(wide, CHUNK * LANES), per_j(CHUNK * LANES, CHUNK * LANES),
                   per_j(1, ST_CHUNK), per_j(1, ST_CHUNK)],
        out_shape=[jax.ShapeDtypeStruct((N_LCHUNK, CHUNK * LANES, wide), BF16),
                   jax.ShapeDtypeStruct((N_LCHUNK, wide, CHUNK * LANES), BF16),
                   jax.ShapeDtypeStruct((N_LCHUNK, CHUNK * LANES, CHUNK * LANES), BF16),
                   jax.ShapeDtypeStruct((N_LCHUNK, 1, ST_CHUNK), F32),
                   jax.ShapeDtypeStruct((N_LCHUNK, 1, ST_CHUNK), F32)],
        compiler_params=_params(1),
        name="ssm_chunk_weights",
    )(a_re, a_im, cols(a_re), cols(a_im), bmat, cmat)


SEG_TILE = ROW_TILE // N_SEG


def _ssm_in_prompt_kernel(x_ref, w_ref, o_ref):
    x = x_ref[...].reshape(N_SEG * SEG_TILE, D_MODEL)
    u = _dot(x.astype(BF16), w_ref[...])
    for s in range(N_SEG):
        for c in range(N_LCHUNK):
            o_ref[c, pl.ds(s, SEG_TILE, stride=N_SEG), :] = u[s * SEG_TILE:(s + 1) * SEG_TILE, c * LANES:(c + 1) * LANES]


def _ssm_in_prompt(y, w):
    return pl.pallas_call(
        _ssm_in_prompt_kernel,
        grid=(BATCH, SEG_LEN // SEG_TILE),
        in_specs=[pl.BlockSpec((None, N_SEG, SEG_TILE, D_MODEL), lambda b, t: (b, 0, t, 0)),
                  _resident((D_MODEL, D_MODEL))],
        out_specs=pl.BlockSpec((None, N_LCHUNK, ROW_TILE, LANES), lambda b, t: (b, 0, t, 0)),
        out_shape=jax.ShapeDtypeStruct((BATCH, N_LCHUNK, SEQ, LANES), F32),
        compiler_params=_params(2),
        name="ssm_in_prompt",
    )(y.reshape(BATCH, N_SEG, SEG_LEN, D_MODEL), w)


def _ssm_in_sample_kernel(x_ref, w_ref, o_ref, slab_ref):
    u = _dot(x_ref[...].astype(BF16), w_ref[...])
    for c in range(N_LCHUNK):
        slab_ref[c] = u[:, c * LANES:(c + 1) * LANES]
    for l in range(DEC_SEQ):
        for c in range(N_LCHUNK):
            o_ref[c, l * DEC_BATCH:(l + 1) * DEC_BATCH, :] = slab_ref[c, pl.ds(l, DEC_BATCH, stride=DEC_SEQ), :]


def _ssm_in_sample(y, w):
    n = DEC_BATCH * DEC_SEQ
    return pl.pallas_call(
        _ssm_in_sample_kernel,
        grid=(1,),
        in_specs=[_resident((n, D_MODEL)), _resident((D_MODEL, D_MODEL))],
        out_specs=pl.BlockSpec((N_LCHUNK, n, LANES), lambda i: (0, 0, 0)),
        out_shape=jax.ShapeDtypeStruct((N_LCHUNK, n, LANES), F32),
        scratch_shapes=[pltpu.VMEM((N_LCHUNK, n, LANES), F32)],
        compiler_params=_params(1),
        name="ssm_in_sample",
    )(y, w)


def _cmul(ar, ai, br, bi):
    return ar * br - ai * bi, ar * bi + ai * br


def _scan_prompt_kernel(u_ref, we_ref, ws_ref, wi_ref, a8r_ref, a8i_ref, d_ref, h0r_ref, h0i_ref,
                        y_ref, hnr_ref, hni_ref, e_s, hs_s):
    n_chunks = u_ref.shape[0]
    rows = n_chunks * N_SEG
    u_flat = jnp.concatenate([u_ref[:, k].reshape(rows, LANES) for k in range(CHUNK)], axis=1)
    ub = u_flat.astype(BF16)
    e_s[...] = _dot(ub, we_ref[...])
    a8r1, a8i1 = a8r_ref[...], a8i_ref[...]
    a8r = jnp.broadcast_to(a8r1, (N_SEG, ST_CHUNK))
    a8i = jnp.broadcast_to(a8i1, (N_SEG, ST_CHUNK))

    def advance(row, hr, hi):
        er = e_s[pl.ds(row, N_SEG), 0:ST_CHUNK]
        ei = e_s[pl.ds(row, N_SEG), ST_CHUNK:2 * ST_CHUNK]
        return a8r * hr - a8i * hi + er, a8r * hi + a8i * hr + ei

    def pass1(c, carry):
        return advance(pl.multiple_of(c * N_SEG, N_SEG), *carry)

    zero = jnp.zeros((N_SEG, ST_CHUNK), F32)
    er, ei = lax.fori_loop(0, n_chunks, pass1, (zero, zero), unroll=8)

    pr, pi = a8r1, a8i1
    for _ in range(int(math.log2(n_chunks))):
        pr, pi = _cmul(pr, pi, pr, pi)
    hr, hi = h0r_ref[...], h0i_ref[...]
    starts_r, starts_i = [], []
    for s in range(N_SEG):
        starts_r.append(hr)
        starts_i.append(hi)
        gr, gi = _cmul(pr, pi, hr, hi)
        hr, hi = gr + er[s:s + 1], gi + ei[s:s + 1]
    hnr_ref[...] = hr
    hni_ref[...] = hi
    init = (jnp.concatenate(starts_r, 0), jnp.concatenate(starts_i, 0))

    def pass2(c, carry):
        row = pl.multiple_of(c * N_SEG, N_SEG)
        hs_s[pl.ds(row, N_SEG), 0:ST_CHUNK] = carry[0]
        hs_s[pl.ds(row, N_SEG), ST_CHUNK:2 * ST_CHUNK] = carry[1]
        return advance(row, *carry)

    lax.fori_loop(0, n_chunks, pass2, init, unroll=8)
    y = _dot(hs_s[...].astype(BF16), ws_ref[...]) + _dot(ub, wi_ref[...]) + d_ref[...] * u_flat
    for k in range(CHUNK):
        y_ref[:, k] = y[:, k * LANES:(k + 1) * LANES].reshape(n_chunks, N_SEG, LANES)


def _scan_prompt(u, weights, d_skip, h0r, h0i):
    bsz, _, seq, _ = u.shape
    n_chunks = seq // (CHUNK * N_SEG)
    rows = n_chunks * N_SEG
    wide = 2 * ST_CHUNK
    split = lambda a: a.reshape(bsz, N_LCHUNK, n_chunks, CHUNK, N_SEG, LANES)
    chunk = pl.BlockSpec((None, None, n_chunks, CHUNK, N_SEG, LANES), lambda j, b: (b, j, 0, 0, 0, 0))
    per_j = lambda r, c: pl.BlockSpec((None, r, c), lambda j, b: (j, 0, 0))
    state = pl.BlockSpec((None, 1, ST_CHUNK), lambda j, b: (b, 0, j))
    y, hr, hi = pl.pallas_call(
        _scan_prompt_kernel,
        grid=(N_LCHUNK, bsz),
        in_specs=[chunk, per_j(CHUNK * LANES, wide), per_j(wide, CHUNK * LANES), per_j(CHUNK * LANES, CHUNK * LANES),
                  per_j(1, ST_CHUNK), per_j(1, ST_CHUNK), per_j(1, CHUNK * LANES), state, state],
        out_specs=[chunk, state, state],
        out_shape=[jax.ShapeDtypeStruct((bsz, N_LCHUNK, n_chunks, CHUNK, N_SEG, LANES), F32),
                   jax.ShapeDtypeStruct((bsz, 1, N_STATE), F32), jax.ShapeDtypeStruct((bsz, 1, N_STATE), F32)],
        scratch_shapes=[pltpu.VMEM((rows, wide), F32), pltpu.VMEM((rows, wide), F32)],
        compiler_params=_params(2),
        name="ssm_scan_prompt",
    )(split(u), *weights, d_skip, h0r, h0i)
    return y.reshape(u.shape), hr, hi


def _scan_sample_kernel(u_ref, bmat_ref, cmat_ref, are_ref, aim_ref, d_ref, h0r_ref, h0i_ref,
                        y_ref, hnr_ref, hni_ref, h_s):
    a_re, a_im = are_ref[...], aim_ref[...]
    hr, hi = h0r_ref[...].T, h0i_ref[...].T
    u = u_ref[...]
    bu = _dot(u.astype(BF16), bmat_ref[...])
    for l in range(DEC_SEQ):
        rows = slice(l * DEC_BATCH, (l + 1) * DEC_BATCH)
        gr, gi = _cmul(a_re, a_im, hr, hi)
        hr, hi = gr + bu[rows, :ST_CHUNK], gi + bu[rows, ST_CHUNK:]
        h_s[rows, 0:ST_CHUNK] = hr.astype(BF16)
        h_s[rows, ST_CHUNK:] = hi.astype(BF16)
    y_ref[...] = _dot(h_s[...], cmat_ref[...]) + d_ref[...] * u
    hnr_ref[...] = hr.T
    hni_ref[...] = hi.T


def _scan_sample(u, bmat, cmat, a_re, a_im, d_skip, h0r, h0i):
    n = DEC_SEQ * DEC_BATCH
    chunk = pl.BlockSpec((None, n, LANES), lambda j: (j, 0, 0))
    per_j = lambda r, c: pl.BlockSpec((None, r, c), lambda j: (j, 0, 0))
    state = pl.BlockSpec((ST_CHUNK, DEC_BATCH), lambda j: (j, 0))
    return pl.pallas_call(
        _scan_sample_kernel,
        grid=(N_LCHUNK,),
        in_specs=[chunk, per_j(LANES, 2 * ST_CHUNK), per_j(2 * ST_CHUNK, LANES), per_j(1, ST_CHUNK),
                  per_j(1, ST_CHUNK), per_j(1, LANES), state, state],
        out_specs=[chunk, state, state],
        out_shape=[jax.ShapeDtypeStruct((N_LCHUNK, n, LANES), F32),
                   jax.ShapeDtypeStruct((N_STATE, DEC_BATCH), F32), jax.ShapeDtypeStruct((N_STATE, DEC_BATCH), F32)],
        scratch_shapes=[pltpu.VMEM((n, 2 * ST_CHUNK), BF16)],
        compiler_params=_params(1),
        name="ssm_scan_sample",
    )(u, bmat, cmat, a_re, a_im, d_skip, h0r, h0i)


def _ssm_out_kernel(sample, s_ref, y_ref, wglu_ref, bglu_ref, wout_ref, g_ref, b_ref, out_ref, z_ref):
    n_rows = z_ref.shape[1]
    if sample:
        parts = [(pl.ds(l, DEC_BATCH, stride=DEC_SEQ), slice(l * DEC_BATCH, (l + 1) * DEC_BATCH))
                 for l in range(DEC_SEQ)]
    else:
        parts = [(slice(s * SEG_TILE, (s + 1) * SEG_TILE), pl.ds(s, SEG_TILE, stride=N_SEG)) for s in range(N_SEG)]
    for c in range(N_LCHUNK):
        for tok_rows, slab_rows in parts:
            z_ref[c, tok_rows, :] = s_ref[c, slab_rows, :]
    z = jax.nn.gelu(jnp.concatenate([z_ref[c] for c in range(N_LCHUNK)], axis=1))
    gate = jax.nn.sigmoid(_dot(z.astype(BF16), wglu_ref[...]) + bglu_ref[...])
    mix = _dot((z * gate).astype(BF16), wout_ref[...])
    res = y_ref[...].reshape(n_rows, D_MODEL)
    out = _layer_norm(DN_ALPHA * res + mix, g_ref[...], b_ref[...])
    out_ref[...] = out.reshape(out_ref.shape)


def _ssm_out(s, y, s_spec, y_spec, grid, rows, sample, w_glu, b_glu, w_out, g, b, name):
    return pl.pallas_call(
        functools.partial(_ssm_out_kernel, sample),
        grid=grid,
        in_specs=[s_spec, y_spec, _resident((D_MODEL, D_MODEL)), _resident((1, D_MODEL)),
                  _resident((D_MODEL, D_MODEL)), _resident((1, D_MODEL)), _resident((1, D_MODEL))],
        out_specs=y_spec,
        out_shape=jax.ShapeDtypeStruct(y.shape, F32),
        scratch_shapes=[pltpu.VMEM((N_LCHUNK, rows, LANES), F32)],
        compiler_params=_params(len(grid)),
        name=name,
    )(s, y, w_glu, b_glu, w_out, g, b)


def _attn_prompt(yp, w_in, sinks, w_out, g, b):
    cos_p, sin_p = _rope_tables(jnp.arange(SEQ))
    tiles_per_seq = SEQ // ROW_TILE
    qa, ka, va, qb, kb, vb, *extra = _attn_proj(
        yp, w_in, cos_p, sin_p, lambda i: i % tiles_per_seq, BF16, prompt_seq=(BATCH, SEQ))
    dilated, tails = extra[:6], extra[6:]
    seq3 = lambda a: a.reshape(BATCH, SEQ, a.shape[-1])
    plane1 = lambda a: a.reshape(BATCH, 1, SEQ, D_A)
    o1, l1 = _band_a(plane1(qa), plane1(ka), plane1(va))
    pats = [(o1.reshape(BATCH * SEQ, D_A), l1.reshape(BATCH * SEQ, LANES))]
    for i in range(len(DILATIONS) - 1):
        pats.append(_band_a(*dilated[3 * i:3 * i + 3]))
    ob = _band_b(seq3(qb), seq3(kb), seq3(vb), sinks)
    yp = _attn_out_prompt(pats, ob, yp, w_out, g, b, SEQ)
    heads = lambda a, nh: jnp.transpose(a.reshape(BATCH, nh, HEAD_DIM, a.shape[-1]), (0, 3, 1, 2))[None]
    prompt_cache = (heads(tails[0], N_HEADS_A), heads(tails[1], N_HEADS_A),
                    heads(tails[2], N_KV_B), heads(tails[3], N_KV_B))
    return yp, prompt_cache


def _attn_sample_path(ys, cache_ak, cache_av, cache_bk, cache_bv, w_in, sinks, w_out, g, b, yp, ffn_args):
    cos_s, sin_s = _rope_tables(PAST_LEN + jnp.arange(DEC_SEQ))
    reps = DEC_BATCH * DEC_SEQ // DEC_SEQ
    cos_s, sin_s = jnp.tile(cos_s, (reps, 1)), jnp.tile(sin_s, (reps, 1))
    qa, ka, va, qb, kb, vb = _attn_proj(ys, w_in, cos_s, sin_s, lambda i: i, F32)
    stored = lambda a: jnp.transpose(a, (0, 2, 3, 1))
    yp, o = _ffn_with_sample_attention(
        yp, *ffn_args, qa, ka, va, stored(cache_ak), stored(cache_av),
        qb, kb, vb, stored(cache_bk), stored(cache_bv), sinks)
    ys = _attn_out_sample(o, ys, w_out, g, b)
    new = lambda a, nh: a.reshape(1, DEC_BATCH, DEC_SEQ, nh, HEAD_DIM)
    sample_cache = (new(ka, N_HEADS_A), new(va, N_HEADS_A), new(kb, N_KV_B), new(vb, N_KV_B))
    return yp, ys, sample_cache


def _attn_layer(yp, ys, cache_ak, cache_av, cache_bk, cache_bv, w_in, sinks, w_out, g, b, ffn_args):
    w_in = w_in.astype(BF16)
    w_out = w_out.astype(BF16)
    yp, prompt_cache = _attn_prompt(yp, w_in, sinks, w_out, g, b)
    yp, ys, sample_cache = _attn_sample_path(ys, cache_ak, cache_av, cache_bk, cache_bv, w_in, sinks, w_out, g, b,
                                             yp, ffn_args)
    return yp, ys, prompt_cache, sample_cache


def _ssm_layer(yp, ys, state_re, state_im, w_in, lam_re, lam_im, log_dt, b_re, b_im, c_re, c_im, d_skip,
               w_glu, b_glu, w_out, g, b):
    w_in, w_glu, w_out = w_in.astype(BF16), w_glu.astype(BF16), w_out.astype(BF16)
    b_glu = b_glu.reshape(1, D_MODEL)
    mats = _ssm_prepare(lam_re, lam_im, log_dt, b_re, b_im, c_re, c_im, d_skip)
    yp, prompt_state = _ssm_prompt(yp, w_in, mats, w_glu, b_glu, w_out, g, b)
    ys, sample_state = _ssm_sample(ys, state_re, state_im, w_in, mats, w_glu, b_glu, w_out, g, b)
    return yp, ys, prompt_state, sample_state


def _ssm_prepare(lam_re, lam_im, log_dt, b_re, b_im, c_re, c_im, d_skip):
    ab_re, ab_im, bb_re, bb_im = _ssm_discretize(lam_re, lam_im, log_dt, b_re, b_im)
    bmat, cmat = _ssm_matrices(bb_re, bb_im, c_re, c_im)
    a_re = ab_re.reshape(N_LCHUNK, 1, ST_CHUNK)
    a_im = ab_im.reshape(N_LCHUNK, 1, ST_CHUNK)
    d3 = d_skip.astype(F32).reshape(N_LCHUNK, 1, LANES)
    prompt = (_chunk_weights(a_re, a_im, bmat, cmat), jnp.tile(d3, (1, 1, CHUNK)))
    sample = (bmat.astype(BF16), cmat.astype(BF16), a_re, a_im, d3)
    return prompt, sample


def _ssm_prompt(yp, w_in, mats, w_glu, b_glu, w_out, g, b):
    weights, d_tiled = mats[0]
    up = _ssm_in_prompt(yp, w_in)
    zero = jnp.zeros((BATCH, 1, N_STATE), F32)
    sp, pr, pi = _scan_prompt(up, weights, d_tiled, zero, zero)
    yp = _ssm_out(sp, yp.reshape(BATCH, N_SEG, SEG_LEN, D_MODEL),
                  pl.BlockSpec((None, N_LCHUNK, ROW_TILE, LANES), lambda bb, t: (bb, 0, t, 0)),
                  pl.BlockSpec((None, N_SEG, SEG_TILE, D_MODEL), lambda bb, t: (bb, 0, t, 0)),
                  (BATCH, SEG_LEN // SEG_TILE), ROW_TILE, False, w_glu, b_glu, w_out, g, b,
                  "ssm_out_prompt").reshape(BATCH * SEQ, D_MODEL)
    prompt_state = (pr.reshape(1, BATCH, N_SSM_GROUPS, SSM_STATE), pi.reshape(1, BATCH, N_SSM_GROUPS, SSM_STATE))
    return yp, prompt_state


def _ssm_sample(ys, state_re, state_im, w_in, mats, w_glu, b_glu, w_out, g, b):
    bmat, cmat, a_re, a_im, d3 = mats[1]
    us = _ssm_in_sample(ys, w_in)
    stored = lambda a: jnp.transpose(a, (1, 2, 0)).reshape(N_STATE, DEC_BATCH)
    logical = lambda a: jnp.transpose(a.reshape(N_SSM_GROUPS, SSM_STATE, DEC_BATCH), (2, 0, 1))[None]
    ss, sr, si = _scan_sample(us, bmat, cmat, a_re, a_im, d3, stored(state_re), stored(state_im))
    n = DEC_BATCH * DEC_SEQ
    ys = _ssm_out(ss, ys,
                  pl.BlockSpec((N_LCHUNK, n, LANES), lambda i: (0, 0, 0)),
                  pl.BlockSpec((n, D_MODEL), lambda i: (0, 0)),
                  (1,), n, True, w_glu, b_glu, w_out, g, b, "ssm_out_sample")
    return ys, (logical(sr), logical(si))


def kernel(x_prompt, x_sample, cache_a_k, cache_a_v, cache_b_k, cache_b_v, state_c_re, state_c_im, ln_g, ln_b, ffn_w_gate, ffn_w_up, ffn_w_down, attn_w_in, attn_sinks, attn_w_out, ssm_w_in, ssm_lambda_re, ssm_lambda_im, ssm_log_dt, ssm_b_re, ssm_b_im, ssm_c_re, ssm_c_im, ssm_d, ssm_w_glu, ssm_b_glu, ssm_w_out):
    yp = x_prompt.reshape(BATCH * SEQ, D_MODEL)
    ys = x_sample.reshape(DEC_BATCH * DEC_SEQ, D_MODEL)
    ln = lambda l, k: (ln_g[l, k].reshape(1, D_MODEL), ln_b[l, k].reshape(1, D_MODEL))

    wg, wu, wd = ffn_w_gate.astype(BF16), ffn_w_up.astype(BF16), ffn_w_down.astype(BF16)

    def ffn_pair(yp, ys, l, k, ln_idx):
        g, b = ln(l, ln_idx)
        return _ffn_pair(yp, ys, wg, wu, wd, g, b, l, k)

    yp, ys = ffn_pair(yp, ys, 0, 0, 0)
    yp, ys, p_cache, s_cache = _attn_layer(yp, ys, cache_a_k[0], cache_a_v[0], cache_b_k[0], cache_b_v[0],
                                           attn_w_in[0], attn_sinks[0], attn_w_out[0], *ln(0, 1),
                                           ffn_args=(wg, wu, wd, *ln(0, 2), 0, 1))
    ys = _ffn(ys, wg, wu, wd, *ln(0, 2), 0, 1)
    yp, ys = ffn_pair(yp, ys, 1, 0, 0)
    yp, ys, p_state, s_state = _ssm_layer(yp, ys, state_c_re[0], state_c_im[0], ssm_w_in[0], ssm_lambda_re[0],
                                          ssm_lambda_im[0], ssm_log_dt[0], ssm_b_re[0], ssm_b_im[0], ssm_c_re[0],
                                          ssm_c_im[0], ssm_d[0], ssm_w_glu[0], ssm_b_glu[0], ssm_w_out[0], *ln(1, 1))
    yp, ys = ffn_pair(yp, ys, 1, 1, 2)
    return (yp.reshape(BATCH, SEQ, D_MODEL), ys.reshape(DEC_BATCH, DEC_SEQ, D_MODEL),
            *p_cache, *p_state, *s_cache, *s_state)
```

```python
import functools
import math

import jax
import jax.numpy as jnp
from jax import lax
from jax.experimental import pallas as pl
from jax.experimental.pallas import tpu as pltpu

F32 = jnp.float32
BF16 = jnp.bfloat16

D_MODEL = 1024
BATCH = 4
SEQ = 4096
DEPTH = 2
DEC_BATCH = 128
DEC_SEQ = 8
PAST_LEN = 16384
HEAD_DIM = 64
N_HEADS_A = 8
DILATIONS = (1, 4, 16)
WIN_A = 2048
N_HEADS_B = 8
N_KV_B = 2
WIN_B = 128
ROPE_THETA = 10000.0
D_A = N_HEADS_A * HEAD_DIM
D_BQ = N_HEADS_B * HEAD_DIM
D_BKV = N_KV_B * HEAD_DIM
D_IN_ATTN = 3 * D_A + D_BQ + 2 * D_BKV
SSM_GROUP = 16
N_SSM_GROUPS = D_MODEL // SSM_GROUP
SSM_STATE = 64
N_STATE = N_SSM_GROUPS * SSM_STATE
D_FF = 2816
DN_ALPHA = (2 * DEPTH) ** 0.25
FFN_RES = 0.5
LN_EPS = 1e-5
ATTN_SCALE = HEAD_DIM ** -0.5
LOG2E = math.log2(math.e)
LSE_LANES = 16

LANES = 128
SUBLANES = 8
MXU_N = 256
VMEM_LIMIT = 56 * 1024 * 1024

ROW_TILE = 512
FFN_ROW_TILE = 1024
FF_CHUNK = MXU_N
TQ = 128
N_SEG = SUBLANES
SEG_LEN = SEQ // N_SEG
N_LCHUNK = D_MODEL // LANES
ST_CHUNK = N_STATE // N_LCHUNK
NK_PAD = WIN_A + LANES
NKB_PAD = 2 * WIN_B

NEG_INF = float("-inf")


def _params(n_axes, vmem=VMEM_LIMIT):
    return pltpu.CompilerParams(dimension_semantics=("arbitrary",) * n_axes, vmem_limit_bytes=vmem)


def _resident(shape):
    return pl.BlockSpec(shape, lambda *_: (0,) * len(shape), pipeline_mode=pl.Buffered(1))


def _layer_norm(x, g, b):
    mu = jnp.mean(x, -1, keepdims=True)
    xc = x - mu
    var = jnp.mean(xc * xc, -1, keepdims=True)
    return xc * lax.rsqrt(var + LN_EPS) * g + b


def _dot(a, b):
    return jnp.dot(a, b, preferred_element_type=F32)


def _dot_nt(a, b):
    return lax.dot_general(a, b, (((1,), (1,)), ((), ())), preferred_element_type=F32)


def _ffn_kernel(x_ref, wg_ref, wu_ref, wd_ref, g_ref, b_ref, o_ref, h_ref):
    x = x_ref[...]
    xb = x.astype(BF16)
    for c in range(D_FF // FF_CHUNK):
        sl = slice(c * FF_CHUNK, (c + 1) * FF_CHUNK)
        gate = _dot(xb, wg_ref[:, sl])
        up = _dot(xb, wu_ref[:, sl])
        h_ref[:, sl] = (gate * jax.nn.sigmoid(gate) * up).astype(BF16)
    y = DN_ALPHA * x + FFN_RES * _dot(h_ref[...], wd_ref[...])
    o_ref[...] = _layer_norm(y, g_ref[...], b_ref[...])


def _ffn(x, wg, wu, wd, g, b, layer=0, which=0):
    n = x.shape[0]
    tm = min(FFN_ROW_TILE, n)
    row = pl.BlockSpec((tm, D_MODEL), lambda i: (i, 0))
    if wg.ndim == 4:
        pick = lambda r, c: pl.BlockSpec((None, None, r, c), lambda i: (layer, which, 0, 0),
                                         pipeline_mode=pl.Buffered(1))
    else:
        pick = lambda r, c: _resident((r, c))
    return pl.pallas_call(
        _ffn_kernel,
        grid=(n // tm,),
        in_specs=[row, pick(D_MODEL, D_FF), pick(D_MODEL, D_FF), pick(D_FF, D_MODEL),
                  _resident((1, D_MODEL)), _resident((1, D_MODEL))],
        out_specs=row,
        out_shape=jax.ShapeDtypeStruct((n, D_MODEL), F32),
        scratch_shapes=[pltpu.VMEM((tm, D_FF), BF16)],
        compiler_params=_params(1),
        name="ffn",
    )(x, wg, wu, wd, g, b)


def _ffn_pair_kernel(n_first, xp_ref, xs_ref, wg_ref, wu_ref, wd_ref, g_ref, b_ref, op_ref, os_ref, h_ref):
    step = pl.program_id(0)
    weights = (wg_ref, wu_ref, wd_ref, g_ref, b_ref)
    pl.when(step < n_first)(functools.partial(_ffn_kernel, xp_ref, *weights, op_ref, h_ref))
    h_small = h_ref.at[pl.ds(0, xs_ref.shape[0])]
    pl.when(step >= n_first)(functools.partial(_ffn_kernel, xs_ref, *weights, os_ref, h_small))


def _ffn_pair(xp, xs, wg, wu, wd, g, b, layer, which):
    tm, tm2 = min(FFN_ROW_TILE, xp.shape[0]), ROW_TILE
    n_first, n_second = xp.shape[0] // tm, xs.shape[0] // tm2
    first = pl.BlockSpec((tm, D_MODEL), lambda i: (jnp.minimum(i, n_first - 1), 0))
    second = pl.BlockSpec((tm2, D_MODEL), lambda i: (jnp.maximum(i - n_first, 0), 0))
    pick = lambda r, c: pl.BlockSpec((None, None, r, c), lambda i: (layer, which, 0, 0), pipeline_mode=pl.Buffered(1))
    return pl.pallas_call(
        functools.partial(_ffn_pair_kernel, n_first),
        grid=(n_first + n_second,),
        in_specs=[first, second, pick(D_MODEL, D_FF), pick(D_MODEL, D_FF), pick(D_FF, D_MODEL),
                  _resident((1, D_MODEL)), _resident((1, D_MODEL))],
        out_specs=[first, second],
        out_shape=[jax.ShapeDtypeStruct(xp.shape, F32), jax.ShapeDtypeStruct(xs.shape, F32)],
        scratch_shapes=[pltpu.VMEM((tm, D_FF), BF16)],
        compiler_params=_params(1),
        name="ffn_pair",
    )(xp, xs, wg, wu, wd, g, b)


def _rope_tables(pos):
    half = HEAD_DIM // 2
    inv_freq = ROPE_THETA ** (-jnp.arange(half, dtype=F32) / half)
    ang = pos.astype(F32)[:, None] * inv_freq[None, :]
    cos, sin = jnp.cos(ang), jnp.sin(ang)
    cos_t = jnp.concatenate([cos, cos, cos, cos], -1)
    sin_t = jnp.concatenate([-sin, sin, -sin, sin], -1)
    return cos_t, sin_t


def _attn_proj_kernel(tiles_per_seq, x_ref, w_ref, cos_ref, sin_ref, qa_ref, ka_ref, va_ref, qb_ref, kb_ref, vb_ref,
                      *extra):
    xb = x_ref[...].astype(BF16)
    slab_ref = extra[-1] if extra else None
    dilated = extra[:6]
    n_chunks = D_A // LANES
    tm = x_ref.shape[0]

    def keep(tensor, c, val):
        if slab_ref is not None:
            slab_ref[tensor * n_chunks + c] = val
    cos = cos_ref[...]
    sin = sin_ref[...]
    lane = lax.broadcasted_iota(jnp.int32, cos.shape, 1)
    first_half = (lane & (HEAD_DIM // 2)) == 0

    def rope(z):
        rot = jnp.where(first_half, pltpu.roll(z, LANES - HEAD_DIM // 2, 1), pltpu.roll(z, HEAD_DIM // 2, 1))
        return z * cos + rot * sin

    def project(col0, ncols):
        return _dot(xb, w_ref[:, col0:col0 + ncols])

    def rope_chunks(z):
        return [rope(z[:, c * LANES:(c + 1) * LANES]) for c in range(z.shape[1] // LANES)]

    q_scale = ATTN_SCALE * LOG2E if extra else ATTN_SCALE
    col = 0
    for c, r in enumerate(rope_chunks(project(col, D_A))):
        r = r * q_scale
        qa_ref[:, c * LANES:(c + 1) * LANES] = r.astype(qa_ref.dtype)
        keep(0, c, r)
    col += D_A
    for c, r in enumerate(rope_chunks(project(col, D_A))):
        ka_ref[:, c * LANES:(c + 1) * LANES] = r.astype(ka_ref.dtype)
        keep(1, c, r)
    col += D_A
    z = project(col, D_A)
    va_ref[...] = z.astype(va_ref.dtype)
    for c in range(n_chunks):
        keep(2, c, z[:, c * LANES:(c + 1) * LANES])
    col += D_A
    if extra:
        kat_ref, vat_ref, kbt_ref, vbt_ref = extra[6:10]
        tile_in_seq = pl.program_id(0) % tiles_per_seq

        @pl.when(tile_in_seq >= tiles_per_seq - WIN_A // tm)
        def _():
            for c in range(n_chunks):
                kat_ref[c * LANES:(c + 1) * LANES, :] = slab_ref[n_chunks + c].T
                vat_ref[c * LANES:(c + 1) * LANES, :] = slab_ref[2 * n_chunks + c].T

        slab4_ref = extra[-2]
        d1, d2 = DILATIONS[1], DILATIONS[2] // DILATIONS[1]
        plane = tm // d1
        for tensor in range(3):
            out1_ref, out2_ref = dilated[tensor], dilated[3 + tensor]
            for c in range(n_chunks):
                idx = tensor * n_chunks + c
                lanes = slice(c * LANES, (c + 1) * LANES)
                for r in range(d1):
                    rows = slab_ref[idx, pl.ds(r, plane, stride=d1), :]
                    slab4_ref[idx, r * plane:(r + 1) * plane, :] = rows
                    out1_ref[r, :, lanes] = rows.astype(out1_ref.dtype)
                for r in range(d1):
                    for m in range(d2):
                        rows = slab4_ref[idx, pl.ds(r * plane + m, plane // d2, stride=d2), :]
                        out2_ref[r + d1 * m, :, lanes] = rows.astype(out2_ref.dtype)
    for c, r in enumerate(rope_chunks(project(col, D_BQ))):
        qb_ref[:, c * LANES:(c + 1) * LANES] = (r * q_scale).astype(qb_ref.dtype)
    col += D_BQ
    z = project(col, 2 * D_BKV)
    r = rope(z[:, :D_BKV])
    if extra:
        lo = _lane_lo(r.shape)
        for ref, val in ((kb_ref, r), (vb_ref, z[:, D_BKV:])):
            swapped = pltpu.roll(val, HEAD_DIM, 1)
            ref[:, 0:LANES] = jnp.where(lo, val, swapped).astype(ref.dtype)
            ref[:, LANES:] = jnp.where(lo, swapped, val).astype(ref.dtype)
    else:
        kb_ref[...] = r.astype(kb_ref.dtype)
        vb_ref[...] = z[:, D_BKV:].astype(vb_ref.dtype)
    if extra:
        @pl.when(tile_in_seq == tiles_per_seq - 1)
        def _():
            kbt_ref[...] = r[tm - WIN_B:, :].T
            vbt_ref[...] = z[tm - WIN_B:, D_BKV:].T


def _attn_proj(x, w, cos_t, sin_t, table_block, act_dtype, prompt_seq=None):
    n = x.shape[0]
    tm = min(ROW_TILE, n)

    def row(width):
        return pl.BlockSpec((tm, width), lambda i: (i, 0))

    tab = pl.BlockSpec((tm, LANES), lambda i: (table_block(i), 0))
    kv_b = D_BKV if prompt_seq is None else N_KV_B * LANES
    widths = (D_A, D_A, D_A, D_BQ, kv_b, kv_b)
    out_shape = [jax.ShapeDtypeStruct((n, wd), act_dtype) for wd in widths]
    out_specs = [row(wd) for wd in widths]
    scratch = []
    tps = None
    if prompt_seq is not None:
        bsz, seq = prompt_seq
        tps = seq // tm
        for dil in DILATIONS[1:]:
            out_shape += [jax.ShapeDtypeStruct((bsz, dil, seq // dil, D_A), BF16)] * 3
            out_specs += [pl.BlockSpec((None, dil, tm // dil, D_A), lambda i: (i // tps, 0, i % tps, 0))] * 3
        first_tail = tps - WIN_A // tm
        out_shape += [jax.ShapeDtypeStruct((bsz, D_A, WIN_A), F32)] * 2
        out_specs += [pl.BlockSpec((None, D_A, tm), lambda i: (i // tps, 0, jnp.maximum(i % tps - first_tail, 0)))] * 2
        out_shape += [jax.ShapeDtypeStruct((bsz, D_BKV, WIN_B), F32)] * 2
        out_specs += [pl.BlockSpec((None, D_BKV, WIN_B), lambda i: (i // tps, 0, 0))] * 2
        assert DILATIONS[2] == DILATIONS[1] ** 2
        scratch = [pltpu.VMEM((3 * D_A // LANES, tm, LANES), F32)] * 2
    return pl.pallas_call(
        functools.partial(_attn_proj_kernel, tps),
        grid=(n // tm,),
        in_specs=[row(D_MODEL), _resident((D_MODEL, D_IN_ATTN)), tab, tab],
        out_specs=out_specs,
        out_shape=out_shape,
        scratch_shapes=scratch,
        compiler_params=_params(1),
        name="attn_proj",
    )(x, w, cos_t, sin_t)


def _lane_lo(shape):
    return lax.broadcasted_iota(jnp.int32, shape, 1) < HEAD_DIM


def _half_masks_bf16():
    lo = jnp.where(_lane_lo((1, LANES)), 1.0, 0.0).astype(BF16)
    return lo, 1 - lo


def _band_masks(n_heads, t, sub):
    row = lax.broadcasted_iota(jnp.int32, (n_heads * TQ, TQ), 0) & (TQ - 1)
    col = lax.broadcasted_iota(jnp.int32, (n_heads * TQ, TQ), 1)
    shift = jnp.where(t > 0, 0, TQ) if sub == 0 else 0
    return col <= row, col >= row + shift


def _sub_tile_kv(sub, sl, kc_ref, kp_ref, vc_ref, vp_ref):
    if sub == 0:
        return kc_ref[0:TQ, sl], kp_ref[:, sl], vc_ref[0:TQ, sl], vp_ref[:, sl]
    return kc_ref[TQ:2 * TQ, sl], kc_ref[0:TQ, sl], vc_ref[TQ:2 * TQ, sl], vc_ref[0:TQ, sl]


def _band_softmax(qs, kc, kp, vc, vp, mask_c, mask_p):
    s_c = jnp.where(mask_c, _dot_nt(qs, kc), NEG_INF)
    s_p = jnp.where(mask_p, _dot_nt(qs, kp), NEG_INF)
    m = jnp.max(jnp.maximum(s_c, s_p), -1, keepdims=True)
    p_c = jnp.exp2(s_c - m)
    p_p = jnp.exp2(s_p - m)
    den = jnp.sum(p_c + p_p, -1, keepdims=True)
    acc = _dot(p_c.astype(BF16), vc) + _dot(p_p.astype(BF16), vp)
    return acc * (1.0 / den), m, den


def _band_a_kernel(q_ref, kc_ref, kp_ref, vc_ref, vp_ref, o_ref, lse_ref):
    t = pl.program_id(2)
    lo = _lane_lo((TQ, LANES))
    lo_bf, hi_bf = _half_masks_bf16()
    lane_head = jnp.right_shift(lax.broadcasted_iota(jnp.int32, (TQ, LANES), 1), int(math.log2(LSE_LANES)))
    for sub in range(2):
        rows = slice(sub * TQ, (sub + 1) * TQ)
        mask_c, mask_p = _band_masks(2, t, sub)
        lse_tile = jnp.zeros((TQ, LANES), F32)
        for c in range(D_A // LANES):
            sl = slice(c * LANES, (c + 1) * LANES)
            kc, kp, vc, vp = _sub_tile_kv(sub, sl, kc_ref, kp_ref, vc_ref, vp_ref)
            q2 = q_ref[rows, sl]
            qs = jnp.concatenate([q2 * lo_bf, q2 * hi_bf], axis=0)
            out, m, den = _band_softmax(qs, kc, kp, vc, vp, mask_c, mask_p)
            lse = m + jnp.log2(den)
            o_ref[rows, sl] = jnp.where(lo, out[0:TQ], out[TQ:]).astype(o_ref.dtype)
            lse_tile = jnp.where(lane_head == 2 * c, lse[0:TQ], lse_tile)
            lse_tile = jnp.where(lane_head == 2 * c + 1, lse[TQ:], lse_tile)
        lse_ref[rows, :] = lse_tile


def _band_a(q, k, v):
    bsz, dil, sub, _ = q.shape
    cur = pl.BlockSpec((None, None, 2 * TQ, D_A), lambda b, r, t: (b, r, t, 0))
    prev = pl.BlockSpec((None, None, TQ, D_A), lambda b, r, t: (b, r, jnp.maximum(2 * t - 1, 0), 0))
    lse = pl.BlockSpec((None, None, 2 * TQ, LANES), lambda b, r, t: (b, r, t, 0))
    return pl.pallas_call(
        _band_a_kernel,
        grid=(bsz, dil, sub // (2 * TQ)),
        in_specs=[cur, cur, prev, cur, prev],
        out_specs=[cur, lse],
        out_shape=[jax.ShapeDtypeStruct((bsz, dil, sub, D_A), BF16),
                   jax.ShapeDtypeStruct((bsz, dil, sub, LANES), F32)],
        compiler_params=_params(3),
        name=f"band_a_d{dil}",
    )(q, k, k, v, v)


def _band_b_body(first_tile, sink_ref, q_ref, kc_ref, kp_ref, vc_ref, vp_ref, o_ref):
    group = N_HEADS_B // N_KV_B
    lo = _lane_lo((TQ, LANES))
    lo_bf, hi_bf = _half_masks_bf16()
    row = lax.broadcasted_iota(jnp.int32, (group * TQ, TQ), 0) & (TQ - 1)
    col = lax.broadcasted_iota(jnp.int32, (group * TQ, TQ), 1)
    in_cur = col <= row
    cur_bf = jnp.where(in_cur, 1.0, 0.0).astype(BF16)
    prev_bf = 1 - cur_bf
    for sub in range(2):
        rows = slice(sub * TQ, (sub + 1) * TQ)
        for g in range(N_KV_B):
            sl = slice(g * LANES, (g + 1) * LANES)
            kc, kp, vc, vp = _sub_tile_kv(sub, sl, kc_ref, kp_ref, vc_ref, vp_ref)
            heads = range(g * group, (g + 1) * group)
            qs = jnp.concatenate(
                [q_ref[rows, (h // 2) * LANES:(h // 2 + 1) * LANES] * (lo_bf if h % 2 == 0 else hi_bf) for h in heads],
                axis=0)
            sink = jnp.concatenate([jnp.full((TQ, 1), sink_ref[h] * LOG2E, F32) for h in heads], axis=0)
            only_cur = first_tile and sub == 0
            s = jnp.where(in_cur, _dot_nt(qs, kc), NEG_INF if only_cur else _dot_nt(qs, kp))
            m = jnp.maximum(jnp.max(s, -1, keepdims=True), sink)
            p = jnp.exp2(s - m)
            den = jnp.sum(p, -1, keepdims=True) + jnp.exp2(sink - m)
            pb = p.astype(BF16)
            acc = _dot(pb, vc) if only_cur else _dot(pb * cur_bf, vc) + _dot(pb * prev_bf, vp)
            out = acc * (1.0 / den)
            for i in range(group // 2):
                c = g * (group // 2) + i
                even, odd = out[2 * i * TQ:(2 * i + 1) * TQ], out[(2 * i + 1) * TQ:(2 * i + 2) * TQ]
                o_ref[rows, c * LANES:(c + 1) * LANES] = jnp.where(lo, even, odd).astype(o_ref.dtype)


def _band_b_kernel(*refs):
    t = pl.program_id(1)
    pl.when(t == 0)(functools.partial(_band_b_body, True, *refs))
    pl.when(t > 0)(functools.partial(_band_b_body, False, *refs))


def _band_b(q, k, v, sinks):
    bsz, seq, _ = q.shape
    kv_lanes = N_KV_B * LANES
    qs = pl.BlockSpec((None, 2 * TQ, D_BQ), lambda b, t: (b, t, 0))
    cur = pl.BlockSpec((None, 2 * TQ, kv_lanes), lambda b, t: (b, t, 0))
    prev = pl.BlockSpec((None, TQ, kv_lanes), lambda b, t: (b, jnp.maximum(2 * t - 1, 0), 0))
    o = pl.pallas_call(
        _band_b_kernel,
        grid=(bsz, seq // (2 * TQ)),
        in_specs=[pl.BlockSpec(memory_space=pltpu.SMEM), qs, cur, prev, cur, prev],
        out_specs=qs,
        out_shape=jax.ShapeDtypeStruct((bsz, seq, D_BQ), BF16),
        compiler_params=_params(2),
        name="band_b",
    )(sinks, q, k, k, v, v)
    return o.reshape(bsz * seq, D_BQ)


def _pattern_count(dist):
    cnt = jnp.zeros(dist.shape, F32)
    for dil in DILATIONS:
        cnt = cnt + ((dist >= 0) & (dist <= 128 * dil) & (dist % dil == 0)).astype(F32)
    return cnt


def _sample_tables():
    i = jnp.arange(DEC_SEQ)
    cnt_c = _pattern_count(WIN_A + i[:, None] - jnp.arange(WIN_A)[None, :])
    j = jnp.arange(LANES)
    cnt_n = jnp.where(j[None, :] < DEC_SEQ, _pattern_count(i[:, None] - j[None, :]), 0.0)
    cnt_n = jnp.tile(cnt_n, (N_HEADS_A, 1))
    jb = jnp.arange(NKB_PAD)[None, :]
    dist_b = WIN_B + i[:, None] - jb
    ok_b = (dist_b >= 0) & (dist_b < WIN_B) & (jb < WIN_B + DEC_SEQ)
    mask_b = jnp.tile(ok_b.astype(F32), (N_HEADS_B, 1))
    return cnt_c, cnt_n, mask_b


def _sample_attend(q, kan, van, kt_ref, vt_ref, cnt_c, cnt_n, qb, kbn, vbn, kbc, vbc, mask_b, sink_col, kb_s, vb_s):
    rows = N_HEADS_A * DEC_SEQ
    q_rep = jnp.concatenate([q] * N_HEADS_A, axis=0)
    row_head = jnp.right_shift(lax.broadcasted_iota(jnp.int32, (rows, D_A), 0), int(math.log2(DEC_SEQ)))
    lane_head = jnp.right_shift(lax.broadcasted_iota(jnp.int32, (rows, D_A), 1), int(math.log2(HEAD_DIM)))
    own = row_head == lane_head
    q_bd = jnp.where(own, q_rep, 0.0).astype(BF16)
    pad = jnp.zeros((LANES - DEC_SEQ, D_A), F32)
    kn = jnp.concatenate([kan, pad], 0).astype(BF16)
    vn = jnp.concatenate([van, pad], 0).astype(BF16)
    s_new = jnp.where(cnt_n > 0.0, _dot_nt(q_bd, kn), NEG_INF)
    outs, p_new = [], []
    for h in range(N_HEADS_A):
        head_rows = slice(h * DEC_SEQ, (h + 1) * DEC_SEQ)
        q_h = q[:, h * HEAD_DIM:(h + 1) * HEAD_DIM].astype(BF16)
        s_c = jnp.where(cnt_c > 0.0, _dot(q_h, kt_ref[h].astype(BF16)), NEG_INF)
        s_n = s_new[head_rows]
        m = jnp.maximum(jnp.max(s_c, -1, keepdims=True), jnp.max(s_n, -1, keepdims=True))
        p_c = jnp.exp(s_c - m) * cnt_c
        p_n = jnp.exp(s_n - m) * cnt_n[head_rows]
        inv = 1.0 / (jnp.sum(p_c, -1, keepdims=True) + jnp.sum(p_n, -1, keepdims=True))
        outs.append(_dot_nt(p_c.astype(BF16), vt_ref[h].astype(BF16)) * inv)
        p_new.append(p_n * inv)
    out_n = jnp.where(own, _dot(jnp.concatenate(p_new, axis=0).astype(BF16), vn), 0.0)
    oa = jnp.concatenate(outs, axis=1)
    for h in range(N_HEADS_A):
        oa = oa + out_n[h * DEC_SEQ:(h + 1) * DEC_SEQ]

    n_pad_b = NKB_PAD - WIN_B - DEC_SEQ
    pad_b = jnp.zeros((n_pad_b, D_BKV), F32)
    kb_s[...] = jnp.concatenate([kbc, kbn, pad_b], 0).astype(BF16)
    vb_s[...] = jnp.concatenate([vbc, vbn, pad_b], 0).astype(BF16)
    lo8 = _lane_lo((DEC_SEQ, LANES))
    group = N_HEADS_B // N_KV_B
    pieces = []
    for h in range(N_HEADS_B):
        chunk = qb[:, (h // 2) * LANES:(h // 2 + 1) * LANES]
        g = h // group
        if h % 2 != g:
            chunk = pltpu.roll(chunk, HEAD_DIM, 1)
        pieces.append(jnp.where(lo8 if g == 0 else jnp.logical_not(lo8), chunk, 0.0))
    qb_bd = jnp.concatenate(pieces, axis=0).astype(BF16)
    sb = jnp.where(mask_b > 0.0, _dot_nt(qb_bd, kb_s[...]), NEG_INF)
    sink = sink_col[:, 0:1]
    mb = jnp.maximum(jnp.max(sb, -1, keepdims=True), sink)
    pb = jnp.exp(sb - mb) * mask_b
    den_b = jnp.sum(pb, -1, keepdims=True) + jnp.exp(sink - mb)
    ob_full = _dot(pb.astype(BF16), vb_s[...]) * (1.0 / den_b)
    ob = []
    for c in range(D_BQ // LANES):
        halves = []
        for half in range(2):
            h = 2 * c + half
            piece = ob_full[h * DEC_SEQ:(h + 1) * DEC_SEQ]
            if half != h // group:
                piece = pltpu.roll(piece, HEAD_DIM, 1)
            halves.append(piece)
        ob.append(jnp.where(lo8, halves[0], halves[1]))
    return jnp.concatenate([oa] + ob, axis=1)


FUSED_ROW_TILE = 512
SEQ_PER_STEP = DEC_BATCH // (BATCH * SEQ // FUSED_ROW_TILE)


def _kv_copies(kt_hbm, vt_hbm, kbuf, vbuf, sems, seq, slot):
    return (pltpu.make_async_copy(kt_hbm.at[seq], kbuf.at[slot], sems.at[0, slot]),
            pltpu.make_async_copy(vt_hbm.at[seq], vbuf.at[slot], sems.at[1, slot]))


def _ffn_attn_kernel(x_ref, wg_ref, wu_ref, wd_ref, g_ref, b_ref,
                     qa_ref, kan_ref, van_ref, kt_hbm, vt_hbm, cnt_c_ref, cnt_n_ref,
                     qb_ref, kbn_ref, vbn_ref, kbc_ref, vbc_ref, maskb_ref, sinkcol_ref,
                     y_ref, o_ref, h_ref, kbuf, vbuf, sems, kb_s, vb_s):
    step = pl.program_id(0)
    n_steps = pl.num_programs(0)
    copies = functools.partial(_kv_copies, kt_hbm, vt_hbm, kbuf, vbuf, sems)

    @pl.when(step == 0)
    def _():
        for cp in copies(0, 0):
            cp.start()

    x = x_ref[...]
    xb = x.astype(BF16)
    n_ff = D_FF // FF_CHUNK
    per_seq = -(-n_ff // SEQ_PER_STEP)
    cnt_c, cnt_n, mask_b, sink_col = cnt_c_ref[...], cnt_n_ref[...], maskb_ref[...], sinkcol_ref[...]
    for s in range(SEQ_PER_STEP):
        slot = s % 2
        seq = step * SEQ_PER_STEP + s
        for cp in copies(seq, slot):
            cp.wait()
        if s + 1 < SEQ_PER_STEP:
            for cp in copies(seq + 1, 1 - slot):
                cp.start()
        else:
            @pl.when(step + 1 < n_steps)
            def _():
                for cp in copies(seq + 1, 1 - slot):
                    cp.start()
        rows = slice(s * DEC_SEQ, (s + 1) * DEC_SEQ)
        kbc = jnp.concatenate([kbc_ref[s, g].T for g in range(N_KV_B)], axis=1)
        vbc = jnp.concatenate([vbc_ref[s, g].T for g in range(N_KV_B)], axis=1)
        o_ref[rows, :] = _sample_attend(
            qa_ref[rows, :], kan_ref[rows, :], van_ref[rows, :], kbuf.at[slot], vbuf.at[slot], cnt_c, cnt_n,
            qb_ref[rows, :], kbn_ref[rows, :], vbn_ref[rows, :], kbc, vbc, mask_b, sink_col, kb_s, vb_s)
        for c in range(s * per_seq, min((s + 1) * per_seq, n_ff)):
            sl = slice(c * FF_CHUNK, (c + 1) * FF_CHUNK)
            gate = _dot(xb, wg_ref[:, sl])
            up = _dot(xb, wu_ref[:, sl])
            h_ref[:, sl] = (gate * jax.nn.sigmoid(gate) * up).astype(BF16)
    y = DN_ALPHA * x + FFN_RES * _dot(h_ref[...], wd_ref[...])
    y_ref[...] = _layer_norm(y, g_ref[...], b_ref[...])


def _ffn_with_sample_attention(x, wg, wu, wd, g, b, layer, which,
                               qa, kan, van, cache_ak, cache_av, qb, kbn, vbn, cache_bk, cache_bv, sinks):
    n = x.shape[0]
    tm = FUSED_ROW_TILE
    assert n // tm * SEQ_PER_STEP == DEC_BATCH and SEQ_PER_STEP % 2 == 0
    cnt_c, cnt_n, mask_b = _sample_tables()
    sink_col = jnp.broadcast_to(jnp.repeat(sinks.astype(F32), DEC_SEQ)[:, None], (N_HEADS_B * DEC_SEQ, LANES))
    row = pl.BlockSpec((tm, D_MODEL), lambda i: (i, 0))
    pick = lambda r, c: pl.BlockSpec((None, None, r, c), lambda i: (layer, which, 0, 0), pipeline_mode=pl.Buffered(1))
    new = lambda width: pl.BlockSpec((SEQ_PER_STEP * DEC_SEQ, width), lambda i: (i, 0))
    cache_b = pl.BlockSpec((SEQ_PER_STEP, N_KV_B, HEAD_DIM, WIN_B), lambda i: (i, 0, 0, 0))
    hbm = pl.BlockSpec(memory_space=pl.ANY)
    rows = N_HEADS_A * DEC_SEQ
    kv_slot = (2, N_HEADS_A, HEAD_DIM, WIN_A)
    return pl.pallas_call(
        _ffn_attn_kernel,
        grid=(n // tm,),
        in_specs=[row, pick(D_MODEL, D_FF), pick(D_MODEL, D_FF), pick(D_FF, D_MODEL),
                  _resident((1, D_MODEL)), _resident((1, D_MODEL)),
                  new(D_A), new(D_A), new(D_A), hbm, hbm, _resident(cnt_c.shape), _resident(cnt_n.shape),
                  new(D_BQ), new(D_BKV), new(D_BKV), cache_b, cache_b,
                  _resident((rows, NKB_PAD)), _resident((rows, LANES))],
        out_specs=[row, new(D_A + D_BQ)],
        out_shape=[jax.ShapeDtypeStruct((n, D_MODEL), F32),
                   jax.ShapeDtypeStruct((DEC_BATCH * DEC_SEQ, D_A + D_BQ), F32)],
        scratch_shapes=[pltpu.VMEM((tm, D_FF), BF16), pltpu.VMEM(kv_slot, F32), pltpu.VMEM(kv_slot, F32),
                        pltpu.SemaphoreType.DMA((2, 2)),
                        pltpu.VMEM((NKB_PAD, D_BKV), BF16), pltpu.VMEM((NKB_PAD, D_BKV), BF16)],
        compiler_params=_params(1, vmem=60 * 1024 * 1024),
        name="ffn_attn_sample",
    )(x, wg, wu, wd, g, b, qa, kan, van, cache_ak, cache_av, cnt_c, cnt_n, qb, kbn, vbn, cache_bk, cache_bv,
      mask_b, sink_col)


def _attn_out_prompt_kernel(o1_ref, l1_ref, o4_ref, l4_ref, o16_ref, l16_ref, ob_ref, y_ref, w_ref, g_ref, b_ref,
                            out_ref, slab_ref, oa_ref):
    tm = y_ref.shape[0]
    n_chunks = D_A // LANES
    slabs = {}
    base = 0
    for name, dil, src, width in (("o4", 4, o4_ref, n_chunks), ("l4", 4, l4_ref, 1),
                                  ("o16", 16, o16_ref, n_chunks), ("l16", 16, l16_ref, 1)):
        slabs[name] = base
        for r in range(dil):
            for c in range(width):
                piece = src[r, :, c * LANES:(c + 1) * LANES]
                slab_ref[base + c, pl.ds(r, tm // dil, stride=dil), :] = piece.astype(F32)
        base += width
    l1, l4, l16 = l1_ref[...], slab_ref[slabs["l4"]], slab_ref[slabs["l16"]]
    m = jnp.maximum(jnp.maximum(l1, l4), l16)
    e1, e4, e16 = jnp.exp2(l1 - m), jnp.exp2(l4 - m), jnp.exp2(l16 - m)
    inv = 1.0 / (e1 + e4 + e16)
    row = lax.broadcasted_iota(jnp.int32, (LANES, D_A), 0)
    head_of_col = jnp.right_shift(lax.broadcasted_iota(jnp.int32, (LANES, D_A), 1), int(math.log2(HEAD_DIM)))
    spread = jnp.where(row == head_of_col * LSE_LANES, 1.0, 0.0).astype(BF16)

    def per_head_lanes(w):
        hi = w.astype(BF16)
        lo = (w - hi.astype(F32)).astype(BF16)
        return _dot(hi, spread) + _dot(lo, spread)

    w1, w4, w16 = per_head_lanes(e1 * inv), per_head_lanes(e4 * inv), per_head_lanes(e16 * inv)
    for c in range(n_chunks):
        sl = slice(c * LANES, (c + 1) * LANES)
        oa = (w1[:, sl] * o1_ref[:, sl].astype(F32) + w4[:, sl] * slab_ref[slabs["o4"] + c]
              + w16[:, sl] * slab_ref[slabs["o16"] + c])
        oa_ref[:, sl] = oa.astype(BF16)
    mix = _dot(oa_ref[...], w_ref[0:D_A, :]) + _dot(ob_ref[...], w_ref[D_A:, :])
    out_ref[...] = _layer_norm(DN_ALPHA * y_ref[...] + mix, g_ref[...], b_ref[...])


def _attn_out_prompt(pats, ob, y, w, g, b, seq):
    n = y.shape[0]
    tm = ROW_TILE
    tps = seq // tm
    half = pl.BlockSpec((tm, D_A), lambda i: (i, 0))
    full = pl.BlockSpec((tm, D_MODEL), lambda i: (i, 0))
    lse1 = pl.BlockSpec((tm, LANES), lambda i: (i, 0))
    planes = lambda dil, width: pl.BlockSpec((None, dil, tm // dil, width), lambda i: (i // tps, 0, i % tps, 0))
    (o1, l1), (o4, l4), (o16, l16) = pats
    return pl.pallas_call(
        _attn_out_prompt_kernel,
        grid=(n // tm,),
        in_specs=[half, lse1, planes(4, D_A), planes(4, LANES), planes(16, D_A), planes(16, LANES), half, full,
                  _resident((D_MODEL, D_MODEL)), _resident((1, D_MODEL)), _resident((1, D_MODEL))],
        out_specs=full,
        out_shape=jax.ShapeDtypeStruct((n, D_MODEL), F32),
        scratch_shapes=[pltpu.VMEM((2 * (D_A // LANES + 1), tm, LANES), F32), pltpu.VMEM((tm, D_A), BF16)],
        compiler_params=_params(1),
        name="attn_out_prompt",
    )(o1, l1, o4, l4, o16, l16, ob, y, w, g, b)


def _mix_out_kernel(o_ref, y_ref, w_ref, g_ref, b_ref, out_ref):
    mix = _dot(o_ref[...].astype(BF16), w_ref[...])
    out_ref[...] = _layer_norm(DN_ALPHA * y_ref[...] + mix, g_ref[...], b_ref[...])


def _attn_out_sample(o, y, w, g, b):
    n = y.shape[0]
    tm = min(ROW_TILE, n)
    full = pl.BlockSpec((tm, D_MODEL), lambda i: (i, 0))
    return pl.pallas_call(
        _mix_out_kernel,
        grid=(n // tm,),
        in_specs=[full, full, _resident((D_MODEL, D_MODEL)), _resident((1, D_MODEL)), _resident((1, D_MODEL))],
        out_specs=full,
        out_shape=jax.ShapeDtypeStruct((n, D_MODEL), F32),
        compiler_params=_params(1),
        name="attn_out_sample",
    )(o, y, w, g, b)


def _ssm_discretize(lam_re, lam_im, log_dt, b_re, b_im):
    dt = jnp.exp(log_dt.astype(F32))[:, None]
    lr, li = lam_re.astype(F32), lam_im.astype(F32)
    mag = jnp.exp(lr * dt)
    ab_re, ab_im = mag * jnp.cos(li * dt), mag * jnp.sin(li * dt)
    nr, ni = ab_re - 1.0, ab_im
    den = lr * lr + li * li
    fr, fi = (nr * lr + ni * li) / den, (ni * lr - nr * li) / den
    bb_re = fr[..., None] * b_re - fi[..., None] * b_im
    bb_im = fr[..., None] * b_im + fi[..., None] * b_re
    return ab_re, ab_im, bb_re, bb_im


def _ssm_matrices(bb_re, bb_im, c_re, c_im):
    gpc = LANES // SSM_GROUP
    eye = jnp.eye(gpc, dtype=F32)

    def in_blocks(bb):
        a = bb.reshape(N_LCHUNK, gpc, SSM_STATE, SSM_GROUP)
        return jnp.einsum("jgpn,gh->jgnhp", a, eye).reshape(N_LCHUNK, LANES, ST_CHUNK)

    def out_blocks(cc):
        a = cc.reshape(N_LCHUNK, gpc, SSM_GROUP, SSM_STATE)
        return jnp.einsum("jgnp,gh->jgphn", a, eye).reshape(N_LCHUNK, ST_CHUNK, LANES)

    bmat = jnp.concatenate([in_blocks(bb_re), in_blocks(bb_im)], -1)
    cmat = jnp.concatenate([out_blocks(c_re), -out_blocks(c_im)], 1)
    return bmat, cmat


CHUNK = 8


def _dot3(a, b):
    a_hi, b_hi = a.astype(BF16), b.astype(BF16)
    a_lo, b_lo = (a - a_hi.astype(F32)).astype(BF16), (b - b_hi.astype(F32)).astype(BF16)
    return _dot(a_hi, b_hi) + _dot(a_hi, b_lo) + _dot(a_lo, b_hi)


def _chunk_weights_kernel(ar_ref, ai_ref, acr_ref, aci_ref, bmat_ref, cmat_ref,
                          we_ref, ws_ref, wi_ref, a8r_ref, a8i_ref):
    def powers(r, i, n):
        out = [(jnp.ones_like(r), jnp.zeros_like(r))]
        for _ in range(n):
            out.append(_cmul(out[-1][0], out[-1][1], r, i))
        return out

    row_pow = powers(ar_ref[...], ai_ref[...], CHUNK)
    col_pow = powers(acr_ref[...], aci_ref[...], CHUNK)
    b_re, b_im = bmat_ref[:, 0:ST_CHUNK], bmat_ref[:, ST_CHUNK:]
    c_re, c_im = cmat_ref[0:ST_CHUNK, :], -cmat_ref[ST_CHUNK:, :]

    def scaled_b(power):
        pr, pi = row_pow[power]
        return jnp.concatenate([b_re * pr - b_im * pi, b_re * pi + b_im * pr], axis=1)

    scaled = [scaled_b(tau) for tau in range(CHUNK)]
    for tau in range(CHUNK):
        we_ref[(CHUNK - 1 - tau) * LANES:(CHUNK - tau) * LANES, :] = scaled[tau].astype(BF16)
    all_taps = _dot3(jnp.concatenate(scaled, axis=0), cmat_ref[...]).astype(BF16)
    taps = [all_taps[tau * LANES:(tau + 1) * LANES] for tau in range(CHUNK)]
    for k in range(CHUNK):
        cols = slice(k * LANES, (k + 1) * LANES)
        qr, qi = col_pow[k + 1]
        ws_ref[0:ST_CHUNK, cols] = (c_re * qr - c_im * qi).astype(BF16)
        ws_ref[ST_CHUNK:, cols] = (-(c_re * qi + c_im * qr)).astype(BF16)
    zero = jnp.zeros((LANES, LANES), BF16)
    for k_in in range(CHUNK):
        for k_out in range(CHUNK):
            wi_ref[k_in * LANES:(k_in + 1) * LANES, k_out * LANES:(k_out + 1) * LANES] = (
                taps[k_out - k_in] if k_out >= k_in else zero)
    a8r_ref[...], a8i_ref[...] = row_pow[CHUNK]


def _chunk_weights(a_re, a_im, bmat, cmat):
    cols = lambda a: jnp.broadcast_to(a.reshape(N_LCHUNK, ST_CHUNK, 1), (N_LCHUNK, ST_CHUNK, LANES))
    per_j = lambda r, c: pl.BlockSpec((None, r, c), lambda j: (j, 0, 0))
    wide = 2 * ST_CHUNK
    return pl.pallas_call(
        _chunk_weights_kernel,
        grid=(N_LCHUNK,),
        in_specs=[per_j(1, ST_CHUNK), per_j(1, ST_CHUNK), per_j(ST_CHUNK, LANES), per_j(ST_CHUNK, LANES),
                  per_j(LANES, wide), per_j(wide, LANES)],
        out_specs=[per_j(CHUNK * LANES, wide), per_j(wide, CHUNK * LANES), per_j(CHUNK * LANES, CHUNK * LANES),
                   per_j(1, ST_CHUNK), per_j(1, ST_CHUNK)],
        out_shape=[jax.ShapeDtypeStruct((N_LCHUNK, CHUNK * LANES, wide), BF16),
                   jax.ShapeDtypeStruct((N_LCHUNK, wide, CHUNK * LANES), BF16),
                   jax.ShapeDtypeStruct((N_LCHUNK, CHUNK * LANES, CHUNK * LANES), BF16),
                   jax.ShapeDtypeStruct((N_LCHUNK, 1, ST_CHUNK), F32),
                   jax.ShapeDtypeStruct((N_LCHUNK, 1, ST_CHUNK), F32)],
        compiler_params=_params(1),
        name="ssm_chunk_weights",
    )(a_re, a_im, cols(a_re), cols(a_im), bmat, cmat)


SEG_TILE = ROW_TILE // N_SEG


def _ssm_in_prompt_kernel(x_ref, w_ref, o_ref):
    x = x_ref[...].reshape(N_SEG * SEG_TILE, D_MODEL)
    u = _dot(x.astype(BF16), w_ref[...])
    for s in range(N_SEG):
        for c in range(N_LCHUNK):
            o_ref[c, pl.ds(s, SEG_TILE, stride=N_SEG), :] = u[s * SEG_TILE:(s + 1) * SEG_TILE, c * LANES:(c + 1) * LANES]


def _ssm_in_prompt(y, w):
    return pl.pallas_call(
        _ssm_in_prompt_kernel,
        grid=(BATCH, SEG_LEN // SEG_TILE),
        in_specs=[pl.BlockSpec((None, N_SEG, SEG_TILE, D_MODEL), lambda b, t: (b, 0, t, 0)),
                  _resident((D_MODEL, D_MODEL))],
        out_specs=pl.BlockSpec((None, N_LCHUNK, ROW_TILE, LANES), lambda b, t: (b, 0, t, 0)),
        out_shape=jax.ShapeDtypeStruct((BATCH, N_LCHUNK, SEQ, LANES), F32),
        compiler_params=_params(2),
        name="ssm_in_prompt",
    )(y.reshape(BATCH, N_SEG, SEG_LEN, D_MODEL), w)


def _ssm_in_sample_kernel(x_ref, w_ref, o_ref, slab_ref):
    u = _dot(x_ref[...].astype(BF16), w_ref[...])
    for c in range(N_LCHUNK):
        slab_ref[c] = u[:, c * LANES:(c + 1) * LANES]
    for l in range(DEC_SEQ):
        for c in range(N_LCHUNK):
            o_ref[c, l * DEC_BATCH:(l + 1) * DEC_BATCH, :] = slab_ref[c, pl.ds(l, DEC_BATCH, stride=DEC_SEQ), :]


def _ssm_in_sample(y, w):
    n = DEC_BATCH * DEC_SEQ
    return pl.pallas_call(
        _ssm_in_sample_kernel,
        grid=(1,),
        in_specs=[_resident((n, D_MODEL)), _resident((D_MODEL, D_MODEL))],
        out_specs=pl.BlockSpec((N_LCHUNK, n, LANES), lambda i: (0, 0, 0)),
        out_shape=jax.ShapeDtypeStruct((N_LCHUNK, n, LANES), F32),
        scratch_shapes=[pltpu.VMEM((N_LCHUNK, n, LANES), F32)],
        compiler_params=_params(1),
        name="ssm_in_sample",
    )(y, w)


def _cmul(ar, ai, br, bi):
    return ar * br - ai * bi, ar * bi + ai * br


def _scan_prompt_kernel(u_ref, we_ref, ws_ref, wi_ref, a8r_ref, a8i_ref, d_ref, h0r_ref, h0i_ref,
                        y_ref, hnr_ref, hni_ref, e_s, hs_s):
    n_chunks = u_ref.shape[0]
    rows = n_chunks * N_SEG
    u_flat = jnp.concatenate([u_ref[:, k].reshape(rows, LANES) for k in range(CHUNK)], axis=1)
    ub = u_flat.astype(BF16)
    e_s[...] = _dot(ub, we_ref[...])
    a8r1, a8i1 = a8r_ref[...], a8i_ref[...]
    a8r = jnp.broadcast_to(a8r1, (N_SEG, ST_CHUNK))
    a8i = jnp.broadcast_to(a8i1, (N_SEG, ST_CHUNK))

    def advance(row, hr, hi):
        er = e_s[pl.ds(row, N_SEG), 0:ST_CHUNK]
        ei = e_s[pl.ds(row, N_SEG), ST_CHUNK:2 * ST_CHUNK]
        return a8r * hr - a8i * hi + er, a8r * hi + a8i * hr + ei

    def pass1(c, carry):
        return advance(pl.multiple_of(c * N_SEG, N_SEG), *carry)

    zero = jnp.zeros((N_SEG, ST_CHUNK), F32)
    er, ei = lax.fori_loop(0, n_chunks, pass1, (zero, zero), unroll=8)

    pr, pi = a8r1, a8i1
    for _ in range(int(math.log2(n_chunks))):
        pr, pi = _cmul(pr, pi, pr, pi)
    hr, hi = h0r_ref[...], h0i_ref[...]
    starts_r, starts_i = [], []
    for s in range(N_SEG):
        starts_r.append(hr)
        starts_i.append(hi)
        gr, gi = _cmul(pr, pi, hr, hi)
        hr, hi = gr + er[s:s + 1], gi + ei[s:s + 1]
    hnr_ref[...] = hr
    hni_ref[...] = hi
    init = (jnp.concatenate(starts_r, 0), jnp.concatenate(starts_i, 0))

    def pass2(c, carry):
        row = pl.multiple_of(c * N_SEG, N_SEG)
        hs_s[pl.ds(row, N_SEG), 0:ST_CHUNK] = carry[0]
        hs_s[pl.ds(row, N_SEG), ST_CHUNK:2 * ST_CHUNK] = carry[1]
        return advance(row, *carry)

    lax.fori_loop(0, n_chunks, pass2, init, unroll=8)
    y = _dot(hs_s[...].astype(BF16), ws_ref[...]) + _dot(ub, wi_ref[...]) + d_ref[...] * u_flat
    for k in range(CHUNK):
        y_ref[:, k] = y[:, k * LANES:(k + 1) * LANES].reshape(n_chunks, N_SEG, LANES)


def _scan_prompt(u, weights, d_skip, h0r, h0i):
    bsz, _, seq, _ = u.shape
    n_chunks = seq // (CHUNK * N_SEG)
    rows = n_chunks * N_SEG
    wide = 2 * ST_CHUNK
    split = lambda a: a.reshape(bsz, N_LCHUNK, n_chunks, CHUNK, N_SEG, LANES)
    chunk = pl.BlockSpec((None, None, n_chunks, CHUNK, N_SEG, LANES), lambda j, b: (b, j, 0, 0, 0, 0))
    per_j = lambda r, c: pl.BlockSpec((None, r, c), lambda j, b: (j, 0, 0))
    state = pl.BlockSpec((None, 1, ST_CHUNK), lambda j, b: (b, 0, j))
    y, hr, hi = pl.pallas_call(
        _scan_prompt_kernel,
        grid=(N_LCHUNK, bsz),
        in_specs=[chunk, per_j(CHUNK * LANES, wide), per_j(wide, CHUNK * LANES), per_j(CHUNK * LANES, CHUNK * LANES),
                  per_j(1, ST_CHUNK), per_j(1, ST_CHUNK), per_j(1, CHUNK * LANES), state, state],
        out_specs=[chunk, state, state],
        out_shape=[jax.ShapeDtypeStruct((bsz, N_LCHUNK, n_chunks, CHUNK, N_SEG, LANES), F32),
                   jax.ShapeDtypeStruct((bsz, 1, N_STATE), F32), jax.ShapeDtypeStruct((bsz, 1, N_STATE), F32)],
        scratch_shapes=[pltpu.VMEM((rows, wide), F32), pltpu.VMEM((rows, wide), F32)],
        compiler_params=_params(2),
        name="ssm_scan_prompt",
    )(split(u), *weights, d_skip, h0r, h0i)
    return y.reshape(u.shape), hr, hi


def _scan_sample_kernel(u_ref, bmat_ref, cmat_ref, are_ref, aim_ref, d_ref, h0r_ref, h0i_ref,
                        y_ref, hnr_ref, hni_ref, h_s):
    a_re, a_im = are_ref[...], aim_ref[...]
    hr, hi = h0r_ref[...].T, h0i_ref[...].T
    u = u_ref[...]
    bu = _dot(u.astype(BF16), bmat_ref[...])
    for l in range(DEC_SEQ):
        rows = slice(l * DEC_BATCH, (l + 1) * DEC_BATCH)
        gr, gi = _cmul(a_re, a_im, hr, hi)
        hr, hi = gr + bu[rows, :ST_CHUNK], gi + bu[rows, ST_CHUNK:]
        h_s[rows, 0:ST_CHUNK] = hr.astype(BF16)
        h_s[rows, ST_CHUNK:] = hi.astype(BF16)
    y_ref[...] = _dot(h_s[...], cmat_ref[...]) + d_ref[...] * u
    hnr_ref[...] = hr.T
    hni_ref[...] = hi.T


def _scan_sample(u, bmat, cmat, a_re, a_im, d_skip, h0r, h0i):
    n = DEC_SEQ * DEC_BATCH
    chunk = pl.BlockSpec((None, n, LANES), lambda j: (j, 0, 0))
    per_j = lambda r, c: pl.BlockSpec((None, r, c), lambda j: (j, 0, 0))
    state = pl.BlockSpec((ST_CHUNK, DEC_BATCH), lambda j: (j, 0))
    return pl.pallas_call(
        _scan_sample_kernel,
        grid=(N_LCHUNK,),
        in_specs=[chunk, per_j(LANES, 2 * ST_CHUNK), per_j(2 * ST_CHUNK, LANES), per_j(1, ST_CHUNK),
                  per_j(1, ST_CHUNK), per_j(1, LANES), state, state],
        out_specs=[chunk, state, state],
        out_shape=[jax.ShapeDtypeStruct((N_LCHUNK, n, LANES), F32),
                   jax.ShapeDtypeStruct((N_STATE, DEC_BATCH), F32), jax.ShapeDtypeStruct((N_STATE, DEC_BATCH), F32)],
        scratch_shapes=[pltpu.VMEM((n, 2 * ST_CHUNK), BF16)],
        compiler_params=_params(1),
        name="ssm_scan_sample",
    )(u, bmat, cmat, a_re, a_im, d_skip, h0r, h0i)


def _ssm_out_kernel(sample, s_ref, y_ref, wglu_ref, bglu_ref, wout_ref, g_ref, b_ref, out_ref, z_ref):
    n_rows = z_ref.shape[1]
    if sample:
        parts = [(pl.ds(l, DEC_BATCH, stride=DEC_SEQ), slice(l * DEC_BATCH, (l + 1) * DEC_BATCH))
                 for l in range(DEC_SEQ)]
    else:
        parts = [(slice(s * SEG_TILE, (s + 1) * SEG_TILE), pl.ds(s, SEG_TILE, stride=N_SEG)) for s in range(N_SEG)]
    for c in range(N_LCHUNK):
        for tok_rows, slab_rows in parts:
            z_ref[c, tok_rows, :] = s_ref[c, slab_rows, :]
    z = jax.nn.gelu(jnp.concatenate([z_ref[c] for c in range(N_LCHUNK)], axis=1))
    gate = jax.nn.sigmoid(_dot(z.astype(BF16), wglu_ref[...]) + bglu_ref[...])
    mix = _dot((z * gate).astype(BF16), wout_ref[...])
    res = y_ref[...].reshape(n_rows, D_MODEL)
    out = _layer_norm(DN_ALPHA * res + mix, g_ref[...], b_ref[...])
    out_ref[...] = out.reshape(out_ref.shape)


def _ssm_out(s, y, s_spec, y_spec, grid, rows, sample, w_glu, b_glu, w_out, g, b, name):
    return pl.pallas_call(
        functools.partial(_ssm_out_kernel, sample),
        grid=grid,
        in_specs=[s_spec, y_spec, _resident((D_MODEL, D_MODEL)), _resident((1, D_MODEL)),
                  _resident((D_MODEL, D_MODEL)), _resident((1, D_MODEL)), _resident((1, D_MODEL))],
        out_specs=y_spec,
        out_shape=jax.ShapeDtypeStruct(y.shape, F32),
        scratch_shapes=[pltpu.VMEM((N_LCHUNK, rows, LANES), F32)],
        compiler_params=_params(len(grid)),
        name=name,
    )(s, y, w_glu, b_glu, w_out, g, b)


def _attn_prompt(yp, w_in, sinks, w_out, g, b):
    cos_p, sin_p = _rope_tables(jnp.arange(SEQ))
    tiles_per_seq = SEQ // ROW_TILE
    qa, ka, va, qb, kb, vb, *extra = _attn_proj(
        yp, w_in, cos_p, sin_p, lambda i: i % tiles_per_seq, BF16, prompt_seq=(BATCH, SEQ))
    dilated, tails = extra[:6], extra[6:]
    seq3 = lambda a: a.reshape(BATCH, SEQ, a.shape[-1])
    plane1 = lambda a: a.reshape(BATCH, 1, SEQ, D_A)
    o1, l1 = _band_a(plane1(qa), plane1(ka), plane1(va))
    pats = [(o1.reshape(BATCH * SEQ, D_A), l1.reshape(BATCH * SEQ, LANES))]
    for i in range(len(DILATIONS) - 1):
        pats.append(_band_a(*dilated[3 * i:3 * i + 3]))
    ob = _band_b(seq3(qb), seq3(kb), seq3(vb), sinks)
    yp = _attn_out_prompt(pats, ob, yp, w_out, g, b, SEQ)
    heads = lambda a, nh: jnp.transpose(a.reshape(BATCH, nh, HEAD_DIM, a.shape[-1]), (0, 3, 1, 2))[None]
    prompt_cache = (heads(tails[0], N_HEADS_A), heads(tails[1], N_HEADS_A),
                    heads(tails[2], N_KV_B), heads(tails[3], N_KV_B))
    return yp, prompt_cache


def _attn_sample_path(ys, cache_ak, cache_av, cache_bk, cache_bv, w_in, sinks, w_out, g, b, yp, ffn_args):
    cos_s, sin_s = _rope_tables(PAST_LEN + jnp.arange(DEC_SEQ))
    reps = DEC_BATCH * DEC_SEQ // DEC_SEQ
    cos_s, sin_s = jnp.tile(cos_s, (reps, 1)), jnp.tile(sin_s, (reps, 1))
    qa, ka, va, qb, kb, vb = _attn_proj(ys, w_in, cos_s, sin_s, lambda i: i, F32)
    stored = lambda a: jnp.transpose(a, (0, 2, 3, 1))
    yp, o = _ffn_with_sample_attention(
        yp, *ffn_args, qa, ka, va, stored(cache_ak), stored(cache_av),
        qb, kb, vb, stored(cache_bk), stored(cache_bv), sinks)
    ys = _attn_out_sample(o, ys, w_out, g, b)
    new = lambda a, nh: a.reshape(1, DEC_BATCH, DEC_SEQ, nh, HEAD_DIM)
    sample_cache = (new(ka, N_HEADS_A), new(va, N_HEADS_A), new(kb, N_KV_B), new(vb, N_KV_B))
    return yp, ys, sample_cache


def _attn_layer(yp, ys, cache_ak, cache_av, cache_bk, cache_bv, w_in, sinks, w_out, g, b, ffn_args):
    w_in = w_in.astype(BF16)
    w_out = w_out.astype(BF16)
    yp, prompt_cache = _attn_prompt(yp, w_in, sinks, w_out, g, b)
    yp, ys, sample_cache = _attn_sample_path(ys, cache_ak, cache_av, cache_bk, cache_bv, w_in, sinks, w_out, g, b,
                                             yp, ffn_args)
    return yp, ys, prompt_cache, sample_cache


def _ssm_layer(yp, ys, state_re, state_im, w_in, lam_re, lam_im, log_dt, b_re, b_im, c_re, c_im, d_skip,
               w_glu, b_glu, w_out, g, b):
    w_in, w_glu, w_out = w_in.astype(BF16), w_glu.astype(BF16), w_out.astype(BF16)
    b_glu = b_glu.reshape(1, D_MODEL)
    mats = _ssm_prepare(lam_re, lam_im, log_dt, b_re, b_im, c_re, c_im, d_skip)
    yp, prompt_state = _ssm_prompt(yp, w_in, mats, w_glu, b_glu, w_out, g, b)
    ys, sample_state = _ssm_sample(ys, state_re, state_im, w_in, mats, w_glu, b_glu, w_out, g, b)
    return yp, ys, prompt_state, sample_state


def _ssm_prepare(lam_re, lam_im, log_dt, b_re, b_im, c_re, c_im, d_skip):
    ab_re, ab_im, bb_re, bb_im = _ssm_discretize(lam_re, lam_im, log_dt, b_re, b_im)
    bmat, cmat = _ssm_matrices(bb_re, bb_im, c_re, c_im)
    a_re = ab_re.reshape(N_LCHUNK, 1, ST_CHUNK)
    a_im = ab_im.reshape(N_LCHUNK, 1, ST_CHUNK)
    d3 = d_skip.astype(F32).reshape(N_LCHUNK, 1, LANES)
    prompt = (_chunk_weights(a_re, a_im, bmat, cmat), jnp.tile(d3, (1, 1, CHUNK)))
    sample = (bmat.astype(BF16), cmat.astype(BF16), a_re, a_im, d3)
    return prompt, sample


def _ssm_prompt(yp, w_in, mats, w_glu, b_glu, w_out, g, b):
    weights, d_tiled = mats[0]
    up = _ssm_in_prompt(yp, w_in)
    zero = jnp.zeros((BATCH, 1, N_STATE), F32)
    sp, pr, pi = _scan_prompt(up, weights, d_tiled, zero, zero)
    yp = _ssm_out(sp, yp.reshape(BATCH, N_SEG, SEG_LEN, D_MODEL),
                  pl.BlockSpec((None, N_LCHUNK, ROW_TILE, LANES), lambda bb, t: (bb, 0, t, 0)),
                  pl.BlockSpec((None, N_SEG, SEG_TILE, D_MODEL), lambda bb, t: (bb, 0, t, 0)),
                  (BATCH, SEG_LEN // SEG_TILE), ROW_TILE, False, w_glu, b_glu, w_out, g, b,
                  "ssm_out_prompt").reshape(BATCH * SEQ, D_MODEL)
    prompt_state = (pr.reshape(1, BATCH, N_SSM_GROUPS, SSM_STATE), pi.reshape(1, BATCH, N_SSM_GROUPS, SSM_STATE))
    return yp, prompt_state


def _ssm_sample(ys, state_re, state_im, w_in, mats, w_glu, b_glu, w_out, g, b):
    bmat, cmat, a_re, a_im, d3 = mats[1]
    us = _ssm_in_sample(ys, w_in)
    stored = lambda a: jnp.transpose(a, (1, 2, 0)).reshape(N_STATE, DEC_BATCH)
    logical = lambda a: jnp.transpose(a.reshape(N_SSM_GROUPS, SSM_STATE, DEC_BATCH), (2, 0, 1))[None]
    ss, sr, si = _scan_sample(us, bmat, cmat, a_re, a_im, d3, stored(state_re), stored(state_im))
    n = DEC_BATCH * DEC_SEQ
    ys = _ssm_out(ss, ys,
                  pl.BlockSpec((N_LCHUNK, n, LANES), lambda i: (0, 0, 0)),
                  pl.BlockSpec((n, D_MODEL), lambda i: (0, 0)),
                  (1,), n, True, w_glu, b_glu, w_out, g, b, "ssm_out_sample")
    return ys, (logical(sr), logical(si))


def kernel(x_prompt, x_sample, cache_a_k, cache_a_v, cache_b_k, cache_b_v, state_c_re, state_c_im, ln_g, ln_b, ffn_w_gate, ffn_w_up, ffn_w_down, attn_w_in, attn_sinks, attn_w_out, ssm_w_in, ssm_lambda_re, ssm_lambda_im, ssm_log_dt, ssm_b_re, ssm_b_im, ssm_c_re, ssm_c_im, ssm_d, ssm_w_glu, ssm_b_glu, ssm_w_out):
    yp = x_prompt.reshape(BATCH * SEQ, D_MODEL)
    ys = x_sample.reshape(DEC_BATCH * DEC_SEQ, D_MODEL)
    ln = lambda l, k: (ln_g[l, k].reshape(1, D_MODEL), ln_b[l, k].reshape(1, D_MODEL))

    wg, wu, wd = ffn_w_gate.astype(BF16), ffn_w_up.astype(BF16), ffn_w_down.astype(BF16)

    def ffn_pair(yp, ys, l, k, ln_idx):
        g, b = ln(l, ln_idx)
        return _ffn_pair(yp, ys, wg, wu, wd, g, b, l, k)

    yp, ys = ffn_pair(yp, ys, 0, 0, 0)
    yp, ys, p_cache, s_cache = _attn_layer(yp, ys, cache_a_k[0], cache_a_v[0], cache_b_k[0], cache_b_v[0],
                                           attn_w_in[0], attn_sinks[0], attn_w_out[0], *ln(0, 1),
                                           ffn_args=(wg, wu, wd, *ln(0, 2), 0, 1))
    ys = _ffn(ys, wg, wu, wd, *ln(0, 2), 0, 1)
    yp, ys = ffn_pair(yp, ys, 1, 0, 0)
    yp, ys, p_state, s_state = _ssm_layer(yp, ys, state_c_re[0], state_c_im[0], ssm_w_in[0], ssm_lambda_re[0],
                                          ssm_lambda_im[0], ssm_log_dt[0], ssm_b_re[0], ssm_b_im[0], ssm_c_re[0],
                                          ssm_c_im[0], ssm_d[0], ssm_w_glu[0], ssm_b_glu[0], ssm_w_out[0], *ln(1, 1))
    yp, ys = ffn_pair(yp, ys, 1, 1, 2)
    return (yp.reshape(BATCH, SEQ, D_MODEL), ys.reshape(DEC_BATCH, DEC_SEQ, D_MODEL),
            *p_cache, *p_state, *s_cache, *s_state)
```

```python
import functools
import math

import jax
import jax.numpy as jnp
from jax import lax
from jax.experimental import pallas as pl
from jax.experimental.pallas import tpu as pltpu

F32 = jnp.float32
BF16 = jnp.bfloat16

D_MODEL = 1024
BATCH = 4
SEQ = 4096
DEPTH = 2
DEC_BATCH = 128
DEC_SEQ = 8
PAST_LEN = 16384
HEAD_DIM = 64
N_HEADS_A = 8
DILATIONS = (1, 4, 16)
WIN_A = 2048
N_HEADS_B = 8
N_KV_B = 2
WIN_B = 128
ROPE_THETA = 10000.0
D_A = N_HEADS_A * HEAD_DIM
D_BQ = N_HEADS_B * HEAD_DIM
D_BKV = N_KV_B * HEAD_DIM
D_IN_ATTN = 3 * D_A + D_BQ + 2 * D_BKV
SSM_GROUP = 16
N_SSM_GROUPS = D_MODEL // SSM_GROUP
SSM_STATE = 64
N_STATE = N_SSM_GROUPS * SSM_STATE
D_FF = 2816
DN_ALPHA = (2 * DEPTH) ** 0.25
FFN_RES = 0.5
LN_EPS = 1e-5
ATTN_SCALE = HEAD_DIM ** -0.5
LOG2E = math.log2(math.e)
LSE_LANES = 16

LANES = 128
SUBLANES = 8
MXU_N = 256
VMEM_LIMIT = 56 * 1024 * 1024

ROW_TILE = 512
WIDE_ROW_TILE = 1024
FFN_ROW_TILE = 1024
FF_CHUNK = MXU_N
TQ = 128
N_SEG = SUBLANES
SEG_LEN = SEQ // N_SEG
N_LCHUNK = D_MODEL // LANES
ST_CHUNK = N_STATE // N_LCHUNK
NK_PAD = WIN_A + LANES
NKB_PAD = 2 * WIN_B

NEG_INF = float("-inf")


def _params(n_axes, vmem=VMEM_LIMIT):
    return pltpu.CompilerParams(dimension_semantics=("arbitrary",) * n_axes, vmem_limit_bytes=vmem)


def _resident(shape):
    return pl.BlockSpec(shape, lambda *_: (0,) * len(shape), pipeline_mode=pl.Buffered(1))


def _layer_norm(x, g, b):
    mu = jnp.mean(x, -1, keepdims=True)
    xc = x - mu
    var = jnp.mean(xc * xc, -1, keepdims=True)
    return xc * lax.rsqrt(var + LN_EPS) * g + b


def _dot(a, b):
    return jnp.dot(a, b, preferred_element_type=F32)


def _dot_nt(a, b):
    return lax.dot_general(a, b, (((1,), (1,)), ((), ())), preferred_element_type=F32)


def _ffn_kernel(x_ref, wg_ref, wu_ref, wd_ref, g_ref, b_ref, o_ref, h_ref):
    x = x_ref[...]
    xb = x.astype(BF16)
    for c in range(D_FF // FF_CHUNK):
        sl = slice(c * FF_CHUNK, (c + 1) * FF_CHUNK)
        gate = _dot(xb, wg_ref[:, sl])
        up = _dot(xb, wu_ref[:, sl])
        h_ref[:, sl] = (gate * jax.nn.sigmoid(gate) * up).astype(BF16)
    y = DN_ALPHA * x + FFN_RES * _dot(h_ref[...], wd_ref[...])
    o_ref[...] = _layer_norm(y, g_ref[...], b_ref[...])


def _ffn(x, wg, wu, wd, g, b, layer=0, which=0):
    n = x.shape[0]
    tm = min(FFN_ROW_TILE, n)
    row = pl.BlockSpec((tm, D_MODEL), lambda i: (i, 0))
    if wg.ndim == 4:
        pick = lambda r, c: pl.BlockSpec((None, None, r, c), lambda i: (layer, which, 0, 0),
                                         pipeline_mode=pl.Buffered(1))
    else:
        pick = lambda r, c: _resident((r, c))
    return pl.pallas_call(
        _ffn_kernel,
        grid=(n // tm,),
        in_specs=[row, pick(D_MODEL, D_FF), pick(D_MODEL, D_FF), pick(D_FF, D_MODEL),
                  _resident((1, D_MODEL)), _resident((1, D_MODEL))],
        out_specs=row,
        out_shape=jax.ShapeDtypeStruct((n, D_MODEL), F32),
        scratch_shapes=[pltpu.VMEM((tm, D_FF), BF16)],
        compiler_params=_params(1),
        name="ffn",
    )(x, wg, wu, wd, g, b)


def _ffn_pair_kernel(n_first, xp_ref, xs_ref, wg_ref, wu_ref, wd_ref, g_ref, b_ref, op_ref, os_ref, h_ref):
    step = pl.program_id(0)
    weights = (wg_ref, wu_ref, wd_ref, g_ref, b_ref)
    pl.when(step < n_first)(functools.partial(_ffn_kernel, xp_ref, *weights, op_ref, h_ref))
    h_small = h_ref.at[pl.ds(0, xs_ref.shape[0])]
    pl.when(step >= n_first)(functools.partial(_ffn_kernel, xs_ref, *weights, os_ref, h_small))


def _ffn_pair(xp, xs, wg, wu, wd, g, b, layer, which):
    tm, tm2 = min(FFN_ROW_TILE, xp.shape[0]), ROW_TILE
    n_first, n_second = xp.shape[0] // tm, xs.shape[0] // tm2
    first = pl.BlockSpec((tm, D_MODEL), lambda i: (jnp.minimum(i, n_first - 1), 0))
    second = pl.BlockSpec((tm2, D_MODEL), lambda i: (jnp.maximum(i - n_first, 0), 0))
    pick = lambda r, c: pl.BlockSpec((None, None, r, c), lambda i: (layer, which, 0, 0), pipeline_mode=pl.Buffered(1))
    return pl.pallas_call(
        functools.partial(_ffn_pair_kernel, n_first),
        grid=(n_first + n_second,),
        in_specs=[first, second, pick(D_MODEL, D_FF), pick(D_MODEL, D_FF), pick(D_FF, D_MODEL),
                  _resident((1, D_MODEL)), _resident((1, D_MODEL))],
        out_specs=[first, second],
        out_shape=[jax.ShapeDtypeStruct(xp.shape, F32), jax.ShapeDtypeStruct(xs.shape, F32)],
        scratch_shapes=[pltpu.VMEM((tm, D_FF), BF16)],
        compiler_params=_params(1),
        name="ffn_pair",
    )(xp, xs, wg, wu, wd, g, b)


def _rope_tables(pos):
    half = HEAD_DIM // 2
    inv_freq = ROPE_THETA ** (-jnp.arange(half, dtype=F32) / half)
    ang = pos.astype(F32)[:, None] * inv_freq[None, :]
    cos, sin = jnp.cos(ang), jnp.sin(ang)
    cos_t = jnp.concatenate([cos, cos, cos, cos], -1)
    sin_t = jnp.concatenate([-sin, sin, -sin, sin], -1)
    return cos_t, sin_t


def _attn_proj_kernel(tiles_per_seq, x_ref, w_ref, cos_ref, sin_ref, qa_ref, ka_ref, va_ref, qb_ref, kb_ref, vb_ref,
                      *extra):
    xb = x_ref[...].astype(BF16)
    slab_ref = extra[-1] if extra else None
    dilated = extra[:6]
    n_chunks = D_A // LANES
    tm = x_ref.shape[0]

    def keep(tensor, c, val):
        if slab_ref is not None:
            slab_ref[tensor * n_chunks + c] = val
    cos = cos_ref[...]
    sin = sin_ref[...]
    lane = lax.broadcasted_iota(jnp.int32, cos.shape, 1)
    first_half = (lane & (HEAD_DIM // 2)) == 0

    def rope(z):
        rot = jnp.where(first_half, pltpu.roll(z, LANES - HEAD_DIM // 2, 1), pltpu.roll(z, HEAD_DIM // 2, 1))
        return z * cos + rot * sin

    def project(col0, ncols):
        return _dot(xb, w_ref[:, col0:col0 + ncols])

    def rope_chunks(z):
        return [rope(z[:, c * LANES:(c + 1) * LANES]) for c in range(z.shape[1] // LANES)]

    q_scale = ATTN_SCALE * LOG2E if extra else ATTN_SCALE
    col = 0
    for c, r in enumerate(rope_chunks(project(col, D_A))):
        r = r * q_scale
        qa_ref[:, c * LANES:(c + 1) * LANES] = r.astype(qa_ref.dtype)
        keep(0, c, r)
    col += D_A
    for c, r in enumerate(rope_chunks(project(col, D_A))):
        ka_ref[:, c * LANES:(c + 1) * LANES] = r.astype(ka_ref.dtype)
        keep(1, c, r)
    col += D_A
    z = project(col, D_A)
    va_ref[...] = z.astype(va_ref.dtype)
    for c in range(n_chunks):
        keep(2, c, z[:, c * LANES:(c + 1) * LANES])
    col += D_A
    if extra:
        kat_ref, vat_ref, kbt_ref, vbt_ref = extra[6:10]
        tile_in_seq = pl.program_id(0) % tiles_per_seq

        @pl.when(tile_in_seq >= tiles_per_seq - WIN_A // tm)
        def _():
            for c in range(n_chunks):
                kat_ref[c * LANES:(c + 1) * LANES, :] = slab_ref[n_chunks + c].T
                vat_ref[c * LANES:(c + 1) * LANES, :] = slab_ref[2 * n_chunks + c].T

        slab4_ref = extra[-2]
        d1, d2 = DILATIONS[1], DILATIONS[2] // DILATIONS[1]
        plane = tm // d1
        for tensor in range(3):
            out1_ref, out2_ref = dilated[tensor], dilated[3 + tensor]
            for c in range(n_chunks):
                idx = tensor * n_chunks + c
                lanes = slice(c * LANES, (c + 1) * LANES)
                for r in range(d1):
                    rows = slab_ref[idx, pl.ds(r, plane, stride=d1), :]
                    slab4_ref[idx, r * plane:(r + 1) * plane, :] = rows
                    out1_ref[r, :, lanes] = rows.astype(out1_ref.dtype)
                for r in range(d1):
                    for m in range(d2):
                        rows = slab4_ref[idx, pl.ds(r * plane + m, plane // d2, stride=d2), :]
                        out2_ref[r + d1 * m, :, lanes] = rows.astype(out2_ref.dtype)
    for c, r in enumerate(rope_chunks(project(col, D_BQ))):
        qb_ref[:, c * LANES:(c + 1) * LANES] = (r * q_scale).astype(qb_ref.dtype)
    col += D_BQ
    z = project(col, 2 * D_BKV)
    r = rope(z[:, :D_BKV])
    if extra:
        lo = _lane_lo(r.shape)
        for ref, val in ((kb_ref, r), (vb_ref, z[:, D_BKV:])):
            swapped = pltpu.roll(val, HEAD_DIM, 1)
            ref[:, 0:LANES] = jnp.where(lo, val, swapped).astype(ref.dtype)
            ref[:, LANES:] = jnp.where(lo, swapped, val).astype(ref.dtype)
    else:
        kb_ref[...] = r.astype(kb_ref.dtype)
        vb_ref[...] = z[:, D_BKV:].astype(vb_ref.dtype)
    if extra:
        @pl.when(tile_in_seq == tiles_per_seq - 1)
        def _():
            kbt_ref[...] = r[tm - WIN_B:, :].T
            vbt_ref[...] = z[tm - WIN_B:, D_BKV:].T


def _attn_proj(x, w, cos_t, sin_t, table_block, act_dtype, prompt_seq=None):
    n = x.shape[0]
    tm = min(ROW_TILE, n)

    def row(width):
        return pl.BlockSpec((tm, width), lambda i: (i, 0))

    tab = pl.BlockSpec((tm, LANES), lambda i: (table_block(i), 0))
    kv_b = D_BKV if prompt_seq is None else N_KV_B * LANES
    widths = (D_A, D_A, D_A, D_BQ, kv_b, kv_b)
    out_shape = [jax.ShapeDtypeStruct((n, wd), act_dtype) for wd in widths]
    out_specs = [row(wd) for wd in widths]
    scratch = []
    tps = None
    if prompt_seq is not None:
        bsz, seq = prompt_seq
        tps = seq // tm
        for dil in DILATIONS[1:]:
            out_shape += [jax.ShapeDtypeStruct((bsz, dil, seq // dil, D_A), BF16)] * 3
            out_specs += [pl.BlockSpec((None, dil, tm // dil, D_A), lambda i: (i // tps, 0, i % tps, 0))] * 3
        first_tail = tps - WIN_A // tm
        out_shape += [jax.ShapeDtypeStruct((bsz, D_A, WIN_A), F32)] * 2
        out_specs += [pl.BlockSpec((None, D_A, tm), lambda i: (i // tps, 0, jnp.maximum(i % tps - first_tail, 0)))] * 2
        out_shape += [jax.ShapeDtypeStruct((bsz, D_BKV, WIN_B), F32)] * 2
        out_specs += [pl.BlockSpec((None, D_BKV, WIN_B), lambda i: (i // tps, 0, 0))] * 2
        assert DILATIONS[2] == DILATIONS[1] ** 2
        scratch = [pltpu.VMEM((3 * D_A // LANES, tm, LANES), F32)] * 2
    return pl.pallas_call(
        functools.partial(_attn_proj_kernel, tps),
        grid=(n // tm,),
        in_specs=[row(D_MODEL), _resident((D_MODEL, D_IN_ATTN)), tab, tab],
        out_specs=out_specs,
        out_shape=out_shape,
        scratch_shapes=scratch,
        compiler_params=_params(1),
        name="attn_proj",
    )(x, w, cos_t, sin_t)


def _lane_lo(shape):
    return lax.broadcasted_iota(jnp.int32, shape, 1) < HEAD_DIM


def _half_masks_bf16():
    lo = jnp.where(_lane_lo((1, LANES)), 1.0, 0.0).astype(BF16)
    return lo, 1 - lo


def _band_masks(n_heads, t, sub):
    row = lax.broadcasted_iota(jnp.int32, (n_heads * TQ, TQ), 0) & (TQ - 1)
    col = lax.broadcasted_iota(jnp.int32, (n_heads * TQ, TQ), 1)
    shift = jnp.where(t > 0, 0, TQ) if sub == 0 else 0
    return col <= row, col >= row + shift


def _sub_tile_kv(sub, sl, kc_ref, kp_ref, vc_ref, vp_ref):
    if sub == 0:
        return kc_ref[0:TQ, sl], kp_ref[:, sl], vc_ref[0:TQ, sl], vp_ref[:, sl]
    return kc_ref[TQ:2 * TQ, sl], kc_ref[0:TQ, sl], vc_ref[TQ:2 * TQ, sl], vc_ref[0:TQ, sl]


def _band_softmax(qs, kc, kp, vc, vp, mask_c, mask_p):
    s_c = jnp.where(mask_c, _dot_nt(qs, kc), NEG_INF)
    s_p = jnp.where(mask_p, _dot_nt(qs, kp), NEG_INF)
    m = jnp.max(jnp.maximum(s_c, s_p), -1, keepdims=True)
    p_c = jnp.exp2(s_c - m)
    p_p = jnp.exp2(s_p - m)
    den = jnp.sum(p_c + p_p, -1, keepdims=True)
    acc = _dot(p_c.astype(BF16), vc) + _dot(p_p.astype(BF16), vp)
    return acc * (1.0 / den), m, den


def _band_a_kernel(q_ref, kc_ref, kp_ref, vc_ref, vp_ref, o_ref, lse_ref):
    t = pl.program_id(2)
    lo = _lane_lo((TQ, LANES))
    lo_bf, hi_bf = _half_masks_bf16()
    lane_head = jnp.right_shift(lax.broadcasted_iota(jnp.int32, (TQ, LANES), 1), int(math.log2(LSE_LANES)))
    for sub in range(2):
        rows = slice(sub * TQ, (sub + 1) * TQ)
        mask_c, mask_p = _band_masks(2, t, sub)
        lse_tile = jnp.zeros((TQ, LANES), F32)
        for c in range(D_A // LANES):
            sl = slice(c * LANES, (c + 1) * LANES)
            kc, kp, vc, vp = _sub_tile_kv(sub, sl, kc_ref, kp_ref, vc_ref, vp_ref)
            q2 = q_ref[rows, sl]
            qs = jnp.concatenate([q2 * lo_bf, q2 * hi_bf], axis=0)
            out, m, den = _band_softmax(qs, kc, kp, vc, vp, mask_c, mask_p)
            lse = m + jnp.log2(den)
            o_ref[rows, sl] = jnp.where(lo, out[0:TQ], out[TQ:]).astype(o_ref.dtype)
            lse_tile = jnp.where(lane_head == 2 * c, lse[0:TQ], lse_tile)
            lse_tile = jnp.where(lane_head == 2 * c + 1, lse[TQ:], lse_tile)
        lse_ref[rows, :] = lse_tile


def _band_a(q, k, v):
    bsz, dil, sub, _ = q.shape
    cur = pl.BlockSpec((None, None, 2 * TQ, D_A), lambda b, r, t: (b, r, t, 0))
    prev = pl.BlockSpec((None, None, TQ, D_A), lambda b, r, t: (b, r, jnp.maximum(2 * t - 1, 0), 0))
    lse = pl.BlockSpec((None, None, 2 * TQ, LANES), lambda b, r, t: (b, r, t, 0))
    return pl.pallas_call(
        _band_a_kernel,
        grid=(bsz, dil, sub // (2 * TQ)),
        in_specs=[cur, cur, prev, cur, prev],
        out_specs=[cur, lse],
        out_shape=[jax.ShapeDtypeStruct((bsz, dil, sub, D_A), BF16),
                   jax.ShapeDtypeStruct((bsz, dil, sub, LANES), F32)],
        compiler_params=_params(3),
        name=f"band_a_d{dil}",
    )(q, k, k, v, v)


def _band_b_body(first_tile, sink_ref, q_ref, kc_ref, kp_ref, vc_ref, vp_ref, o_ref):
    group = N_HEADS_B // N_KV_B
    lo = _lane_lo((TQ, LANES))
    lo_bf, hi_bf = _half_masks_bf16()
    row = lax.broadcasted_iota(jnp.int32, (group * TQ, TQ), 0) & (TQ - 1)
    col = lax.broadcasted_iota(jnp.int32, (group * TQ, TQ), 1)
    in_cur = col <= row
    cur_bf = jnp.where(in_cur, 1.0, 0.0).astype(BF16)
    prev_bf = 1 - cur_bf
    for sub in range(2):
        rows = slice(sub * TQ, (sub + 1) * TQ)
        for g in range(N_KV_B):
            sl = slice(g * LANES, (g + 1) * LANES)
            kc, kp, vc, vp = _sub_tile_kv(sub, sl, kc_ref, kp_ref, vc_ref, vp_ref)
            heads = range(g * group, (g + 1) * group)
            qs = jnp.concatenate(
                [q_ref[rows, (h // 2) * LANES:(h // 2 + 1) * LANES] * (lo_bf if h % 2 == 0 else hi_bf) for h in heads],
                axis=0)
            sink = jnp.concatenate([jnp.full((TQ, 1), sink_ref[h] * LOG2E, F32) for h in heads], axis=0)
            only_cur = first_tile and sub == 0
            s = jnp.where(in_cur, _dot_nt(qs, kc), NEG_INF if only_cur else _dot_nt(qs, kp))
            m = jnp.maximum(jnp.max(s, -1, keepdims=True), sink)
            p = jnp.exp2(s - m)
            den = jnp.sum(p, -1, keepdims=True) + jnp.exp2(sink - m)
            pb = p.astype(BF16)
            acc = _dot(pb, vc) if only_cur else _dot(pb * cur_bf, vc) + _dot(pb * prev_bf, vp)
            out = acc * (1.0 / den)
            for i in range(group // 2):
                c = g * (group // 2) + i
                even, odd = out[2 * i * TQ:(2 * i + 1) * TQ], out[(2 * i + 1) * TQ:(2 * i + 2) * TQ]
                o_ref[rows, c * LANES:(c + 1) * LANES] = jnp.where(lo, even, odd).astype(o_ref.dtype)


def _band_b_kernel(*refs):
    t = pl.program_id(1)
    pl.when(t == 0)(functools.partial(_band_b_body, True, *refs))
    pl.when(t > 0)(functools.partial(_band_b_body, False, *refs))


def _band_b(q, k, v, sinks):
    bsz, seq, _ = q.shape
    kv_lanes = N_KV_B * LANES
    qs = pl.BlockSpec((None, 2 * TQ, D_BQ), lambda b, t: (b, t, 0))
    cur = pl.BlockSpec((None, 2 * TQ, kv_lanes), lambda b, t: (b, t, 0))
    prev = pl.BlockSpec((None, TQ, kv_lanes), lambda b, t: (b, jnp.maximum(2 * t - 1, 0), 0))
    o = pl.pallas_call(
        _band_b_kernel,
        grid=(bsz, seq // (2 * TQ)),
        in_specs=[pl.BlockSpec(memory_space=pltpu.SMEM), qs, cur, prev, cur, prev],
        out_specs=qs,
        out_shape=jax.ShapeDtypeStruct((bsz, seq, D_BQ), BF16),
        compiler_params=_params(2),
        name="band_b",
    )(sinks, q, k, k, v, v)
    return o.reshape(bsz * seq, D_BQ)


def _pattern_count(dist):
    cnt = jnp.zeros(dist.shape, F32)
    for dil in DILATIONS:
        cnt = cnt + ((dist >= 0) & (dist <= 128 * dil) & (dist % dil == 0)).astype(F32)
    return cnt


def _sample_tables():
    i = jnp.arange(DEC_SEQ)
    cnt_c = _pattern_count(WIN_A + i[:, None] - jnp.arange(WIN_A)[None, :])
    j = jnp.arange(LANES)
    cnt_n = jnp.where(j[None, :] < DEC_SEQ, _pattern_count(i[:, None] - j[None, :]), 0.0)
    cnt_n = jnp.tile(cnt_n, (N_HEADS_A, 1))
    jb = jnp.arange(NKB_PAD)[None, :]
    dist_b = WIN_B + i[:, None] - jb
    ok_b = (dist_b >= 0) & (dist_b < WIN_B) & (jb < WIN_B + DEC_SEQ)
    mask_b = jnp.tile(ok_b.astype(F32), (N_HEADS_B, 1))
    return cnt_c, cnt_n, mask_b


def _sample_attend(q, kan, van, kt_ref, vt_ref, cnt_c, cnt_n, qb, kbn, vbn, kbc, vbc, mask_b, sink_col, kb_s, vb_s):
    rows = N_HEADS_A * DEC_SEQ
    q_rep = jnp.concatenate([q] * N_HEADS_A, axis=0)
    row_head = jnp.right_shift(lax.broadcasted_iota(jnp.int32, (rows, D_A), 0), int(math.log2(DEC_SEQ)))
    lane_head = jnp.right_shift(lax.broadcasted_iota(jnp.int32, (rows, D_A), 1), int(math.log2(HEAD_DIM)))
    own = row_head == lane_head
    q_bd = jnp.where(own, q_rep, 0.0).astype(BF16)
    pad = jnp.zeros((LANES - DEC_SEQ, D_A), F32)
    kn = jnp.concatenate([kan, pad], 0).astype(BF16)
    vn = jnp.concatenate([van, pad], 0).astype(BF16)
    s_new = jnp.where(cnt_n > 0.0, _dot_nt(q_bd, kn), NEG_INF)
    outs, p_new = [], []
    for h in range(N_HEADS_A):
        head_rows = slice(h * DEC_SEQ, (h + 1) * DEC_SEQ)
        q_h = q[:, h * HEAD_DIM:(h + 1) * HEAD_DIM].astype(BF16)
        s_c = jnp.where(cnt_c > 0.0, _dot(q_h, kt_ref[h].astype(BF16)), NEG_INF)
        s_n = s_new[head_rows]
        m = jnp.maximum(jnp.max(s_c, -1, keepdims=True), jnp.max(s_n, -1, keepdims=True))
        p_c = jnp.exp(s_c - m) * cnt_c
        p_n = jnp.exp(s_n - m) * cnt_n[head_rows]
        inv = 1.0 / (jnp.sum(p_c, -1, keepdims=True) + jnp.sum(p_n, -1, keepdims=True))
        outs.append(_dot_nt(p_c.astype(BF16), vt_ref[h].astype(BF16)) * inv)
        p_new.append(p_n * inv)
    out_n = jnp.where(own, _dot(jnp.concatenate(p_new, axis=0).astype(BF16), vn), 0.0)
    oa = jnp.concatenate(outs, axis=1)
    for h in range(N_HEADS_A):
        oa = oa + out_n[h * DEC_SEQ:(h + 1) * DEC_SEQ]

    n_pad_b = NKB_PAD - WIN_B - DEC_SEQ
    pad_b = jnp.zeros((n_pad_b, D_BKV), F32)
    kb_s[...] = jnp.concatenate([kbc, kbn, pad_b], 0).astype(BF16)
    vb_s[...] = jnp.concatenate([vbc, vbn, pad_b], 0).astype(BF16)
    lo8 = _lane_lo((DEC_SEQ, LANES))
    group = N_HEADS_B // N_KV_B
    pieces = []
    for h in range(N_HEADS_B):
        chunk = qb[:, (h // 2) * LANES:(h // 2 + 1) * LANES]
        g = h // group
        if h % 2 != g:
            chunk = pltpu.roll(chunk, HEAD_DIM, 1)
        pieces.append(jnp.where(lo8 if g == 0 else jnp.logical_not(lo8), chunk, 0.0))
    qb_bd = jnp.concatenate(pieces, axis=0).astype(BF16)
    sb = jnp.where(mask_b > 0.0, _dot_nt(qb_bd, kb_s[...]), NEG_INF)
    sink = sink_col[:, 0:1]
    mb = jnp.maximum(jnp.max(sb, -1, keepdims=True), sink)
    pb = jnp.exp(sb - mb) * mask_b
    den_b = jnp.sum(pb, -1, keepdims=True) + jnp.exp(sink - mb)
    ob_full = _dot(pb.astype(BF16), vb_s[...]) * (1.0 / den_b)
    ob = []
    for c in range(D_BQ // LANES):
        halves = []
        for half in range(2):
            h = 2 * c + half
            piece = ob_full[h * DEC_SEQ:(h + 1) * DEC_SEQ]
            if half != h // group:
                piece = pltpu.roll(piece, HEAD_DIM, 1)
            halves.append(piece)
        ob.append(jnp.where(lo8, halves[0], halves[1]))
    return jnp.concatenate([oa] + ob, axis=1)


FUSED_ROW_TILE = 512
SEQ_PER_STEP = DEC_BATCH // (BATCH * SEQ // FUSED_ROW_TILE)


def _kv_copies(kt_hbm, vt_hbm, kbuf, vbuf, sems, seq, slot):
    return (pltpu.make_async_copy(kt_hbm.at[seq], kbuf.at[slot], sems.at[0, slot]),
            pltpu.make_async_copy(vt_hbm.at[seq], vbuf.at[slot], sems.at[1, slot]))


def _ffn_attn_kernel(x_ref, wg_ref, wu_ref, wd_ref, g_ref, b_ref,
                     qa_ref, kan_ref, van_ref, kt_hbm, vt_hbm, cnt_c_ref, cnt_n_ref,
                     qb_ref, kbn_ref, vbn_ref, kbc_ref, vbc_ref, maskb_ref, sinkcol_ref,
                     y_ref, o_ref, h_ref, kbuf, vbuf, sems, kb_s, vb_s):
    step = pl.program_id(0)
    n_steps = pl.num_programs(0)
    copies = functools.partial(_kv_copies, kt_hbm, vt_hbm, kbuf, vbuf, sems)

    @pl.when(step == 0)
    def _():
        for cp in copies(0, 0):
            cp.start()

    x = x_ref[...]
    xb = x.astype(BF16)
    n_ff = D_FF // FF_CHUNK
    per_seq = -(-n_ff // SEQ_PER_STEP)
    cnt_c, cnt_n, mask_b, sink_col = cnt_c_ref[...], cnt_n_ref[...], maskb_ref[...], sinkcol_ref[...]
    for s in range(SEQ_PER_STEP):
        slot = s % 2
        seq = step * SEQ_PER_STEP + s
        for cp in copies(seq, slot):
            cp.wait()
        if s + 1 < SEQ_PER_STEP:
            for cp in copies(seq + 1, 1 - slot):
                cp.start()
        else:
            @pl.when(step + 1 < n_steps)
            def _():
                for cp in copies(seq + 1, 1 - slot):
                    cp.start()
        rows = slice(s * DEC_SEQ, (s + 1) * DEC_SEQ)
        kbc = jnp.concatenate([kbc_ref[s, g].T for g in range(N_KV_B)], axis=1)
        vbc = jnp.concatenate([vbc_ref[s, g].T for g in range(N_KV_B)], axis=1)
        o_ref[rows, :] = _sample_attend(
            qa_ref[rows, :], kan_ref[rows, :], van_ref[rows, :], kbuf.at[slot], vbuf.at[slot], cnt_c, cnt_n,
            qb_ref[rows, :], kbn_ref[rows, :], vbn_ref[rows, :], kbc, vbc, mask_b, sink_col, kb_s, vb_s)
        for c in range(s * per_seq, min((s + 1) * per_seq, n_ff)):
            sl = slice(c * FF_CHUNK, (c + 1) * FF_CHUNK)
            gate = _dot(xb, wg_ref[:, sl])
            up = _dot(xb, wu_ref[:, sl])
            h_ref[:, sl] = (gate * jax.nn.sigmoid(gate) * up).astype(BF16)
    y = DN_ALPHA * x + FFN_RES * _dot(h_ref[...], wd_ref[...])
    y_ref[...] = _layer_norm(y, g_ref[...], b_ref[...])


def _ffn_with_sample_attention(x, wg, wu, wd, g, b, layer, which,
                               qa, kan, van, cache_ak, cache_av, qb, kbn, vbn, cache_bk, cache_bv, sinks):
    n = x.shape[0]
    tm = FUSED_ROW_TILE
    assert n // tm * SEQ_PER_STEP == DEC_BATCH and SEQ_PER_STEP % 2 == 0
    cnt_c, cnt_n, mask_b = _sample_tables()
    sink_col = jnp.broadcast_to(jnp.repeat(sinks.astype(F32), DEC_SEQ)[:, None], (N_HEADS_B * DEC_SEQ, LANES))
    row = pl.BlockSpec((tm, D_MODEL), lambda i: (i, 0))
    pick = lambda r, c: pl.BlockSpec((None, None, r, c), lambda i: (layer, which, 0, 0), pipeline_mode=pl.Buffered(1))
    new = lambda width: pl.BlockSpec((SEQ_PER_STEP * DEC_SEQ, width), lambda i: (i, 0))
    cache_b = pl.BlockSpec((SEQ_PER_STEP, N_KV_B, HEAD_DIM, WIN_B), lambda i: (i, 0, 0, 0))
    hbm = pl.BlockSpec(memory_space=pl.ANY)
    rows = N_HEADS_A * DEC_SEQ
    kv_slot = (2, N_HEADS_A, HEAD_DIM, WIN_A)
    return pl.pallas_call(
        _ffn_attn_kernel,
        grid=(n // tm,),
        in_specs=[row, pick(D_MODEL, D_FF), pick(D_MODEL, D_FF), pick(D_FF, D_MODEL),
                  _resident((1, D_MODEL)), _resident((1, D_MODEL)),
                  new(D_A), new(D_A), new(D_A), hbm, hbm, _resident(cnt_c.shape), _resident(cnt_n.shape),
                  new(D_BQ), new(D_BKV), new(D_BKV), cache_b, cache_b,
                  _resident((rows, NKB_PAD)), _resident((rows, LANES))],
        out_specs=[row, new(D_A + D_BQ)],
        out_shape=[jax.ShapeDtypeStruct((n, D_MODEL), F32),
                   jax.ShapeDtypeStruct((DEC_BATCH * DEC_SEQ, D_A + D_BQ), F32)],
        scratch_shapes=[pltpu.VMEM((tm, D_FF), BF16), pltpu.VMEM(kv_slot, F32), pltpu.VMEM(kv_slot, F32),
                        pltpu.SemaphoreType.DMA((2, 2)),
                        pltpu.VMEM((NKB_PAD, D_BKV), BF16), pltpu.VMEM((NKB_PAD, D_BKV), BF16)],
        compiler_params=_params(1, vmem=60 * 1024 * 1024),
        name="ffn_attn_sample",
    )(x, wg, wu, wd, g, b, qa, kan, van, cache_ak, cache_av, cnt_c, cnt_n, qb, kbn, vbn, cache_bk, cache_bv,
      mask_b, sink_col)


def _attn_out_prompt_kernel(o1_ref, l1_ref, o4_ref, l4_ref, o16_ref, l16_ref, ob_ref, y_ref, w_ref, g_ref, b_ref,
                            out_ref, slab_ref, oa_ref):
    tm = y_ref.shape[0]
    n_chunks = D_A // LANES
    slabs = {}
    base = 0
    for name, dil, src, width in (("o4", 4, o4_ref, n_chunks), ("l4", 4, l4_ref, 1),
                                  ("o16", 16, o16_ref, n_chunks), ("l16", 16, l16_ref, 1)):
        slabs[name] = base
        for r in range(dil):
            for c in range(width):
                piece = src[r, :, c * LANES:(c + 1) * LANES]
                slab_ref[base + c, pl.ds(r, tm // dil, stride=dil), :] = piece.astype(F32)
        base += width
    l1, l4, l16 = l1_ref[...], slab_ref[slabs["l4"]], slab_ref[slabs["l16"]]
    m = jnp.maximum(jnp.maximum(l1, l4), l16)
    e1, e4, e16 = jnp.exp2(l1 - m), jnp.exp2(l4 - m), jnp.exp2(l16 - m)
    inv = 1.0 / (e1 + e4 + e16)
    row = lax.broadcasted_iota(jnp.int32, (LANES, D_A), 0)
    head_of_col = jnp.right_shift(lax.broadcasted_iota(jnp.int32, (LANES, D_A), 1), int(math.log2(HEAD_DIM)))
    spread = jnp.where(row == head_of_col * LSE_LANES, 1.0, 0.0).astype(BF16)

    def per_head_lanes(w):
        hi = w.astype(BF16)
        lo = (w - hi.astype(F32)).astype(BF16)
        return _dot(hi, spread) + _dot(lo, spread)

    w1, w4, w16 = per_head_lanes(e1 * inv), per_head_lanes(e4 * inv), per_head_lanes(e16 * inv)
    for c in range(n_chunks):
        sl = slice(c * LANES, (c + 1) * LANES)
        oa = (w1[:, sl] * o1_ref[:, sl].astype(F32) + w4[:, sl] * slab_ref[slabs["o4"] + c]
              + w16[:, sl] * slab_ref[slabs["o16"] + c])
        oa_ref[:, sl] = oa.astype(BF16)
    mix = _dot(oa_ref[...], w_ref[0:D_A, :]) + _dot(ob_ref[...], w_ref[D_A:, :])
    out_ref[...] = _layer_norm(DN_ALPHA * y_ref[...] + mix, g_ref[...], b_ref[...])


def _attn_out_prompt(pats, ob, y, w, g, b, seq):
    n = y.shape[0]
    tm = WIDE_ROW_TILE
    tps = seq // tm
    half = pl.BlockSpec((tm, D_A), lambda i: (i, 0))
    full = pl.BlockSpec((tm, D_MODEL), lambda i: (i, 0))
    lse1 = pl.BlockSpec((tm, LANES), lambda i: (i, 0))
    planes = lambda dil, width: pl.BlockSpec((None, dil, tm // dil, width), lambda i: (i // tps, 0, i % tps, 0))
    (o1, l1), (o4, l4), (o16, l16) = pats
    return pl.pallas_call(
        _attn_out_prompt_kernel,
        grid=(n // tm,),
        in_specs=[half, lse1, planes(4, D_A), planes(4, LANES), planes(16, D_A), planes(16, LANES), half, full,
                  _resident((D_MODEL, D_MODEL)), _resident((1, D_MODEL)), _resident((1, D_MODEL))],
        out_specs=full,
        out_shape=jax.ShapeDtypeStruct((n, D_MODEL), F32),
        scratch_shapes=[pltpu.VMEM((2 * (D_A // LANES + 1), tm, LANES), F32), pltpu.VMEM((tm, D_A), BF16)],
        compiler_params=_params(1),
        name="attn_out_prompt",
    )(o1, l1, o4, l4, o16, l16, ob, y, w, g, b)


def _mix_out_kernel(o_ref, y_ref, w_ref, g_ref, b_ref, out_ref):
    mix = _dot(o_ref[...].astype(BF16), w_ref[...])
    out_ref[...] = _layer_norm(DN_ALPHA * y_ref[...] + mix, g_ref[...], b_ref[...])


def _attn_out_sample(o, y, w, g, b):
    n = y.shape[0]
    tm = min(ROW_TILE, n)
    full = pl.BlockSpec((tm, D_MODEL), lambda i: (i, 0))
    return pl.pallas_call(
        _mix_out_kernel,
        grid=(n // tm,),
        in_specs=[full, full, _resident((D_MODEL, D_MODEL)), _resident((1, D_MODEL)), _resident((1, D_MODEL))],
        out_specs=full,
        out_shape=jax.ShapeDtypeStruct((n, D_MODEL), F32),
        compiler_params=_params(1),
        name="attn_out_sample",
    )(o, y, w, g, b)


def _ssm_discretize(lam_re, lam_im, log_dt, b_re, b_im):
    dt = jnp.exp(log_dt.astype(F32))[:, None]
    lr, li = lam_re.astype(F32), lam_im.astype(F32)
    mag = jnp.exp(lr * dt)
    ab_re, ab_im = mag * jnp.cos(li * dt), mag * jnp.sin(li * dt)
    nr, ni = ab_re - 1.0, ab_im
    den = lr * lr + li * li
    fr, fi = (nr * lr + ni * li) / den, (ni * lr - nr * li) / den
    bb_re = fr[..., None] * b_re - fi[..., None] * b_im
    bb_im = fr[..., None] * b_im + fi[..., None] * b_re
    return ab_re, ab_im, bb_re, bb_im


def _ssm_matrices(bb_re, bb_im, c_re, c_im):
    gpc = LANES // SSM_GROUP
    eye = jnp.eye(gpc, dtype=F32)

    def in_blocks(bb):
        a = bb.reshape(N_LCHUNK, gpc, SSM_STATE, SSM_GROUP)
        return jnp.einsum("jgpn,gh->jgnhp", a, eye).reshape(N_LCHUNK, LANES, ST_CHUNK)

    def out_blocks(cc):
        a = cc.reshape(N_LCHUNK, gpc, SSM_GROUP, SSM_STATE)
        return jnp.einsum("jgnp,gh->jgphn", a, eye).reshape(N_LCHUNK, ST_CHUNK, LANES)

    bmat = jnp.concatenate([in_blocks(bb_re), in_blocks(bb_im)], -1)
    cmat = jnp.concatenate([out_blocks(c_re), -out_blocks(c_im)], 1)
    return bmat, cmat


CHUNK = 8


def _dot3(a, b):
    a_hi, b_hi = a.astype(BF16), b.astype(BF16)
    a_lo, b_lo = (a - a_hi.astype(F32)).astype(BF16), (b - b_hi.astype(F32)).astype(BF16)
    return _dot(a_hi, b_hi) + _dot(a_hi, b_lo) + _dot(a_lo, b_hi)


def _chunk_weights_kernel(ar_ref, ai_ref, acr_ref, aci_ref, bmat_ref, cmat_ref,
                          we_ref, ws_ref, wi_ref, a8r_ref, a8i_ref):
    def powers(r, i, n):
        out = [(jnp.ones_like(r), jnp.zeros_like(r))]
        for _ in range(n):
            out.append(_cmul(out[-1][0], out[-1][1], r, i))
        return out

    row_pow = powers(ar_ref[...], ai_ref[...], CHUNK)
    col_pow = powers(acr_ref[...], aci_ref[...], CHUNK)
    b_re, b_im = bmat_ref[:, 0:ST_CHUNK], bmat_ref[:, ST_CHUNK:]
    c_re, c_im = cmat_ref[0:ST_CHUNK, :], -cmat_ref[ST_CHUNK:, :]

    def scaled_b(power):
        pr, pi = row_pow[power]
        return jnp.concatenate([b_re * pr - b_im * pi, b_re * pi + b_im * pr], axis=1)

    scaled = [scaled_b(tau) for tau in range(CHUNK)]
    for tau in range(CHUNK):
        we_ref[(CHUNK - 1 - tau) * LANES:(CHUNK - tau) * LANES, :] = scaled[tau].astype(BF16)
    all_taps = _dot3(jnp.concatenate(scaled, axis=0), cmat_ref[...]).astype(BF16)
    taps = [all_taps[tau * LANES:(tau + 1) * LANES] for tau in range(CHUNK)]
    for k in range(CHUNK):
        cols = slice(k * LANES, (k + 1) * LANES)
        qr, qi = col_pow[k + 1]
        ws_ref[0:ST_CHUNK, cols] = (c_re * qr - c_im * qi).astype(BF16)
        ws_ref[ST_CHUNK:, cols] = (-(c_re * qi + c_im * qr)).astype(BF16)
    zero = jnp.zeros((LANES, LANES), BF16)
    for k_in in range(CHUNK):
        for k_out in range(CHUNK):
            wi_ref[k_in * LANES:(k_in + 1) * LANES, k_out * LANES:(k_out + 1) * LANES] = (
                taps[k_out - k_in] if k_out >= k_in else zero)
    a8r_ref[...], a8i_ref[...] = row_pow[CHUNK]


def _chunk_weights(a_re, a_im, bmat, cmat):
    cols = lambda a: jnp.broadcast_to(a.reshape(N_LCHUNK, ST_CHUNK, 1), (N_LCHUNK, ST_CHUNK, LANES))
    per_j = lambda r, c: pl.BlockSpec((None, r, c), lambda j: (j, 0, 0))
    wide = 2 * ST_CHUNK
    return pl.pallas_call(
        _chunk_weights_kernel,
        grid=(N_LCHUNK,),
        in_specs=[per_j(1, ST_CHUNK), per_j(1, ST_CHUNK), per_j(ST_CHUNK, LANES), per_j(ST_CHUNK, LANES),
                  per_j(LANES, wide), per_j(wide, LANES)],
        out_specs=[per_j(CHUNK * LANES, wide), per_j(wide, CHUNK * LANES), per_j(CHUNK * LANES, CHUNK * LANES),
                   per_j(1, ST_CHUNK), per_j(1, ST_CHUNK)],
        out_shape=[jax.ShapeDtypeStruct((N_LCHUNK, CHUNK * LANES, wide), BF16),
                   jax.ShapeDtypeStruct((N_LCHUNK, wide, CHUNK * LANES), BF16),
                   jax.ShapeDtypeStruct((N_LCHUNK, CHUNK * LANES, CHUNK * LANES), BF16),
                   jax.ShapeDtypeStruct((N_LCHUNK, 1, ST_CHUNK), F32),
                   jax.ShapeDtypeStruct((N_LCHUNK, 1, ST_CHUNK), F32)],
        compiler_params=_params(1),
        name="ssm_chunk_weights",
    )(a_re, a_im, cols(a_re), cols(a_im), bmat, cmat)


SEG_TILE = WIDE_ROW_TILE // N_SEG


def _ssm_in_prompt_kernel(x_ref, w_ref, o_ref):
    x = x_ref[...].reshape(N_SEG * SEG_TILE, D_MODEL)
    u = _dot(x.astype(BF16), w_ref[...])
    for s in range(N_SEG):
        for c in range(N_LCHUNK):
            o_ref[c, pl.ds(s, SEG_TILE, stride=N_SEG), :] = u[s * SEG_TILE:(s + 1) * SEG_TILE, c * LANES:(c + 1) * LANES]


def _ssm_in_prompt(y, w):
    return pl.pallas_call(
        _ssm_in_prompt_kernel,
        grid=(BATCH, SEG_LEN // SEG_TILE),
        in_specs=[pl.BlockSpec((None, N_SEG, SEG_TILE, D_MODEL), lambda b, t: (b, 0, t, 0)),
                  _resident((D_MODEL, D_MODEL))],
        out_specs=pl.BlockSpec((None, N_LCHUNK, WIDE_ROW_TILE, LANES), lambda b, t: (b, 0, t, 0)),
        out_shape=jax.ShapeDtypeStruct((BATCH, N_LCHUNK, SEQ, LANES), F32),
        compiler_params=_params(2),
        name="ssm_in_prompt",
    )(y.reshape(BATCH, N_SEG, SEG_LEN, D_MODEL), w)


def _ssm_in_sample_kernel(x_ref, w_ref, o_ref, slab_ref):
    u = _dot(x_ref[...].astype(BF16), w_ref[...])
    for c in range(N_LCHUNK):
        slab_ref[c] = u[:, c * LANES:(c + 1) * LANES]
    for l in range(DEC_SEQ):
        for c in range(N_LCHUNK):
            o_ref[c, l * DEC_BATCH:(l + 1) * DEC_BATCH, :] = slab_ref[c, pl.ds(l, DEC_BATCH, stride=DEC_SEQ), :]


def _ssm_in_sample(y, w):
    n = DEC_BATCH * DEC_SEQ
    return pl.pallas_call(
        _ssm_in_sample_kernel,
        grid=(1,),
        in_specs=[_resident((n, D_MODEL)), _resident((D_MODEL, D_MODEL))],
        out_specs=pl.BlockSpec((N_LCHUNK, n, LANES), lambda i: (0, 0, 0)),
        out_shape=jax.ShapeDtypeStruct((N_LCHUNK, n, LANES), F32),
        scratch_shapes=[pltpu.VMEM((N_LCHUNK, n, LANES), F32)],
        compiler_params=_params(1),
        name="ssm_in_sample",
    )(y, w)


def _cmul(ar, ai, br, bi):
    return ar * br - ai * bi, ar * bi + ai * br


def _scan_prompt_kernel(u_ref, we_ref, ws_ref, wi_ref, a8r_ref, a8i_ref, d_ref, h0r_ref, h0i_ref,
                        y_ref, hnr_ref, hni_ref, e_s, hs_s):
    n_chunks = u_ref.shape[0]
    rows = n_chunks * N_SEG
    u_flat = jnp.concatenate([u_ref[:, k].reshape(rows, LANES) for k in range(CHUNK)], axis=1)
    ub = u_flat.astype(BF16)
    e_s[...] = _dot(ub, we_ref[...])
    a8r1, a8i1 = a8r_ref[...], a8i_ref[...]
    a8r = jnp.broadcast_to(a8r1, (N_SEG, ST_CHUNK))
    a8i = jnp.broadcast_to(a8i1, (N_SEG, ST_CHUNK))

    def advance(row, hr, hi):
        er = e_s[pl.ds(row, N_SEG), 0:ST_CHUNK]
        ei = e_s[pl.ds(row, N_SEG), ST_CHUNK:2 * ST_CHUNK]
        return a8r * hr - a8i * hi + er, a8r * hi + a8i * hr + ei

    def pass1(c, carry):
        return advance(pl.multiple_of(c * N_SEG, N_SEG), *carry)

    zero = jnp.zeros((N_SEG, ST_CHUNK), F32)
    er, ei = lax.fori_loop(0, n_chunks, pass1, (zero, zero), unroll=8)

    pr, pi = a8r1, a8i1
    for _ in range(int(math.log2(n_chunks))):
        pr, pi = _cmul(pr, pi, pr, pi)
    hr, hi = h0r_ref[...], h0i_ref[...]
    starts_r, starts_i = [], []
    for s in range(N_SEG):
        starts_r.append(hr)
        starts_i.append(hi)
        gr, gi = _cmul(pr, pi, hr, hi)
        hr, hi = gr + er[s:s + 1], gi + ei[s:s + 1]
    hnr_ref[...] = hr
    hni_ref[...] = hi
    init = (jnp.concatenate(starts_r, 0), jnp.concatenate(starts_i, 0))

    def pass2(c, carry):
        row = pl.multiple_of(c * N_SEG, N_SEG)
        hs_s[pl.ds(row, N_SEG), 0:ST_CHUNK] = carry[0]
        hs_s[pl.ds(row, N_SEG), ST_CHUNK:2 * ST_CHUNK] = carry[1]
        return advance(row, *carry)

    lax.fori_loop(0, n_chunks, pass2, init, unroll=8)
    y = _dot(hs_s[...].astype(BF16), ws_ref[...]) + _dot(ub, wi_ref[...]) + d_ref[...] * u_flat
    for k in range(CHUNK):
        y_ref[:, k] = y[:, k * LANES:(k + 1) * LANES].reshape(n_chunks, N_SEG, LANES)


def _scan_prompt(u, weights, d_skip, h0r, h0i):
    bsz, _, seq, _ = u.shape
    n_chunks = seq // (CHUNK * N_SEG)
    rows = n_chunks * N_SEG
    wide = 2 * ST_CHUNK
    split = lambda a: a.reshape(bsz, N_LCHUNK, n_chunks, CHUNK, N_SEG, LANES)
    chunk = pl.BlockSpec((None, None, n_chunks, CHUNK, N_SEG, LANES), lambda j, b: (b, j, 0, 0, 0, 0))
    per_j = lambda r, c: pl.BlockSpec((None, r, c), lambda j, b: (j, 0, 0))
    state = pl.BlockSpec((None, 1, ST_CHUNK), lambda j, b: (b, 0, j))
    y, hr, hi = pl.pallas_call(
        _scan_prompt_kernel,
        grid=(N_LCHUNK, bsz),
        in_specs=[chunk, per_j(CHUNK * LANES, wide), per_j(wide, CHUNK * LANES), per_j(CHUNK * LANES, CHUNK * LANES),
                  per_j(1, ST_CHUNK), per_j(1, ST_CHUNK), per_j(1, CHUNK * LANES), state, state],
        out_specs=[chunk, state, state],
        out_shape=[jax.ShapeDtypeStruct((bsz, N_LCHUNK, n_chunks, CHUNK, N_SEG, LANES), F32),
                   jax.ShapeDtypeStruct((bsz, 1, N_STATE), F32), jax.ShapeDtypeStruct((bsz, 1, N_STATE), F32)],
        scratch_shapes=[pltpu.VMEM((rows, wide), F32), pltpu.VMEM((rows, wide), F32)],
        compiler_params=_params(2),
        name="ssm_scan_prompt",
    )(split(u), *weights, d_skip, h0r, h0i)
    return y.reshape(u.shape), hr, hi


def _scan_sample_kernel(u_ref, bmat_ref, cmat_ref, are_ref, aim_ref, d_ref, h0r_ref, h0i_ref,
                        y_ref, hnr_ref, hni_ref, h_s):
    a_re, a_im = are_ref[...], aim_ref[...]
    hr, hi = h0r_ref[...].T, h0i_ref[...].T
    u = u_ref[...]
    bu = _dot(u.astype(BF16), bmat_ref[...])
    for l in range(DEC_SEQ):
        rows = slice(l * DEC_BATCH, (l + 1) * DEC_BATCH)
        gr, gi = _cmul(a_re, a_im, hr, hi)
        hr, hi = gr + bu[rows, :ST_CHUNK], gi + bu[rows, ST_CHUNK:]
        h_s[rows, 0:ST_CHUNK] = hr.astype(BF16)
        h_s[rows, ST_CHUNK:] = hi.astype(BF16)
    y_ref[...] = _dot(h_s[...], cmat_ref[...]) + d_ref[...] * u
    hnr_ref[...] = hr.T
    hni_ref[...] = hi.T


def _scan_sample(u, bmat, cmat, a_re, a_im, d_skip, h0r, h0i):
    n = DEC_SEQ * DEC_BATCH
    chunk = pl.BlockSpec((None, n, LANES), lambda j: (j, 0, 0))
    per_j = lambda r, c: pl.BlockSpec((None, r, c), lambda j: (j, 0, 0))
    state = pl.BlockSpec((ST_CHUNK, DEC_BATCH), lambda j: (j, 0))
    return pl.pallas_call(
        _scan_sample_kernel,
        grid=(N_LCHUNK,),
        in_specs=[chunk, per_j(LANES, 2 * ST_CHUNK), per_j(2 * ST_CHUNK, LANES), per_j(1, ST_CHUNK),
                  per_j(1, ST_CHUNK), per_j(1, LANES), state, state],
        out_specs=[chunk, state, state],
        out_shape=[jax.ShapeDtypeStruct((N_LCHUNK, n, LANES), F32),
                   jax.ShapeDtypeStruct((N_STATE, DEC_BATCH), F32), jax.ShapeDtypeStruct((N_STATE, DEC_BATCH), F32)],
        scratch_shapes=[pltpu.VMEM((n, 2 * ST_CHUNK), BF16)],
        compiler_params=_params(1),
        name="ssm_scan_sample",
    )(u, bmat, cmat, a_re, a_im, d_skip, h0r, h0i)


def _ssm_out_kernel(sample, s_ref, y_ref, wglu_ref, bglu_ref, wout_ref, g_ref, b_ref, out_ref, z_ref):
    n_rows = z_ref.shape[1]
    if sample:
        parts = [(pl.ds(l, DEC_BATCH, stride=DEC_SEQ), slice(l * DEC_BATCH, (l + 1) * DEC_BATCH))
                 for l in range(DEC_SEQ)]
    else:
        parts = [(slice(s * SEG_TILE, (s + 1) * SEG_TILE), pl.ds(s, SEG_TILE, stride=N_SEG)) for s in range(N_SEG)]
    for c in range(N_LCHUNK):
        for tok_rows, slab_rows in parts:
            z_ref[c, tok_rows, :] = s_ref[c, slab_rows, :]
    z = jax.nn.gelu(jnp.concatenate([z_ref[c] for c in range(N_LCHUNK)], axis=1))
    gate = jax.nn.sigmoid(_dot(z.astype(BF16), wglu_ref[...]) + bglu_ref[...])
    mix = _dot((z * gate).astype(BF16), wout_ref[...])
    res = y_ref[...].reshape(n_rows, D_MODEL)
    out = _layer_norm(DN_ALPHA * res + mix, g_ref[...], b_ref[...])
    out_ref[...] = out.reshape(out_ref.shape)


def _ssm_out(s, y, s_spec, y_spec, grid, rows, sample, w_glu, b_glu, w_out, g, b, name):
    return pl.pallas_call(
        functools.partial(_ssm_out_kernel, sample),
        grid=grid,
        in_specs=[s_spec, y_spec, _resident((D_MODEL, D_MODEL)), _resident((1, D_MODEL)),
                  _resident((D_MODEL, D_MODEL)), _resident((1, D_MODEL)), _resident((1, D_MODEL))],
        out_specs=y_spec,
        out_shape=jax.ShapeDtypeStruct(y.shape, F32),
        scratch_shapes=[pltpu.VMEM((N_LCHUNK, rows, LANES), F32)],
        compiler_params=_params(len(grid)),
        name=name,
    )(s, y, w_glu, b_glu, w_out, g, b)


def _attn_prompt(yp, w_in, sinks, w_out, g, b):
    cos_p, sin_p = _rope_tables(jnp.arange(SEQ))
    tiles_per_seq = SEQ // ROW_TILE
    qa, ka, va, qb, kb, vb, *extra = _attn_proj(
        yp, w_in, cos_p, sin_p, lambda i: i % tiles_per_seq, BF16, prompt_seq=(BATCH, SEQ))
    dilated, tails = extra[:6], extra[6:]
    seq3 = lambda a: a.reshape(BATCH, SEQ, a.shape[-1])
    plane1 = lambda a: a.reshape(BATCH, 1, SEQ, D_A)
    o1, l1 = _band_a(plane1(qa), plane1(ka), plane1(va))
    pats = [(o1.reshape(BATCH * SEQ, D_A), l1.reshape(BATCH * SEQ, LANES))]
    for i in range(len(DILATIONS) - 1):
        pats.append(_band_a(*dilated[3 * i:3 * i + 3]))
    ob = _band_b(seq3(qb), seq3(kb), seq3(vb), sinks)
    yp = _attn_out_prompt(pats, ob, yp, w_out, g, b, SEQ)
    heads = lambda a, nh: jnp.transpose(a.reshape(BATCH, nh, HEAD_DIM, a.shape[-1]), (0, 3, 1, 2))[None]
    prompt_cache = (heads(tails[0], N_HEADS_A), heads(tails[1], N_HEADS_A),
                    heads(tails[2], N_KV_B), heads(tails[3], N_KV_B))
    return yp, prompt_cache


def _attn_sample_path(ys, cache_ak, cache_av, cache_bk, cache_bv, w_in, sinks, w_out, g, b, yp, ffn_args):
    cos_s, sin_s = _rope_tables(PAST_LEN + jnp.arange(DEC_SEQ))
    reps = DEC_BATCH * DEC_SEQ // DEC_SEQ
    cos_s, sin_s = jnp.tile(cos_s, (reps, 1)), jnp.tile(sin_s, (reps, 1))
    qa, ka, va, qb, kb, vb = _attn_proj(ys, w_in, cos_s, sin_s, lambda i: i, F32)
    stored = lambda a: jnp.transpose(a, (0, 2, 3, 1))
    yp, o = _ffn_with_sample_attention(
        yp, *ffn_args, qa, ka, va, stored(cache_ak), stored(cache_av),
        qb, kb, vb, stored(cache_bk), stored(cache_bv), sinks)
    ys = _attn_out_sample(o, ys, w_out, g, b)
    new = lambda a, nh: a.reshape(1, DEC_BATCH, DEC_SEQ, nh, HEAD_DIM)
    sample_cache = (new(ka, N_HEADS_A), new(va, N_HEADS_A), new(kb, N_KV_B), new(vb, N_KV_B))
    return yp, ys, sample_cache


def _attn_layer(yp, ys, cache_ak, cache_av, cache_bk, cache_bv, w_in, sinks, w_out, g, b, ffn_args):
    w_in = w_in.astype(BF16)
    w_out = w_out.astype(BF16)
    yp, prompt_cache = _attn_prompt(yp, w_in, sinks, w_out, g, b)
    yp, ys, sample_cache = _attn_sample_path(ys, cache_ak, cache_av, cache_bk, cache_bv, w_in, sinks, w_out, g, b,
                                             yp, ffn_args)
    return yp, ys, prompt_cache, sample_cache


def _ssm_layer(yp, ys, state_re, state_im, w_in, lam_re, lam_im, log_dt, b_re, b_im, c_re, c_im, d_skip,
               w_glu, b_glu, w_out, g, b):
    w_in, w_glu, w_out = w_in.astype(BF16), w_glu.astype(BF16), w_out.astype(BF16)
    b_glu = b_glu.reshape(1, D_MODEL)
    mats = _ssm_prepare(lam_re, lam_im, log_dt, b_re, b_im, c_re, c_im, d_skip)
    yp, prompt_state = _ssm_prompt(yp, w_in, mats, w_glu, b_glu, w_out, g, b)
    ys, sample_state = _ssm_sample(ys, state_re, state_im, w_in, mats, w_glu, b_glu, w_out, g, b)
    return yp, ys, prompt_state, sample_state


def _ssm_prepare(lam_re, lam_im, log_dt, b_re, b_im, c_re, c_im, d_skip):
    ab_re, ab_im, bb_re, bb_im = _ssm_discretize(lam_re, lam_im, log_dt, b_re, b_im)
    bmat, cmat = _ssm_matrices(bb_re, bb_im, c_re, c_im)
    a_re = ab_re.reshape(N_LCHUNK, 1, ST_CHUNK)
    a_im = ab_im.reshape(N_LCHUNK, 1, ST_CHUNK)
    d3 = d_skip.astype(F32).reshape(N_LCHUNK, 1, LANES)
    prompt = (_chunk_weights(a_re, a_im, bmat, cmat), jnp.tile(d3, (1, 1, CHUNK)))
    sample = (bmat.astype(BF16), cmat.astype(BF16), a_re, a_im, d3)
    return prompt, sample


def _ssm_prompt(yp, w_in, mats, w_glu, b_glu, w_out, g, b):
    weights, d_tiled = mats[0]
    up = _ssm_in_prompt(yp, w_in)
    zero = jnp.zeros((BATCH, 1, N_STATE), F32)
    sp, pr, pi = _scan_prompt(up, weights, d_tiled, zero, zero)
    yp = _ssm_out(sp, yp.reshape(BATCH, N_SEG, SEG_LEN, D_MODEL),
                  pl.BlockSpec((None, N_LCHUNK, WIDE_ROW_TILE, LANES), lambda bb, t: (bb, 0, t, 0)),
                  pl.BlockSpec((None, N_SEG, SEG_TILE, D_MODEL), lambda bb, t: (bb, 0, t, 0)),
                  (BATCH, SEG_LEN // SEG_TILE), WIDE_ROW_TILE, False, w_glu, b_glu, w_out, g, b,
                  "ssm_out_prompt").reshape(BATCH * SEQ, D_MODEL)
    prompt_state = (pr.reshape(1, BATCH, N_SSM_GROUPS, SSM_STATE), pi.reshape(1, BATCH, N_SSM_GROUPS, SSM_STATE))
    return yp, prompt_state


def _ssm_sample(ys, state_re, state_im, w_in, mats, w_glu, b_glu, w_out, g, b):
    bmat, cmat, a_re, a_im, d3 = mats[1]
    us = _ssm_in_sample(ys, w_in)
    stored = lambda a: jnp.transpose(a, (1, 2, 0)).reshape(N_STATE, DEC_BATCH)
    logical = lambda a: jnp.transpose(a.reshape(N_SSM_GROUPS, SSM_STATE, DEC_BATCH), (2, 0, 1))[None]
    ss, sr, si = _scan_sample(us, bmat, cmat, a_re, a_im, d3, stored(state_re), stored(state_im))
    n = DEC_BATCH * DEC_SEQ
    ys = _ssm_out(ss, ys,
                  pl.BlockSpec((N_LCHUNK, n, LANES), lambda i: (0, 0, 0)),
                  pl.BlockSpec((n, D_MODEL), lambda i: (0, 0)),
                  (1,), n, True, w_glu, b_glu, w_out, g, b, "ssm_out_sample")
    return ys, (logical(sr), logical(si))


def kernel(x_prompt, x_sample, cache_a_k, cache_a_v, cache_b_k, cache_b_v, state_c_re, state_c_im, ln_g, ln_b, ffn_w_gate, ffn_w_up, ffn_w_down, attn_w_in, attn_sinks, attn_w_out, ssm_w_in, ssm_lambda_re, ssm_lambda_im, ssm_log_dt, ssm_b_re, ssm_b_im, ssm_c_re, ssm_c_im, ssm_d, ssm_w_glu, ssm_b_glu, ssm_w_out):
    yp = x_prompt.reshape(BATCH * SEQ, D_MODEL)
    ys = x_sample.reshape(DEC_BATCH * DEC_SEQ, D_MODEL)
    ln = lambda l, k: (ln_g[l, k].reshape(1, D_MODEL), ln_b[l, k].reshape(1, D_MODEL))

    wg, wu, wd = ffn_w_gate.astype(BF16), ffn_w_up.astype(BF16), ffn_w_down.astype(BF16)

    def ffn_pair(yp, ys, l, k, ln_idx):
        g, b = ln(l, ln_idx)
        return _ffn_pair(yp, ys, wg, wu, wd, g, b, l, k)

    yp, ys = ffn_pair(yp, ys, 0, 0, 0)
    yp, ys, p_cache, s_cache = _attn_layer(yp, ys, cache_a_k[0], cache_a_v[0], cache_b_k[0], cache_b_v[0],
                                           attn_w_in[0], attn_sinks[0], attn_w_out[0], *ln(0, 1),
                                           ffn_args=(wg, wu, wd, *ln(0, 2), 0, 1))
    ys = _ffn(ys, wg, wu, wd, *ln(0, 2), 0, 1)
    yp, ys = ffn_pair(yp, ys, 1, 0, 0)
    yp, ys, p_state, s_state = _ssm_layer(yp, ys, state_c_re[0], state_c_im[0], ssm_w_in[0], ssm_lambda_re[0],
                                          ssm_lambda_im[0], ssm_log_dt[0], ssm_b_re[0], ssm_b_im[0], ssm_c_re[0],
                                          ssm_c_im[0], ssm_d[0], ssm_w_glu[0], ssm_b_glu[0], ssm_w_out[0], *ln(1, 1))
    yp, ys = ffn_pair(yp, ys, 1, 1, 2)
    return (yp.reshape(BATCH, SEQ, D_MODEL), ys.reshape(DEC_BATCH, DEC_SEQ, D_MODEL),
            *p_cache, *p_state, *s_cache, *s_state)
```

```python
import functools
import math

import jax
import jax.numpy as jnp
from jax import lax
from jax.experimental import pallas as pl
from jax.experimental.pallas import tpu as pltpu

F32 = jnp.float32
BF16 = jnp.bfloat16

D_MODEL = 1024
BATCH = 4
SEQ = 4096
DEPTH = 2
DEC_BATCH = 128
DEC_SEQ = 8
PAST_LEN = 16384
HEAD_DIM = 64
N_HEADS_A = 8
DILATIONS = (1, 4, 16)
WIN_A = 2048
N_HEADS_B = 8
N_KV_B = 2
WIN_B = 128
ROPE_THETA = 10000.0
D_A = N_HEADS_A * HEAD_DIM
D_BQ = N_HEADS_B * HEAD_DIM
D_BKV = N_KV_B * HEAD_DIM
D_IN_ATTN = 3 * D_A + D_BQ + 2 * D_BKV
SSM_GROUP = 16
N_SSM_GROUPS = D_MODEL // SSM_GROUP
SSM_STATE = 64
N_STATE = N_SSM_GROUPS * SSM_STATE
D_FF = 2816
DN_ALPHA = (2 * DEPTH) ** 0.25
FFN_RES = 0.5
LN_EPS = 1e-5
ATTN_SCALE = HEAD_DIM ** -0.5
LOG2E = math.log2(math.e)
LSE_LANES = 16

LANES = 128
SUBLANES = 8
MXU_N = 256
VMEM_LIMIT = 56 * 1024 * 1024

ROW_TILE = 512
WIDE_ROW_TILE = 1024
FFN_ROW_TILE = 1024
FF_CHUNK = MXU_N
TQ = 128
BAND_SUB_TILES = 4
N_SEG = SUBLANES
SEG_LEN = SEQ // N_SEG
N_LCHUNK = D_MODEL // LANES
ST_CHUNK = N_STATE // N_LCHUNK
NK_PAD = WIN_A + LANES
NKB_PAD = 2 * WIN_B

NEG_INF = float("-inf")


def _params(n_axes, vmem=VMEM_LIMIT):
    return pltpu.CompilerParams(dimension_semantics=("arbitrary",) * n_axes, vmem_limit_bytes=vmem)


def _resident(shape):
    return pl.BlockSpec(shape, lambda *_: (0,) * len(shape), pipeline_mode=pl.Buffered(1))


def _layer_norm(x, g, b):
    mu = jnp.mean(x, -1, keepdims=True)
    xc = x - mu
    var = jnp.mean(xc * xc, -1, keepdims=True)
    return xc * lax.rsqrt(var + LN_EPS) * g + b


def _dot(a, b):
    return jnp.dot(a, b, preferred_element_type=F32)


def _dot_nt(a, b):
    return lax.dot_general(a, b, (((1,), (1,)), ((), ())), preferred_element_type=F32)


def _ffn_kernel(x_ref, wg_ref, wu_ref, wd_ref, g_ref, b_ref, o_ref, h_ref):
    x = x_ref[...]
    xb = x.astype(BF16)
    for c in range(D_FF // FF_CHUNK):
        sl = slice(c * FF_CHUNK, (c + 1) * FF_CHUNK)
        gate = _dot(xb, wg_ref[:, sl])
        up = _dot(xb, wu_ref[:, sl])
        h_ref[:, sl] = (gate * jax.nn.sigmoid(gate) * up).astype(BF16)
    y = DN_ALPHA * x + FFN_RES * _dot(h_ref[...], wd_ref[...])
    o_ref[...] = _layer_norm(y, g_ref[...], b_ref[...])


def _ffn(x, wg, wu, wd, g, b, layer=0, which=0):
    n = x.shape[0]
    tm = min(FFN_ROW_TILE, n)
    row = pl.BlockSpec((tm, D_MODEL), lambda i: (i, 0))
    if wg.ndim == 4:
        pick = lambda r, c: pl.BlockSpec((None, None, r, c), lambda i: (layer, which, 0, 0),
                                         pipeline_mode=pl.Buffered(1))
    else:
        pick = lambda r, c: _resident((r, c))
    return pl.pallas_call(
        _ffn_kernel,
        grid=(n // tm,),
        in_specs=[row, pick(D_MODEL, D_FF), pick(D_MODEL, D_FF), pick(D_FF, D_MODEL),
                  _resident((1, D_MODEL)), _resident((1, D_MODEL))],
        out_specs=row,
        out_shape=jax.ShapeDtypeStruct((n, D_MODEL), F32),
        scratch_shapes=[pltpu.VMEM((tm, D_FF), BF16)],
        compiler_params=_params(1),
        name="ffn",
    )(x, wg, wu, wd, g, b)


def _ffn_pair_kernel(n_first, xp_ref, xs_ref, wg_ref, wu_ref, wd_ref, g_ref, b_ref, op_ref, os_ref, h_ref):
    step = pl.program_id(0)
    weights = (wg_ref, wu_ref, wd_ref, g_ref, b_ref)
    pl.when(step < n_first)(functools.partial(_ffn_kernel, xp_ref, *weights, op_ref, h_ref))
    h_small = h_ref.at[pl.ds(0, xs_ref.shape[0])]
    pl.when(step >= n_first)(functools.partial(_ffn_kernel, xs_ref, *weights, os_ref, h_small))


def _ffn_pair(xp, xs, wg, wu, wd, g, b, layer, which):
    tm, tm2 = min(FFN_ROW_TILE, xp.shape[0]), ROW_TILE
    n_first, n_second = xp.shape[0] // tm, xs.shape[0] // tm2
    first = pl.BlockSpec((tm, D_MODEL), lambda i: (jnp.minimum(i, n_first - 1), 0))
    second = pl.BlockSpec((tm2, D_MODEL), lambda i: (jnp.maximum(i - n_first, 0), 0))
    pick = lambda r, c: pl.BlockSpec((None, None, r, c), lambda i: (layer, which, 0, 0), pipeline_mode=pl.Buffered(1))
    return pl.pallas_call(
        functools.partial(_ffn_pair_kernel, n_first),
        grid=(n_first + n_second,),
        in_specs=[first, second, pick(D_MODEL, D_FF), pick(D_MODEL, D_FF), pick(D_FF, D_MODEL),
                  _resident((1, D_MODEL)), _resident((1, D_MODEL))],
        out_specs=[first, second],
        out_shape=[jax.ShapeDtypeStruct(xp.shape, F32), jax.ShapeDtypeStruct(xs.shape, F32)],
        scratch_shapes=[pltpu.VMEM((tm, D_FF), BF16)],
        compiler_params=_params(1),
        name="ffn_pair",
    )(xp, xs, wg, wu, wd, g, b)


def _rope_tables(pos):
    half = HEAD_DIM // 2
    inv_freq = ROPE_THETA ** (-jnp.arange(half, dtype=F32) / half)
    ang = pos.astype(F32)[:, None] * inv_freq[None, :]
    cos, sin = jnp.cos(ang), jnp.sin(ang)
    cos_t = jnp.concatenate([cos, cos, cos, cos], -1)
    sin_t = jnp.concatenate([-sin, sin, -sin, sin], -1)
    return cos_t, sin_t


def _attn_proj_kernel(tiles_per_seq, x_ref, w_ref, cos_ref, sin_ref, qa_ref, ka_ref, va_ref, qb_ref, kb_ref, vb_ref,
                      *extra):
    xb = x_ref[...].astype(BF16)
    slab_ref = extra[-1] if extra else None
    dilated = extra[:6]
    n_chunks = D_A // LANES
    tm = x_ref.shape[0]

    def keep(tensor, c, val):
        if slab_ref is not None:
            slab_ref[tensor * n_chunks + c] = val
    cos = cos_ref[...]
    sin = sin_ref[...]
    lane = lax.broadcasted_iota(jnp.int32, cos.shape, 1)
    first_half = (lane & (HEAD_DIM // 2)) == 0

    def rope(z):
        rot = jnp.where(first_half, pltpu.roll(z, LANES - HEAD_DIM // 2, 1), pltpu.roll(z, HEAD_DIM // 2, 1))
        return z * cos + rot * sin

    def project(col0, ncols):
        return _dot(xb, w_ref[:, col0:col0 + ncols])

    def rope_chunks(z):
        return [rope(z[:, c * LANES:(c + 1) * LANES]) for c in range(z.shape[1] // LANES)]

    q_scale = ATTN_SCALE * LOG2E if extra else ATTN_SCALE
    col = 0
    for c, r in enumerate(rope_chunks(project(col, D_A))):
        r = r * q_scale
        qa_ref[:, c * LANES:(c + 1) * LANES] = r.astype(qa_ref.dtype)
        keep(0, c, r)
    col += D_A
    for c, r in enumerate(rope_chunks(project(col, D_A))):
        ka_ref[:, c * LANES:(c + 1) * LANES] = r.astype(ka_ref.dtype)
        keep(1, c, r)
    col += D_A
    z = project(col, D_A)
    va_ref[...] = z.astype(va_ref.dtype)
    for c in range(n_chunks):
        keep(2, c, z[:, c * LANES:(c + 1) * LANES])
    col += D_A
    if extra:
        kat_ref, vat_ref, kbt_ref, vbt_ref = extra[6:10]
        tile_in_seq = pl.program_id(0) % tiles_per_seq

        @pl.when(tile_in_seq >= tiles_per_seq - WIN_A // tm)
        def _():
            for c in range(n_chunks):
                kat_ref[c * LANES:(c + 1) * LANES, :] = slab_ref[n_chunks + c].T
                vat_ref[c * LANES:(c + 1) * LANES, :] = slab_ref[2 * n_chunks + c].T

        slab4_ref = extra[-2]
        d1, d2 = DILATIONS[1], DILATIONS[2] // DILATIONS[1]
        plane = tm // d1
        for tensor in range(3):
            out1_ref, out2_ref = dilated[tensor], dilated[3 + tensor]
            for c in range(n_chunks):
                idx = tensor * n_chunks + c
                lanes = slice(c * LANES, (c + 1) * LANES)
                for r in range(d1):
                    rows = slab_ref[idx, pl.ds(r, plane, stride=d1), :]
                    slab4_ref[idx, r * plane:(r + 1) * plane, :] = rows
                    out1_ref[r, :, lanes] = rows.astype(out1_ref.dtype)
                for r in range(d1):
                    for m in range(d2):
                        rows = slab4_ref[idx, pl.ds(r * plane + m, plane // d2, stride=d2), :]
                        out2_ref[r + d1 * m, :, lanes] = rows.astype(out2_ref.dtype)
    for c, r in enumerate(rope_chunks(project(col, D_BQ))):
        qb_ref[:, c * LANES:(c + 1) * LANES] = (r * q_scale).astype(qb_ref.dtype)
    col += D_BQ
    z = project(col, 2 * D_BKV)
    r = rope(z[:, :D_BKV])
    if extra:
        lo = _lane_lo(r.shape)
        for ref, val in ((kb_ref, r), (vb_ref, z[:, D_BKV:])):
            swapped = pltpu.roll(val, HEAD_DIM, 1)
            ref[:, 0:LANES] = jnp.where(lo, val, swapped).astype(ref.dtype)
            ref[:, LANES:] = jnp.where(lo, swapped, val).astype(ref.dtype)
    else:
        kb_ref[...] = r.astype(kb_ref.dtype)
        vb_ref[...] = z[:, D_BKV:].astype(vb_ref.dtype)
    if extra:
        @pl.when(tile_in_seq == tiles_per_seq - 1)
        def _():
            kbt_ref[...] = r[tm - WIN_B:, :].T
            vbt_ref[...] = z[tm - WIN_B:, D_BKV:].T


def _attn_proj(x, w, cos_t, sin_t, table_block, act_dtype, prompt_seq=None):
    n = x.shape[0]
    tm = min(ROW_TILE, n)

    def row(width):
        return pl.BlockSpec((tm, width), lambda i: (i, 0))

    tab = pl.BlockSpec((tm, LANES), lambda i: (table_block(i), 0))
    kv_b = D_BKV if prompt_seq is None else N_KV_B * LANES
    widths = (D_A, D_A, D_A, D_BQ, kv_b, kv_b)
    out_shape = [jax.ShapeDtypeStruct((n, wd), act_dtype) for wd in widths]
    out_specs = [row(wd) for wd in widths]
    scratch = []
    tps = None
    if prompt_seq is not None:
        bsz, seq = prompt_seq
        tps = seq // tm
        for dil in DILATIONS[1:]:
            out_shape += [jax.ShapeDtypeStruct((bsz, dil, seq // dil, D_A), BF16)] * 3
            out_specs += [pl.BlockSpec((None, dil, tm // dil, D_A), lambda i: (i // tps, 0, i % tps, 0))] * 3
        first_tail = tps - WIN_A // tm
        out_shape += [jax.ShapeDtypeStruct((bsz, D_A, WIN_A), F32)] * 2
        out_specs += [pl.BlockSpec((None, D_A, tm), lambda i: (i // tps, 0, jnp.maximum(i % tps - first_tail, 0)))] * 2
        out_shape += [jax.ShapeDtypeStruct((bsz, D_BKV, WIN_B), F32)] * 2
        out_specs += [pl.BlockSpec((None, D_BKV, WIN_B), lambda i: (i // tps, 0, 0))] * 2
        assert DILATIONS[2] == DILATIONS[1] ** 2
        scratch = [pltpu.VMEM((3 * D_A // LANES, tm, LANES), F32)] * 2
    return pl.pallas_call(
        functools.partial(_attn_proj_kernel, tps),
        grid=(n // tm,),
        in_specs=[row(D_MODEL), _resident((D_MODEL, D_IN_ATTN)), tab, tab],
        out_specs=out_specs,
        out_shape=out_shape,
        scratch_shapes=scratch,
        compiler_params=_params(1),
        name="attn_proj",
    )(x, w, cos_t, sin_t)


def _lane_lo(shape):
    return lax.broadcasted_iota(jnp.int32, shape, 1) < HEAD_DIM


def _half_masks_bf16():
    lo = jnp.where(_lane_lo((1, LANES)), 1.0, 0.0).astype(BF16)
    return lo, 1 - lo


def _band_masks(n_heads, t, sub):
    row = lax.broadcasted_iota(jnp.int32, (n_heads * TQ, TQ), 0) & (TQ - 1)
    col = lax.broadcasted_iota(jnp.int32, (n_heads * TQ, TQ), 1)
    shift = jnp.where(t > 0, 0, TQ) if sub == 0 else 0
    return col <= row, col >= row + shift


def _sub_tile_kv(sub, sl, kc_ref, kp_ref, vc_ref, vp_ref):
    cur = slice(sub * TQ, (sub + 1) * TQ)
    if sub == 0:
        return kc_ref[cur, sl], kp_ref[:, sl], vc_ref[cur, sl], vp_ref[:, sl]
    prev = slice((sub - 1) * TQ, sub * TQ)
    return kc_ref[cur, sl], kc_ref[prev, sl], vc_ref[cur, sl], vc_ref[prev, sl]


def _band_softmax(qs, kc, kp, vc, vp, mask_c, mask_p):
    s_c = jnp.where(mask_c, _dot_nt(qs, kc), NEG_INF)
    s_p = jnp.where(mask_p, _dot_nt(qs, kp), NEG_INF)
    m = jnp.max(jnp.maximum(s_c, s_p), -1, keepdims=True)
    p_c = jnp.exp2(s_c - m)
    p_p = jnp.exp2(s_p - m)
    den = jnp.sum(p_c + p_p, -1, keepdims=True)
    acc = _dot(p_c.astype(BF16), vc) + _dot(p_p.astype(BF16), vp)
    return acc * (1.0 / den), m, den


def _band_a_kernel(q_ref, kc_ref, kp_ref, vc_ref, vp_ref, o_ref, lse_ref):
    t = pl.program_id(2)
    lo = _lane_lo((TQ, LANES))
    lo_bf, hi_bf = _half_masks_bf16()
    lane_head = jnp.right_shift(lax.broadcasted_iota(jnp.int32, (TQ, LANES), 1), int(math.log2(LSE_LANES)))
    for plane in range(q_ref.shape[0]):
        refs = (kc_ref.at[plane], kp_ref.at[plane], vc_ref.at[plane], vp_ref.at[plane])
        for sub in range(q_ref.shape[1] // TQ):
            rows = slice(sub * TQ, (sub + 1) * TQ)
            mask_c, mask_p = _band_masks(2, t, sub)
            lse_tile = jnp.zeros((TQ, LANES), F32)
            for c in range(D_A // LANES):
                sl = slice(c * LANES, (c + 1) * LANES)
                kc, kp, vc, vp = _sub_tile_kv(sub, sl, *refs)
                q2 = q_ref[plane, rows, sl]
                qs = jnp.concatenate([q2 * lo_bf, q2 * hi_bf], axis=0)
                out, m, den = _band_softmax(qs, kc, kp, vc, vp, mask_c, mask_p)
                lse = m + jnp.log2(den)
                o_ref[plane, rows, sl] = jnp.where(lo, out[0:TQ], out[TQ:]).astype(o_ref.dtype)
                lse_tile = jnp.where(lane_head == 2 * c, lse[0:TQ], lse_tile)
                lse_tile = jnp.where(lane_head == 2 * c + 1, lse[TQ:], lse_tile)
            lse_ref[plane, rows, :] = lse_tile


def _band_a(q, k, v):
    bsz, dil, sub, _ = q.shape
    n_sub = min(BAND_SUB_TILES, sub // TQ)
    n_planes = BAND_SUB_TILES // n_sub
    assert dil % n_planes == 0 and sub % (n_sub * TQ) == 0
    cur = pl.BlockSpec((None, n_planes, n_sub * TQ, D_A), lambda b, r, t: (b, r, t, 0))
    prev = pl.BlockSpec((None, n_planes, TQ, D_A), lambda b, r, t: (b, r, jnp.maximum(n_sub * t - 1, 0), 0))
    lse = pl.BlockSpec((None, n_planes, n_sub * TQ, LANES), lambda b, r, t: (b, r, t, 0))
    return pl.pallas_call(
        _band_a_kernel,
        grid=(bsz, dil // n_planes, sub // (n_sub * TQ)),
        in_specs=[cur, cur, prev, cur, prev],
        out_specs=[cur, lse],
        out_shape=[jax.ShapeDtypeStruct((bsz, dil, sub, D_A), BF16),
                   jax.ShapeDtypeStruct((bsz, dil, sub, LANES), F32)],
        compiler_params=_params(3),
        name=f"band_a_d{dil}",
    )(q, k, k, v, v)


def _band_b_body(first_tile, sink_ref, q_ref, kc_ref, kp_ref, vc_ref, vp_ref, o_ref):
    group = N_HEADS_B // N_KV_B
    lo = _lane_lo((TQ, LANES))
    lo_bf, hi_bf = _half_masks_bf16()
    row = lax.broadcasted_iota(jnp.int32, (group * TQ, TQ), 0) & (TQ - 1)
    col = lax.broadcasted_iota(jnp.int32, (group * TQ, TQ), 1)
    in_cur = col <= row
    cur_bf = jnp.where(in_cur, 1.0, 0.0).astype(BF16)
    prev_bf = 1 - cur_bf
    for sub in range(q_ref.shape[0] // TQ):
        rows = slice(sub * TQ, (sub + 1) * TQ)
        for g in range(N_KV_B):
            sl = slice(g * LANES, (g + 1) * LANES)
            kc, kp, vc, vp = _sub_tile_kv(sub, sl, kc_ref, kp_ref, vc_ref, vp_ref)
            heads = range(g * group, (g + 1) * group)
            qs = jnp.concatenate(
                [q_ref[rows, (h // 2) * LANES:(h // 2 + 1) * LANES] * (lo_bf if h % 2 == 0 else hi_bf) for h in heads],
                axis=0)
            sink = jnp.concatenate([jnp.full((TQ, 1), sink_ref[h] * LOG2E, F32) for h in heads], axis=0)
            only_cur = first_tile and sub == 0
            s = jnp.where(in_cur, _dot_nt(qs, kc), NEG_INF if only_cur else _dot_nt(qs, kp))
            m = jnp.maximum(jnp.max(s, -1, keepdims=True), sink)
            p = jnp.exp2(s - m)
            den = jnp.sum(p, -1, keepdims=True) + jnp.exp2(sink - m)
            pb = p.astype(BF16)
            acc = _dot(pb, vc) if only_cur else _dot(pb * cur_bf, vc) + _dot(pb * prev_bf, vp)
            out = acc * (1.0 / den)
            for i in range(group // 2):
                c = g * (group // 2) + i
                even, odd = out[2 * i * TQ:(2 * i + 1) * TQ], out[(2 * i + 1) * TQ:(2 * i + 2) * TQ]
                o_ref[rows, c * LANES:(c + 1) * LANES] = jnp.where(lo, even, odd).astype(o_ref.dtype)


def _band_b_kernel(*refs):
    t = pl.program_id(1)
    pl.when(t == 0)(functools.partial(_band_b_body, True, *refs))
    pl.when(t > 0)(functools.partial(_band_b_body, False, *refs))


def _band_b(q, k, v, sinks):
    bsz, seq, _ = q.shape
    kv_lanes = N_KV_B * LANES
    n_sub = BAND_SUB_TILES
    qs = pl.BlockSpec((None, n_sub * TQ, D_BQ), lambda b, t: (b, t, 0))
    cur = pl.BlockSpec((None, n_sub * TQ, kv_lanes), lambda b, t: (b, t, 0))
    prev = pl.BlockSpec((None, TQ, kv_lanes), lambda b, t: (b, jnp.maximum(n_sub * t - 1, 0), 0))
    o = pl.pallas_call(
        _band_b_kernel,
        grid=(bsz, seq // (n_sub * TQ)),
        in_specs=[pl.BlockSpec(memory_space=pltpu.SMEM), qs, cur, prev, cur, prev],
        out_specs=qs,
        out_shape=jax.ShapeDtypeStruct((bsz, seq, D_BQ), BF16),
        compiler_params=_params(2),
        name="band_b",
    )(sinks, q, k, k, v, v)
    return o.reshape(bsz * seq, D_BQ)


def _pattern_count(dist):
    cnt = jnp.zeros(dist.shape, F32)
    for dil in DILATIONS:
        cnt = cnt + ((dist >= 0) & (dist <= 128 * dil) & (dist % dil == 0)).astype(F32)
    return cnt


def _sample_tables():
    i = jnp.arange(DEC_SEQ)
    cnt_c = _pattern_count(WIN_A + i[:, None] - jnp.arange(WIN_A)[None, :])
    j = jnp.arange(LANES)
    cnt_n = jnp.where(j[None, :] < DEC_SEQ, _pattern_count(i[:, None] - j[None, :]), 0.0)
    cnt_n = jnp.tile(cnt_n, (N_HEADS_A, 1))
    jb = jnp.arange(NKB_PAD)[None, :]
    dist_b = WIN_B + i[:, None] - jb
    ok_b = (dist_b >= 0) & (dist_b < WIN_B) & (jb < WIN_B + DEC_SEQ)
    mask_b = jnp.tile(ok_b.astype(F32), (N_HEADS_B, 1))
    return cnt_c, cnt_n, mask_b


def _sample_attend(q, kan, van, kt_ref, vt_ref, cnt_c, cnt_n, qb, kbn, vbn, kbc, vbc, mask_b, sink_col, kb_s, vb_s):
    rows = N_HEADS_A * DEC_SEQ
    q_rep = jnp.concatenate([q] * N_HEADS_A, axis=0)
    row_head = jnp.right_shift(lax.broadcasted_iota(jnp.int32, (rows, D_A), 0), int(math.log2(DEC_SEQ)))
    lane_head = jnp.right_shift(lax.broadcasted_iota(jnp.int32, (rows, D_A), 1), int(math.log2(HEAD_DIM)))
    own = row_head == lane_head
    q_bd = jnp.where(own, q_rep, 0.0).astype(BF16)
    pad = jnp.zeros((LANES - DEC_SEQ, D_A), F32)
    kn = jnp.concatenate([kan, pad], 0).astype(BF16)
    vn = jnp.concatenate([van, pad], 0).astype(BF16)
    s_new = jnp.where(cnt_n > 0.0, _dot_nt(q_bd, kn), NEG_INF)
    outs, p_new = [], []
    for h in range(N_HEADS_A):
        head_rows = slice(h * DEC_SEQ, (h + 1) * DEC_SEQ)
        q_h = q[:, h * HEAD_DIM:(h + 1) * HEAD_DIM].astype(BF16)
        s_c = jnp.where(cnt_c > 0.0, _dot(q_h, kt_ref[h].astype(BF16)), NEG_INF)
        s_n = s_new[head_rows]
        m = jnp.maximum(jnp.max(s_c, -1, keepdims=True), jnp.max(s_n, -1, keepdims=True))
        p_c = jnp.exp(s_c - m) * cnt_c
        p_n = jnp.exp(s_n - m) * cnt_n[head_rows]
        inv = 1.0 / (jnp.sum(p_c, -1, keepdims=True) + jnp.sum(p_n, -1, keepdims=True))
        outs.append(_dot_nt(p_c.astype(BF16), vt_ref[h].astype(BF16)) * inv)
        p_new.append(p_n * inv)
    out_n = jnp.where(own, _dot(jnp.concatenate(p_new, axis=0).astype(BF16), vn), 0.0)
    oa = jnp.concatenate(outs, axis=1)
    for h in range(N_HEADS_A):
        oa = oa + out_n[h * DEC_SEQ:(h + 1) * DEC_SEQ]

    n_pad_b = NKB_PAD - WIN_B - DEC_SEQ
    pad_b = jnp.zeros((n_pad_b, D_BKV), F32)
    kb_s[...] = jnp.concatenate([kbc, kbn, pad_b], 0).astype(BF16)
    vb_s[...] = jnp.concatenate([vbc, vbn, pad_b], 0).astype(BF16)
    lo8 = _lane_lo((DEC_SEQ, LANES))
    group = N_HEADS_B // N_KV_B
    pieces = []
    for h in range(N_HEADS_B):
        chunk = qb[:, (h // 2) * LANES:(h // 2 + 1) * LANES]
        g = h // group
        if h % 2 != g:
            chunk = pltpu.roll(chunk, HEAD_DIM, 1)
        pieces.append(jnp.where(lo8 if g == 0 else jnp.logical_not(lo8), chunk, 0.0))
    qb_bd = jnp.concatenate(pieces, axis=0).astype(BF16)
    sb = jnp.where(mask_b > 0.0, _dot_nt(qb_bd, kb_s[...]), NEG_INF)
    sink = sink_col[:, 0:1]
    mb = jnp.maximum(jnp.max(sb, -1, keepdims=True), sink)
    pb = jnp.exp(sb - mb) * mask_b
    den_b = jnp.sum(pb, -1, keepdims=True) + jnp.exp(sink - mb)
    ob_full = _dot(pb.astype(BF16), vb_s[...]) * (1.0 / den_b)
    ob = []
    for c in range(D_BQ // LANES):
        halves = []
        for half in range(2):
            h = 2 * c + half
            piece = ob_full[h * DEC_SEQ:(h + 1) * DEC_SEQ]
            if half != h // group:
                piece = pltpu.roll(piece, HEAD_DIM, 1)
            halves.append(piece)
        ob.append(jnp.where(lo8, halves[0], halves[1]))
    return jnp.concatenate([oa] + ob, axis=1)


FUSED_ROW_TILE = 512
SEQ_PER_STEP = DEC_BATCH // (BATCH * SEQ // FUSED_ROW_TILE)


def _kv_copies(kt_hbm, vt_hbm, kbuf, vbuf, sems, seq, slot):
    return (pltpu.make_async_copy(kt_hbm.at[seq], kbuf.at[slot], sems.at[0, slot]),
            pltpu.make_async_copy(vt_hbm.at[seq], vbuf.at[slot], sems.at[1, slot]))


def _ffn_attn_kernel(x_ref, wg_ref, wu_ref, wd_ref, g_ref, b_ref,
                     qa_ref, kan_ref, van_ref, kt_hbm, vt_hbm, cnt_c_ref, cnt_n_ref,
                     qb_ref, kbn_ref, vbn_ref, kbc_ref, vbc_ref, maskb_ref, sinkcol_ref,
                     y_ref, o_ref, h_ref, kbuf, vbuf, sems, kb_s, vb_s):
    step = pl.program_id(0)
    n_steps = pl.num_programs(0)
    copies = functools.partial(_kv_copies, kt_hbm, vt_hbm, kbuf, vbuf, sems)

    @pl.when(step == 0)
    def _():
        for cp in copies(0, 0):
            cp.start()

    x = x_ref[...]
    xb = x.astype(BF16)
    n_ff = D_FF // FF_CHUNK
    per_seq = -(-n_ff // SEQ_PER_STEP)
    cnt_c, cnt_n, mask_b, sink_col = cnt_c_ref[...], cnt_n_ref[...], maskb_ref[...], sinkcol_ref[...]
    for s in range(SEQ_PER_STEP):
        slot = s % 2
        seq = step * SEQ_PER_STEP + s
        for cp in copies(seq, slot):
            cp.wait()
        if s + 1 < SEQ_PER_STEP:
            for cp in copies(seq + 1, 1 - slot):
                cp.start()
        else:
            @pl.when(step + 1 < n_steps)
            def _():
                for cp in copies(seq + 1, 1 - slot):
                    cp.start()
        rows = slice(s * DEC_SEQ, (s + 1) * DEC_SEQ)
        kbc = jnp.concatenate([kbc_ref[s, g].T for g in range(N_KV_B)], axis=1)
        vbc = jnp.concatenate([vbc_ref[s, g].T for g in range(N_KV_B)], axis=1)
        o_ref[rows, :] = _sample_attend(
            qa_ref[rows, :], kan_ref[rows, :], van_ref[rows, :], kbuf.at[slot], vbuf.at[slot], cnt_c, cnt_n,
            qb_ref[rows, :], kbn_ref[rows, :], vbn_ref[rows, :], kbc, vbc, mask_b, sink_col, kb_s, vb_s)
        for c in range(s * per_seq, min((s + 1) * per_seq, n_ff)):
            sl = slice(c * FF_CHUNK, (c + 1) * FF_CHUNK)
            gate = _dot(xb, wg_ref[:, sl])
            up = _dot(xb, wu_ref[:, sl])
            h_ref[:, sl] = (gate * jax.nn.sigmoid(gate) * up).astype(BF16)
    y = DN_ALPHA * x + FFN_RES * _dot(h_ref[...], wd_ref[...])
    y_ref[...] = _layer_norm(y, g_ref[...], b_ref[...])


def _ffn_with_sample_attention(x, wg, wu, wd, g, b, layer, which,
                               qa, kan, van, cache_ak, cache_av, qb, kbn, vbn, cache_bk, cache_bv, sinks):
    n = x.shape[0]
    tm = FUSED_ROW_TILE
    assert n // tm * SEQ_PER_STEP == DEC_BATCH and SEQ_PER_STEP % 2 == 0
    cnt_c, cnt_n, mask_b = _sample_tables()
    sink_col = jnp.broadcast_to(jnp.repeat(sinks.astype(F32), DEC_SEQ)[:, None], (N_HEADS_B * DEC_SEQ, LANES))
    row = pl.BlockSpec((tm, D_MODEL), lambda i: (i, 0))
    pick = lambda r, c: pl.BlockSpec((None, None, r, c), lambda i: (layer, which, 0, 0), pipeline_mode=pl.Buffered(1))
    new = lambda width: pl.BlockSpec((SEQ_PER_STEP * DEC_SEQ, width), lambda i: (i, 0))
    cache_b = pl.BlockSpec((SEQ_PER_STEP, N_KV_B, HEAD_DIM, WIN_B), lambda i: (i, 0, 0, 0))
    hbm = pl.BlockSpec(memory_space=pl.ANY)
    rows = N_HEADS_A * DEC_SEQ
    kv_slot = (2, N_HEADS_A, HEAD_DIM, WIN_A)
    return pl.pallas_call(
        _ffn_attn_kernel,
        grid=(n // tm,),
        in_specs=[row, pick(D_MODEL, D_FF), pick(D_MODEL, D_FF), pick(D_FF, D_MODEL),
                  _resident((1, D_MODEL)), _resident((1, D_MODEL)),
                  new(D_A), new(D_A), new(D_A), hbm, hbm, _resident(cnt_c.shape), _resident(cnt_n.shape),
                  new(D_BQ), new(D_BKV), new(D_BKV), cache_b, cache_b,
                  _resident((rows, NKB_PAD)), _resident((rows, LANES))],
        out_specs=[row, new(D_A + D_BQ)],
        out_shape=[jax.ShapeDtypeStruct((n, D_MODEL), F32),
                   jax.ShapeDtypeStruct((DEC_BATCH * DEC_SEQ, D_A + D_BQ), F32)],
        scratch_shapes=[pltpu.VMEM((tm, D_FF), BF16), pltpu.VMEM(kv_slot, F32), pltpu.VMEM(kv_slot, F32),
                        pltpu.SemaphoreType.DMA((2, 2)),
                        pltpu.VMEM((NKB_PAD, D_BKV), BF16), pltpu.VMEM((NKB_PAD, D_BKV), BF16)],
        compiler_params=_params(1, vmem=60 * 1024 * 1024),
        name="ffn_attn_sample",
    )(x, wg, wu, wd, g, b, qa, kan, van, cache_ak, cache_av, cnt_c, cnt_n, qb, kbn, vbn, cache_bk, cache_bv,
      mask_b, sink_col)


def _attn_out_prompt_kernel(o1_ref, l1_ref, o4_ref, l4_ref, o16_ref, l16_ref, ob_ref, y_ref, w_ref, g_ref, b_ref,
                            out_ref, slab_ref, oa_ref):
    tm = y_ref.shape[0]
    n_chunks = D_A // LANES
    slabs = {}
    base = 0
    for name, dil, src, width in (("o4", 4, o4_ref, n_chunks), ("l4", 4, l4_ref, 1),
                                  ("o16", 16, o16_ref, n_chunks), ("l16", 16, l16_ref, 1)):
        slabs[name] = base
        for r in range(dil):
            for c in range(width):
                piece = src[r, :, c * LANES:(c + 1) * LANES]
                slab_ref[base + c, pl.ds(r, tm // dil, stride=dil), :] = piece.astype(F32)
        base += width
    l1, l4, l16 = l1_ref[...], slab_ref[slabs["l4"]], slab_ref[slabs["l16"]]
    m = jnp.maximum(jnp.maximum(l1, l4), l16)
    e1, e4, e16 = jnp.exp2(l1 - m), jnp.exp2(l4 - m), jnp.exp2(l16 - m)
    inv = 1.0 / (e1 + e4 + e16)
    row = lax.broadcasted_iota(jnp.int32, (LANES, D_A), 0)
    head_of_col = jnp.right_shift(lax.broadcasted_iota(jnp.int32, (LANES, D_A), 1), int(math.log2(HEAD_DIM)))
    spread = jnp.where(row == head_of_col * LSE_LANES, 1.0, 0.0).astype(BF16)

    def per_head_lanes(w):
        hi = w.astype(BF16)
        lo = (w - hi.astype(F32)).astype(BF16)
        return _dot(hi, spread) + _dot(lo, spread)

    w1, w4, w16 = per_head_lanes(e1 * inv), per_head_lanes(e4 * inv), per_head_lanes(e16 * inv)
    for c in range(n_chunks):
        sl = slice(c * LANES, (c + 1) * LANES)
        oa = (w1[:, sl] * o1_ref[:, sl].astype(F32) + w4[:, sl] * slab_ref[slabs["o4"] + c]
              + w16[:, sl] * slab_ref[slabs["o16"] + c])
        oa_ref[:, sl] = oa.astype(BF16)
    mix = _dot(oa_ref[...], w_ref[0:D_A, :]) + _dot(ob_ref[...], w_ref[D_A:, :])
    out_ref[...] = _layer_norm(DN_ALPHA * y_ref[...] + mix, g_ref[...], b_ref[...])


def _attn_out_prompt(pats, ob, y, w, g, b, seq):
    n = y.shape[0]
    tm = WIDE_ROW_TILE
    tps = seq // tm
    half = pl.BlockSpec((tm, D_A), lambda i: (i, 0))
    full = pl.BlockSpec((tm, D_MODEL), lambda i: (i, 0))
    lse1 = pl.BlockSpec((tm, LANES), lambda i: (i, 0))
    planes = lambda dil, width: pl.BlockSpec((None, dil, tm // dil, width), lambda i: (i // tps, 0, i % tps, 0))
    (o1, l1), (o4, l4), (o16, l16) = pats
    return pl.pallas_call(
        _attn_out_prompt_kernel,
        grid=(n // tm,),
        in_specs=[half, lse1, planes(4, D_A), planes(4, LANES), planes(16, D_A), planes(16, LANES), half, full,
                  _resident((D_MODEL, D_MODEL)), _resident((1, D_MODEL)), _resident((1, D_MODEL))],
        out_specs=full,
        out_shape=jax.ShapeDtypeStruct((n, D_MODEL), F32),
        scratch_shapes=[pltpu.VMEM((2 * (D_A // LANES + 1), tm, LANES), F32), pltpu.VMEM((tm, D_A), BF16)],
        compiler_params=_params(1),
        name="attn_out_prompt",
    )(o1, l1, o4, l4, o16, l16, ob, y, w, g, b)


def _mix_out_kernel(o_ref, y_ref, w_ref, g_ref, b_ref, out_ref):
    mix = _dot(o_ref[...].astype(BF16), w_ref[...])
    out_ref[...] = _layer_norm(DN_ALPHA * y_ref[...] + mix, g_ref[...], b_ref[...])


def _attn_out_sample(o, y, w, g, b):
    n = y.shape[0]
    tm = min(ROW_TILE, n)
    full = pl.BlockSpec((tm, D_MODEL), lambda i: (i, 0))
    return pl.pallas_call(
        _mix_out_kernel,
        grid=(n // tm,),
        in_specs=[full, full, _resident((D_MODEL, D_MODEL)), _resident((1, D_MODEL)), _resident((1, D_MODEL))],
        out_specs=full,
        out_shape=jax.ShapeDtypeStruct((n, D_MODEL), F32),
        compiler_params=_params(1),
        name="attn_out_sample",
    )(o, y, w, g, b)


def _ssm_discretize(lam_re, lam_im, log_dt, b_re, b_im):
    dt = jnp.exp(log_dt.astype(F32))[:, None]
    lr, li = lam_re.astype(F32), lam_im.astype(F32)
    mag = jnp.exp(lr * dt)
    ab_re, ab_im = mag * jnp.cos(li * dt), mag * jnp.sin(li * dt)
    nr, ni = ab_re - 1.0, ab_im
    den = lr * lr + li * li
    fr, fi = (nr * lr + ni * li) / den, (ni * lr - nr * li) / den
    bb_re = fr[..., None] * b_re - fi[..., None] * b_im
    bb_im = fr[..., None] * b_im + fi[..., None] * b_re
    return ab_re, ab_im, bb_re, bb_im


def _ssm_matrices(bb_re, bb_im, c_re, c_im):
    gpc = LANES // SSM_GROUP
    eye = jnp.eye(gpc, dtype=F32)

    def in_blocks(bb):
        a = bb.reshape(N_LCHUNK, gpc, SSM_STATE, SSM_GROUP)
        return jnp.einsum("jgpn,gh->jgnhp", a, eye).reshape(N_LCHUNK, LANES, ST_CHUNK)

    def out_blocks(cc):
        a = cc.reshape(N_LCHUNK, gpc, SSM_GROUP, SSM_STATE)
        return jnp.einsum("jgnp,gh->jgphn", a, eye).reshape(N_LCHUNK, ST_CHUNK, LANES)

    bmat = jnp.concatenate([in_blocks(bb_re), in_blocks(bb_im)], -1)
    cmat = jnp.concatenate([out_blocks(c_re), -out_blocks(c_im)], 1)
    return bmat, cmat


CHUNK = 8


def _dot3(a, b):
    a_hi, b_hi = a.astype(BF16), b.astype(BF16)
    a_lo, b_lo = (a - a_hi.astype(F32)).astype(BF16), (b - b_hi.astype(F32)).astype(BF16)
    return _dot(a_hi, b_hi) + _dot(a_hi, b_lo) + _dot(a_lo, b_hi)


def _chunk_weights_kernel(ar_ref, ai_ref, acr_ref, aci_ref, bmat_ref, cmat_ref,
                          we_ref, ws_ref, wi_ref, a8r_ref, a8i_ref):
    def powers(r, i, n):
        out = [(jnp.ones_like(r), jnp.zeros_like(r))]
        for _ in range(n):
            out.append(_cmul(out[-1][0], out[-1][1], r, i))
        return out

    row_pow = powers(ar_ref[...], ai_ref[...], CHUNK)
    col_pow = powers(acr_ref[...], aci_ref[...], CHUNK)
    b_re, b_im = bmat_ref[:, 0:ST_CHUNK], bmat_ref[:, ST_CHUNK:]
    c_re, c_im = cmat_ref[0:ST_CHUNK, :], -cmat_ref[ST_CHUNK:, :]

    def scaled_b(power):
        pr, pi = row_pow[power]
        return jnp.concatenate([b_re * pr - b_im * pi, b_re * pi + b_im * pr], axis=1)

    scaled = [scaled_b(tau) for tau in range(CHUNK)]
    for tau in range(CHUNK):
        we_ref[(CHUNK - 1 - tau) * LANES:(CHUNK - tau) * LANES, :] = scaled[tau].astype(BF16)
    all_taps = _dot3(jnp.concatenate(scaled, axis=0), cmat_ref[...]).astype(BF16)
    taps = [all_taps[tau * LANES:(tau + 1) * LANES] for tau in range(CHUNK)]
    for k in range(CHUNK):
        cols = slice(k * LANES, (k + 1) * LANES)
        qr, qi = col_pow[k + 1]
        ws_ref[0:ST_CHUNK, cols] = (c_re * qr - c_im * qi).astype(BF16)
        ws_ref[ST_CHUNK:, cols] = (-(c_re * qi + c_im * qr)).astype(BF16)
    zero = jnp.zeros((LANES, LANES), BF16)
    for k_in in range(CHUNK):
        for k_out in range(CHUNK):
            wi_ref[k_in * LANES:(k_in + 1) * LANES, k_out * LANES:(k_out + 1) * LANES] = (
                taps[k_out - k_in] if k_out >= k_in else zero)
    a8r_ref[...], a8i_ref[...] = row_pow[CHUNK]


def _chunk_weights(a_re, a_im, bmat, cmat):
    cols = lambda a: jnp.broadcast_to(a.reshape(N_LCHUNK, ST_CHUNK, 1), (N_LCHUNK, ST_CHUNK, LANES))
    per_j = lambda r, c: pl.BlockSpec((None, r, c), lambda j: (j, 0, 0))
    wide = 2 * ST_CHUNK
    return pl.pallas_call(
        _chunk_weights_kernel,
        grid=(N_LCHUNK,),
        in_specs=[per_j(1, ST_CHUNK), per_j(1, ST_CHUNK), per_j(ST_CHUNK, LANES), per_j(ST_CHUNK, LANES),
                  per_j(LANES, wide), per_j(wide, LANES)],
        out_specs=[per_j(CHUNK * LANES, wide), per_j(wide, CHUNK * LANES), per_j(CHUNK * LANES, CHUNK * LANES),
                   per_j(1, ST_CHUNK), per_j(1, ST_CHUNK)],
        out_shape=[jax.ShapeDtypeStruct((N_LCHUNK, CHUNK * LANES, wide), BF16),
                   jax.ShapeDtypeStruct((N_LCHUNK, wide, CHUNK * LANES), BF16),
                   jax.ShapeDtypeStruct((N_LCHUNK, CHUNK * LANES, CHUNK * LANES), BF16),
                   jax.ShapeDtypeStruct((N_LCHUNK, 1, ST_CHUNK), F32),
                   jax.ShapeDtypeStruct((N_LCHUNK, 1, ST_CHUNK), F32)],
        compiler_params=_params(1),
        name="ssm_chunk_weights",
    )(a_re, a_im, cols(a_re), cols(a_im), bmat, cmat)


SEG_TILE = WIDE_ROW_TILE // N_SEG


def _ssm_in_prompt_kernel(x_ref, w_ref, o_ref):
    x = x_ref[...].reshape(N_SEG * SEG_TILE, D_MODEL)
    u = _dot(x.astype(BF16), w_ref[...])
    for s in range(N_SEG):
        for c in range(N_LCHUNK):
            o_ref[c, pl.ds(s, SEG_TILE, stride=N_SEG), :] = u[s * SEG_TILE:(s + 1) * SEG_TILE, c * LANES:(c + 1) * LANES]


def _ssm_in_prompt(y, w):
    return pl.pallas_call(
        _ssm_in_prompt_kernel,
        grid=(BATCH, SEG_LEN // SEG_TILE),
        in_specs=[pl.BlockSpec((None, N_SEG, SEG_TILE, D_MODEL), lambda b, t: (b, 0, t, 0)),
                  _resident((D_MODEL, D_MODEL))],
        out_specs=pl.BlockSpec((None, N_LCHUNK, WIDE_ROW_TILE, LANES), lambda b, t: (b, 0, t, 0)),
        out_shape=jax.ShapeDtypeStruct((BATCH, N_LCHUNK, SEQ, LANES), F32),
        compiler_params=_params(2),
        name="ssm_in_prompt",
    )(y.reshape(BATCH, N_SEG, SEG_LEN, D_MODEL), w)


def _ssm_in_sample_kernel(x_ref, w_ref, o_ref, slab_ref):
    u = _dot(x_ref[...].astype(BF16), w_ref[...])
    for c in range(N_LCHUNK):
        slab_ref[c] = u[:, c * LANES:(c + 1) * LANES]
    for l in range(DEC_SEQ):
        for c in range(N_LCHUNK):
            o_ref[c, l * DEC_BATCH:(l + 1) * DEC_BATCH, :] = slab_ref[c, pl.ds(l, DEC_BATCH, stride=DEC_SEQ), :]


def _ssm_in_sample(y, w):
    n = DEC_BATCH * DEC_SEQ
    return pl.pallas_call(
        _ssm_in_sample_kernel,
        grid=(1,),
        in_specs=[_resident((n, D_MODEL)), _resident((D_MODEL, D_MODEL))],
        out_specs=pl.BlockSpec((N_LCHUNK, n, LANES), lambda i: (0, 0, 0)),
        out_shape=jax.ShapeDtypeStruct((N_LCHUNK, n, LANES), F32),
        scratch_shapes=[pltpu.VMEM((N_LCHUNK, n, LANES), F32)],
        compiler_params=_params(1),
        name="ssm_in_sample",
    )(y, w)


def _cmul(ar, ai, br, bi):
    return ar * br - ai * bi, ar * bi + ai * br


def _scan_prompt_kernel(u_ref, we_ref, ws_ref, wi_ref, a8r_ref, a8i_ref, d_ref, h0r_ref, h0i_ref,
                        y_ref, hnr_ref, hni_ref, e_s, hs_s):
    n_chunks = u_ref.shape[0]
    rows = n_chunks * N_SEG
    u_flat = jnp.concatenate([u_ref[:, k].reshape(rows, LANES) for k in range(CHUNK)], axis=1)
    ub = u_flat.astype(BF16)
    e_s[...] = _dot(ub, we_ref[...])
    a8r1, a8i1 = a8r_ref[...], a8i_ref[...]
    a8r = jnp.broadcast_to(a8r1, (N_SEG, ST_CHUNK))
    a8i = jnp.broadcast_to(a8i1, (N_SEG, ST_CHUNK))

    def advance(row, hr, hi):
        er = e_s[pl.ds(row, N_SEG), 0:ST_CHUNK]
        ei = e_s[pl.ds(row, N_SEG), ST_CHUNK:2 * ST_CHUNK]
        return a8r * hr - a8i * hi + er, a8r * hi + a8i * hr + ei

    def pass1(c, carry):
        return advance(pl.multiple_of(c * N_SEG, N_SEG), *carry)

    zero = jnp.zeros((N_SEG, ST_CHUNK), F32)
    er, ei = lax.fori_loop(0, n_chunks, pass1, (zero, zero), unroll=8)

    pr, pi = a8r1, a8i1
    for _ in range(int(math.log2(n_chunks))):
        pr, pi = _cmul(pr, pi, pr, pi)
    hr, hi = h0r_ref[...], h0i_ref[...]
    starts_r, starts_i = [], []
    for s in range(N_SEG):
        starts_r.append(hr)
        starts_i.append(hi)
        gr, gi = _cmul(pr, pi, hr, hi)
        hr, hi = gr + er[s:s + 1], gi + ei[s:s + 1]
    hnr_ref[...] = hr
    hni_ref[...] = hi
    init = (jnp.concatenate(starts_r, 0), jnp.concatenate(starts_i, 0))

    def pass2(c, carry):
        row = pl.multiple_of(c * N_SEG, N_SEG)
        hs_s[pl.ds(row, N_SEG), 0:ST_CHUNK] = carry[0]
        hs_s[pl.ds(row, N_SEG), ST_CHUNK:2 * ST_CHUNK] = carry[1]
        return advance(row, *carry)

    lax.fori_loop(0, n_chunks, pass2, init, unroll=8)
    y = _dot(hs_s[...].astype(BF16), ws_ref[...]) + _dot(ub, wi_ref[...]) + d_ref[...] * u_flat
    for k in range(CHUNK):
        y_ref[:, k] = y[:, k * LANES:(k + 1) * LANES].reshape(n_chunks, N_SEG, LANES)


def _scan_prompt(u, weights, d_skip, h0r, h0i):
    bsz, _, seq, _ = u.shape
    n_chunks = seq // (CHUNK * N_SEG)
    rows = n_chunks * N_SEG
    wide = 2 * ST_CHUNK
    split = lambda a: a.reshape(bsz, N_LCHUNK, n_chunks, CHUNK, N_SEG, LANES)
    chunk = pl.BlockSpec((None, None, n_chunks, CHUNK, N_SEG, LANES), lambda j, b: (b, j, 0, 0, 0, 0))
    per_j = lambda r, c: pl.BlockSpec((None, r, c), lambda j, b: (j, 0, 0))
    state = pl.BlockSpec((None, 1, ST_CHUNK), lambda j, b: (b, 0, j))
    y, hr, hi = pl.pallas_call(
        _scan_prompt_kernel,
        grid=(N_LCHUNK, bsz),
        in_specs=[chunk, per_j(CHUNK * LANES, wide), per_j(wide, CHUNK * LANES), per_j(CHUNK * LANES, CHUNK * LANES),
                  per_j(1, ST_CHUNK), per_j(1, ST_CHUNK), per_j(1, CHUNK * LANES), state, state],
        out_specs=[chunk, state, state],
        out_shape=[jax.ShapeDtypeStruct((bsz, N_LCHUNK, n_chunks, CHUNK, N_SEG, LANES), F32),
                   jax.ShapeDtypeStruct((bsz, 1, N_STATE), F32), jax.ShapeDtypeStruct((bsz, 1, N_STATE), F32)],
        scratch_shapes=[pltpu.VMEM((rows, wide), F32), pltpu.VMEM((rows, wide), F32)],
        compiler_params=_params(2),
        name="ssm_scan_prompt",
    )(split(u), *weights, d_skip, h0r, h0i)
    return y.reshape(u.shape), hr, hi


def _scan_sample_kernel(u_ref, bmat_ref, cmat_ref, are_ref, aim_ref, d_ref, h0r_ref, h0i_ref,
                        y_ref, hnr_ref, hni_ref, h_s):
    a_re, a_im = are_ref[...], aim_ref[...]
    hr, hi = h0r_ref[...].T, h0i_ref[...].T
    u = u_ref[...]
    bu = _dot(u.astype(BF16), bmat_ref[...])
    for l in range(DEC_SEQ):
        rows = slice(l * DEC_BATCH, (l + 1) * DEC_BATCH)
        gr, gi = _cmul(a_re, a_im, hr, hi)
        hr, hi = gr + bu[rows, :ST_CHUNK], gi + bu[rows, ST_CHUNK:]
        h_s[rows, 0:ST_CHUNK] = hr.astype(BF16)
        h_s[rows, ST_CHUNK:] = hi.astype(BF16)
    y_ref[...] = _dot(h_s[...], cmat_ref[...]) + d_ref[...] * u
    hnr_ref[...] = hr.T
    hni_ref[...] = hi.T


def _scan_sample(u, bmat, cmat, a_re, a_im, d_skip, h0r, h0i):
    n = DEC_SEQ * DEC_BATCH
    chunk = pl.BlockSpec((None, n, LANES), lambda j: (j, 0, 0))
    per_j = lambda r, c: pl.BlockSpec((None, r, c), lambda j: (j, 0, 0))
    state = pl.BlockSpec((ST_CHUNK, DEC_BATCH), lambda j: (j, 0))
    return pl.pallas_call(
        _scan_sample_kernel,
        grid=(N_LCHUNK,),
        in_specs=[chunk, per_j(LANES, 2 * ST_CHUNK), per_j(2 * ST_CHUNK, LANES), per_j(1, ST_CHUNK),
                  per_j(1, ST_CHUNK), per_j(1, LANES), state, state],
        out_specs=[chunk, state, state],
        out_shape=[jax.ShapeDtypeStruct((N_LCHUNK, n, LANES), F32),
                   jax.ShapeDtypeStruct((N_STATE, DEC_BATCH), F32), jax.ShapeDtypeStruct((N_STATE, DEC_BATCH), F32)],
        scratch_shapes=[pltpu.VMEM((n, 2 * ST_CHUNK), BF16)],
        compiler_params=_params(1),
        name="ssm_scan_sample",
    )(u, bmat, cmat, a_re, a_im, d_skip, h0r, h0i)


def _ssm_out_kernel(sample, s_ref, y_ref, wglu_ref, bglu_ref, wout_ref, g_ref, b_ref, out_ref, z_ref):
    n_rows = z_ref.shape[1]
    if sample:
        parts = [(pl.ds(l, DEC_BATCH, stride=DEC_SEQ), slice(l * DEC_BATCH, (l + 1) * DEC_BATCH))
                 for l in range(DEC_SEQ)]
    else:
        parts = [(slice(s * SEG_TILE, (s + 1) * SEG_TILE), pl.ds(s, SEG_TILE, stride=N_SEG)) for s in range(N_SEG)]
    for c in range(N_LCHUNK):
        for tok_rows, slab_rows in parts:
            z_ref[c, tok_rows, :] = s_ref[c, slab_rows, :]
    z = jax.nn.gelu(jnp.concatenate([z_ref[c] for c in range(N_LCHUNK)], axis=1))
    gate = jax.nn.sigmoid(_dot(z.astype(BF16), wglu_ref[...]) + bglu_ref[...])
    mix = _dot((z * gate).astype(BF16), wout_ref[...])
    res = y_ref[...].reshape(n_rows, D_MODEL)
    out = _layer_norm(DN_ALPHA * res + mix, g_ref[...], b_ref[...])
    out_ref[...] = out.reshape(out_ref.shape)


def _ssm_out(s, y, s_spec, y_spec, grid, rows, sample, w_glu, b_glu, w_out, g, b, name):
    return pl.pallas_call(
        functools.partial(_ssm_out_kernel, sample),
        grid=grid,
        in_specs=[s_spec, y_spec, _resident((D_MODEL, D_MODEL)), _resident((1, D_MODEL)),
                  _resident((D_MODEL, D_MODEL)), _resident((1, D_MODEL)), _resident((1, D_MODEL))],
        out_specs=y_spec,
        out_shape=jax.ShapeDtypeStruct(y.shape, F32),
        scratch_shapes=[pltpu.VMEM((N_LCHUNK, rows, LANES), F32)],
        compiler_params=_params(len(grid)),
        name=name,
    )(s, y, w_glu, b_glu, w_out, g, b)


def _attn_prompt(yp, w_in, sinks, w_out, g, b):
    cos_p, sin_p = _rope_tables(jnp.arange(SEQ))
    tiles_per_seq = SEQ // ROW_TILE
    qa, ka, va, qb, kb, vb, *extra = _attn_proj(
        yp, w_in, cos_p, sin_p, lambda i: i % tiles_per_seq, BF16, prompt_seq=(BATCH, SEQ))
    dilated, tails = extra[:6], extra[6:]
    seq3 = lambda a: a.reshape(BATCH, SEQ, a.shape[-1])
    plane1 = lambda a: a.reshape(BATCH, 1, SEQ, D_A)
    o1, l1 = _band_a(plane1(qa), plane1(ka), plane1(va))
    pats = [(o1.reshape(BATCH * SEQ, D_A), l1.reshape(BATCH * SEQ, LANES))]
    for i in range(len(DILATIONS) - 1):
        pats.append(_band_a(*dilated[3 * i:3 * i + 3]))
    ob = _band_b(seq3(qb), seq3(kb), seq3(vb), sinks)
    yp = _attn_out_prompt(pats, ob, yp, w_out, g, b, SEQ)
    heads = lambda a, nh: jnp.transpose(a.reshape(BATCH, nh, HEAD_DIM, a.shape[-1]), (0, 3, 1, 2))[None]
    prompt_cache = (heads(tails[0], N_HEADS_A), heads(tails[1], N_HEADS_A),
                    heads(tails[2], N_KV_B), heads(tails[3], N_KV_B))
    return yp, prompt_cache


def _attn_sample_path(ys, cache_ak, cache_av, cache_bk, cache_bv, w_in, sinks, w_out, g, b, yp, ffn_args):
    cos_s, sin_s = _rope_tables(PAST_LEN + jnp.arange(DEC_SEQ))
    reps = DEC_BATCH * DEC_SEQ // DEC_SEQ
    cos_s, sin_s = jnp.tile(cos_s, (reps, 1)), jnp.tile(sin_s, (reps, 1))
    qa, ka, va, qb, kb, vb = _attn_proj(ys, w_in, cos_s, sin_s, lambda i: i, F32)
    stored = lambda a: jnp.transpose(a, (0, 2, 3, 1))
    yp, o = _ffn_with_sample_attention(
        yp, *ffn_args, qa, ka, va, stored(cache_ak), stored(cache_av),
        qb, kb, vb, stored(cache_bk), stored(cache_bv), sinks)
    ys = _attn_out_sample(o, ys, w_out, g, b)
    new = lambda a, nh: a.reshape(1, DEC_BATCH, DEC_SEQ, nh, HEAD_DIM)
    sample_cache = (new(ka, N_HEADS_A), new(va, N_HEADS_A), new(kb, N_KV_B), new(vb, N_KV_B))
    return yp, ys, sample_cache


def _attn_layer(yp, ys, cache_ak, cache_av, cache_bk, cache_bv, w_in, sinks, w_out, g, b, ffn_args):
    w_in = w_in.astype(BF16)
    w_out = w_out.astype(BF16)
    yp, prompt_cache = _attn_prompt(yp, w_in, sinks, w_out, g, b)
    yp, ys, sample_cache = _attn_sample_path(ys, cache_ak, cache_av, cache_bk, cache_bv, w_in, sinks, w_out, g, b,
                                             yp, ffn_args)
    return yp, ys, prompt_cache, sample_cache


def _ssm_layer(yp, ys, state_re, state_im, w_in, lam_re, lam_im, log_dt, b_re, b_im, c_re, c_im, d_skip,
               w_glu, b_glu, w_out, g, b):
    w_in, w_glu, w_out = w_in.astype(BF16), w_glu.astype(BF16), w_out.astype(BF16)
    b_glu = b_glu.reshape(1, D_MODEL)
    mats = _ssm_prepare(lam_re, lam_im, log_dt, b_re, b_im, c_re, c_im, d_skip)
    yp, prompt_state = _ssm_prompt(yp, w_in, mats, w_glu, b_glu, w_out, g, b)
    ys, sample_state = _ssm_sample(ys, state_re, state_im, w_in, mats, w_glu, b_glu, w_out, g, b)
    return yp, ys, prompt_state, sample_state


def _ssm_prepare(lam_re, lam_im, log_dt, b_re, b_im, c_re, c_im, d_skip):
    ab_re, ab_im, bb_re, bb_im = _ssm_discretize(lam_re, lam_im, log_dt, b_re, b_im)
    bmat, cmat = _ssm_matrices(bb_re, bb_im, c_re, c_im)
    a_re = ab_re.reshape(N_LCHUNK, 1, ST_CHUNK)
    a_im = ab_im.reshape(N_LCHUNK, 1, ST_CHUNK)
    d3 = d_skip.astype(F32).reshape(N_LCHUNK, 1, LANES)
    prompt = (_chunk_weights(a_re, a_im, bmat, cmat), jnp.tile(d3, (1, 1, CHUNK)))
    sample = (bmat.astype(BF16), cmat.astype(BF16), a_re, a_im, d3)
    return prompt, sample


def _ssm_prompt(yp, w_in, mats, w_glu, b_glu, w_out, g, b):
    weights, d_tiled = mats[0]
    up = _ssm_in_prompt(yp, w_in)
    zero = jnp.zeros((BATCH, 1, N_STATE), F32)
    sp, pr, pi = _scan_prompt(up, weights, d_tiled, zero, zero)
    yp = _ssm_out(sp, yp.reshape(BATCH, N_SEG, SEG_LEN, D_MODEL),
                  pl.BlockSpec((None, N_LCHUNK, WIDE_ROW_TILE, LANES), lambda bb, t: (bb, 0, t, 0)),
                  pl.BlockSpec((None, N_SEG, SEG_TILE, D_MODEL), lambda bb, t: (bb, 0, t, 0)),
                  (BATCH, SEG_LEN // SEG_TILE), WIDE_ROW_TILE, False, w_glu, b_glu, w_out, g, b,
                  "ssm_out_prompt").reshape(BATCH * SEQ, D_MODEL)
    prompt_state = (pr.reshape(1, BATCH, N_SSM_GROUPS, SSM_STATE), pi.reshape(1, BATCH, N_SSM_GROUPS, SSM_STATE))
    return yp, prompt_state


def _ssm_sample(ys, state_re, state_im, w_in, mats, w_glu, b_glu, w_out, g, b):
    bmat, cmat, a_re, a_im, d3 = mats[1]
    us = _ssm_in_sample(ys, w_in)
    stored = lambda a: jnp.transpose(a, (1, 2, 0)).reshape(N_STATE, DEC_BATCH)
    logical = lambda a: jnp.transpose(a.reshape(N_SSM_GROUPS, SSM_STATE, DEC_BATCH), (2, 0, 1))[None]
    ss, sr, si = _scan_sample(us, bmat, cmat, a_re, a_im, d3, stored(state_re), stored(state_im))
    n = DEC_BATCH * DEC_SEQ
    ys = _ssm_out(ss, ys,
                  pl.BlockSpec((N_LCHUNK, n, LANES), lambda i: (0, 0, 0)),
                  pl.BlockSpec((n, D_MODEL), lambda i: (0, 0)),
                  (1,), n, True, w_glu, b_glu, w_out, g, b, "ssm_out_sample")
    return ys, (logical(sr), logical(si))


def kernel(x_prompt, x_sample, cache_a_k, cache_a_v, cache_b_k, cache_b_v, state_c_re, state_c_im, ln_g, ln_b, ffn_w_gate, ffn_w_up, ffn_w_down, attn_w_in, attn_sinks, attn_w_out, ssm_w_in, ssm_lambda_re, ssm_lambda_im, ssm_log_dt, ssm_b_re, ssm_b_im, ssm_c_re, ssm_c_im, ssm_d, ssm_w_glu, ssm_b_glu, ssm_w_out):
    yp = x_prompt.reshape(BATCH * SEQ, D_MODEL)
    ys = x_sample.reshape(DEC_BATCH * DEC_SEQ, D_MODEL)
    ln = lambda l, k: (ln_g[l, k].reshape(1, D_MODEL), ln_b[l, k].reshape(1, D_MODEL))

    wg, wu, wd = ffn_w_gate.astype(BF16), ffn_w_up.astype(BF16), ffn_w_down.astype(BF16)

    def ffn_pair(yp, ys, l, k, ln_idx):
        g, b = ln(l, ln_idx)
        return _ffn_pair(yp, ys, wg, wu, wd, g, b, l, k)

    yp, ys = ffn_pair(yp, ys, 0, 0, 0)
    yp, ys, p_cache, s_cache = _attn_layer(yp, ys, cache_a_k[0], cache_a_v[0], cache_b_k[0], cache_b_v[0],
                                           attn_w_in[0], attn_sinks[0], attn_w_out[0], *ln(0, 1),
                                           ffn_args=(wg, wu, wd, *ln(0, 2), 0, 1))
    ys = _ffn(ys, wg, wu, wd, *ln(0, 2), 0, 1)
    yp, ys = ffn_pair(yp, ys, 1, 0, 0)
    yp, ys, p_state, s_state = _ssm_layer(yp, ys, state_c_re[0], state_c_im[0], ssm_w_in[0], ssm_lambda_re[0],
                                          ssm_lambda_im[0], ssm_log_dt[0], ssm_b_re[0], ssm_b_im[0], ssm_c_re[0],
                                          ssm_c_im[0], ssm_d[0], ssm_w_glu[0], ssm_b_glu[0], ssm_w_out[0], *ln(1, 1))
    yp, ys = ffn_pair(yp, ys, 1, 1, 2)
    return (yp.reshape(BATCH, SEQ, D_MODEL), ys.reshape(DEC_BATCH, DEC_SEQ, D_MODEL),
            *p_cache, *p_state, *s_cache, *s_state)
```

```python
import functools
import math

import jax
import jax.numpy as jnp
from jax import lax
from jax.experimental import pallas as pl
from jax.experimental.pallas import tpu as pltpu

F32 = jnp.float32
BF16 = jnp.bfloat16

D_MODEL = 1024
BATCH = 4
SEQ = 4096
DEPTH = 2
DEC_BATCH = 128
DEC_SEQ = 8
PAST_LEN = 16384
HEAD_DIM = 64
N_HEADS_A = 8
DILATIONS = (1, 4, 16)
WIN_A = 2048
N_HEADS_B = 8
N_KV_B = 2
WIN_B = 128
ROPE_THETA = 10000.0
D_A = N_HEADS_A * HEAD_DIM
D_BQ = N_HEADS_B * HEAD_DIM
D_BKV = N_KV_B * HEAD_DIM
D_IN_ATTN = 3 * D_A + D_BQ + 2 * D_BKV
SSM_GROUP = 16
N_SSM_GROUPS = D_MODEL // SSM_GROUP
SSM_STATE = 64
N_STATE = N_SSM_GROUPS * SSM_STATE
D_FF = 2816
DN_ALPHA = (2 * DEPTH) ** 0.25
FFN_RES = 0.5
LN_EPS = 1e-5
ATTN_SCALE = HEAD_DIM ** -0.5
LOG2E = math.log2(math.e)
LSE_LANES = 16

LANES = 128
SUBLANES = 8
MXU_N = 256
VMEM_LIMIT = 56 * 1024 * 1024

ROW_TILE = 512
WIDE_ROW_TILE = 1024
FFN_ROW_TILE = 1024
FF_CHUNK = MXU_N
TQ = 128
BAND_SUB_TILES = 8
N_SEG = SUBLANES
SEG_LEN = SEQ // N_SEG
N_LCHUNK = D_MODEL // LANES
ST_CHUNK = N_STATE // N_LCHUNK
NK_PAD = WIN_A + LANES
NKB_PAD = 2 * WIN_B

NEG_INF = float("-inf")


def _params(n_axes, vmem=VMEM_LIMIT):
    return pltpu.CompilerParams(dimension_semantics=("arbitrary",) * n_axes, vmem_limit_bytes=vmem)


def _resident(shape):
    return pl.BlockSpec(shape, lambda *_: (0,) * len(shape), pipeline_mode=pl.Buffered(1))


def _layer_norm(x, g, b):
    mu = jnp.mean(x, -1, keepdims=True)
    xc = x - mu
    var = jnp.mean(xc * xc, -1, keepdims=True)
    return xc * lax.rsqrt(var + LN_EPS) * g + b


def _dot(a, b):
    return jnp.dot(a, b, preferred_element_type=F32)


def _dot_nt(a, b):
    return lax.dot_general(a, b, (((1,), (1,)), ((), ())), preferred_element_type=F32)


def _ffn_kernel(x_ref, wg_ref, wu_ref, wd_ref, g_ref, b_ref, o_ref, h_ref):
    x = x_ref[...]
    xb = x.astype(BF16)
    for c in range(D_FF // FF_CHUNK):
        sl = slice(c * FF_CHUNK, (c + 1) * FF_CHUNK)
        gate = _dot(xb, wg_ref[:, sl])
        up = _dot(xb, wu_ref[:, sl])
        h_ref[:, sl] = (gate * jax.nn.sigmoid(gate) * up).astype(BF16)
    y = DN_ALPHA * x + FFN_RES * _dot(h_ref[...], wd_ref[...])
    o_ref[...] = _layer_norm(y, g_ref[...], b_ref[...])


def _ffn(x, wg, wu, wd, g, b, layer=0, which=0):
    n = x.shape[0]
    tm = min(FFN_ROW_TILE, n)
    row = pl.BlockSpec((tm, D_MODEL), lambda i: (i, 0))
    if wg.ndim == 4:
        pick = lambda r, c: pl.BlockSpec((None, None, r, c), lambda i: (layer, which, 0, 0),
                                         pipeline_mode=pl.Buffered(1))
    else:
        pick = lambda r, c: _resident((r, c))
    return pl.pallas_call(
        _ffn_kernel,
        grid=(n // tm,),
        in_specs=[row, pick(D_MODEL, D_FF), pick(D_MODEL, D_FF), pick(D_FF, D_MODEL),
                  _resident((1, D_MODEL)), _resident((1, D_MODEL))],
        out_specs=row,
        out_shape=jax.ShapeDtypeStruct((n, D_MODEL), F32),
        scratch_shapes=[pltpu.VMEM((tm, D_FF), BF16)],
        compiler_params=_params(1),
        name="ffn",
    )(x, wg, wu, wd, g, b)


def _ffn_pair_kernel(n_first, xp_ref, xs_ref, wg_ref, wu_ref, wd_ref, g_ref, b_ref, op_ref, os_ref, h_ref):
    step = pl.program_id(0)
    weights = (wg_ref, wu_ref, wd_ref, g_ref, b_ref)
    pl.when(step < n_first)(functools.partial(_ffn_kernel, xp_ref, *weights, op_ref, h_ref))
    h_small = h_ref.at[pl.ds(0, xs_ref.shape[0])]
    pl.when(step >= n_first)(functools.partial(_ffn_kernel, xs_ref, *weights, os_ref, h_small))


def _ffn_pair(xp, xs, wg, wu, wd, g, b, layer, which):
    tm, tm2 = min(FFN_ROW_TILE, xp.shape[0]), ROW_TILE
    n_first, n_second = xp.shape[0] // tm, xs.shape[0] // tm2
    first = pl.BlockSpec((tm, D_MODEL), lambda i: (jnp.minimum(i, n_first - 1), 0))
    second = pl.BlockSpec((tm2, D_MODEL), lambda i: (jnp.maximum(i - n_first, 0), 0))
    pick = lambda r, c: pl.BlockSpec((None, None, r, c), lambda i: (layer, which, 0, 0), pipeline_mode=pl.Buffered(1))
    return pl.pallas_call(
        functools.partial(_ffn_pair_kernel, n_first),
        grid=(n_first + n_second,),
        in_specs=[first, second, pick(D_MODEL, D_FF), pick(D_MODEL, D_FF), pick(D_FF, D_MODEL),
                  _resident((1, D_MODEL)), _resident((1, D_MODEL))],
        out_specs=[first, second],
        out_shape=[jax.ShapeDtypeStruct(xp.shape, F32), jax.ShapeDtypeStruct(xs.shape, F32)],
        scratch_shapes=[pltpu.VMEM((tm, D_FF), BF16)],
        compiler_params=_params(1),
        name="ffn_pair",
    )(xp, xs, wg, wu, wd, g, b)


def _rope_tables(pos):
    half = HEAD_DIM // 2
    inv_freq = ROPE_THETA ** (-jnp.arange(half, dtype=F32) / half)
    ang = pos.astype(F32)[:, None] * inv_freq[None, :]
    cos, sin = jnp.cos(ang), jnp.sin(ang)
    cos_t = jnp.concatenate([cos, cos, cos, cos], -1)
    sin_t = jnp.concatenate([-sin, sin, -sin, sin], -1)
    return cos_t, sin_t


def _attn_proj_kernel(tiles_per_seq, x_ref, w_ref, cos_ref, sin_ref, qa_ref, ka_ref, va_ref, qb_ref, kb_ref, vb_ref,
                      *extra):
    xb = x_ref[...].astype(BF16)
    slab_ref = extra[-1] if extra else None
    dilated = extra[:6]
    n_chunks = D_A // LANES
    tm = x_ref.shape[0]

    def keep(tensor, c, val):
        if slab_ref is not None:
            slab_ref[tensor * n_chunks + c] = val
    cos = cos_ref[...]
    sin = sin_ref[...]
    lane = lax.broadcasted_iota(jnp.int32, cos.shape, 1)
    first_half = (lane & (HEAD_DIM // 2)) == 0

    def rope(z):
        rot = jnp.where(first_half, pltpu.roll(z, LANES - HEAD_DIM // 2, 1), pltpu.roll(z, HEAD_DIM // 2, 1))
        return z * cos + rot * sin

    def project(col0, ncols):
        return _dot(xb, w_ref[:, col0:col0 + ncols])

    def rope_chunks(z):
        return [rope(z[:, c * LANES:(c + 1) * LANES]) for c in range(z.shape[1] // LANES)]

    q_scale = ATTN_SCALE * LOG2E if extra else ATTN_SCALE
    col = 0
    for c, r in enumerate(rope_chunks(project(col, D_A))):
        r = r * q_scale
        qa_ref[:, c * LANES:(c + 1) * LANES] = r.astype(qa_ref.dtype)
        keep(0, c, r)
    col += D_A
    for c, r in enumerate(rope_chunks(project(col, D_A))):
        ka_ref[:, c * LANES:(c + 1) * LANES] = r.astype(ka_ref.dtype)
        keep(1, c, r)
    col += D_A
    z = project(col, D_A)
    va_ref[...] = z.astype(va_ref.dtype)
    for c in range(n_chunks):
        keep(2, c, z[:, c * LANES:(c + 1) * LANES])
    col += D_A
    if extra:
        kat_ref, vat_ref, kbt_ref, vbt_ref = extra[6:10]
        tile_in_seq = pl.program_id(0) % tiles_per_seq

        @pl.when(tile_in_seq >= tiles_per_seq - WIN_A // tm)
        def _():
            for c in range(n_chunks):
                kat_ref[c * LANES:(c + 1) * LANES, :] = slab_ref[n_chunks + c].T
                vat_ref[c * LANES:(c + 1) * LANES, :] = slab_ref[2 * n_chunks + c].T

        slab4_ref = extra[-2]
        d1, d2 = DILATIONS[1], DILATIONS[2] // DILATIONS[1]
        plane = tm // d1
        for tensor in range(3):
            out1_ref, out2_ref = dilated[tensor], dilated[3 + tensor]
            for c in range(n_chunks):
                idx = tensor * n_chunks + c
                lanes = slice(c * LANES, (c + 1) * LANES)
                for r in range(d1):
                    rows = slab_ref[idx, pl.ds(r, plane, stride=d1), :]
                    slab4_ref[idx, r * plane:(r + 1) * plane, :] = rows
                    out1_ref[r, :, lanes] = rows.astype(out1_ref.dtype)
                for r in range(d1):
                    for m in range(d2):
                        rows = slab4_ref[idx, pl.ds(r * plane + m, plane // d2, stride=d2), :]
                        out2_ref[r + d1 * m, :, lanes] = rows.astype(out2_ref.dtype)
    for c, r in enumerate(rope_chunks(project(col, D_BQ))):
        qb_ref[:, c * LANES:(c + 1) * LANES] = (r * q_scale).astype(qb_ref.dtype)
    col += D_BQ
    z = project(col, 2 * D_BKV)
    r = rope(z[:, :D_BKV])
    if extra:
        lo = _lane_lo(r.shape)
        for ref, val in ((kb_ref, r), (vb_ref, z[:, D_BKV:])):
            swapped = pltpu.roll(val, HEAD_DIM, 1)
            ref[:, 0:LANES] = jnp.where(lo, val, swapped).astype(ref.dtype)
            ref[:, LANES:] = jnp.where(lo, swapped, val).astype(ref.dtype)
    else:
        kb_ref[...] = r.astype(kb_ref.dtype)
        vb_ref[...] = z[:, D_BKV:].astype(vb_ref.dtype)
    if extra:
        @pl.when(tile_in_seq == tiles_per_seq - 1)
        def _():
            kbt_ref[...] = r[tm - WIN_B:, :].T
            vbt_ref[...] = z[tm - WIN_B:, D_BKV:].T


def _attn_proj(x, w, cos_t, sin_t, table_block, act_dtype, prompt_seq=None):
    n = x.shape[0]
    tm = min(ROW_TILE, n)

    def row(width):
        return pl.BlockSpec((tm, width), lambda i: (i, 0))

    tab = pl.BlockSpec((tm, LANES), lambda i: (table_block(i), 0))
    kv_b = D_BKV if prompt_seq is None else N_KV_B * LANES
    widths = (D_A, D_A, D_A, D_BQ, kv_b, kv_b)
    out_shape = [jax.ShapeDtypeStruct((n, wd), act_dtype) for wd in widths]
    out_specs = [row(wd) for wd in widths]
    scratch = []
    tps = None
    if prompt_seq is not None:
        bsz, seq = prompt_seq
        tps = seq // tm
        for dil in DILATIONS[1:]:
            out_shape += [jax.ShapeDtypeStruct((bsz, dil, seq // dil, D_A), BF16)] * 3
            out_specs += [pl.BlockSpec((None, dil, tm // dil, D_A), lambda i: (i // tps, 0, i % tps, 0))] * 3
        first_tail = tps - WIN_A // tm
        out_shape += [jax.ShapeDtypeStruct((bsz, D_A, WIN_A), F32)] * 2
        out_specs += [pl.BlockSpec((None, D_A, tm), lambda i: (i // tps, 0, jnp.maximum(i % tps - first_tail, 0)))] * 2
        out_shape += [jax.ShapeDtypeStruct((bsz, D_BKV, WIN_B), F32)] * 2
        out_specs += [pl.BlockSpec((None, D_BKV, WIN_B), lambda i: (i // tps, 0, 0))] * 2
        assert DILATIONS[2] == DILATIONS[1] ** 2
        scratch = [pltpu.VMEM((3 * D_A // LANES, tm, LANES), F32)] * 2
    return pl.pallas_call(
        functools.partial(_attn_proj_kernel, tps),
        grid=(n // tm,),
        in_specs=[row(D_MODEL), _resident((D_MODEL, D_IN_ATTN)), tab, tab],
        out_specs=out_specs,
        out_shape=out_shape,
        scratch_shapes=scratch,
        compiler_params=_params(1),
        name="attn_proj",
    )(x, w, cos_t, sin_t)


def _lane_lo(shape):
    return lax.broadcasted_iota(jnp.int32, shape, 1) < HEAD_DIM


def _half_masks_bf16():
    lo = jnp.where(_lane_lo((1, LANES)), 1.0, 0.0).astype(BF16)
    return lo, 1 - lo


def _band_masks(n_heads, t, sub):
    row = lax.broadcasted_iota(jnp.int32, (n_heads * TQ, TQ), 0) & (TQ - 1)
    col = lax.broadcasted_iota(jnp.int32, (n_heads * TQ, TQ), 1)
    shift = jnp.where(t > 0, 0, TQ) if sub == 0 else 0
    return col <= row, col >= row + shift


def _sub_tile_kv(sub, sl, kc_ref, kp_ref, vc_ref, vp_ref):
    cur = slice(sub * TQ, (sub + 1) * TQ)
    if sub == 0:
        return kc_ref[cur, sl], kp_ref[:, sl], vc_ref[cur, sl], vp_ref[:, sl]
    prev = slice((sub - 1) * TQ, sub * TQ)
    return kc_ref[cur, sl], kc_ref[prev, sl], vc_ref[cur, sl], vc_ref[prev, sl]


def _band_softmax(qs, kc, kp, vc, vp, mask_c, mask_p):
    s_c = jnp.where(mask_c, _dot_nt(qs, kc), NEG_INF)
    s_p = jnp.where(mask_p, _dot_nt(qs, kp), NEG_INF)
    m = jnp.max(jnp.maximum(s_c, s_p), -1, keepdims=True)
    p_c = jnp.exp2(s_c - m)
    p_p = jnp.exp2(s_p - m)
    den = jnp.sum(p_c + p_p, -1, keepdims=True)
    acc = _dot(p_c.astype(BF16), vc) + _dot(p_p.astype(BF16), vp)
    return acc * (1.0 / den), m, den


def _band_a_kernel(q_ref, kc_ref, kp_ref, vc_ref, vp_ref, o_ref, lse_ref):
    t = pl.program_id(2)
    lo = _lane_lo((TQ, LANES))
    lo_bf, hi_bf = _half_masks_bf16()
    lane_head = jnp.right_shift(lax.broadcasted_iota(jnp.int32, (TQ, LANES), 1), int(math.log2(LSE_LANES)))
    for plane in range(q_ref.shape[0]):
        refs = (kc_ref.at[plane], kp_ref.at[plane], vc_ref.at[plane], vp_ref.at[plane])
        for sub in range(q_ref.shape[1] // TQ):
            rows = slice(sub * TQ, (sub + 1) * TQ)
            mask_c, mask_p = _band_masks(2, t, sub)
            lse_tile = jnp.zeros((TQ, LANES), F32)
            for c in range(D_A // LANES):
                sl = slice(c * LANES, (c + 1) * LANES)
                kc, kp, vc, vp = _sub_tile_kv(sub, sl, *refs)
                q2 = q_ref[plane, rows, sl]
                qs = jnp.concatenate([q2 * lo_bf, q2 * hi_bf], axis=0)
                out, m, den = _band_softmax(qs, kc, kp, vc, vp, mask_c, mask_p)
                lse = m + jnp.log2(den)
                o_ref[plane, rows, sl] = jnp.where(lo, out[0:TQ], out[TQ:]).astype(o_ref.dtype)
                lse_tile = jnp.where(lane_head == 2 * c, lse[0:TQ], lse_tile)
                lse_tile = jnp.where(lane_head == 2 * c + 1, lse[TQ:], lse_tile)
            lse_ref[plane, rows, :] = lse_tile


def _band_a(q, k, v):
    bsz, dil, sub, _ = q.shape
    n_sub = min(BAND_SUB_TILES, sub // TQ)
    n_planes = BAND_SUB_TILES // n_sub
    assert dil % n_planes == 0 and sub % (n_sub * TQ) == 0
    cur = pl.BlockSpec((None, n_planes, n_sub * TQ, D_A), lambda b, r, t: (b, r, t, 0))
    prev = pl.BlockSpec((None, n_planes, TQ, D_A), lambda b, r, t: (b, r, jnp.maximum(n_sub * t - 1, 0), 0))
    lse = pl.BlockSpec((None, n_planes, n_sub * TQ, LANES), lambda b, r, t: (b, r, t, 0))
    return pl.pallas_call(
        _band_a_kernel,
        grid=(bsz, dil // n_planes, sub // (n_sub * TQ)),
        in_specs=[cur, cur, prev, cur, prev],
        out_specs=[cur, lse],
        out_shape=[jax.ShapeDtypeStruct((bsz, dil, sub, D_A), BF16),
                   jax.ShapeDtypeStruct((bsz, dil, sub, LANES), F32)],
        compiler_params=_params(3),
        name=f"band_a_d{dil}",
    )(q, k, k, v, v)


def _band_b_body(first_tile, sink_ref, q_ref, kc_ref, kp_ref, vc_ref, vp_ref, o_ref):
    group = N_HEADS_B // N_KV_B
    lo = _lane_lo((TQ, LANES))
    lo_bf, hi_bf = _half_masks_bf16()
    row = lax.broadcasted_iota(jnp.int32, (group * TQ, TQ), 0) & (TQ - 1)
    col = lax.broadcasted_iota(jnp.int32, (group * TQ, TQ), 1)
    in_cur = col <= row
    cur_bf = jnp.where(in_cur, 1.0, 0.0).astype(BF16)
    prev_bf = 1 - cur_bf
    for sub in range(q_ref.shape[0] // TQ):
        rows = slice(sub * TQ, (sub + 1) * TQ)
        for g in range(N_KV_B):
            sl = slice(g * LANES, (g + 1) * LANES)
            kc, kp, vc, vp = _sub_tile_kv(sub, sl, kc_ref, kp_ref, vc_ref, vp_ref)
            heads = range(g * group, (g + 1) * group)
            qs = jnp.concatenate(
                [q_ref[rows, (h // 2) * LANES:(h // 2 + 1) * LANES] * (lo_bf if h % 2 == 0 else hi_bf) for h in heads],
                axis=0)
            sink = jnp.concatenate([jnp.full((TQ, 1), sink_ref[h] * LOG2E, F32) for h in heads], axis=0)
            only_cur = first_tile and sub == 0
            s = jnp.where(in_cur, _dot_nt(qs, kc), NEG_INF if only_cur else _dot_nt(qs, kp))
            m = jnp.maximum(jnp.max(s, -1, keepdims=True), sink)
            p = jnp.exp2(s - m)
            den = jnp.sum(p, -1, keepdims=True) + jnp.exp2(sink - m)
            pb = p.astype(BF16)
            acc = _dot(pb, vc) if only_cur else _dot(pb * cur_bf, vc) + _dot(pb * prev_bf, vp)
            out = acc * (1.0 / den)
            for i in range(group // 2):
                c = g * (group // 2) + i
                even, odd = out[2 * i * TQ:(2 * i + 1) * TQ], out[(2 * i + 1) * TQ:(2 * i + 2) * TQ]
                o_ref[rows, c * LANES:(c + 1) * LANES] = jnp.where(lo, even, odd).astype(o_ref.dtype)


def _band_b_kernel(*refs):
    t = pl.program_id(1)
    pl.when(t == 0)(functools.partial(_band_b_body, True, *refs))
    pl.when(t > 0)(functools.partial(_band_b_body, False, *refs))


def _band_b(q, k, v, sinks):
    bsz, seq, _ = q.shape
    kv_lanes = N_KV_B * LANES
    n_sub = BAND_SUB_TILES
    qs = pl.BlockSpec((None, n_sub * TQ, D_BQ), lambda b, t: (b, t, 0))
    cur = pl.BlockSpec((None, n_sub * TQ, kv_lanes), lambda b, t: (b, t, 0))
    prev = pl.BlockSpec((None, TQ, kv_lanes), lambda b, t: (b, jnp.maximum(n_sub * t - 1, 0), 0))
    o = pl.pallas_call(
        _band_b_kernel,
        grid=(bsz, seq // (n_sub * TQ)),
        in_specs=[pl.BlockSpec(memory_space=pltpu.SMEM), qs, cur, prev, cur, prev],
        out_specs=qs,
        out_shape=jax.ShapeDtypeStruct((bsz, seq, D_BQ), BF16),
        compiler_params=_params(2),
        name="band_b",
    )(sinks, q, k, k, v, v)
    return o.reshape(bsz * seq, D_BQ)


def _pattern_count(dist):
    cnt = jnp.zeros(dist.shape, F32)
    for dil in DILATIONS:
        cnt = cnt + ((dist >= 0) & (dist <= 128 * dil) & (dist % dil == 0)).astype(F32)
    return cnt


def _sample_tables():
    i = jnp.arange(DEC_SEQ)
    cnt_c = _pattern_count(WIN_A + i[:, None] - jnp.arange(WIN_A)[None, :])
    j = jnp.arange(LANES)
    cnt_n = jnp.where(j[None, :] < DEC_SEQ, _pattern_count(i[:, None] - j[None, :]), 0.0)
    cnt_n = jnp.tile(cnt_n, (N_HEADS_A, 1))
    jb = jnp.arange(NKB_PAD)[None, :]
    dist_b = WIN_B + i[:, None] - jb
    ok_b = (dist_b >= 0) & (dist_b < WIN_B) & (jb < WIN_B + DEC_SEQ)
    mask_b = jnp.tile(ok_b.astype(F32), (N_HEADS_B, 1))
    return cnt_c, cnt_n, mask_b


def _sample_attend(q, kan, van, kt_ref, vt_ref, cnt_c, cnt_n, qb, kbn, vbn, kbc, vbc, mask_b, sink_col, kb_s, vb_s):
    rows = N_HEADS_A * DEC_SEQ
    q_rep = jnp.concatenate([q] * N_HEADS_A, axis=0)
    row_head = jnp.right_shift(lax.broadcasted_iota(jnp.int32, (rows, D_A), 0), int(math.log2(DEC_SEQ)))
    lane_head = jnp.right_shift(lax.broadcasted_iota(jnp.int32, (rows, D_A), 1), int(math.log2(HEAD_DIM)))
    own = row_head == lane_head
    q_bd = jnp.where(own, q_rep, 0.0).astype(BF16)
    pad = jnp.zeros((LANES - DEC_SEQ, D_A), F32)
    kn = jnp.concatenate([kan, pad], 0).astype(BF16)
    vn = jnp.concatenate([van, pad], 0).astype(BF16)
    s_new = jnp.where(cnt_n > 0.0, _dot_nt(q_bd, kn), NEG_INF)
    outs, p_new = [], []
    for h in range(N_HEADS_A):
        head_rows = slice(h * DEC_SEQ, (h + 1) * DEC_SEQ)
        q_h = q[:, h * HEAD_DIM:(h + 1) * HEAD_DIM].astype(BF16)
        s_c = jnp.where(cnt_c > 0.0, _dot(q_h, kt_ref[h].astype(BF16)), NEG_INF)
        s_n = s_new[head_rows]
        m = jnp.maximum(jnp.max(s_c, -1, keepdims=True), jnp.max(s_n, -1, keepdims=True))
        p_c = jnp.exp(s_c - m) * cnt_c
        p_n = jnp.exp(s_n - m) * cnt_n[head_rows]
        inv = 1.0 / (jnp.sum(p_c, -1, keepdims=True) + jnp.sum(p_n, -1, keepdims=True))
        outs.append(_dot_nt(p_c.astype(BF16), vt_ref[h].astype(BF16)) * inv)
        p_new.append(p_n * inv)
    out_n = jnp.where(own, _dot(jnp.concatenate(p_new, axis=0).astype(BF16), vn), 0.0)
    oa = jnp.concatenate(outs, axis=1)
    for h in range(N_HEADS_A):
        oa = oa + out_n[h * DEC_SEQ:(h + 1) * DEC_SEQ]

    n_pad_b = NKB_PAD - WIN_B - DEC_SEQ
    pad_b = jnp.zeros((n_pad_b, D_BKV), F32)
    kb_s[...] = jnp.concatenate([kbc, kbn, pad_b], 0).astype(BF16)
    vb_s[...] = jnp.concatenate([vbc, vbn, pad_b], 0).astype(BF16)
    lo8 = _lane_lo((DEC_SEQ, LANES))
    group = N_HEADS_B // N_KV_B
    pieces = []
    for h in range(N_HEADS_B):
        chunk = qb[:, (h // 2) * LANES:(h // 2 + 1) * LANES]
        g = h // group
        if h % 2 != g:
            chunk = pltpu.roll(chunk, HEAD_DIM, 1)
        pieces.append(jnp.where(lo8 if g == 0 else jnp.logical_not(lo8), chunk, 0.0))
    qb_bd = jnp.concatenate(pieces, axis=0).astype(BF16)
    sb = jnp.where(mask_b > 0.0, _dot_nt(qb_bd, kb_s[...]), NEG_INF)
    sink = sink_col[:, 0:1]
    mb = jnp.maximum(jnp.max(sb, -1, keepdims=True), sink)
    pb = jnp.exp(sb - mb) * mask_b
    den_b = jnp.sum(pb, -1, keepdims=True) + jnp.exp(sink - mb)
    ob_full = _dot(pb.astype(BF16), vb_s[...]) * (1.0 / den_b)
    ob = []
    for c in range(D_BQ // LANES):
        halves = []
        for half in range(2):
            h = 2 * c + half
            piece = ob_full[h * DEC_SEQ:(h + 1) * DEC_SEQ]
            if half != h // group:
                piece = pltpu.roll(piece, HEAD_DIM, 1)
            halves.append(piece)
        ob.append(jnp.where(lo8, halves[0], halves[1]))
    return jnp.concatenate([oa] + ob, axis=1)


FUSED_ROW_TILE = 512
SEQ_PER_STEP = DEC_BATCH // (BATCH * SEQ // FUSED_ROW_TILE)


def _kv_copies(kt_hbm, vt_hbm, kbuf, vbuf, sems, seq, slot):
    return (pltpu.make_async_copy(kt_hbm.at[seq], kbuf.at[slot], sems.at[0, slot]),
            pltpu.make_async_copy(vt_hbm.at[seq], vbuf.at[slot], sems.at[1, slot]))


def _ffn_attn_kernel(x_ref, wg_ref, wu_ref, wd_ref, g_ref, b_ref,
                     qa_ref, kan_ref, van_ref, kt_hbm, vt_hbm, cnt_c_ref, cnt_n_ref,
                     qb_ref, kbn_ref, vbn_ref, kbc_ref, vbc_ref, maskb_ref, sinkcol_ref,
                     y_ref, o_ref, h_ref, kbuf, vbuf, sems, kb_s, vb_s):
    step = pl.program_id(0)
    n_steps = pl.num_programs(0)
    copies = functools.partial(_kv_copies, kt_hbm, vt_hbm, kbuf, vbuf, sems)

    @pl.when(step == 0)
    def _():
        for cp in copies(0, 0):
            cp.start()

    x = x_ref[...]
    xb = x.astype(BF16)
    n_ff = D_FF // FF_CHUNK
    per_seq = -(-n_ff // SEQ_PER_STEP)
    cnt_c, cnt_n, mask_b, sink_col = cnt_c_ref[...], cnt_n_ref[...], maskb_ref[...], sinkcol_ref[...]
    for s in range(SEQ_PER_STEP):
        slot = s % 2
        seq = step * SEQ_PER_STEP + s
        for cp in copies(seq, slot):
            cp.wait()
        if s + 1 < SEQ_PER_STEP:
            for cp in copies(seq + 1, 1 - slot):
                cp.start()
        else:
            @pl.when(step + 1 < n_steps)
            def _():
                for cp in copies(seq + 1, 1 - slot):
                    cp.start()
        rows = slice(s * DEC_SEQ, (s + 1) * DEC_SEQ)
        kbc = jnp.concatenate([kbc_ref[s, g].T for g in range(N_KV_B)], axis=1)
        vbc = jnp.concatenate([vbc_ref[s, g].T for g in range(N_KV_B)], axis=1)
        o_ref[rows, :] = _sample_attend(
            qa_ref[rows, :], kan_ref[rows, :], van_ref[rows, :], kbuf.at[slot], vbuf.at[slot], cnt_c, cnt_n,
            qb_ref[rows, :], kbn_ref[rows, :], vbn_ref[rows, :], kbc, vbc, mask_b, sink_col, kb_s, vb_s)
        for c in range(s * per_seq, min((s + 1) * per_seq, n_ff)):
            sl = slice(c * FF_CHUNK, (c + 1) * FF_CHUNK)
            gate = _dot(xb, wg_ref[:, sl])
            up = _dot(xb, wu_ref[:, sl])
            h_ref[:, sl] = (gate * jax.nn.sigmoid(gate) * up).astype(BF16)
    y = DN_ALPHA * x + FFN_RES * _dot(h_ref[...], wd_ref[...])
    y_ref[...] = _layer_norm(y, g_ref[...], b_ref[...])


def _ffn_with_sample_attention(x, wg, wu, wd, g, b, layer, which,
                               qa, kan, van, cache_ak, cache_av, qb, kbn, vbn, cache_bk, cache_bv, sinks):
    n = x.shape[0]
    tm = FUSED_ROW_TILE
    assert n // tm * SEQ_PER_STEP == DEC_BATCH and SEQ_PER_STEP % 2 == 0
    cnt_c, cnt_n, mask_b = _sample_tables()
    sink_col = jnp.broadcast_to(jnp.repeat(sinks.astype(F32), DEC_SEQ)[:, None], (N_HEADS_B * DEC_SEQ, LANES))
    row = pl.BlockSpec((tm, D_MODEL), lambda i: (i, 0))
    pick = lambda r, c: pl.BlockSpec((None, None, r, c), lambda i: (layer, which, 0, 0), pipeline_mode=pl.Buffered(1))
    new = lambda width: pl.BlockSpec((SEQ_PER_STEP * DEC_SEQ, width), lambda i: (i, 0))
    cache_b = pl.BlockSpec((SEQ_PER_STEP, N_KV_B, HEAD_DIM, WIN_B), lambda i: (i, 0, 0, 0))
    hbm = pl.BlockSpec(memory_space=pl.ANY)
    rows = N_HEADS_A * DEC_SEQ
    kv_slot = (2, N_HEADS_A, HEAD_DIM, WIN_A)
    return pl.pallas_call(
        _ffn_attn_kernel,
        grid=(n // tm,),
        in_specs=[row, pick(D_MODEL, D_FF), pick(D_MODEL, D_FF), pick(D_FF, D_MODEL),
                  _resident((1, D_MODEL)), _resident((1, D_MODEL)),
                  new(D_A), new(D_A), new(D_A), hbm, hbm, _resident(cnt_c.shape), _resident(cnt_n.shape),
                  new(D_BQ), new(D_BKV), new(D_BKV), cache_b, cache_b,
                  _resident((rows, NKB_PAD)), _resident((rows, LANES))],
        out_specs=[row, new(D_A + D_BQ)],
        out_shape=[jax.ShapeDtypeStruct((n, D_MODEL), F32),
                   jax.ShapeDtypeStruct((DEC_BATCH * DEC_SEQ, D_A + D_BQ), F32)],
        scratch_shapes=[pltpu.VMEM((tm, D_FF), BF16), pltpu.VMEM(kv_slot, F32), pltpu.VMEM(kv_slot, F32),
                        pltpu.SemaphoreType.DMA((2, 2)),
                        pltpu.VMEM((NKB_PAD, D_BKV), BF16), pltpu.VMEM((NKB_PAD, D_BKV), BF16)],
        compiler_params=_params(1, vmem=60 * 1024 * 1024),
        name="ffn_attn_sample",
    )(x, wg, wu, wd, g, b, qa, kan, van, cache_ak, cache_av, cnt_c, cnt_n, qb, kbn, vbn, cache_bk, cache_bv,
      mask_b, sink_col)


def _attn_out_prompt_kernel(o1_ref, l1_ref, o4_ref, l4_ref, o16_ref, l16_ref, ob_ref, y_ref, w_ref, g_ref, b_ref,
                            out_ref, slab_ref, oa_ref):
    tm = y_ref.shape[0]
    n_chunks = D_A // LANES
    slabs = {}
    base = 0
    for name, dil, src, width in (("o4", 4, o4_ref, n_chunks), ("l4", 4, l4_ref, 1),
                                  ("o16", 16, o16_ref, n_chunks), ("l16", 16, l16_ref, 1)):
        slabs[name] = base
        for r in range(dil):
            for c in range(width):
                piece = src[r, :, c * LANES:(c + 1) * LANES]
                slab_ref[base + c, pl.ds(r, tm // dil, stride=dil), :] = piece.astype(F32)
        base += width
    l1, l4, l16 = l1_ref[...], slab_ref[slabs["l4"]], slab_ref[slabs["l16"]]
    m = jnp.maximum(jnp.maximum(l1, l4), l16)
    e1, e4, e16 = jnp.exp2(l1 - m), jnp.exp2(l4 - m), jnp.exp2(l16 - m)
    inv = 1.0 / (e1 + e4 + e16)
    row = lax.broadcasted_iota(jnp.int32, (LANES, D_A), 0)
    head_of_col = jnp.right_shift(lax.broadcasted_iota(jnp.int32, (LANES, D_A), 1), int(math.log2(HEAD_DIM)))
    spread = jnp.where(row == head_of_col * LSE_LANES, 1.0, 0.0).astype(BF16)

    def per_head_lanes(w):
        hi = w.astype(BF16)
        lo = (w - hi.astype(F32)).astype(BF16)
        return _dot(hi, spread) + _dot(lo, spread)

    w1, w4, w16 = per_head_lanes(e1 * inv), per_head_lanes(e4 * inv), per_head_lanes(e16 * inv)
    for c in range(n_chunks):
        sl = slice(c * LANES, (c + 1) * LANES)
        oa = (w1[:, sl] * o1_ref[:, sl].astype(F32) + w4[:, sl] * slab_ref[slabs["o4"] + c]
              + w16[:, sl] * slab_ref[slabs["o16"] + c])
        oa_ref[:, sl] = oa.astype(BF16)
    mix = _dot(oa_ref[...], w_ref[0:D_A, :]) + _dot(ob_ref[...], w_ref[D_A:, :])
    out_ref[...] = _layer_norm(DN_ALPHA * y_ref[...] + mix, g_ref[...], b_ref[...])


def _attn_out_prompt(pats, ob, y, w, g, b, seq):
    n = y.shape[0]
    tm = WIDE_ROW_TILE
    tps = seq // tm
    half = pl.BlockSpec((tm, D_A), lambda i: (i, 0))
    full = pl.BlockSpec((tm, D_MODEL), lambda i: (i, 0))
    lse1 = pl.BlockSpec((tm, LANES), lambda i: (i, 0))
    planes = lambda dil, width: pl.BlockSpec((None, dil, tm // dil, width), lambda i: (i // tps, 0, i % tps, 0))
    (o1, l1), (o4, l4), (o16, l16) = pats
    return pl.pallas_call(
        _attn_out_prompt_kernel,
        grid=(n // tm,),
        in_specs=[half, lse1, planes(4, D_A), planes(4, LANES), planes(16, D_A), planes(16, LANES), half, full,
                  _resident((D_MODEL, D_MODEL)), _resident((1, D_MODEL)), _resident((1, D_MODEL))],
        out_specs=full,
        out_shape=jax.ShapeDtypeStruct((n, D_MODEL), F32),
        scratch_shapes=[pltpu.VMEM((2 * (D_A // LANES + 1), tm, LANES), F32), pltpu.VMEM((tm, D_A), BF16)],
        compiler_params=_params(1),
        name="attn_out_prompt",
    )(o1, l1, o4, l4, o16, l16, ob, y, w, g, b)


def _mix_out_kernel(o_ref, y_ref, w_ref, g_ref, b_ref, out_ref):
    mix = _dot(o_ref[...].astype(BF16), w_ref[...])
    out_ref[...] = _layer_norm(DN_ALPHA * y_ref[...] + mix, g_ref[...], b_ref[...])


def _attn_out_sample(o, y, w, g, b):
    n = y.shape[0]
    tm = min(ROW_TILE, n)
    full = pl.BlockSpec((tm, D_MODEL), lambda i: (i, 0))
    return pl.pallas_call(
        _mix_out_kernel,
        grid=(n // tm,),
        in_specs=[full, full, _resident((D_MODEL, D_MODEL)), _resident((1, D_MODEL)), _resident((1, D_MODEL))],
        out_specs=full,
        out_shape=jax.ShapeDtypeStruct((n, D_MODEL), F32),
        compiler_params=_params(1),
        name="attn_out_sample",
    )(o, y, w, g, b)


def _ssm_discretize(lam_re, lam_im, log_dt, b_re, b_im):
    dt = jnp.exp(log_dt.astype(F32))[:, None]
    lr, li = lam_re.astype(F32), lam_im.astype(F32)
    mag = jnp.exp(lr * dt)
    ab_re, ab_im = mag * jnp.cos(li * dt), mag * jnp.sin(li * dt)
    nr, ni = ab_re - 1.0, ab_im
    den = lr * lr + li * li
    fr, fi = (nr * lr + ni * li) / den, (ni * lr - nr * li) / den
    bb_re = fr[..., None] * b_re - fi[..., None] * b_im
    bb_im = fr[..., None] * b_im + fi[..., None] * b_re
    return ab_re, ab_im, bb_re, bb_im


def _ssm_matrices(bb_re, bb_im, c_re, c_im):
    gpc = LANES // SSM_GROUP
    eye = jnp.eye(gpc, dtype=F32)

    def in_blocks(bb):
        a = bb.reshape(N_LCHUNK, gpc, SSM_STATE, SSM_GROUP)
        return jnp.einsum("jgpn,gh->jgnhp", a, eye).reshape(N_LCHUNK, LANES, ST_CHUNK)

    def out_blocks(cc):
        a = cc.reshape(N_LCHUNK, gpc, SSM_GROUP, SSM_STATE)
        return jnp.einsum("jgnp,gh->jgphn", a, eye).reshape(N_LCHUNK, ST_CHUNK, LANES)

    bmat = jnp.concatenate([in_blocks(bb_re), in_blocks(bb_im)], -1)
    cmat = jnp.concatenate([out_blocks(c_re), -out_blocks(c_im)], 1)
    return bmat, cmat


CHUNK = 8


def _dot3(a, b):
    a_hi, b_hi = a.astype(BF16), b.astype(BF16)
    a_lo, b_lo = (a - a_hi.astype(F32)).astype(BF16), (b - b_hi.astype(F32)).astype(BF16)
    return _dot(a_hi, b_hi) + _dot(a_hi, b_lo) + _dot(a_lo, b_hi)


def _chunk_weights_kernel(ar_ref, ai_ref, acr_ref, aci_ref, bmat_ref, cmat_ref,
                          we_ref, ws_ref, wi_ref, a8r_ref, a8i_ref):
    def powers(r, i, n):
        out = [(jnp.ones_like(r), jnp.zeros_like(r))]
        for _ in range(n):
            out.append(_cmul(out[-1][0], out[-1][1], r, i))
        return out

    row_pow = powers(ar_ref[...], ai_ref[...], CHUNK)
    col_pow = powers(acr_ref[...], aci_ref[...], CHUNK)
    b_re, b_im = bmat_ref[:, 0:ST_CHUNK], bmat_ref[:, ST_CHUNK:]
    c_re, c_im = cmat_ref[0:ST_CHUNK, :], -cmat_ref[ST_CHUNK:, :]

    def scaled_b(power):
        pr, pi = row_pow[power]
        return jnp.concatenate([b_re * pr - b_im * pi, b_re * pi + b_im * pr], axis=1)

    scaled = [scaled_b(tau) for tau in range(CHUNK)]
    for tau in range(CHUNK):
        we_ref[(CHUNK - 1 - tau) * LANES:(CHUNK - tau) * LANES, :] = scaled[tau].astype(BF16)
    all_taps = _dot3(jnp.concatenate(scaled, axis=0), cmat_ref[...]).astype(BF16)
    taps = [all_taps[tau * LANES:(tau + 1) * LANES] for tau in range(CHUNK)]
    for k in range(CHUNK):
        cols = slice(k * LANES, (k + 1) * LANES)
        qr, qi = col_pow[k + 1]
        ws_ref[0:ST_CHUNK, cols] = (c_re * qr - c_im * qi).astype(BF16)
        ws_ref[ST_CHUNK:, cols] = (-(c_re * qi + c_im * qr)).astype(BF16)
    zero = jnp.zeros((LANES, LANES), BF16)
    for k_in in range(CHUNK):
        for k_out in range(CHUNK):
            wi_ref[k_in * LANES:(k_in + 1) * LANES, k_out * LANES:(k_out + 1) * LANES] = (
                taps[k_out - k_in] if k_out >= k_in else zero)
    a8r_ref[...], a8i_ref[...] = row_pow[CHUNK]


def _chunk_weights(a_re, a_im, bmat, cmat):
    cols = lambda a: jnp.broadcast_to(a.reshape(N_LCHUNK, ST_CHUNK, 1), (N_LCHUNK, ST_CHUNK, LANES))
    per_j = lambda r, c: pl.BlockSpec((None, r, c), lambda j: (j, 0, 0))
    wide = 2 * ST_CHUNK
    return pl.pallas_call(
        _chunk_weights_kernel,
        grid=(N_LCHUNK,),
        in_specs=[per_j(1, ST_CHUNK), per_j(1, ST_CHUNK), per_j(ST_CHUNK, LANES), per_j(ST_CHUNK, LANES),
                  per_j(LANES, wide), per_j(wide, LANES)],
        out_specs=[per_j(CHUNK * LANES, wide), per_j(wide, CHUNK * LANES), per_j(CHUNK * LANES, CHUNK * LANES),
                   per_j(1, ST_CHUNK), per_j(1, ST_CHUNK)],
        out_shape=[jax.ShapeDtypeStruct((N_LCHUNK, CHUNK * LANES, wide), BF16),
                   jax.ShapeDtypeStruct((N_LCHUNK, wide, CHUNK * LANES), BF16),
                   jax.ShapeDtypeStruct((N_LCHUNK, CHUNK * LANES, CHUNK * LANES), BF16),
                   jax.ShapeDtypeStruct((N_LCHUNK, 1, ST_CHUNK), F32),
                   jax.ShapeDtypeStruct((N_LCHUNK, 1, ST_CHUNK), F32)],
        compiler_params=_params(1),
        name="ssm_chunk_weights",
    )(a_re, a_im, cols(a_re), cols(a_im), bmat, cmat)


SEG_TILE = WIDE_ROW_TILE // N_SEG


def _ssm_in_prompt_kernel(x_ref, w_ref, o_ref):
    x = x_ref[...].reshape(N_SEG * SEG_TILE, D_MODEL)
    u = _dot(x.astype(BF16), w_ref[...])
    for s in range(N_SEG):
        for c in range(N_LCHUNK):
            o_ref[c, pl.ds(s, SEG_TILE, stride=N_SEG), :] = u[s * SEG_TILE:(s + 1) * SEG_TILE, c * LANES:(c + 1) * LANES]


def _ssm_in_prompt(y, w):
    return pl.pallas_call(
        _ssm_in_prompt_kernel,
        grid=(BATCH, SEG_LEN // SEG_TILE),
        in_specs=[pl.BlockSpec((None, N_SEG, SEG_TILE, D_MODEL), lambda b, t: (b, 0, t, 0)),
                  _resident((D_MODEL, D_MODEL))],
        out_specs=pl.BlockSpec((None, N_LCHUNK, WIDE_ROW_TILE, LANES), lambda b, t: (b, 0, t, 0)),
        out_shape=jax.ShapeDtypeStruct((BATCH, N_LCHUNK, SEQ, LANES), F32),
        compiler_params=_params(2),
        name="ssm_in_prompt",
    )(y.reshape(BATCH, N_SEG, SEG_LEN, D_MODEL), w)


def _ssm_in_sample_kernel(x_ref, w_ref, o_ref, slab_ref):
    u = _dot(x_ref[...].astype(BF16), w_ref[...])
    for c in range(N_LCHUNK):
        slab_ref[c] = u[:, c * LANES:(c + 1) * LANES]
    for l in range(DEC_SEQ):
        for c in range(N_LCHUNK):
            o_ref[c, l * DEC_BATCH:(l + 1) * DEC_BATCH, :] = slab_ref[c, pl.ds(l, DEC_BATCH, stride=DEC_SEQ), :]


def _ssm_in_sample(y, w):
    n = DEC_BATCH * DEC_SEQ
    return pl.pallas_call(
        _ssm_in_sample_kernel,
        grid=(1,),
        in_specs=[_resident((n, D_MODEL)), _resident((D_MODEL, D_MODEL))],
        out_specs=pl.BlockSpec((N_LCHUNK, n, LANES), lambda i: (0, 0, 0)),
        out_shape=jax.ShapeDtypeStruct((N_LCHUNK, n, LANES), F32),
        scratch_shapes=[pltpu.VMEM((N_LCHUNK, n, LANES), F32)],
        compiler_params=_params(1),
        name="ssm_in_sample",
    )(y, w)


def _cmul(ar, ai, br, bi):
    return ar * br - ai * bi, ar * bi + ai * br


def _scan_prompt_kernel(u_ref, we_ref, ws_ref, wi_ref, a8r_ref, a8i_ref, d_ref, h0r_ref, h0i_ref,
                        y_ref, hnr_ref, hni_ref, e_s, hs_s):
    n_chunks = u_ref.shape[0]
    rows = n_chunks * N_SEG
    u_flat = jnp.concatenate([u_ref[:, k].reshape(rows, LANES) for k in range(CHUNK)], axis=1)
    ub = u_flat.astype(BF16)
    e_s[...] = _dot(ub, we_ref[...])
    a8r1, a8i1 = a8r_ref[...], a8i_ref[...]
    a8r = jnp.broadcast_to(a8r1, (N_SEG, ST_CHUNK))
    a8i = jnp.broadcast_to(a8i1, (N_SEG, ST_CHUNK))

    def advance(row, hr, hi):
        er = e_s[pl.ds(row, N_SEG), 0:ST_CHUNK]
        ei = e_s[pl.ds(row, N_SEG), ST_CHUNK:2 * ST_CHUNK]
        return a8r * hr - a8i * hi + er, a8r * hi + a8i * hr + ei

    def pass1(c, carry):
        return advance(pl.multiple_of(c * N_SEG, N_SEG), *carry)

    zero = jnp.zeros((N_SEG, ST_CHUNK), F32)
    er, ei = lax.fori_loop(0, n_chunks, pass1, (zero, zero), unroll=8)

    pr, pi = a8r1, a8i1
    for _ in range(int(math.log2(n_chunks))):
        pr, pi = _cmul(pr, pi, pr, pi)
    hr, hi = h0r_ref[...], h0i_ref[...]
    starts_r, starts_i = [], []
    for s in range(N_SEG):
        starts_r.append(hr)
        starts_i.append(hi)
        gr, gi = _cmul(pr, pi, hr, hi)
        hr, hi = gr + er[s:s + 1], gi + ei[s:s + 1]
    hnr_ref[...] = hr
    hni_ref[...] = hi
    init = (jnp.concatenate(starts_r, 0), jnp.concatenate(starts_i, 0))

    def pass2(c, carry):
        row = pl.multiple_of(c * N_SEG, N_SEG)
        hs_s[pl.ds(row, N_SEG), 0:ST_CHUNK] = carry[0]
        hs_s[pl.ds(row, N_SEG), ST_CHUNK:2 * ST_CHUNK] = carry[1]
        return advance(row, *carry)

    lax.fori_loop(0, n_chunks, pass2, init, unroll=8)
    y = _dot(hs_s[...].astype(BF16), ws_ref[...]) + _dot(ub, wi_ref[...]) + d_ref[...] * u_flat
    for k in range(CHUNK):
        y_ref[:, k] = y[:, k * LANES:(k + 1) * LANES].reshape(n_chunks, N_SEG, LANES)


def _scan_prompt(u, weights, d_skip, h0r, h0i):
    bsz, _, seq, _ = u.shape
    n_chunks = seq // (CHUNK * N_SEG)
    rows = n_chunks * N_SEG
    wide = 2 * ST_CHUNK
    split = lambda a: a.reshape(bsz, N_LCHUNK, n_chunks, CHUNK, N_SEG, LANES)
    chunk = pl.BlockSpec((None, None, n_chunks, CHUNK, N_SEG, LANES), lambda j, b: (b, j, 0, 0, 0, 0))
    per_j = lambda r, c: pl.BlockSpec((None, r, c), lambda j, b: (j, 0, 0))
    state = pl.BlockSpec((None, 1, ST_CHUNK), lambda j, b: (b, 0, j))
    y, hr, hi = pl.pallas_call(
        _scan_prompt_kernel,
        grid=(N_LCHUNK, bsz),
        in_specs=[chunk, per_j(CHUNK * LANES, wide), per_j(wide, CHUNK * LANES), per_j(CHUNK * LANES, CHUNK * LANES),
                  per_j(1, ST_CHUNK), per_j(1, ST_CHUNK), per_j(1, CHUNK * LANES), state, state],
        out_specs=[chunk, state, state],
        out_shape=[jax.ShapeDtypeStruct((bsz, N_LCHUNK, n_chunks, CHUNK, N_SEG, LANES), F32),
                   jax.ShapeDtypeStruct((bsz, 1, N_STATE), F32), jax.ShapeDtypeStruct((bsz, 1, N_STATE), F32)],
        scratch_shapes=[pltpu.VMEM((rows, wide), F32), pltpu.VMEM((rows, wide), F32)],
        compiler_params=_params(2),
        name="ssm_scan_prompt",
    )(split(u), *weights, d_skip, h0r, h0i)
    return y.reshape(u.shape), hr, hi


def _scan_sample_kernel(u_ref, bmat_ref, cmat_ref, are_ref, aim_ref, d_ref, h0r_ref, h0i_ref,
                        y_ref, hnr_ref, hni_ref, h_s):
    a_re, a_im = are_ref[...], aim_ref[...]
    hr, hi = h0r_ref[...].T, h0i_ref[...].T
    u = u_ref[...]
    bu = _dot(u.astype(BF16), bmat_ref[...])
    for l in range(DEC_SEQ):
        rows = slice(l * DEC_BATCH, (l + 1) * DEC_BATCH)
        gr, gi = _cmul(a_re, a_im, hr, hi)
        hr, hi = gr + bu[rows, :ST_CHUNK], gi + bu[rows, ST_CHUNK:]
        h_s[rows, 0:ST_CHUNK] = hr.astype(BF16)
        h_s[rows, ST_CHUNK:] = hi.astype(BF16)
    y_ref[...] = _dot(h_s[...], cmat_ref[...]) + d_ref[...] * u
    hnr_ref[...] = hr.T
    hni_ref[...] = hi.T


def _scan_sample(u, bmat, cmat, a_re, a_im, d_skip, h0r, h0i):
    n = DEC_SEQ * DEC_BATCH
    chunk = pl.BlockSpec((None, n, LANES), lambda j: (j, 0, 0))
    per_j = lambda r, c: pl.BlockSpec((None, r, c), lambda j: (j, 0, 0))
    state = pl.BlockSpec((ST_CHUNK, DEC_BATCH), lambda j: (j, 0))
    return pl.pallas_call(
        _scan_sample_kernel,
        grid=(N_LCHUNK,),
        in_specs=[chunk, per_j(LANES, 2 * ST_CHUNK), per_j(2 * ST_CHUNK, LANES), per_j(1, ST_CHUNK),
                  per_j(1, ST_CHUNK), per_j(1, LANES), state, state],
        out_specs=[chunk, state, state],
        out_shape=[jax.ShapeDtypeStruct((N_LCHUNK, n, LANES), F32),
                   jax.ShapeDtypeStruct((N_STATE, DEC_BATCH), F32), jax.ShapeDtypeStruct((N_STATE, DEC_BATCH), F32)],
        scratch_shapes=[pltpu.VMEM((n, 2 * ST_CHUNK), BF16)],
        compiler_params=_params(1),
        name="ssm_scan_sample",
    )(u, bmat, cmat, a_re, a_im, d_skip, h0r, h0i)


def _ssm_out_kernel(sample, s_ref, y_ref, wglu_ref, bglu_ref, wout_ref, g_ref, b_ref, out_ref, z_ref):
    n_rows = z_ref.shape[1]
    if sample:
        parts = [(pl.ds(l, DEC_BATCH, stride=DEC_SEQ), slice(l * DEC_BATCH, (l + 1) * DEC_BATCH))
                 for l in range(DEC_SEQ)]
    else:
        parts = [(slice(s * SEG_TILE, (s + 1) * SEG_TILE), pl.ds(s, SEG_TILE, stride=N_SEG)) for s in range(N_SEG)]
    for c in range(N_LCHUNK):
        for tok_rows, slab_rows in parts:
            z_ref[c, tok_rows, :] = s_ref[c, slab_rows, :]
    z = jax.nn.gelu(jnp.concatenate([z_ref[c] for c in range(N_LCHUNK)], axis=1))
    gate = jax.nn.sigmoid(_dot(z.astype(BF16), wglu_ref[...]) + bglu_ref[...])
    mix = _dot((z * gate).astype(BF16), wout_ref[...])
    res = y_ref[...].reshape(n_rows, D_MODEL)
    out = _layer_norm(DN_ALPHA * res + mix, g_ref[...], b_ref[...])
    out_ref[...] = out.reshape(out_ref.shape)


def _ssm_out(s, y, s_spec, y_spec, grid, rows, sample, w_glu, b_glu, w_out, g, b, name):
    return pl.pallas_call(
        functools.partial(_ssm_out_kernel, sample),
        grid=grid,
        in_specs=[s_spec, y_spec, _resident((D_MODEL, D_MODEL)), _resident((1, D_MODEL)),
                  _resident((D_MODEL, D_MODEL)), _resident((1, D_MODEL)), _resident((1, D_MODEL))],
        out_specs=y_spec,
        out_shape=jax.ShapeDtypeStruct(y.shape, F32),
        scratch_shapes=[pltpu.VMEM((N_LCHUNK, rows, LANES), F32)],
        compiler_params=_params(len(grid)),
        name=name,
    )(s, y, w_glu, b_glu, w_out, g, b)


def _attn_prompt(yp, w_in, sinks, w_out, g, b):
    cos_p, sin_p = _rope_tables(jnp.arange(SEQ))
    tiles_per_seq = SEQ // ROW_TILE
    qa, ka, va, qb, kb, vb, *extra = _attn_proj(
        yp, w_in, cos_p, sin_p, lambda i: i % tiles_per_seq, BF16, prompt_seq=(BATCH, SEQ))
    dilated, tails = extra[:6], extra[6:]
    seq3 = lambda a: a.reshape(BATCH, SEQ, a.shape[-1])
    plane1 = lambda a: a.reshape(BATCH, 1, SEQ, D_A)
    o1, l1 = _band_a(plane1(qa), plane1(ka), plane1(va))
    pats = [(o1.reshape(BATCH * SEQ, D_A), l1.reshape(BATCH * SEQ, LANES))]
    for i in range(len(DILATIONS) - 1):
        pats.append(_band_a(*dilated[3 * i:3 * i + 3]))
    ob = _band_b(seq3(qb), seq3(kb), seq3(vb), sinks)
    yp = _attn_out_prompt(pats, ob, yp, w_out, g, b, SEQ)
    heads = lambda a, nh: jnp.transpose(a.reshape(BATCH, nh, HEAD_DIM, a.shape[-1]), (0, 3, 1, 2))[None]
    prompt_cache = (heads(tails[0], N_HEADS_A), heads(tails[1], N_HEADS_A),
                    heads(tails[2], N_KV_B), heads(tails[3], N_KV_B))
    return yp, prompt_cache


def _attn_sample_path(ys, cache_ak, cache_av, cache_bk, cache_bv, w_in, sinks, w_out, g, b, yp, ffn_args):
    cos_s, sin_s = _rope_tables(PAST_LEN + jnp.arange(DEC_SEQ))
    reps = DEC_BATCH * DEC_SEQ // DEC_SEQ
    cos_s, sin_s = jnp.tile(cos_s, (reps, 1)), jnp.tile(sin_s, (reps, 1))
    qa, ka, va, qb, kb, vb = _attn_proj(ys, w_in, cos_s, sin_s, lambda i: i, F32)
    stored = lambda a: jnp.transpose(a, (0, 2, 3, 1))
    yp, o = _ffn_with_sample_attention(
        yp, *ffn_args, qa, ka, va, stored(cache_ak), stored(cache_av),
        qb, kb, vb, stored(cache_bk), stored(cache_bv), sinks)
    ys = _attn_out_sample(o, ys, w_out, g, b)
    new = lambda a, nh: a.reshape(1, DEC_BATCH, DEC_SEQ, nh, HEAD_DIM)
    sample_cache = (new(ka, N_HEADS_A), new(va, N_HEADS_A), new(kb, N_KV_B), new(vb, N_KV_B))
    return yp, ys, sample_cache


def _attn_layer(yp, ys, cache_ak, cache_av, cache_bk, cache_bv, w_in, sinks, w_out, g, b, ffn_args):
    w_in = w_in.astype(BF16)
    w_out = w_out.astype(BF16)
    yp, prompt_cache = _attn_prompt(yp, w_in, sinks, w_out, g, b)
    yp, ys, sample_cache = _attn_sample_path(ys, cache_ak, cache_av, cache_bk, cache_bv, w_in, sinks, w_out, g, b,
                                             yp, ffn_args)
    return yp, ys, prompt_cache, sample_cache


def _ssm_layer(yp, ys, state_re, state_im, w_in, lam_re, lam_im, log_dt, b_re, b_im, c_re, c_im, d_skip,
               w_glu, b_glu, w_out, g, b):
    w_in, w_glu, w_out = w_in.astype(BF16), w_glu.astype(BF16), w_out.astype(BF16)
    b_glu = b_glu.reshape(1, D_MODEL)
    mats = _ssm_prepare(lam_re, lam_im, log_dt, b_re, b_im, c_re, c_im, d_skip)
    yp, prompt_state = _ssm_prompt(yp, w_in, mats, w_glu, b_glu, w_out, g, b)
    ys, sample_state = _ssm_sample(ys, state_re, state_im, w_in, mats, w_glu, b_glu, w_out, g, b)
    return yp, ys, prompt_state, sample_state


def _ssm_prepare(lam_re, lam_im, log_dt, b_re, b_im, c_re, c_im, d_skip):
    ab_re, ab_im, bb_re, bb_im = _ssm_discretize(lam_re, lam_im, log_dt, b_re, b_im)
    bmat, cmat = _ssm_matrices(bb_re, bb_im, c_re, c_im)
    a_re = ab_re.reshape(N_LCHUNK, 1, ST_CHUNK)
    a_im = ab_im.reshape(N_LCHUNK, 1, ST_CHUNK)
    d3 = d_skip.astype(F32).reshape(N_LCHUNK, 1, LANES)
    prompt = (_chunk_weights(a_re, a_im, bmat, cmat), jnp.tile(d3, (1, 1, CHUNK)))
    sample = (bmat.astype(BF16), cmat.astype(BF16), a_re, a_im, d3)
    return prompt, sample


def _ssm_prompt(yp, w_in, mats, w_glu, b_glu, w_out, g, b):
    weights, d_tiled = mats[0]
    up = _ssm_in_prompt(yp, w_in)
    zero = jnp.zeros((BATCH, 1, N_STATE), F32)
    sp, pr, pi = _scan_prompt(up, weights, d_tiled, zero, zero)
    yp = _ssm_out(sp, yp.reshape(BATCH, N_SEG, SEG_LEN, D_MODEL),
                  pl.BlockSpec((None, N_LCHUNK, WIDE_ROW_TILE, LANES), lambda bb, t: (bb, 0, t, 0)),
                  pl.BlockSpec((None, N_SEG, SEG_TILE, D_MODEL), lambda bb, t: (bb, 0, t, 0)),
                  (BATCH, SEG_LEN // SEG_TILE), WIDE_ROW_TILE, False, w_glu, b_glu, w_out, g, b,
                  "ssm_out_prompt").reshape(BATCH * SEQ, D_MODEL)
    prompt_state = (pr.reshape(1, BATCH, N_SSM_GROUPS, SSM_STATE), pi.reshape(1, BATCH, N_SSM_GROUPS, SSM_STATE))
    return yp, prompt_state


def _ssm_sample(ys, state_re, state_im, w_in, mats, w_glu, b_glu, w_out, g, b):
    bmat, cmat, a_re, a_im, d3 = mats[1]
    us = _ssm_in_sample(ys, w_in)
    stored = lambda a: jnp.transpose(a, (1, 2, 0)).reshape(N_STATE, DEC_BATCH)
    logical = lambda a: jnp.transpose(a.reshape(N_SSM_GROUPS, SSM_STATE, DEC_BATCH), (2, 0, 1))[None]
    ss, sr, si = _scan_sample(us, bmat, cmat, a_re, a_im, d3, stored(state_re), stored(state_im))
    n = DEC_BATCH * DEC_SEQ
    ys = _ssm_out(ss, ys,
                  pl.BlockSpec((N_LCHUNK, n, LANES), lambda i: (0, 0, 0)),
                  pl.BlockSpec((n, D_MODEL), lambda i: (0, 0)),
                  (1,), n, True, w_glu, b_glu, w_out, g, b, "ssm_out_sample")
    return ys, (logical(sr), logical(si))


def kernel(x_prompt, x_sample, cache_a_k, cache_a_v, cache_b_k, cache_b_v, state_c_re, state_c_im, ln_g, ln_b, ffn_w_gate, ffn_w_up, ffn_w_down, attn_w_in, attn_sinks, attn_w_out, ssm_w_in, ssm_lambda_re, ssm_lambda_im, ssm_log_dt, ssm_b_re, ssm_b_im, ssm_c_re, ssm_c_im, ssm_d, ssm_w_glu, ssm_b_glu, ssm_w_out):
    yp = x_prompt.reshape(BATCH * SEQ, D_MODEL)
    ys = x_sample.reshape(DEC_BATCH * DEC_SEQ, D_MODEL)
    ln = lambda l, k: (ln_g[l, k].reshape(1, D_MODEL), ln_b[l, k].reshape(1, D_MODEL))

    wg, wu, wd = ffn_w_gate.astype(BF16), ffn_w_up.astype(BF16), ffn_w_down.astype(BF16)

    def ffn_pair(yp, ys, l, k, ln_idx):
        g, b = ln(l, ln_idx)
        return _ffn_pair(yp, ys, wg, wu, wd, g, b, l, k)

    yp, ys = ffn_pair(yp, ys, 0, 0, 0)
    yp, ys, p_cache, s_cache = _attn_layer(yp, ys, cache_a_k[0], cache_a_v[0], cache_b_k[0], cache_b_v[0],
                                           attn_w_in[0], attn_sinks[0], attn_w_out[0], *ln(0, 1),
                                           ffn_args=(wg, wu, wd, *ln(0, 2), 0, 1))
    ys = _ffn(ys, wg, wu, wd, *ln(0, 2), 0, 1)
    yp, ys = ffn_pair(yp, ys, 1, 0, 0)
    yp, ys, p_state, s_state = _ssm_layer(yp, ys, state_c_re[0], state_c_im[0], ssm_w_in[0], ssm_lambda_re[0],
                                          ssm_lambda_im[0], ssm_log_dt[0], ssm_b_re[0], ssm_b_im[0], ssm_c_re[0],
                                          ssm_c_im[0], ssm_d[0], ssm_w_glu[0], ssm_b_glu[0], ssm_w_out[0], *ln(1, 1))
    yp, ys = ffn_pair(yp, ys, 1, 1, 2)
    return (yp.reshape(BATCH, SEQ, D_MODEL), ys.reshape(DEC_BATCH, DEC_SEQ, D_MODEL),
            *p_cache, *p_state, *s_cache, *s_state)
```

```python
import functools
import math

import jax
import jax.numpy as jnp
from jax import lax
from jax.experimental import pallas as pl
from jax.experimental.pallas import tpu as pltpu

F32 = jnp.float32
BF16 = jnp.bfloat16

D_MODEL = 1024
BATCH = 4
SEQ = 4096
DEPTH = 2
DEC_BATCH = 128
DEC_SEQ = 8
PAST_LEN = 16384
HEAD_DIM = 64
N_HEADS_A = 8
DILATIONS = (1, 4, 16)
WIN_A = 2048
N_HEADS_B = 8
N_KV_B = 2
WIN_B = 128
ROPE_THETA = 10000.0
D_A = N_HEADS_A * HEAD_DIM
D_BQ = N_HEADS_B * HEAD_DIM
D_BKV = N_KV_B * HEAD_DIM
D_IN_ATTN = 3 * D_A + D_BQ + 2 * D_BKV
SSM_GROUP = 16
N_SSM_GROUPS = D_MODEL // SSM_GROUP
SSM_STATE = 64
N_STATE = N_SSM_GROUPS * SSM_STATE
D_FF = 2816
DN_ALPHA = (2 * DEPTH) ** 0.25
FFN_RES = 0.5
LN_EPS = 1e-5
ATTN_SCALE = HEAD_DIM ** -0.5
LOG2E = math.log2(math.e)
LSE_LANES = 16

LANES = 128
SUBLANES = 8
MXU_N = 256
VMEM_LIMIT = 56 * 1024 * 1024

ROW_TILE = 512
WIDE_ROW_TILE = 1024
FFN_ROW_TILE = 1024
FF_CHUNK = MXU_N
TQ = 128
BAND_SUB_TILES = 16
N_SEG = SUBLANES
SEG_LEN = SEQ // N_SEG
N_LCHUNK = D_MODEL // LANES
ST_CHUNK = N_STATE // N_LCHUNK
NK_PAD = WIN_A + LANES
NKB_PAD = 2 * WIN_B

NEG_INF = float("-inf")


def _params(n_axes, vmem=VMEM_LIMIT):
    return pltpu.CompilerParams(dimension_semantics=("arbitrary",) * n_axes, vmem_limit_bytes=vmem)


def _resident(shape):
    return pl.BlockSpec(shape, lambda *_: (0,) * len(shape), pipeline_mode=pl.Buffered(1))


def _layer_norm(x, g, b):
    mu = jnp.mean(x, -1, keepdims=True)
    xc = x - mu
    var = jnp.mean(xc * xc, -1, keepdims=True)
    return xc * lax.rsqrt(var + LN_EPS) * g + b


def _dot(a, b):
    return jnp.dot(a, b, preferred_element_type=F32)


def _dot_nt(a, b):
    return lax.dot_general(a, b, (((1,), (1,)), ((), ())), preferred_element_type=F32)


def _ffn_kernel(x_ref, wg_ref, wu_ref, wd_ref, g_ref, b_ref, o_ref, h_ref):
    x = x_ref[...]
    xb = x.astype(BF16)
    for c in range(D_FF // FF_CHUNK):
        sl = slice(c * FF_CHUNK, (c + 1) * FF_CHUNK)
        gate = _dot(xb, wg_ref[:, sl])
        up = _dot(xb, wu_ref[:, sl])
        h_ref[:, sl] = (gate * jax.nn.sigmoid(gate) * up).astype(BF16)
    y = DN_ALPHA * x + FFN_RES * _dot(h_ref[...], wd_ref[...])
    o_ref[...] = _layer_norm(y, g_ref[...], b_ref[...])


def _ffn(x, wg, wu, wd, g, b, layer=0, which=0):
    n = x.shape[0]
    tm = min(FFN_ROW_TILE, n)
    row = pl.BlockSpec((tm, D_MODEL), lambda i: (i, 0))
    if wg.ndim == 4:
        pick = lambda r, c: pl.BlockSpec((None, None, r, c), lambda i: (layer, which, 0, 0),
                                         pipeline_mode=pl.Buffered(1))
    else:
        pick = lambda r, c: _resident((r, c))
    return pl.pallas_call(
        _ffn_kernel,
        grid=(n // tm,),
        in_specs=[row, pick(D_MODEL, D_FF), pick(D_MODEL, D_FF), pick(D_FF, D_MODEL),
                  _resident((1, D_MODEL)), _resident((1, D_MODEL))],
        out_specs=row,
        out_shape=jax.ShapeDtypeStruct((n, D_MODEL), F32),
        scratch_shapes=[pltpu.VMEM((tm, D_FF), BF16)],
        compiler_params=_params(1),
        name="ffn",
    )(x, wg, wu, wd, g, b)


def _ffn_pair_kernel(n_first, xp_ref, xs_ref, wg_ref, wu_ref, wd_ref, g_ref, b_ref, op_ref, os_ref, h_ref):
    step = pl.program_id(0)
    weights = (wg_ref, wu_ref, wd_ref, g_ref, b_ref)
    pl.when(step < n_first)(functools.partial(_ffn_kernel, xp_ref, *weights, op_ref, h_ref))
    h_small = h_ref.at[pl.ds(0, xs_ref.shape[0])]
    pl.when(step >= n_first)(functools.partial(_ffn_kernel, xs_ref, *weights, os_ref, h_small))


def _ffn_pair(xp, xs, wg, wu, wd, g, b, layer, which):
    tm, tm2 = min(FFN_ROW_TILE, xp.shape[0]), ROW_TILE
    n_first, n_second = xp.shape[0] // tm, xs.shape[0] // tm2
    first = pl.BlockSpec((tm, D_MODEL), lambda i: (jnp.minimum(i, n_first - 1), 0))
    second = pl.BlockSpec((tm2, D_MODEL), lambda i: (jnp.maximum(i - n_first, 0), 0))
    pick = lambda r, c: pl.BlockSpec((None, None, r, c), lambda i: (layer, which, 0, 0), pipeline_mode=pl.Buffered(1))
    return pl.pallas_call(
        functools.partial(_ffn_pair_kernel, n_first),
        grid=(n_first + n_second,),
        in_specs=[first, second, pick(D_MODEL, D_FF), pick(D_MODEL, D_FF), pick(D_FF, D_MODEL),
                  _resident((1, D_MODEL)), _resident((1, D_MODEL))],
        out_specs=[first, second],
        out_shape=[jax.ShapeDtypeStruct(xp.shape, F32), jax.ShapeDtypeStruct(xs.shape, F32)],
        scratch_shapes=[pltpu.VMEM((tm, D_FF), BF16)],
        compiler_params=_params(1),
        name="ffn_pair",
    )(xp, xs, wg, wu, wd, g, b)


def _rope_tables(pos):
    half = HEAD_DIM // 2
    inv_freq = ROPE_THETA ** (-jnp.arange(half, dtype=F32) / half)
    ang = pos.astype(F32)[:, None] * inv_freq[None, :]
    cos, sin = jnp.cos(ang), jnp.sin(ang)
    cos_t = jnp.concatenate([cos, cos, cos, cos], -1)
    sin_t = jnp.concatenate([-sin, sin, -sin, sin], -1)
    return cos_t, sin_t


def _attn_proj_kernel(tiles_per_seq, x_ref, w_ref, cos_ref, sin_ref, qa_ref, ka_ref, va_ref, qb_ref, kb_ref, vb_ref,
                      *extra):
    xb = x_ref[...].astype(BF16)
    slab_ref = extra[-1] if extra else None
    dilated = extra[:6]
    n_chunks = D_A // LANES
    tm = x_ref.shape[0]

    def keep(tensor, c, val):
        if slab_ref is not None:
            slab_ref[tensor * n_chunks + c] = val
    cos = cos_ref[...]
    sin = sin_ref[...]
    lane = lax.broadcasted_iota(jnp.int32, cos.shape, 1)
    first_half = (lane & (HEAD_DIM // 2)) == 0

    def rope(z):
        rot = jnp.where(first_half, pltpu.roll(z, LANES - HEAD_DIM // 2, 1), pltpu.roll(z, HEAD_DIM // 2, 1))
        return z * cos + rot * sin

    def project(col0, ncols):
        return _dot(xb, w_ref[:, col0:col0 + ncols])

    def rope_chunks(z):
        return [rope(z[:, c * LANES:(c + 1) * LANES]) for c in range(z.shape[1] // LANES)]

    q_scale = ATTN_SCALE * LOG2E if extra else ATTN_SCALE
    col = 0
    for c, r in enumerate(rope_chunks(project(col, D_A))):
        r = r * q_scale
        qa_ref[:, c * LANES:(c + 1) * LANES] = r.astype(qa_ref.dtype)
        keep(0, c, r)
    col += D_A
    for c, r in enumerate(rope_chunks(project(col, D_A))):
        ka_ref[:, c * LANES:(c + 1) * LANES] = r.astype(ka_ref.dtype)
        keep(1, c, r)
    col += D_A
    z = project(col, D_A)
    va_ref[...] = z.astype(va_ref.dtype)
    for c in range(n_chunks):
        keep(2, c, z[:, c * LANES:(c + 1) * LANES])
    col += D_A
    if extra:
        kat_ref, vat_ref, kbt_ref, vbt_ref = extra[6:10]
        tile_in_seq = pl.program_id(0) % tiles_per_seq

        @pl.when(tile_in_seq >= tiles_per_seq - WIN_A // tm)
        def _():
            for c in range(n_chunks):
                kat_ref[c * LANES:(c + 1) * LANES, :] = slab_ref[n_chunks + c].T
                vat_ref[c * LANES:(c + 1) * LANES, :] = slab_ref[2 * n_chunks + c].T

        slab4_ref = extra[-2]
        d1, d2 = DILATIONS[1], DILATIONS[2] // DILATIONS[1]
        plane = tm // d1
        for tensor in range(3):
            out1_ref, out2_ref = dilated[tensor], dilated[3 + tensor]
            for c in range(n_chunks):
                idx = tensor * n_chunks + c
                lanes = slice(c * LANES, (c + 1) * LANES)
                for r in range(d1):
                    rows = slab_ref[idx, pl.ds(r, plane, stride=d1), :]
                    slab4_ref[idx, r * plane:(r + 1) * plane, :] = rows
                    out1_ref[r, :, lanes] = rows.astype(out1_ref.dtype)
                for r in range(d1):
                    for m in range(d2):
                        rows = slab4_ref[idx, pl.ds(r * plane + m, plane // d2, stride=d2), :]
                        out2_ref[r + d1 * m, :, lanes] = rows.astype(out2_ref.dtype)
    for c, r in enumerate(rope_chunks(project(col, D_BQ))):
        qb_ref[:, c * LANES:(c + 1) * LANES] = (r * q_scale).astype(qb_ref.dtype)
    col += D_BQ
    z = project(col, 2 * D_BKV)
    r = rope(z[:, :D_BKV])
    if extra:
        lo = _lane_lo(r.shape)
        for ref, val in ((kb_ref, r), (vb_ref, z[:, D_BKV:])):
            swapped = pltpu.roll(val, HEAD_DIM, 1)
            ref[:, 0:LANES] = jnp.where(lo, val, swapped).astype(ref.dtype)
            ref[:, LANES:] = jnp.where(lo, swapped, val).astype(ref.dtype)
    else:
        kb_ref[...] = r.astype(kb_ref.dtype)
        vb_ref[...] = z[:, D_BKV:].astype(vb_ref.dtype)
    if extra:
        @pl.when(tile_in_seq == tiles_per_seq - 1)
        def _():
            kbt_ref[...] = r[tm - WIN_B:, :].T
            vbt_ref[...] = z[tm - WIN_B:, D_BKV:].T


def _attn_proj(x, w, cos_t, sin_t, table_block, act_dtype, prompt_seq=None):
    n = x.shape[0]
    tm = min(ROW_TILE, n)

    def row(width):
        return pl.BlockSpec((tm, width), lambda i: (i, 0))

    tab = pl.BlockSpec((tm, LANES), lambda i: (table_block(i), 0))
    kv_b = D_BKV if prompt_seq is None else N_KV_B * LANES
    widths = (D_A, D_A, D_A, D_BQ, kv_b, kv_b)
    out_shape = [jax.ShapeDtypeStruct((n, wd), act_dtype) for wd in widths]
    out_specs = [row(wd) for wd in widths]
    scratch = []
    tps = None
    if prompt_seq is not None:
        bsz, seq = prompt_seq
        tps = seq // tm
        for dil in DILATIONS[1:]:
            out_shape += [jax.ShapeDtypeStruct((bsz, dil, seq // dil, D_A), BF16)] * 3
            out_specs += [pl.BlockSpec((None, dil, tm // dil, D_A), lambda i: (i // tps, 0, i % tps, 0))] * 3
        first_tail = tps - WIN_A // tm
        out_shape += [jax.ShapeDtypeStruct((bsz, D_A, WIN_A), F32)] * 2
        out_specs += [pl.BlockSpec((None, D_A, tm), lambda i: (i // tps, 0, jnp.maximum(i % tps - first_tail, 0)))] * 2
        out_shape += [jax.ShapeDtypeStruct((bsz, D_BKV, WIN_B), F32)] * 2
        out_specs += [pl.BlockSpec((None, D_BKV, WIN_B), lambda i: (i // tps, 0, 0))] * 2
        assert DILATIONS[2] == DILATIONS[1] ** 2
        scratch = [pltpu.VMEM((3 * D_A // LANES, tm, LANES), F32)] * 2
    return pl.pallas_call(
        functools.partial(_attn_proj_kernel, tps),
        grid=(n // tm,),
        in_specs=[row(D_MODEL), _resident((D_MODEL, D_IN_ATTN)), tab, tab],
        out_specs=out_specs,
        out_shape=out_shape,
        scratch_shapes=scratch,
        compiler_params=_params(1),
        name="attn_proj",
    )(x, w, cos_t, sin_t)


def _lane_lo(shape):
    return lax.broadcasted_iota(jnp.int32, shape, 1) < HEAD_DIM


def _half_masks_bf16():
    lo = jnp.where(_lane_lo((1, LANES)), 1.0, 0.0).astype(BF16)
    return lo, 1 - lo


def _band_masks(n_heads, t, sub):
    row = lax.broadcasted_iota(jnp.int32, (n_heads * TQ, TQ), 0) & (TQ - 1)
    col = lax.broadcasted_iota(jnp.int32, (n_heads * TQ, TQ), 1)
    shift = jnp.where(t > 0, 0, TQ) if sub == 0 else 0
    return col <= row, col >= row + shift


def _sub_tile_kv(sub, sl, kc_ref, kp_ref, vc_ref, vp_ref):
    cur = slice(sub * TQ, (sub + 1) * TQ)
    if sub == 0:
        return kc_ref[cur, sl], kp_ref[:, sl], vc_ref[cur, sl], vp_ref[:, sl]
    prev = slice((sub - 1) * TQ, sub * TQ)
    return kc_ref[cur, sl], kc_ref[prev, sl], vc_ref[cur, sl], vc_ref[prev, sl]


def _band_softmax(qs, kc, kp, vc, vp, mask_c, mask_p):
    s_c = jnp.where(mask_c, _dot_nt(qs, kc), NEG_INF)
    s_p = jnp.where(mask_p, _dot_nt(qs, kp), NEG_INF)
    m = jnp.max(jnp.maximum(s_c, s_p), -1, keepdims=True)
    p_c = jnp.exp2(s_c - m)
    p_p = jnp.exp2(s_p - m)
    den = jnp.sum(p_c + p_p, -1, keepdims=True)
    acc = _dot(p_c.astype(BF16), vc) + _dot(p_p.astype(BF16), vp)
    return acc * (1.0 / den), m, den


def _band_a_kernel(q_ref, kc_ref, kp_ref, vc_ref, vp_ref, o_ref, lse_ref):
    t = pl.program_id(2)
    lo = _lane_lo((TQ, LANES))
    lo_bf, hi_bf = _half_masks_bf16()
    lane_head = jnp.right_shift(lax.broadcasted_iota(jnp.int32, (TQ, LANES), 1), int(math.log2(LSE_LANES)))
    for plane in range(q_ref.shape[0]):
        refs = (kc_ref.at[plane], kp_ref.at[plane], vc_ref.at[plane], vp_ref.at[plane])
        for sub in range(q_ref.shape[1] // TQ):
            rows = slice(sub * TQ, (sub + 1) * TQ)
            mask_c, mask_p = _band_masks(2, t, sub)
            lse_tile = jnp.zeros((TQ, LANES), F32)
            for c in range(D_A // LANES):
                sl = slice(c * LANES, (c + 1) * LANES)
                kc, kp, vc, vp = _sub_tile_kv(sub, sl, *refs)
                q2 = q_ref[plane, rows, sl]
                qs = jnp.concatenate([q2 * lo_bf, q2 * hi_bf], axis=0)
                out, m, den = _band_softmax(qs, kc, kp, vc, vp, mask_c, mask_p)
                lse = m + jnp.log2(den)
                o_ref[plane, rows, sl] = jnp.where(lo, out[0:TQ], out[TQ:]).astype(o_ref.dtype)
                lse_tile = jnp.where(lane_head == 2 * c, lse[0:TQ], lse_tile)
                lse_tile = jnp.where(lane_head == 2 * c + 1, lse[TQ:], lse_tile)
            lse_ref[plane, rows, :] = lse_tile


def _band_a(q, k, v):
    bsz, dil, sub, _ = q.shape
    n_sub = min(BAND_SUB_TILES, sub // TQ)
    n_planes = BAND_SUB_TILES // n_sub
    assert dil % n_planes == 0 and sub % (n_sub * TQ) == 0
    cur = pl.BlockSpec((None, n_planes, n_sub * TQ, D_A), lambda b, r, t: (b, r, t, 0))
    prev = pl.BlockSpec((None, n_planes, TQ, D_A), lambda b, r, t: (b, r, jnp.maximum(n_sub * t - 1, 0), 0))
    lse = pl.BlockSpec((None, n_planes, n_sub * TQ, LANES), lambda b, r, t: (b, r, t, 0))
    return pl.pallas_call(
        _band_a_kernel,
        grid=(bsz, dil // n_planes, sub // (n_sub * TQ)),
        in_specs=[cur, cur, prev, cur, prev],
        out_specs=[cur, lse],
        out_shape=[jax.ShapeDtypeStruct((bsz, dil, sub, D_A), BF16),
                   jax.ShapeDtypeStruct((bsz, dil, sub, LANES), F32)],
        compiler_params=_params(3),
        name=f"band_a_d{dil}",
    )(q, k, k, v, v)


def _band_b_body(first_tile, sink_ref, q_ref, kc_ref, kp_ref, vc_ref, vp_ref, o_ref):
    group = N_HEADS_B // N_KV_B
    lo = _lane_lo((TQ, LANES))
    lo_bf, hi_bf = _half_masks_bf16()
    row = lax.broadcasted_iota(jnp.int32, (group * TQ, TQ), 0) & (TQ - 1)
    col = lax.broadcasted_iota(jnp.int32, (group * TQ, TQ), 1)
    in_cur = col <= row
    cur_bf = jnp.where(in_cur, 1.0, 0.0).astype(BF16)
    prev_bf = 1 - cur_bf
    for sub in range(q_ref.shape[0] // TQ):
        rows = slice(sub * TQ, (sub + 1) * TQ)
        for g in range(N_KV_B):
            sl = slice(g * LANES, (g + 1) * LANES)
            kc, kp, vc, vp = _sub_tile_kv(sub, sl, kc_ref, kp_ref, vc_ref, vp_ref)
            heads = range(g * group, (g + 1) * group)
            qs = jnp.concatenate(
                [q_ref[rows, (h // 2) * LANES:(h // 2 + 1) * LANES] * (lo_bf if h % 2 == 0 else hi_bf) for h in heads],
                axis=0)
            sink = jnp.concatenate([jnp.full((TQ, 1), sink_ref[h] * LOG2E, F32) for h in heads], axis=0)
            only_cur = first_tile and sub == 0
            s = jnp.where(in_cur, _dot_nt(qs, kc), NEG_INF if only_cur else _dot_nt(qs, kp))
            m = jnp.maximum(jnp.max(s, -1, keepdims=True), sink)
            p = jnp.exp2(s - m)
            den = jnp.sum(p, -1, keepdims=True) + jnp.exp2(sink - m)
            pb = p.astype(BF16)
            acc = _dot(pb, vc) if only_cur else _dot(pb * cur_bf, vc) + _dot(pb * prev_bf, vp)
            out = acc * (1.0 / den)
            for i in range(group // 2):
                c = g * (group // 2) + i
                even, odd = out[2 * i * TQ:(2 * i + 1) * TQ], out[(2 * i + 1) * TQ:(2 * i + 2) * TQ]
                o_ref[rows, c * LANES:(c + 1) * LANES] = jnp.where(lo, even, odd).astype(o_ref.dtype)


def _band_b_kernel(*refs):
    t = pl.program_id(1)
    pl.when(t == 0)(functools.partial(_band_b_body, True, *refs))
    pl.when(t > 0)(functools.partial(_band_b_body, False, *refs))


def _band_b(q, k, v, sinks):
    bsz, seq, _ = q.shape
    kv_lanes = N_KV_B * LANES
    n_sub = BAND_SUB_TILES
    qs = pl.BlockSpec((None, n_sub * TQ, D_BQ), lambda b, t: (b, t, 0))
    cur = pl.BlockSpec((None, n_sub * TQ, kv_lanes), lambda b, t: (b, t, 0))
    prev = pl.BlockSpec((None, TQ, kv_lanes), lambda b, t: (b, jnp.maximum(n_sub * t - 1, 0), 0))
    o = pl.pallas_call(
        _band_b_kernel,
        grid=(bsz, seq // (n_sub * TQ)),
        in_specs=[pl.BlockSpec(memory_space=pltpu.SMEM), qs, cur, prev, cur, prev],
        out_specs=qs,
        out_shape=jax.ShapeDtypeStruct((bsz, seq, D_BQ), BF16),
        compiler_params=_params(2),
        name="band_b",
    )(sinks, q, k, k, v, v)
    return o.reshape(bsz * seq, D_BQ)


def _pattern_count(dist):
    cnt = jnp.zeros(dist.shape, F32)
    for dil in DILATIONS:
        cnt = cnt + ((dist >= 0) & (dist <= 128 * dil) & (dist % dil == 0)).astype(F32)
    return cnt


def _sample_tables():
    i = jnp.arange(DEC_SEQ)
    cnt_c = _pattern_count(WIN_A + i[:, None] - jnp.arange(WIN_A)[None, :])
    j = jnp.arange(LANES)
    cnt_n = jnp.where(j[None, :] < DEC_SEQ, _pattern_count(i[:, None] - j[None, :]), 0.0)
    cnt_n = jnp.tile(cnt_n, (N_HEADS_A, 1))
    jb = jnp.arange(NKB_PAD)[None, :]
    dist_b = WIN_B + i[:, None] - jb
    ok_b = (dist_b >= 0) & (dist_b < WIN_B) & (jb < WIN_B + DEC_SEQ)
    mask_b = jnp.tile(ok_b.astype(F32), (N_HEADS_B, 1))
    return cnt_c, cnt_n, mask_b


def _sample_attend(q, kan, van, kt_ref, vt_ref, cnt_c, cnt_n, qb, kbn, vbn, kbc, vbc, mask_b, sink_col, kb_s, vb_s):
    rows = N_HEADS_A * DEC_SEQ
    q_rep = jnp.concatenate([q] * N_HEADS_A, axis=0)
    row_head = jnp.right_shift(lax.broadcasted_iota(jnp.int32, (rows, D_A), 0), int(math.log2(DEC_SEQ)))
    lane_head = jnp.right_shift(lax.broadcasted_iota(jnp.int32, (rows, D_A), 1), int(math.log2(HEAD_DIM)))
    own = row_head == lane_head
    q_bd = jnp.where(own, q_rep, 0.0).astype(BF16)
    pad = jnp.zeros((LANES - DEC_SEQ, D_A), F32)
    kn = jnp.concatenate([kan, pad], 0).astype(BF16)
    vn = jnp.concatenate([van, pad], 0).astype(BF16)
    s_new = jnp.where(cnt_n > 0.0, _dot_nt(q_bd, kn), NEG_INF)
    outs, p_new = [], []
    for h in range(N_HEADS_A):
        head_rows = slice(h * DEC_SEQ, (h + 1) * DEC_SEQ)
        q_h = q[:, h * HEAD_DIM:(h + 1) * HEAD_DIM].astype(BF16)
        s_c = jnp.where(cnt_c > 0.0, _dot(q_h, kt_ref[h].astype(BF16)), NEG_INF)
        s_n = s_new[head_rows]
        m = jnp.maximum(jnp.max(s_c, -1, keepdims=True), jnp.max(s_n, -1, keepdims=True))
        p_c = jnp.exp(s_c - m) * cnt_c
        p_n = jnp.exp(s_n - m) * cnt_n[head_rows]
        inv = 1.0 / (jnp.sum(p_c, -1, keepdims=True) + jnp.sum(p_n, -1, keepdims=True))
        outs.append(_dot_nt(p_c.astype(BF16), vt_ref[h].astype(BF16)) * inv)
        p_new.append(p_n * inv)
    out_n = jnp.where(own, _dot(jnp.concatenate(p_new, axis=0).astype(BF16), vn), 0.0)
    oa = jnp.concatenate(outs, axis=1)
    for h in range(N_HEADS_A):
        oa = oa + out_n[h * DEC_SEQ:(h + 1) * DEC_SEQ]

    n_pad_b = NKB_PAD - WIN_B - DEC_SEQ
    pad_b = jnp.zeros((n_pad_b, D_BKV), F32)
    kb_s[...] = jnp.concatenate([kbc, kbn, pad_b], 0).astype(BF16)
    vb_s[...] = jnp.concatenate([vbc, vbn, pad_b], 0).astype(BF16)
    lo8 = _lane_lo((DEC_SEQ, LANES))
    group = N_HEADS_B // N_KV_B
    pieces = []
    for h in range(N_HEADS_B):
        chunk = qb[:, (h // 2) * LANES:(h // 2 + 1) * LANES]
        g = h // group
        if h % 2 != g:
            chunk = pltpu.roll(chunk, HEAD_DIM, 1)
        pieces.append(jnp.where(lo8 if g == 0 else jnp.logical_not(lo8), chunk, 0.0))
    qb_bd = jnp.concatenate(pieces, axis=0).astype(BF16)
    sb = jnp.where(mask_b > 0.0, _dot_nt(qb_bd, kb_s[...]), NEG_INF)
    sink = sink_col[:, 0:1]
    mb = jnp.maximum(jnp.max(sb, -1, keepdims=True), sink)
    pb = jnp.exp(sb - mb) * mask_b
    den_b = jnp.sum(pb, -1, keepdims=True) + jnp.exp(sink - mb)
    ob_full = _dot(pb.astype(BF16), vb_s[...]) * (1.0 / den_b)
    ob = []
    for c in range(D_BQ // LANES):
        halves = []
        for half in range(2):
            h = 2 * c + half
            piece = ob_full[h * DEC_SEQ:(h + 1) * DEC_SEQ]
            if half != h // group:
                piece = pltpu.roll(piece, HEAD_DIM, 1)
            halves.append(piece)
        ob.append(jnp.where(lo8, halves[0], halves[1]))
    return jnp.concatenate([oa] + ob, axis=1)


FUSED_ROW_TILE = 512
SEQ_PER_STEP = DEC_BATCH // (BATCH * SEQ // FUSED_ROW_TILE)


def _kv_copies(kt_hbm, vt_hbm, kbuf, vbuf, sems, seq, slot):
    return (pltpu.make_async_copy(kt_hbm.at[seq], kbuf.at[slot], sems.at[0, slot]),
            pltpu.make_async_copy(vt_hbm.at[seq], vbuf.at[slot], sems.at[1, slot]))


def _ffn_attn_kernel(x_ref, wg_ref, wu_ref, wd_ref, g_ref, b_ref,
                     qa_ref, kan_ref, van_ref, kt_hbm, vt_hbm, cnt_c_ref, cnt_n_ref,
                     qb_ref, kbn_ref, vbn_ref, kbc_ref, vbc_ref, maskb_ref, sinkcol_ref,
                     y_ref, o_ref, h_ref, kbuf, vbuf, sems, kb_s, vb_s):
    step = pl.program_id(0)
    n_steps = pl.num_programs(0)
    copies = functools.partial(_kv_copies, kt_hbm, vt_hbm, kbuf, vbuf, sems)

    @pl.when(step == 0)
    def _():
        for cp in copies(0, 0):
            cp.start()

    x = x_ref[...]
    xb = x.astype(BF16)
    n_ff = D_FF // FF_CHUNK
    per_seq = -(-n_ff // SEQ_PER_STEP)
    cnt_c, cnt_n, mask_b, sink_col = cnt_c_ref[...], cnt_n_ref[...], maskb_ref[...], sinkcol_ref[...]
    for s in range(SEQ_PER_STEP):
        slot = s % 2
        seq = step * SEQ_PER_STEP + s
        for cp in copies(seq, slot):
            cp.wait()
        if s + 1 < SEQ_PER_STEP:
            for cp in copies(seq + 1, 1 - slot):
                cp.start()
        else:
            @pl.when(step + 1 < n_steps)
            def _():
                for cp in copies(seq + 1, 1 - slot):
                    cp.start()
        rows = slice(s * DEC_SEQ, (s + 1) * DEC_SEQ)
        kbc = jnp.concatenate([kbc_ref[s, g].T for g in range(N_KV_B)], axis=1)
        vbc = jnp.concatenate([vbc_ref[s, g].T for g in range(N_KV_B)], axis=1)
        o_ref[rows, :] = _sample_attend(
            qa_ref[rows, :], kan_ref[rows, :], van_ref[rows, :], kbuf.at[slot], vbuf.at[slot], cnt_c, cnt_n,
            qb_ref[rows, :], kbn_ref[rows, :], vbn_ref[rows, :], kbc, vbc, mask_b, sink_col, kb_s, vb_s)
        for c in range(s * per_seq, min((s + 1) * per_seq, n_ff)):
            sl = slice(c * FF_CHUNK, (c + 1) * FF_CHUNK)
            gate = _dot(xb, wg_ref[:, sl])
            up = _dot(xb, wu_ref[:, sl])
            h_ref[:, sl] = (gate * jax.nn.sigmoid(gate) * up).astype(BF16)
    y = DN_ALPHA * x + FFN_RES * _dot(h_ref[...], wd_ref[...])
    y_ref[...] = _layer_norm(y, g_ref[...], b_ref[...])


def _ffn_with_sample_attention(x, wg, wu, wd, g, b, layer, which,
                               qa, kan, van, cache_ak, cache_av, qb, kbn, vbn, cache_bk, cache_bv, sinks):
    n = x.shape[0]
    tm = FUSED_ROW_TILE
    assert n // tm * SEQ_PER_STEP == DEC_BATCH and SEQ_PER_STEP % 2 == 0
    cnt_c, cnt_n, mask_b = _sample_tables()
    sink_col = jnp.broadcast_to(jnp.repeat(sinks.astype(F32), DEC_SEQ)[:, None], (N_HEADS_B * DEC_SEQ, LANES))
    row = pl.BlockSpec((tm, D_MODEL), lambda i: (i, 0))
    pick = lambda r, c: pl.BlockSpec((None, None, r, c), lambda i: (layer, which, 0, 0), pipeline_mode=pl.Buffered(1))
    new = lambda width: pl.BlockSpec((SEQ_PER_STEP * DEC_SEQ, width), lambda i: (i, 0))
    cache_b = pl.BlockSpec((SEQ_PER_STEP, N_KV_B, HEAD_DIM, WIN_B), lambda i: (i, 0, 0, 0))
    hbm = pl.BlockSpec(memory_space=pl.ANY)
    rows = N_HEADS_A * DEC_SEQ
    kv_slot = (2, N_HEADS_A, HEAD_DIM, WIN_A)
    return pl.pallas_call(
        _ffn_attn_kernel,
        grid=(n // tm,),
        in_specs=[row, pick(D_MODEL, D_FF), pick(D_MODEL, D_FF), pick(D_FF, D_MODEL),
                  _resident((1, D_MODEL)), _resident((1, D_MODEL)),
                  new(D_A), new(D_A), new(D_A), hbm, hbm, _resident(cnt_c.shape), _resident(cnt_n.shape),
                  new(D_BQ), new(D_BKV), new(D_BKV), cache_b, cache_b,
                  _resident((rows, NKB_PAD)), _resident((rows, LANES))],
        out_specs=[row, new(D_A + D_BQ)],
        out_shape=[jax.ShapeDtypeStruct((n, D_MODEL), F32),
                   jax.ShapeDtypeStruct((DEC_BATCH * DEC_SEQ, D_A + D_BQ), F32)],
        scratch_shapes=[pltpu.VMEM((tm, D_FF), BF16), pltpu.VMEM(kv_slot, F32), pltpu.VMEM(kv_slot, F32),
                        pltpu.SemaphoreType.DMA((2, 2)),
                        pltpu.VMEM((NKB_PAD, D_BKV), BF16), pltpu.VMEM((NKB_PAD, D_BKV), BF16)],
        compiler_params=_params(1, vmem=60 * 1024 * 1024),
        name="ffn_attn_sample",
    )(x, wg, wu, wd, g, b, qa, kan, van, cache_ak, cache_av, cnt_c, cnt_n, qb, kbn, vbn, cache_bk, cache_bv,
      mask_b, sink_col)


def _attn_out_prompt_kernel(o1_ref, l1_ref, o4_ref, l4_ref, o16_ref, l16_ref, ob_ref, y_ref, w_ref, g_ref, b_ref,
                            out_ref, slab_ref, oa_ref):
    tm = y_ref.shape[0]
    n_chunks = D_A // LANES
    slabs = {}
    base = 0
    for name, dil, src, width in (("o4", 4, o4_ref, n_chunks), ("l4", 4, l4_ref, 1),
                                  ("o16", 16, o16_ref, n_chunks), ("l16", 16, l16_ref, 1)):
        slabs[name] = base
        for r in range(dil):
            for c in range(width):
                piece = src[r, :, c * LANES:(c + 1) * LANES]
                slab_ref[base + c, pl.ds(r, tm // dil, stride=dil), :] = piece.astype(F32)
        base += width
    l1, l4, l16 = l1_ref[...], slab_ref[slabs["l4"]], slab_ref[slabs["l16"]]
    m = jnp.maximum(jnp.maximum(l1, l4), l16)
    e1, e4, e16 = jnp.exp2(l1 - m), jnp.exp2(l4 - m), jnp.exp2(l16 - m)
    inv = 1.0 / (e1 + e4 + e16)
    row = lax.broadcasted_iota(jnp.int32, (LANES, D_A), 0)
    head_of_col = jnp.right_shift(lax.broadcasted_iota(jnp.int32, (LANES, D_A), 1), int(math.log2(HEAD_DIM)))
    spread = jnp.where(row == head_of_col * LSE_LANES, 1.0, 0.0).astype(BF16)

    def per_head_lanes(w):
        hi = w.astype(BF16)
        lo = (w - hi.astype(F32)).astype(BF16)
        return _dot(hi, spread) + _dot(lo, spread)

    w1, w4, w16 = per_head_lanes(e1 * inv), per_head_lanes(e4 * inv), per_head_lanes(e16 * inv)
    for c in range(n_chunks):
        sl = slice(c * LANES, (c + 1) * LANES)
        oa = (w1[:, sl] * o1_ref[:, sl].astype(F32) + w4[:, sl] * slab_ref[slabs["o4"] + c]
              + w16[:, sl] * slab_ref[slabs["o16"] + c])
        oa_ref[:, sl] = oa.astype(BF16)
    mix = _dot(oa_ref[...], w_ref[0:D_A, :]) + _dot(ob_ref[...], w_ref[D_A:, :])
    out_ref[...] = _layer_norm(DN_ALPHA * y_ref[...] + mix, g_ref[...], b_ref[...])


def _attn_out_prompt(pats, ob, y, w, g, b, seq):
    n = y.shape[0]
    tm = WIDE_ROW_TILE
    tps = seq // tm
    half = pl.BlockSpec((tm, D_A), lambda i: (i, 0))
    full = pl.BlockSpec((tm, D_MODEL), lambda i: (i, 0))
    lse1 = pl.BlockSpec((tm, LANES), lambda i: (i, 0))
    planes = lambda dil, width: pl.BlockSpec((None, dil, tm // dil, width), lambda i: (i // tps, 0, i % tps, 0))
    (o1, l1), (o4, l4), (o16, l16) = pats
    return pl.pallas_call(
        _attn_out_prompt_kernel,
        grid=(n // tm,),
        in_specs=[half, lse1, planes(4, D_A), planes(4, LANES), planes(16, D_A), planes(16, LANES), half, full,
                  _resident((D_MODEL, D_MODEL)), _resident((1, D_MODEL)), _resident((1, D_MODEL))],
        out_specs=full,
        out_shape=jax.ShapeDtypeStruct((n, D_MODEL), F32),
        scratch_shapes=[pltpu.VMEM((2 * (D_A // LANES + 1), tm, LANES), F32), pltpu.VMEM((tm, D_A), BF16)],
        compiler_params=_params(1),
        name="attn_out_prompt",
    )(o1, l1, o4, l4, o16, l16, ob, y, w, g, b)


def _mix_out_kernel(o_ref, y_ref, w_ref, g_ref, b_ref, out_ref):
    mix = _dot(o_ref[...].astype(BF16), w_ref[...])
    out_ref[...] = _layer_norm(DN_ALPHA * y_ref[...] + mix, g_ref[...], b_ref[...])


def _attn_out_sample(o, y, w, g, b):
    n = y.shape[0]
    tm = min(ROW_TILE, n)
    full = pl.BlockSpec((tm, D_MODEL), lambda i: (i, 0))
    return pl.pallas_call(
        _mix_out_kernel,
        grid=(n // tm,),
        in_specs=[full, full, _resident((D_MODEL, D_MODEL)), _resident((1, D_MODEL)), _resident((1, D_MODEL))],
        out_specs=full,
        out_shape=jax.ShapeDtypeStruct((n, D_MODEL), F32),
        compiler_params=_params(1),
        name="attn_out_sample",
    )(o, y, w, g, b)


def _ssm_discretize(lam_re, lam_im, log_dt, b_re, b_im):
    dt = jnp.exp(log_dt.astype(F32))[:, None]
    lr, li = lam_re.astype(F32), lam_im.astype(F32)
    mag = jnp.exp(lr * dt)
    ab_re, ab_im = mag * jnp.cos(li * dt), mag * jnp.sin(li * dt)
    nr, ni = ab_re - 1.0, ab_im
    den = lr * lr + li * li
    fr, fi = (nr * lr + ni * li) / den, (ni * lr - nr * li) / den
    bb_re = fr[..., None] * b_re - fi[..., None] * b_im
    bb_im = fr[..., None] * b_im + fi[..., None] * b_re
    return ab_re, ab_im, bb_re, bb_im


def _ssm_matrices(bb_re, bb_im, c_re, c_im):
    gpc = LANES // SSM_GROUP
    eye = jnp.eye(gpc, dtype=F32)

    def in_blocks(bb):
        a = bb.reshape(N_LCHUNK, gpc, SSM_STATE, SSM_GROUP)
        return jnp.einsum("jgpn,gh->jgnhp", a, eye).reshape(N_LCHUNK, LANES, ST_CHUNK)

    def out_blocks(cc):
        a = cc.reshape(N_LCHUNK, gpc, SSM_GROUP, SSM_STATE)
        return jnp.einsum("jgnp,gh->jgphn", a, eye).reshape(N_LCHUNK, ST_CHUNK, LANES)

    bmat = jnp.concatenate([in_blocks(bb_re), in_blocks(bb_im)], -1)
    cmat = jnp.concatenate([out_blocks(c_re), -out_blocks(c_im)], 1)
    return bmat, cmat


CHUNK = 8


def _dot3(a, b):
    a_hi, b_hi = a.astype(BF16), b.astype(BF16)
    a_lo, b_lo = (a - a_hi.astype(F32)).astype(BF16), (b - b_hi.astype(F32)).astype(BF16)
    return _dot(a_hi, b_hi) + _dot(a_hi, b_lo) + _dot(a_lo, b_hi)


def _chunk_weights_kernel(ar_ref, ai_ref, acr_ref, aci_ref, bmat_ref, cmat_ref,
                          we_ref, ws_ref, wi_ref, a8r_ref, a8i_ref):
    def powers(r, i, n):
        out = [(jnp.ones_like(r), jnp.zeros_like(r))]
        for _ in range(n):
            out.append(_cmul(out[-1][0], out[-1][1], r, i))
        return out

    row_pow = powers(ar_ref[...], ai_ref[...], CHUNK)
    col_pow = powers(acr_ref[...], aci_ref[...], CHUNK)
    b_re, b_im = bmat_ref[:, 0:ST_CHUNK], bmat_ref[:, ST_CHUNK:]
    c_re, c_im = cmat_ref[0:ST_CHUNK, :], -cmat_ref[ST_CHUNK:, :]

    def scaled_b(power):
        pr, pi = row_pow[power]
        return jnp.concatenate([b_re * pr - b_im * pi, b_re * pi + b_im * pr], axis=1)

    scaled = [scaled_b(tau) for tau in range(CHUNK)]
    for tau in range(CHUNK):
        we_ref[(CHUNK - 1 - tau) * LANES:(CHUNK - tau) * LANES, :] = scaled[tau].astype(BF16)
    all_taps = _dot3(jnp.concatenate(scaled, axis=0), cmat_ref[...]).astype(BF16)
    taps = [all_taps[tau * LANES:(tau + 1) * LANES] for tau in range(CHUNK)]
    for k in range(CHUNK):
        cols = slice(k * LANES, (k + 1) * LANES)
        qr, qi = col_pow[k + 1]
        ws_ref[0:ST_CHUNK, cols] = (c_re * qr - c_im * qi).astype(BF16)
        ws_ref[ST_CHUNK:, cols] = (-(c_re * qi + c_im * qr)).astype(BF16)
    zero = jnp.zeros((LANES, LANES), BF16)
    for k_in in range(CHUNK):
        for k_out in range(CHUNK):
            wi_ref[k_in * LANES:(k_in + 1) * LANES, k_out * LANES:(k_out + 1) * LANES] = (
                taps[k_out - k_in] if k_out >= k_in else zero)
    a8r_ref[...], a8i_ref[...] = row_pow[CHUNK]


def _chunk_weights(a_re, a_im, bmat, cmat):
    cols = lambda a: jnp.broadcast_to(a.reshape(N_LCHUNK, ST_CHUNK, 1), (N_LCHUNK, ST_CHUNK, LANES))
    per_j = lambda r, c: pl.BlockSpec((None, r, c), lambda j: (j, 0, 0))
    wide = 2 * ST_CHUNK
    return pl.pallas_call(
        _chunk_weights_kernel,
        grid=(N_LCHUNK,),
        in_specs=[per_j(1, ST_CHUNK), per_j(1, ST_CHUNK), per_j(ST_CHUNK, LANES), per_j(ST_CHUNK, LANES),
                  per_j(LANES, wide), per_j(wide, LANES)],
        out_specs=[per_j(CHUNK * LANES, wide), per_j(wide, CHUNK * LANES), per_j(CHUNK * LANES, CHUNK * LANES),
                   per_j(1, ST_CHUNK), per_j(1, ST_CHUNK)],
        out_shape=[jax.ShapeDtypeStruct((N_LCHUNK, CHUNK * LANES, wide), BF16),
                   jax.ShapeDtypeStruct((N_LCHUNK, wide, CHUNK * LANES), BF16),
                   jax.ShapeDtypeStruct((N_LCHUNK, CHUNK * LANES, CHUNK * LANES), BF16),
                   jax.ShapeDtypeStruct((N_LCHUNK, 1, ST_CHUNK), F32),
                   jax.ShapeDtypeStruct((N_LCHUNK, 1, ST_CHUNK), F32)],
        compiler_params=_params(1),
        name="ssm_chunk_weights",
    )(a_re, a_im, cols(a_re), cols(a_im), bmat, cmat)


SEG_TILE = WIDE_ROW_TILE // N_SEG


def _ssm_in_prompt_kernel(x_ref, w_ref, o_ref):
    x = x_ref[...].reshape(N_SEG * SEG_TILE, D_MODEL)
    u = _dot(x.astype(BF16), w_ref[...])
    for s in range(N_SEG):
        for c in range(N_LCHUNK):
            o_ref[c, pl.ds(s, SEG_TILE, stride=N_SEG), :] = u[s * SEG_TILE:(s + 1) * SEG_TILE, c * LANES:(c + 1) * LANES]


def _ssm_in_prompt(y, w):
    return pl.pallas_call(
        _ssm_in_prompt_kernel,
        grid=(BATCH, SEG_LEN // SEG_TILE),
        in_specs=[pl.BlockSpec((None, N_SEG, SEG_TILE, D_MODEL), lambda b, t: (b, 0, t, 0)),
                  _resident((D_MODEL, D_MODEL))],
        out_specs=pl.BlockSpec((None, N_LCHUNK, WIDE_ROW_TILE, LANES), lambda b, t: (b, 0, t, 0)),
        out_shape=jax.ShapeDtypeStruct((BATCH, N_LCHUNK, SEQ, LANES), F32),
        compiler_params=_params(2),
        name="ssm_in_prompt",
    )(y.reshape(BATCH, N_SEG, SEG_LEN, D_MODEL), w)


def _ssm_in_sample_kernel(x_ref, w_ref, o_ref, slab_ref):
    u = _dot(x_ref[...].astype(BF16), w_ref[...])
    for c in range(N_LCHUNK):
        slab_ref[c] = u[:, c * LANES:(c + 1) * LANES]
    for l in range(DEC_SEQ):
        for c in range(N_LCHUNK):
            o_ref[c, l * DEC_BATCH:(l + 1) * DEC_BATCH, :] = slab_ref[c, pl.ds(l, DEC_BATCH, stride=DEC_SEQ), :]


def _ssm_in_sample(y, w):
    n = DEC_BATCH * DEC_SEQ
    return pl.pallas_call(
        _ssm_in_sample_kernel,
        grid=(1,),
        in_specs=[_resident((n, D_MODEL)), _resident((D_MODEL, D_MODEL))],
        out_specs=pl.BlockSpec((N_LCHUNK, n, LANES), lambda i: (0, 0, 0)),
        out_shape=jax.ShapeDtypeStruct((N_LCHUNK, n, LANES), F32),
        scratch_shapes=[pltpu.VMEM((N_LCHUNK, n, LANES), F32)],
        compiler_params=_params(1),
        name="ssm_in_sample",
    )(y, w)


def _cmul(ar, ai, br, bi):
    return ar * br - ai * bi, ar * bi + ai * br


def _scan_prompt_kernel(u_ref, we_ref, ws_ref, wi_ref, a8r_ref, a8i_ref, d_ref, h0r_ref, h0i_ref,
                        y_ref, hnr_ref, hni_ref, e_s, hs_s):
    n_chunks = u_ref.shape[0]
    rows = n_chunks * N_SEG
    u_flat = jnp.concatenate([u_ref[:, k].reshape(rows, LANES) for k in range(CHUNK)], axis=1)
    ub = u_flat.astype(BF16)
    e_s[...] = _dot(ub, we_ref[...])
    a8r1, a8i1 = a8r_ref[...], a8i_ref[...]
    a8r = jnp.broadcast_to(a8r1, (N_SEG, ST_CHUNK))
    a8i = jnp.broadcast_to(a8i1, (N_SEG, ST_CHUNK))

    def advance(row, hr, hi):
        er = e_s[pl.ds(row, N_SEG), 0:ST_CHUNK]
        ei = e_s[pl.ds(row, N_SEG), ST_CHUNK:2 * ST_CHUNK]
        return a8r * hr - a8i * hi + er, a8r * hi + a8i * hr + ei

    def pass1(c, carry):
        return advance(pl.multiple_of(c * N_SEG, N_SEG), *carry)

    zero = jnp.zeros((N_SEG, ST_CHUNK), F32)
    er, ei = lax.fori_loop(0, n_chunks, pass1, (zero, zero), unroll=8)

    pr, pi = a8r1, a8i1
    for _ in range(int(math.log2(n_chunks))):
        pr, pi = _cmul(pr, pi, pr, pi)
    hr, hi = h0r_ref[...], h0i_ref[...]
    starts_r, starts_i = [], []
    for s in range(N_SEG):
        starts_r.append(hr)
        starts_i.append(hi)
        gr, gi = _cmul(pr, pi, hr, hi)
        hr, hi = gr + er[s:s + 1], gi + ei[s:s + 1]
    hnr_ref[...] = hr
    hni_ref[...] = hi
    init = (jnp.concatenate(starts_r, 0), jnp.concatenate(starts_i, 0))

    def pass2(c, carry):
        row = pl.multiple_of(c * N_SEG, N_SEG)
        hs_s[pl.ds(row, N_SEG), 0:ST_CHUNK] = carry[0]
        hs_s[pl.ds(row, N_SEG), ST_CHUNK:2 * ST_CHUNK] = carry[1]
        return advance(row, *carry)

    lax.fori_loop(0, n_chunks, pass2, init, unroll=8)
    y = _dot(hs_s[...].astype(BF16), ws_ref[...]) + _dot(ub, wi_ref[...]) + d_ref[...] * u_flat
    for k in range(CHUNK):
        y_ref[:, k] = y[:, k * LANES:(k + 1) * LANES].reshape(n_chunks, N_SEG, LANES)


def _scan_prompt(u, weights, d_skip, h0r, h0i):
    bsz, _, seq, _ = u.shape
    n_chunks = seq // (CHUNK * N_SEG)
    rows = n_chunks * N_SEG
    wide = 2 * ST_CHUNK
    split = lambda a: a.reshape(bsz, N_LCHUNK, n_chunks, CHUNK, N_SEG, LANES)
    chunk = pl.BlockSpec((None, None, n_chunks, CHUNK, N_SEG, LANES), lambda j, b: (b, j, 0, 0, 0, 0))
    per_j = lambda r, c: pl.BlockSpec((None, r, c), lambda j, b: (j, 0, 0))
    state = pl.BlockSpec((None, 1, ST_CHUNK), lambda j, b: (b, 0, j))
    y, hr, hi = pl.pallas_call(
        _scan_prompt_kernel,
        grid=(N_LCHUNK, bsz),
        in_specs=[chunk, per_j(CHUNK * LANES, wide), per_j(wide, CHUNK * LANES), per_j(CHUNK * LANES, CHUNK * LANES),
                  per_j(1, ST_CHUNK), per_j(1, ST_CHUNK), per_j(1, CHUNK * LANES), state, state],
        out_specs=[chunk, state, state],
        out_shape=[jax.ShapeDtypeStruct((bsz, N_LCHUNK, n_chunks, CHUNK, N_SEG, LANES), F32),
                   jax.ShapeDtypeStruct((bsz, 1, N_STATE), F32), jax.ShapeDtypeStruct((bsz, 1, N_STATE), F32)],
        scratch_shapes=[pltpu.VMEM((rows, wide), F32), pltpu.VMEM((rows, wide), F32)],
        compiler_params=_params(2),
        name="ssm_scan_prompt",
    )(split(u), *weights, d_skip, h0r, h0i)
    return y.reshape(u.shape), hr, hi


def _scan_sample_kernel(u_ref, bmat_ref, cmat_ref, are_ref, aim_ref, d_ref, h0r_ref, h0i_ref,
                        y_ref, hnr_ref, hni_ref, h_s):
    a_re, a_im = are_ref[...], aim_ref[...]
    hr, hi = h0r_ref[...].T, h0i_ref[...].T
    u = u_ref[...]
    bu = _dot(u.astype(BF16), bmat_ref[...])
    for l in range(DEC_SEQ):
        rows = slice(l * DEC_BATCH, (l + 1) * DEC_BATCH)
        gr, gi = _cmul(a_re, a_im, hr, hi)
        hr, hi = gr + bu[rows, :ST_CHUNK], gi + bu[rows, ST_CHUNK:]
        h_s[rows, 0:ST_CHUNK] = hr.astype(BF16)
        h_s[rows, ST_CHUNK:] = hi.astype(BF16)
    y_ref[...] = _dot(h_s[...], cmat_ref[...]) + d_ref[...] * u
    hnr_ref[...] = hr.T
    hni_ref[...] = hi.T


def _scan_sample(u, bmat, cmat, a_re, a_im, d_skip, h0r, h0i):
    n = DEC_SEQ * DEC_BATCH
    chunk = pl.BlockSpec((None, n, LANES), lambda j: (j, 0, 0))
    per_j = lambda r, c: pl.BlockSpec((None, r, c), lambda j: (j, 0, 0))
    state = pl.BlockSpec((ST_CHUNK, DEC_BATCH), lambda j: (j, 0))
    return pl.pallas_call(
        _scan_sample_kernel,
        grid=(N_LCHUNK,),
        in_specs=[chunk, per_j(LANES, 2 * ST_CHUNK), per_j(2 * ST_CHUNK, LANES), per_j(1, ST_CHUNK),
                  per_j(1, ST_CHUNK), per_j(1, LANES), state, state],
        out_specs=[chunk, state, state],
        out_shape=[jax.ShapeDtypeStruct((N_LCHUNK, n, LANES), F32),
                   jax.ShapeDtypeStruct((N_STATE, DEC_BATCH), F32), jax.ShapeDtypeStruct((N_STATE, DEC_BATCH), F32)],
        scratch_shapes=[pltpu.VMEM((n, 2 * ST_CHUNK), BF16)],
        compiler_params=_params(1),
        name="ssm_scan_sample",
    )(u, bmat, cmat, a_re, a_im, d_skip, h0r, h0i)


def _ssm_out_kernel(sample, s_ref, y_ref, wglu_ref, bglu_ref, wout_ref, g_ref, b_ref, out_ref, z_ref):
    n_rows = z_ref.shape[1]
    if sample:
        parts = [(pl.ds(l, DEC_BATCH, stride=DEC_SEQ), slice(l * DEC_BATCH, (l + 1) * DEC_BATCH))
                 for l in range(DEC_SEQ)]
    else:
        parts = [(slice(s * SEG_TILE, (s + 1) * SEG_TILE), pl.ds(s, SEG_TILE, stride=N_SEG)) for s in range(N_SEG)]
    for c in range(N_LCHUNK):
        for tok_rows, slab_rows in parts:
            z_ref[c, tok_rows, :] = s_ref[c, slab_rows, :]
    z = jax.nn.gelu(jnp.concatenate([z_ref[c] for c in range(N_LCHUNK)], axis=1))
    gate = jax.nn.sigmoid(_dot(z.astype(BF16), wglu_ref[...]) + bglu_ref[...])
    mix = _dot((z * gate).astype(BF16), wout_ref[...])
    res = y_ref[...].reshape(n_rows, D_MODEL)
    out = _layer_norm(DN_ALPHA * res + mix, g_ref[...], b_ref[...])
    out_ref[...] = out.reshape(out_ref.shape)


def _ssm_out(s, y, s_spec, y_spec, grid, rows, sample, w_glu, b_glu, w_out, g, b, name):
    return pl.pallas_call(
        functools.partial(_ssm_out_kernel, sample),
        grid=grid,
        in_specs=[s_spec, y_spec, _resident((D_MODEL, D_MODEL)), _resident((1, D_MODEL)),
                  _resident((D_MODEL, D_MODEL)), _resident((1, D_MODEL)), _resident((1, D_MODEL))],
        out_specs=y_spec,
        out_shape=jax.ShapeDtypeStruct(y.shape, F32),
        scratch_shapes=[pltpu.VMEM((N_LCHUNK, rows, LANES), F32)],
        compiler_params=_params(len(grid)),
        name=name,
    )(s, y, w_glu, b_glu, w_out, g, b)


def _attn_prompt(yp, w_in, sinks, w_out, g, b):
    cos_p, sin_p = _rope_tables(jnp.arange(SEQ))
    tiles_per_seq = SEQ // ROW_TILE
    qa, ka, va, qb, kb, vb, *extra = _attn_proj(
        yp, w_in, cos_p, sin_p, lambda i: i % tiles_per_seq, BF16, prompt_seq=(BATCH, SEQ))
    dilated, tails = extra[:6], extra[6:]
    seq3 = lambda a: a.reshape(BATCH, SEQ, a.shape[-1])
    plane1 = lambda a: a.reshape(BATCH, 1, SEQ, D_A)
    o1, l1 = _band_a(plane1(qa), plane1(ka), plane1(va))
    pats = [(o1.reshape(BATCH * SEQ, D_A), l1.reshape(BATCH * SEQ, LANES))]
    for i in range(len(DILATIONS) - 1):
        pats.append(_band_a(*dilated[3 * i:3 * i + 3]))
    ob = _band_b(seq3(qb), seq3(kb), seq3(vb), sinks)
    yp = _attn_out_prompt(pats, ob, yp, w_out, g, b, SEQ)
    heads = lambda a, nh: jnp.transpose(a.reshape(BATCH, nh, HEAD_DIM, a.shape[-1]), (0, 3, 1, 2))[None]
    prompt_cache = (heads(tails[0], N_HEADS_A), heads(tails[1], N_HEADS_A),
                    heads(tails[2], N_KV_B), heads(tails[3], N_KV_B))
    return yp, prompt_cache


def _attn_sample_path(ys, cache_ak, cache_av, cache_bk, cache_bv, w_in, sinks, w_out, g, b, yp, ffn_args):
    cos_s, sin_s = _rope_tables(PAST_LEN + jnp.arange(DEC_SEQ))
    reps = DEC_BATCH * DEC_SEQ // DEC_SEQ
    cos_s, sin_s = jnp.tile(cos_s, (reps, 1)), jnp.tile(sin_s, (reps, 1))
    qa, ka, va, qb, kb, vb = _attn_proj(ys, w_in, cos_s, sin_s, lambda i: i, F32)
    stored = lambda a: jnp.transpose(a, (0, 2, 3, 1))
    yp, o = _ffn_with_sample_attention(
        yp, *ffn_args, qa, ka, va, stored(cache_ak), stored(cache_av),
        qb, kb, vb, stored(cache_bk), stored(cache_bv), sinks)
    ys = _attn_out_sample(o, ys, w_out, g, b)
    new = lambda a, nh: a.reshape(1, DEC_BATCH, DEC_SEQ, nh, HEAD_DIM)
    sample_cache = (new(ka, N_HEADS_A), new(va, N_HEADS_A), new(kb, N_KV_B), new(vb, N_KV_B))
    return yp, ys, sample_cache


def _attn_layer(yp, ys, cache_ak, cache_av, cache_bk, cache_bv, w_in, sinks, w_out, g, b, ffn_args):
    w_in = w_in.astype(BF16)
    w_out = w_out.astype(BF16)
    yp, prompt_cache = _attn_prompt(yp, w_in, sinks, w_out, g, b)
    yp, ys, sample_cache = _attn_sample_path(ys, cache_ak, cache_av, cache_bk, cache_bv, w_in, sinks, w_out, g, b,
                                             yp, ffn_args)
    return yp, ys, prompt_cache, sample_cache


def _ssm_layer(yp, ys, state_re, state_im, w_in, lam_re, lam_im, log_dt, b_re, b_im, c_re, c_im, d_skip,
               w_glu, b_glu, w_out, g, b):
    w_in, w_glu, w_out = w_in.astype(BF16), w_glu.astype(BF16), w_out.astype(BF16)
    b_glu = b_glu.reshape(1, D_MODEL)
    mats = _ssm_prepare(lam_re, lam_im, log_dt, b_re, b_im, c_re, c_im, d_skip)
    yp, prompt_state = _ssm_prompt(yp, w_in, mats, w_glu, b_glu, w_out, g, b)
    ys, sample_state = _ssm_sample(ys, state_re, state_im, w_in, mats, w_glu, b_glu, w_out, g, b)
    return yp, ys, prompt_state, sample_state


def _ssm_prepare(lam_re, lam_im, log_dt, b_re, b_im, c_re, c_im, d_skip):
    ab_re, ab_im, bb_re, bb_im = _ssm_discretize(lam_re, lam_im, log_dt, b_re, b_im)
    bmat, cmat = _ssm_matrices(bb_re, bb_im, c_re, c_im)
    a_re = ab_re.reshape(N_LCHUNK, 1, ST_CHUNK)
    a_im = ab_im.reshape(N_LCHUNK, 1, ST_CHUNK)
    d3 = d_skip.astype(F32).reshape(N_LCHUNK, 1, LANES)
    prompt = (_chunk_weights(a_re, a_im, bmat, cmat), jnp.tile(d3, (1, 1, CHUNK)))
    sample = (bmat.astype(BF16), cmat.astype(BF16), a_re, a_im, d3)
    return prompt, sample


def _ssm_prompt(yp, w_in, mats, w_glu, b_glu, w_out, g, b):
    weights, d_tiled = mats[0]
    up = _ssm_in_prompt(yp, w_in)
    zero = jnp.zeros((BATCH, 1, N_STATE), F32)
    sp, pr, pi = _scan_prompt(up, weights, d_tiled, zero, zero)
    yp = _ssm_out(sp, yp.reshape(BATCH, N_SEG, SEG_LEN, D_MODEL),
                  pl.BlockSpec((None, N_LCHUNK, WIDE_ROW_TILE, LANES), lambda bb, t: (bb, 0, t, 0)),
                  pl.BlockSpec((None, N_SEG, SEG_TILE, D_MODEL), lambda bb, t: (bb, 0, t, 0)),
                  (BATCH, SEG_LEN // SEG_TILE), WIDE_ROW_TILE, False, w_glu, b_glu, w_out, g, b,
                  "ssm_out_prompt").reshape(BATCH * SEQ, D_MODEL)
    prompt_state = (pr.reshape(1, BATCH, N_SSM_GROUPS, SSM_STATE), pi.reshape(1, BATCH, N_SSM_GROUPS, SSM_STATE))
    return yp, prompt_state


def _ssm_sample(ys, state_re, state_im, w_in, mats, w_glu, b_glu, w_out, g, b):
    bmat, cmat, a_re, a_im, d3 = mats[1]
    us = _ssm_in_sample(ys, w_in)
    stored = lambda a: jnp.transpose(a, (1, 2, 0)).reshape(N_STATE, DEC_BATCH)
    logical = lambda a: jnp.transpose(a.reshape(N_SSM_GROUPS, SSM_STATE, DEC_BATCH), (2, 0, 1))[None]
    ss, sr, si = _scan_sample(us, bmat, cmat, a_re, a_im, d3, stored(state_re), stored(state_im))
    n = DEC_BATCH * DEC_SEQ
    ys = _ssm_out(ss, ys,
                  pl.BlockSpec((N_LCHUNK, n, LANES), lambda i: (0, 0, 0)),
                  pl.BlockSpec((n, D_MODEL), lambda i: (0, 0)),
                  (1,), n, True, w_glu, b_glu, w_out, g, b, "ssm_out_sample")
    return ys, (logical(sr), logical(si))


def kernel(x_prompt, x_sample, cache_a_k, cache_a_v, cache_b_k, cache_b_v, state_c_re, state_c_im, ln_g, ln_b, ffn_w_gate, ffn_w_up, ffn_w_down, attn_w_in, attn_sinks, attn_w_out, ssm_w_in, ssm_lambda_re, ssm_lambda_im, ssm_log_dt, ssm_b_re, ssm_b_im, ssm_c_re, ssm_c_im, ssm_d, ssm_w_glu, ssm_b_glu, ssm_w_out):
    yp = x_prompt.reshape(BATCH * SEQ, D_MODEL)
    ys = x_sample.reshape(DEC_BATCH * DEC_SEQ, D_MODEL)
    ln = lambda l, k: (ln_g[l, k].reshape(1, D_MODEL), ln_b[l, k].reshape(1, D_MODEL))

    wg, wu, wd = ffn_w_gate.astype(BF16), ffn_w_up.astype(BF16), ffn_w_down.astype(BF16)

    def ffn_pair(yp, ys, l, k, ln_idx):
        g, b = ln(l, ln_idx)
        return _ffn_pair(yp, ys, wg, wu, wd, g, b, l, k)

    yp, ys = ffn_pair(yp, ys, 0, 0, 0)
    yp, ys, p_cache, s_cache = _attn_layer(yp, ys, cache_a_k[0], cache_a_v[0], cache_b_k[0], cache_b_v[0],
                                           attn_w_in[0], attn_sinks[0], attn_w_out[0], *ln(0, 1),
                                           ffn_args=(wg, wu, wd, *ln(0, 2), 0, 1))
    ys = _ffn(ys, wg, wu, wd, *ln(0, 2), 0, 1)
    yp, ys = ffn_pair(yp, ys, 1, 0, 0)
    yp, ys, p_state, s_state = _ssm_layer(yp, ys, state_c_re[0], state_c_im[0], ssm_w_in[0], ssm_lambda_re[0],
                                          ssm_lambda_im[0], ssm_log_dt[0], ssm_b_re[0], ssm_b_im[0], ssm_c_re[0],
                                          ssm_c_im[0], ssm_d[0], ssm_w_glu[0], ssm_b_glu[0], ssm_w_out[0], *ln(1, 1))
    yp, ys = ffn_pair(yp, ys, 1, 1, 2)
    return (yp.reshape(BATCH, SEQ, D_MODEL), ys.reshape(DEC_BATCH, DEC_SEQ, D_MODEL),
            *p_cache, *p_state, *s_cache, *s_state)
```
